```python
import jax, jax.numpy as jnp
from jax import lax
import numpy as np

D_MODEL = 2048
BATCH = 8
SEQ = 2048
DEPTH = 4

CHUNK = 64
N_MIXERS = 4
D_FF = 4 * D_MODEL
NORM_EPS = 1e-6

RET_HEADS = 8
RET_DK = D_MODEL // RET_HEADS
RET_DV = 2 * D_MODEL // RET_HEADS
RET_IN = 2 * RET_HEADS * RET_DK + 2 * RET_HEADS * RET_DV
ROPE_BASE = 10000.0

GDN_HEADS = 16
GDN_DK = D_MODEL // GDN_HEADS
GDN_DV = D_MODEL // GDN_HEADS
GDN_QKV = GDN_HEADS * (2 * GDN_DK + GDN_DV)
GDN_IN = GDN_QKV + GDN_HEADS * GDN_DV + 2 * GDN_HEADS
CONV_WIDTH = 4

GLA_HEADS = 4
GLA_DK = D_MODEL // 2 // GLA_HEADS
GLA_DV = D_MODEL // GLA_HEADS
GLA_GATE_RANK = 16
GLA_TAU = 16.0
GLA_IN = 2 * GLA_HEADS * GLA_DK + 2 * GLA_HEADS * GLA_DV + GLA_GATE_RANK

LRU_WIDTH = D_MODEL
LRU_BLOCKS = 16
LRU_BLOCK = LRU_WIDTH // LRU_BLOCKS
LRU_C = 8.0

kernel_name = "interleaved_hybrid_chunk_causal_encoder"

F32 = jnp.float32


def _layers_of(m):
    return (DEPTH - m + N_MIXERS - 1) // N_MIXERS


def rmsnorm(x, g):
    xf = x.astype(F32)
    y = xf * lax.rsqrt(jnp.mean(xf * xf, axis=-1, keepdims=True) + NORM_EPS)
    return (y * g.astype(F32)).astype(x.dtype)


def head_norm(o, gain, center):
    of = o.astype(F32)
    if center:
        of = of - jnp.mean(of, axis=-1, keepdims=True)
    of = of * lax.rsqrt(jnp.mean(of * of, axis=-1, keepdims=True) + NORM_EPS)
    of = of * gain.astype(F32)
    return of.reshape(o.shape[0], o.shape[1], -1)


def l2norm(x):
    return x * lax.rsqrt(jnp.sum(x * x, axis=-1, keepdims=True) + NORM_EPS)


def causal_depthwise_conv(x, w):
    width, s = w.shape[0], x.shape[1]
    xp = jnp.pad(x, ((0, 0), (width - 1, 0), (0, 0)))
    out = xp[:, 0:s] * w[0]
    for tap in range(1, width):
        out = out + xp[:, tap:tap + s] * w[tap]
    return out


def to_chunks(x):
    b, s, h, d = x.shape
    return x.reshape(b, s // CHUNK, CHUNK, h, d).transpose(0, 3, 1, 2, 4)


def from_chunks(x):
    b, h, n, c, d = x.shape
    return x.transpose(0, 2, 3, 1, 4).reshape(b, n * c, h, d)


def scalar_chunks(x):
    b, s, h = x.shape
    return x.reshape(b, s // CHUNK, CHUNK, h).transpose(0, 3, 1, 2)


def rotary(x):
    s, d = x.shape[1], x.shape[-1]
    inv = ROPE_BASE ** (-jnp.arange(0, d, 2, dtype=F32) / d)
    ang = jnp.arange(s, dtype=F32)[:, None] * inv[None, :]
    cos, sin = jnp.cos(ang)[:, None, :], jnp.sin(ang)[:, None, :]
    x1, x2 = x[..., : d // 2], x[..., d // 2:]
    return jnp.concatenate([x1 * cos - x2 * sin, x1 * sin + x2 * cos], axis=-1)


def retention_mixer(h, w_in, gn_gain, w_out):
    b, s, _ = h.shape
    H, DK, DV = RET_HEADS, RET_DK, RET_DV
    q, k, v, g = jnp.split(h @ w_in, [H * DK, 2 * H * DK, 2 * H * DK + H * DV], axis=-1)
    q = rotary(q.astype(F32).reshape(b, s, H, DK))
    k = rotary(k.astype(F32).reshape(b, s, H, DK)) * DK ** -0.5
    v = v.astype(F32).reshape(b, s, H, DV)
    qc, kc, vc = to_chunks(q), to_chunks(k), to_chunks(v)
    log_gamma = jnp.log1p(-jnp.exp2(-5.0 - jnp.arange(H, dtype=F32)))
    pos = jnp.arange(CHUNK, dtype=F32)
    dist = jnp.abs(pos[:, None] - pos[None, :])
    decay_intra = jnp.exp(log_gamma[:, None, None] * dist)
    scores = jnp.einsum('bhncd,bhnmd->bhncm', qc, kc) * decay_intra[None, :, None]
    o_intra = jnp.einsum('bhncm,bhnmv->bhncv', scores, vc)
    q_decay = jnp.exp(log_gamma[:, None] * (pos + 1.0))[None, :, :, None]
    k_decay = jnp.exp(log_gamma[:, None] * (CHUNK - 1.0 - pos))[None, :, :, None]
    chunk_decay = jnp.exp(log_gamma * CHUNK)[None, :, None, None]

    def step(state, xs):
        q_n, k_n, v_n = xs
        o = jnp.einsum('bhcd,bhdv->bhcv', q_n * q_decay, state)
        state = state * chunk_decay + jnp.einsum('bhcd,bhcv->bhdv', k_n * k_decay, v_n)
        return state, o

    xs = (jnp.moveaxis(qc, 2, 0), jnp.moveaxis(kc, 2, 0), jnp.moveaxis(vc, 2, 0))
    _, o_inter = lax.scan(step, jnp.zeros((b, H, DK, DV), F32), xs)
    o = from_chunks(o_intra + jnp.moveaxis(o_inter, 0, 2))
    o = head_norm(o, gn_gain, center=True) * jax.nn.silu(g.astype(F32))
    return o.astype(h.dtype) @ w_out


def gated_deltanet_mixer(h, w_in, conv_w, a_log, dt_bias, norm_gain, w_out):
    b, s, _ = h.shape
    H, DK, DV = GDN_HEADS, GDN_DK, GDN_DV
    qkv, z, beta_logit, a_logit = jnp.split(
        h @ w_in, [GDN_QKV, GDN_QKV + H * DV, GDN_QKV + H * DV + H], axis=-1)
    qkv = jax.nn.silu(causal_depthwise_conv(qkv, conv_w)).astype(F32)
    q, k, v = jnp.split(qkv, [H * DK, 2 * H * DK], axis=-1)
    q = l2norm(q.reshape(b, s, H, DK)) * DK ** -0.5
    k = l2norm(k.reshape(b, s, H, DK))
    v = v.reshape(b, s, H, DV)
    beta = jax.nn.sigmoid(beta_logit.astype(F32))
    log_alpha = -jnp.exp(a_log.astype(F32)) * jax.nn.softplus(a_logit.astype(F32) + dt_bias.astype(F32))
    qc, kc, vc = to_chunks(q), to_chunks(k), to_chunks(v)
    beta_c = scalar_chunks(beta)
    cum = jnp.cumsum(scalar_chunks(log_alpha), axis=-1)
    idx = jnp.arange(CHUNK)
    strict = idx[:, None] > idx[None, :]
    rel = jnp.where(strict, jnp.exp(jnp.where(strict, cum[..., :, None] - cum[..., None, :], 0.0)), 0.0)
    a_mat = beta_c[..., :, None] * rel * jnp.einsum('bhncd,bhnmd->bhncm', kc, kc)
    l_mat = a_mat + jnp.eye(CHUNK, dtype=F32)
    rhs = jnp.concatenate([beta_c[..., None] * vc, (beta_c * jnp.exp(cum))[..., None] * kc], axis=-1)
    sol = lax.linalg.triangular_solve(l_mat, rhs, left_side=True, lower=True, unit_diagonal=True)
    u, w = sol[..., :DV], sol[..., DV:]
    k_end = kc * jnp.exp(cum[..., -1:] - cum)[..., None]
    trans = (jnp.exp(cum[..., -1])[..., None, None] * jnp.eye(DK, dtype=F32)
             - jnp.einsum('bhnck,bhncj->bhnkj', k_end, w))
    inject = jnp.einsum('bhnck,bhncv->bhnkv', k_end, u)

    def step(state, xs):
        t, g, q_n = xs
        state = jnp.einsum('bhkj,bhjv->bhkv', t, state) + g
        return state, jnp.einsum('bhck,bhkv->bhcv', q_n, state)

    xs = (jnp.moveaxis(trans, 2, 0), jnp.moveaxis(inject, 2, 0), jnp.moveaxis(qc, 2, 0))
    _, o = lax.scan(step, jnp.zeros((b, H, DK, DV), F32), xs)
    o = from_chunks(jnp.moveaxis(o, 0, 2))
    o = head_norm(o, norm_gain, center=False) * jax.nn.silu(z.astype(F32))
    return o.astype(h.dtype) @ w_out


def gla_mixer(h, w_in, w_gate_up, gate_bias, norm_gain, w_out):
    b, s, _ = h.shape
    H, DK, DV = GLA_HEADS, GLA_DK, GLA_DV
    QK, V = H * DK, H * DV
    q, k, v, r, gate_low = jnp.split(h @ w_in, [QK, 2 * QK, 2 * QK + V, 2 * QK + 2 * V], axis=-1)
    gate_logit = (gate_low @ w_gate_up + gate_bias).astype(F32)
    log_alpha = jax.nn.log_sigmoid(gate_logit) / GLA_TAU
    qc = to_chunks(q.astype(F32).reshape(b, s, H, DK)) * DK ** -0.5
    kc = to_chunks(k.astype(F32).reshape(b, s, H, DK))
    vc = to_chunks(v.astype(F32).reshape(b, s, H, DV))
    cum = jnp.cumsum(to_chunks(log_alpha.reshape(b, s, H, DK)), axis=-2)
    ref = cum[..., CHUNK // 2 - 1:CHUNK // 2, :]
    fwd, bwd = jnp.exp(cum - ref), jnp.exp(ref - cum)
    s_lo = jnp.einsum('bhnck,bhnmk->bhncm', qc * fwd, kc * bwd)
    s_up = jnp.einsum('bhnck,bhnmk->bhncm', qc * bwd, kc * fwd)
    idx = jnp.arange(CHUNK)
    scores = jnp.where(idx[:, None] >= idx[None, :], s_lo, s_up)
    o_intra = jnp.einsum('bhncm,bhnmv->bhncv', scores, vc)
    q_in = qc * jnp.exp(cum)
    k_end = kc * jnp.exp(cum[..., -1:, :] - cum)
    chunk_dec = jnp.exp(cum[..., -1, :])

    def step(state, xs):
        q_n, k_n, v_n, dec = xs
        o = jnp.einsum('bhck,bhkv->bhcv', q_n, state)
        state = state * dec[..., None] + jnp.einsum('bhck,bhcv->bhkv', k_n, v_n)
        return state, o

    xs = tuple(jnp.moveaxis(t, 2, 0) for t in (q_in, k_end, vc, chunk_dec))
    _, o_inter = lax.scan(step, jnp.zeros((b, H, DK, DV), F32), xs)
    o = from_chunks(o_intra + jnp.moveaxis(o_inter, 0, 2))
    o = head_norm(o, norm_gain, center=False) * jax.nn.silu(r.astype(F32))
    return o.astype(h.dtype) @ w_out


def rglru_mixer(h, w_in, conv_w, conv_b, w_rgate, b_rgate, w_igate, b_igate, lam, w_out):
    b, s, _ = h.shape
    xb, yb = jnp.split(h @ w_in, [LRU_WIDTH], axis=-1)
    yb = jax.nn.gelu(yb.astype(F32))
    xb = (causal_depthwise_conv(xb, conv_w) + conv_b).astype(F32)
    xblk = xb.reshape(b, s, LRU_BLOCKS, LRU_BLOCK)
    r = jax.nn.sigmoid(jnp.einsum('bsnd,nde->bsne', xblk, w_rgate.astype(F32)) + b_rgate.astype(F32))
    i = jax.nn.sigmoid(jnp.einsum('bsnd,nde->bsne', xblk, w_igate.astype(F32)) + b_igate.astype(F32))
    r, i = r.reshape(b, s, LRU_WIDTH), i.reshape(b, s, LRU_WIDTH)
    log_a = -LRU_C * jax.nn.softplus(-lam.astype(F32)) * r
    a = jnp.exp(log_a)
    u = jnp.sqrt(-jnp.expm1(2.0 * log_a)) * (i * xb)

    def combine(left, right):
        a_l, b_l = left
        a_r, b_r = right
        return a_l * a_r, a_r * b_l + b_r

    _, hs = lax.associative_scan(combine, (a, u), axis=1)
    return (hs * yb).astype(h.dtype) @ w_out


def squared_relu_mlp(h, w_up, w_down):
    return jnp.square(jax.nn.relu(h @ w_up)) @ w_down


def _fwd_setup_inputs(seed: int = 0) -> dict:
    key = jax.random.key(seed)
    ks = iter(jax.random.split(key, 40))

    def nrm(shape, scale):
        return scale * jax.random.normal(next(ks), shape, F32)

    def uni(shape, lo, hi):
        return jax.random.uniform(next(ks), shape, F32, lo, hi)

    n_ret, n_gdn, n_gla, n_lru = (_layers_of(m) for m in range(N_MIXERS))
    d_in = D_MODEL ** -0.5
    dt = jnp.exp(uni((n_gdn, GDN_HEADS), float(np.log(1e-3)), float(np.log(1e-1))))
    a0 = uni((n_lru, LRU_WIDTH), 0.9, 0.999)
    s0 = a0 ** (1.0 / LRU_C)
    return {
        "x": nrm((BATCH, SEQ, D_MODEL), 1.0),
        "norm1": 1.0 + nrm((DEPTH, D_MODEL), 0.02),
        "norm2": 1.0 + nrm((DEPTH, D_MODEL), 0.02),
        "final_norm": 1.0 + nrm((D_MODEL,), 0.02),
        "ret_w_in": nrm((n_ret, D_MODEL, RET_IN), d_in),
        "ret_gn_gain": 1.0 + nrm((n_ret, RET_HEADS, RET_DV), 0.02),
        "ret_w_out": nrm((n_ret, RET_HEADS * RET_DV, D_MODEL), (RET_HEADS * RET_DV) ** -0.5),
        "gdn_w_in": nrm((n_gdn, D_MODEL, GDN_IN), d_in),
        "gdn_conv_w": nrm((n_gdn, CONV_WIDTH, GDN_QKV), CONV_WIDTH ** -0.5),
        "gdn_a_log": jnp.log(uni((n_gdn, GDN_HEADS), 1.0, 16.0)),
        "gdn_dt_bias": dt + jnp.log(-jnp.expm1(-dt)),
        "gdn_norm_gain": 1.0 + nrm((n_gdn, GDN_DV), 0.02),
        "gdn_w_out": nrm((n_gdn, GDN_HEADS * GDN_DV, D_MODEL), (GDN_HEADS * GDN_DV) ** -0.5),
        "gla_w_in": nrm((n_gla, D_MODEL, GLA_IN), d_in),
        "gla_w_gate_up": nrm((n_gla, GLA_GATE_RANK, GLA_HEADS * GLA_DK), GLA_GATE_RANK ** -0.5),
        "gla_gate_bias": nrm((n_gla, GLA_HEADS * GLA_DK), 0.1),
        "gla_norm_gain": 1.0 + nrm((n_gla, GLA_HEADS, GLA_DV), 0.02),
        "gla_w_out": nrm((n_gla, GLA_HEADS * GLA_DV, D_MODEL), (GLA_HEADS * GLA_DV) ** -0.5),
        "lru_w_in": nrm((n_lru, D_MODEL, 2 * LRU_WIDTH), d_in),
        "lru_conv_w": nrm((n_lru, CONV_WIDTH, LRU_WIDTH), CONV_WIDTH ** -0.5),
        "lru_conv_b": nrm((n_lru, LRU_WIDTH), 0.02),
        "lru_w_rgate": nrm((n_lru, LRU_BLOCKS, LRU_BLOCK, LRU_BLOCK), LRU_BLOCK ** -0.5),
        "lru_b_rgate": nrm((n_lru, LRU_BLOCKS, LRU_BLOCK), 0.02),
        "lru_w_igate": nrm((n_lru, LRU_BLOCKS, LRU_BLOCK, LRU_BLOCK), LRU_BLOCK ** -0.5),
        "lru_b_igate": nrm((n_lru, LRU_BLOCKS, LRU_BLOCK), 0.02),
        "lru_lambda": jnp.log(s0) - jnp.log1p(-s0),
        "lru_w_out": nrm((n_lru, LRU_WIDTH, D_MODEL), LRU_WIDTH ** -0.5),
        "mlp_w_up": nrm((DEPTH, D_MODEL, D_FF), d_in),
        "mlp_w_down": nrm((DEPTH, D_FF, D_MODEL), D_FF ** -0.5),
    }


def _fwd_reference(x, norm1, norm2, final_norm,
              ret_w_in, ret_gn_gain, ret_w_out,
              gdn_w_in, gdn_conv_w, gdn_a_log, gdn_dt_bias, gdn_norm_gain, gdn_w_out,
              gla_w_in, gla_w_gate_up, gla_gate_bias, gla_norm_gain, gla_w_out,
              lru_w_in, lru_conv_w, lru_conv_b, lru_w_rgate, lru_b_rgate, lru_w_igate,
              lru_b_igate, lru_lambda, lru_w_out,
              mlp_w_up, mlp_w_down):
    for layer in range(DEPTH):
        m, j = layer % N_MIXERS, layer // N_MIXERS
        hn = rmsnorm(x, norm1[layer])
        if m == 0:
            y = retention_mixer(hn, ret_w_in[j], ret_gn_gain[j], ret_w_out[j])
        elif m == 1:
            y = gated_deltanet_mixer(hn, gdn_w_in[j], gdn_conv_w[j], gdn_a_log[j], gdn_dt_bias[j],
                                     gdn_norm_gain[j], gdn_w_out[j])
        elif m == 2:
            y = gla_mixer(hn, gla_w_in[j], gla_w_gate_up[j], gla_gate_bias[j], gla_norm_gain[j], gla_w_out[j])
        else:
            y = rglru_mixer(hn, lru_w_in[j], lru_conv_w[j], lru_conv_b[j], lru_w_rgate[j], lru_b_rgate[j],
                            lru_w_igate[j], lru_b_igate[j], lru_lambda[j], lru_w_out[j])
        x = x + y
        x = x + squared_relu_mlp(rmsnorm(x, norm2[layer]), mlp_w_up[layer], mlp_w_down[layer])
    return rmsnorm(x, final_norm)


import jax as _jax
import jax.numpy as _jnp

TWIN_FORMAT = 'train_step'
FWD_PARAMS = ['x', 'norm1', 'norm2', 'final_norm', 'ret_w_in', 'ret_gn_gain', 'ret_w_out', 'gdn_w_in', 'gdn_conv_w', 'gdn_a_log', 'gdn_dt_bias', 'gdn_norm_gain', 'gdn_w_out', 'gla_w_in', 'gla_w_gate_up', 'gla_gate_bias', 'gla_norm_gain', 'gla_w_out', 'lru_w_in', 'lru_conv_w', 'lru_conv_b', 'lru_w_rgate', 'lru_b_rgate', 'lru_w_igate', 'lru_b_igate', 'lru_lambda', 'lru_w_out', 'mlp_w_up', 'mlp_w_down']
TWIN_WEIGHTS = ['norm1', 'norm2', 'final_norm', 'ret_w_in', 'ret_gn_gain', 'ret_w_out', 'gdn_w_in', 'gdn_conv_w', 'gdn_a_log', 'gdn_dt_bias', 'gdn_norm_gain', 'gdn_w_out', 'gla_w_in', 'gla_w_gate_up', 'gla_gate_bias', 'gla_norm_gain', 'gla_w_out', 'lru_w_in', 'lru_conv_w', 'lru_conv_b', 'lru_w_rgate', 'lru_b_rgate', 'lru_w_igate', 'lru_b_igate', 'lru_lambda', 'lru_w_out', 'mlp_w_up', 'mlp_w_down']
TWIN_DIFF_INPUT = 'x'
TWIN_INPUTS = ['x', 'norm1', 'norm2', 'final_norm', 'ret_w_in', 'ret_gn_gain', 'ret_w_out', 'gdn_w_in', 'gdn_conv_w', 'gdn_a_log', 'gdn_dt_bias', 'gdn_norm_gain', 'gdn_w_out', 'gla_w_in', 'gla_w_gate_up', 'gla_gate_bias', 'gla_norm_gain', 'gla_w_out', 'lru_w_in', 'lru_conv_w', 'lru_conv_b', 'lru_w_rgate', 'lru_b_rgate', 'lru_w_igate', 'lru_b_igate', 'lru_lambda', 'lru_w_out', 'mlp_w_up', 'mlp_w_down', 'loss_target', 'm_norm1', 'm_norm2', 'm_final_norm', 'm_ret_w_in', 'm_ret_gn_gain', 'm_ret_w_out', 'm_gdn_w_in', 'm_gdn_conv_w', 'm_gdn_a_log', 'm_gdn_dt_bias', 'm_gdn_norm_gain', 'm_gdn_w_out', 'm_gla_w_in', 'm_gla_w_gate_up', 'm_gla_gate_bias', 'm_gla_norm_gain', 'm_gla_w_out', 'm_lru_w_in', 'm_lru_conv_w', 'm_lru_conv_b', 'm_lru_w_rgate', 'm_lru_b_rgate', 'm_lru_w_igate', 'm_lru_b_igate', 'm_lru_lambda', 'm_lru_w_out', 'm_mlp_w_up', 'm_mlp_w_down', 'v_norm1', 'v_norm2', 'v_final_norm', 'v_ret_w_in', 'v_ret_gn_gain', 'v_ret_w_out', 'v_gdn_w_in', 'v_gdn_conv_w', 'v_gdn_a_log', 'v_gdn_dt_bias', 'v_gdn_norm_gain', 'v_gdn_w_out', 'v_gla_w_in', 'v_gla_w_gate_up', 'v_gla_gate_bias', 'v_gla_norm_gain', 'v_gla_w_out', 'v_lru_w_in', 'v_lru_conv_w', 'v_lru_conv_b', 'v_lru_w_rgate', 'v_lru_b_rgate', 'v_lru_w_igate', 'v_lru_b_igate', 'v_lru_lambda', 'v_lru_w_out', 'v_mlp_w_up', 'v_mlp_w_down']
TWIN_OUTPUTS = ['loss', 'grad_x', 'grad_norm1', 'grad_norm2', 'grad_final_norm', 'grad_ret_w_in', 'grad_ret_gn_gain', 'grad_ret_w_out', 'grad_gdn_w_in', 'grad_gdn_conv_w', 'grad_gdn_a_log', 'grad_gdn_dt_bias', 'grad_gdn_norm_gain', 'grad_gdn_w_out', 'grad_gla_w_in', 'grad_gla_w_gate_up', 'grad_gla_gate_bias', 'grad_gla_norm_gain', 'grad_gla_w_out', 'grad_lru_w_in', 'grad_lru_conv_w', 'grad_lru_conv_b', 'grad_lru_w_rgate', 'grad_lru_b_rgate', 'grad_lru_w_igate', 'grad_lru_b_igate', 'grad_lru_lambda', 'grad_lru_w_out', 'grad_mlp_w_up', 'grad_mlp_w_down', 'delta_norm1', 'delta_norm2', 'delta_final_norm', 'delta_ret_w_in', 'delta_ret_gn_gain', 'delta_ret_w_out', 'delta_gdn_w_in', 'delta_gdn_conv_w', 'delta_gdn_a_log', 'delta_gdn_dt_bias', 'delta_gdn_norm_gain', 'delta_gdn_w_out', 'delta_gla_w_in', 'delta_gla_w_gate_up', 'delta_gla_gate_bias', 'delta_gla_norm_gain', 'delta_gla_w_out', 'delta_lru_w_in', 'delta_lru_conv_w', 'delta_lru_conv_b', 'delta_lru_w_rgate', 'delta_lru_b_rgate', 'delta_lru_w_igate', 'delta_lru_b_igate', 'delta_lru_lambda', 'delta_lru_w_out', 'delta_mlp_w_up', 'delta_mlp_w_down', 'new_m_norm1', 'new_m_norm2', 'new_m_final_norm', 'new_m_ret_w_in', 'new_m_ret_gn_gain', 'new_m_ret_w_out', 'new_m_gdn_w_in', 'new_m_gdn_conv_w', 'new_m_gdn_a_log', 'new_m_gdn_dt_bias', 'new_m_gdn_norm_gain', 'new_m_gdn_w_out', 'new_m_gla_w_in', 'new_m_gla_w_gate_up', 'new_m_gla_gate_bias', 'new_m_gla_norm_gain', 'new_m_gla_w_out', 'new_m_lru_w_in', 'new_m_lru_conv_w', 'new_m_lru_conv_b', 'new_m_lru_w_rgate', 'new_m_lru_b_rgate', 'new_m_lru_w_igate', 'new_m_lru_b_igate', 'new_m_lru_lambda', 'new_m_lru_w_out', 'new_m_mlp_w_up', 'new_m_mlp_w_down', 'new_v_norm1', 'new_v_norm2', 'new_v_final_norm', 'new_v_ret_w_in', 'new_v_ret_gn_gain', 'new_v_ret_w_out', 'new_v_gdn_w_in', 'new_v_gdn_conv_w', 'new_v_gdn_a_log', 'new_v_gdn_dt_bias', 'new_v_gdn_norm_gain', 'new_v_gdn_w_out', 'new_v_gla_w_in', 'new_v_gla_w_gate_up', 'new_v_gla_gate_bias', 'new_v_gla_norm_gain', 'new_v_gla_w_out', 'new_v_lru_w_in', 'new_v_lru_conv_w', 'new_v_lru_conv_b', 'new_v_lru_w_rgate', 'new_v_lru_b_rgate', 'new_v_lru_w_igate', 'new_v_lru_b_igate', 'new_v_lru_lambda', 'new_v_lru_w_out', 'new_v_mlp_w_up', 'new_v_mlp_w_down']
TWIN_LEAF_KINDS = {'loss': 'loss', 'grad_x': 'grad_x', 'grad_norm1': 'grad_w', 'grad_norm2': 'grad_w', 'grad_final_norm': 'grad_w', 'grad_ret_w_in': 'grad_w', 'grad_ret_gn_gain': 'grad_w', 'grad_ret_w_out': 'grad_w', 'grad_gdn_w_in': 'grad_w', 'grad_gdn_conv_w': 'grad_w', 'grad_gdn_a_log': 'grad_w', 'grad_gdn_dt_bias': 'grad_w', 'grad_gdn_norm_gain': 'grad_w', 'grad_gdn_w_out': 'grad_w', 'grad_gla_w_in': 'grad_w', 'grad_gla_w_gate_up': 'grad_w', 'grad_gla_gate_bias': 'grad_w', 'grad_gla_norm_gain': 'grad_w', 'grad_gla_w_out': 'grad_w', 'grad_lru_w_in': 'grad_w', 'grad_lru_conv_w': 'grad_w', 'grad_lru_conv_b': 'grad_w', 'grad_lru_w_rgate': 'grad_w', 'grad_lru_b_rgate': 'grad_w', 'grad_lru_w_igate': 'grad_w', 'grad_lru_b_igate': 'grad_w', 'grad_lru_lambda': 'grad_w', 'grad_lru_w_out': 'grad_w', 'grad_mlp_w_up': 'grad_w', 'grad_mlp_w_down': 'grad_w', 'delta_norm1': 'delta_w', 'delta_norm2': 'delta_w', 'delta_final_norm': 'delta_w', 'delta_ret_w_in': 'delta_w', 'delta_ret_gn_gain': 'delta_w', 'delta_ret_w_out': 'delta_w', 'delta_gdn_w_in': 'delta_w', 'delta_gdn_conv_w': 'delta_w', 'delta_gdn_a_log': 'delta_w', 'delta_gdn_dt_bias': 'delta_w', 'delta_gdn_norm_gain': 'delta_w', 'delta_gdn_w_out': 'delta_w', 'delta_gla_w_in': 'delta_w', 'delta_gla_w_gate_up': 'delta_w', 'delta_gla_gate_bias': 'delta_w', 'delta_gla_norm_gain': 'delta_w', 'delta_gla_w_out': 'delta_w', 'delta_lru_w_in': 'delta_w', 'delta_lru_conv_w': 'delta_w', 'delta_lru_conv_b': 'delta_w', 'delta_lru_w_rgate': 'delta_w', 'delta_lru_b_rgate': 'delta_w', 'delta_lru_w_igate': 'delta_w', 'delta_lru_b_igate': 'delta_w', 'delta_lru_lambda': 'delta_w', 'delta_lru_w_out': 'delta_w', 'delta_mlp_w_up': 'delta_w', 'delta_mlp_w_down': 'delta_w', 'new_m_norm1': 'new_m', 'new_m_norm2': 'new_m', 'new_m_final_norm': 'new_m', 'new_m_ret_w_in': 'new_m', 'new_m_ret_gn_gain': 'new_m', 'new_m_ret_w_out': 'new_m', 'new_m_gdn_w_in': 'new_m', 'new_m_gdn_conv_w': 'new_m', 'new_m_gdn_a_log': 'new_m', 'new_m_gdn_dt_bias': 'new_m', 'new_m_gdn_norm_gain': 'new_m', 'new_m_gdn_w_out': 'new_m', 'new_m_gla_w_in': 'new_m', 'new_m_gla_w_gate_up': 'new_m', 'new_m_gla_gate_bias': 'new_m', 'new_m_gla_norm_gain': 'new_m', 'new_m_gla_w_out': 'new_m', 'new_m_lru_w_in': 'new_m', 'new_m_lru_conv_w': 'new_m', 'new_m_lru_conv_b': 'new_m', 'new_m_lru_w_rgate': 'new_m', 'new_m_lru_b_rgate': 'new_m', 'new_m_lru_w_igate': 'new_m', 'new_m_lru_b_igate': 'new_m', 'new_m_lru_lambda': 'new_m', 'new_m_lru_w_out': 'new_m', 'new_m_mlp_w_up': 'new_m', 'new_m_mlp_w_down': 'new_m', 'new_v_norm1': 'new_v', 'new_v_norm2': 'new_v', 'new_v_final_norm': 'new_v', 'new_v_ret_w_in': 'new_v', 'new_v_ret_gn_gain': 'new_v', 'new_v_ret_w_out': 'new_v', 'new_v_gdn_w_in': 'new_v', 'new_v_gdn_conv_w': 'new_v', 'new_v_gdn_a_log': 'new_v', 'new_v_gdn_dt_bias': 'new_v', 'new_v_gdn_norm_gain': 'new_v', 'new_v_gdn_w_out': 'new_v', 'new_v_gla_w_in': 'new_v', 'new_v_gla_w_gate_up': 'new_v', 'new_v_gla_gate_bias': 'new_v', 'new_v_gla_norm_gain': 'new_v', 'new_v_gla_w_out': 'new_v', 'new_v_lru_w_in': 'new_v', 'new_v_lru_conv_w': 'new_v', 'new_v_lru_conv_b': 'new_v', 'new_v_lru_w_rgate': 'new_v', 'new_v_lru_b_rgate': 'new_v', 'new_v_lru_w_igate': 'new_v', 'new_v_lru_b_igate': 'new_v', 'new_v_lru_lambda': 'new_v', 'new_v_lru_w_out': 'new_v', 'new_v_mlp_w_up': 'new_v', 'new_v_mlp_w_down': 'new_v'}


def _forward(args):
    return _fwd_reference(*[args[k] for k in FWD_PARAMS])


def _output_shape():
    out = _jax.eval_shape(lambda: _forward(_fwd_setup_inputs(0)))
    return out.shape, out.dtype

N_MICROBATCH = 1
ADAM_LR = 0.001
ADAM_B1 = 0.9
ADAM_B2 = 0.999
ADAM_EPS = 1e-08
ADAM_WD = 0.01
ADAM_STEP = 10
PER_EXAMPLE_BATCH_AXIS = {'x': 0, 'loss_target': 0}
SHARED_INPUTS = []
_WEIGHT_DTYPES = {'norm1': _jnp.float32, 'norm2': _jnp.float32, 'final_norm': _jnp.float32, 'ret_w_in': _jnp.float32, 'ret_gn_gain': _jnp.float32, 'ret_w_out': _jnp.float32, 'gdn_w_in': _jnp.float32, 'gdn_conv_w': _jnp.float32, 'gdn_a_log': _jnp.float32, 'gdn_dt_bias': _jnp.float32, 'gdn_norm_gain': _jnp.float32, 'gdn_w_out': _jnp.float32, 'gla_w_in': _jnp.float32, 'gla_w_gate_up': _jnp.float32, 'gla_gate_bias': _jnp.float32, 'gla_norm_gain': _jnp.float32, 'gla_w_out': _jnp.float32, 'lru_w_in': _jnp.float32, 'lru_conv_w': _jnp.float32, 'lru_conv_b': _jnp.float32, 'lru_w_rgate': _jnp.float32, 'lru_b_rgate': _jnp.float32, 'lru_w_igate': _jnp.float32, 'lru_b_igate': _jnp.float32, 'lru_lambda': _jnp.float32, 'lru_w_out': _jnp.float32, 'mlp_w_up': _jnp.float32, 'mlp_w_down': _jnp.float32}
MOMENT_SCALE = {'norm1': 6.300811e-02, 'norm2': 5.038751e-02, 'final_norm': 8.226446e+00, 'ret_w_in': 3.926705e-02, 'ret_gn_gain': 3.349567e-02, 'ret_w_out': 4.753202e-02, 'gdn_w_in': 2.248957e-02, 'gdn_conv_w': 2.066790e-02, 'gdn_a_log': 1.269703e-01, 'gdn_dt_bias': 1.235678e-01, 'gdn_norm_gain': 1.089820e-01, 'gdn_w_out': 2.814332e-02, 'gla_w_in': 2.230658e-02, 'gla_w_gate_up': 6.297609e-03, 'gla_gate_bias': 1.398590e-02, 'gla_norm_gain': 1.821263e-02, 'gla_w_out': 1.792969e-02, 'lru_w_in': 4.131400e-02, 'lru_conv_w': 4.268858e-02, 'lru_conv_b': 1.185131e-01, 'lru_w_rgate': 3.709571e-03, 'lru_b_rgate': 7.508073e-03, 'lru_w_igate': 8.168484e-03, 'lru_b_igate': 1.575915e-02, 'lru_lambda': 1.908958e-02, 'lru_w_out': 4.537655e-02, 'mlp_w_up': 2.513397e-02, 'mlp_w_down': 5.168631e-02}


def _to_microbatches(a, axis):
    t = _jnp.moveaxis(a, axis, 0)
    t = t.reshape((N_MICROBATCH, t.shape[0] // N_MICROBATCH) + t.shape[1:])
    return _jnp.moveaxis(t, 1, axis + 1)


def setup_inputs(seed: int = 0) -> dict:
    inp = _fwd_setup_inputs(seed)
    key = _jax.random.fold_in(_jax.random.key(seed), 7919)
    shape, _ = _output_shape()
    out = dict(inp)
    out["loss_target"] = _jax.random.normal(_jax.random.fold_in(key, 0), shape, _jnp.float32)
    for i, name in enumerate(TWIN_WEIGHTS):
        w = inp[name].astype(_jnp.float32)
        if MOMENT_SCALE is None:
            s = _jnp.sqrt(_jnp.mean(_jnp.square(w)) + 1e-30)
        else:
            s = MOMENT_SCALE[name]
        km, kv = _jax.random.split(_jax.random.fold_in(key, i + 1))
        out[name] = w
        out["m_" + name] = s * _jax.random.normal(km, w.shape, _jnp.float32)
        out["v_" + name] = (s * s) * _jax.random.uniform(kv, w.shape, _jnp.float32, 0.5, 1.5)
    if N_MICROBATCH > 1:
        for name, axis in PER_EXAMPLE_BATCH_AXIS.items():
            out[name] = _to_microbatches(out[name], axis)
    return {'x': out['x'], 'norm1': out['norm1'], 'norm2': out['norm2'], 'final_norm': out['final_norm'], 'ret_w_in': out['ret_w_in'], 'ret_gn_gain': out['ret_gn_gain'], 'ret_w_out': out['ret_w_out'], 'gdn_w_in': out['gdn_w_in'], 'gdn_conv_w': out['gdn_conv_w'], 'gdn_a_log': out['gdn_a_log'], 'gdn_dt_bias': out['gdn_dt_bias'], 'gdn_norm_gain': out['gdn_norm_gain'], 'gdn_w_out': out['gdn_w_out'], 'gla_w_in': out['gla_w_in'], 'gla_w_gate_up': out['gla_w_gate_up'], 'gla_gate_bias': out['gla_gate_bias'], 'gla_norm_gain': out['gla_norm_gain'], 'gla_w_out': out['gla_w_out'], 'lru_w_in': out['lru_w_in'], 'lru_conv_w': out['lru_conv_w'], 'lru_conv_b': out['lru_conv_b'], 'lru_w_rgate': out['lru_w_rgate'], 'lru_b_rgate': out['lru_b_rgate'], 'lru_w_igate': out['lru_w_igate'], 'lru_b_igate': out['lru_b_igate'], 'lru_lambda': out['lru_lambda'], 'lru_w_out': out['lru_w_out'], 'mlp_w_up': out['mlp_w_up'], 'mlp_w_down': out['mlp_w_down'], 'loss_target': out['loss_target'], 'm_norm1': out['m_norm1'], 'm_norm2': out['m_norm2'], 'm_final_norm': out['m_final_norm'], 'm_ret_w_in': out['m_ret_w_in'], 'm_ret_gn_gain': out['m_ret_gn_gain'], 'm_ret_w_out': out['m_ret_w_out'], 'm_gdn_w_in': out['m_gdn_w_in'], 'm_gdn_conv_w': out['m_gdn_conv_w'], 'm_gdn_a_log': out['m_gdn_a_log'], 'm_gdn_dt_bias': out['m_gdn_dt_bias'], 'm_gdn_norm_gain': out['m_gdn_norm_gain'], 'm_gdn_w_out': out['m_gdn_w_out'], 'm_gla_w_in': out['m_gla_w_in'], 'm_gla_w_gate_up': out['m_gla_w_gate_up'], 'm_gla_gate_bias': out['m_gla_gate_bias'], 'm_gla_norm_gain': out['m_gla_norm_gain'], 'm_gla_w_out': out['m_gla_w_out'], 'm_lru_w_in': out['m_lru_w_in'], 'm_lru_conv_w': out['m_lru_conv_w'], 'm_lru_conv_b': out['m_lru_conv_b'], 'm_lru_w_rgate': out['m_lru_w_rgate'], 'm_lru_b_rgate': out['m_lru_b_rgate'], 'm_lru_w_igate': out['m_lru_w_igate'], 'm_lru_b_igate': out['m_lru_b_igate'], 'm_lru_lambda': out['m_lru_lambda'], 'm_lru_w_out': out['m_lru_w_out'], 'm_mlp_w_up': out['m_mlp_w_up'], 'm_mlp_w_down': out['m_mlp_w_down'], 'v_norm1': out['v_norm1'], 'v_norm2': out['v_norm2'], 'v_final_norm': out['v_final_norm'], 'v_ret_w_in': out['v_ret_w_in'], 'v_ret_gn_gain': out['v_ret_gn_gain'], 'v_ret_w_out': out['v_ret_w_out'], 'v_gdn_w_in': out['v_gdn_w_in'], 'v_gdn_conv_w': out['v_gdn_conv_w'], 'v_gdn_a_log': out['v_gdn_a_log'], 'v_gdn_dt_bias': out['v_gdn_dt_bias'], 'v_gdn_norm_gain': out['v_gdn_norm_gain'], 'v_gdn_w_out': out['v_gdn_w_out'], 'v_gla_w_in': out['v_gla_w_in'], 'v_gla_w_gate_up': out['v_gla_w_gate_up'], 'v_gla_gate_bias': out['v_gla_gate_bias'], 'v_gla_norm_gain': out['v_gla_norm_gain'], 'v_gla_w_out': out['v_gla_w_out'], 'v_lru_w_in': out['v_lru_w_in'], 'v_lru_conv_w': out['v_lru_conv_w'], 'v_lru_conv_b': out['v_lru_conv_b'], 'v_lru_w_rgate': out['v_lru_w_rgate'], 'v_lru_b_rgate': out['v_lru_b_rgate'], 'v_lru_w_igate': out['v_lru_w_igate'], 'v_lru_b_igate': out['v_lru_b_igate'], 'v_lru_lambda': out['v_lru_lambda'], 'v_lru_w_out': out['v_lru_w_out'], 'v_mlp_w_up': out['v_mlp_w_up'], 'v_mlp_w_down': out['v_mlp_w_down']}


def _loss(weights, diff, rest, loss_target):
    with _jax.named_scope("forward"):
        args = {**rest, TWIN_DIFF_INPUT: diff, **{k: w.astype(_WEIGHT_DTYPES[k]) for k, w in weights.items()}}
        y = _forward(args)
    with _jax.named_scope("loss_head"):
        err = _jnp.square(y.astype(_jnp.float32) - loss_target)
        return 0.5 * _jnp.sum(_jnp.mean(err, axis=-1)) if err.ndim else 0.5 * err


def _adamw(w, g, m, v):
    m = ADAM_B1 * m + (1.0 - ADAM_B1) * g
    v = ADAM_B2 * v + (1.0 - ADAM_B2) * _jnp.square(g)
    m_hat = m / (1.0 - ADAM_B1 ** ADAM_STEP)
    v_hat = v / (1.0 - ADAM_B2 ** ADAM_STEP)
    delta = -ADAM_LR * (m_hat / (_jnp.sqrt(v_hat) + ADAM_EPS) + ADAM_WD * w)
    return delta, m, v


def reference(x, norm1, norm2, final_norm, ret_w_in, ret_gn_gain, ret_w_out, gdn_w_in, gdn_conv_w, gdn_a_log, gdn_dt_bias, gdn_norm_gain, gdn_w_out, gla_w_in, gla_w_gate_up, gla_gate_bias, gla_norm_gain, gla_w_out, lru_w_in, lru_conv_w, lru_conv_b, lru_w_rgate, lru_b_rgate, lru_w_igate, lru_b_igate, lru_lambda, lru_w_out, mlp_w_up, mlp_w_down, loss_target, m_norm1, m_norm2, m_final_norm, m_ret_w_in, m_ret_gn_gain, m_ret_w_out, m_gdn_w_in, m_gdn_conv_w, m_gdn_a_log, m_gdn_dt_bias, m_gdn_norm_gain, m_gdn_w_out, m_gla_w_in, m_gla_w_gate_up, m_gla_gate_bias, m_gla_norm_gain, m_gla_w_out, m_lru_w_in, m_lru_conv_w, m_lru_conv_b, m_lru_w_rgate, m_lru_b_rgate, m_lru_w_igate, m_lru_b_igate, m_lru_lambda, m_lru_w_out, m_mlp_w_up, m_mlp_w_down, v_norm1, v_norm2, v_final_norm, v_ret_w_in, v_ret_gn_gain, v_ret_w_out, v_gdn_w_in, v_gdn_conv_w, v_gdn_a_log, v_gdn_dt_bias, v_gdn_norm_gain, v_gdn_w_out, v_gla_w_in, v_gla_w_gate_up, v_gla_gate_bias, v_gla_norm_gain, v_gla_w_out, v_lru_w_in, v_lru_conv_w, v_lru_conv_b, v_lru_w_rgate, v_lru_b_rgate, v_lru_w_igate, v_lru_b_igate, v_lru_lambda, v_lru_w_out, v_mlp_w_up, v_mlp_w_down):
    given = dict(x=x, norm1=norm1, norm2=norm2, final_norm=final_norm, ret_w_in=ret_w_in, ret_gn_gain=ret_gn_gain, ret_w_out=ret_w_out, gdn_w_in=gdn_w_in, gdn_conv_w=gdn_conv_w, gdn_a_log=gdn_a_log, gdn_dt_bias=gdn_dt_bias, gdn_norm_gain=gdn_norm_gain, gdn_w_out=gdn_w_out, gla_w_in=gla_w_in, gla_w_gate_up=gla_w_gate_up, gla_gate_bias=gla_gate_bias, gla_norm_gain=gla_norm_gain, gla_w_out=gla_w_out, lru_w_in=lru_w_in, lru_conv_w=lru_conv_w, lru_conv_b=lru_conv_b, lru_w_rgate=lru_w_rgate, lru_b_rgate=lru_b_rgate, lru_w_igate=lru_w_igate, lru_b_igate=lru_b_igate, lru_lambda=lru_lambda, lru_w_out=lru_w_out, mlp_w_up=mlp_w_up, mlp_w_down=mlp_w_down, loss_target=loss_target, m_norm1=m_norm1, m_norm2=m_norm2, m_final_norm=m_final_norm, m_ret_w_in=m_ret_w_in, m_ret_gn_gain=m_ret_gn_gain, m_ret_w_out=m_ret_w_out, m_gdn_w_in=m_gdn_w_in, m_gdn_conv_w=m_gdn_conv_w, m_gdn_a_log=m_gdn_a_log, m_gdn_dt_bias=m_gdn_dt_bias, m_gdn_norm_gain=m_gdn_norm_gain, m_gdn_w_out=m_gdn_w_out, m_gla_w_in=m_gla_w_in, m_gla_w_gate_up=m_gla_w_gate_up, m_gla_gate_bias=m_gla_gate_bias, m_gla_norm_gain=m_gla_norm_gain, m_gla_w_out=m_gla_w_out, m_lru_w_in=m_lru_w_in, m_lru_conv_w=m_lru_conv_w, m_lru_conv_b=m_lru_conv_b, m_lru_w_rgate=m_lru_w_rgate, m_lru_b_rgate=m_lru_b_rgate, m_lru_w_igate=m_lru_w_igate, m_lru_b_igate=m_lru_b_igate, m_lru_lambda=m_lru_lambda, m_lru_w_out=m_lru_w_out, m_mlp_w_up=m_mlp_w_up, m_mlp_w_down=m_mlp_w_down, v_norm1=v_norm1, v_norm2=v_norm2, v_final_norm=v_final_norm, v_ret_w_in=v_ret_w_in, v_ret_gn_gain=v_ret_gn_gain, v_ret_w_out=v_ret_w_out, v_gdn_w_in=v_gdn_w_in, v_gdn_conv_w=v_gdn_conv_w, v_gdn_a_log=v_gdn_a_log, v_gdn_dt_bias=v_gdn_dt_bias, v_gdn_norm_gain=v_gdn_norm_gain, v_gdn_w_out=v_gdn_w_out, v_gla_w_in=v_gla_w_in, v_gla_w_gate_up=v_gla_w_gate_up, v_gla_gate_bias=v_gla_gate_bias, v_gla_norm_gain=v_gla_norm_gain, v_gla_w_out=v_gla_w_out, v_lru_w_in=v_lru_w_in, v_lru_conv_w=v_lru_conv_w, v_lru_conv_b=v_lru_conv_b, v_lru_w_rgate=v_lru_w_rgate, v_lru_b_rgate=v_lru_b_rgate, v_lru_w_igate=v_lru_w_igate, v_lru_b_igate=v_lru_b_igate, v_lru_lambda=v_lru_lambda, v_lru_w_out=v_lru_w_out, v_mlp_w_up=v_mlp_w_up, v_mlp_w_down=v_mlp_w_down)
    weights = {n: given[n] for n in TWIN_WEIGHTS}
    shared = {n: given[n] for n in SHARED_INPUTS}
    per_example = {n: given[n] for n in ['x']}
    grad_fn = _jax.value_and_grad(_loss, argnums=(0, 1))

    def one_microbatch(ex, loss_target):
        ex = dict(ex)
        diff = ex.pop(TWIN_DIFF_INPUT)
        return grad_fn(weights, diff, {**shared, **ex}, loss_target)

    if N_MICROBATCH == 1:
        loss, (grad_w, grad_x) = one_microbatch(per_example, given["loss_target"])
    else:
        def body(carry, xs):
            loss_sum, grad_sum = carry
            l_k, (gw_k, gx_k) = one_microbatch(xs[0], xs[1])
            with _jax.named_scope("update"):
                return (loss_sum + l_k, _jax.tree.map(_jnp.add, grad_sum, gw_k)), gx_k

        init = (_jnp.zeros((), _jnp.float32), _jax.tree.map(_jnp.zeros_like, weights))
        (loss, grad_w), grad_x = _jax.lax.scan(body, init, (per_example, given["loss_target"]))
    with _jax.named_scope("update"):
        delta_w, new_m, new_v = {}, {}, {}
        for n in TWIN_WEIGHTS:
            delta_w[n], new_m[n], new_v[n] = _adamw(weights[n], grad_w[n], given["m_" + n], given["v_" + n])
    return (loss, grad_x, *[grad_w[n] for n in TWIN_WEIGHTS], *[delta_w[n] for n in TWIN_WEIGHTS],
            *[new_m[n] for n in TWIN_WEIGHTS], *[new_v[n] for n in TWIN_WEIGHTS])
```

```python
import functools
import math

import numpy as np
import jax
import jax.numpy as jnp
from jax import lax
from jax.experimental import pallas as pl
from jax.experimental.pallas import tpu as pltpu

F32 = jnp.float32
BF16 = jnp.bfloat16

D_MODEL = 2048
DEPTH = 4
CHUNK = 64
D_FF = 4 * D_MODEL
NORM_EPS = 1e-6
N_DEV = 8

RET_HEADS, RET_DK, RET_DV = 8, 256, 512
GDN_HEADS, GDN_DK, GDN_DV = 16, 128, 128
GDN_QKV = GDN_HEADS * (2 * GDN_DK + GDN_DV)
CONV_WIDTH = 4
GLA_HEADS, GLA_DK, GLA_DV = 4, 256, 512
GLA_GATE_RANK = 16
GLA_TAU = 16.0
LRU_WIDTH, LRU_BLOCKS, LRU_BLOCK = 2048, 16, 128
LRU_C = 8.0
ROPE_BASE = 10000.0

ADAM_LR, ADAM_B1, ADAM_B2, ADAM_EPS, ADAM_WD, ADAM_STEP = 0.001, 0.9, 0.999, 1e-08, 0.01, 10

LANE = 128
VMEM_LIMIT = 48 * 1024 * 1024

NN = (((1,), (0,)), ((), ()))
NT = (((1,), (1,)), ((), ()))
TN = (((0,), (0,)), ((), ()))


def _params(sem=None):
    return pltpu.CompilerParams(dimension_semantics=sem, vmem_limit_bytes=VMEM_LIMIT)


def _bdot(a, b, dn=NN):
    return lax.dot_general(a.astype(BF16), b.astype(BF16), dn, preferred_element_type=F32)


def _split(x):
    hi = x.astype(BF16)
    lo = (x - hi.astype(F32)).astype(BF16)
    return hi, lo


def _fdot(a, b, dn=NN):
    a1, a2 = _split(a)
    b1, b2 = _split(b)
    d = functools.partial(lax.dot_general, dimension_numbers=dn, preferred_element_type=F32)
    return d(a1, b1) + (d(a1, b2) + d(a2, b1))


def _sigmoid(x):
    return 1.0 / (1.0 + jnp.exp(-x))


def _softplus(x):
    return jnp.maximum(x, 0.0) + jnp.log(1.0 + jnp.exp(-jnp.abs(x)))


def _silu(x):
    return x * _sigmoid(x)


def _dsilu(x):
    s = _sigmoid(x)
    return s * (1.0 + x * (1.0 - s))


GELU_C = math.sqrt(2.0 / math.pi)


def _gelu(x):
    return 0.5 * x * (1.0 + jnp.tanh(GELU_C * (x + 0.044715 * x * x * x)))


def _dgelu(x):
    t = jnp.tanh(GELU_C * (x + 0.044715 * x * x * x))
    return 0.5 * (1.0 + t) + 0.5 * x * (1.0 - t * t) * GELU_C * (1.0 + 3.0 * 0.044715 * x * x)


def _expm1(x):
    poly = x * (1.0 + x * 0.5 * (1.0 + x * (1.0 / 3.0) * (1.0 + x * 0.25 * (1.0 + x * 0.2))))
    return jnp.where(jnp.abs(x) < 0.05, poly, jnp.exp(x) - 1.0)


def _iota2(shape, axis):
    return lax.broadcasted_iota(jnp.int32, shape, axis)


def _col_to_row(col):
    n = col.shape[0]
    eye = _iota2((n, n), 0) == _iota2((n, n), 1)
    return jnp.sum(jnp.where(eye, col, 0.0), axis=0, keepdims=True)


def _row_to_col(row):
    n = row.shape[1]
    eye = _iota2((n, n), 0) == _iota2((n, n), 1)
    return jnp.sum(jnp.where(eye, row, 0.0), axis=1, keepdims=True)


def _pick_row(x, r):
    rows = _iota2(x.shape, 0)
    return jnp.sum(jnp.where(rows == r, x, 0.0), axis=0, keepdims=True)


def _shift_down(x, s):
    if s == 0:
        return x
    y = pltpu.roll(x, s, 0)
    return jnp.where(_iota2(x.shape, 0) < s, 0.0, y)


def _shift_up(x, s):
    if s == 0:
        return x
    n = x.shape[0]
    y = pltpu.roll(x, n - s, 0)
    return jnp.where(_iota2(x.shape, 0) >= n - s, 0.0, y)


def _tile(dim, prefs):
    for p in prefs:
        if dim % p == 0:
            return p
    return dim


def _matmul(a, b, *, ta=False, tb=False, epi="none", extra=None, out_dtype=F32, name):
    if ta:
        K, M = a.shape
    else:
        M, K = a.shape
    if tb:
        N, K2 = b.shape
    else:
        K2, N = b.shape
    assert K == K2, (a.shape, b.shape, ta, tb)
    tm = _tile(M, (1024, 512, 256, 128))
    tn = _tile(N, (1024, 512, 256, 128))
    tk = _tile(K, (512, 256, 128))
    nk = K // tk
    dn = (((0 if ta else 1,), (1 if tb else 0,)), ((), ()))
    n_extra = 0 if extra is None else 1
    n_out = 2 if epi == "relu2" else 1

    def body(*refs):
        a_ref, b_ref = refs[0], refs[1]
        e_ref = refs[2] if n_extra else None
        outs = refs[2 + n_extra:2 + n_extra + n_out]
        acc = refs[-1]
        k = pl.program_id(2)

        @pl.when(k == 0)
        def _():
            acc[...] = jnp.zeros_like(acc)

        acc[...] += lax.dot_general(a_ref[...].astype(BF16), b_ref[...].astype(BF16), dn,
                                    preferred_element_type=F32)

        @pl.when(k == nk - 1)
        def _():
            r = acc[...]
            if epi == "none":
                outs[0][...] = r.astype(outs[0].dtype)
            elif epi == "add":
                outs[0][...] = (r + e_ref[...]).astype(outs[0].dtype)
            elif epi == "relu2":
                outs[0][...] = r
                p = jnp.maximum(r, 0.0)
                outs[1][...] = (p * p).astype(outs[1].dtype)
            elif epi == "drelu2":
                outs[0][...] = (r * 2.0 * jnp.maximum(e_ref[...], 0.0)).astype(outs[0].dtype)

    a_spec = pl.BlockSpec((tk, tm), lambda i, j, k: (k, i)) if ta else pl.BlockSpec((tm, tk), lambda i, j, k: (i, k))
    b_spec = pl.BlockSpec((tn, tk), lambda i, j, k: (j, k)) if tb else pl.BlockSpec((tk, tn), lambda i, j, k: (k, j))
    o_spec = pl.BlockSpec((tm, tn), lambda i, j, k: (i, j))
    in_specs = [a_spec, b_spec] + ([o_spec] if n_extra else [])
    if epi == "relu2":
        out_shape = (jax.ShapeDtypeStruct((M, N), F32), jax.ShapeDtypeStruct((M, N), BF16))
        out_specs = (o_spec, o_spec)
    else:
        out_shape = jax.ShapeDtypeStruct((M, N), out_dtype)
        out_specs = o_spec
    args = (a, b) + ((extra,) if n_extra else ())
    return pl.pallas_call(
        body, name=name, grid=(M // tm, N // tn, nk), in_specs=in_specs, out_specs=out_specs,
        out_shape=out_shape, scratch_shapes=[pltpu.VMEM((tm, tn), F32)],
        compiler_params=_params(("parallel", "parallel", "arbitrary")),
    )(*args)


ROW_BLOCK = 256


def _rmsnorm_fwd(x, g, *, name):
    T, D = x.shape
    tr = _tile(T, (ROW_BLOCK, 128, 64))

    def body(x_ref, g_ref, o_ref):
        xv = x_ref[...]
        r = lax.rsqrt(jnp.mean(xv * xv, axis=-1, keepdims=True) + NORM_EPS)
        o_ref[...] = (xv * r * g_ref[...]).astype(o_ref.dtype)

    return pl.pallas_call(
        body, name=name, grid=(T // tr,),
        in_specs=[pl.BlockSpec((tr, D), lambda i: (i, 0)), pl.BlockSpec((1, D), lambda i: (0, 0))],
        out_specs=pl.BlockSpec((tr, D), lambda i: (i, 0)),
        out_shape=jax.ShapeDtypeStruct((T, D), BF16), compiler_params=_params(("parallel",)),
    )(x, g)


def _rmsnorm_bwd(x, g, dy, dres, *, name):
    T, D = x.shape
    tr = _tile(T, (ROW_BLOCK, 128, 64))

    def body(x_ref, g_ref, dy_ref, dres_ref, dx_ref, dg_ref):
        i = pl.program_id(0)
        xv = x_ref[...]
        r = lax.rsqrt(jnp.mean(xv * xv, axis=-1, keepdims=True) + NORM_EPS)
        xh = xv * r
        dyv = dy_ref[...].astype(F32)
        dxh = dyv * g_ref[...]
        dx = r * (dxh - xh * jnp.mean(dxh * xh, axis=-1, keepdims=True))
        dx_ref[...] = dres_ref[...] + dx

        @pl.when(i == 0)
        def _():
            dg_ref[...] = jnp.zeros_like(dg_ref)

        dg_ref[...] += jnp.sum(dyv * xh, axis=0, keepdims=True)

    blk = pl.BlockSpec((tr, D), lambda i: (i, 0))
    vec = pl.BlockSpec((1, D), lambda i: (0, 0))
    return pl.pallas_call(
        body, name=name, grid=(T // tr,), in_specs=[blk, vec, blk, blk], out_specs=(blk, vec),
        out_shape=(jax.ShapeDtypeStruct((T, D), F32), jax.ShapeDtypeStruct((1, D), F32)),
        compiler_params=_params(("arbitrary",)),
    )(x, g, dy, dres)


def _final_loss(x, g, target, *, name):
    T, D = x.shape
    tr = _tile(T, (ROW_BLOCK, 128, 64))

    def body(x_ref, g_ref, t_ref, dy_ref, l_ref):
        i = pl.program_id(0)
        xv = x_ref[...]
        r = lax.rsqrt(jnp.mean(xv * xv, axis=-1, keepdims=True) + NORM_EPS)
        err = xv * r * g_ref[...] - t_ref[...]
        dy_ref[...] = err * (1.0 / D)

        @pl.when(i == 0)
        def _():
            l_ref[...] = jnp.zeros_like(l_ref)

        part = 0.5 * jnp.sum(jnp.mean(err * err, axis=-1, keepdims=True), axis=0, keepdims=True)
        l_ref[...] += jnp.broadcast_to(part, l_ref.shape)

    blk = pl.BlockSpec((tr, D), lambda i: (i, 0))
    vec = pl.BlockSpec((1, D), lambda i: (0, 0))
    return pl.pallas_call(
        body, name=name, grid=(T // tr,), in_specs=[blk, vec, blk],
        out_specs=(blk, pl.BlockSpec((1, LANE), lambda i: (0, 0))),
        out_shape=(jax.ShapeDtypeStruct((T, D), F32), jax.ShapeDtypeStruct((1, LANE), F32)),
        compiler_params=_params(("arbitrary",)),
    )(x, g, target)


def _colsum(x, *, name):
    T, C = x.shape
    tc = _tile(C, (512, 256, 128))

    def body(x_ref, o_ref):
        o_ref[...] = jnp.sum(x_ref[...], axis=0, keepdims=True)

    return pl.pallas_call(
        body, name=name, grid=(C // tc,), in_specs=[pl.BlockSpec((T, tc), lambda j: (0, j))],
        out_specs=pl.BlockSpec((1, tc), lambda j: (0, j)),
        out_shape=jax.ShapeDtypeStruct((1, C), F32), compiler_params=_params(("parallel",)),
    )(x)


ADAM_C1 = 1.0 - ADAM_B1 ** ADAM_STEP
ADAM_C2 = 1.0 - ADAM_B2 ** ADAM_STEP


def _adamw(w, m, v, grads, *, name):
    R, C = w.shape
    tr = _tile(R, (256, 128, 64, 32, 16, 8))
    n_g = len(grads)

    def body(*refs):
        w_ref, m_ref, v_ref = refs[:3]
        g_refs = refs[3:3 + n_g]
        g_out, d_out, m_out, v_out = refs[3 + n_g:]
        g = g_refs[0][...].astype(F32)
        for r in g_refs[1:]:
            g = g + r[...].astype(F32)
        mn = ADAM_B1 * m_ref[...] + (1.0 - ADAM_B1) * g
        vn = ADAM_B2 * v_ref[...] + (1.0 - ADAM_B2) * (g * g)
        m_hat = mn / ADAM_C1
        v_hat = vn / ADAM_C2
        g_out[...] = g
        d_out[...] = -ADAM_LR * (m_hat / (jnp.sqrt(v_hat) + ADAM_EPS) + ADAM_WD * w_ref[...])
        m_out[...] = mn
        v_out[...] = vn

    blk = pl.BlockSpec((tr, C), lambda i: (i, 0))
    sh = jax.ShapeDtypeStruct((R, C), F32)
    return pl.pallas_call(
        body, name=name, grid=(R // tr,), in_specs=[blk] * (3 + n_g), out_specs=(blk,) * 4,
        out_shape=(sh,) * 4, compiler_params=_params(("parallel",)),
    )(w, m, v, *grads)


def _sum_parts(parts, *, name):
    P, R, C = parts.shape
    tr = _tile(R, (256, 128, 64, 32, 16, 8))

    def body(p_ref, o_ref):
        s = p_ref[0].astype(F32)
        for i in range(1, P):
            s = s + p_ref[i].astype(F32)
        o_ref[...] = s

    return pl.pallas_call(
        body, name=name, grid=(R // tr,), in_specs=[pl.BlockSpec((P, tr, C), lambda i: (0, i, 0))],
        out_specs=pl.BlockSpec((tr, C), lambda i: (i, 0)),
        out_shape=jax.ShapeDtypeStruct((R, C), F32), compiler_params=_params(("parallel",)),
    )(parts)


def _ret_tables(T):
    H, C = RET_HEADS, CHUNK
    log_gamma = jnp.log1p(-jnp.exp2(-5.0 - jnp.arange(H, dtype=F32)))
    pos = jnp.arange(C, dtype=F32)
    dist = jnp.abs(pos[:, None] - pos[None, :])
    dm = jnp.exp(log_gamma[:, None, None] * dist)
    qdec = jnp.exp(log_gamma[:, None] * (pos + 1.0))[:, :, None]
    kdec = jnp.exp(log_gamma[:, None] * (C - 1.0 - pos))[:, :, None]
    cdec = jnp.exp(log_gamma * C)[:, None, None]
    inv = ROPE_BASE ** (-jnp.arange(0, RET_DK, 2, dtype=F32) / RET_DK)
    ang = jnp.arange(T, dtype=F32)[:, None] * inv[None, :]
    return dm, qdec, kdec, cdec, jnp.cos(ang), jnp.sin(ang)


def _rot(x, cos, sin):
    h = x.shape[1] // 2
    x1, x2 = x[:, :h], x[:, h:]
    return jnp.concatenate([x1 * cos - x2 * sin, x1 * sin + x2 * cos], axis=1)


def _unrot(dy, cos, sin):
    h = dy.shape[1] // 2
    d1, d2 = dy[:, :h], dy[:, h:]
    return jnp.concatenate([d1 * cos + d2 * sin, d2 * cos - d1 * sin], axis=1)


def _ret_specs(N, rev):
    H, C, DK, DV = RET_HEADS, CHUNK, RET_DK, RET_DV
    cn = (lambda n: N - 1 - n) if rev else (lambda n: n)
    q = pl.BlockSpec((C, DK), lambda h, n: (cn(n), h))
    k = pl.BlockSpec((C, DK), lambda h, n: (cn(n), H + h))
    v = pl.BlockSpec((C, DV), lambda h, n: (cn(n), H + h))
    g = pl.BlockSpec((C, DV), lambda h, n: (cn(n), 2 * H + h))
    cs = pl.BlockSpec((C, DK // 2), lambda h, n: (cn(n), 0))
    dm = pl.BlockSpec((1, C, C), lambda h, n: (h, 0, 0))
    dec = pl.BlockSpec((1, C, 1), lambda h, n: (h, 0, 0))
    cd = pl.BlockSpec((1, 1, 1), lambda h, n: (h, 0, 0))
    gain = pl.BlockSpec((1, 1, DV), lambda h, n: (h, 0, 0))
    st = pl.BlockSpec((1, 1, DK, DV), lambda h, n: (h, cn(n), 0, 0))
    ov = pl.BlockSpec((C, DV), lambda h, n: (cn(n), h))
    ok = pl.BlockSpec((C, DK), lambda h, n: (cn(n), h))
    return q, k, v, g, cs, dm, dec, cd, gain, st, ov, ok


def _ret_fwd(proj, gain, tables, *, name):
    T = proj.shape[0]
    H, C, DK, DV = RET_HEADS, CHUNK, RET_DK, RET_DV
    N = T // C
    dm_t, qdec_t, kdec_t, cdec_t, cos_t, sin_t = tables

    def body(q_ref, k_ref, v_ref, g_ref, cos_ref, sin_ref, dm_ref, qd_ref, kd_ref, cd_ref, gain_ref,
             og_ref, st_ref, s_acc):
        n = pl.program_id(1)

        @pl.when(n == 0)
        def _():
            s_acc[...] = jnp.zeros_like(s_acc)

        cos, sin = cos_ref[...], sin_ref[...]
        qr = _rot(q_ref[...], cos, sin)
        kr = _rot(k_ref[...], cos, sin) * (DK ** -0.5)
        v = v_ref[...]
        sp = s_acc[...]
        st_ref[0, 0] = sp.astype(st_ref.dtype)
        scores = _bdot(qr, kr, NT) * dm_ref[0]
        o = _bdot(scores, v) + _bdot(qr * qd_ref[0], sp)
        s_acc[...] = sp * cd_ref[0] + _bdot(kr * kd_ref[0], v, TN)
        oc = o - jnp.mean(o, axis=-1, keepdims=True)
        rstd = lax.rsqrt(jnp.mean(oc * oc, axis=-1, keepdims=True) + NORM_EPS)
        og_ref[...] = (oc * rstd * gain_ref[0] * _silu(g_ref[...])).astype(og_ref.dtype)

    q, k, v, g, cs, dm, dec, cd, gn, st, ov, _ = _ret_specs(N, False)
    return pl.pallas_call(
        body, name=name, grid=(H, N),
        in_specs=[q, k, v, g, cs, cs, dm, dec, dec, cd, gn], out_specs=(ov, st),
        out_shape=(jax.ShapeDtypeStruct((T, H * DV), BF16), jax.ShapeDtypeStruct((H, N, DK, DV), BF16)),
        scratch_shapes=[pltpu.VMEM((DK, DV), F32)],
        compiler_params=_params(("arbitrary", "arbitrary")),
    )(proj, proj, proj, proj, cos_t, sin_t, dm_t, qdec_t, kdec_t, cdec_t, gain)


def _ret_bwd(proj, gain, tables, states, dog, *, name):
    T = proj.shape[0]
    H, C, DK, DV = RET_HEADS, CHUNK, RET_DK, RET_DV
    N = T // C
    dm_t, qdec_t, kdec_t, cdec_t, cos_t, sin_t = tables

    def body(q_ref, k_ref, v_ref, g_ref, cos_ref, sin_ref, dm_ref, qd_ref, kd_ref, cd_ref, gain_ref,
             st_ref, dog_ref, dq_ref, dk_ref, dv_ref, dg_ref, dgain_ref, ds_acc):
        n = pl.program_id(1)

        @pl.when(n == 0)
        def _():
            ds_acc[...] = jnp.zeros_like(ds_acc)
            dgain_ref[...] = jnp.zeros_like(dgain_ref)

        cos, sin = cos_ref[...], sin_ref[...]
        scale = DK ** -0.5
        qr = _rot(q_ref[...], cos, sin)
        kr = _rot(k_ref[...], cos, sin) * scale
        v = v_ref[...]
        g = g_ref[...]
        sp = st_ref[0, 0]
        dm = dm_ref[0]
        qd, kd = qd_ref[0], kd_ref[0]
        gain_v = gain_ref[0]
        scores = _bdot(qr, kr, NT) * dm
        o = _bdot(scores, v) + _bdot(qr * qd, sp)
        oc = o - jnp.mean(o, axis=-1, keepdims=True)
        rstd = lax.rsqrt(jnp.mean(oc * oc, axis=-1, keepdims=True) + NORM_EPS)
        oh = oc * rstd
        dy = dog_ref[...].astype(F32)
        dg_ref[...] = (dy * oh * gain_v * _dsilu(g)).astype(dg_ref.dtype)
        dnorm = dy * _silu(g)
        dgain_ref[0] += jnp.sum(dnorm * oh, axis=0, keepdims=True)
        doh = dnorm * gain_v
        do = rstd * (doh - jnp.mean(doh, axis=-1, keepdims=True)
                     - oh * jnp.mean(doh * oh, axis=-1, keepdims=True))
        dsn = ds_acc[...]
        dp = _bdot(do, v, NT) * dm
        dqr = _bdot(dp, kr) + _bdot(do, sp, NT) * qd
        dkr = _bdot(dp, qr, TN) + _bdot(v, dsn, NT) * kd
        dv_ref[...] = (_bdot(scores, do, TN) + _bdot(kr * kd, dsn)).astype(dv_ref.dtype)
        ds_acc[...] = dsn * cd_ref[0] + _bdot(qr * qd, do, TN)
        dq_ref[...] = _unrot(dqr, cos, sin).astype(dq_ref.dtype)
        dk_ref[...] = _unrot(dkr * scale, cos, sin).astype(dk_ref.dtype)

    q, k, v, g, cs, dm, dec, cd, gn, st, ov, ok = _ret_specs(N, True)
    return pl.pallas_call(
        body, name=name, grid=(H, N),
        in_specs=[q, k, v, g, cs, cs, dm, dec, dec, cd, gn, st, ov],
        out_specs=(ok, ok, ov, ov, gn),
        out_shape=(jax.ShapeDtypeStruct((T, H * DK), BF16), jax.ShapeDtypeStruct((T, H * DK), BF16),
                   jax.ShapeDtypeStruct((T, H * DV), BF16), jax.ShapeDtypeStruct((T, H * DV), BF16),
                   jax.ShapeDtypeStruct((H, 1, DV), F32)),
        scratch_shapes=[pltpu.VMEM((DK, DV), F32)],
        compiler_params=_params(("arbitrary", "arbitrary")),
    )(proj, proj, proj, proj, cos_t, sin_t, dm_t, qdec_t, kdec_t, cdec_t, gain, states, dog)


def _gla_specs(N, rev):
    H, C, DK, DV = GLA_HEADS, CHUNK, GLA_DK, GLA_DV
    cn = (lambda n: N - 1 - n) if rev else (lambda n: n)
    q = pl.BlockSpec((C, DK), lambda h, n: (cn(n), h))
    k = pl.BlockSpec((C, DK), lambda h, n: (cn(n), H + h))
    v = pl.BlockSpec((C, DV), lambda h, n: (cn(n), H + h))
    r = pl.BlockSpec((C, DV), lambda h, n: (cn(n), 2 * H + h))
    bias = pl.BlockSpec((1, DK), lambda h, n: (0, h))
    gain = pl.BlockSpec((1, 1, DV), lambda h, n: (h, 0, 0))
    st = pl.BlockSpec((1, 1, DV, DK), lambda h, n: (h, cn(n), 0, 0))
    ov = pl.BlockSpec((C, DV), lambda h, n: (cn(n), h))
    return q, k, v, r, bias, gain, st, ov


def _gla_chunk(q_ref, k_ref, v_ref, gl_ref, bias_ref):
    C, DK = q_ref.shape
    gl = gl_ref[...] + bias_ref[...]
    la = -_softplus(-gl) * (1.0 / GLA_TAU)
    lower = _iota2((C, C), 0) >= _iota2((C, C), 1)
    cum = _fdot(jnp.where(lower, 1.0, 0.0), la)
    ref = _pick_row(cum, C // 2 - 1)
    clast = _pick_row(cum, C - 1)
    fw, bw = jnp.exp(cum - ref), jnp.exp(ref - cum)
    qs = q_ref[...] * (DK ** -0.5)
    k = k_ref[...]
    s_lo = _bdot(qs * fw, k * bw, NT)
    s_up = _bdot(qs * bw, k * fw, NT)
    scores = jnp.where(lower, s_lo, s_up)
    return gl, cum, clast, fw, bw, qs, k, v_ref[...], scores, lower


def _gla_fwd(proj, glogit, bias, gain, *, name):
    T = proj.shape[0]
    H, C, DK, DV = GLA_HEADS, CHUNK, GLA_DK, GLA_DV
    N = T // C

    def body(q_ref, k_ref, v_ref, r_ref, gl_ref, bias_ref, gain_ref, og_ref, st_ref, s_acc):
        n = pl.program_id(1)

        @pl.when(n == 0)
        def _():
            s_acc[...] = jnp.zeros_like(s_acc)

        gl, cum, clast, fw, bw, qs, k, v, scores, lower = _gla_chunk(q_ref, k_ref, v_ref, gl_ref, bias_ref)
        sp = s_acc[...]
        st_ref[0, 0] = sp
        o = _bdot(scores, v) + _bdot(qs * jnp.exp(cum), sp, NT)
        s_acc[...] = sp * jnp.exp(clast) + _bdot(v, k * jnp.exp(clast - cum), TN)
        rstd = lax.rsqrt(jnp.mean(o * o, axis=-1, keepdims=True) + NORM_EPS)
        og_ref[...] = (o * rstd * gain_ref[0] * _silu(r_ref[...])).astype(og_ref.dtype)

    q, k, v, r, bias_s, gn, st, ov = _gla_specs(N, False)
    return pl.pallas_call(
        body, name=name, grid=(H, N), in_specs=[q, k, v, r, q, bias_s, gn], out_specs=(ov, st),
        out_shape=(jax.ShapeDtypeStruct((T, H * DV), BF16), jax.ShapeDtypeStruct((H, N, DV, DK), F32)),
        scratch_shapes=[pltpu.VMEM((DV, DK), F32)],
        compiler_params=_params(("arbitrary", "arbitrary")),
    )(proj, proj, proj, proj, glogit, bias, gain)


def _gla_bwd(proj, glogit, bias, gain, states, dog, *, name):
    T = proj.shape[0]
    H, C, DK, DV = GLA_HEADS, CHUNK, GLA_DK, GLA_DV
    N = T // C

    def body(q_ref, k_ref, v_ref, r_ref, gl_ref, bias_ref, gain_ref, st_ref, dog_ref,
             dq_ref, dk_ref, dv_ref, dr_ref, dgl_ref, dgain_ref, ds_acc):
        n = pl.program_id(1)

        @pl.when(n == 0)
        def _():
            ds_acc[...] = jnp.zeros_like(ds_acc)
            dgain_ref[...] = jnp.zeros_like(dgain_ref)

        gl, cum, clast, fw, bw, qs, k, v, scores, lower = _gla_chunk(q_ref, k_ref, v_ref, gl_ref, bias_ref)
        sp = st_ref[0, 0]
        ecum, e2, cdec = jnp.exp(cum), jnp.exp(clast - cum), jnp.exp(clast)
        q_in, k_end = qs * ecum, k * e2
        o = _bdot(scores, v) + _bdot(q_in, sp, NT)
        rstd = lax.rsqrt(jnp.mean(o * o, axis=-1, keepdims=True) + NORM_EPS)
        oh = o * rstd
        r = r_ref[...]
        gain_v = gain_ref[0]
        dy = dog_ref[...].astype(F32)
        dr_ref[...] = (dy * oh * gain_v * _dsilu(r)).astype(dr_ref.dtype)
        dnorm = dy * _silu(r)
        dgain_ref[0] += jnp.sum(dnorm * oh, axis=0, keepdims=True)
        doh = dnorm * gain_v
        do = rstd * (doh - oh * jnp.mean(doh * oh, axis=-1, keepdims=True))
        dsn = ds_acc[...]
        dq_in = _bdot(do, sp)
        dk_end = _bdot(v, dsn)
        dv_ref[...] = (_bdot(k_end, dsn, NT) + _bdot(scores, do, TN)).astype(dv_ref.dtype)
        dcdec = jnp.sum(dsn * sp, axis=0, keepdims=True)
        ds_acc[...] = dsn * cdec + _bdot(do, q_in, TN)
        dsc = _bdot(do, v, NT)
        ds_lo = jnp.where(lower, dsc, 0.0)
        ds_up = jnp.where(lower, 0.0, dsc)
        qf, kb, qb, kf = qs * fw, k * bw, qs * bw, k * fw
        dqf, dkb = _bdot(ds_lo, kb), _bdot(ds_lo, qf, TN)
        dqb, dkf = _bdot(ds_up, kf), _bdot(ds_up, qb, TN)
        dq_ref[...] = ((dqf * fw + dqb * bw + dq_in * ecum) * (DK ** -0.5)).astype(dq_ref.dtype)
        dk_ref[...] = (dkb * bw + dkf * fw + dk_end * e2).astype(dk_ref.dtype)
        dz = (dqf * qs + dkf * k) * fw - (dqb * qs + dkb * k) * bw
        kk = dk_end * k_end
        dcum = dz + dq_in * q_in - kk
        rows = _iota2((C, DK), 0)
        dcum = dcum + jnp.where(rows == C // 2 - 1, -jnp.sum(dz, axis=0, keepdims=True), 0.0)
        dcum = dcum + jnp.where(rows == C - 1, jnp.sum(kk, axis=0, keepdims=True) + dcdec * cdec, 0.0)
        upper = _iota2((C, C), 0) <= _iota2((C, C), 1)
        dla = _fdot(jnp.where(upper, 1.0, 0.0), dcum)
        dgl_ref[...] = dla * (1.0 / GLA_TAU) * _sigmoid(-gl)

    q, k, v, r, bias_s, gn, st, ov = _gla_specs(N, True)
    return pl.pallas_call(
        body, name=name, grid=(H, N), in_specs=[q, k, v, r, q, bias_s, gn, st, ov],
        out_specs=(q, q, ov, ov, q, gn),
        out_shape=(jax.ShapeDtypeStruct((T, H * DK), BF16), jax.ShapeDtypeStruct((T, H * DK), BF16),
                   jax.ShapeDtypeStruct((T, H * DV), BF16), jax.ShapeDtypeStruct((T, H * DV), BF16),
                   jax.ShapeDtypeStruct((T, H * DK), F32), jax.ShapeDtypeStruct((H, 1, DV), F32)),
        scratch_shapes=[pltpu.VMEM((DV, DK), F32)],
        compiler_params=_params(("arbitrary", "arbitrary")),
    )(proj, proj, proj, proj, glogit, bias, gain, states, dog)


def _conv(xv, w_ref):
    out = _shift_down(xv, CONV_WIDTH - 1) * w_ref[0:1, :]
    for tap in range(1, CONV_WIDTH):
        out = out + _shift_down(xv, CONV_WIDTH - 1 - tap) * w_ref[tap:tap + 1, :]
    return out


def _conv_bwd(xv, w_ref, dpre, dw_ref):
    dx = None
    for tap in range(CONV_WIDTH):
        s = CONV_WIDTH - 1 - tap
        t = _shift_up(dpre, s) * w_ref[tap:tap + 1, :]
        dx = t if dx is None else dx + t
        dw_ref[tap:tap + 1, :] = jnp.sum(dpre * _shift_down(xv, s), axis=0, keepdims=True)
    return dx


CONV_COLS = 256


def _conv_silu_fwd(x, w, *, name):
    T = x.shape[0]
    n = w.shape[1]

    def body(x_ref, w_ref, o_ref):
        o_ref[...] = _silu(_conv(x_ref[...], w_ref))

    return pl.pallas_call(
        body, name=name, grid=(n // CONV_COLS,),
        in_specs=[pl.BlockSpec((T, CONV_COLS), lambda j: (0, j)), pl.BlockSpec((CONV_WIDTH, CONV_COLS), lambda j: (0, j))],
        out_specs=pl.BlockSpec((T, CONV_COLS), lambda j: (0, j)),
        out_shape=jax.ShapeDtypeStruct((T, n), F32), compiler_params=_params(("parallel",)),
    )(x, w)


def _conv_silu_bwd(x, w, dact, *, name):
    T = x.shape[0]
    n = w.shape[1]

    def body(x_ref, w_ref, da_ref, dx_ref, dw_ref):
        xv = x_ref[...]
        dpre = da_ref[...] * _dsilu(_conv(xv, w_ref))
        dx_ref[...] = _conv_bwd(xv, w_ref, dpre, dw_ref).astype(dx_ref.dtype)

    blk = pl.BlockSpec((T, CONV_COLS), lambda j: (0, j))
    wb = pl.BlockSpec((CONV_WIDTH, CONV_COLS), lambda j: (0, j))
    return pl.pallas_call(
        body, name=name, grid=(n // CONV_COLS,), in_specs=[blk, wb, blk], out_specs=(blk, wb),
        out_shape=(jax.ShapeDtypeStruct((T, n), BF16), jax.ShapeDtypeStruct((CONV_WIDTH, n), F32)),
        compiler_params=_params(("parallel",)),
    )(x, w, dact)


def _unit_lower_inverse(a):
    n = a.shape[0]
    eye = jnp.where(_iota2((n, n), 0) == _iota2((n, n), 1), 1.0, 0.0)
    p = -a
    t = eye + p
    for _ in range(5):
        p = _fdot(p, p)
        t = t + _fdot(t, p)
    return t


def _gdn_specs(N, rev):
    H, C, DK, DV = GDN_HEADS, CHUNK, GDN_DK, GDN_DV
    cn = (lambda n: N - 1 - n) if rev else (lambda n: n)
    q = pl.BlockSpec((C, DK), lambda h, n: (cn(n), h))
    k = pl.BlockSpec((C, DK), lambda h, n: (cn(n), H + h))
    v = pl.BlockSpec((C, DV), lambda h, n: (cn(n), 2 * H + h))
    z = pl.BlockSpec((C, DV), lambda h, n: (cn(n), 3 * H + h))
    gates = pl.BlockSpec((C, LANE), lambda h, n: (cn(n), 0))
    sc = pl.BlockSpec((1, 1, 1), lambda h, n: (h, 0, 0))
    gain = pl.BlockSpec((1, DV), lambda h, n: (0, 0))
    st = pl.BlockSpec((1, 1, DK, DV), lambda h, n: (h, cn(n), 0, 0))
    return q, k, v, z, gates, sc, gain, st


def _gdn_chunk(q_ref, k_ref, v_ref, gates_ref, alog_ref, dtb_ref, h):
    H, C, DK, DV = GDN_HEADS, CHUNK, GDN_DK, GDN_DV
    gates = gates_ref[...]
    lane = _iota2(gates.shape, 1)
    bl = jnp.sum(jnp.where(lane == h, gates, 0.0), axis=1, keepdims=True)
    al = jnp.sum(jnp.where(lane == H + h, gates, 0.0), axis=1, keepdims=True)
    beta = _sigmoid(bl)
    ea = jnp.exp(alog_ref[0])
    xs = al + dtb_ref[0]
    la = -ea * _softplus(xs)
    ii, jj = _iota2((C, C), 0), _iota2((C, C), 1)
    strict = ii > jj
    cum_col = jnp.sum(jnp.where(ii >= jj, _col_to_row(la), 0.0), axis=1, keepdims=True)
    cum_row = jnp.sum(jnp.where(ii <= jj, la, 0.0), axis=0, keepdims=True)
    q, k, v = q_ref[...], k_ref[...], v_ref[...]
    rq = lax.rsqrt(jnp.sum(q * q, axis=-1, keepdims=True) + NORM_EPS)
    rk = lax.rsqrt(jnp.sum(k * k, axis=-1, keepdims=True) + NORM_EPS)
    qn = q * rq * (DK ** -0.5)
    kn = k * rk
    rel = jnp.where(strict, jnp.exp(jnp.where(strict, cum_col - cum_row, 0.0)), 0.0)
    rg = rel * _bdot(kn, kn, NT)
    a = beta * rg
    tm = _unit_lower_inverse(a)
    e_col = jnp.exp(cum_col)
    clast = _pick_row(cum_col, C - 1)
    rhs = jnp.concatenate([beta * v, (beta * e_col) * kn], axis=1)
    sol = _fdot(tm, rhs)
    u, w = sol[:, :DV], sol[:, DV:]
    dd = jnp.exp(clast - cum_col)
    ke = kn * dd
    g = jnp.exp(clast)
    eye = jnp.where(_iota2((DK, DK), 0) == _iota2((DK, DK), 1), 1.0, 0.0)
    trans = g * eye - _bdot(ke, w, TN)
    inject = _bdot(ke, u, TN)
    return dict(beta=beta, ea=ea, xs=xs, la=la, strict=strict, ii=ii, jj=jj, q=q, k=k, v=v, rq=rq, rk=rk,
                qn=qn, kn=kn, rel=rel, rg=rg, a=a, tm=tm, e_col=e_col, sol=sol, u=u, w=w, dd=dd, ke=ke,
                g=g, eye=eye, trans=trans, inject=inject)


def _gdn_fwd(act, proj, gates, a_log, dt_bias, gain, *, name):
    T = act.shape[0]
    H, C, DK, DV = GDN_HEADS, CHUNK, GDN_DK, GDN_DV
    N = T // C

    def body(q_ref, k_ref, v_ref, z_ref, gates_ref, alog_ref, dtb_ref, gain_ref, og_ref, st_ref, s_acc):
        h, n = pl.program_id(0), pl.program_id(1)

        @pl.when(n == 0)
        def _():
            s_acc[...] = jnp.zeros_like(s_acc)

        c = _gdn_chunk(q_ref, k_ref, v_ref, gates_ref, alog_ref, dtb_ref, h)
        sp = s_acc[...]
        st_ref[0, 0] = sp
        snew = _bdot(c["trans"], sp) + c["inject"]
        s_acc[...] = snew
        o = _bdot(c["qn"], snew)
        rstd = lax.rsqrt(jnp.mean(o * o, axis=-1, keepdims=True) + NORM_EPS)
        og_ref[...] = (o * rstd * gain_ref[...] * _silu(z_ref[...])).astype(og_ref.dtype)

    q, k, v, z, gt, sc, gn, st = _gdn_specs(N, False)
    return pl.pallas_call(
        body, name=name, grid=(H, N), in_specs=[q, k, v, z, gt, sc, sc, gn], out_specs=(q, st),
        out_shape=(jax.ShapeDtypeStruct((T, H * DV), BF16), jax.ShapeDtypeStruct((H, N, DK, DV), F32)),
        scratch_shapes=[pltpu.VMEM((DK, DV), F32)],
        compiler_params=_params(("arbitrary", "arbitrary")),
    )(act, act, act, proj, gates, a_log, dt_bias, gain)


def _gdn_bwd(act, proj, gates, a_log, dt_bias, gain, states, dog, *, name):
    T = act.shape[0]
    H, C, DK, DV = GDN_HEADS, CHUNK, GDN_DK, GDN_DV
    N = T // C

    def rsum(x):
        return jnp.sum(x, axis=1, keepdims=True)

    def body(q_ref, k_ref, v_ref, z_ref, gates_ref, alog_ref, dtb_ref, gain_ref, st_ref, dog_ref,
             dq_ref, dk_ref, dv_ref, dz_ref, dgates_ref, dscal_ref, dgain_ref, ds_acc):
        h, n = pl.program_id(0), pl.program_id(1)

        @pl.when(n == 0)
        def _():
            ds_acc[...] = jnp.zeros_like(ds_acc)
            dgain_ref[...] = jnp.zeros_like(dgain_ref)
            dscal_ref[...] = jnp.zeros_like(dscal_ref)

        c = _gdn_chunk(q_ref, k_ref, v_ref, gates_ref, alog_ref, dtb_ref, h)
        beta, kn, qn, v, ke, u, w, dd, e_col = c["beta"], c["kn"], c["qn"], c["v"], c["ke"], c["u"], c["w"], c["dd"], c["e_col"]
        sp = st_ref[0, 0]
        snew = _bdot(c["trans"], sp) + c["inject"]
        o = _bdot(qn, snew)
        rstd = lax.rsqrt(jnp.mean(o * o, axis=-1, keepdims=True) + NORM_EPS)
        oh = o * rstd
        z = z_ref[...]
        gain_v = gain_ref[...]
        dy = dog_ref[...].astype(F32)
        dz_ref[...] = (dy * oh * gain_v * _dsilu(z)).astype(dz_ref.dtype)
        dnorm = dy * _silu(z)
        dgain_ref[0] += jnp.sum(dnorm * oh, axis=0, keepdims=True)
        doh = dnorm * gain_v
        do = rstd * (doh - oh * jnp.mean(doh * oh, axis=-1, keepdims=True))

        dstot = ds_acc[...] + _bdot(qn, do, TN)
        dqn = _bdot(do, snew, NT)
        dtrans = _bdot(dstot, sp, NT)
        ds_acc[...] = _bdot(c["trans"], dstot, TN)
        dg = jnp.sum(jnp.sum(dtrans * c["eye"], axis=1, keepdims=True), axis=0, keepdims=True)
        m = -dtrans
        dke = _bdot(w, m, NT) + _bdot(u, dstot, NT)
        dw = _bdot(ke, m)
        du = _bdot(ke, dstot)
        drhs = _fdot(c["tm"], jnp.concatenate([du, dw], axis=1), TN)
        da = jnp.where(c["strict"], -_fdot(drhs, c["sol"], NT), 0.0)
        drhs_u, drhs_w = drhs[:, :DV], drhs[:, DV:]
        rwk = rsum(drhs_w * kn)
        dbeta = rsum(da * c["rg"]) + rsum(drhs_u * v) + rwk * e_col
        dgm = da * beta * c["rel"]
        dkn = _bdot(dgm, kn) + _bdot(dgm, kn, TN) + (beta * e_col) * drhs_w + dd * dke
        dv_ref[...] = beta * drhs_u
        r_ = da * c["a"]
        ddd = rsum(dke * kn)
        dc = rsum(r_) - _row_to_col(jnp.sum(r_, axis=0, keepdims=True)) + beta * rwk * e_col - ddd * dd
        dclast = jnp.sum(ddd * dd, axis=0, keepdims=True) + dg * c["g"]
        dc = dc + jnp.where(_iota2((C, 1), 0) == C - 1, dclast, 0.0)
        dla = jnp.sum(jnp.where(c["ii"] <= c["jj"], _col_to_row(dc), 0.0), axis=1, keepdims=True)
        dalog = jnp.sum(dla * c["la"], axis=0, keepdims=True)
        dxs = dla * (-c["ea"]) * _sigmoid(c["xs"])
        ddtb = jnp.sum(dxs, axis=0, keepdims=True)
        dbl = dbeta * beta * (1.0 - beta)
        lane = _iota2((C, LANE), 1)
        dgates_ref[0] = jnp.where(lane == 0, dbl, jnp.where(lane == 1, dxs, 0.0))
        lane8 = _iota2((8, LANE), 1)
        dscal_ref[0] += jnp.where(lane8 == 0, dalog, jnp.where(lane8 == 1, ddtb, 0.0))
        dk_ref[...] = c["rk"] * (dkn - kn * rsum(dkn * kn))
        qh = c["q"] * c["rq"]
        dqs = dqn * (DK ** -0.5)
        dq_ref[...] = c["rq"] * (dqs - qh * rsum(dqs * qh))

    q, k, v, z, gt, sc, gn, st = _gdn_specs(N, True)
    dgt = pl.BlockSpec((1, C, LANE), lambda h, n: (h, N - 1 - n, 0))
    dsc = pl.BlockSpec((1, 8, LANE), lambda h, n: (h, 0, 0))
    dgn = pl.BlockSpec((1, 1, DV), lambda h, n: (h, 0, 0))
    sh = jax.ShapeDtypeStruct((T, H * DK), F32)
    return pl.pallas_call(
        body, name=name, grid=(H, N), in_specs=[q, k, v, z, gt, sc, sc, gn, st, q],
        out_specs=(q, q, q, q, dgt, dsc, dgn),
        out_shape=(sh, sh, sh, jax.ShapeDtypeStruct((T, H * DV), BF16),
                   jax.ShapeDtypeStruct((H, T, LANE), F32), jax.ShapeDtypeStruct((H, 8, LANE), F32),
                   jax.ShapeDtypeStruct((H, 1, DV), F32)),
        scratch_shapes=[pltpu.VMEM((DK, DV), F32)],
        compiler_params=_params(("arbitrary", "arbitrary")),
    )(act, act, act, proj, gates, a_log, dt_bias, gain, states, dog)


SUBLANES = 8


def _linear_scan(a_ref, b_ref, h_ref, reverse):
    T, W = a_ref.shape
    nb = T // SUBLANES
    row = _iota2((SUBLANES, W), 0)

    def blk(bi, carry):
        bb = (nb - 1 - bi) if reverse else bi
        off = pl.multiple_of(bb * SUBLANES, SUBLANES)
        a = a_ref[pl.ds(off, SUBLANES), :]
        b = b_ref[pl.ds(off, SUBLANES), :]
        for d in (1, 2, 4):
            if reverse:
                edge = row >= SUBLANES - d
                a_sh = jnp.where(edge, 1.0, pltpu.roll(a, SUBLANES - d, 0))
                b_sh = jnp.where(edge, 0.0, pltpu.roll(b, SUBLANES - d, 0))
            else:
                edge = row < d
                a_sh = jnp.where(edge, 1.0, pltpu.roll(a, d, 0))
                b_sh = jnp.where(edge, 0.0, pltpu.roll(b, d, 0))
            b = a * b_sh + b
            a = a * a_sh
        h = a * carry + b
        h_ref[pl.ds(off, SUBLANES), :] = h
        return h[0:1, :] if reverse else h[SUBLANES - 1:SUBLANES, :]

    lax.fori_loop(0, nb, blk, jnp.zeros((1, W), F32))


def _lru_specs(T):
    B, W = LRU_BLOCKS, LRU_BLOCK
    xb = pl.BlockSpec((T, W), lambda j: (0, j))
    yb = pl.BlockSpec((T, W), lambda j: (0, B + j))
    cw = pl.BlockSpec((CONV_WIDTH, W), lambda j: (0, j))
    vec = pl.BlockSpec((1, W), lambda j: (0, j))
    wg = pl.BlockSpec((1, W, W), lambda j: (j, 0, 0))
    bg = pl.BlockSpec((1, 1, W), lambda j: (j, 0, 0))
    return xb, yb, cw, vec, wg, bg


def _lru_gates(xb_ref, cw_ref, cb_ref, wr_ref, br_ref, wi_ref, bi_ref, lam_ref):
    xv = xb_ref[...]
    xc = _conv(xv, cw_ref) + cb_ref[...]
    r = _sigmoid(_bdot(xc, wr_ref[0]) + br_ref[0])
    i = _sigmoid(_bdot(xc, wi_ref[0]) + bi_ref[0])
    sp = _softplus(-lam_ref[...])
    la = -LRU_C * sp * r
    a = jnp.exp(la)
    s = jnp.sqrt(-_expm1(2.0 * la))
    return xv, xc, r, i, sp, a, s


def _lru_fwd(proj, conv_w, conv_b, w_r, b_r, w_i, b_i, lam, *, name):
    T = proj.shape[0]
    B, W = LRU_BLOCKS, LRU_BLOCK

    def body(xb_ref, yb_ref, cw_ref, cb_ref, wr_ref, br_ref, wi_ref, bi_ref, lam_ref, og_ref, hs_ref, a_s, u_s):
        xv, xc, r, i, sp, a, s = _lru_gates(xb_ref, cw_ref, cb_ref, wr_ref, br_ref, wi_ref, bi_ref, lam_ref)
        a_s[...] = a
        u_s[...] = s * (i * xc)
        _linear_scan(a_s, u_s, hs_ref, False)
        og_ref[...] = (hs_ref[...] * _gelu(yb_ref[...])).astype(og_ref.dtype)

    xb, yb, cw, vec, wg, bg = _lru_specs(T)
    return pl.pallas_call(
        body, name=name, grid=(B,), in_specs=[xb, yb, cw, vec, wg, bg, wg, bg, vec], out_specs=(xb, xb),
        out_shape=(jax.ShapeDtypeStruct((T, B * W), BF16), jax.ShapeDtypeStruct((T, B * W), F32)),
        scratch_shapes=[pltpu.VMEM((T, W), F32), pltpu.VMEM((T, W), F32)],
        compiler_params=_params(("arbitrary",)),
    )(proj, proj, conv_w, conv_b, w_r, b_r, w_i, b_i, lam)


def _lru_bwd(proj, conv_w, conv_b, w_r, b_r, w_i, b_i, lam, hs, dout, *, name):
    T = proj.shape[0]
    B, W = LRU_BLOCKS, LRU_BLOCK

    def csum(x):
        return jnp.sum(x, axis=0, keepdims=True)

    def body(xb_ref, yb_ref, cw_ref, cb_ref, wr_ref, br_ref, wi_ref, bi_ref, lam_ref, hs_ref, do_ref,
             dxb_ref, dyb_ref, dcw_ref, dcb_ref, dwr_ref, dbr_ref, dwi_ref, dbi_ref, dlam_ref, a_s, b_s, g_s):
        xv, xc, r, i, sp, a, s = _lru_gates(xb_ref, cw_ref, cb_ref, wr_ref, br_ref, wi_ref, bi_ref, lam_ref)
        h = hs_ref[...]
        yb = yb_ref[...]
        dout = do_ref[...].astype(F32)
        dyb_ref[...] = (dout * h * _dgelu(yb)).astype(dyb_ref.dtype)
        a_s[...] = _shift_up(a, 1)
        b_s[...] = dout * _gelu(yb)
        _linear_scan(a_s, b_s, g_s, True)
        g = g_s[...]
        da = g * _shift_down(h, 1)
        ds = g * (i * xc)
        di = g * s * xc
        dxc = g * s * i
        dla = da * a - ds * (a * a) / s
        dr = dla * (-LRU_C * sp)
        dlam_ref[...] = csum(dla * r) * (LRU_C * _sigmoid(-lam_ref[...]))
        dpr = dr * r * (1.0 - r)
        dpi = di * i * (1.0 - i)
        dxc = dxc + _bdot(dpr, wr_ref[0], NT) + _bdot(dpi, wi_ref[0], NT)
        dwr_ref[0] = _bdot(xc, dpr, TN)
        dwi_ref[0] = _bdot(xc, dpi, TN)
        dbr_ref[0] = csum(dpr)
        dbi_ref[0] = csum(dpi)
        dcb_ref[...] = csum(dxc)
        dxb_ref[...] = _conv_bwd(xv, cw_ref, dxc, dcw_ref).astype(dxb_ref.dtype)

    xb, yb, cw, vec, wg, bg = _lru_specs(T)
    act = jax.ShapeDtypeStruct((T, B * W), BF16)
    return pl.pallas_call(
        body, name=name, grid=(B,), in_specs=[xb, yb, cw, vec, wg, bg, wg, bg, vec, xb, xb],
        out_specs=(xb, xb, cw, vec, wg, bg, wg, bg, vec),
        out_shape=(act, act, jax.ShapeDtypeStruct((CONV_WIDTH, B * W), F32), jax.ShapeDtypeStruct((1, B * W), F32),
                   jax.ShapeDtypeStruct((B, W, W), F32), jax.ShapeDtypeStruct((B, 1, W), F32),
                   jax.ShapeDtypeStruct((B, W, W), F32), jax.ShapeDtypeStruct((B, 1, W), F32),
                   jax.ShapeDtypeStruct((1, B * W), F32)),
        scratch_shapes=[pltpu.VMEM((T, W), F32)] * 3,
        compiler_params=_params(("arbitrary",)),
    )(proj, proj, conv_w, conv_b, w_r, b_r, w_i, b_i, lam, hs, dout)


MESH = pl.DeviceIdType.MESH
N_CHIPS = 4
AG_COPIES = 7


def _mesh_pos():
    return lax.axis_index("x"), lax.axis_index("y"), lax.axis_index("c")


def _hbm_specs(n):
    return [pl.BlockSpec(memory_space=pltpu.HBM)] * n


def _all_gather(shards, *, name):
    n = len(shards)

    def body(*refs):
        xs, outs = refs[:n], refs[n:2 * n]
        send_sems, recv_sems, local_sems = refs[2 * n:]
        x, y, c = _mesh_pos()
        me, sibling = (x, y, c), (x, y, 1 - c)
        chips = [(1 - x, y), (x, 1 - y), (1 - x, 1 - y)]

        def rows(t, px, py, pc):
            return outs[t].at[4 * px + 2 * py + pc]

        def copy(t, k, block, to, src=None):
            return pltpu.make_async_remote_copy(
                src_ref=rows(t, *block) if src is None else src, dst_ref=rows(t, *block),
                send_sem=send_sems.at[t * AG_COPIES + k], recv_sem=recv_sems.at[t * AG_COPIES + k],
                device_id=to, device_id_type=MESH)

        mine = [pltpu.make_async_copy(xs[t], rows(t, *me), local_sems.at[t]) for t in range(n)]
        for cp in mine:
            cp.start()
        first = []
        for t in range(n):
            first.append(copy(t, 0, me, sibling, src=xs[t]))
            first += [copy(t, 1 + j, me, (*chip, c), src=xs[t]) for j, chip in enumerate(chips)]
        for cp in first:
            cp.start()
        passed = []
        for j, chip in enumerate(chips):
            for t in range(n):
                copy(t, 1 + j, (*chip, c), me).wait_recv()
                cp = copy(t, 4 + j, (*chip, c), sibling)
                cp.start()
                passed.append(cp)
        for t in range(n):
            copy(t, 0, sibling, me).wait_recv()
        for j, chip in enumerate(chips):
            for t in range(n):
                copy(t, 4 + j, (*chip, 1 - c), me).wait_recv()
        for cp in first + passed:
            cp.wait_send()
        for cp in mine:
            cp.wait()

    return pl.pallas_call(
        body, name=name,
        out_shape=[jax.ShapeDtypeStruct((N_DEV,) + s.shape, s.dtype) for s in shards],
        in_specs=_hbm_specs(n), out_specs=_hbm_specs(n),
        scratch_shapes=[pltpu.SemaphoreType.DMA((n * AG_COPIES,)), pltpu.SemaphoreType.DMA((n * AG_COPIES,)),
                        pltpu.SemaphoreType.DMA((n,))],
    )(*shards)


def _exchange_sibling(parts, *, name):
    n = len(parts)

    def body(*refs):
        srcs, outs = refs[:n], refs[n:2 * n]
        send_sems, recv_sems = refs[2 * n:]
        x, y, c = _mesh_pos()
        copies = []
        for t in range(n):
            for q in range(N_CHIPS):
                copies.append(pltpu.make_async_remote_copy(
                    src_ref=srcs[t].at[2 * q + (1 - c)], dst_ref=outs[t].at[q],
                    send_sem=send_sems.at[t * N_CHIPS + q], recv_sem=recv_sems.at[t * N_CHIPS + q],
                    device_id=(x, y, 1 - c), device_id_type=MESH))
        for cp in copies:
            cp.start()
        for cp in copies:
            cp.wait()

    return pl.pallas_call(
        body, name=name,
        out_shape=[jax.ShapeDtypeStruct((N_CHIPS,) + p.shape[1:], p.dtype) for p in parts],
        in_specs=_hbm_specs(n), out_specs=_hbm_specs(n),
        scratch_shapes=[pltpu.SemaphoreType.DMA((n * N_CHIPS,)), pltpu.SemaphoreType.DMA((n * N_CHIPS,))],
    )(*parts)


def _exchange_chips(sums, *, name):
    n = len(sums)
    peers = N_CHIPS - 1

    def body(*refs):
        srcs, outs = refs[:n], refs[n:2 * n]
        send_sems, recv_sems = refs[2 * n:]
        x, y, c = _mesh_pos()
        chips = [(1 - x, y), (x, 1 - y), (1 - x, 1 - y)]
        copies = []
        for t in range(n):
            for j, (px, py) in enumerate(chips):
                copies.append(pltpu.make_async_remote_copy(
                    src_ref=srcs[t].at[2 * px + py], dst_ref=outs[t].at[j],
                    send_sem=send_sems.at[t * peers + j], recv_sem=recv_sems.at[t * peers + j],
                    device_id=(px, py, c), device_id_type=MESH))
        for cp in copies:
            cp.start()
        for cp in copies:
            cp.wait()

    return pl.pallas_call(
        body, name=name,
        out_shape=[jax.ShapeDtypeStruct((peers,) + p.shape[1:], p.dtype) for p in sums],
        in_specs=_hbm_specs(n), out_specs=_hbm_specs(n),
        scratch_shapes=[pltpu.SemaphoreType.DMA((n * peers,)), pltpu.SemaphoreType.DMA((n * peers,))],
    )(*sums)


COMM_ROWS = 128


def _pair_add(g, a, core, *, name):
    _, R, C = g.shape
    tr = _tile(R, (COMM_ROWS,))

    def body(c_ref, g_ref, a_ref, o_ref):
        o_ref[...] = (g_ref[...].astype(F32) + a_ref[...].astype(F32)).astype(o_ref.dtype)

    blk = pl.BlockSpec((1, tr, C), lambda q, i, c: (q, i, 0))
    return pl.pallas_call(
        body, name=name,
        grid_spec=pltpu.PrefetchScalarGridSpec(
            num_scalar_prefetch=1, grid=(N_CHIPS, R // tr),
            in_specs=[pl.BlockSpec((1, tr, C), lambda q, i, c: (2 * q + c[0], i, 0)), blk], out_specs=blk),
        out_shape=jax.ShapeDtypeStruct((N_CHIPS, R, C), BF16),
        compiler_params=_params(("parallel", "parallel")),
    )(core, g, a)


def _adamw_sharded(w, m, v, s4, b3, chip, *, name):
    R, C = w.shape
    tr = _tile(R, (COMM_ROWS,))

    def body(q_ref, w_ref, m_ref, v_ref, s_ref, b_ref, g_out, d_out, m_out, v_out):
        g = s_ref[0].astype(F32)
        for j in range(N_CHIPS - 1):
            g = g + b_ref[j].astype(F32)
        mn = ADAM_B1 * m_ref[...] + (1.0 - ADAM_B1) * g
        vn = ADAM_B2 * v_ref[...] + (1.0 - ADAM_B2) * (g * g)
        g_out[...] = g
        d_out[...] = -ADAM_LR * ((mn / ADAM_C1) / (jnp.sqrt(vn / ADAM_C2) + ADAM_EPS) + ADAM_WD * w_ref[...])
        m_out[...] = mn
        v_out[...] = vn

    blk = pl.BlockSpec((tr, C), lambda i, q: (i, 0))
    sh = jax.ShapeDtypeStruct((R, C), F32)
    return pl.pallas_call(
        body, name=name,
        grid_spec=pltpu.PrefetchScalarGridSpec(
            num_scalar_prefetch=1, grid=(R // tr,),
            in_specs=[blk, blk, blk, pl.BlockSpec((1, tr, C), lambda i, q: (q[0], i, 0)),
                      pl.BlockSpec((N_CHIPS - 1, tr, C), lambda i, q: (0, i, 0))],
            out_specs=(blk,) * 4),
        out_shape=(sh,) * 4, compiler_params=_params(("parallel",)),
    )(chip, w, m, v, s4, b3)


FWD_NAMES = ['x', 'norm1', 'norm2', 'final_norm', 'ret_w_in', 'ret_gn_gain', 'ret_w_out', 'gdn_w_in', 'gdn_conv_w',
             'gdn_a_log', 'gdn_dt_bias', 'gdn_norm_gain', 'gdn_w_out', 'gla_w_in', 'gla_w_gate_up', 'gla_gate_bias',
             'gla_norm_gain', 'gla_w_out', 'lru_w_in', 'lru_conv_w', 'lru_conv_b', 'lru_w_rgate', 'lru_b_rgate',
             'lru_w_igate', 'lru_b_igate', 'lru_lambda', 'lru_w_out', 'mlp_w_up', 'mlp_w_down']
WEIGHT_NAMES = FWD_NAMES[1:]
ARG_NAMES = FWD_NAMES + ['loss_target'] + ['m_' + n for n in WEIGHT_NAMES] + ['v_' + n for n in WEIGHT_NAMES]

MIXER_IN = ('ret_w_in', 'gdn_w_in', 'gla_w_in', 'lru_w_in')
MIXER_OUT = ('ret_w_out', 'gdn_w_out', 'gla_w_out', 'lru_w_out')
BIG_NAMES = MIXER_IN + MIXER_OUT + ('mlp_w_up', 'mlp_w_down')
SMALL = {'norm1': False, 'norm2': False, 'final_norm': False, 'ret_gn_gain': True, 'gdn_conv_w': True,
         'gdn_a_log': False, 'gdn_dt_bias': False, 'gdn_norm_gain': False, 'gla_w_gate_up': True,
         'gla_gate_bias': True, 'gla_norm_gain': True, 'lru_conv_w': True, 'lru_conv_b': True,
         'lru_w_rgate': False, 'lru_b_rgate': False, 'lru_w_igate': False, 'lru_b_igate': False, 'lru_lambda': True}
SMALL_NAMES = tuple(n for n in WEIGHT_NAMES if n in SMALL)
GDN_TAIL = 2 * GDN_HEADS


def _pack(arrs):
    rows = []
    for a in arrs:
        f = a.reshape(-1).astype(F32)
        rows.append(jnp.pad(f, (0, (-f.shape[0]) % LANE)).reshape(-1, LANE))
    buf = jnp.concatenate(rows, axis=0)
    return jnp.pad(buf, ((0, (-buf.shape[0]) % SUBLANES), (0, 0)))


def _unpack(buf, shapes, lead=()):
    out, r0 = [], 0
    for s in shapes:
        n = int(np.prod(s))
        nr = -(-n // LANE)
        blk = buf[..., r0:r0 + nr, :].reshape(lead + (nr * LANE,))[..., :n]
        out.append(blk.reshape(lead + tuple(s)))
        r0 += nr
    return out


def _full_cols(g):
    return jnp.transpose(g, (1, 0, 2)).reshape(g.shape[1], N_DEV * g.shape[2])


def _full_rows(g):
    return g.reshape(N_DEV * g.shape[1], g.shape[2])


def _blocks_cols(dw):
    r, c = dw.shape[0], dw.shape[1] // N_DEV
    return jnp.transpose(dw.reshape(r, N_DEV, c), (1, 0, 2))


def _blocks_rows(dw):
    return dw.reshape(N_DEV, dw.shape[0] // N_DEV, dw.shape[1])


def _pad_cols(a, n=LANE):
    return jnp.pad(a, ((0, 0), (0, n - a.shape[1])))


def _mixer_fwd(layer, hn, w_in, sm, tables):
    tag = f"l{layer}"
    if layer == 0:
        proj = _matmul(hn, w_in, name=tag + "_in")
        gain = sm['ret_gn_gain'][0][:, None, :]
        og, st = _ret_fwd(proj, gain, tables, name=tag + "_ret_fwd")
        return og, dict(proj=proj, st=st, gain=gain)
    if layer == 1:
        w_main, w_tail = w_in[:, :4 * D_MODEL], _pad_cols(w_in[:, 4 * D_MODEL:])
        proj = _matmul(hn, w_main, name=tag + "_in")
        gates = _matmul(hn, w_tail, name=tag + "_in_tail")
        conv_w = sm['gdn_conv_w'][0]
        act = _conv_silu_fwd(proj, conv_w, name=tag + "_conv")
        a_log = sm['gdn_a_log'].reshape(GDN_HEADS, 1, 1)
        dt_bias = sm['gdn_dt_bias'].reshape(GDN_HEADS, 1, 1)
        gain = sm['gdn_norm_gain']
        og, st = _gdn_fwd(act, proj, gates, a_log, dt_bias, gain, name=tag + "_gdn_fwd")
        return og, dict(proj=proj, gates=gates, act=act, st=st, conv_w=conv_w, a_log=a_log, dt_bias=dt_bias,
                        gain=gain, w_main=w_main, w_tail=w_tail)
    if layer == 2:
        w_main, w_tail = w_in[:, :3 * D_MODEL], _pad_cols(w_in[:, 3 * D_MODEL:])
        proj = _matmul(hn, w_main, name=tag + "_in")
        glow = _matmul(hn, w_tail, name=tag + "_in_tail")
        wgu = jnp.pad(sm['gla_w_gate_up'][0], ((0, LANE - GLA_GATE_RANK), (0, 0)))
        glogit = _matmul(glow, wgu, name=tag + "_gate_up")
        bias = sm['gla_gate_bias']
        gain = sm['gla_norm_gain'][0][:, None, :]
        og, st = _gla_fwd(proj, glogit, bias, gain, name=tag + "_gla_fwd")
        return og, dict(proj=proj, glow=glow, glogit=glogit, wgu=wgu, bias=bias, gain=gain, st=st,
                        w_main=w_main, w_tail=w_tail)
    proj = _matmul(hn, w_in, name=tag + "_in")
    args = (proj, sm['lru_conv_w'][0], sm['lru_conv_b'], sm['lru_w_rgate'][0], sm['lru_b_rgate'][0][:, None, :],
            sm['lru_w_igate'][0], sm['lru_b_igate'][0][:, None, :], sm['lru_lambda'])
    og, hs = _lru_fwd(*args, name=tag + "_lru_fwd")
    return og, dict(args=args, hs=hs)


def _mixer_bwd(layer, hn, w_in, dog, sv, tables):
    tag = f"l{layer}"
    if layer == 0:
        dq, dk, dv, dg, dgain = _ret_bwd(sv['proj'], sv['gain'], tables, sv['st'], dog, name=tag + "_ret_bwd")
        dproj = jnp.concatenate([dq, dk, dv, dg], axis=1)
        dhn = _matmul(dproj, w_in, tb=True, name=tag + "_in_dx")
        dw = _matmul(hn, dproj, ta=True, out_dtype=BF16, name=tag + "_in_dw")
        return dhn, dw, {'ret_gn_gain': dgain[:, 0][None]}
    if layer == 1:
        dq, dk, dv, dz, dgates, dscal, dgain = _gdn_bwd(
            sv['act'], sv['proj'], sv['gates'], sv['a_log'], sv['dt_bias'], sv['gain'], sv['st'], dog,
            name=tag + "_gdn_bwd")
        dact = jnp.concatenate([dq, dk, dv], axis=1)
        dqkv, dconv = _conv_silu_bwd(sv['proj'], sv['conv_w'], dact, name=tag + "_conv_bwd")
        dmain = jnp.concatenate([dqkv, dz], axis=1)
        T = dmain.shape[0]
        dtail = _pad_cols(jnp.transpose(dgates[:, :, :2], (1, 2, 0)).reshape(T, GDN_TAIL))
        dhn = _matmul(dmain, sv['w_main'], tb=True, name=tag + "_in_dx")
        dhn = _matmul(dtail, sv['w_tail'], tb=True, epi="add", extra=dhn, name=tag + "_in_tail_dx")
        dw_main = _matmul(hn, dmain, ta=True, out_dtype=BF16, name=tag + "_in_dw")
        dw_tail = _matmul(hn, dtail, ta=True, out_dtype=BF16, name=tag + "_in_tail_dw")
        dw = jnp.concatenate([dw_main, dw_tail[:, :GDN_TAIL]], axis=1)
        small = {'gdn_conv_w': dconv[None], 'gdn_a_log': dscal[:, 0, 0][None], 'gdn_dt_bias': dscal[:, 0, 1][None],
                 'gdn_norm_gain': jnp.sum(dgain[:, 0], axis=0)[None]}
        return dhn, dw, small
    if layer == 2:
        dq, dk, dv, dr, dgl, dgain = _gla_bwd(sv['proj'], sv['glogit'], sv['bias'], sv['gain'], sv['st'], dog,
                                              name=tag + "_gla_bwd")
        dmain = jnp.concatenate([dq, dk, dv, dr], axis=1)
        dglow = _matmul(dgl, sv['wgu'], tb=True, name=tag + "_gate_up_dx")
        dwgu = _matmul(sv['glow'], dgl, ta=True, name=tag + "_gate_up_dw")
        dbias = _colsum(dgl, name=tag + "_gate_bias")
        dhn = _matmul(dmain, sv['w_main'], tb=True, name=tag + "_in_dx")
        dhn = _matmul(dglow, sv['w_tail'], tb=True, epi="add", extra=dhn, name=tag + "_in_tail_dx")
        dw_main = _matmul(hn, dmain, ta=True, out_dtype=BF16, name=tag + "_in_dw")
        dw_tail = _matmul(hn, dglow, ta=True, out_dtype=BF16, name=tag + "_in_tail_dw")
        dw = jnp.concatenate([dw_main, dw_tail[:, :GLA_GATE_RANK]], axis=1)
        small = {'gla_w_gate_up': dwgu[:GLA_GATE_RANK][None], 'gla_gate_bias': dbias,
                 'gla_norm_gain': dgain[:, 0][None]}
        return dhn, dw, small
    dxb, dyb, dcw, dcb, dwr, dbr, dwi, dbi, dlam = _lru_bwd(*sv['args'], sv['hs'], dog, name=tag + "_lru_bwd")
    dproj = jnp.concatenate([dxb, dyb], axis=1)
    dhn = _matmul(dproj, w_in, tb=True, name=tag + "_in_dx")
    dw = _matmul(hn, dproj, ta=True, out_dtype=BF16, name=tag + "_in_dw")
    small = {'lru_conv_w': dcw[None], 'lru_conv_b': dcb, 'lru_w_rgate': dwr[None], 'lru_b_rgate': dbr[:, 0][None],
             'lru_w_igate': dwi[None], 'lru_b_igate': dbi[:, 0][None], 'lru_lambda': dlam}
    return dhn, dw, small


def _step(*args):
    assert len(args) == len(ARG_NAMES)
    p = dict(zip(ARG_NAMES, args))
    xi, yi, ci = _mesh_pos()
    dev = 4 * xi + 2 * yi + ci
    core = ci.astype(jnp.int32).reshape(1)
    chip = (2 * xi + yi).astype(jnp.int32).reshape(1)
    x = p['x'][0]
    target = p['loss_target'][0]
    T = x.shape[0]
    tables = _ret_tables(T)

    big = {}
    for layer in range(DEPTH):
        shards = [p[MIXER_IN[layer]][0], p[MIXER_OUT[layer]][0], p['mlp_w_up'][layer], p['mlp_w_down'][layer]]
        g_in, g_out, g_up, g_down = _all_gather([s.astype(BF16) for s in shards], name=f"gather_l{layer}")
        big[layer] = (_full_cols(g_in), _full_rows(g_out), _full_cols(g_up), _full_rows(g_down))
    sharded_small = [n for n in SMALL_NAMES if SMALL[n]]
    gathered, = _all_gather([_pack([p[n] for n in sharded_small])], name="gather_small")
    parts = _unpack(gathered, [p[n].shape for n in sharded_small], lead=(N_DEV,))
    sm = {n: p[n] for n in SMALL_NAMES if not SMALL[n]}
    for n, blk in zip(sharded_small, parts):
        full = jnp.moveaxis(blk, 0, -2)
        sm[n] = full.reshape(full.shape[:-2] + (N_DEV * full.shape[-1],))

    saved = []
    for layer in range(DEPTH):
        w_in, w_out, w_up, w_down = big[layer]
        tag = f"l{layer}"
        hn = _rmsnorm_fwd(x, sm['norm1'][layer][None], name=tag + "_norm1")
        og, sv = _mixer_fwd(layer, hn, w_in, sm, tables)
        x_mid = _matmul(og, w_out, epi="add", extra=x, name=tag + "_out")
        hn2 = _rmsnorm_fwd(x_mid, sm['norm2'][layer][None], name=tag + "_norm2")
        u, a = _matmul(hn2, w_up, epi="relu2", name=tag + "_up")
        x_new = _matmul(a, w_down, epi="add", extra=x_mid, name=tag + "_down")
        saved.append(dict(x=x, hn=hn, og=og, sv=sv, x_mid=x_mid, hn2=hn2, u=u, a=a))
        x = x_new
    dy, loss_part = _final_loss(x, sm['final_norm'][None], target, name="final_loss")
    loss = lax.psum(loss_part[0, 0], ("x", "y", "c"))
    dx, dfinal = _rmsnorm_bwd(x, sm['final_norm'][None], dy, jnp.zeros_like(x), name="final_norm_bwd")

    outs = {}
    small_grads = {'final_norm': dfinal[0]}
    dnorm1, dnorm2 = [None] * DEPTH, [None] * DEPTH
    mlp_res = {'mlp_w_up': [None] * DEPTH, 'mlp_w_down': [None] * DEPTH}
    for layer in reversed(range(DEPTH)):
        w_in, w_out, w_up, w_down = big[layer]
        s = saved[layer]
        tag = f"l{layer}"
        du = _matmul(dx, w_down, tb=True, epi="drelu2", extra=s['u'], out_dtype=BF16, name=tag + "_down_dx")
        dw_down = _matmul(s['a'], dx, ta=True, out_dtype=BF16, name=tag + "_down_dw")
        dhn2 = _matmul(du, w_up, tb=True, name=tag + "_up_dx")
        dw_up = _matmul(s['hn2'], du, ta=True, out_dtype=BF16, name=tag + "_up_dw")
        dx, dn2 = _rmsnorm_bwd(s['x_mid'], sm['norm2'][layer][None], dhn2, dx, name=tag + "_norm2_bwd")
        dog = _matmul(dx, w_out, tb=True, name=tag + "_out_dx")
        dw_out = _matmul(s['og'], dx, ta=True, out_dtype=BF16, name=tag + "_out_dw")
        dhn, dw_in, sg = _mixer_bwd(layer, s['hn'], w_in, dog, s['sv'], tables)
        dx, dn1 = _rmsnorm_bwd(s['x'], sm['norm1'][layer][None], dhn, dx, name=tag + "_norm1_bwd")
        dnorm1[layer], dnorm2[layer] = dn1[0], dn2[0]
        small_grads.update(sg)

        blocks = [_blocks_cols(dw_in), _blocks_rows(dw_out), _blocks_cols(dw_up), _blocks_rows(dw_down)]
        from_sibling = _exchange_sibling(blocks, name=f"scatter_pair_l{layer}")
        sums = [_pair_add(g, a_, core, name=f"{tag}_pair_add{i}") for i, (g, a_) in enumerate(zip(blocks, from_sibling))]
        from_chips = _exchange_chips(sums, name=f"scatter_chips_l{layer}")
        for i, n in enumerate((MIXER_IN[layer], MIXER_OUT[layer], 'mlp_w_up', 'mlp_w_down')):
            idx = layer if n.startswith('mlp') else 0
            res = _adamw_sharded(p[n][idx], p['m_' + n][idx], p['v_' + n][idx], sums[i], from_chips[i], chip,
                                 name=f"{tag}_adamw{i}")
            if n.startswith('mlp'):
                mlp_res[n][layer] = res
            else:
                outs[n] = tuple(r[None] for r in res)
    for n in ('mlp_w_up', 'mlp_w_down'):
        outs[n] = tuple(jnp.stack([mlp_res[n][layer][k] for layer in range(DEPTH)]) for k in range(4))
    small_grads['norm1'] = jnp.stack(dnorm1)
    small_grads['norm2'] = jnp.stack(dnorm2)

    full_shapes = [small_grads[n].shape for n in SMALL_NAMES]
    all_parts, = _all_gather([_pack([small_grads[n] for n in SMALL_NAMES])], name="gather_small_grads")
    summed = _unpack(_sum_parts(all_parts, name="sum_small_grads"), full_shapes)
    local_g = []
    for n, g in zip(SMALL_NAMES, summed):
        if SMALL[n]:
            width = p[n].shape[-1]
            g = lax.dynamic_slice_in_dim(g, dev * width, width, axis=g.ndim - 1)
        local_g.append(g.reshape(p[n].shape))
    res = _adamw(_pack([p[n] for n in SMALL_NAMES]), _pack([p['m_' + n] for n in SMALL_NAMES]),
                 _pack([p['v_' + n] for n in SMALL_NAMES]), [_pack(local_g)], name="adamw_small")
    local_shapes = [p[n].shape for n in SMALL_NAMES]
    unpacked = [_unpack(r, local_shapes) for r in res]
    for i, n in enumerate(SMALL_NAMES):
        outs[n] = tuple(unpacked[k][i] for k in range(4))

    result = [loss, dx[None]]
    for k in range(4):
        result += [outs[n][k] for n in WEIGHT_NAMES]
    return tuple(result)


def kernel(x, norm1, norm2, final_norm, ret_w_in, ret_gn_gain, ret_w_out, gdn_w_in, gdn_conv_w, gdn_a_log, gdn_dt_bias, gdn_norm_gain, gdn_w_out, gla_w_in, gla_w_gate_up, gla_gate_bias, gla_norm_gain, gla_w_out, lru_w_in, lru_conv_w, lru_conv_b, lru_w_rgate, lru_b_rgate, lru_w_igate, lru_b_igate, lru_lambda, lru_w_out, mlp_w_up, mlp_w_down, loss_target, m_norm1, m_norm2, m_final_norm, m_ret_w_in, m_ret_gn_gain, m_ret_w_out, m_gdn_w_in, m_gdn_conv_w, m_gdn_a_log, m_gdn_dt_bias, m_gdn_norm_gain, m_gdn_w_out, m_gla_w_in, m_gla_w_gate_up, m_gla_gate_bias, m_gla_norm_gain, m_gla_w_out, m_lru_w_in, m_lru_conv_w, m_lru_conv_b, m_lru_w_rgate, m_lru_b_rgate, m_lru_w_igate, m_lru_b_igate, m_lru_lambda, m_lru_w_out, m_mlp_w_up, m_mlp_w_down, v_norm1, v_norm2, v_final_norm, v_ret_w_in, v_ret_gn_gain, v_ret_w_out, v_gdn_w_in, v_gdn_conv_w, v_gdn_a_log, v_gdn_dt_bias, v_gdn_norm_gain, v_gdn_w_out, v_gla_w_in, v_gla_w_gate_up, v_gla_gate_bias, v_gla_norm_gain, v_gla_w_out, v_lru_w_in, v_lru_conv_w, v_lru_conv_b, v_lru_w_rgate, v_lru_b_rgate, v_lru_w_igate, v_lru_b_igate, v_lru_lambda, v_lru_w_out, v_mlp_w_up, v_mlp_w_down):
    return _step(x, norm1, norm2, final_norm, ret_w_in, ret_gn_gain, ret_w_out, gdn_w_in, gdn_conv_w, gdn_a_log, gdn_dt_bias, gdn_norm_gain, gdn_w_out, gla_w_in, gla_w_gate_up, gla_gate_bias, gla_norm_gain, gla_w_out, lru_w_in, lru_conv_w, lru_conv_b, lru_w_rgate, lru_b_rgate, lru_w_igate, lru_b_igate, lru_lambda, lru_w_out, mlp_w_up, mlp_w_down, loss_target, m_norm1, m_norm2, m_final_norm, m_ret_w_in, m_ret_gn_gain, m_ret_w_out, m_gdn_w_in, m_gdn_conv_w, m_gdn_a_log, m_gdn_dt_bias, m_gdn_norm_gain, m_gdn_w_out, m_gla_w_in, m_gla_w_gate_up, m_gla_gate_bias, m_gla_norm_gain, m_gla_w_out, m_lru_w_in, m_lru_conv_w, m_lru_conv_b, m_lru_w_rgate, m_lru_b_rgate, m_lru_w_igate, m_lru_b_igate, m_lru_lambda, m_lru_w_out, m_mlp_w_up, m_mlp_w_down, v_norm1, v_norm2, v_final_norm, v_ret_w_in, v_ret_gn_gain, v_ret_w_out, v_gdn_w_in, v_gdn_conv_w, v_gdn_a_log, v_gdn_dt_bias, v_gdn_norm_gain, v_gdn_w_out, v_gla_w_in, v_gla_w_gate_up, v_gla_gate_bias, v_gla_norm_gain, v_gla_w_out, v_lru_w_in, v_lru_conv_w, v_lru_conv_b, v_lru_w_rgate, v_lru_b_rgate, v_lru_w_igate, v_lru_b_igate, v_lru_lambda, v_lru_w_out, v_mlp_w_up, v_mlp_w_down)
```

```python
import functools
import math

import numpy as np
import jax
import jax.numpy as jnp
from jax import lax
from jax.experimental import pallas as pl
from jax.experimental.pallas import tpu as pltpu

F32 = jnp.float32
BF16 = jnp.bfloat16

D_MODEL = 2048
DEPTH = 4
CHUNK = 64
D_FF = 4 * D_MODEL
NORM_EPS = 1e-6
N_DEV = 8

RET_HEADS, RET_DK, RET_DV = 8, 256, 512
GDN_HEADS, GDN_DK, GDN_DV = 16, 128, 128
GDN_QKV = GDN_HEADS * (2 * GDN_DK + GDN_DV)
CONV_WIDTH = 4
GLA_HEADS, GLA_DK, GLA_DV = 4, 256, 512
GLA_GATE_RANK = 16
GLA_TAU = 16.0
LRU_WIDTH, LRU_BLOCKS, LRU_BLOCK = 2048, 16, 128
LRU_C = 8.0
ROPE_BASE = 10000.0

ADAM_LR, ADAM_B1, ADAM_B2, ADAM_EPS, ADAM_WD, ADAM_STEP = 0.001, 0.9, 0.999, 1e-08, 0.01, 10

LANE = 128
VMEM_LIMIT = 48 * 1024 * 1024

NN = (((1,), (0,)), ((), ()))
NT = (((1,), (1,)), ((), ()))
TN = (((0,), (0,)), ((), ()))


def _params(sem=None):
    return pltpu.CompilerParams(dimension_semantics=sem, vmem_limit_bytes=VMEM_LIMIT)


def _bdot(a, b, dn=NN):
    return lax.dot_general(a.astype(BF16), b.astype(BF16), dn, preferred_element_type=F32)


def _split(x):
    hi = x.astype(BF16)
    lo = (x - hi.astype(F32)).astype(BF16)
    return hi, lo


def _fdot(a, b, dn=NN):
    a1, a2 = _split(a)
    b1, b2 = _split(b)
    d = functools.partial(lax.dot_general, dimension_numbers=dn, preferred_element_type=F32)
    return d(a1, b1) + (d(a1, b2) + d(a2, b1))


def _sigmoid(x):
    return 1.0 / (1.0 + jnp.exp(-x))


def _softplus(x):
    return jnp.maximum(x, 0.0) + jnp.log(1.0 + jnp.exp(-jnp.abs(x)))


def _silu(x):
    return x * _sigmoid(x)


def _dsilu(x):
    s = _sigmoid(x)
    return s * (1.0 + x * (1.0 - s))


GELU_C = math.sqrt(2.0 / math.pi)


def _gelu(x):
    return 0.5 * x * (1.0 + jnp.tanh(GELU_C * (x + 0.044715 * x * x * x)))


def _dgelu(x):
    t = jnp.tanh(GELU_C * (x + 0.044715 * x * x * x))
    return 0.5 * (1.0 + t) + 0.5 * x * (1.0 - t * t) * GELU_C * (1.0 + 3.0 * 0.044715 * x * x)


def _expm1(x):
    poly = x * (1.0 + x * 0.5 * (1.0 + x * (1.0 / 3.0) * (1.0 + x * 0.25 * (1.0 + x * 0.2))))
    return jnp.where(jnp.abs(x) < 0.05, poly, jnp.exp(x) - 1.0)


def _iota2(shape, axis):
    return lax.broadcasted_iota(jnp.int32, shape, axis)


def _col_to_row(col):
    n = col.shape[0]
    eye = _iota2((n, n), 0) == _iota2((n, n), 1)
    return jnp.sum(jnp.where(eye, col, 0.0), axis=0, keepdims=True)


def _row_to_col(row):
    n = row.shape[1]
    eye = _iota2((n, n), 0) == _iota2((n, n), 1)
    return jnp.sum(jnp.where(eye, row, 0.0), axis=1, keepdims=True)


def _pick_row(x, r):
    rows = _iota2(x.shape, 0)
    return jnp.sum(jnp.where(rows == r, x, 0.0), axis=0, keepdims=True)


def _shift_down(x, s):
    if s == 0:
        return x
    y = pltpu.roll(x, s, 0)
    return jnp.where(_iota2(x.shape, 0) < s, 0.0, y)


def _shift_up(x, s):
    if s == 0:
        return x
    n = x.shape[0]
    y = pltpu.roll(x, n - s, 0)
    return jnp.where(_iota2(x.shape, 0) >= n - s, 0.0, y)


def _tile(dim, prefs):
    for p in prefs:
        if dim % p == 0:
            return p
    return dim


TOKEN_SHAPE = (8, LANE)


def _dep_specs(deps):
    return [pl.BlockSpec(TOKEN_SHAPE, lambda *_: (0, 0)) for _ in deps]


def _matmul(a, b, *, ta=False, tb=False, epi="none", extra=None, out_dtype=F32, name, deps=()):
    if ta:
        K, M = a.shape
    else:
        M, K = a.shape
    if tb:
        N, K2 = b.shape
    else:
        K2, N = b.shape
    assert K == K2, (a.shape, b.shape, ta, tb)
    tm = _tile(M, (1024, 512, 256, 128))
    tn = _tile(N, (1024, 512, 256, 128))
    tk = _tile(K, (512, 256, 128))
    nk = K // tk
    dn = (((0 if ta else 1,), (1 if tb else 0,)), ((), ()))
    n_extra = 0 if extra is None else 1
    n_out = 2 if epi == "relu2" else 1

    def body(*refs):
        a_ref, b_ref = refs[0], refs[1]
        e_ref = refs[2] if n_extra else None
        outs = refs[2 + n_extra + len(deps):2 + n_extra + len(deps) + n_out]
        acc = refs[-1]
        k = pl.program_id(2)

        @pl.when(k == 0)
        def _():
            acc[...] = jnp.zeros_like(acc)

        acc[...] += lax.dot_general(a_ref[...].astype(BF16), b_ref[...].astype(BF16), dn,
                                    preferred_element_type=F32)

        @pl.when(k == nk - 1)
        def _():
            r = acc[...]
            if epi == "none":
                outs[0][...] = r.astype(outs[0].dtype)
            elif epi == "add":
                outs[0][...] = (r + e_ref[...]).astype(outs[0].dtype)
            elif epi == "relu2":
                outs[0][...] = r
                p = jnp.maximum(r, 0.0)
                outs[1][...] = (p * p).astype(outs[1].dtype)
            elif epi == "drelu2":
                outs[0][...] = (r * 2.0 * jnp.maximum(e_ref[...], 0.0)).astype(outs[0].dtype)

    a_spec = pl.BlockSpec((tk, tm), lambda i, j, k: (k, i)) if ta else pl.BlockSpec((tm, tk), lambda i, j, k: (i, k))
    b_spec = pl.BlockSpec((tn, tk), lambda i, j, k: (j, k)) if tb else pl.BlockSpec((tk, tn), lambda i, j, k: (k, j))
    o_spec = pl.BlockSpec((tm, tn), lambda i, j, k: (i, j))
    in_specs = [a_spec, b_spec] + ([o_spec] if n_extra else []) + _dep_specs(deps)
    if epi == "relu2":
        out_shape = (jax.ShapeDtypeStruct((M, N), F32), jax.ShapeDtypeStruct((M, N), BF16))
        out_specs = (o_spec, o_spec)
    else:
        out_shape = jax.ShapeDtypeStruct((M, N), out_dtype)
        out_specs = o_spec
    args = (a, b) + ((extra,) if n_extra else ()) + tuple(deps)
    return pl.pallas_call(
        body, name=name, grid=(M // tm, N // tn, nk), in_specs=in_specs, out_specs=out_specs,
        out_shape=out_shape, scratch_shapes=[pltpu.VMEM((tm, tn), F32)],
        compiler_params=_params(("parallel", "parallel", "arbitrary")),
    )(*args)


ROW_BLOCK = 256


def _rmsnorm_fwd(x, g, *, name, deps=()):
    T, D = x.shape
    tr = _tile(T, (ROW_BLOCK, 128, 64))

    def body(x_ref, g_ref, *rest):
        o_ref = rest[-1]
        xv = x_ref[...]
        r = lax.rsqrt(jnp.mean(xv * xv, axis=-1, keepdims=True) + NORM_EPS)
        o_ref[...] = (xv * r * g_ref[...]).astype(o_ref.dtype)

    return pl.pallas_call(
        body, name=name, grid=(T // tr,),
        in_specs=[pl.BlockSpec((tr, D), lambda i: (i, 0)), pl.BlockSpec((1, D), lambda i: (0, 0))] + _dep_specs(deps),
        out_specs=pl.BlockSpec((tr, D), lambda i: (i, 0)),
        out_shape=jax.ShapeDtypeStruct((T, D), BF16), compiler_params=_params(("parallel",)),
    )(x, g, *deps)


def _rmsnorm_bwd(x, g, dy, dres, *, name):
    T, D = x.shape
    tr = _tile(T, (ROW_BLOCK, 128, 64))

    def body(x_ref, g_ref, dy_ref, dres_ref, dx_ref, dg_ref):
        i = pl.program_id(0)
        xv = x_ref[...]
        r = lax.rsqrt(jnp.mean(xv * xv, axis=-1, keepdims=True) + NORM_EPS)
        xh = xv * r
        dyv = dy_ref[...].astype(F32)
        dxh = dyv * g_ref[...]
        dx = r * (dxh - xh * jnp.mean(dxh * xh, axis=-1, keepdims=True))
        dx_ref[...] = dres_ref[...] + dx

        @pl.when(i == 0)
        def _():
            dg_ref[...] = jnp.zeros_like(dg_ref)

        dg_ref[...] += jnp.sum(dyv * xh, axis=0, keepdims=True)

    blk = pl.BlockSpec((tr, D), lambda i: (i, 0))
    vec = pl.BlockSpec((1, D), lambda i: (0, 0))
    return pl.pallas_call(
        body, name=name, grid=(T // tr,), in_specs=[blk, vec, blk, blk], out_specs=(blk, vec),
        out_shape=(jax.ShapeDtypeStruct((T, D), F32), jax.ShapeDtypeStruct((1, D), F32)),
        compiler_params=_params(("arbitrary",)),
    )(x, g, dy, dres)


def _final_loss(x, g, target, *, name):
    T, D = x.shape
    tr = _tile(T, (ROW_BLOCK, 128, 64))

    def body(x_ref, g_ref, t_ref, dy_ref, l_ref):
        i = pl.program_id(0)
        xv = x_ref[...]
        r = lax.rsqrt(jnp.mean(xv * xv, axis=-1, keepdims=True) + NORM_EPS)
        err = xv * r * g_ref[...] - t_ref[...]
        dy_ref[...] = err * (1.0 / D)

        @pl.when(i == 0)
        def _():
            l_ref[...] = jnp.zeros_like(l_ref)

        part = 0.5 * jnp.sum(jnp.mean(err * err, axis=-1, keepdims=True), axis=0, keepdims=True)
        l_ref[...] += jnp.broadcast_to(part, l_ref.shape)

    blk = pl.BlockSpec((tr, D), lambda i: (i, 0))
    vec = pl.BlockSpec((1, D), lambda i: (0, 0))
    return pl.pallas_call(
        body, name=name, grid=(T // tr,), in_specs=[blk, vec, blk],
        out_specs=(blk, pl.BlockSpec((1, LANE), lambda i: (0, 0))),
        out_shape=(jax.ShapeDtypeStruct((T, D), F32), jax.ShapeDtypeStruct((1, LANE), F32)),
        compiler_params=_params(("arbitrary",)),
    )(x, g, target)


def _colsum(x, *, name):
    T, C = x.shape
    tc = _tile(C, (512, 256, 128))

    def body(x_ref, o_ref):
        o_ref[...] = jnp.sum(x_ref[...], axis=0, keepdims=True)

    return pl.pallas_call(
        body, name=name, grid=(C // tc,), in_specs=[pl.BlockSpec((T, tc), lambda j: (0, j))],
        out_specs=pl.BlockSpec((1, tc), lambda j: (0, j)),
        out_shape=jax.ShapeDtypeStruct((1, C), F32), compiler_params=_params(("parallel",)),
    )(x)


ADAM_C1 = 1.0 - ADAM_B1 ** ADAM_STEP
ADAM_C2 = 1.0 - ADAM_B2 ** ADAM_STEP


def _adamw(w, m, v, grads, *, name):
    R, C = w.shape
    tr = _tile(R, (256, 128, 64, 32, 16, 8))
    n_g = len(grads)

    def body(*refs):
        w_ref, m_ref, v_ref = refs[:3]
        g_refs = refs[3:3 + n_g]
        g_out, d_out, m_out, v_out = refs[3 + n_g:]
        g = g_refs[0][...].astype(F32)
        for r in g_refs[1:]:
            g = g + r[...].astype(F32)
        mn = ADAM_B1 * m_ref[...] + (1.0 - ADAM_B1) * g
        vn = ADAM_B2 * v_ref[...] + (1.0 - ADAM_B2) * (g * g)
        m_hat = mn / ADAM_C1
        v_hat = vn / ADAM_C2
        g_out[...] = g
        d_out[...] = -ADAM_LR * (m_hat / (jnp.sqrt(v_hat) + ADAM_EPS) + ADAM_WD * w_ref[...])
        m_out[...] = mn
        v_out[...] = vn

    blk = pl.BlockSpec((tr, C), lambda i: (i, 0))
    sh = jax.ShapeDtypeStruct((R, C), F32)
    return pl.pallas_call(
        body, name=name, grid=(R // tr,), in_specs=[blk] * (3 + n_g), out_specs=(blk,) * 4,
        out_shape=(sh,) * 4, compiler_params=_params(("parallel",)),
    )(w, m, v, *grads)


def _sum_parts(parts, *, name):
    P, R, C = parts.shape
    tr = _tile(R, (256, 128, 64, 32, 16, 8))

    def body(p_ref, o_ref):
        s = p_ref[0].astype(F32)
        for i in range(1, P):
            s = s + p_ref[i].astype(F32)
        o_ref[...] = s

    return pl.pallas_call(
        body, name=name, grid=(R // tr,), in_specs=[pl.BlockSpec((P, tr, C), lambda i: (0, i, 0))],
        out_specs=pl.BlockSpec((tr, C), lambda i: (i, 0)),
        out_shape=jax.ShapeDtypeStruct((R, C), F32), compiler_params=_params(("parallel",)),
    )(parts)


def _ret_tables(T):
    H, C = RET_HEADS, CHUNK
    log_gamma = jnp.log1p(-jnp.exp2(-5.0 - jnp.arange(H, dtype=F32)))
    pos = jnp.arange(C, dtype=F32)
    dist = jnp.abs(pos[:, None] - pos[None, :])
    dm = jnp.exp(log_gamma[:, None, None] * dist)
    qdec = jnp.exp(log_gamma[:, None] * (pos + 1.0))[:, :, None]
    kdec = jnp.exp(log_gamma[:, None] * (C - 1.0 - pos))[:, :, None]
    cdec = jnp.exp(log_gamma * C)[:, None, None]
    inv = ROPE_BASE ** (-jnp.arange(0, RET_DK, 2, dtype=F32) / RET_DK)
    ang = jnp.arange(T, dtype=F32)[:, None] * inv[None, :]
    return dm, qdec, kdec, cdec, jnp.cos(ang), jnp.sin(ang)


def _rot(x, cos, sin):
    h = x.shape[1] // 2
    x1, x2 = x[:, :h], x[:, h:]
    return jnp.concatenate([x1 * cos - x2 * sin, x1 * sin + x2 * cos], axis=1)


def _unrot(dy, cos, sin):
    h = dy.shape[1] // 2
    d1, d2 = dy[:, :h], dy[:, h:]
    return jnp.concatenate([d1 * cos + d2 * sin, d2 * cos - d1 * sin], axis=1)


def _ret_specs(N, rev):
    H, C, DK, DV = RET_HEADS, CHUNK, RET_DK, RET_DV
    cn = (lambda n: N - 1 - n) if rev else (lambda n: n)
    q = pl.BlockSpec((C, DK), lambda h, n: (cn(n), h))
    k = pl.BlockSpec((C, DK), lambda h, n: (cn(n), H + h))
    v = pl.BlockSpec((C, DV), lambda h, n: (cn(n), H + h))
    g = pl.BlockSpec((C, DV), lambda h, n: (cn(n), 2 * H + h))
    cs = pl.BlockSpec((C, DK // 2), lambda h, n: (cn(n), 0))
    dm = pl.BlockSpec((1, C, C), lambda h, n: (h, 0, 0))
    dec = pl.BlockSpec((1, C, 1), lambda h, n: (h, 0, 0))
    cd = pl.BlockSpec((1, 1, 1), lambda h, n: (h, 0, 0))
    gain = pl.BlockSpec((1, 1, DV), lambda h, n: (h, 0, 0))
    st = pl.BlockSpec((1, 1, DK, DV), lambda h, n: (h, cn(n), 0, 0))
    ov = pl.BlockSpec((C, DV), lambda h, n: (cn(n), h))
    ok = pl.BlockSpec((C, DK), lambda h, n: (cn(n), h))
    return q, k, v, g, cs, dm, dec, cd, gain, st, ov, ok


def _ret_fwd(proj, gain, tables, *, name):
    T = proj.shape[0]
    H, C, DK, DV = RET_HEADS, CHUNK, RET_DK, RET_DV
    N = T // C
    dm_t, qdec_t, kdec_t, cdec_t, cos_t, sin_t = tables

    def body(q_ref, k_ref, v_ref, g_ref, cos_ref, sin_ref, dm_ref, qd_ref, kd_ref, cd_ref, gain_ref,
             og_ref, st_ref, s_acc):
        n = pl.program_id(1)

        @pl.when(n == 0)
        def _():
            s_acc[...] = jnp.zeros_like(s_acc)

        cos, sin = cos_ref[...], sin_ref[...]
        qr = _rot(q_ref[...], cos, sin)
        kr = _rot(k_ref[...], cos, sin) * (DK ** -0.5)
        v = v_ref[...]
        sp = s_acc[...]
        st_ref[0, 0] = sp.astype(st_ref.dtype)
        scores = _bdot(qr, kr, NT) * dm_ref[0]
        o = _bdot(scores, v) + _bdot(qr * qd_ref[0], sp)
        s_acc[...] = sp * cd_ref[0] + _bdot(kr * kd_ref[0], v, TN)
        oc = o - jnp.mean(o, axis=-1, keepdims=True)
        rstd = lax.rsqrt(jnp.mean(oc * oc, axis=-1, keepdims=True) + NORM_EPS)
        og_ref[...] = (oc * rstd * gain_ref[0] * _silu(g_ref[...])).astype(og_ref.dtype)

    q, k, v, g, cs, dm, dec, cd, gn, st, ov, _ = _ret_specs(N, False)
    return pl.pallas_call(
        body, name=name, grid=(H, N),
        in_specs=[q, k, v, g, cs, cs, dm, dec, dec, cd, gn], out_specs=(ov, st),
        out_shape=(jax.ShapeDtypeStruct((T, H * DV), BF16), jax.ShapeDtypeStruct((H, N, DK, DV), BF16)),
        scratch_shapes=[pltpu.VMEM((DK, DV), F32)],
        compiler_params=_params(("arbitrary", "arbitrary")),
    )(proj, proj, proj, proj, cos_t, sin_t, dm_t, qdec_t, kdec_t, cdec_t, gain)


def _ret_bwd(proj, gain, tables, states, dog, *, name):
    T = proj.shape[0]
    H, C, DK, DV = RET_HEADS, CHUNK, RET_DK, RET_DV
    N = T // C
    dm_t, qdec_t, kdec_t, cdec_t, cos_t, sin_t = tables

    def body(q_ref, k_ref, v_ref, g_ref, cos_ref, sin_ref, dm_ref, qd_ref, kd_ref, cd_ref, gain_ref,
             st_ref, dog_ref, dq_ref, dk_ref, dv_ref, dg_ref, dgain_ref, ds_acc):
        n = pl.program_id(1)

        @pl.when(n == 0)
        def _():
            ds_acc[...] = jnp.zeros_like(ds_acc)
            dgain_ref[...] = jnp.zeros_like(dgain_ref)

        cos, sin = cos_ref[...], sin_ref[...]
        scale = DK ** -0.5
        qr = _rot(q_ref[...], cos, sin)
        kr = _rot(k_ref[...], cos, sin) * scale
        v = v_ref[...]
        g = g_ref[...]
        sp = st_ref[0, 0]
        dm = dm_ref[0]
        qd, kd = qd_ref[0], kd_ref[0]
        gain_v = gain_ref[0]
        scores = _bdot(qr, kr, NT) * dm
        o = _bdot(scores, v) + _bdot(qr * qd, sp)
        oc = o - jnp.mean(o, axis=-1, keepdims=True)
        rstd = lax.rsqrt(jnp.mean(oc * oc, axis=-1, keepdims=True) + NORM_EPS)
        oh = oc * rstd
        dy = dog_ref[...].astype(F32)
        dg_ref[...] = (dy * oh * gain_v * _dsilu(g)).astype(dg_ref.dtype)
        dnorm = dy * _silu(g)
        dgain_ref[0] += jnp.sum(dnorm * oh, axis=0, keepdims=True)
        doh = dnorm * gain_v
        do = rstd * (doh - jnp.mean(doh, axis=-1, keepdims=True)
                     - oh * jnp.mean(doh * oh, axis=-1, keepdims=True))
        dsn = ds_acc[...]
        dp = _bdot(do, v, NT) * dm
        dqr = _bdot(dp, kr) + _bdot(do, sp, NT) * qd
        dkr = _bdot(dp, qr, TN) + _bdot(v, dsn, NT) * kd
        dv_ref[...] = (_bdot(scores, do, TN) + _bdot(kr * kd, dsn)).astype(dv_ref.dtype)
        ds_acc[...] = dsn * cd_ref[0] + _bdot(qr * qd, do, TN)
        dq_ref[...] = _unrot(dqr, cos, sin).astype(dq_ref.dtype)
        dk_ref[...] = _unrot(dkr * scale, cos, sin).astype(dk_ref.dtype)

    q, k, v, g, cs, dm, dec, cd, gn, st, ov, ok = _ret_specs(N, True)
    return pl.pallas_call(
        body, name=name, grid=(H, N),
        in_specs=[q, k, v, g, cs, cs, dm, dec, dec, cd, gn, st, ov],
        out_specs=(ok, ok, ov, ov, gn),
        out_shape=(jax.ShapeDtypeStruct((T, H * DK), BF16), jax.ShapeDtypeStruct((T, H * DK), BF16),
                   jax.ShapeDtypeStruct((T, H * DV), BF16), jax.ShapeDtypeStruct((T, H * DV), BF16),
                   jax.ShapeDtypeStruct((H, 1, DV), F32)),
        scratch_shapes=[pltpu.VMEM((DK, DV), F32)],
        compiler_params=_params(("arbitrary", "arbitrary")),
    )(proj, proj, proj, proj, cos_t, sin_t, dm_t, qdec_t, kdec_t, cdec_t, gain, states, dog)


def _gla_specs(N, rev):
    H, C, DK, DV = GLA_HEADS, CHUNK, GLA_DK, GLA_DV
    cn = (lambda n: N - 1 - n) if rev else (lambda n: n)
    q = pl.BlockSpec((C, DK), lambda h, n: (cn(n), h))
    k = pl.BlockSpec((C, DK), lambda h, n: (cn(n), H + h))
    v = pl.BlockSpec((C, DV), lambda h, n: (cn(n), H + h))
    r = pl.BlockSpec((C, DV), lambda h, n: (cn(n), 2 * H + h))
    bias = pl.BlockSpec((1, DK), lambda h, n: (0, h))
    gain = pl.BlockSpec((1, 1, DV), lambda h, n: (h, 0, 0))
    st = pl.BlockSpec((1, 1, DV, DK), lambda h, n: (h, cn(n), 0, 0))
    ov = pl.BlockSpec((C, DV), lambda h, n: (cn(n), h))
    return q, k, v, r, bias, gain, st, ov


def _gla_chunk(q_ref, k_ref, v_ref, gl_ref, bias_ref):
    C, DK = q_ref.shape
    gl = gl_ref[...] + bias_ref[...]
    la = -_softplus(-gl) * (1.0 / GLA_TAU)
    lower = _iota2((C, C), 0) >= _iota2((C, C), 1)
    cum = _fdot(jnp.where(lower, 1.0, 0.0), la)
    ref = _pick_row(cum, C // 2 - 1)
    clast = _pick_row(cum, C - 1)
    fw, bw = jnp.exp(cum - ref), jnp.exp(ref - cum)
    qs = q_ref[...] * (DK ** -0.5)
    k = k_ref[...]
    s_lo = _bdot(qs * fw, k * bw, NT)
    s_up = _bdot(qs * bw, k * fw, NT)
    scores = jnp.where(lower, s_lo, s_up)
    return gl, cum, clast, fw, bw, qs, k, v_ref[...], scores, lower


def _gla_fwd(proj, glogit, bias, gain, *, name):
    T = proj.shape[0]
    H, C, DK, DV = GLA_HEADS, CHUNK, GLA_DK, GLA_DV
    N = T // C

    def body(q_ref, k_ref, v_ref, r_ref, gl_ref, bias_ref, gain_ref, og_ref, st_ref, s_acc):
        n = pl.program_id(1)

        @pl.when(n == 0)
        def _():
            s_acc[...] = jnp.zeros_like(s_acc)

        gl, cum, clast, fw, bw, qs, k, v, scores, lower = _gla_chunk(q_ref, k_ref, v_ref, gl_ref, bias_ref)
        sp = s_acc[...]
        st_ref[0, 0] = sp
        o = _bdot(scores, v) + _bdot(qs * jnp.exp(cum), sp, NT)
        s_acc[...] = sp * jnp.exp(clast) + _bdot(v, k * jnp.exp(clast - cum), TN)
        rstd = lax.rsqrt(jnp.mean(o * o, axis=-1, keepdims=True) + NORM_EPS)
        og_ref[...] = (o * rstd * gain_ref[0] * _silu(r_ref[...])).astype(og_ref.dtype)

    q, k, v, r, bias_s, gn, st, ov = _gla_specs(N, False)
    return pl.pallas_call(
        body, name=name, grid=(H, N), in_specs=[q, k, v, r, q, bias_s, gn], out_specs=(ov, st),
        out_shape=(jax.ShapeDtypeStruct((T, H * DV), BF16), jax.ShapeDtypeStruct((H, N, DV, DK), F32)),
        scratch_shapes=[pltpu.VMEM((DV, DK), F32)],
        compiler_params=_params(("arbitrary", "arbitrary")),
    )(proj, proj, proj, proj, glogit, bias, gain)


def _gla_bwd(proj, glogit, bias, gain, states, dog, *, name):
    T = proj.shape[0]
    H, C, DK, DV = GLA_HEADS, CHUNK, GLA_DK, GLA_DV
    N = T // C

    def body(q_ref, k_ref, v_ref, r_ref, gl_ref, bias_ref, gain_ref, st_ref, dog_ref,
             dq_ref, dk_ref, dv_ref, dr_ref, dgl_ref, dgain_ref, ds_acc):
        n = pl.program_id(1)

        @pl.when(n == 0)
        def _():
            ds_acc[...] = jnp.zeros_like(ds_acc)
            dgain_ref[...] = jnp.zeros_like(dgain_ref)

        gl, cum, clast, fw, bw, qs, k, v, scores, lower = _gla_chunk(q_ref, k_ref, v_ref, gl_ref, bias_ref)
        sp = st_ref[0, 0]
        ecum, e2, cdec = jnp.exp(cum), jnp.exp(clast - cum), jnp.exp(clast)
        q_in, k_end = qs * ecum, k * e2
        o = _bdot(scores, v) + _bdot(q_in, sp, NT)
        rstd = lax.rsqrt(jnp.mean(o * o, axis=-1, keepdims=True) + NORM_EPS)
        oh = o * rstd
        r = r_ref[...]
        gain_v = gain_ref[0]
        dy = dog_ref[...].astype(F32)
        dr_ref[...] = (dy * oh * gain_v * _dsilu(r)).astype(dr_ref.dtype)
        dnorm = dy * _silu(r)
        dgain_ref[0] += jnp.sum(dnorm * oh, axis=0, keepdims=True)
        doh = dnorm * gain_v
        do = rstd * (doh - oh * jnp.mean(doh * oh, axis=-1, keepdims=True))
        dsn = ds_acc[...]
        dq_in = _bdot(do, sp)
        dk_end = _bdot(v, dsn)
        dv_ref[...] = (_bdot(k_end, dsn, NT) + _bdot(scores, do, TN)).astype(dv_ref.dtype)
        dcdec = jnp.sum(dsn * sp, axis=0, keepdims=True)
        ds_acc[...] = dsn * cdec + _bdot(do, q_in, TN)
        dsc = _bdot(do, v, NT)
        ds_lo = jnp.where(lower, dsc, 0.0)
        ds_up = jnp.where(lower, 0.0, dsc)
        qf, kb, qb, kf = qs * fw, k * bw, qs * bw, k * fw
        dqf, dkb = _bdot(ds_lo, kb), _bdot(ds_lo, qf, TN)
        dqb, dkf = _bdot(ds_up, kf), _bdot(ds_up, qb, TN)
        dq_ref[...] = ((dqf * fw + dqb * bw + dq_in * ecum) * (DK ** -0.5)).astype(dq_ref.dtype)
        dk_ref[...] = (dkb * bw + dkf * fw + dk_end * e2).astype(dk_ref.dtype)
        dz = (dqf * qs + dkf * k) * fw - (dqb * qs + dkb * k) * bw
        kk = dk_end * k_end
        dcum = dz + dq_in * q_in - kk
        rows = _iota2((C, DK), 0)
        dcum = dcum + jnp.where(rows == C // 2 - 1, -jnp.sum(dz, axis=0, keepdims=True), 0.0)
        dcum = dcum + jnp.where(rows == C - 1, jnp.sum(kk, axis=0, keepdims=True) + dcdec * cdec, 0.0)
        upper = _iota2((C, C), 0) <= _iota2((C, C), 1)
        dla = _fdot(jnp.where(upper, 1.0, 0.0), dcum)
        dgl_ref[...] = dla * (1.0 / GLA_TAU) * _sigmoid(-gl)

    q, k, v, r, bias_s, gn, st, ov = _gla_specs(N, True)
    return pl.pallas_call(
        body, name=name, grid=(H, N), in_specs=[q, k, v, r, q, bias_s, gn, st, ov],
        out_specs=(q, q, ov, ov, q, gn),
        out_shape=(jax.ShapeDtypeStruct((T, H * DK), BF16), jax.ShapeDtypeStruct((T, H * DK), BF16),
                   jax.ShapeDtypeStruct((T, H * DV), BF16), jax.ShapeDtypeStruct((T, H * DV), BF16),
                   jax.ShapeDtypeStruct((T, H * DK), F32), jax.ShapeDtypeStruct((H, 1, DV), F32)),
        scratch_shapes=[pltpu.VMEM((DV, DK), F32)],
        compiler_params=_params(("arbitrary", "arbitrary")),
    )(proj, proj, proj, proj, glogit, bias, gain, states, dog)


def _conv(xv, w_ref):
    out = _shift_down(xv, CONV_WIDTH - 1) * w_ref[0:1, :]
    for tap in range(1, CONV_WIDTH):
        out = out + _shift_down(xv, CONV_WIDTH - 1 - tap) * w_ref[tap:tap + 1, :]
    return out


def _conv_bwd(xv, w_ref, dpre, dw_ref):
    dx = None
    for tap in range(CONV_WIDTH):
        s = CONV_WIDTH - 1 - tap
        t = _shift_up(dpre, s) * w_ref[tap:tap + 1, :]
        dx = t if dx is None else dx + t
        dw_ref[tap:tap + 1, :] = jnp.sum(dpre * _shift_down(xv, s), axis=0, keepdims=True)
    return dx


CONV_COLS = 256


def _conv_silu_fwd(x, w, *, name):
    T = x.shape[0]
    n = w.shape[1]

    def body(x_ref, w_ref, o_ref):
        o_ref[...] = _silu(_conv(x_ref[...], w_ref))

    return pl.pallas_call(
        body, name=name, grid=(n // CONV_COLS,),
        in_specs=[pl.BlockSpec((T, CONV_COLS), lambda j: (0, j)), pl.BlockSpec((CONV_WIDTH, CONV_COLS), lambda j: (0, j))],
        out_specs=pl.BlockSpec((T, CONV_COLS), lambda j: (0, j)),
        out_shape=jax.ShapeDtypeStruct((T, n), F32), compiler_params=_params(("parallel",)),
    )(x, w)


def _conv_silu_bwd(x, w, dact, *, name):
    T = x.shape[0]
    n = w.shape[1]

    def body(x_ref, w_ref, da_ref, dx_ref, dw_ref):
        xv = x_ref[...]
        dpre = da_ref[...] * _dsilu(_conv(xv, w_ref))
        dx_ref[...] = _conv_bwd(xv, w_ref, dpre, dw_ref).astype(dx_ref.dtype)

    blk = pl.BlockSpec((T, CONV_COLS), lambda j: (0, j))
    wb = pl.BlockSpec((CONV_WIDTH, CONV_COLS), lambda j: (0, j))
    return pl.pallas_call(
        body, name=name, grid=(n // CONV_COLS,), in_specs=[blk, wb, blk], out_specs=(blk, wb),
        out_shape=(jax.ShapeDtypeStruct((T, n), BF16), jax.ShapeDtypeStruct((CONV_WIDTH, n), F32)),
        compiler_params=_params(("parallel",)),
    )(x, w, dact)


def _unit_lower_inverse(a):
    n = a.shape[0]
    eye = jnp.where(_iota2((n, n), 0) == _iota2((n, n), 1), 1.0, 0.0)
    p = -a
    t = eye + p
    for _ in range(5):
        p = _fdot(p, p)
        t = t + _fdot(t, p)
    return t


def _gdn_specs(N, rev):
    H, C, DK, DV = GDN_HEADS, CHUNK, GDN_DK, GDN_DV
    cn = (lambda n: N - 1 - n) if rev else (lambda n: n)
    q = pl.BlockSpec((C, DK), lambda h, n: (cn(n), h))
    k = pl.BlockSpec((C, DK), lambda h, n: (cn(n), H + h))
    v = pl.BlockSpec((C, DV), lambda h, n: (cn(n), 2 * H + h))
    z = pl.BlockSpec((C, DV), lambda h, n: (cn(n), 3 * H + h))
    gates = pl.BlockSpec((C, LANE), lambda h, n: (cn(n), 0))
    sc = pl.BlockSpec((1, 1, 1), lambda h, n: (h, 0, 0))
    gain = pl.BlockSpec((1, DV), lambda h, n: (0, 0))
    st = pl.BlockSpec((1, 1, DK, DV), lambda h, n: (h, cn(n), 0, 0))
    return q, k, v, z, gates, sc, gain, st


def _gdn_chunk(q_ref, k_ref, v_ref, gates_ref, alog_ref, dtb_ref, h):
    H, C, DK, DV = GDN_HEADS, CHUNK, GDN_DK, GDN_DV
    gates = gates_ref[...]
    lane = _iota2(gates.shape, 1)
    bl = jnp.sum(jnp.where(lane == h, gates, 0.0), axis=1, keepdims=True)
    al = jnp.sum(jnp.where(lane == H + h, gates, 0.0), axis=1, keepdims=True)
    beta = _sigmoid(bl)
    ea = jnp.exp(alog_ref[0])
    xs = al + dtb_ref[0]
    la = -ea * _softplus(xs)
    ii, jj = _iota2((C, C), 0), _iota2((C, C), 1)
    strict = ii > jj
    cum_col = jnp.sum(jnp.where(ii >= jj, _col_to_row(la), 0.0), axis=1, keepdims=True)
    cum_row = jnp.sum(jnp.where(ii <= jj, la, 0.0), axis=0, keepdims=True)
    q, k, v = q_ref[...], k_ref[...], v_ref[...]
    rq = lax.rsqrt(jnp.sum(q * q, axis=-1, keepdims=True) + NORM_EPS)
    rk = lax.rsqrt(jnp.sum(k * k, axis=-1, keepdims=True) + NORM_EPS)
    qn = q * rq * (DK ** -0.5)
    kn = k * rk
    rel = jnp.where(strict, jnp.exp(jnp.where(strict, cum_col - cum_row, 0.0)), 0.0)
    rg = rel * _bdot(kn, kn, NT)
    a = beta * rg
    tm = _unit_lower_inverse(a)
    e_col = jnp.exp(cum_col)
    clast = _pick_row(cum_col, C - 1)
    rhs = jnp.concatenate([beta * v, (beta * e_col) * kn], axis=1)
    sol = _fdot(tm, rhs)
    u, w = sol[:, :DV], sol[:, DV:]
    dd = jnp.exp(clast - cum_col)
    ke = kn * dd
    g = jnp.exp(clast)
    eye = jnp.where(_iota2((DK, DK), 0) == _iota2((DK, DK), 1), 1.0, 0.0)
    trans = g * eye - _bdot(ke, w, TN)
    inject = _bdot(ke, u, TN)
    return dict(beta=beta, ea=ea, xs=xs, la=la, strict=strict, ii=ii, jj=jj, q=q, k=k, v=v, rq=rq, rk=rk,
                qn=qn, kn=kn, rel=rel, rg=rg, a=a, tm=tm, e_col=e_col, sol=sol, u=u, w=w, dd=dd, ke=ke,
                g=g, eye=eye, trans=trans, inject=inject)


def _gdn_fwd(act, proj, gates, a_log, dt_bias, gain, *, name):
    T = act.shape[0]
    H, C, DK, DV = GDN_HEADS, CHUNK, GDN_DK, GDN_DV
    N = T // C

    def body(q_ref, k_ref, v_ref, z_ref, gates_ref, alog_ref, dtb_ref, gain_ref, og_ref, st_ref, s_acc):
        h, n = pl.program_id(0), pl.program_id(1)

        @pl.when(n == 0)
        def _():
            s_acc[...] = jnp.zeros_like(s_acc)

        c = _gdn_chunk(q_ref, k_ref, v_ref, gates_ref, alog_ref, dtb_ref, h)
        sp = s_acc[...]
        st_ref[0, 0] = sp
        snew = _bdot(c["trans"], sp) + c["inject"]
        s_acc[...] = snew
        o = _bdot(c["qn"], snew)
        rstd = lax.rsqrt(jnp.mean(o * o, axis=-1, keepdims=True) + NORM_EPS)
        og_ref[...] = (o * rstd * gain_ref[...] * _silu(z_ref[...])).astype(og_ref.dtype)

    q, k, v, z, gt, sc, gn, st = _gdn_specs(N, False)
    return pl.pallas_call(
        body, name=name, grid=(H, N), in_specs=[q, k, v, z, gt, sc, sc, gn], out_specs=(q, st),
        out_shape=(jax.ShapeDtypeStruct((T, H * DV), BF16), jax.ShapeDtypeStruct((H, N, DK, DV), F32)),
        scratch_shapes=[pltpu.VMEM((DK, DV), F32)],
        compiler_params=_params(("arbitrary", "arbitrary")),
    )(act, act, act, proj, gates, a_log, dt_bias, gain)


def _gdn_bwd(act, proj, gates, a_log, dt_bias, gain, states, dog, *, name):
    T = act.shape[0]
    H, C, DK, DV = GDN_HEADS, CHUNK, GDN_DK, GDN_DV
    N = T // C

    def rsum(x):
        return jnp.sum(x, axis=1, keepdims=True)

    def body(q_ref, k_ref, v_ref, z_ref, gates_ref, alog_ref, dtb_ref, gain_ref, st_ref, dog_ref,
             dq_ref, dk_ref, dv_ref, dz_ref, dgates_ref, dscal_ref, dgain_ref, ds_acc):
        h, n = pl.program_id(0), pl.program_id(1)

        @pl.when(n == 0)
        def _():
            ds_acc[...] = jnp.zeros_like(ds_acc)
            dgain_ref[...] = jnp.zeros_like(dgain_ref)
            dscal_ref[...] = jnp.zeros_like(dscal_ref)

        c = _gdn_chunk(q_ref, k_ref, v_ref, gates_ref, alog_ref, dtb_ref, h)
        beta, kn, qn, v, ke, u, w, dd, e_col = c["beta"], c["kn"], c["qn"], c["v"], c["ke"], c["u"], c["w"], c["dd"], c["e_col"]
        sp = st_ref[0, 0]
        snew = _bdot(c["trans"], sp) + c["inject"]
        o = _bdot(qn, snew)
        rstd = lax.rsqrt(jnp.mean(o * o, axis=-1, keepdims=True) + NORM_EPS)
        oh = o * rstd
        z = z_ref[...]
        gain_v = gain_ref[...]
        dy = dog_ref[...].astype(F32)
        dz_ref[...] = (dy * oh * gain_v * _dsilu(z)).astype(dz_ref.dtype)
        dnorm = dy * _silu(z)
        dgain_ref[0] += jnp.sum(dnorm * oh, axis=0, keepdims=True)
        doh = dnorm * gain_v
        do = rstd * (doh - oh * jnp.mean(doh * oh, axis=-1, keepdims=True))

        dstot = ds_acc[...] + _bdot(qn, do, TN)
        dqn = _bdot(do, snew, NT)
        dtrans = _bdot(dstot, sp, NT)
        ds_acc[...] = _bdot(c["trans"], dstot, TN)
        dg = jnp.sum(jnp.sum(dtrans * c["eye"], axis=1, keepdims=True), axis=0, keepdims=True)
        m = -dtrans
        dke = _bdot(w, m, NT) + _bdot(u, dstot, NT)
        dw = _bdot(ke, m)
        du = _bdot(ke, dstot)
        drhs = _fdot(c["tm"], jnp.concatenate([du, dw], axis=1), TN)
        da = jnp.where(c["strict"], -_fdot(drhs, c["sol"], NT), 0.0)
        drhs_u, drhs_w = drhs[:, :DV], drhs[:, DV:]
        rwk = rsum(drhs_w * kn)
        dbeta = rsum(da * c["rg"]) + rsum(drhs_u * v) + rwk * e_col
        dgm = da * beta * c["rel"]
        dkn = _bdot(dgm, kn) + _bdot(dgm, kn, TN) + (beta * e_col) * drhs_w + dd * dke
        dv_ref[...] = beta * drhs_u
        r_ = da * c["a"]
        ddd = rsum(dke * kn)
        dc = rsum(r_) - _row_to_col(jnp.sum(r_, axis=0, keepdims=True)) + beta * rwk * e_col - ddd * dd
        dclast = jnp.sum(ddd * dd, axis=0, keepdims=True) + dg * c["g"]
        dc = dc + jnp.where(_iota2((C, 1), 0) == C - 1, dclast, 0.0)
        dla = jnp.sum(jnp.where(c["ii"] <= c["jj"], _col_to_row(dc), 0.0), axis=1, keepdims=True)
        dalog = jnp.sum(dla * c["la"], axis=0, keepdims=True)
        dxs = dla * (-c["ea"]) * _sigmoid(c["xs"])
        ddtb = jnp.sum(dxs, axis=0, keepdims=True)
        dbl = dbeta * beta * (1.0 - beta)
        lane = _iota2((C, LANE), 1)
        dgates_ref[0] = jnp.where(lane == 0, dbl, jnp.where(lane == 1, dxs, 0.0))
        lane8 = _iota2((8, LANE), 1)
        dscal_ref[0] += jnp.where(lane8 == 0, dalog, jnp.where(lane8 == 1, ddtb, 0.0))
        dk_ref[...] = c["rk"] * (dkn - kn * rsum(dkn * kn))
        qh = c["q"] * c["rq"]
        dqs = dqn * (DK ** -0.5)
        dq_ref[...] = c["rq"] * (dqs - qh * rsum(dqs * qh))

    q, k, v, z, gt, sc, gn, st = _gdn_specs(N, True)
    dgt = pl.BlockSpec((1, C, LANE), lambda h, n: (h, N - 1 - n, 0))
    dsc = pl.BlockSpec((1, 8, LANE), lambda h, n: (h, 0, 0))
    dgn = pl.BlockSpec((1, 1, DV), lambda h, n: (h, 0, 0))
    sh = jax.ShapeDtypeStruct((T, H * DK), F32)
    return pl.pallas_call(
        body, name=name, grid=(H, N), in_specs=[q, k, v, z, gt, sc, sc, gn, st, q],
        out_specs=(q, q, q, q, dgt, dsc, dgn),
        out_shape=(sh, sh, sh, jax.ShapeDtypeStruct((T, H * DV), BF16),
                   jax.ShapeDtypeStruct((H, T, LANE), F32), jax.ShapeDtypeStruct((H, 8, LANE), F32),
                   jax.ShapeDtypeStruct((H, 1, DV), F32)),
        scratch_shapes=[pltpu.VMEM((DK, DV), F32)],
        compiler_params=_params(("arbitrary", "arbitrary")),
    )(act, act, act, proj, gates, a_log, dt_bias, gain, states, dog)


SUBLANES = 8


def _linear_scan(a_ref, b_ref, h_ref, reverse):
    T, W = a_ref.shape
    nb = T // SUBLANES
    row = _iota2((SUBLANES, W), 0)

    def blk(bi, carry):
        bb = (nb - 1 - bi) if reverse else bi
        off = pl.multiple_of(bb * SUBLANES, SUBLANES)
        a = a_ref[pl.ds(off, SUBLANES), :]
        b = b_ref[pl.ds(off, SUBLANES), :]
        for d in (1, 2, 4):
            if reverse:
                edge = row >= SUBLANES - d
                a_sh = jnp.where(edge, 1.0, pltpu.roll(a, SUBLANES - d, 0))
                b_sh = jnp.where(edge, 0.0, pltpu.roll(b, SUBLANES - d, 0))
            else:
                edge = row < d
                a_sh = jnp.where(edge, 1.0, pltpu.roll(a, d, 0))
                b_sh = jnp.where(edge, 0.0, pltpu.roll(b, d, 0))
            b = a * b_sh + b
            a = a * a_sh
        h = a * carry + b
        h_ref[pl.ds(off, SUBLANES), :] = h
        return h[0:1, :] if reverse else h[SUBLANES - 1:SUBLANES, :]

    lax.fori_loop(0, nb, blk, jnp.zeros((1, W), F32))


def _lru_specs(T):
    B, W = LRU_BLOCKS, LRU_BLOCK
    xb = pl.BlockSpec((T, W), lambda j: (0, j))
    yb = pl.BlockSpec((T, W), lambda j: (0, B + j))
    cw = pl.BlockSpec((CONV_WIDTH, W), lambda j: (0, j))
    vec = pl.BlockSpec((1, W), lambda j: (0, j))
    wg = pl.BlockSpec((1, W, W), lambda j: (j, 0, 0))
    bg = pl.BlockSpec((1, 1, W), lambda j: (j, 0, 0))
    return xb, yb, cw, vec, wg, bg


def _lru_gates(xb_ref, cw_ref, cb_ref, wr_ref, br_ref, wi_ref, bi_ref, lam_ref):
    xv = xb_ref[...]
    xc = _conv(xv, cw_ref) + cb_ref[...]
    r = _sigmoid(_bdot(xc, wr_ref[0]) + br_ref[0])
    i = _sigmoid(_bdot(xc, wi_ref[0]) + bi_ref[0])
    sp = _softplus(-lam_ref[...])
    la = -LRU_C * sp * r
    a = jnp.exp(la)
    s = jnp.sqrt(-_expm1(2.0 * la))
    return xv, xc, r, i, sp, a, s


def _lru_fwd(proj, conv_w, conv_b, w_r, b_r, w_i, b_i, lam, *, name):
    T = proj.shape[0]
    B, W = LRU_BLOCKS, LRU_BLOCK

    def body(xb_ref, yb_ref, cw_ref, cb_ref, wr_ref, br_ref, wi_ref, bi_ref, lam_ref, og_ref, hs_ref, a_s, u_s):
        xv, xc, r, i, sp, a, s = _lru_gates(xb_ref, cw_ref, cb_ref, wr_ref, br_ref, wi_ref, bi_ref, lam_ref)
        a_s[...] = a
        u_s[...] = s * (i * xc)
        _linear_scan(a_s, u_s, hs_ref, False)
        og_ref[...] = (hs_ref[...] * _gelu(yb_ref[...])).astype(og_ref.dtype)

    xb, yb, cw, vec, wg, bg = _lru_specs(T)
    return pl.pallas_call(
        body, name=name, grid=(B,), in_specs=[xb, yb, cw, vec, wg, bg, wg, bg, vec], out_specs=(xb, xb),
        out_shape=(jax.ShapeDtypeStruct((T, B * W), BF16), jax.ShapeDtypeStruct((T, B * W), F32)),
        scratch_shapes=[pltpu.VMEM((T, W), F32), pltpu.VMEM((T, W), F32)],
        compiler_params=_params(("arbitrary",)),
    )(proj, proj, conv_w, conv_b, w_r, b_r, w_i, b_i, lam)


def _lru_bwd(proj, conv_w, conv_b, w_r, b_r, w_i, b_i, lam, hs, dout, *, name):
    T = proj.shape[0]
    B, W = LRU_BLOCKS, LRU_BLOCK

    def csum(x):
        return jnp.sum(x, axis=0, keepdims=True)

    def body(xb_ref, yb_ref, cw_ref, cb_ref, wr_ref, br_ref, wi_ref, bi_ref, lam_ref, hs_ref, do_ref,
             dxb_ref, dyb_ref, dcw_ref, dcb_ref, dwr_ref, dbr_ref, dwi_ref, dbi_ref, dlam_ref, a_s, b_s, g_s):
        xv, xc, r, i, sp, a, s = _lru_gates(xb_ref, cw_ref, cb_ref, wr_ref, br_ref, wi_ref, bi_ref, lam_ref)
        h = hs_ref[...]
        yb = yb_ref[...]
        dout = do_ref[...].astype(F32)
        dyb_ref[...] = (dout * h * _dgelu(yb)).astype(dyb_ref.dtype)
        a_s[...] = _shift_up(a, 1)
        b_s[...] = dout * _gelu(yb)
        _linear_scan(a_s, b_s, g_s, True)
        g = g_s[...]
        da = g * _shift_down(h, 1)
        ds = g * (i * xc)
        di = g * s * xc
        dxc = g * s * i
        dla = da * a - ds * (a * a) / s
        dr = dla * (-LRU_C * sp)
        dlam_ref[...] = csum(dla * r) * (LRU_C * _sigmoid(-lam_ref[...]))
        dpr = dr * r * (1.0 - r)
        dpi = di * i * (1.0 - i)
        dxc = dxc + _bdot(dpr, wr_ref[0], NT) + _bdot(dpi, wi_ref[0], NT)
        dwr_ref[0] = _bdot(xc, dpr, TN)
        dwi_ref[0] = _bdot(xc, dpi, TN)
        dbr_ref[0] = csum(dpr)
        dbi_ref[0] = csum(dpi)
        dcb_ref[...] = csum(dxc)
        dxb_ref[...] = _conv_bwd(xv, cw_ref, dxc, dcw_ref).astype(dxb_ref.dtype)

    xb, yb, cw, vec, wg, bg = _lru_specs(T)
    act = jax.ShapeDtypeStruct((T, B * W), BF16)
    return pl.pallas_call(
        body, name=name, grid=(B,), in_specs=[xb, yb, cw, vec, wg, bg, wg, bg, vec, xb, xb],
        out_specs=(xb, xb, cw, vec, wg, bg, wg, bg, vec),
        out_shape=(act, act, jax.ShapeDtypeStruct((CONV_WIDTH, B * W), F32), jax.ShapeDtypeStruct((1, B * W), F32),
                   jax.ShapeDtypeStruct((B, W, W), F32), jax.ShapeDtypeStruct((B, 1, W), F32),
                   jax.ShapeDtypeStruct((B, W, W), F32), jax.ShapeDtypeStruct((B, 1, W), F32),
                   jax.ShapeDtypeStruct((1, B * W), F32)),
        scratch_shapes=[pltpu.VMEM((T, W), F32)] * 3,
        compiler_params=_params(("arbitrary",)),
    )(proj, proj, conv_w, conv_b, w_r, b_r, w_i, b_i, lam, hs, dout)


MESH = pl.DeviceIdType.MESH
N_CHIPS = 4
AG_COPIES = 7


def _mesh_pos():
    return lax.axis_index("x"), lax.axis_index("y"), lax.axis_index("c")


def _hbm_specs(n):
    return [pl.BlockSpec(memory_space=pltpu.HBM)] * n


def _all_gather(shards, *, name):
    n = len(shards)

    def body(*refs):
        xs, outs = refs[:n], refs[n:2 * n]
        send_sems, recv_sems, local_sems = refs[2 * n:]
        x, y, c = _mesh_pos()
        me, sibling = (x, y, c), (x, y, 1 - c)
        chips = [(1 - x, y), (x, 1 - y), (1 - x, 1 - y)]

        def rows(t, px, py, pc):
            return outs[t].at[4 * px + 2 * py + pc]

        def copy(t, k, block, to, src=None):
            return pltpu.make_async_remote_copy(
                src_ref=rows(t, *block) if src is None else src, dst_ref=rows(t, *block),
                send_sem=send_sems.at[t * AG_COPIES + k], recv_sem=recv_sems.at[t * AG_COPIES + k],
                device_id=to, device_id_type=MESH)

        mine = [pltpu.make_async_copy(xs[t], rows(t, *me), local_sems.at[t]) for t in range(n)]
        for cp in mine:
            cp.start()
        first = []
        for t in range(n):
            first.append(copy(t, 0, me, sibling, src=xs[t]))
            first += [copy(t, 1 + j, me, (*chip, c), src=xs[t]) for j, chip in enumerate(chips)]
        for cp in first:
            cp.start()
        passed = []
        for j, chip in enumerate(chips):
            for t in range(n):
                copy(t, 1 + j, (*chip, c), me).wait_recv()
                cp = copy(t, 4 + j, (*chip, c), sibling)
                cp.start()
                passed.append(cp)
        for t in range(n):
            copy(t, 0, sibling, me).wait_recv()
        for j, chip in enumerate(chips):
            for t in range(n):
                copy(t, 4 + j, (*chip, 1 - c), me).wait_recv()
        for cp in first + passed:
            cp.wait_send()
        for cp in mine:
            cp.wait()

    return pl.pallas_call(
        body, name=name,
        out_shape=[jax.ShapeDtypeStruct((N_DEV,) + s.shape, s.dtype) for s in shards],
        in_specs=_hbm_specs(n), out_specs=_hbm_specs(n),
        scratch_shapes=[pltpu.SemaphoreType.DMA((n * AG_COPIES,)), pltpu.SemaphoreType.DMA((n * AG_COPIES,)),
                        pltpu.SemaphoreType.DMA((n,))],
    )(*shards)


SIDE_EFFECT = pltpu.SideEffectType.DATAFLOW_SIDE_EFFECTING


def _copies(plan, refs, send_sems, recv_sems):
    return [pltpu.make_async_remote_copy(src_ref=src, dst_ref=dst, send_sem=send_sems.at[k], recv_sem=recv_sems.at[k],
                                         device_id=to, device_id_type=MESH)
            for k, (src, dst, to) in enumerate(plan(refs))]


def _split_start(bufs, plan, n_copies, *, name, deps=()):
    n = len(bufs)

    def body(*refs):
        send_sems, recv_sems = refs[n + len(deps)], refs[n + len(deps) + 1]
        token = refs[-1]
        for cp in _copies(plan, refs[:n], send_sems, recv_sems):
            cp.start()
        token[...] = jnp.zeros_like(token)

    hbm, sem = pl.BlockSpec(memory_space=pltpu.HBM), pl.BlockSpec(memory_space=pltpu.SEMAPHORE)
    out = pl.pallas_call(
        body, name=name,
        out_shape=(pltpu.SemaphoreType.DMA((n_copies,)), pltpu.SemaphoreType.DMA((n_copies,)),
                   *[pltpu.HBM(b.shape, b.dtype) for b in bufs], jax.ShapeDtypeStruct(TOKEN_SHAPE, F32)),
        in_specs=[hbm] * n + [pl.BlockSpec(memory_space=pl.ANY)] * len(deps),
        out_specs=(sem, sem, *[hbm] * n, pl.BlockSpec(memory_space=pltpu.VMEM)),
        input_output_aliases={i: 2 + i for i in range(n)},
        compiler_params=pltpu.CompilerParams(has_side_effects=SIDE_EFFECT),
    )(*[pltpu.with_memory_space_constraint(b, pltpu.HBM) for b in bufs], *deps)
    return out[0], out[1], list(out[2:2 + n]), out[-1]


def _split_wait(send_sems, recv_sems, bufs, plan, after, *, name):
    n = len(bufs)

    def body(*refs):
        for cp in _copies(plan, refs[:n], refs[n], refs[n + 1]):
            cp.wait_send()
            cp.wait_recv()

    hbm, sem = pl.BlockSpec(memory_space=pltpu.HBM), pl.BlockSpec(memory_space=pltpu.SEMAPHORE)
    out = pl.pallas_call(
        body, name=name, out_shape=tuple(pltpu.HBM(b.shape, b.dtype) for b in bufs),
        in_specs=[hbm] * n + [sem, sem, pl.BlockSpec(memory_space=pl.ANY)], out_specs=tuple([hbm] * n),
        input_output_aliases={i: i for i in range(n)},
        compiler_params=pltpu.CompilerParams(has_side_effects=SIDE_EFFECT),
    )(*bufs, send_sems, recv_sems, after)
    return list(out)


def _plan_gather_a(n):
    def plan(refs):
        x, y, c = _mesh_pos()
        me = 4 * x + 2 * y + c
        peers = [(x, y, 1 - c), (1 - x, y, c), (x, 1 - y, c), (1 - x, 1 - y, c)]
        return [(refs[t], refs[n + t].at[me], to) for t in range(n) for to in peers]
    return plan


def _plan_gather_b(n):
    def plan(refs):
        x, y, c = _mesh_pos()
        rows = [4 * px + 2 * py + c for px, py in [(1 - x, y), (x, 1 - y), (1 - x, 1 - y)]]
        return [(refs[t].at[r], refs[t].at[r], (x, y, 1 - c)) for t in range(n) for r in rows]
    return plan


def _plan_scatter_pair(n):
    def plan(refs):
        x, y, c = _mesh_pos()
        return [(refs[t].at[2 * q + (1 - c)], refs[n + t].at[q], (x, y, 1 - c))
                for t in range(n) for q in range(N_CHIPS)]
    return plan


def _plan_scatter_chips(n):
    def plan(refs):
        x, y, c = _mesh_pos()
        chips = [(1 - x, y), (x, 1 - y), (1 - x, 1 - y)]
        return [(refs[t].at[2 * px + py], refs[n + t].at[j], (px, py, c))
                for t in range(n) for j, (px, py) in enumerate(chips)]
    return plan


COMM_ROWS = 128


def _pair_add(g, a, core, *, name):
    _, R, C = g.shape
    tr = _tile(R, (COMM_ROWS,))

    def body(c_ref, g_ref, a_ref, o_ref):
        o_ref[...] = (g_ref[...].astype(F32) + a_ref[...].astype(F32)).astype(o_ref.dtype)

    blk = pl.BlockSpec((1, tr, C), lambda q, i, c: (q, i, 0))
    return pl.pallas_call(
        body, name=name,
        grid_spec=pltpu.PrefetchScalarGridSpec(
            num_scalar_prefetch=1, grid=(N_CHIPS, R // tr),
            in_specs=[pl.BlockSpec((1, tr, C), lambda q, i, c: (2 * q + c[0], i, 0)), blk], out_specs=blk),
        out_shape=jax.ShapeDtypeStruct((N_CHIPS, R, C), BF16),
        compiler_params=_params(("parallel", "parallel")),
    )(core, g, a)


def _adamw_sharded(w, m, v, s4, b3, chip, *, name):
    R, C = w.shape
    tr = _tile(R, (COMM_ROWS,))

    def body(q_ref, w_ref, m_ref, v_ref, s_ref, b_ref, g_out, d_out, m_out, v_out):
        g = s_ref[0].astype(F32)
        for j in range(N_CHIPS - 1):
            g = g + b_ref[j].astype(F32)
        mn = ADAM_B1 * m_ref[...] + (1.0 - ADAM_B1) * g
        vn = ADAM_B2 * v_ref[...] + (1.0 - ADAM_B2) * (g * g)
        g_out[...] = g
        d_out[...] = -ADAM_LR * ((mn / ADAM_C1) / (jnp.sqrt(vn / ADAM_C2) + ADAM_EPS) + ADAM_WD * w_ref[...])
        m_out[...] = mn
        v_out[...] = vn

    blk = pl.BlockSpec((tr, C), lambda i, q: (i, 0))
    sh = jax.ShapeDtypeStruct((R, C), F32)
    return pl.pallas_call(
        body, name=name,
        grid_spec=pltpu.PrefetchScalarGridSpec(
            num_scalar_prefetch=1, grid=(R // tr,),
            in_specs=[blk, blk, blk, pl.BlockSpec((1, tr, C), lambda i, q: (q[0], i, 0)),
                      pl.BlockSpec((N_CHIPS - 1, tr, C), lambda i, q: (0, i, 0))],
            out_specs=(blk,) * 4),
        out_shape=(sh,) * 4, compiler_params=_params(("parallel",)),
    )(chip, w, m, v, s4, b3)


FWD_NAMES = ['x', 'norm1', 'norm2', 'final_norm', 'ret_w_in', 'ret_gn_gain', 'ret_w_out', 'gdn_w_in', 'gdn_conv_w',
             'gdn_a_log', 'gdn_dt_bias', 'gdn_norm_gain', 'gdn_w_out', 'gla_w_in', 'gla_w_gate_up', 'gla_gate_bias',
             'gla_norm_gain', 'gla_w_out', 'lru_w_in', 'lru_conv_w', 'lru_conv_b', 'lru_w_rgate', 'lru_b_rgate',
             'lru_w_igate', 'lru_b_igate', 'lru_lambda', 'lru_w_out', 'mlp_w_up', 'mlp_w_down']
WEIGHT_NAMES = FWD_NAMES[1:]
ARG_NAMES = FWD_NAMES + ['loss_target'] + ['m_' + n for n in WEIGHT_NAMES] + ['v_' + n for n in WEIGHT_NAMES]

MIXER_IN = ('ret_w_in', 'gdn_w_in', 'gla_w_in', 'lru_w_in')
MIXER_OUT = ('ret_w_out', 'gdn_w_out', 'gla_w_out', 'lru_w_out')
BIG_NAMES = MIXER_IN + MIXER_OUT + ('mlp_w_up', 'mlp_w_down')
SMALL = {'norm1': False, 'norm2': False, 'final_norm': False, 'ret_gn_gain': True, 'gdn_conv_w': True,
         'gdn_a_log': False, 'gdn_dt_bias': False, 'gdn_norm_gain': False, 'gla_w_gate_up': True,
         'gla_gate_bias': True, 'gla_norm_gain': True, 'lru_conv_w': True, 'lru_conv_b': True,
         'lru_w_rgate': False, 'lru_b_rgate': False, 'lru_w_igate': False, 'lru_b_igate': False, 'lru_lambda': True}
SMALL_NAMES = tuple(n for n in WEIGHT_NAMES if n in SMALL)
GDN_TAIL = 2 * GDN_HEADS


def _pack(arrs):
    rows = []
    for a in arrs:
        f = a.reshape(-1).astype(F32)
        rows.append(jnp.pad(f, (0, (-f.shape[0]) % LANE)).reshape(-1, LANE))
    buf = jnp.concatenate(rows, axis=0)
    return jnp.pad(buf, ((0, (-buf.shape[0]) % SUBLANES), (0, 0)))


def _unpack(buf, shapes, lead=()):
    out, r0 = [], 0
    for s in shapes:
        n = int(np.prod(s))
        nr = -(-n // LANE)
        blk = buf[..., r0:r0 + nr, :].reshape(lead + (nr * LANE,))[..., :n]
        out.append(blk.reshape(lead + tuple(s)))
        r0 += nr
    return out


def _full_cols(g):
    return jnp.transpose(g, (1, 0, 2)).reshape(g.shape[1], N_DEV * g.shape[2])


def _full_rows(g):
    return g.reshape(N_DEV * g.shape[1], g.shape[2])


def _blocks_cols(dw):
    r, c = dw.shape[0], dw.shape[1] // N_DEV
    return jnp.transpose(dw.reshape(r, N_DEV, c), (1, 0, 2))


def _blocks_rows(dw):
    return dw.reshape(N_DEV, dw.shape[0] // N_DEV, dw.shape[1])


def _pad_cols(a, n=LANE):
    return jnp.pad(a, ((0, 0), (0, n - a.shape[1])))


def _mixer_fwd(layer, hn, w_in, sm, tables):
    tag = f"l{layer}"
    if layer == 0:
        proj = _matmul(hn, w_in, name=tag + "_in")
        gain = sm['ret_gn_gain'][0][:, None, :]
        og, st = _ret_fwd(proj, gain, tables, name=tag + "_ret_fwd")
        return og, dict(proj=proj, st=st, gain=gain)
    if layer == 1:
        w_main, w_tail = w_in[:, :4 * D_MODEL], _pad_cols(w_in[:, 4 * D_MODEL:])
        proj = _matmul(hn, w_main, name=tag + "_in")
        gates = _matmul(hn, w_tail, name=tag + "_in_tail")
        conv_w = sm['gdn_conv_w'][0]
        act = _conv_silu_fwd(proj, conv_w, name=tag + "_conv")
        a_log = sm['gdn_a_log'].reshape(GDN_HEADS, 1, 1)
        dt_bias = sm['gdn_dt_bias'].reshape(GDN_HEADS, 1, 1)
        gain = sm['gdn_norm_gain']
        og, st = _gdn_fwd(act, proj, gates, a_log, dt_bias, gain, name=tag + "_gdn_fwd")
        return og, dict(proj=proj, gates=gates, act=act, st=st, conv_w=conv_w, a_log=a_log, dt_bias=dt_bias,
                        gain=gain, w_main=w_main, w_tail=w_tail)
    if layer == 2:
        w_main, w_tail = w_in[:, :3 * D_MODEL], _pad_cols(w_in[:, 3 * D_MODEL:])
        proj = _matmul(hn, w_main, name=tag + "_in")
        glow = _matmul(hn, w_tail, name=tag + "_in_tail")
        wgu = jnp.pad(sm['gla_w_gate_up'][0], ((0, LANE - GLA_GATE_RANK), (0, 0)))
        glogit = _matmul(glow, wgu, name=tag + "_gate_up")
        bias = sm['gla_gate_bias']
        gain = sm['gla_norm_gain'][0][:, None, :]
        og, st = _gla_fwd(proj, glogit, bias, gain, name=tag + "_gla_fwd")
        return og, dict(proj=proj, glow=glow, glogit=glogit, wgu=wgu, bias=bias, gain=gain, st=st,
                        w_main=w_main, w_tail=w_tail)
    proj = _matmul(hn, w_in, name=tag + "_in")
    args = (proj, sm['lru_conv_w'][0], sm['lru_conv_b'], sm['lru_w_rgate'][0], sm['lru_b_rgate'][0][:, None, :],
            sm['lru_w_igate'][0], sm['lru_b_igate'][0][:, None, :], sm['lru_lambda'])
    og, hs = _lru_fwd(*args, name=tag + "_lru_fwd")
    return og, dict(args=args, hs=hs)


def _mixer_bwd(layer, hn, w_in, dog, sv, tables):
    tag = f"l{layer}"
    if layer == 0:
        dq, dk, dv, dg, dgain = _ret_bwd(sv['proj'], sv['gain'], tables, sv['st'], dog, name=tag + "_ret_bwd")
        dproj = jnp.concatenate([dq, dk, dv, dg], axis=1)
        dhn = _matmul(dproj, w_in, tb=True, name=tag + "_in_dx")
        dw = _matmul(hn, dproj, ta=True, out_dtype=BF16, name=tag + "_in_dw")
        return dhn, dw, {'ret_gn_gain': dgain[:, 0][None]}
    if layer == 1:
        dq, dk, dv, dz, dgates, dscal, dgain = _gdn_bwd(
            sv['act'], sv['proj'], sv['gates'], sv['a_log'], sv['dt_bias'], sv['gain'], sv['st'], dog,
            name=tag + "_gdn_bwd")
        dact = jnp.concatenate([dq, dk, dv], axis=1)
        dqkv, dconv = _conv_silu_bwd(sv['proj'], sv['conv_w'], dact, name=tag + "_conv_bwd")
        dmain = jnp.concatenate([dqkv, dz], axis=1)
        T = dmain.shape[0]
        dtail = _pad_cols(jnp.transpose(dgates[:, :, :2], (1, 2, 0)).reshape(T, GDN_TAIL))
        dhn = _matmul(dmain, sv['w_main'], tb=True, name=tag + "_in_dx")
        dhn = _matmul(dtail, sv['w_tail'], tb=True, epi="add", extra=dhn, name=tag + "_in_tail_dx")
        dw_main = _matmul(hn, dmain, ta=True, out_dtype=BF16, name=tag + "_in_dw")
        dw_tail = _matmul(hn, dtail, ta=True, out_dtype=BF16, name=tag + "_in_tail_dw")
        dw = jnp.concatenate([dw_main, dw_tail[:, :GDN_TAIL]], axis=1)
        small = {'gdn_conv_w': dconv[None], 'gdn_a_log': dscal[:, 0, 0][None], 'gdn_dt_bias': dscal[:, 0, 1][None],
                 'gdn_norm_gain': jnp.sum(dgain[:, 0], axis=0)[None]}
        return dhn, dw, small
    if layer == 2:
        dq, dk, dv, dr, dgl, dgain = _gla_bwd(sv['proj'], sv['glogit'], sv['bias'], sv['gain'], sv['st'], dog,
                                              name=tag + "_gla_bwd")
        dmain = jnp.concatenate([dq, dk, dv, dr], axis=1)
        dglow = _matmul(dgl, sv['wgu'], tb=True, name=tag + "_gate_up_dx")
        dwgu = _matmul(sv['glow'], dgl, ta=True, name=tag + "_gate_up_dw")
        dbias = _colsum(dgl, name=tag + "_gate_bias")
        dhn = _matmul(dmain, sv['w_main'], tb=True, name=tag + "_in_dx")
        dhn = _matmul(dglow, sv['w_tail'], tb=True, epi="add", extra=dhn, name=tag + "_in_tail_dx")
        dw_main = _matmul(hn, dmain, ta=True, out_dtype=BF16, name=tag + "_in_dw")
        dw_tail = _matmul(hn, dglow, ta=True, out_dtype=BF16, name=tag + "_in_tail_dw")
        dw = jnp.concatenate([dw_main, dw_tail[:, :GLA_GATE_RANK]], axis=1)
        small = {'gla_w_gate_up': dwgu[:GLA_GATE_RANK][None], 'gla_gate_bias': dbias,
                 'gla_norm_gain': dgain[:, 0][None]}
        return dhn, dw, small
    dxb, dyb, dcw, dcb, dwr, dbr, dwi, dbi, dlam = _lru_bwd(*sv['args'], sv['hs'], dog, name=tag + "_lru_bwd")
    dproj = jnp.concatenate([dxb, dyb], axis=1)
    dhn = _matmul(dproj, w_in, tb=True, name=tag + "_in_dx")
    dw = _matmul(hn, dproj, ta=True, out_dtype=BF16, name=tag + "_in_dw")
    small = {'lru_conv_w': dcw[None], 'lru_conv_b': dcb, 'lru_w_rgate': dwr[None], 'lru_b_rgate': dbr[:, 0][None],
             'lru_w_igate': dwi[None], 'lru_b_igate': dbi[:, 0][None], 'lru_lambda': dlam}
    return dhn, dw, small


def _step(*args):
    assert len(args) == len(ARG_NAMES)
    p = dict(zip(ARG_NAMES, args))
    xi, yi, ci = _mesh_pos()
    dev = 4 * xi + 2 * yi + ci
    core = ci.astype(jnp.int32).reshape(1)
    chip = (2 * xi + yi).astype(jnp.int32).reshape(1)
    x = p['x'][0]
    target = p['loss_target'][0]
    T = x.shape[0]
    tables = _ret_tables(T)

    n_big = 4
    gathers = {}
    token = None
    for layer in range(DEPTH):
        shards = [w.astype(BF16) for w in (p[MIXER_IN[layer]][0], p[MIXER_OUT[layer]][0],
                                           p['mlp_w_up'][layer], p['mlp_w_down'][layer])]
        lands = [lax.empty((N_DEV,) + s.shape, BF16) for s in shards]
        send, recv, bufs, token = _split_start(shards + lands, _plan_gather_a(n_big), 4 * n_big,
                                               name=f"gather_a_start_l{layer}",
                                               deps=() if token is None else (token,))
        gathers[layer] = (send, recv, bufs)

    def gather_forward(layer, after):
        send, recv, bufs = gathers[layer]
        bufs = _split_wait(send, recv, bufs, _plan_gather_a(n_big), after, name=f"gather_a_wait_l{layer}")
        send, recv, lands, tok = _split_start(bufs[n_big:], _plan_gather_b(n_big), 3 * n_big,
                                              name=f"gather_b_start_l{layer}")
        gathers[layer] = (send, recv, lands, bufs[:n_big])
        return tok

    def gather_finish(layer, after):
        send, recv, lands, shards = gathers[layer]
        lands = _split_wait(send, recv, lands, _plan_gather_b(n_big), after, name=f"gather_b_wait_l{layer}")
        full = [lax.dynamic_update_slice(l, s[None], (dev, 0, 0)) for l, s in zip(lands, shards)]
        return _full_cols(full[0]), _full_rows(full[1]), _full_cols(full[2]), _full_rows(full[3])

    big = {0: gather_finish(0, gather_forward(0, token))}
    sharded_small = [n for n in SMALL_NAMES if SMALL[n]]
    gathered, = _all_gather([_pack([p[n] for n in sharded_small])], name="gather_small")
    parts = _unpack(gathered, [p[n].shape for n in sharded_small], lead=(N_DEV,))
    sm = {n: p[n] for n in SMALL_NAMES if not SMALL[n]}
    for n, blk in zip(sharded_small, parts):
        full = jnp.moveaxis(blk, 0, -2)
        sm[n] = full.reshape(full.shape[:-2] + (N_DEV * full.shape[-1],))

    saved = []
    for layer in range(DEPTH):
        w_in, w_out, w_up, w_down = big[layer]
        tag = f"l{layer}"
        hn = _rmsnorm_fwd(x, sm['norm1'][layer][None], name=tag + "_norm1")
        og, sv = _mixer_fwd(layer, hn, w_in, sm, tables)
        deps = (gather_forward(layer + 1, og),) if layer + 1 < DEPTH else ()
        x_mid = _matmul(og, w_out, epi="add", extra=x, name=tag + "_out", deps=deps)
        hn2 = _rmsnorm_fwd(x_mid, sm['norm2'][layer][None], name=tag + "_norm2")
        u, a = _matmul(hn2, w_up, epi="relu2", name=tag + "_up")
        x_new = _matmul(a, w_down, epi="add", extra=x_mid, name=tag + "_down")
        if layer + 1 < DEPTH:
            big[layer + 1] = gather_finish(layer + 1, x_new)
        saved.append(dict(x=x, hn=hn, og=og, sv=sv, x_mid=x_mid, hn2=hn2, u=u, a=a))
        x = x_new
    dy, loss_part = _final_loss(x, sm['final_norm'][None], target, name="final_loss")
    loss = lax.psum(loss_part[0, 0], ("x", "y", "c"))
    dx, dfinal = _rmsnorm_bwd(x, sm['final_norm'][None], dy, jnp.zeros_like(x), name="final_norm_bwd")

    outs = {}
    small_grads = {'final_norm': dfinal[0]}
    dnorm1, dnorm2 = [None] * DEPTH, [None] * DEPTH
    mlp_res = {'mlp_w_up': [None] * DEPTH, 'mlp_w_down': [None] * DEPTH}

    def scatter_start(items, deps, tag):
        blocks = [b for _, _, b in items]
        n = len(blocks)
        lands = [lax.empty((N_CHIPS,) + b.shape[1:], BF16) for b in blocks]
        send, recv, bufs, tok = _split_start(blocks + lands, _plan_scatter_pair(n), N_CHIPS * n,
                                             name=f"scatter_pair_start_{tag}", deps=deps)
        return dict(items=items, n=n, tag=tag, h=(send, recv, bufs)), tok

    def scatter_forward(g, after):
        n, tag = g['n'], g['tag']
        send, recv, bufs = g['h']
        bufs = _split_wait(send, recv, bufs, _plan_scatter_pair(n), after, name=f"scatter_pair_wait_{tag}")
        sums = [_pair_add(b, a_, core, name=f"pair_add_{tag}_{i}") for i, (b, a_) in enumerate(zip(bufs[:n], bufs[n:]))]
        lands = [lax.empty((N_CHIPS - 1,) + s_.shape[1:], BF16) for s_ in sums]
        send, recv, bufs, tok = _split_start(sums + lands, _plan_scatter_chips(n), (N_CHIPS - 1) * n,
                                             name=f"scatter_chips_start_{tag}")
        g['h'] = (send, recv, bufs)
        return tok

    def scatter_finish(g, after):
        n, tag = g['n'], g['tag']
        send, recv, bufs = g['h']
        bufs = _split_wait(send, recv, bufs, _plan_scatter_chips(n), after, name=f"scatter_chips_wait_{tag}")
        for i, (wname, idx, _) in enumerate(g['items']):
            res = _adamw_sharded(p[wname][idx], p['m_' + wname][idx], p['v_' + wname][idx], bufs[i], bufs[n + i], chip,
                                 name=f"adamw_{tag}_{i}")
            if wname.startswith('mlp'):
                mlp_res[wname][idx] = res
            else:
                outs[wname] = tuple(r[None] for r in res)

    next_deps = ()
    mixer_group = None
    unfinished = []
    for layer in reversed(range(DEPTH)):
        w_in, w_out, w_up, w_down = big[layer]
        s = saved[layer]
        tag = f"l{layer}"
        du = _matmul(dx, w_down, tb=True, epi="drelu2", extra=s['u'], out_dtype=BF16, name=tag + "_down_dx",
                     deps=next_deps)
        dw_down = _matmul(s['a'], dx, ta=True, out_dtype=BF16, name=tag + "_down_dw")
        deps = ()
        if mixer_group is not None:
            deps = (scatter_forward(mixer_group, dw_down),)
            unfinished.append(mixer_group)
        dhn2 = _matmul(du, w_up, tb=True, name=tag + "_up_dx", deps=deps)
        dw_up = _matmul(s['hn2'], du, ta=True, out_dtype=BF16, name=tag + "_up_dw")
        dx, dn2 = _rmsnorm_bwd(s['x_mid'], sm['norm2'][layer][None], dhn2, dx, name=tag + "_norm2_bwd")
        mlp_group, tok = scatter_start([('mlp_w_up', layer, _blocks_cols(dw_up)),
                                        ('mlp_w_down', layer, _blocks_rows(dw_down))], (), f"mlp_l{layer}")
        dw_out = _matmul(s['og'], dx, ta=True, out_dtype=BF16, name=tag + "_out_dw", deps=(tok,))
        tok = scatter_forward(mlp_group, dw_out)
        dog = _matmul(dx, w_out, tb=True, name=tag + "_out_dx", deps=(tok,))
        dhn, dw_in, sg = _mixer_bwd(layer, s['hn'], w_in, dog, s['sv'], tables)
        dx, dn1 = _rmsnorm_bwd(s['x'], sm['norm1'][layer][None], dhn, dx, name=tag + "_norm1_bwd")
        dnorm1[layer], dnorm2[layer] = dn1[0], dn2[0]
        small_grads.update(sg)
        for g in unfinished:
            scatter_finish(g, dx)
        unfinished = [mlp_group]
        mixer_group, tok = scatter_start([(MIXER_IN[layer], 0, _blocks_cols(dw_in)),
                                          (MIXER_OUT[layer], 0, _blocks_rows(dw_out))], (), f"mix_l{layer}")
        next_deps = (tok,)
    last_token = scatter_forward(mixer_group, dx)
    unfinished.append(mixer_group)
    small_grads['norm1'] = jnp.stack(dnorm1)
    small_grads['norm2'] = jnp.stack(dnorm2)

    full_shapes = [small_grads[n].shape for n in SMALL_NAMES]
    packed_grads = _pack([small_grads[n] for n in SMALL_NAMES]) + last_token[0, 0]
    all_parts, = _all_gather([packed_grads], name="gather_small_grads")
    summed_buf = _sum_parts(all_parts, name="sum_small_grads")
    for g in unfinished:
        scatter_finish(g, summed_buf)
    for n in ('mlp_w_up', 'mlp_w_down'):
        outs[n] = tuple(jnp.stack([mlp_res[n][layer][k] for layer in range(DEPTH)]) for k in range(4))
    summed = _unpack(summed_buf, full_shapes)
    local_g = []
    for n, g in zip(SMALL_NAMES, summed):
        if SMALL[n]:
            width = p[n].shape[-1]
            g = lax.dynamic_slice_in_dim(g, dev * width, width, axis=g.ndim - 1)
        local_g.append(g.reshape(p[n].shape))
    res = _adamw(_pack([p[n] for n in SMALL_NAMES]), _pack([p['m_' + n] for n in SMALL_NAMES]),
                 _pack([p['v_' + n] for n in SMALL_NAMES]), [_pack(local_g)], name="adamw_small")
    local_shapes = [p[n].shape for n in SMALL_NAMES]
    unpacked = [_unpack(r, local_shapes) for r in res]
    for i, n in enumerate(SMALL_NAMES):
        outs[n] = tuple(unpacked[k][i] for k in range(4))

    result = [loss, dx[None]]
    for k in range(4):
        result += [outs[n][k] for n in WEIGHT_NAMES]
    return tuple(result)


def kernel(x, norm1, norm2, final_norm, ret_w_in, ret_gn_gain, ret_w_out, gdn_w_in, gdn_conv_w, gdn_a_log, gdn_dt_bias, gdn_norm_gain, gdn_w_out, gla_w_in, gla_w_gate_up, gla_gate_bias, gla_norm_gain, gla_w_out, lru_w_in, lru_conv_w, lru_conv_b, lru_w_rgate, lru_b_rgate, lru_w_igate, lru_b_igate, lru_lambda, lru_w_out, mlp_w_up, mlp_w_down, loss_target, m_norm1, m_norm2, m_final_norm, m_ret_w_in, m_ret_gn_gain, m_ret_w_out, m_gdn_w_in, m_gdn_conv_w, m_gdn_a_log, m_gdn_dt_bias, m_gdn_norm_gain, m_gdn_w_out, m_gla_w_in, m_gla_w_gate_up, m_gla_gate_bias, m_gla_norm_gain, m_gla_w_out, m_lru_w_in, m_lru_conv_w, m_lru_conv_b, m_lru_w_rgate, m_lru_b_rgate, m_lru_w_igate, m_lru_b_igate, m_lru_lambda, m_lru_w_out, m_mlp_w_up, m_mlp_w_down, v_norm1, v_norm2, v_final_norm, v_ret_w_in, v_ret_gn_gain, v_ret_w_out, v_gdn_w_in, v_gdn_conv_w, v_gdn_a_log, v_gdn_dt_bias, v_gdn_norm_gain, v_gdn_w_out, v_gla_w_in, v_gla_w_gate_up, v_gla_gate_bias, v_gla_norm_gain, v_gla_w_out, v_lru_w_in, v_lru_conv_w, v_lru_conv_b, v_lru_w_rgate, v_lru_b_rgate, v_lru_w_igate, v_lru_b_igate, v_lru_lambda, v_lru_w_out, v_mlp_w_up, v_mlp_w_down):
    return _step(x, norm1, norm2, final_norm, ret_w_in, ret_gn_gain, ret_w_out, gdn_w_in, gdn_conv_w, gdn_a_log, gdn_dt_bias, gdn_norm_gain, gdn_w_out, gla_w_in, gla_w_gate_up, gla_gate_bias, gla_norm_gain, gla_w_out, lru_w_in, lru_conv_w, lru_conv_b, lru_w_rgate, lru_b_rgate, lru_w_igate, lru_b_igate, lru_lambda, lru_w_out, mlp_w_up, mlp_w_down, loss_target, m_norm1, m_norm2, m_final_norm, m_ret_w_in, m_ret_gn_gain, m_ret_w_out, m_gdn_w_in, m_gdn_conv_w, m_gdn_a_log, m_gdn_dt_bias, m_gdn_norm_gain, m_gdn_w_out, m_gla_w_in, m_gla_w_gate_up, m_gla_gate_bias, m_gla_norm_gain, m_gla_w_out, m_lru_w_in, m_lru_conv_w, m_lru_conv_b, m_lru_w_rgate, m_lru_b_rgate, m_lru_w_igate, m_lru_b_igate, m_lru_lambda, m_lru_w_out, m_mlp_w_up, m_mlp_w_down, v_norm1, v_norm2, v_final_norm, v_ret_w_in, v_ret_gn_gain, v_ret_w_out, v_gdn_w_in, v_gdn_conv_w, v_gdn_a_log, v_gdn_dt_bias, v_gdn_norm_gain, v_gdn_w_out, v_gla_w_in, v_gla_w_gate_up, v_gla_gate_bias, v_gla_norm_gain, v_gla_w_out, v_lru_w_in, v_lru_conv_w, v_lru_conv_b, v_lru_w_rgate, v_lru_b_rgate, v_lru_w_igate, v_lru_b_igate, v_lru_lambda, v_lru_w_out, v_mlp_w_up, v_mlp_w_down)
```

```python
import functools
import math

import numpy as np
import jax
import jax.numpy as jnp
from jax import lax
from jax.experimental import pallas as pl
from jax.experimental.pallas import tpu as pltpu

F32 = jnp.float32
BF16 = jnp.bfloat16

D_MODEL = 2048
DEPTH = 4
CHUNK = 64
D_FF = 4 * D_MODEL
NORM_EPS = 1e-6
N_DEV = 8

RET_HEADS, RET_DK, RET_DV = 8, 256, 512
GDN_HEADS, GDN_DK, GDN_DV = 16, 128, 128
GDN_HB = 8
GDN_QKV = GDN_HEADS * (2 * GDN_DK + GDN_DV)
CONV_WIDTH = 4
GLA_HEADS, GLA_DK, GLA_DV = 4, 256, 512
GLA_GATE_RANK = 16
GLA_TAU = 16.0
LRU_WIDTH, LRU_BLOCKS, LRU_BLOCK = 2048, 16, 128
LRU_C = 8.0
ROPE_BASE = 10000.0

ADAM_LR, ADAM_B1, ADAM_B2, ADAM_EPS, ADAM_WD, ADAM_STEP = 0.001, 0.9, 0.999, 1e-08, 0.01, 10

LANE = 128
VMEM_LIMIT = 48 * 1024 * 1024

NN = (((1,), (0,)), ((), ()))
NT = (((1,), (1,)), ((), ()))
TN = (((0,), (0,)), ((), ()))


def _params(sem=None):
    return pltpu.CompilerParams(dimension_semantics=sem, vmem_limit_bytes=VMEM_LIMIT)


def _bdot(a, b, dn=NN):
    return lax.dot_general(a.astype(BF16), b.astype(BF16), dn, preferred_element_type=F32)


def _split(x):
    hi = x.astype(BF16)
    lo = (x - hi.astype(F32)).astype(BF16)
    return hi, lo


def _fdot(a, b, dn=NN):
    a1, a2 = _split(a)
    b1, b2 = _split(b)
    d = functools.partial(lax.dot_general, dimension_numbers=dn, preferred_element_type=F32)
    return d(a1, b1) + (d(a1, b2) + d(a2, b1))


def _sigmoid(x):
    return 1.0 / (1.0 + jnp.exp(-x))


def _softplus(x):
    return jnp.maximum(x, 0.0) + jnp.log(1.0 + jnp.exp(-jnp.abs(x)))


def _silu(x):
    return x * _sigmoid(x)


def _dsilu(x):
    s = _sigmoid(x)
    return s * (1.0 + x * (1.0 - s))


GELU_C = math.sqrt(2.0 / math.pi)


def _gelu(x):
    return 0.5 * x * (1.0 + jnp.tanh(GELU_C * (x + 0.044715 * x * x * x)))


def _dgelu(x):
    t = jnp.tanh(GELU_C * (x + 0.044715 * x * x * x))
    return 0.5 * (1.0 + t) + 0.5 * x * (1.0 - t * t) * GELU_C * (1.0 + 3.0 * 0.044715 * x * x)


def _expm1(x):
    poly = x * (1.0 + x * 0.5 * (1.0 + x * (1.0 / 3.0) * (1.0 + x * 0.25 * (1.0 + x * 0.2))))
    return jnp.where(jnp.abs(x) < 0.05, poly, jnp.exp(x) - 1.0)


def _iota2(shape, axis):
    return lax.broadcasted_iota(jnp.int32, shape, axis)


def _col_to_row(col):
    n = col.shape[0]
    eye = _iota2((n, n), 0) == _iota2((n, n), 1)
    return jnp.sum(jnp.where(eye, col, 0.0), axis=0, keepdims=True)


def _row_to_col(row):
    n = row.shape[1]
    eye = _iota2((n, n), 0) == _iota2((n, n), 1)
    return jnp.sum(jnp.where(eye, row, 0.0), axis=1, keepdims=True)


def _pick_row(x, r):
    rows = _iota2(x.shape, 0)
    return jnp.sum(jnp.where(rows == r, x, 0.0), axis=0, keepdims=True)


def _shift_down(x, s):
    if s == 0:
        return x
    y = pltpu.roll(x, s, 0)
    return jnp.where(_iota2(x.shape, 0) < s, 0.0, y)


def _shift_up(x, s):
    if s == 0:
        return x
    n = x.shape[0]
    y = pltpu.roll(x, n - s, 0)
    return jnp.where(_iota2(x.shape, 0) >= n - s, 0.0, y)


def _tile(dim, prefs):
    for p in prefs:
        if dim % p == 0:
            return p
    return dim


TOKEN_SHAPE = (8, LANE)


def _dep_specs(deps):
    return [pl.BlockSpec(TOKEN_SHAPE, lambda *_: (0, 0)) for _ in deps]


def _matmul(a, b, *, ta=False, tb=False, epi="none", extra=None, out_dtype=F32, name, deps=()):
    if ta:
        K, M = a.shape
    else:
        M, K = a.shape
    if tb:
        N, K2 = b.shape
    else:
        K2, N = b.shape
    assert K == K2, (a.shape, b.shape, ta, tb)
    tm = _tile(M, (1024, 512, 256, 128))
    tn = _tile(N, (512, 256, 128))
    tk = _tile(K, (2048, 1024, 512, 256, 128))
    nk = K // tk
    dn = (((0 if ta else 1,), (1 if tb else 0,)), ((), ()))
    n_extra = 0 if extra is None else 1
    n_out = 2 if epi == "relu2" else 1

    def body(*refs):
        a_ref, b_ref = refs[0], refs[1]
        e_ref = refs[2] if n_extra else None
        outs = refs[2 + n_extra + len(deps):2 + n_extra + len(deps) + n_out]

        def finish(r):
            if epi == "none":
                outs[0][...] = r.astype(outs[0].dtype)
            elif epi == "add":
                outs[0][...] = (r + e_ref[...]).astype(outs[0].dtype)
            elif epi == "relu2":
                outs[0][...] = r
                p = jnp.maximum(r, 0.0)
                outs[1][...] = (p * p).astype(outs[1].dtype)
            elif epi == "drelu2":
                outs[0][...] = (r * 2.0 * jnp.maximum(e_ref[...], 0.0)).astype(outs[0].dtype)

        def product():
            return lax.dot_general(a_ref[...].astype(BF16), b_ref[...].astype(BF16), dn, preferred_element_type=F32)

        if nk == 1:
            finish(product())
            return
        acc = refs[-1]
        k = pl.program_id(2)

        @pl.when(k == 0)
        def _():
            acc[...] = jnp.zeros_like(acc)

        acc[...] += product()

        @pl.when(k == nk - 1)
        def _():
            finish(acc[...])

    a_spec = pl.BlockSpec((tk, tm), lambda i, j, k: (k, i)) if ta else pl.BlockSpec((tm, tk), lambda i, j, k: (i, k))
    b_spec = pl.BlockSpec((tn, tk), lambda i, j, k: (j, k)) if tb else pl.BlockSpec((tk, tn), lambda i, j, k: (k, j))
    o_spec = pl.BlockSpec((tm, tn), lambda i, j, k: (i, j))
    in_specs = [a_spec, b_spec] + ([o_spec] if n_extra else []) + _dep_specs(deps)
    if epi == "relu2":
        out_shape = (jax.ShapeDtypeStruct((M, N), F32), jax.ShapeDtypeStruct((M, N), BF16))
        out_specs = (o_spec, o_spec)
    else:
        out_shape = jax.ShapeDtypeStruct((M, N), out_dtype)
        out_specs = o_spec
    args = (a, b) + ((extra,) if n_extra else ()) + tuple(deps)
    return pl.pallas_call(
        body, name=name, grid=(M // tm, N // tn, nk), in_specs=in_specs, out_specs=out_specs,
        out_shape=out_shape, scratch_shapes=[pltpu.VMEM((tm, tn), F32)] if nk > 1 else [],
        compiler_params=_params(("parallel", "parallel", "arbitrary")),
    )(*args)


ROW_BLOCK = 256


def _rmsnorm_fwd(x, g, *, name, deps=()):
    T, D = x.shape
    tr = _tile(T, (ROW_BLOCK, 128, 64))

    def body(x_ref, g_ref, *rest):
        o_ref = rest[-1]
        xv = x_ref[...]
        r = lax.rsqrt(jnp.mean(xv * xv, axis=-1, keepdims=True) + NORM_EPS)
        o_ref[...] = (xv * r * g_ref[...]).astype(o_ref.dtype)

    return pl.pallas_call(
        body, name=name, grid=(T // tr,),
        in_specs=[pl.BlockSpec((tr, D), lambda i: (i, 0)), pl.BlockSpec((1, D), lambda i: (0, 0))] + _dep_specs(deps),
        out_specs=pl.BlockSpec((tr, D), lambda i: (i, 0)),
        out_shape=jax.ShapeDtypeStruct((T, D), BF16), compiler_params=_params(("parallel",)),
    )(x, g, *deps)


def _rmsnorm_bwd(x, g, dy, dres, *, name, deps=()):
    T, D = x.shape
    tr = _tile(T, (ROW_BLOCK, 128, 64))

    def body(x_ref, g_ref, dy_ref, dres_ref, *rest):
        dx_ref, dxb_ref, dg_ref = rest[len(deps):]
        i = pl.program_id(0)
        xv = x_ref[...]
        r = lax.rsqrt(jnp.mean(xv * xv, axis=-1, keepdims=True) + NORM_EPS)
        xh = xv * r
        dyv = dy_ref[...].astype(F32)
        dxh = dyv * g_ref[...]
        dx = dres_ref[...] + r * (dxh - xh * jnp.mean(dxh * xh, axis=-1, keepdims=True))
        dx_ref[...] = dx
        dxb_ref[...] = dx.astype(dxb_ref.dtype)

        @pl.when(i == 0)
        def _():
            dg_ref[...] = jnp.zeros_like(dg_ref)

        dg_ref[...] += jnp.sum(dyv * xh, axis=0, keepdims=True)

    blk = pl.BlockSpec((tr, D), lambda i: (i, 0))
    vec = pl.BlockSpec((1, D), lambda i: (0, 0))
    return pl.pallas_call(
        body, name=name, grid=(T // tr,), in_specs=[blk, vec, blk, blk] + _dep_specs(deps), out_specs=(blk, blk, vec),
        out_shape=(jax.ShapeDtypeStruct((T, D), F32), jax.ShapeDtypeStruct((T, D), BF16),
                   jax.ShapeDtypeStruct((1, D), F32)),
        compiler_params=_params(("arbitrary",)),
    )(x, g, dy, dres, *deps)


def _final_loss(x, g, target, *, name):
    T, D = x.shape
    tr = _tile(T, (ROW_BLOCK, 128, 64))

    def body(x_ref, g_ref, t_ref, dy_ref, l_ref):
        i = pl.program_id(0)
        xv = x_ref[...]
        r = lax.rsqrt(jnp.mean(xv * xv, axis=-1, keepdims=True) + NORM_EPS)
        err = xv * r * g_ref[...] - t_ref[...]
        dy_ref[...] = err * (1.0 / D)

        @pl.when(i == 0)
        def _():
            l_ref[...] = jnp.zeros_like(l_ref)

        part = 0.5 * jnp.sum(jnp.mean(err * err, axis=-1, keepdims=True), axis=0, keepdims=True)
        l_ref[...] += jnp.broadcast_to(part, l_ref.shape)

    blk = pl.BlockSpec((tr, D), lambda i: (i, 0))
    vec = pl.BlockSpec((1, D), lambda i: (0, 0))
    return pl.pallas_call(
        body, name=name, grid=(T // tr,), in_specs=[blk, vec, blk],
        out_specs=(blk, pl.BlockSpec((1, LANE), lambda i: (0, 0))),
        out_shape=(jax.ShapeDtypeStruct((T, D), F32), jax.ShapeDtypeStruct((1, LANE), F32)),
        compiler_params=_params(("arbitrary",)),
    )(x, g, target)


def _colsum(x, *, name):
    T, C = x.shape
    tc = _tile(C, (512, 256, 128))

    def body(x_ref, o_ref):
        o_ref[...] = jnp.sum(x_ref[...], axis=0, keepdims=True)

    return pl.pallas_call(
        body, name=name, grid=(C // tc,), in_specs=[pl.BlockSpec((T, tc), lambda j: (0, j))],
        out_specs=pl.BlockSpec((1, tc), lambda j: (0, j)),
        out_shape=jax.ShapeDtypeStruct((1, C), F32), compiler_params=_params(("parallel",)),
    )(x)


ADAM_C1 = 1.0 - ADAM_B1 ** ADAM_STEP
ADAM_C2 = 1.0 - ADAM_B2 ** ADAM_STEP


def _adamw(w, m, v, grads, *, name):
    R, C = w.shape
    tr = _tile(R, (256, 128, 64, 32, 16, 8))
    n_g = len(grads)

    def body(*refs):
        w_ref, m_ref, v_ref = refs[:3]
        g_refs = refs[3:3 + n_g]
        g_out, d_out, m_out, v_out = refs[3 + n_g:]
        g = g_refs[0][...].astype(F32)
        for r in g_refs[1:]:
            g = g + r[...].astype(F32)
        mn = ADAM_B1 * m_ref[...] + (1.0 - ADAM_B1) * g
        vn = ADAM_B2 * v_ref[...] + (1.0 - ADAM_B2) * (g * g)
        m_hat = mn / ADAM_C1
        v_hat = vn / ADAM_C2
        g_out[...] = g
        d_out[...] = -ADAM_LR * (m_hat / (jnp.sqrt(v_hat) + ADAM_EPS) + ADAM_WD * w_ref[...])
        m_out[...] = mn
        v_out[...] = vn

    blk = pl.BlockSpec((tr, C), lambda i: (i, 0))
    sh = jax.ShapeDtypeStruct((R, C), F32)
    return pl.pallas_call(
        body, name=name, grid=(R // tr,), in_specs=[blk] * (3 + n_g), out_specs=(blk,) * 4,
        out_shape=(sh,) * 4, compiler_params=_params(("parallel",)),
    )(w, m, v, *grads)


def _sum_parts(parts, *, name, deps=()):
    P, R, C = parts.shape
    tr = _tile(R, (256, 128, 64, 32, 16, 8))

    def body(p_ref, *rest):
        o_ref = rest[-1]
        s = p_ref[0].astype(F32)
        for i in range(1, P):
            s = s + p_ref[i].astype(F32)
        o_ref[...] = s

    return pl.pallas_call(
        body, name=name, grid=(R // tr,),
        in_specs=[pl.BlockSpec((P, tr, C), lambda i: (0, i, 0))] + _dep_specs(deps),
        out_specs=pl.BlockSpec((tr, C), lambda i: (i, 0)),
        out_shape=jax.ShapeDtypeStruct((R, C), F32), compiler_params=_params(("parallel",)),
    )(parts, *deps)


def _ret_tables(T):
    H, C = RET_HEADS, CHUNK
    log_gamma = jnp.log1p(-jnp.exp2(-5.0 - jnp.arange(H, dtype=F32)))
    pos = jnp.arange(C, dtype=F32)
    dist = jnp.abs(pos[:, None] - pos[None, :])
    dm = jnp.exp(log_gamma[:, None, None] * dist)
    qdec = jnp.exp(log_gamma[:, None] * (pos + 1.0))[:, :, None]
    kdec = jnp.exp(log_gamma[:, None] * (C - 1.0 - pos))[:, :, None]
    cdec = jnp.exp(log_gamma * C)[:, None, None]
    inv = ROPE_BASE ** (-jnp.arange(0, RET_DK, 2, dtype=F32) / RET_DK)
    ang = jnp.arange(T, dtype=F32)[:, None] * inv[None, :]
    return dm, qdec, kdec, cdec, jnp.cos(ang), jnp.sin(ang)


def _rot(x, cos, sin):
    h = x.shape[1] // 2
    x1, x2 = x[:, :h], x[:, h:]
    return jnp.concatenate([x1 * cos - x2 * sin, x1 * sin + x2 * cos], axis=1)


def _unrot(dy, cos, sin):
    h = dy.shape[1] // 2
    d1, d2 = dy[:, :h], dy[:, h:]
    return jnp.concatenate([d1 * cos + d2 * sin, d2 * cos - d1 * sin], axis=1)


def _ret_specs(N, rev):
    H, C, DK, DV = RET_HEADS, CHUNK, RET_DK, RET_DV
    cn = (lambda n: N - 1 - n) if rev else (lambda n: n)
    q = pl.BlockSpec((C, DK), lambda h, n: (cn(n), h))
    k = pl.BlockSpec((C, DK), lambda h, n: (cn(n), H + h))
    v = pl.BlockSpec((C, DV), lambda h, n: (cn(n), H + h))
    g = pl.BlockSpec((C, DV), lambda h, n: (cn(n), 2 * H + h))
    cs = pl.BlockSpec((C, DK // 2), lambda h, n: (cn(n), 0))
    dm = pl.BlockSpec((1, C, C), lambda h, n: (h, 0, 0))
    dec = pl.BlockSpec((1, C, 1), lambda h, n: (h, 0, 0))
    cd = pl.BlockSpec((1, 1, 1), lambda h, n: (h, 0, 0))
    gain = pl.BlockSpec((1, 1, DV), lambda h, n: (h, 0, 0))
    st = pl.BlockSpec((1, 1, DK, DV), lambda h, n: (h, cn(n), 0, 0))
    ov = pl.BlockSpec((C, DV), lambda h, n: (cn(n), h))
    ok = pl.BlockSpec((C, DK), lambda h, n: (cn(n), h))
    return q, k, v, g, cs, dm, dec, cd, gain, st, ov, ok


def _ret_fwd(proj, gain, tables, *, name):
    T = proj.shape[0]
    H, C, DK, DV = RET_HEADS, CHUNK, RET_DK, RET_DV
    N = T // C
    dm_t, qdec_t, kdec_t, cdec_t, cos_t, sin_t = tables

    def body(q_ref, k_ref, v_ref, g_ref, cos_ref, sin_ref, dm_ref, qd_ref, kd_ref, cd_ref, gain_ref,
             og_ref, st_ref, s_acc):
        n = pl.program_id(1)

        @pl.when(n == 0)
        def _():
            s_acc[...] = jnp.zeros_like(s_acc)

        cos, sin = cos_ref[...], sin_ref[...]
        qr = _rot(q_ref[...], cos, sin)
        kr = _rot(k_ref[...], cos, sin) * (DK ** -0.5)
        v = v_ref[...]
        sp = s_acc[...]
        st_ref[0, 0] = sp.astype(st_ref.dtype)
        scores = _bdot(qr, kr, NT) * dm_ref[0]
        o = _bdot(scores, v) + _bdot(qr * qd_ref[0], sp)
        s_acc[...] = sp * cd_ref[0] + _bdot(kr * kd_ref[0], v, TN)
        oc = o - jnp.mean(o, axis=-1, keepdims=True)
        rstd = lax.rsqrt(jnp.mean(oc * oc, axis=-1, keepdims=True) + NORM_EPS)
        og_ref[...] = (oc * rstd * gain_ref[0] * _silu(g_ref[...])).astype(og_ref.dtype)

    q, k, v, g, cs, dm, dec, cd, gn, st, ov, _ = _ret_specs(N, False)
    return pl.pallas_call(
        body, name=name, grid=(H, N),
        in_specs=[q, k, v, g, cs, cs, dm, dec, dec, cd, gn], out_specs=(ov, st),
        out_shape=(jax.ShapeDtypeStruct((T, H * DV), BF16), jax.ShapeDtypeStruct((H, N, DK, DV), BF16)),
        scratch_shapes=[pltpu.VMEM((DK, DV), F32)],
        compiler_params=_params(("arbitrary", "arbitrary")),
    )(proj, proj, proj, proj, cos_t, sin_t, dm_t, qdec_t, kdec_t, cdec_t, gain)


def _ret_bwd(proj, gain, tables, states, dog, *, name):
    T = proj.shape[0]
    H, C, DK, DV = RET_HEADS, CHUNK, RET_DK, RET_DV
    N = T // C
    dm_t, qdec_t, kdec_t, cdec_t, cos_t, sin_t = tables

    def body(q_ref, k_ref, v_ref, g_ref, cos_ref, sin_ref, dm_ref, qd_ref, kd_ref, cd_ref, gain_ref,
             st_ref, dog_ref, dq_ref, dk_ref, dv_ref, dg_ref, dgain_ref, ds_acc):
        n = pl.program_id(1)

        @pl.when(n == 0)
        def _():
            ds_acc[...] = jnp.zeros_like(ds_acc)
            dgain_ref[...] = jnp.zeros_like(dgain_ref)

        cos, sin = cos_ref[...], sin_ref[...]
        scale = DK ** -0.5
        qr = _rot(q_ref[...], cos, sin)
        kr = _rot(k_ref[...], cos, sin) * scale
        v = v_ref[...]
        g = g_ref[...]
        sp = st_ref[0, 0]
        dm = dm_ref[0]
        qd, kd = qd_ref[0], kd_ref[0]
        gain_v = gain_ref[0]
        scores = _bdot(qr, kr, NT) * dm
        o = _bdot(scores, v) + _bdot(qr * qd, sp)
        oc = o - jnp.mean(o, axis=-1, keepdims=True)
        rstd = lax.rsqrt(jnp.mean(oc * oc, axis=-1, keepdims=True) + NORM_EPS)
        oh = oc * rstd
        dy = dog_ref[...].astype(F32)
        dg_ref[...] = (dy * oh * gain_v * _dsilu(g)).astype(dg_ref.dtype)
        dnorm = dy * _silu(g)
        dgain_ref[0] += jnp.sum(dnorm * oh, axis=0, keepdims=True)
        doh = dnorm * gain_v
        do = rstd * (doh - jnp.mean(doh, axis=-1, keepdims=True)
                     - oh * jnp.mean(doh * oh, axis=-1, keepdims=True))
        dsn = ds_acc[...]
        dp = _bdot(do, v, NT) * dm
        dqr = _bdot(dp, kr) + _bdot(do, sp, NT) * qd
        dkr = _bdot(dp, qr, TN) + _bdot(v, dsn, NT) * kd
        dv_ref[...] = (_bdot(scores, do, TN) + _bdot(kr * kd, dsn)).astype(dv_ref.dtype)
        ds_acc[...] = dsn * cd_ref[0] + _bdot(qr * qd, do, TN)
        dq_ref[...] = _unrot(dqr, cos, sin).astype(dq_ref.dtype)
        dk_ref[...] = _unrot(dkr * scale, cos, sin).astype(dk_ref.dtype)

    q, k, v, g, cs, dm, dec, cd, gn, st, ov, ok = _ret_specs(N, True)
    return pl.pallas_call(
        body, name=name, grid=(H, N),
        in_specs=[q, k, v, g, cs, cs, dm, dec, dec, cd, gn, st, ov],
        out_specs=(ok, ok, ov, ov, gn),
        out_shape=(jax.ShapeDtypeStruct((T, H * DK), BF16), jax.ShapeDtypeStruct((T, H * DK), BF16),
                   jax.ShapeDtypeStruct((T, H * DV), BF16), jax.ShapeDtypeStruct((T, H * DV), BF16),
                   jax.ShapeDtypeStruct((H, 1, DV), F32)),
        scratch_shapes=[pltpu.VMEM((DK, DV), F32)],
        compiler_params=_params(("arbitrary", "arbitrary")),
    )(proj, proj, proj, proj, cos_t, sin_t, dm_t, qdec_t, kdec_t, cdec_t, gain, states, dog)


def _gla_specs(N, rev):
    H, C, DK, DV = GLA_HEADS, CHUNK, GLA_DK, GLA_DV
    cn = (lambda n: N - 1 - n) if rev else (lambda n: n)
    q = pl.BlockSpec((C, DK), lambda h, n: (cn(n), h))
    k = pl.BlockSpec((C, DK), lambda h, n: (cn(n), H + h))
    v = pl.BlockSpec((C, DV), lambda h, n: (cn(n), H + h))
    r = pl.BlockSpec((C, DV), lambda h, n: (cn(n), 2 * H + h))
    bias = pl.BlockSpec((1, DK), lambda h, n: (0, h))
    gain = pl.BlockSpec((1, 1, DV), lambda h, n: (h, 0, 0))
    st = pl.BlockSpec((1, 1, DV, DK), lambda h, n: (h, cn(n), 0, 0))
    ov = pl.BlockSpec((C, DV), lambda h, n: (cn(n), h))
    return q, k, v, r, bias, gain, st, ov


def _gla_chunk(q_ref, k_ref, v_ref, gl_ref, bias_ref):
    C, DK = q_ref.shape
    gl = gl_ref[...] + bias_ref[...]
    la = -_softplus(-gl) * (1.0 / GLA_TAU)
    lower = _iota2((C, C), 0) >= _iota2((C, C), 1)
    cum = _fdot(jnp.where(lower, 1.0, 0.0), la)
    ref = _pick_row(cum, C // 2 - 1)
    clast = _pick_row(cum, C - 1)
    fw, bw = jnp.exp(cum - ref), jnp.exp(ref - cum)
    qs = q_ref[...] * (DK ** -0.5)
    k = k_ref[...]
    s_lo = _bdot(qs * fw, k * bw, NT)
    s_up = _bdot(qs * bw, k * fw, NT)
    scores = jnp.where(lower, s_lo, s_up)
    return gl, cum, clast, fw, bw, qs, k, v_ref[...], scores, lower


def _gla_fwd(proj, glogit, bias, gain, *, name):
    T = proj.shape[0]
    H, C, DK, DV = GLA_HEADS, CHUNK, GLA_DK, GLA_DV
    N = T // C

    def body(q_ref, k_ref, v_ref, r_ref, gl_ref, bias_ref, gain_ref, og_ref, st_ref, s_acc):
        n = pl.program_id(1)

        @pl.when(n == 0)
        def _():
            s_acc[...] = jnp.zeros_like(s_acc)

        gl, cum, clast, fw, bw, qs, k, v, scores, lower = _gla_chunk(q_ref, k_ref, v_ref, gl_ref, bias_ref)
        sp = s_acc[...]
        st_ref[0, 0] = sp
        o = _bdot(scores, v) + _bdot(qs * jnp.exp(cum), sp, NT)
        s_acc[...] = sp * jnp.exp(clast) + _bdot(v, k * jnp.exp(clast - cum), TN)
        rstd = lax.rsqrt(jnp.mean(o * o, axis=-1, keepdims=True) + NORM_EPS)
        og_ref[...] = (o * rstd * gain_ref[0] * _silu(r_ref[...])).astype(og_ref.dtype)

    q, k, v, r, bias_s, gn, st, ov = _gla_specs(N, False)
    return pl.pallas_call(
        body, name=name, grid=(H, N), in_specs=[q, k, v, r, q, bias_s, gn], out_specs=(ov, st),
        out_shape=(jax.ShapeDtypeStruct((T, H * DV), BF16), jax.ShapeDtypeStruct((H, N, DV, DK), F32)),
        scratch_shapes=[pltpu.VMEM((DV, DK), F32)],
        compiler_params=_params(("arbitrary", "arbitrary")),
    )(proj, proj, proj, proj, glogit, bias, gain)


def _gla_bwd(proj, glogit, bias, gain, states, dog, *, name):
    T = proj.shape[0]
    H, C, DK, DV = GLA_HEADS, CHUNK, GLA_DK, GLA_DV
    N = T // C

    def body(q_ref, k_ref, v_ref, r_ref, gl_ref, bias_ref, gain_ref, st_ref, dog_ref,
             dq_ref, dk_ref, dv_ref, dr_ref, dgl_ref, dgain_ref, ds_acc):
        n = pl.program_id(1)

        @pl.when(n == 0)
        def _():
            ds_acc[...] = jnp.zeros_like(ds_acc)
            dgain_ref[...] = jnp.zeros_like(dgain_ref)

        gl, cum, clast, fw, bw, qs, k, v, scores, lower = _gla_chunk(q_ref, k_ref, v_ref, gl_ref, bias_ref)
        sp = st_ref[0, 0]
        ecum, e2, cdec = jnp.exp(cum), jnp.exp(clast - cum), jnp.exp(clast)
        q_in, k_end = qs * ecum, k * e2
        o = _bdot(scores, v) + _bdot(q_in, sp, NT)
        rstd = lax.rsqrt(jnp.mean(o * o, axis=-1, keepdims=True) + NORM_EPS)
        oh = o * rstd
        r = r_ref[...]
        gain_v = gain_ref[0]
        dy = dog_ref[...].astype(F32)
        dr_ref[...] = (dy * oh * gain_v * _dsilu(r)).astype(dr_ref.dtype)
        dnorm = dy * _silu(r)
        dgain_ref[0] += jnp.sum(dnorm * oh, axis=0, keepdims=True)
        doh = dnorm * gain_v
        do = rstd * (doh - oh * jnp.mean(doh * oh, axis=-1, keepdims=True))
        dsn = ds_acc[...]
        dq_in = _bdot(do, sp)
        dk_end = _bdot(v, dsn)
        dv_ref[...] = (_bdot(k_end, dsn, NT) + _bdot(scores, do, TN)).astype(dv_ref.dtype)
        dcdec = jnp.sum(dsn * sp, axis=0, keepdims=True)
        ds_acc[...] = dsn * cdec + _bdot(do, q_in, TN)
        dsc = _bdot(do, v, NT)
        ds_lo = jnp.where(lower, dsc, 0.0)
        ds_up = jnp.where(lower, 0.0, dsc)
        qf, kb, qb, kf = qs * fw, k * bw, qs * bw, k * fw
        dqf, dkb = _bdot(ds_lo, kb), _bdot(ds_lo, qf, TN)
        dqb, dkf = _bdot(ds_up, kf), _bdot(ds_up, qb, TN)
        dq_ref[...] = ((dqf * fw + dqb * bw + dq_in * ecum) * (DK ** -0.5)).astype(dq_ref.dtype)
        dk_ref[...] = (dkb * bw + dkf * fw + dk_end * e2).astype(dk_ref.dtype)
        dz = (dqf * qs + dkf * k) * fw - (dqb * qs + dkb * k) * bw
        kk = dk_end * k_end
        dcum = dz + dq_in * q_in - kk
        rows = _iota2((C, DK), 0)
        dcum = dcum + jnp.where(rows == C // 2 - 1, -jnp.sum(dz, axis=0, keepdims=True), 0.0)
        dcum = dcum + jnp.where(rows == C - 1, jnp.sum(kk, axis=0, keepdims=True) + dcdec * cdec, 0.0)
        upper = _iota2((C, C), 0) <= _iota2((C, C), 1)
        dla = _fdot(jnp.where(upper, 1.0, 0.0), dcum)
        dgl_ref[...] = dla * (1.0 / GLA_TAU) * _sigmoid(-gl)

    q, k, v, r, bias_s, gn, st, ov = _gla_specs(N, True)
    return pl.pallas_call(
        body, name=name, grid=(H, N), in_specs=[q, k, v, r, q, bias_s, gn, st, ov],
        out_specs=(q, q, ov, ov, q, gn),
        out_shape=(jax.ShapeDtypeStruct((T, H * DK), BF16), jax.ShapeDtypeStruct((T, H * DK), BF16),
                   jax.ShapeDtypeStruct((T, H * DV), BF16), jax.ShapeDtypeStruct((T, H * DV), BF16),
                   jax.ShapeDtypeStruct((T, H * DK), F32), jax.ShapeDtypeStruct((H, 1, DV), F32)),
        scratch_shapes=[pltpu.VMEM((DV, DK), F32)],
        compiler_params=_params(("arbitrary", "arbitrary")),
    )(proj, proj, proj, proj, glogit, bias, gain, states, dog)


def _conv(xv, w_ref):
    out = _shift_down(xv, CONV_WIDTH - 1) * w_ref[0:1, :]
    for tap in range(1, CONV_WIDTH):
        out = out + _shift_down(xv, CONV_WIDTH - 1 - tap) * w_ref[tap:tap + 1, :]
    return out


def _conv_bwd(xv, w_ref, dpre, dw_ref):
    dx = None
    for tap in range(CONV_WIDTH):
        s = CONV_WIDTH - 1 - tap
        t = _shift_up(dpre, s) * w_ref[tap:tap + 1, :]
        dx = t if dx is None else dx + t
        dw_ref[tap:tap + 1, :] = jnp.sum(dpre * _shift_down(xv, s), axis=0, keepdims=True)
    return dx


CONV_COLS = 256


def _conv_silu_fwd(x, w, *, name):
    T = x.shape[0]
    n = w.shape[1]

    def body(x_ref, w_ref, o_ref):
        o_ref[...] = _silu(_conv(x_ref[...], w_ref))

    return pl.pallas_call(
        body, name=name, grid=(n // CONV_COLS,),
        in_specs=[pl.BlockSpec((T, CONV_COLS), lambda j: (0, j)), pl.BlockSpec((CONV_WIDTH, CONV_COLS), lambda j: (0, j))],
        out_specs=pl.BlockSpec((T, CONV_COLS), lambda j: (0, j)),
        out_shape=jax.ShapeDtypeStruct((T, n), F32), compiler_params=_params(("parallel",)),
    )(x, w)


def _conv_silu_bwd(x, w, dact, *, name):
    T = x.shape[0]
    n = w.shape[1]

    def body(x_ref, w_ref, da_ref, dx_ref, dw_ref):
        xv = x_ref[...]
        dpre = da_ref[...] * _dsilu(_conv(xv, w_ref))
        dx_ref[...] = _conv_bwd(xv, w_ref, dpre, dw_ref).astype(dx_ref.dtype)

    blk = pl.BlockSpec((T, CONV_COLS), lambda j: (0, j))
    wb = pl.BlockSpec((CONV_WIDTH, CONV_COLS), lambda j: (0, j))
    return pl.pallas_call(
        body, name=name, grid=(n // CONV_COLS,), in_specs=[blk, wb, blk], out_specs=(blk, wb),
        out_shape=(jax.ShapeDtypeStruct((T, n), BF16), jax.ShapeDtypeStruct((CONV_WIDTH, n), F32)),
        compiler_params=_params(("parallel",)),
    )(x, w, dact)


def _interleave(gens):
    results = [None] * len(gens)
    live = list(range(len(gens)))
    while live:
        for i in list(live):
            try:
                next(gens[i])
            except StopIteration as done:
                results[i] = done.value
                live.remove(i)
    return results


def _unit_lower_inverse(a):
    n = a.shape[0]
    eye = jnp.where(_iota2((n, n), 0) == _iota2((n, n), 1), 1.0, 0.0)
    p = -a
    t = eye + p
    for _ in range(5):
        p = _fdot(p, p)
        yield
        t = t + _fdot(t, p)
        yield
    return t


def _gdn_specs(N, rev):
    H, C, DK, DV = GDN_HEADS, CHUNK, GDN_DK, GDN_DV
    cn = (lambda n: N - 1 - n) if rev else (lambda n: n)
    HB, G = GDN_HB, H // GDN_HB
    q = pl.BlockSpec((C, HB * DK), lambda h, n: (cn(n), h))
    k = pl.BlockSpec((C, HB * DK), lambda h, n: (cn(n), G + h))
    v = pl.BlockSpec((C, HB * DV), lambda h, n: (cn(n), 2 * G + h))
    z = pl.BlockSpec((C, HB * DV), lambda h, n: (cn(n), 3 * G + h))
    gates = pl.BlockSpec((C, LANE), lambda h, n: (cn(n), 0))
    sc = pl.BlockSpec((HB, 1, 1), lambda h, n: (h, 0, 0))
    gain = pl.BlockSpec((1, DV), lambda h, n: (0, 0))
    st = pl.BlockSpec((HB, 1, DK, DV), lambda h, n: (h, cn(n), 0, 0))
    return q, k, v, z, gates, sc, gain, st


def _gdn_chunk(q_ref, k_ref, v_ref, gates_ref, alog_ref, dtb_ref, h):
    H, C, DK, DV = GDN_HEADS, CHUNK, GDN_DK, GDN_DV
    gates = gates_ref[...]
    lane = _iota2(gates.shape, 1)
    bl = jnp.sum(jnp.where(lane == h, gates, 0.0), axis=1, keepdims=True)
    al = jnp.sum(jnp.where(lane == H + h, gates, 0.0), axis=1, keepdims=True)
    beta = _sigmoid(bl)
    ea = jnp.exp(alog_ref[...])
    xs = al + dtb_ref[...]
    la = -ea * _softplus(xs)
    ii, jj = _iota2((C, C), 0), _iota2((C, C), 1)
    strict = ii > jj
    cum_col = jnp.sum(jnp.where(ii >= jj, _col_to_row(la), 0.0), axis=1, keepdims=True)
    cum_row = jnp.sum(jnp.where(ii <= jj, la, 0.0), axis=0, keepdims=True)
    q, k, v = q_ref[...], k_ref[...], v_ref[...]
    rq = lax.rsqrt(jnp.sum(q * q, axis=-1, keepdims=True) + NORM_EPS)
    rk = lax.rsqrt(jnp.sum(k * k, axis=-1, keepdims=True) + NORM_EPS)
    qn = q * rq * (DK ** -0.5)
    kn = k * rk
    rel = jnp.where(strict, jnp.exp(jnp.where(strict, cum_col - cum_row, 0.0)), 0.0)
    rg = rel * _bdot(kn, kn, NT)
    yield
    a = beta * rg
    tm = yield from _unit_lower_inverse(a)
    e_col = jnp.exp(cum_col)
    clast = _pick_row(cum_col, C - 1)
    rhs = jnp.concatenate([beta * v, (beta * e_col) * kn], axis=1)
    sol = _fdot(tm, rhs)
    yield
    u, w = sol[:, :DV], sol[:, DV:]
    dd = jnp.exp(clast - cum_col)
    ke = kn * dd
    g = jnp.exp(clast)
    eye = jnp.where(_iota2((DK, DK), 0) == _iota2((DK, DK), 1), 1.0, 0.0)
    trans = g * eye - _bdot(ke, w, TN)
    inject = _bdot(ke, u, TN)
    yield
    return dict(beta=beta, ea=ea, xs=xs, la=la, strict=strict, ii=ii, jj=jj, q=q, k=k, v=v, rq=rq, rk=rk,
                qn=qn, kn=kn, rel=rel, rg=rg, a=a, tm=tm, e_col=e_col, sol=sol, u=u, w=w, dd=dd, ke=ke,
                g=g, eye=eye, trans=trans, inject=inject)


def _gdn_fwd(act, proj, gates, a_log, dt_bias, gain, *, name):
    T = act.shape[0]
    H, C, DK, DV = GDN_HEADS, CHUNK, GDN_DK, GDN_DV
    N = T // C

    def body(q_ref, k_ref, v_ref, z_ref, gates_ref, alog_ref, dtb_ref, gain_ref, og_ref, st_ref, s_acc):
        hg, n = pl.program_id(0), pl.program_id(1)

        @pl.when(n == 0)
        def _():
            s_acc[...] = jnp.zeros_like(s_acc)

        def head(hb):
            cols = pl.ds(hb * DK, DK)
            c = yield from _gdn_chunk(q_ref.at[:, cols], k_ref.at[:, cols], v_ref.at[:, cols], gates_ref,
                                      alog_ref.at[hb], dtb_ref.at[hb], hg * GDN_HB + hb)
            sp = s_acc[hb]
            st_ref[hb, 0] = sp
            snew = _bdot(c["trans"], sp) + c["inject"]
            yield
            s_acc[hb] = snew
            o = _bdot(c["qn"], snew)
            yield
            rstd = lax.rsqrt(jnp.mean(o * o, axis=-1, keepdims=True) + NORM_EPS)
            og_ref[:, cols] = (o * rstd * gain_ref[...] * _silu(z_ref[:, cols])).astype(og_ref.dtype)

        _interleave([head(hb) for hb in range(GDN_HB)])

    q, k, v, z, gt, sc, gn, st = _gdn_specs(N, False)
    return pl.pallas_call(
        body, name=name, grid=(H // GDN_HB, N), in_specs=[q, k, v, z, gt, sc, sc, gn], out_specs=(q, st),
        out_shape=(jax.ShapeDtypeStruct((T, H * DV), BF16), jax.ShapeDtypeStruct((H, N, DK, DV), F32)),
        scratch_shapes=[pltpu.VMEM((GDN_HB, DK, DV), F32)],
        compiler_params=_params(("arbitrary", "arbitrary")),
    )(act, act, act, proj, gates, a_log, dt_bias, gain)


def _gdn_bwd(act, proj, gates, a_log, dt_bias, gain, states, dog, *, name):
    T = act.shape[0]
    H, C, DK, DV = GDN_HEADS, CHUNK, GDN_DK, GDN_DV
    N = T // C

    def rsum(x):
        return jnp.sum(x, axis=1, keepdims=True)

    def body(q_ref, k_ref, v_ref, z_ref, gates_ref, alog_ref, dtb_ref, gain_ref, st_ref, dog_ref,
             dq_ref, dk_ref, dv_ref, dz_ref, dgates_ref, dscal_ref, dgain_ref, ds_acc):
        hg, n = pl.program_id(0), pl.program_id(1)

        @pl.when(n == 0)
        def _():
            ds_acc[...] = jnp.zeros_like(ds_acc)
            dgain_ref[...] = jnp.zeros_like(dgain_ref)
            dscal_ref[...] = jnp.zeros_like(dscal_ref)

        _interleave([one_head(hb, hg * GDN_HB + hb, q_ref, k_ref, v_ref, z_ref, gates_ref, alog_ref, dtb_ref, gain_ref,
                              st_ref, dog_ref, dq_ref, dk_ref, dv_ref, dz_ref, dgates_ref, dscal_ref, dgain_ref, ds_acc)
                     for hb in range(GDN_HB)])

    def one_head(hb, h, q_ref, k_ref, v_ref, z_ref, gates_ref, alog_ref, dtb_ref, gain_ref, st_ref, dog_ref,
                 dq_ref, dk_ref, dv_ref, dz_ref, dgates_ref, dscal_ref, dgain_ref, ds_acc):
        cols = pl.ds(hb * DK, DK)
        c = yield from _gdn_chunk(q_ref.at[:, cols], k_ref.at[:, cols], v_ref.at[:, cols], gates_ref,
                                  alog_ref.at[hb], dtb_ref.at[hb], h)
        beta, kn, qn, v, ke, u, w, dd, e_col = c["beta"], c["kn"], c["qn"], c["v"], c["ke"], c["u"], c["w"], c["dd"], c["e_col"]
        sp = st_ref[hb, 0]
        snew = _bdot(c["trans"], sp) + c["inject"]
        yield
        o = _bdot(qn, snew)
        yield
        rstd = lax.rsqrt(jnp.mean(o * o, axis=-1, keepdims=True) + NORM_EPS)
        oh = o * rstd
        z = z_ref[:, cols]
        gain_v = gain_ref[...]
        dy = dog_ref[:, cols].astype(F32)
        dz_ref[:, cols] = (dy * oh * gain_v * _dsilu(z)).astype(dz_ref.dtype)
        dnorm = dy * _silu(z)
        dgain_ref[hb] += jnp.sum(dnorm * oh, axis=0, keepdims=True)
        doh = dnorm * gain_v
        do = rstd * (doh - oh * jnp.mean(doh * oh, axis=-1, keepdims=True))

        dstot = ds_acc[hb] + _bdot(qn, do, TN)
        dqn = _bdot(do, snew, NT)
        yield
        dtrans = _bdot(dstot, sp, NT)
        ds_acc[hb] = _bdot(c["trans"], dstot, TN)
        yield
        dg = jnp.sum(jnp.sum(dtrans * c["eye"], axis=1, keepdims=True), axis=0, keepdims=True)
        m = -dtrans
        dke = _bdot(w, m, NT) + _bdot(u, dstot, NT)
        dw = _bdot(ke, m)
        du = _bdot(ke, dstot)
        yield
        drhs = _fdot(c["tm"], jnp.concatenate([du, dw], axis=1), TN)
        yield
        da = jnp.where(c["strict"], -_fdot(drhs, c["sol"], NT), 0.0)
        yield
        drhs_u, drhs_w = drhs[:, :DV], drhs[:, DV:]
        rwk = rsum(drhs_w * kn)
        dbeta = rsum(da * c["rg"]) + rsum(drhs_u * v) + rwk * e_col
        dgm = da * beta * c["rel"]
        dkn = _bdot(dgm, kn) + _bdot(dgm, kn, TN) + (beta * e_col) * drhs_w + dd * dke
        yield
        dv_ref[:, cols] = beta * drhs_u
        r_ = da * c["a"]
        ddd = rsum(dke * kn)
        dc = rsum(r_) - _row_to_col(jnp.sum(r_, axis=0, keepdims=True)) + beta * rwk * e_col - ddd * dd
        dclast = jnp.sum(ddd * dd, axis=0, keepdims=True) + dg * c["g"]
        dc = dc + jnp.where(_iota2((C, 1), 0) == C - 1, dclast, 0.0)
        dla = jnp.sum(jnp.where(c["ii"] <= c["jj"], _col_to_row(dc), 0.0), axis=1, keepdims=True)
        dalog = jnp.sum(dla * c["la"], axis=0, keepdims=True)
        dxs = dla * (-c["ea"]) * _sigmoid(c["xs"])
        ddtb = jnp.sum(dxs, axis=0, keepdims=True)
        dbl = dbeta * beta * (1.0 - beta)
        lane = _iota2((C, LANE), 1)
        dgates_ref[hb] = jnp.where(lane == 0, dbl, jnp.where(lane == 1, dxs, 0.0))
        lane8 = _iota2((8, LANE), 1)
        dscal_ref[hb] += jnp.where(lane8 == 0, dalog, jnp.where(lane8 == 1, ddtb, 0.0))
        dk_ref[:, cols] = c["rk"] * (dkn - kn * rsum(dkn * kn))
        qh = c["q"] * c["rq"]
        dqs = dqn * (DK ** -0.5)
        dq_ref[:, cols] = c["rq"] * (dqs - qh * rsum(dqs * qh))

    q, k, v, z, gt, sc, gn, st = _gdn_specs(N, True)
    dgt = pl.BlockSpec((GDN_HB, C, LANE), lambda h, n: (h, N - 1 - n, 0))
    dsc = pl.BlockSpec((GDN_HB, 8, LANE), lambda h, n: (h, 0, 0))
    dgn = pl.BlockSpec((GDN_HB, 1, DV), lambda h, n: (h, 0, 0))
    sh = jax.ShapeDtypeStruct((T, H * DK), F32)
    return pl.pallas_call(
        body, name=name, grid=(H // GDN_HB, N), in_specs=[q, k, v, z, gt, sc, sc, gn, st, q],
        out_specs=(q, q, q, q, dgt, dsc, dgn),
        out_shape=(sh, sh, sh, jax.ShapeDtypeStruct((T, H * DV), BF16),
                   jax.ShapeDtypeStruct((H, T, LANE), F32), jax.ShapeDtypeStruct((H, 8, LANE), F32),
                   jax.ShapeDtypeStruct((H, 1, DV), F32)),
        scratch_shapes=[pltpu.VMEM((GDN_HB, DK, DV), F32)],
        compiler_params=_params(("arbitrary", "arbitrary")),
    )(act, act, act, proj, gates, a_log, dt_bias, gain, states, dog)


SUBLANES = 8


def _linear_scan(a_ref, b_ref, h_ref, reverse):
    T, W = a_ref.shape
    nb = T // SUBLANES
    row = _iota2((SUBLANES, W), 0)

    def blk(bi, carry):
        bb = (nb - 1 - bi) if reverse else bi
        off = pl.multiple_of(bb * SUBLANES, SUBLANES)
        a = a_ref[pl.ds(off, SUBLANES), :]
        b = b_ref[pl.ds(off, SUBLANES), :]
        for d in (1, 2, 4):
            if reverse:
                edge = row >= SUBLANES - d
                a_sh = jnp.where(edge, 1.0, pltpu.roll(a, SUBLANES - d, 0))
                b_sh = jnp.where(edge, 0.0, pltpu.roll(b, SUBLANES - d, 0))
            else:
                edge = row < d
                a_sh = jnp.where(edge, 1.0, pltpu.roll(a, d, 0))
                b_sh = jnp.where(edge, 0.0, pltpu.roll(b, d, 0))
            b = a * b_sh + b
            a = a * a_sh
        h = a * carry + b
        h_ref[pl.ds(off, SUBLANES), :] = h
        return h[0:1, :] if reverse else h[SUBLANES - 1:SUBLANES, :]

    lax.fori_loop(0, nb, blk, jnp.zeros((1, W), F32))


def _lru_specs(T):
    B, W = LRU_BLOCKS, LRU_BLOCK
    xb = pl.BlockSpec((T, W), lambda j: (0, j))
    yb = pl.BlockSpec((T, W), lambda j: (0, B + j))
    cw = pl.BlockSpec((CONV_WIDTH, W), lambda j: (0, j))
    vec = pl.BlockSpec((1, W), lambda j: (0, j))
    wg = pl.BlockSpec((1, W, W), lambda j: (j, 0, 0))
    bg = pl.BlockSpec((1, 1, W), lambda j: (j, 0, 0))
    return xb, yb, cw, vec, wg, bg


def _lru_gates(xb_ref, cw_ref, cb_ref, wr_ref, br_ref, wi_ref, bi_ref, lam_ref):
    xv = xb_ref[...]
    xc = _conv(xv, cw_ref) + cb_ref[...]
    r = _sigmoid(_bdot(xc, wr_ref[0]) + br_ref[0])
    i = _sigmoid(_bdot(xc, wi_ref[0]) + bi_ref[0])
    sp = _softplus(-lam_ref[...])
    la = -LRU_C * sp * r
    a = jnp.exp(la)
    s = jnp.sqrt(-_expm1(2.0 * la))
    return xv, xc, r, i, sp, a, s


def _lru_fwd(proj, conv_w, conv_b, w_r, b_r, w_i, b_i, lam, *, name):
    T = proj.shape[0]
    B, W = LRU_BLOCKS, LRU_BLOCK

    def body(xb_ref, yb_ref, cw_ref, cb_ref, wr_ref, br_ref, wi_ref, bi_ref, lam_ref, og_ref, hs_ref, a_s, u_s):
        xv, xc, r, i, sp, a, s = _lru_gates(xb_ref, cw_ref, cb_ref, wr_ref, br_ref, wi_ref, bi_ref, lam_ref)
        a_s[...] = a
        u_s[...] = s * (i * xc)
        _linear_scan(a_s, u_s, hs_ref, False)
        og_ref[...] = (hs_ref[...] * _gelu(yb_ref[...])).astype(og_ref.dtype)

    xb, yb, cw, vec, wg, bg = _lru_specs(T)
    return pl.pallas_call(
        body, name=name, grid=(B,), in_specs=[xb, yb, cw, vec, wg, bg, wg, bg, vec], out_specs=(xb, xb),
        out_shape=(jax.ShapeDtypeStruct((T, B * W), BF16), jax.ShapeDtypeStruct((T, B * W), F32)),
        scratch_shapes=[pltpu.VMEM((T, W), F32), pltpu.VMEM((T, W), F32)],
        compiler_params=_params(("arbitrary",)),
    )(proj, proj, conv_w, conv_b, w_r, b_r, w_i, b_i, lam)


def _lru_bwd(proj, conv_w, conv_b, w_r, b_r, w_i, b_i, lam, hs, dout, *, name):
    T = proj.shape[0]
    B, W = LRU_BLOCKS, LRU_BLOCK

    def csum(x):
        return jnp.sum(x, axis=0, keepdims=True)

    def body(xb_ref, yb_ref, cw_ref, cb_ref, wr_ref, br_ref, wi_ref, bi_ref, lam_ref, hs_ref, do_ref,
             dxb_ref, dyb_ref, dcw_ref, dcb_ref, dwr_ref, dbr_ref, dwi_ref, dbi_ref, dlam_ref, a_s, b_s, g_s):
        xv, xc, r, i, sp, a, s = _lru_gates(xb_ref, cw_ref, cb_ref, wr_ref, br_ref, wi_ref, bi_ref, lam_ref)
        h = hs_ref[...]
        yb = yb_ref[...]
        dout = do_ref[...].astype(F32)
        dyb_ref[...] = (dout * h * _dgelu(yb)).astype(dyb_ref.dtype)
        a_s[...] = _shift_up(a, 1)
        b_s[...] = dout * _gelu(yb)
        _linear_scan(a_s, b_s, g_s, True)
        g = g_s[...]
        da = g * _shift_down(h, 1)
        ds = g * (i * xc)
        di = g * s * xc
        dxc = g * s * i
        dla = da * a - ds * (a * a) / s
        dr = dla * (-LRU_C * sp)
        dlam_ref[...] = csum(dla * r) * (LRU_C * _sigmoid(-lam_ref[...]))
        dpr = dr * r * (1.0 - r)
        dpi = di * i * (1.0 - i)
        dxc = dxc + _bdot(dpr, wr_ref[0], NT) + _bdot(dpi, wi_ref[0], NT)
        dwr_ref[0] = _bdot(xc, dpr, TN)
        dwi_ref[0] = _bdot(xc, dpi, TN)
        dbr_ref[0] = csum(dpr)
        dbi_ref[0] = csum(dpi)
        dcb_ref[...] = csum(dxc)
        dxb_ref[...] = _conv_bwd(xv, cw_ref, dxc, dcw_ref).astype(dxb_ref.dtype)

    xb, yb, cw, vec, wg, bg = _lru_specs(T)
    act = jax.ShapeDtypeStruct((T, B * W), BF16)
    return pl.pallas_call(
        body, name=name, grid=(B,), in_specs=[xb, yb, cw, vec, wg, bg, wg, bg, vec, xb, xb],
        out_specs=(xb, xb, cw, vec, wg, bg, wg, bg, vec),
        out_shape=(act, act, jax.ShapeDtypeStruct((CONV_WIDTH, B * W), F32), jax.ShapeDtypeStruct((1, B * W), F32),
                   jax.ShapeDtypeStruct((B, W, W), F32), jax.ShapeDtypeStruct((B, 1, W), F32),
                   jax.ShapeDtypeStruct((B, W, W), F32), jax.ShapeDtypeStruct((B, 1, W), F32),
                   jax.ShapeDtypeStruct((1, B * W), F32)),
        scratch_shapes=[pltpu.VMEM((T, W), F32)] * 3,
        compiler_params=_params(("arbitrary",)),
    )(proj, proj, conv_w, conv_b, w_r, b_r, w_i, b_i, lam, hs, dout)


MESH = pl.DeviceIdType.MESH
N_CHIPS = 4
AG_COPIES = 7


def _mesh_pos():
    return lax.axis_index("x"), lax.axis_index("y"), lax.axis_index("c")


def _hbm_specs(n):
    return [pl.BlockSpec(memory_space=pltpu.HBM)] * n


def _all_gather(shards, *, name):
    n = len(shards)

    def body(*refs):
        xs, outs = refs[:n], refs[n:2 * n]
        send_sems, recv_sems, local_sems = refs[2 * n:]
        x, y, c = _mesh_pos()
        me, sibling = (x, y, c), (x, y, 1 - c)
        chips = [(1 - x, y), (x, 1 - y), (1 - x, 1 - y)]

        def rows(t, px, py, pc):
            return outs[t].at[4 * px + 2 * py + pc]

        def copy(t, k, block, to, src=None):
            return pltpu.make_async_remote_copy(
                src_ref=rows(t, *block) if src is None else src, dst_ref=rows(t, *block),
                send_sem=send_sems.at[t * AG_COPIES + k], recv_sem=recv_sems.at[t * AG_COPIES + k],
                device_id=to, device_id_type=MESH)

        mine = [pltpu.make_async_copy(xs[t], rows(t, *me), local_sems.at[t]) for t in range(n)]
        for cp in mine:
            cp.start()
        first = []
        for t in range(n):
            first.append(copy(t, 0, me, sibling, src=xs[t]))
            first += [copy(t, 1 + j, me, (*chip, c), src=xs[t]) for j, chip in enumerate(chips)]
        for cp in first:
            cp.start()
        passed = []
        for j, chip in enumerate(chips):
            for t in range(n):
                copy(t, 1 + j, (*chip, c), me).wait_recv()
                cp = copy(t, 4 + j, (*chip, c), sibling)
                cp.start()
                passed.append(cp)
        for t in range(n):
            copy(t, 0, sibling, me).wait_recv()
        for j, chip in enumerate(chips):
            for t in range(n):
                copy(t, 4 + j, (*chip, 1 - c), me).wait_recv()
        for cp in first + passed:
            cp.wait_send()
        for cp in mine:
            cp.wait()

    return pl.pallas_call(
        body, name=name,
        out_shape=[jax.ShapeDtypeStruct((N_DEV,) + s.shape, s.dtype) for s in shards],
        in_specs=_hbm_specs(n), out_specs=_hbm_specs(n),
        scratch_shapes=[pltpu.SemaphoreType.DMA((n * AG_COPIES,)), pltpu.SemaphoreType.DMA((n * AG_COPIES,)),
                        pltpu.SemaphoreType.DMA((n,))],
    )(*shards)


SIDE_EFFECT = pltpu.SideEffectType.DATAFLOW_SIDE_EFFECTING


def _copies(plan, refs, send_sems, recv_sems):
    return [pltpu.make_async_remote_copy(src_ref=src, dst_ref=dst, send_sem=send_sems.at[k], recv_sem=recv_sems.at[k],
                                         device_id=to, device_id_type=MESH)
            for k, (src, dst, to) in enumerate(plan(refs))]


def _split_start(bufs, plan, n_copies, *, name, deps=()):
    n = len(bufs)

    def body(*refs):
        send_sems, recv_sems = refs[n + len(deps)], refs[n + len(deps) + 1]
        token = refs[-1]
        for cp in _copies(plan, refs[:n], send_sems, recv_sems):
            cp.start()
        token[...] = jnp.zeros_like(token)

    hbm, sem = pl.BlockSpec(memory_space=pltpu.HBM), pl.BlockSpec(memory_space=pltpu.SEMAPHORE)
    out = pl.pallas_call(
        body, name=name,
        out_shape=(pltpu.SemaphoreType.DMA((n_copies,)), pltpu.SemaphoreType.DMA((n_copies,)),
                   *[pltpu.HBM(b.shape, b.dtype) for b in bufs], jax.ShapeDtypeStruct(TOKEN_SHAPE, F32)),
        in_specs=[hbm] * n + [pl.BlockSpec(memory_space=pl.ANY)] * len(deps),
        out_specs=(sem, sem, *[hbm] * n, pl.BlockSpec(memory_space=pltpu.VMEM)),
        input_output_aliases={i: 2 + i for i in range(n)},
        compiler_params=pltpu.CompilerParams(has_side_effects=SIDE_EFFECT),
    )(*[pltpu.with_memory_space_constraint(b, pltpu.HBM) for b in bufs], *deps)
    return out[0], out[1], list(out[2:2 + n]), out[-1]


def _split_wait(send_sems, recv_sems, bufs, plan, after, *, name):
    n = len(bufs)

    def body(*refs):
        for cp in _copies(plan, refs[:n], refs[n], refs[n + 1]):
            cp.wait_send()
            cp.wait_recv()

    hbm, sem = pl.BlockSpec(memory_space=pltpu.HBM), pl.BlockSpec(memory_space=pltpu.SEMAPHORE)
    out = pl.pallas_call(
        body, name=name, out_shape=tuple(pltpu.HBM(b.shape, b.dtype) for b in bufs),
        in_specs=[hbm] * n + [sem, sem, pl.BlockSpec(memory_space=pl.ANY)], out_specs=tuple([hbm] * n),
        input_output_aliases={i: i for i in range(n)},
        compiler_params=pltpu.CompilerParams(has_side_effects=SIDE_EFFECT),
    )(*bufs, send_sems, recv_sems, after)
    return list(out)


def _plan_gather_a(n):
    def plan(refs):
        x, y, c = _mesh_pos()
        me = 4 * x + 2 * y + c
        peers = [(x, y, 1 - c), (1 - x, y, c), (x, 1 - y, c), (1 - x, 1 - y, c)]
        return [(refs[t], refs[n + t].at[me], to) for t in range(n) for to in peers]
    return plan


def _plan_gather_b(n):
    def plan(refs):
        x, y, c = _mesh_pos()
        rows = [4 * px + 2 * py + c for px, py in [(1 - x, y), (x, 1 - y), (1 - x, 1 - y)]]
        return [(refs[t].at[r], refs[t].at[r], (x, y, 1 - c)) for t in range(n) for r in rows]
    return plan


def _plan_scatter_pair(n):
    def plan(refs):
        x, y, c = _mesh_pos()
        return [(refs[t].at[2 * q + (1 - c)], refs[n + t].at[q], (x, y, 1 - c))
                for t in range(n) for q in range(N_CHIPS)]
    return plan


def _plan_scatter_chips(n):
    def plan(refs):
        x, y, c = _mesh_pos()
        chips = [(1 - x, y), (x, 1 - y), (1 - x, 1 - y)]
        return [(refs[t].at[2 * px + py], refs[n + t].at[j], (px, py, c))
                for t in range(n) for j, (px, py) in enumerate(chips)]
    return plan


COMM_ROWS = 128


def _pair_add(g, a, core, *, name):
    _, R, C = g.shape
    tr = _tile(R, (COMM_ROWS,))

    def body(c_ref, g_ref, a_ref, o_ref):
        o_ref[...] = (g_ref[...].astype(F32) + a_ref[...].astype(F32)).astype(o_ref.dtype)

    blk = pl.BlockSpec((1, tr, C), lambda q, i, c: (q, i, 0))
    return pl.pallas_call(
        body, name=name,
        grid_spec=pltpu.PrefetchScalarGridSpec(
            num_scalar_prefetch=1, grid=(N_CHIPS, R // tr),
            in_specs=[pl.BlockSpec((1, tr, C), lambda q, i, c: (2 * q + c[0], i, 0)), blk], out_specs=blk),
        out_shape=jax.ShapeDtypeStruct((N_CHIPS, R, C), BF16),
        compiler_params=_params(("parallel", "parallel")),
    )(core, g, a)


def _adamw_sharded(w, m, v, s4, b3, chip, *, name):
    R, C = w.shape
    tr = _tile(R, (COMM_ROWS,))

    def body(q_ref, w_ref, m_ref, v_ref, s_ref, b_ref, g_out, d_out, m_out, v_out):
        g = s_ref[0].astype(F32)
        for j in range(N_CHIPS - 1):
            g = g + b_ref[j].astype(F32)
        mn = ADAM_B1 * m_ref[...] + (1.0 - ADAM_B1) * g
        vn = ADAM_B2 * v_ref[...] + (1.0 - ADAM_B2) * (g * g)
        g_out[...] = g
        d_out[...] = -ADAM_LR * ((mn / ADAM_C1) / (jnp.sqrt(vn / ADAM_C2) + ADAM_EPS) + ADAM_WD * w_ref[...])
        m_out[...] = mn
        v_out[...] = vn

    blk = pl.BlockSpec((tr, C), lambda i, q: (i, 0))
    sh = jax.ShapeDtypeStruct((R, C), F32)
    return pl.pallas_call(
        body, name=name,
        grid_spec=pltpu.PrefetchScalarGridSpec(
            num_scalar_prefetch=1, grid=(R // tr,),
            in_specs=[blk, blk, blk, pl.BlockSpec((1, tr, C), lambda i, q: (q[0], i, 0)),
                      pl.BlockSpec((N_CHIPS - 1, tr, C), lambda i, q: (0, i, 0))],
            out_specs=(blk,) * 4),
        out_shape=(sh,) * 4, compiler_params=_params(("parallel",)),
    )(chip, w, m, v, s4, b3)


FWD_NAMES = ['x', 'norm1', 'norm2', 'final_norm', 'ret_w_in', 'ret_gn_gain', 'ret_w_out', 'gdn_w_in', 'gdn_conv_w',
             'gdn_a_log', 'gdn_dt_bias', 'gdn_norm_gain', 'gdn_w_out', 'gla_w_in', 'gla_w_gate_up', 'gla_gate_bias',
             'gla_norm_gain', 'gla_w_out', 'lru_w_in', 'lru_conv_w', 'lru_conv_b', 'lru_w_rgate', 'lru_b_rgate',
             'lru_w_igate', 'lru_b_igate', 'lru_lambda', 'lru_w_out', 'mlp_w_up', 'mlp_w_down']
WEIGHT_NAMES = FWD_NAMES[1:]
ARG_NAMES = FWD_NAMES + ['loss_target'] + ['m_' + n for n in WEIGHT_NAMES] + ['v_' + n for n in WEIGHT_NAMES]

MIXER_IN = ('ret_w_in', 'gdn_w_in', 'gla_w_in', 'lru_w_in')
MIXER_OUT = ('ret_w_out', 'gdn_w_out', 'gla_w_out', 'lru_w_out')
BIG_NAMES = MIXER_IN + MIXER_OUT + ('mlp_w_up', 'mlp_w_down')
SMALL = {'norm1': False, 'norm2': False, 'final_norm': False, 'ret_gn_gain': True, 'gdn_conv_w': True,
         'gdn_a_log': False, 'gdn_dt_bias': False, 'gdn_norm_gain': False, 'gla_w_gate_up': True,
         'gla_gate_bias': True, 'gla_norm_gain': True, 'lru_conv_w': True, 'lru_conv_b': True,
         'lru_w_rgate': False, 'lru_b_rgate': False, 'lru_w_igate': False, 'lru_b_igate': False, 'lru_lambda': True}
SMALL_NAMES = tuple(n for n in WEIGHT_NAMES if n in SMALL)
GDN_TAIL = 2 * GDN_HEADS


def _pack(arrs):
    rows = []
    for a in arrs:
        f = a.reshape(-1).astype(F32)
        rows.append(jnp.pad(f, (0, (-f.shape[0]) % LANE)).reshape(-1, LANE))
    buf = jnp.concatenate(rows, axis=0)
    return jnp.pad(buf, ((0, (-buf.shape[0]) % SUBLANES), (0, 0)))


def _unpack(buf, shapes, lead=()):
    out, r0 = [], 0
    for s in shapes:
        n = int(np.prod(s))
        nr = -(-n // LANE)
        blk = buf[..., r0:r0 + nr, :].reshape(lead + (nr * LANE,))[..., :n]
        out.append(blk.reshape(lead + tuple(s)))
        r0 += nr
    return out


def _full_cols(g):
    return jnp.transpose(g, (1, 0, 2)).reshape(g.shape[1], N_DEV * g.shape[2])


def _full_rows(g):
    return g.reshape(N_DEV * g.shape[1], g.shape[2])


def _blocks_cols(dw):
    r, c = dw.shape[0], dw.shape[1] // N_DEV
    return jnp.transpose(dw.reshape(r, N_DEV, c), (1, 0, 2))


def _blocks_rows(dw):
    return dw.reshape(N_DEV, dw.shape[0] // N_DEV, dw.shape[1])


def _pad_cols(a, n=LANE):
    return jnp.pad(a, ((0, 0), (0, n - a.shape[1])))


def _mixer_fwd(layer, hn, w_in, sm, tables, deps=()):
    tag = f"l{layer}"
    if layer == 0:
        proj = _matmul(hn, w_in, name=tag + "_in", deps=deps)
        gain = sm['ret_gn_gain'][0][:, None, :]
        og, st = _ret_fwd(proj, gain, tables, name=tag + "_ret_fwd")
        return og, dict(proj=proj, st=st, gain=gain)
    if layer == 1:
        w_main, w_tail = w_in[:, :4 * D_MODEL], _pad_cols(w_in[:, 4 * D_MODEL:])
        proj = _matmul(hn, w_main, name=tag + "_in", deps=deps)
        gates = _matmul(hn, w_tail, name=tag + "_in_tail")
        conv_w = sm['gdn_conv_w'][0]
        act = _conv_silu_fwd(proj, conv_w, name=tag + "_conv")
        a_log = sm['gdn_a_log'].reshape(GDN_HEADS, 1, 1)
        dt_bias = sm['gdn_dt_bias'].reshape(GDN_HEADS, 1, 1)
        gain = sm['gdn_norm_gain']
        og, st = _gdn_fwd(act, proj, gates, a_log, dt_bias, gain, name=tag + "_gdn_fwd")
        return og, dict(proj=proj, gates=gates, act=act, st=st, conv_w=conv_w, a_log=a_log, dt_bias=dt_bias,
                        gain=gain, w_main=w_main, w_tail=w_tail)
    if layer == 2:
        w_main, w_tail = w_in[:, :3 * D_MODEL], _pad_cols(w_in[:, 3 * D_MODEL:])
        proj = _matmul(hn, w_main, name=tag + "_in", deps=deps)
        glow = _matmul(hn, w_tail, name=tag + "_in_tail")
        wgu = jnp.pad(sm['gla_w_gate_up'][0], ((0, LANE - GLA_GATE_RANK), (0, 0)))
        glogit = _matmul(glow, wgu, name=tag + "_gate_up")
        bias = sm['gla_gate_bias']
        gain = sm['gla_norm_gain'][0][:, None, :]
        og, st = _gla_fwd(proj, glogit, bias, gain, name=tag + "_gla_fwd")
        return og, dict(proj=proj, glow=glow, glogit=glogit, wgu=wgu, bias=bias, gain=gain, st=st,
                        w_main=w_main, w_tail=w_tail)
    proj = _matmul(hn, w_in, name=tag + "_in", deps=deps)
    args = (proj, sm['lru_conv_w'][0], sm['lru_conv_b'], sm['lru_w_rgate'][0], sm['lru_b_rgate'][0][:, None, :],
            sm['lru_w_igate'][0], sm['lru_b_igate'][0][:, None, :], sm['lru_lambda'])
    og, hs = _lru_fwd(*args, name=tag + "_lru_fwd")
    return og, dict(args=args, hs=hs)


def _mixer_bwd(layer, hn, w_in, dog, sv, tables, on_dw):
    tag = f"l{layer}"
    if layer == 0:
        dq, dk, dv, dg, dgain = _ret_bwd(sv['proj'], sv['gain'], tables, sv['st'], dog, name=tag + "_ret_bwd")
        dproj = jnp.concatenate([dq, dk, dv, dg], axis=1)
        deps = on_dw(_matmul(hn, dproj, ta=True, out_dtype=BF16, name=tag + "_in_dw"))
        dhn = _matmul(dproj, w_in, tb=True, name=tag + "_in_dx", deps=deps)
        return dhn, {'ret_gn_gain': dgain[:, 0][None]}
    if layer == 1:
        dq, dk, dv, dz, dgates, dscal, dgain = _gdn_bwd(
            sv['act'], sv['proj'], sv['gates'], sv['a_log'], sv['dt_bias'], sv['gain'], sv['st'], dog,
            name=tag + "_gdn_bwd")
        dact = jnp.concatenate([dq, dk, dv], axis=1)
        dqkv, dconv = _conv_silu_bwd(sv['proj'], sv['conv_w'], dact, name=tag + "_conv_bwd")
        dmain = jnp.concatenate([dqkv, dz], axis=1)
        T = dmain.shape[0]
        dtail = _pad_cols(jnp.transpose(dgates[:, :, :2], (1, 2, 0)).reshape(T, GDN_TAIL))
        dw_main = _matmul(hn, dmain, ta=True, out_dtype=BF16, name=tag + "_in_dw")
        dw_tail = _matmul(hn, dtail, ta=True, out_dtype=BF16, name=tag + "_in_tail_dw")
        deps = on_dw(jnp.concatenate([dw_main, dw_tail[:, :GDN_TAIL]], axis=1))
        dhn = _matmul(dmain, sv['w_main'], tb=True, name=tag + "_in_dx", deps=deps)
        dhn = _matmul(dtail, sv['w_tail'], tb=True, epi="add", extra=dhn, name=tag + "_in_tail_dx")
        small = {'gdn_conv_w': dconv[None], 'gdn_a_log': dscal[:, 0, 0][None], 'gdn_dt_bias': dscal[:, 0, 1][None],
                 'gdn_norm_gain': jnp.sum(dgain[:, 0], axis=0)[None]}
        return dhn, small
    if layer == 2:
        dq, dk, dv, dr, dgl, dgain = _gla_bwd(sv['proj'], sv['glogit'], sv['bias'], sv['gain'], sv['st'], dog,
                                              name=tag + "_gla_bwd")
        dmain = jnp.concatenate([dq, dk, dv, dr], axis=1)
        dglow = _matmul(dgl, sv['wgu'], tb=True, name=tag + "_gate_up_dx")
        dwgu = _matmul(sv['glow'], dgl, ta=True, name=tag + "_gate_up_dw")
        dbias = _colsum(dgl, name=tag + "_gate_bias")
        dw_main = _matmul(hn, dmain, ta=True, out_dtype=BF16, name=tag + "_in_dw")
        dw_tail = _matmul(hn, dglow, ta=True, out_dtype=BF16, name=tag + "_in_tail_dw")
        deps = on_dw(jnp.concatenate([dw_main, dw_tail[:, :GLA_GATE_RANK]], axis=1))
        dhn = _matmul(dmain, sv['w_main'], tb=True, name=tag + "_in_dx", deps=deps)
        dhn = _matmul(dglow, sv['w_tail'], tb=True, epi="add", extra=dhn, name=tag + "_in_tail_dx")
        small = {'gla_w_gate_up': dwgu[:GLA_GATE_RANK][None], 'gla_gate_bias': dbias,
                 'gla_norm_gain': dgain[:, 0][None]}
        return dhn, small
    dxb, dyb, dcw, dcb, dwr, dbr, dwi, dbi, dlam = _lru_bwd(*sv['args'], sv['hs'], dog, name=tag + "_lru_bwd")
    dproj = jnp.concatenate([dxb, dyb], axis=1)
    deps = on_dw(_matmul(hn, dproj, ta=True, out_dtype=BF16, name=tag + "_in_dw"))
    dhn = _matmul(dproj, w_in, tb=True, name=tag + "_in_dx", deps=deps)
    small = {'lru_conv_w': dcw[None], 'lru_conv_b': dcb, 'lru_w_rgate': dwr[None], 'lru_b_rgate': dbr[:, 0][None],
             'lru_w_igate': dwi[None], 'lru_b_igate': dbi[:, 0][None], 'lru_lambda': dlam}
    return dhn, small


def _step(*args):
    assert len(args) == len(ARG_NAMES)
    p = dict(zip(ARG_NAMES, args))
    xi, yi, ci = _mesh_pos()
    dev = 4 * xi + 2 * yi + ci
    core = ci.astype(jnp.int32).reshape(1)
    chip = (2 * xi + yi).astype(jnp.int32).reshape(1)
    x = p['x'][0]
    target = p['loss_target'][0]
    T = x.shape[0]
    tables = _ret_tables(T)

    sharded_small = [n for n in SMALL_NAMES if SMALL[n]]
    gathered, = _all_gather([_pack([p[n] for n in sharded_small])], name="gather_small")
    gathers = {}
    token = gathered
    for layer in range(DEPTH):
        groups = {'a': [(p[MIXER_IN[layer]][0], _full_cols)],
                  'b': [(p[MIXER_OUT[layer]][0], _full_rows), (p['mlp_w_up'][layer], _full_cols),
                        (p['mlp_w_down'][layer], _full_rows)]}
        for key, members in groups.items():
            shards = [w.astype(BF16) for w, _ in members]
            n = len(shards)
            lands = [lax.empty((N_DEV,) + s.shape, BF16) for s in shards]
            send, recv, bufs, token = _split_start(shards + lands, _plan_gather_a(n), 4 * n,
                                                   name=f"gather_a_start_l{layer}{key}", deps=(token,))
            gathers[layer, key] = dict(n=n, h=(send, recv, bufs), full_of=[f for _, f in members])

    def gather_forward(layer, key, after):
        g = gathers[layer, key]
        n = g['n']
        send, recv, bufs = g['h']
        bufs = _split_wait(send, recv, bufs, _plan_gather_a(n), after, name=f"gather_a_wait_l{layer}{key}")
        send, recv, lands, tok = _split_start(bufs[n:], _plan_gather_b(n), 3 * n, name=f"gather_b_start_l{layer}{key}")
        g['h'], g['shards'] = (send, recv, lands), bufs[:n]
        return tok

    def gather_finish(layer, key, after):
        g = gathers[layer, key]
        send, recv, lands = g['h']
        lands = _split_wait(send, recv, lands, _plan_gather_b(g['n']), after, name=f"gather_b_wait_l{layer}{key}")
        return [full_of(lax.dynamic_update_slice(l, s[None], (dev, 0, 0)))
                for l, s, full_of in zip(lands, g['shards'], g['full_of'])]

    w_in_next, = gather_finish(0, 'a', gather_forward(0, 'a', token))
    parts = _unpack(gathered, [p[n].shape for n in sharded_small], lead=(N_DEV,))
    sm = {n: p[n] for n in SMALL_NAMES if not SMALL[n]}
    for n, blk in zip(sharded_small, parts):
        full = jnp.moveaxis(blk, 0, -2)
        sm[n] = full.reshape(full.shape[:-2] + (N_DEV * full.shape[-1],))

    saved = []
    big = {}
    for layer in range(DEPTH):
        w_in = w_in_next
        tag = f"l{layer}"
        hn = _rmsnorm_fwd(x, sm['norm1'][layer][None], name=tag + "_norm1")
        if layer == 0:
            og, sv = _mixer_fwd(layer, hn, w_in, sm, tables)
            w_out, w_up, w_down = gather_finish(layer, 'b', gather_forward(layer, 'b', og))
        else:
            og, sv = _mixer_fwd(layer, hn, w_in, sm, tables, deps=(gather_forward(layer, 'b', hn),))
            w_out, w_up, w_down = gather_finish(layer, 'b', og)
        big[layer] = (w_in, w_out, w_up, w_down)
        deps = (gather_forward(layer + 1, 'a', og),) if layer + 1 < DEPTH else ()
        x_mid = _matmul(og, w_out, epi="add", extra=x, name=tag + "_out", deps=deps)
        hn2 = _rmsnorm_fwd(x_mid, sm['norm2'][layer][None], name=tag + "_norm2")
        u, a = _matmul(hn2, w_up, epi="relu2", name=tag + "_up")
        x_new = _matmul(a, w_down, epi="add", extra=x_mid, name=tag + "_down")
        if layer + 1 < DEPTH:
            w_in_next, = gather_finish(layer + 1, 'a', x_new)
        saved.append(dict(x=x, hn=hn, og=og, sv=sv, x_mid=x_mid, hn2=hn2, u=u, a=a))
        x = x_new
    dy, loss_part = _final_loss(x, sm['final_norm'][None], target, name="final_loss")
    loss = lax.psum(loss_part[0, 0], ("x", "y", "c"))
    dx, dxb, dfinal = _rmsnorm_bwd(x, sm['final_norm'][None], dy, jnp.zeros_like(x), name="final_norm_bwd")

    outs = {}
    small_grads = {'final_norm': dfinal[0]}
    dnorm1, dnorm2 = [None] * DEPTH, [None] * DEPTH
    mlp_res = {'mlp_w_up': [None] * DEPTH, 'mlp_w_down': [None] * DEPTH}

    def scatter_start(items, deps, tag):
        blocks = [b for _, _, b in items]
        n = len(blocks)
        lands = [lax.empty((N_CHIPS,) + b.shape[1:], BF16) for b in blocks]
        send, recv, bufs, tok = _split_start(blocks + lands, _plan_scatter_pair(n), N_CHIPS * n,
                                             name=f"scatter_pair_start_{tag}", deps=deps)
        return dict(items=items, n=n, tag=tag, h=(send, recv, bufs)), tok

    def scatter_forward(g, after):
        n, tag = g['n'], g['tag']
        send, recv, bufs = g['h']
        bufs = _split_wait(send, recv, bufs, _plan_scatter_pair(n), after, name=f"scatter_pair_wait_{tag}")
        sums = [_pair_add(b, a_, core, name=f"pair_add_{tag}_{i}") for i, (b, a_) in enumerate(zip(bufs[:n], bufs[n:]))]
        lands = [lax.empty((N_CHIPS - 1,) + s_.shape[1:], BF16) for s_ in sums]
        send, recv, bufs, tok = _split_start(sums + lands, _plan_scatter_chips(n), (N_CHIPS - 1) * n,
                                             name=f"scatter_chips_start_{tag}")
        g['h'] = (send, recv, bufs)
        return tok

    def scatter_finish(g, after):
        n, tag = g['n'], g['tag']
        send, recv, bufs = g['h']
        bufs = _split_wait(send, recv, bufs, _plan_scatter_chips(n), after, name=f"scatter_chips_wait_{tag}")
        for i, (wname, idx, _) in enumerate(g['items']):
            res = _adamw_sharded(p[wname][idx], p['m_' + wname][idx], p['v_' + wname][idx], bufs[i], bufs[n + i], chip,
                                 name=f"adamw_{tag}_{i}")
            if wname.startswith('mlp'):
                mlp_res[wname][idx] = res
            else:
                outs[wname] = tuple(r[None] for r in res)

    older = []
    for layer in reversed(range(DEPTH)):
        w_in, w_out, w_up, w_down = big[layer]
        s = saved[layer]
        tag = f"l{layer}"
        du = _matmul(dxb, w_down, tb=True, epi="drelu2", extra=s['u'], out_dtype=BF16, name=tag + "_down_dx")
        dw_down = _matmul(s['a'], dxb, ta=True, out_dtype=BF16, name=tag + "_down_dw")
        dhn2 = _matmul(du, w_up, tb=True, name=tag + "_up_dx")
        dw_up = _matmul(s['hn2'], du, ta=True, out_dtype=BF16, name=tag + "_up_dw")
        dx, dxb, dn2 = _rmsnorm_bwd(s['x_mid'], sm['norm2'][layer][None], dhn2, dx, name=tag + "_norm2_bwd")
        mlp_group, tok = scatter_start([('mlp_w_up', layer, _blocks_cols(dw_up)),
                                        ('mlp_w_down', layer, _blocks_rows(dw_down))], (), f"mlp_l{layer}")
        dw_out = _matmul(s['og'], dxb, ta=True, out_dtype=BF16, name=tag + "_out_dw", deps=(tok,))
        tok = scatter_forward(mlp_group, dw_out)
        dog = _matmul(dxb, w_out, tb=True, name=tag + "_out_dx", deps=(tok,))
        started = []

        def on_dw(dw_in):
            group, tok_ = scatter_start([(MIXER_IN[layer], 0, _blocks_cols(dw_in)),
                                         (MIXER_OUT[layer], 0, _blocks_rows(dw_out))], (), f"mix_l{layer}")
            started.append(group)
            return (tok_,)

        dhn, sg = _mixer_bwd(layer, s['hn'], w_in, dog, s['sv'], tables, on_dw)
        mixer_group, = started
        small_grads.update(sg)
        if layer > 0:
            tok = scatter_forward(mixer_group, dhn)
            dx, dxb, dn1 = _rmsnorm_bwd(s['x'], sm['norm1'][layer][None], dhn, dx, name=tag + "_norm1_bwd", deps=(tok,))
        else:
            dx, dxb, dn1 = _rmsnorm_bwd(s['x'], sm['norm1'][layer][None], dhn, dx, name=tag + "_norm1_bwd")
        dnorm1[layer], dnorm2[layer] = dn1[0], dn2[0]
        for g in older:
            scatter_finish(g, dx)
        older = [mlp_group, mixer_group]
    small_grads['norm1'] = jnp.stack(dnorm1)
    small_grads['norm2'] = jnp.stack(dnorm2)

    full_shapes = [small_grads[n].shape for n in SMALL_NAMES]
    all_parts, = _all_gather([_pack([small_grads[n] for n in SMALL_NAMES])], name="gather_small_grads")
    last_token = scatter_forward(mixer_group, all_parts)
    summed_buf = _sum_parts(all_parts, name="sum_small_grads", deps=(last_token,))
    for g in older:
        scatter_finish(g, summed_buf)
    for n in ('mlp_w_up', 'mlp_w_down'):
        outs[n] = tuple(jnp.stack([mlp_res[n][layer][k] for layer in range(DEPTH)]) for k in range(4))
    summed = _unpack(summed_buf, full_shapes)
    local_g = []
    for n, g in zip(SMALL_NAMES, summed):
        if SMALL[n]:
            width = p[n].shape[-1]
            g = lax.dynamic_slice_in_dim(g, dev * width, width, axis=g.ndim - 1)
        local_g.append(g.reshape(p[n].shape))
    res = _adamw(_pack([p[n] for n in SMALL_NAMES]), _pack([p['m_' + n] for n in SMALL_NAMES]),
                 _pack([p['v_' + n] for n in SMALL_NAMES]), [_pack(local_g)], name="adamw_small")
    local_shapes = [p[n].shape for n in SMALL_NAMES]
    unpacked = [_unpack(r, local_shapes) for r in res]
    for i, n in enumerate(SMALL_NAMES):
        outs[n] = tuple(unpacked[k][i] for k in range(4))

    result = [loss, dx[None]]
    for k in range(4):
        result += [outs[n][k] for n in WEIGHT_NAMES]
    return tuple(result)


def kernel(x, norm1, norm2, final_norm, ret_w_in, ret_gn_gain, ret_w_out, gdn_w_in, gdn_conv_w, gdn_a_log, gdn_dt_bias, gdn_norm_gain, gdn_w_out, gla_w_in, gla_w_gate_up, gla_gate_bias, gla_norm_gain, gla_w_out, lru_w_in, lru_conv_w, lru_conv_b, lru_w_rgate, lru_b_rgate, lru_w_igate, lru_b_igate, lru_lambda, lru_w_out, mlp_w_up, mlp_w_down, loss_target, m_norm1, m_norm2, m_final_norm, m_ret_w_in, m_ret_gn_gain, m_ret_w_out, m_gdn_w_in, m_gdn_conv_w, m_gdn_a_log, m_gdn_dt_bias, m_gdn_norm_gain, m_gdn_w_out, m_gla_w_in, m_gla_w_gate_up, m_gla_gate_bias, m_gla_norm_gain, m_gla_w_out, m_lru_w_in, m_lru_conv_w, m_lru_conv_b, m_lru_w_rgate, m_lru_b_rgate, m_lru_w_igate, m_lru_b_igate, m_lru_lambda, m_lru_w_out, m_mlp_w_up, m_mlp_w_down, v_norm1, v_norm2, v_final_norm, v_ret_w_in, v_ret_gn_gain, v_ret_w_out, v_gdn_w_in, v_gdn_conv_w, v_gdn_a_log, v_gdn_dt_bias, v_gdn_norm_gain, v_gdn_w_out, v_gla_w_in, v_gla_w_gate_up, v_gla_gate_bias, v_gla_norm_gain, v_gla_w_out, v_lru_w_in, v_lru_conv_w, v_lru_conv_b, v_lru_w_rgate, v_lru_b_rgate, v_lru_w_igate, v_lru_b_igate, v_lru_lambda, v_lru_w_out, v_mlp_w_up, v_mlp_w_down):
    return _step(x, norm1, norm2, final_norm, ret_w_in, ret_gn_gain, ret_w_out, gdn_w_in, gdn_conv_w, gdn_a_log, gdn_dt_bias, gdn_norm_gain, gdn_w_out, gla_w_in, gla_w_gate_up, gla_gate_bias, gla_norm_gain, gla_w_out, lru_w_in, lru_conv_w, lru_conv_b, lru_w_rgate, lru_b_rgate, lru_w_igate, lru_b_igate, lru_lambda, lru_w_out, mlp_w_up, mlp_w_down, loss_target, m_norm1, m_norm2, m_final_norm, m_ret_w_in, m_ret_gn_gain, m_ret_w_out, m_gdn_w_in, m_gdn_conv_w, m_gdn_a_log, m_gdn_dt_bias, m_gdn_norm_gain, m_gdn_w_out, m_gla_w_in, m_gla_w_gate_up, m_gla_gate_bias, m_gla_norm_gain, m_gla_w_out, m_lru_w_in, m_lru_conv_w, m_lru_conv_b, m_lru_w_rgate, m_lru_b_rgate, m_lru_w_igate, m_lru_b_igate, m_lru_lambda, m_lru_w_out, m_mlp_w_up, m_mlp_w_down, v_norm1, v_norm2, v_final_norm, v_ret_w_in, v_ret_gn_gain, v_ret_w_out, v_gdn_w_in, v_gdn_conv_w, v_gdn_a_log, v_gdn_dt_bias, v_gdn_norm_gain, v_gdn_w_out, v_gla_w_in, v_gla_w_gate_up, v_gla_gate_bias, v_gla_norm_gain, v_gla_w_out, v_lru_w_in, v_lru_conv_w, v_lru_conv_b, v_lru_w_rgate, v_lru_b_rgate, v_lru_w_igate, v_lru_b_igate, v_lru_lambda, v_lru_w_out, v_mlp_w_up, v_mlp_w_down)
```

```python
import functools
import math

import numpy as np
import jax
import jax.numpy as jnp
from jax import lax
from jax.experimental import pallas as pl
from jax.experimental.pallas import tpu as pltpu

F32 = jnp.float32
BF16 = jnp.bfloat16

D_MODEL = 2048
DEPTH = 4
CHUNK = 64
D_FF = 4 * D_MODEL
NORM_EPS = 1e-6
N_DEV = 8

RET_HEADS, RET_DK, RET_DV = 8, 256, 512
RET_HB = 4
GDN_HEADS, GDN_DK, GDN_DV = 16, 128, 128
GDN_HB = 8
GDN_QKV = GDN_HEADS * (2 * GDN_DK + GDN_DV)
CONV_WIDTH = 4
GLA_HEADS, GLA_DK, GLA_DV = 4, 256, 512
GLA_HB = 4
GLA_GATE_RANK = 16
GLA_TAU = 16.0
LRU_WIDTH, LRU_BLOCKS, LRU_BLOCK = 2048, 16, 128
LRU_C = 8.0
ROPE_BASE = 10000.0

ADAM_LR, ADAM_B1, ADAM_B2, ADAM_EPS, ADAM_WD, ADAM_STEP = 0.001, 0.9, 0.999, 1e-08, 0.01, 10

LANE = 128
VMEM_LIMIT = 48 * 1024 * 1024

NN = (((1,), (0,)), ((), ()))
NT = (((1,), (1,)), ((), ()))
TN = (((0,), (0,)), ((), ()))


def _params(sem=None):
    return pltpu.CompilerParams(dimension_semantics=sem, vmem_limit_bytes=VMEM_LIMIT)


def _bdot(a, b, dn=NN):
    return lax.dot_general(a.astype(BF16), b.astype(BF16), dn, preferred_element_type=F32)


def _split(x):
    hi = x.astype(BF16)
    lo = (x - hi.astype(F32)).astype(BF16)
    return hi, lo


def _fdot(a, b, dn=NN):
    a1, a2 = _split(a)
    b1, b2 = _split(b)
    d = functools.partial(lax.dot_general, dimension_numbers=dn, preferred_element_type=F32)
    return d(a1, b1) + (d(a1, b2) + d(a2, b1))


def _sigmoid(x):
    return 1.0 / (1.0 + jnp.exp(-x))


def _softplus(x):
    return jnp.maximum(x, 0.0) + jnp.log(1.0 + jnp.exp(-jnp.abs(x)))


def _silu(x):
    return x * _sigmoid(x)


def _dsilu(x):
    s = _sigmoid(x)
    return s * (1.0 + x * (1.0 - s))


GELU_C = math.sqrt(2.0 / math.pi)


def _gelu(x):
    return 0.5 * x * (1.0 + jnp.tanh(GELU_C * (x + 0.044715 * x * x * x)))


def _dgelu(x):
    t = jnp.tanh(GELU_C * (x + 0.044715 * x * x * x))
    return 0.5 * (1.0 + t) + 0.5 * x * (1.0 - t * t) * GELU_C * (1.0 + 3.0 * 0.044715 * x * x)


def _expm1(x):
    poly = x * (1.0 + x * 0.5 * (1.0 + x * (1.0 / 3.0) * (1.0 + x * 0.25 * (1.0 + x * 0.2))))
    return jnp.where(jnp.abs(x) < 0.05, poly, jnp.exp(x) - 1.0)


def _iota2(shape, axis):
    return lax.broadcasted_iota(jnp.int32, shape, axis)


def _col_to_row(col):
    n = col.shape[0]
    eye = _iota2((n, n), 0) == _iota2((n, n), 1)
    return jnp.sum(jnp.where(eye, col, 0.0), axis=0, keepdims=True)


def _row_to_col(row):
    n = row.shape[1]
    eye = _iota2((n, n), 0) == _iota2((n, n), 1)
    return jnp.sum(jnp.where(eye, row, 0.0), axis=1, keepdims=True)


def _pick_row(x, r):
    rows = _iota2(x.shape, 0)
    return jnp.sum(jnp.where(rows == r, x, 0.0), axis=0, keepdims=True)


def _shift_down(x, s):
    if s == 0:
        return x
    y = pltpu.roll(x, s, 0)
    return jnp.where(_iota2(x.shape, 0) < s, 0.0, y)


def _shift_up(x, s):
    if s == 0:
        return x
    n = x.shape[0]
    y = pltpu.roll(x, n - s, 0)
    return jnp.where(_iota2(x.shape, 0) >= n - s, 0.0, y)


def _tile(dim, prefs):
    for p in prefs:
        if dim % p == 0:
            return p
    return dim


TOKEN_SHAPE = (8, LANE)


def _dep_specs(deps):
    return [pl.BlockSpec(TOKEN_SHAPE, lambda *_: (0, 0)) for _ in deps]


def _matmul(a, b, *, ta=False, tb=False, epi="none", extra=None, out_dtype=F32, name, deps=()):
    if ta:
        K, M = a.shape
    else:
        M, K = a.shape
    if tb:
        N, K2 = b.shape
    else:
        K2, N = b.shape
    assert K == K2, (a.shape, b.shape, ta, tb)
    tm = _tile(M, (1024, 512, 256, 128))
    tn = _tile(N, (512, 256, 128))
    tk = _tile(K, (2048, 1024, 512, 256, 128))
    nk = K // tk
    dn = (((0 if ta else 1,), (1 if tb else 0,)), ((), ()))
    n_extra = 0 if extra is None else 1
    n_out = 2 if epi == "relu2" else 1

    def body(*refs):
        a_ref, b_ref = refs[0], refs[1]
        e_ref = refs[2] if n_extra else None
        outs = refs[2 + n_extra + len(deps):2 + n_extra + len(deps) + n_out]

        def finish(r):
            if epi == "none":
                outs[0][...] = r.astype(outs[0].dtype)
            elif epi == "add":
                outs[0][...] = (r + e_ref[...]).astype(outs[0].dtype)
            elif epi == "relu2":
                outs[0][...] = r
                p = jnp.maximum(r, 0.0)
                outs[1][...] = (p * p).astype(outs[1].dtype)
            elif epi == "drelu2":
                outs[0][...] = (r * 2.0 * jnp.maximum(e_ref[...], 0.0)).astype(outs[0].dtype)

        def product():
            return lax.dot_general(a_ref[...].astype(BF16), b_ref[...].astype(BF16), dn, preferred_element_type=F32)

        if nk == 1:
            finish(product())
            return
        acc = refs[-1]
        k = pl.program_id(2)

        @pl.when(k == 0)
        def _():
            acc[...] = jnp.zeros_like(acc)

        acc[...] += product()

        @pl.when(k == nk - 1)
        def _():
            finish(acc[...])

    a_spec = pl.BlockSpec((tk, tm), lambda i, j, k: (k, i)) if ta else pl.BlockSpec((tm, tk), lambda i, j, k: (i, k))
    b_spec = pl.BlockSpec((tn, tk), lambda i, j, k: (j, k)) if tb else pl.BlockSpec((tk, tn), lambda i, j, k: (k, j))
    o_spec = pl.BlockSpec((tm, tn), lambda i, j, k: (i, j))
    in_specs = [a_spec, b_spec] + ([o_spec] if n_extra else []) + _dep_specs(deps)
    if epi == "relu2":
        out_shape = (jax.ShapeDtypeStruct((M, N), F32), jax.ShapeDtypeStruct((M, N), BF16))
        out_specs = (o_spec, o_spec)
    else:
        out_shape = jax.ShapeDtypeStruct((M, N), out_dtype)
        out_specs = o_spec
    args = (a, b) + ((extra,) if n_extra else ()) + tuple(deps)
    return pl.pallas_call(
        body, name=name, grid=(M // tm, N // tn, nk), in_specs=in_specs, out_specs=out_specs,
        out_shape=out_shape, scratch_shapes=[pltpu.VMEM((tm, tn), F32)] if nk > 1 else [],
        compiler_params=_params(("parallel", "parallel", "arbitrary")),
    )(*args)


ROW_BLOCK = 256


def _rmsnorm_fwd(x, g, *, name, deps=()):
    T, D = x.shape
    tr = _tile(T, (ROW_BLOCK, 128, 64))

    def body(x_ref, g_ref, *rest):
        o_ref = rest[-1]
        xv = x_ref[...]
        r = lax.rsqrt(jnp.mean(xv * xv, axis=-1, keepdims=True) + NORM_EPS)
        o_ref[...] = (xv * r * g_ref[...]).astype(o_ref.dtype)

    return pl.pallas_call(
        body, name=name, grid=(T // tr,),
        in_specs=[pl.BlockSpec((tr, D), lambda i: (i, 0)), pl.BlockSpec((1, D), lambda i: (0, 0))] + _dep_specs(deps),
        out_specs=pl.BlockSpec((tr, D), lambda i: (i, 0)),
        out_shape=jax.ShapeDtypeStruct((T, D), BF16), compiler_params=_params(("parallel",)),
    )(x, g, *deps)


def _rmsnorm_bwd(x, g, dy, dres, *, name, deps=()):
    T, D = x.shape
    tr = _tile(T, (ROW_BLOCK, 128, 64))

    def body(x_ref, g_ref, dy_ref, dres_ref, *rest):
        dx_ref, dxb_ref, dg_ref = rest[len(deps):]
        i = pl.program_id(0)
        xv = x_ref[...]
        r = lax.rsqrt(jnp.mean(xv * xv, axis=-1, keepdims=True) + NORM_EPS)
        xh = xv * r
        dyv = dy_ref[...].astype(F32)
        dxh = dyv * g_ref[...]
        dx = dres_ref[...] + r * (dxh - xh * jnp.mean(dxh * xh, axis=-1, keepdims=True))
        dx_ref[...] = dx
        dxb_ref[...] = dx.astype(dxb_ref.dtype)

        @pl.when(i == 0)
        def _():
            dg_ref[...] = jnp.zeros_like(dg_ref)

        dg_ref[...] += jnp.sum(dyv * xh, axis=0, keepdims=True)

    blk = pl.BlockSpec((tr, D), lambda i: (i, 0))
    vec = pl.BlockSpec((1, D), lambda i: (0, 0))
    return pl.pallas_call(
        body, name=name, grid=(T // tr,), in_specs=[blk, vec, blk, blk] + _dep_specs(deps), out_specs=(blk, blk, vec),
        out_shape=(jax.ShapeDtypeStruct((T, D), F32), jax.ShapeDtypeStruct((T, D), BF16),
                   jax.ShapeDtypeStruct((1, D), F32)),
        compiler_params=_params(("arbitrary",)),
    )(x, g, dy, dres, *deps)


def _final_loss(x, g, target, *, name):
    T, D = x.shape
    tr = _tile(T, (ROW_BLOCK, 128, 64))

    def body(x_ref, g_ref, t_ref, dy_ref, l_ref):
        i = pl.program_id(0)
        xv = x_ref[...]
        r = lax.rsqrt(jnp.mean(xv * xv, axis=-1, keepdims=True) + NORM_EPS)
        err = xv * r * g_ref[...] - t_ref[...]
        dy_ref[...] = err * (1.0 / D)

        @pl.when(i == 0)
        def _():
            l_ref[...] = jnp.zeros_like(l_ref)

        part = 0.5 * jnp.sum(jnp.mean(err * err, axis=-1, keepdims=True), axis=0, keepdims=True)
        l_ref[...] += jnp.broadcast_to(part, l_ref.shape)

    blk = pl.BlockSpec((tr, D), lambda i: (i, 0))
    vec = pl.BlockSpec((1, D), lambda i: (0, 0))
    return pl.pallas_call(
        body, name=name, grid=(T // tr,), in_specs=[blk, vec, blk],
        out_specs=(blk, pl.BlockSpec((1, LANE), lambda i: (0, 0))),
        out_shape=(jax.ShapeDtypeStruct((T, D), F32), jax.ShapeDtypeStruct((1, LANE), F32)),
        compiler_params=_params(("arbitrary",)),
    )(x, g, target)


def _colsum(x, *, name):
    T, C = x.shape
    tc = _tile(C, (512, 256, 128))

    def body(x_ref, o_ref):
        o_ref[...] = jnp.sum(x_ref[...], axis=0, keepdims=True)

    return pl.pallas_call(
        body, name=name, grid=(C // tc,), in_specs=[pl.BlockSpec((T, tc), lambda j: (0, j))],
        out_specs=pl.BlockSpec((1, tc), lambda j: (0, j)),
        out_shape=jax.ShapeDtypeStruct((1, C), F32), compiler_params=_params(("parallel",)),
    )(x)


ADAM_C1 = 1.0 - ADAM_B1 ** ADAM_STEP
ADAM_C2 = 1.0 - ADAM_B2 ** ADAM_STEP


def _adamw(w, m, v, grads, *, name):
    R, C = w.shape
    tr = _tile(R, (256, 128, 64, 32, 16, 8))
    n_g = len(grads)

    def body(*refs):
        w_ref, m_ref, v_ref = refs[:3]
        g_refs = refs[3:3 + n_g]
        g_out, d_out, m_out, v_out = refs[3 + n_g:]
        g = g_refs[0][...].astype(F32)
        for r in g_refs[1:]:
            g = g + r[...].astype(F32)
        mn = ADAM_B1 * m_ref[...] + (1.0 - ADAM_B1) * g
        vn = ADAM_B2 * v_ref[...] + (1.0 - ADAM_B2) * (g * g)
        m_hat = mn / ADAM_C1
        v_hat = vn / ADAM_C2
        g_out[...] = g
        d_out[...] = -ADAM_LR * (m_hat / (jnp.sqrt(v_hat) + ADAM_EPS) + ADAM_WD * w_ref[...])
        m_out[...] = mn
        v_out[...] = vn

    blk = pl.BlockSpec((tr, C), lambda i: (i, 0))
    sh = jax.ShapeDtypeStruct((R, C), F32)
    return pl.pallas_call(
        body, name=name, grid=(R // tr,), in_specs=[blk] * (3 + n_g), out_specs=(blk,) * 4,
        out_shape=(sh,) * 4, compiler_params=_params(("parallel",)),
    )(w, m, v, *grads)


def _sum_parts(parts, *, name, deps=()):
    P, R, C = parts.shape
    tr = _tile(R, (256, 128, 64, 32, 16, 8))

    def body(p_ref, *rest):
        o_ref = rest[-1]
        s = p_ref[0].astype(F32)
        for i in range(1, P):
            s = s + p_ref[i].astype(F32)
        o_ref[...] = s

    return pl.pallas_call(
        body, name=name, grid=(R // tr,),
        in_specs=[pl.BlockSpec((P, tr, C), lambda i: (0, i, 0))] + _dep_specs(deps),
        out_specs=pl.BlockSpec((tr, C), lambda i: (i, 0)),
        out_shape=jax.ShapeDtypeStruct((R, C), F32), compiler_params=_params(("parallel",)),
    )(parts, *deps)


def _ret_tables(T):
    H, C = RET_HEADS, CHUNK
    log_gamma = jnp.log1p(-jnp.exp2(-5.0 - jnp.arange(H, dtype=F32)))
    pos = jnp.arange(C, dtype=F32)
    dist = jnp.abs(pos[:, None] - pos[None, :])
    dm = jnp.exp(log_gamma[:, None, None] * dist)
    qdec = jnp.exp(log_gamma[:, None] * (pos + 1.0))[:, :, None]
    kdec = jnp.exp(log_gamma[:, None] * (C - 1.0 - pos))[:, :, None]
    cdec = jnp.exp(log_gamma * C)[:, None, None]
    inv = ROPE_BASE ** (-jnp.arange(0, RET_DK, 2, dtype=F32) / RET_DK)
    ang = jnp.arange(T, dtype=F32)[:, None] * inv[None, :]
    return dm, qdec, kdec, cdec, jnp.cos(ang), jnp.sin(ang)


def _rot(x, cos, sin):
    h = x.shape[1] // 2
    x1, x2 = x[:, :h], x[:, h:]
    return jnp.concatenate([x1 * cos - x2 * sin, x1 * sin + x2 * cos], axis=1)


def _unrot(dy, cos, sin):
    h = dy.shape[1] // 2
    d1, d2 = dy[:, :h], dy[:, h:]
    return jnp.concatenate([d1 * cos + d2 * sin, d2 * cos - d1 * sin], axis=1)


def _ret_specs(N, rev):
    H, C, DK, DV = RET_HEADS, CHUNK, RET_DK, RET_DV
    cn = (lambda n: N - 1 - n) if rev else (lambda n: n)
    HB, G = RET_HB, H // RET_HB
    q = pl.BlockSpec((C, HB * DK), lambda h, n: (cn(n), h))
    k = pl.BlockSpec((C, HB * DK), lambda h, n: (cn(n), G + h))
    v = pl.BlockSpec((C, HB * DV), lambda h, n: (cn(n), G + h))
    g = pl.BlockSpec((C, HB * DV), lambda h, n: (cn(n), 2 * G + h))
    cs = pl.BlockSpec((C, DK // 2), lambda h, n: (cn(n), 0))
    dm = pl.BlockSpec((HB, C, C), lambda h, n: (h, 0, 0))
    dec = pl.BlockSpec((HB, C, 1), lambda h, n: (h, 0, 0))
    cd = pl.BlockSpec((HB, 1, 1), lambda h, n: (h, 0, 0))
    gain = pl.BlockSpec((HB, 1, DV), lambda h, n: (h, 0, 0))
    st = pl.BlockSpec((HB, 1, DK, DV), lambda h, n: (h, cn(n), 0, 0))
    ov = pl.BlockSpec((C, HB * DV), lambda h, n: (cn(n), h))
    return q, k, v, g, cs, dm, dec, cd, gain, st, ov


def _ret_fwd(proj, gain, tables, *, name):
    T = proj.shape[0]
    H, C, DK, DV = RET_HEADS, CHUNK, RET_DK, RET_DV
    N = T // C
    dm_t, qdec_t, kdec_t, cdec_t, cos_t, sin_t = tables

    def body(q_ref, k_ref, v_ref, g_ref, cos_ref, sin_ref, dm_ref, qd_ref, kd_ref, cd_ref, gain_ref,
             og_ref, st_ref, s_acc):
        n = pl.program_id(1)

        @pl.when(n == 0)
        def _():
            s_acc[...] = jnp.zeros_like(s_acc)

        cos, sin = cos_ref[...], sin_ref[...]

        def head(hb):
            kc, vc = pl.ds(hb * DK, DK), pl.ds(hb * DV, DV)
            qr = _rot(q_ref[:, kc], cos, sin)
            kr = _rot(k_ref[:, kc], cos, sin) * (DK ** -0.5)
            v = v_ref[:, vc]
            sp = s_acc[hb]
            st_ref[hb, 0] = sp.astype(st_ref.dtype)
            scores = _bdot(qr, kr, NT) * dm_ref[hb]
            inter = _bdot(qr * qd_ref[hb], sp)
            s_acc[hb] = sp * cd_ref[hb] + _bdot(kr * kd_ref[hb], v, TN)
            yield
            o = _bdot(scores, v) + inter
            yield
            oc = o - jnp.mean(o, axis=-1, keepdims=True)
            rstd = lax.rsqrt(jnp.mean(oc * oc, axis=-1, keepdims=True) + NORM_EPS)
            og_ref[:, vc] = (oc * rstd * gain_ref[hb] * _silu(g_ref[:, vc])).astype(og_ref.dtype)

        _interleave([head(hb) for hb in range(RET_HB)])

    q, k, v, g, cs, dm, dec, cd, gn, st, ov = _ret_specs(N, False)
    return pl.pallas_call(
        body, name=name, grid=(H // RET_HB, N),
        in_specs=[q, k, v, g, cs, cs, dm, dec, dec, cd, gn], out_specs=(ov, st),
        out_shape=(jax.ShapeDtypeStruct((T, H * DV), BF16), jax.ShapeDtypeStruct((H, N, DK, DV), BF16)),
        scratch_shapes=[pltpu.VMEM((RET_HB, DK, DV), F32)],
        compiler_params=_params(("arbitrary", "arbitrary")),
    )(proj, proj, proj, proj, cos_t, sin_t, dm_t, qdec_t, kdec_t, cdec_t, gain)


def _ret_bwd(proj, gain, tables, states, dog, *, name):
    T = proj.shape[0]
    H, C, DK, DV = RET_HEADS, CHUNK, RET_DK, RET_DV
    N = T // C
    dm_t, qdec_t, kdec_t, cdec_t, cos_t, sin_t = tables

    def body(q_ref, k_ref, v_ref, g_ref, cos_ref, sin_ref, dm_ref, qd_ref, kd_ref, cd_ref, gain_ref,
             st_ref, dog_ref, dq_ref, dk_ref, dv_ref, dg_ref, dgain_ref, ds_acc):
        n = pl.program_id(1)

        @pl.when(n == 0)
        def _():
            ds_acc[...] = jnp.zeros_like(ds_acc)
            dgain_ref[...] = jnp.zeros_like(dgain_ref)

        cos, sin = cos_ref[...], sin_ref[...]
        scale = DK ** -0.5

        def head(hb):
            kc, vc = pl.ds(hb * DK, DK), pl.ds(hb * DV, DV)
            qr = _rot(q_ref[:, kc], cos, sin)
            kr = _rot(k_ref[:, kc], cos, sin) * scale
            v = v_ref[:, vc]
            g = g_ref[:, vc]
            sp = st_ref[hb, 0]
            dm = dm_ref[hb]
            qd, kd = qd_ref[hb], kd_ref[hb]
            gain_v = gain_ref[hb]
            scores = _bdot(qr, kr, NT) * dm
            inter = _bdot(qr * qd, sp)
            yield
            o = _bdot(scores, v) + inter
            yield
            oc = o - jnp.mean(o, axis=-1, keepdims=True)
            rstd = lax.rsqrt(jnp.mean(oc * oc, axis=-1, keepdims=True) + NORM_EPS)
            oh = oc * rstd
            dy = dog_ref[:, vc].astype(F32)
            dg_ref[:, vc] = (dy * oh * gain_v * _dsilu(g)).astype(dg_ref.dtype)
            dnorm = dy * _silu(g)
            dgain_ref[hb] += jnp.sum(dnorm * oh, axis=0, keepdims=True)
            doh = dnorm * gain_v
            do = rstd * (doh - jnp.mean(doh, axis=-1, keepdims=True)
                         - oh * jnp.mean(doh * oh, axis=-1, keepdims=True))
            dsn = ds_acc[hb]
            dp = _bdot(do, v, NT) * dm
            dq_inter = _bdot(do, sp, NT) * qd
            dk_inter = _bdot(v, dsn, NT) * kd
            dv_ref[:, vc] = (_bdot(scores, do, TN) + _bdot(kr * kd, dsn)).astype(dv_ref.dtype)
            ds_acc[hb] = dsn * cd_ref[hb] + _bdot(qr * qd, do, TN)
            yield
            dqr = _bdot(dp, kr) + dq_inter
            dkr = _bdot(dp, qr, TN) + dk_inter
            yield
            dq_ref[:, kc] = _unrot(dqr, cos, sin).astype(dq_ref.dtype)
            dk_ref[:, kc] = _unrot(dkr * scale, cos, sin).astype(dk_ref.dtype)

        _interleave([head(hb) for hb in range(RET_HB)])

    q, k, v, g, cs, dm, dec, cd, gn, st, ov = _ret_specs(N, True)
    return pl.pallas_call(
        body, name=name, grid=(H // RET_HB, N),
        in_specs=[q, k, v, g, cs, cs, dm, dec, dec, cd, gn, st, ov],
        out_specs=(q, q, ov, ov, gn),
        out_shape=(jax.ShapeDtypeStruct((T, H * DK), BF16), jax.ShapeDtypeStruct((T, H * DK), BF16),
                   jax.ShapeDtypeStruct((T, H * DV), BF16), jax.ShapeDtypeStruct((T, H * DV), BF16),
                   jax.ShapeDtypeStruct((H, 1, DV), F32)),
        scratch_shapes=[pltpu.VMEM((RET_HB, DK, DV), F32)],
        compiler_params=_params(("arbitrary", "arbitrary")),
    )(proj, proj, proj, proj, cos_t, sin_t, dm_t, qdec_t, kdec_t, cdec_t, gain, states, dog)


def _gla_specs(N, rev):
    H, C, DK, DV = GLA_HEADS, CHUNK, GLA_DK, GLA_DV
    cn = (lambda n: N - 1 - n) if rev else (lambda n: n)
    HB, G = GLA_HB, H // GLA_HB
    q = pl.BlockSpec((C, HB * DK), lambda h, n: (cn(n), h))
    k = pl.BlockSpec((C, HB * DK), lambda h, n: (cn(n), G + h))
    v = pl.BlockSpec((C, HB * DV), lambda h, n: (cn(n), G + h))
    r = pl.BlockSpec((C, HB * DV), lambda h, n: (cn(n), 2 * G + h))
    bias = pl.BlockSpec((1, HB * DK), lambda h, n: (0, h))
    gain = pl.BlockSpec((HB, 1, DV), lambda h, n: (h, 0, 0))
    st = pl.BlockSpec((HB, 1, DV, DK), lambda h, n: (h, cn(n), 0, 0))
    ov = pl.BlockSpec((C, HB * DV), lambda h, n: (cn(n), h))
    return q, k, v, r, bias, gain, st, ov


def _gla_chunk(q, k, v, gl_raw, bias):
    C, DK = q.shape
    gl = gl_raw + bias
    la = -_softplus(-gl) * (1.0 / GLA_TAU)
    lower = _iota2((C, C), 0) >= _iota2((C, C), 1)
    cum = _fdot(jnp.where(lower, 1.0, 0.0), la)
    yield
    ref = _pick_row(cum, C // 2 - 1)
    clast = _pick_row(cum, C - 1)
    fw, bw = jnp.exp(cum - ref), jnp.exp(ref - cum)
    qs = q * (DK ** -0.5)
    s_lo = _bdot(qs * fw, k * bw, NT)
    s_up = _bdot(qs * bw, k * fw, NT)
    yield
    scores = jnp.where(lower, s_lo, s_up)
    return gl, cum, clast, fw, bw, qs, k, v, scores, lower


def _gla_fwd(proj, glogit, bias, gain, *, name):
    T = proj.shape[0]
    H, C, DK, DV = GLA_HEADS, CHUNK, GLA_DK, GLA_DV
    N = T // C

    def body(q_ref, k_ref, v_ref, r_ref, gl_ref, bias_ref, gain_ref, og_ref, st_ref, s_acc):
        n = pl.program_id(1)

        @pl.when(n == 0)
        def _():
            s_acc[...] = jnp.zeros_like(s_acc)

        def head(hb):
            kc, vc = pl.ds(hb * DK, DK), pl.ds(hb * DV, DV)
            gl, cum, clast, fw, bw, qs, k, v, scores, lower = yield from _gla_chunk(
                q_ref[:, kc], k_ref[:, kc], v_ref[:, vc], gl_ref[:, kc], bias_ref[:, kc])
            sp = s_acc[hb]
            st_ref[hb, 0] = sp
            o = _bdot(scores, v) + _bdot(qs * jnp.exp(cum), sp, NT)
            s_acc[hb] = sp * jnp.exp(clast) + _bdot(v, k * jnp.exp(clast - cum), TN)
            yield
            rstd = lax.rsqrt(jnp.mean(o * o, axis=-1, keepdims=True) + NORM_EPS)
            og_ref[:, vc] = (o * rstd * gain_ref[hb] * _silu(r_ref[:, vc])).astype(og_ref.dtype)

        _interleave([head(hb) for hb in range(GLA_HB)])

    q, k, v, r, bias_s, gn, st, ov = _gla_specs(N, False)
    return pl.pallas_call(
        body, name=name, grid=(H // GLA_HB, N), in_specs=[q, k, v, r, q, bias_s, gn], out_specs=(ov, st),
        out_shape=(jax.ShapeDtypeStruct((T, H * DV), BF16), jax.ShapeDtypeStruct((H, N, DV, DK), F32)),
        scratch_shapes=[pltpu.VMEM((GLA_HB, DV, DK), F32)],
        compiler_params=_params(("arbitrary", "arbitrary")),
    )(proj, proj, proj, proj, glogit, bias, gain)


def _gla_bwd(proj, glogit, bias, gain, states, dog, *, name):
    T = proj.shape[0]
    H, C, DK, DV = GLA_HEADS, CHUNK, GLA_DK, GLA_DV
    N = T // C

    def body(q_ref, k_ref, v_ref, r_ref, gl_ref, bias_ref, gain_ref, st_ref, dog_ref,
             dq_ref, dk_ref, dv_ref, dr_ref, dgl_ref, dgain_ref, ds_acc):
        n = pl.program_id(1)

        @pl.when(n == 0)
        def _():
            ds_acc[...] = jnp.zeros_like(ds_acc)
            dgain_ref[...] = jnp.zeros_like(dgain_ref)

        def head(hb):
            kc, vc = pl.ds(hb * DK, DK), pl.ds(hb * DV, DV)
            gl, cum, clast, fw, bw, qs, k, v, scores, lower = yield from _gla_chunk(
                q_ref[:, kc], k_ref[:, kc], v_ref[:, vc], gl_ref[:, kc], bias_ref[:, kc])
            sp = st_ref[hb, 0]
            ecum, e2, cdec = jnp.exp(cum), jnp.exp(clast - cum), jnp.exp(clast)
            q_in, k_end = qs * ecum, k * e2
            o = _bdot(scores, v) + _bdot(q_in, sp, NT)
            yield
            rstd = lax.rsqrt(jnp.mean(o * o, axis=-1, keepdims=True) + NORM_EPS)
            oh = o * rstd
            r = r_ref[:, vc]
            gain_v = gain_ref[hb]
            dy = dog_ref[:, vc].astype(F32)
            dr_ref[:, vc] = (dy * oh * gain_v * _dsilu(r)).astype(dr_ref.dtype)
            dnorm = dy * _silu(r)
            dgain_ref[hb] += jnp.sum(dnorm * oh, axis=0, keepdims=True)
            doh = dnorm * gain_v
            do = rstd * (doh - oh * jnp.mean(doh * oh, axis=-1, keepdims=True))
            dsn = ds_acc[hb]
            dq_in = _bdot(do, sp)
            dk_end = _bdot(v, dsn)
            dv_ref[:, vc] = (_bdot(k_end, dsn, NT) + _bdot(scores, do, TN)).astype(dv_ref.dtype)
            dcdec = jnp.sum(dsn * sp, axis=0, keepdims=True)
            ds_acc[hb] = dsn * cdec + _bdot(do, q_in, TN)
            dsc = _bdot(do, v, NT)
            yield
            ds_lo = jnp.where(lower, dsc, 0.0)
            ds_up = jnp.where(lower, 0.0, dsc)
            qf, kb, qb, kf = qs * fw, k * bw, qs * bw, k * fw
            dqf, dkb = _bdot(ds_lo, kb), _bdot(ds_lo, qf, TN)
            dqb, dkf = _bdot(ds_up, kf), _bdot(ds_up, qb, TN)
            yield
            dq_ref[:, kc] = ((dqf * fw + dqb * bw + dq_in * ecum) * (DK ** -0.5)).astype(dq_ref.dtype)
            dk_ref[:, kc] = (dkb * bw + dkf * fw + dk_end * e2).astype(dk_ref.dtype)
            dz = (dqf * qs + dkf * k) * fw - (dqb * qs + dkb * k) * bw
            kk = dk_end * k_end
            dcum = dz + dq_in * q_in - kk
            rows = _iota2((C, DK), 0)
            dcum = dcum + jnp.where(rows == C // 2 - 1, -jnp.sum(dz, axis=0, keepdims=True), 0.0)
            dcum = dcum + jnp.where(rows == C - 1, jnp.sum(kk, axis=0, keepdims=True) + dcdec * cdec, 0.0)
            upper = _iota2((C, C), 0) <= _iota2((C, C), 1)
            dla = _fdot(jnp.where(upper, 1.0, 0.0), dcum)
            yield
            dgl_ref[:, kc] = dla * (1.0 / GLA_TAU) * _sigmoid(-gl)

        _interleave([head(hb) for hb in range(GLA_HB)])

    q, k, v, r, bias_s, gn, st, ov = _gla_specs(N, True)
    return pl.pallas_call(
        body, name=name, grid=(H // GLA_HB, N), in_specs=[q, k, v, r, q, bias_s, gn, st, ov],
        out_specs=(q, q, ov, ov, q, gn),
        out_shape=(jax.ShapeDtypeStruct((T, H * DK), BF16), jax.ShapeDtypeStruct((T, H * DK), BF16),
                   jax.ShapeDtypeStruct((T, H * DV), BF16), jax.ShapeDtypeStruct((T, H * DV), BF16),
                   jax.ShapeDtypeStruct((T, H * DK), F32), jax.ShapeDtypeStruct((H, 1, DV), F32)),
        scratch_shapes=[pltpu.VMEM((GLA_HB, DV, DK), F32)],
        compiler_params=_params(("arbitrary", "arbitrary")),
    )(proj, proj, proj, proj, glogit, bias, gain, states, dog)


def _conv(xv, w_ref):
    out = _shift_down(xv, CONV_WIDTH - 1) * w_ref[0:1, :]
    for tap in range(1, CONV_WIDTH):
        out = out + _shift_down(xv, CONV_WIDTH - 1 - tap) * w_ref[tap:tap + 1, :]
    return out


def _conv_bwd(xv, w_ref, dpre, dw_ref):
    dx = None
    for tap in range(CONV_WIDTH):
        s = CONV_WIDTH - 1 - tap
        t = _shift_up(dpre, s) * w_ref[tap:tap + 1, :]
        dx = t if dx is None else dx + t
        dw_ref[tap:tap + 1, :] = jnp.sum(dpre * _shift_down(xv, s), axis=0, keepdims=True)
    return dx


CONV_COLS = 256


def _conv_silu_fwd(x, w, *, name):
    T = x.shape[0]
    n = w.shape[1]

    def body(x_ref, w_ref, o_ref):
        o_ref[...] = _silu(_conv(x_ref[...], w_ref))

    return pl.pallas_call(
        body, name=name, grid=(n // CONV_COLS,),
        in_specs=[pl.BlockSpec((T, CONV_COLS), lambda j: (0, j)), pl.BlockSpec((CONV_WIDTH, CONV_COLS), lambda j: (0, j))],
        out_specs=pl.BlockSpec((T, CONV_COLS), lambda j: (0, j)),
        out_shape=jax.ShapeDtypeStruct((T, n), F32), compiler_params=_params(("parallel",)),
    )(x, w)


def _conv_silu_bwd(x, w, dact, *, name):
    T = x.shape[0]
    n = w.shape[1]

    def body(x_ref, w_ref, da_ref, dx_ref, dw_ref):
        xv = x_ref[...]
        dpre = da_ref[...] * _dsilu(_conv(xv, w_ref))
        dx_ref[...] = _conv_bwd(xv, w_ref, dpre, dw_ref).astype(dx_ref.dtype)

    blk = pl.BlockSpec((T, CONV_COLS), lambda j: (0, j))
    wb = pl.BlockSpec((CONV_WIDTH, CONV_COLS), lambda j: (0, j))
    return pl.pallas_call(
        body, name=name, grid=(n // CONV_COLS,), in_specs=[blk, wb, blk], out_specs=(blk, wb),
        out_shape=(jax.ShapeDtypeStruct((T, n), BF16), jax.ShapeDtypeStruct((CONV_WIDTH, n), F32)),
        compiler_params=_params(("parallel",)),
    )(x, w, dact)


def _interleave(gens):
    results = [None] * len(gens)
    live = list(range(len(gens)))
    while live:
        for i in list(live):
            try:
                next(gens[i])
            except StopIteration as done:
                results[i] = done.value
                live.remove(i)
    return results


def _unit_lower_inverse(a):
    n = a.shape[0]
    eye = jnp.where(_iota2((n, n), 0) == _iota2((n, n), 1), 1.0, 0.0)
    p = -a
    t = eye + p
    for _ in range(5):
        p = _fdot(p, p)
        yield
        t = t + _fdot(t, p)
        yield
    return t


def _gdn_specs(N, rev):
    H, C, DK, DV = GDN_HEADS, CHUNK, GDN_DK, GDN_DV
    cn = (lambda n: N - 1 - n) if rev else (lambda n: n)
    HB, G = GDN_HB, H // GDN_HB
    q = pl.BlockSpec((C, HB * DK), lambda h, n: (cn(n), h))
    k = pl.BlockSpec((C, HB * DK), lambda h, n: (cn(n), G + h))
    v = pl.BlockSpec((C, HB * DV), lambda h, n: (cn(n), 2 * G + h))
    z = pl.BlockSpec((C, HB * DV), lambda h, n: (cn(n), 3 * G + h))
    gates = pl.BlockSpec((C, LANE), lambda h, n: (cn(n), 0))
    sc = pl.BlockSpec((HB, 1, 1), lambda h, n: (h, 0, 0))
    gain = pl.BlockSpec((1, DV), lambda h, n: (0, 0))
    st = pl.BlockSpec((HB, 1, DK, DV), lambda h, n: (h, cn(n), 0, 0))
    return q, k, v, z, gates, sc, gain, st


def _gdn_chunk(q_ref, k_ref, v_ref, gates_ref, alog_ref, dtb_ref, h):
    H, C, DK, DV = GDN_HEADS, CHUNK, GDN_DK, GDN_DV
    gates = gates_ref[...]
    lane = _iota2(gates.shape, 1)
    bl = jnp.sum(jnp.where(lane == h, gates, 0.0), axis=1, keepdims=True)
    al = jnp.sum(jnp.where(lane == H + h, gates, 0.0), axis=1, keepdims=True)
    beta = _sigmoid(bl)
    ea = jnp.exp(alog_ref[...])
    xs = al + dtb_ref[...]
    la = -ea * _softplus(xs)
    ii, jj = _iota2((C, C), 0), _iota2((C, C), 1)
    strict = ii > jj
    cum_col = jnp.sum(jnp.where(ii >= jj, _col_to_row(la), 0.0), axis=1, keepdims=True)
    cum_row = jnp.sum(jnp.where(ii <= jj, la, 0.0), axis=0, keepdims=True)
    q, k, v = q_ref[...], k_ref[...], v_ref[...]
    rq = lax.rsqrt(jnp.sum(q * q, axis=-1, keepdims=True) + NORM_EPS)
    rk = lax.rsqrt(jnp.sum(k * k, axis=-1, keepdims=True) + NORM_EPS)
    qn = q * rq * (DK ** -0.5)
    kn = k * rk
    rel = jnp.where(strict, jnp.exp(jnp.where(strict, cum_col - cum_row, 0.0)), 0.0)
    rg = rel * _bdot(kn, kn, NT)
    yield
    a = beta * rg
    tm = yield from _unit_lower_inverse(a)
    e_col = jnp.exp(cum_col)
    clast = _pick_row(cum_col, C - 1)
    rhs = jnp.concatenate([beta * v, (beta * e_col) * kn], axis=1)
    sol = _fdot(tm, rhs)
    yield
    u, w = sol[:, :DV], sol[:, DV:]
    dd = jnp.exp(clast - cum_col)
    ke = kn * dd
    g = jnp.exp(clast)
    eye = jnp.where(_iota2((DK, DK), 0) == _iota2((DK, DK), 1), 1.0, 0.0)
    trans = g * eye - _bdot(ke, w, TN)
    inject = _bdot(ke, u, TN)
    yield
    return dict(beta=beta, ea=ea, xs=xs, la=la, strict=strict, ii=ii, jj=jj, q=q, k=k, v=v, rq=rq, rk=rk,
                qn=qn, kn=kn, rel=rel, rg=rg, a=a, tm=tm, e_col=e_col, sol=sol, u=u, w=w, dd=dd, ke=ke,
                g=g, eye=eye, trans=trans, inject=inject)


def _gdn_fwd(act, proj, gates, a_log, dt_bias, gain, *, name):
    T = act.shape[0]
    H, C, DK, DV = GDN_HEADS, CHUNK, GDN_DK, GDN_DV
    N = T // C

    def body(q_ref, k_ref, v_ref, z_ref, gates_ref, alog_ref, dtb_ref, gain_ref, og_ref, st_ref, s_acc):
        hg, n = pl.program_id(0), pl.program_id(1)

        @pl.when(n == 0)
        def _():
            s_acc[...] = jnp.zeros_like(s_acc)

        def head(hb):
            cols = pl.ds(hb * DK, DK)
            c = yield from _gdn_chunk(q_ref.at[:, cols], k_ref.at[:, cols], v_ref.at[:, cols], gates_ref,
                                      alog_ref.at[hb], dtb_ref.at[hb], hg * GDN_HB + hb)
            sp = s_acc[hb]
            st_ref[hb, 0] = sp
            snew = _bdot(c["trans"], sp) + c["inject"]
            yield
            s_acc[hb] = snew
            o = _bdot(c["qn"], snew)
            yield
            rstd = lax.rsqrt(jnp.mean(o * o, axis=-1, keepdims=True) + NORM_EPS)
            og_ref[:, cols] = (o * rstd * gain_ref[...] * _silu(z_ref[:, cols])).astype(og_ref.dtype)

        _interleave([head(hb) for hb in range(GDN_HB)])

    q, k, v, z, gt, sc, gn, st = _gdn_specs(N, False)
    return pl.pallas_call(
        body, name=name, grid=(H // GDN_HB, N), in_specs=[q, k, v, z, gt, sc, sc, gn], out_specs=(q, st),
        out_shape=(jax.ShapeDtypeStruct((T, H * DV), BF16), jax.ShapeDtypeStruct((H, N, DK, DV), F32)),
        scratch_shapes=[pltpu.VMEM((GDN_HB, DK, DV), F32)],
        compiler_params=_params(("arbitrary", "arbitrary")),
    )(act, act, act, proj, gates, a_log, dt_bias, gain)


def _gdn_bwd(act, proj, gates, a_log, dt_bias, gain, states, dog, *, name):
    T = act.shape[0]
    H, C, DK, DV = GDN_HEADS, CHUNK, GDN_DK, GDN_DV
    N = T // C

    def rsum(x):
        return jnp.sum(x, axis=1, keepdims=True)

    def body(q_ref, k_ref, v_ref, z_ref, gates_ref, alog_ref, dtb_ref, gain_ref, st_ref, dog_ref,
             dq_ref, dk_ref, dv_ref, dz_ref, dgates_ref, dscal_ref, dgain_ref, ds_acc):
        hg, n = pl.program_id(0), pl.program_id(1)

        @pl.when(n == 0)
        def _():
            ds_acc[...] = jnp.zeros_like(ds_acc)
            dgain_ref[...] = jnp.zeros_like(dgain_ref)
            dscal_ref[...] = jnp.zeros_like(dscal_ref)

        _interleave([one_head(hb, hg * GDN_HB + hb, q_ref, k_ref, v_ref, z_ref, gates_ref, alog_ref, dtb_ref, gain_ref,
                              st_ref, dog_ref, dq_ref, dk_ref, dv_ref, dz_ref, dgates_ref, dscal_ref, dgain_ref, ds_acc)
                     for hb in range(GDN_HB)])

    def one_head(hb, h, q_ref, k_ref, v_ref, z_ref, gates_ref, alog_ref, dtb_ref, gain_ref, st_ref, dog_ref,
                 dq_ref, dk_ref, dv_ref, dz_ref, dgates_ref, dscal_ref, dgain_ref, ds_acc):
        cols = pl.ds(hb * DK, DK)
        c = yield from _gdn_chunk(q_ref.at[:, cols], k_ref.at[:, cols], v_ref.at[:, cols], gates_ref,
                                  alog_ref.at[hb], dtb_ref.at[hb], h)
        beta, kn, qn, v, ke, u, w, dd, e_col = c["beta"], c["kn"], c["qn"], c["v"], c["ke"], c["u"], c["w"], c["dd"], c["e_col"]
        sp = st_ref[hb, 0]
        snew = _bdot(c["trans"], sp) + c["inject"]
        yield
        o = _bdot(qn, snew)
        yield
        rstd = lax.rsqrt(jnp.mean(o * o, axis=-1, keepdims=True) + NORM_EPS)
        oh = o * rstd
        z = z_ref[:, cols]
        gain_v = gain_ref[...]
        dy = dog_ref[:, cols].astype(F32)
        dz_ref[:, cols] = (dy * oh * gain_v * _dsilu(z)).astype(dz_ref.dtype)
        dnorm = dy * _silu(z)
        dgain_ref[hb] += jnp.sum(dnorm * oh, axis=0, keepdims=True)
        doh = dnorm * gain_v
        do = rstd * (doh - oh * jnp.mean(doh * oh, axis=-1, keepdims=True))

        dstot = ds_acc[hb] + _bdot(qn, do, TN)
        dqn = _bdot(do, snew, NT)
        yield
        dtrans = _bdot(dstot, sp, NT)
        ds_acc[hb] = _bdot(c["trans"], dstot, TN)
        yield
        dg = jnp.sum(jnp.sum(dtrans * c["eye"], axis=1, keepdims=True), axis=0, keepdims=True)
        m = -dtrans
        dke = _bdot(w, m, NT) + _bdot(u, dstot, NT)
        dw = _bdot(ke, m)
        du = _bdot(ke, dstot)
        yield
        drhs = _fdot(c["tm"], jnp.concatenate([du, dw], axis=1), TN)
        yield
        da = jnp.where(c["strict"], -_fdot(drhs, c["sol"], NT), 0.0)
        yield
        drhs_u, drhs_w = drhs[:, :DV], drhs[:, DV:]
        rwk = rsum(drhs_w * kn)
        dbeta = rsum(da * c["rg"]) + rsum(drhs_u * v) + rwk * e_col
        dgm = da * beta * c["rel"]
        dkn = _bdot(dgm, kn) + _bdot(dgm, kn, TN) + (beta * e_col) * drhs_w + dd * dke
        yield
        dv_ref[:, cols] = beta * drhs_u
        r_ = da * c["a"]
        ddd = rsum(dke * kn)
        dc = rsum(r_) - _row_to_col(jnp.sum(r_, axis=0, keepdims=True)) + beta * rwk * e_col - ddd * dd
        dclast = jnp.sum(ddd * dd, axis=0, keepdims=True) + dg * c["g"]
        dc = dc + jnp.where(_iota2((C, 1), 0) == C - 1, dclast, 0.0)
        dla = jnp.sum(jnp.where(c["ii"] <= c["jj"], _col_to_row(dc), 0.0), axis=1, keepdims=True)
        dalog = jnp.sum(dla * c["la"], axis=0, keepdims=True)
        dxs = dla * (-c["ea"]) * _sigmoid(c["xs"])
        ddtb = jnp.sum(dxs, axis=0, keepdims=True)
        dbl = dbeta * beta * (1.0 - beta)
        lane = _iota2((C, LANE), 1)
        dgates_ref[hb] = jnp.where(lane == 0, dbl, jnp.where(lane == 1, dxs, 0.0))
        lane8 = _iota2((8, LANE), 1)
        dscal_ref[hb] += jnp.where(lane8 == 0, dalog, jnp.where(lane8 == 1, ddtb, 0.0))
        dk_ref[:, cols] = c["rk"] * (dkn - kn * rsum(dkn * kn))
        qh = c["q"] * c["rq"]
        dqs = dqn * (DK ** -0.5)
        dq_ref[:, cols] = c["rq"] * (dqs - qh * rsum(dqs * qh))

    q, k, v, z, gt, sc, gn, st = _gdn_specs(N, True)
    dgt = pl.BlockSpec((GDN_HB, C, LANE), lambda h, n: (h, N - 1 - n, 0))
    dsc = pl.BlockSpec((GDN_HB, 8, LANE), lambda h, n: (h, 0, 0))
    dgn = pl.BlockSpec((GDN_HB, 1, DV), lambda h, n: (h, 0, 0))
    sh = jax.ShapeDtypeStruct((T, H * DK), F32)
    return pl.pallas_call(
        body, name=name, grid=(H // GDN_HB, N), in_specs=[q, k, v, z, gt, sc, sc, gn, st, q],
        out_specs=(q, q, q, q, dgt, dsc, dgn),
        out_shape=(sh, sh, sh, jax.ShapeDtypeStruct((T, H * DV), BF16),
                   jax.ShapeDtypeStruct((H, T, LANE), F32), jax.ShapeDtypeStruct((H, 8, LANE), F32),
                   jax.ShapeDtypeStruct((H, 1, DV), F32)),
        scratch_shapes=[pltpu.VMEM((GDN_HB, DK, DV), F32)],
        compiler_params=_params(("arbitrary", "arbitrary")),
    )(act, act, act, proj, gates, a_log, dt_bias, gain, states, dog)


SUBLANES = 8


def _linear_scan(a_ref, b_ref, h_ref, reverse):
    T, W = a_ref.shape
    nb = T // SUBLANES
    row = _iota2((SUBLANES, W), 0)

    def blk(bi, carry):
        bb = (nb - 1 - bi) if reverse else bi
        off = pl.multiple_of(bb * SUBLANES, SUBLANES)
        a = a_ref[pl.ds(off, SUBLANES), :]
        b = b_ref[pl.ds(off, SUBLANES), :]
        for d in (1, 2, 4):
            if reverse:
                edge = row >= SUBLANES - d
                a_sh = jnp.where(edge, 1.0, pltpu.roll(a, SUBLANES - d, 0))
                b_sh = jnp.where(edge, 0.0, pltpu.roll(b, SUBLANES - d, 0))
            else:
                edge = row < d
                a_sh = jnp.where(edge, 1.0, pltpu.roll(a, d, 0))
                b_sh = jnp.where(edge, 0.0, pltpu.roll(b, d, 0))
            b = a * b_sh + b
            a = a * a_sh
        h = a * carry + b
        h_ref[pl.ds(off, SUBLANES), :] = h
        return h[0:1, :] if reverse else h[SUBLANES - 1:SUBLANES, :]

    lax.fori_loop(0, nb, blk, jnp.zeros((1, W), F32))


def _lru_specs(T):
    B, W = LRU_BLOCKS, LRU_BLOCK
    xb = pl.BlockSpec((T, W), lambda j: (0, j))
    yb = pl.BlockSpec((T, W), lambda j: (0, B + j))
    cw = pl.BlockSpec((CONV_WIDTH, W), lambda j: (0, j))
    vec = pl.BlockSpec((1, W), lambda j: (0, j))
    wg = pl.BlockSpec((1, W, W), lambda j: (j, 0, 0))
    bg = pl.BlockSpec((1, 1, W), lambda j: (j, 0, 0))
    return xb, yb, cw, vec, wg, bg


def _lru_gates(xb_ref, cw_ref, cb_ref, wr_ref, br_ref, wi_ref, bi_ref, lam_ref):
    xv = xb_ref[...]
    xc = _conv(xv, cw_ref) + cb_ref[...]
    r = _sigmoid(_bdot(xc, wr_ref[0]) + br_ref[0])
    i = _sigmoid(_bdot(xc, wi_ref[0]) + bi_ref[0])
    sp = _softplus(-lam_ref[...])
    la = -LRU_C * sp * r
    a = jnp.exp(la)
    s = jnp.sqrt(-_expm1(2.0 * la))
    return xv, xc, r, i, sp, a, s


def _lru_fwd(proj, conv_w, conv_b, w_r, b_r, w_i, b_i, lam, *, name):
    T = proj.shape[0]
    B, W = LRU_BLOCKS, LRU_BLOCK

    def body(xb_ref, yb_ref, cw_ref, cb_ref, wr_ref, br_ref, wi_ref, bi_ref, lam_ref, og_ref, hs_ref, a_s, u_s):
        xv, xc, r, i, sp, a, s = _lru_gates(xb_ref, cw_ref, cb_ref, wr_ref, br_ref, wi_ref, bi_ref, lam_ref)
        a_s[...] = a
        u_s[...] = s * (i * xc)
        _linear_scan(a_s, u_s, hs_ref, False)
        og_ref[...] = (hs_ref[...] * _gelu(yb_ref[...])).astype(og_ref.dtype)

    xb, yb, cw, vec, wg, bg = _lru_specs(T)
    return pl.pallas_call(
        body, name=name, grid=(B,), in_specs=[xb, yb, cw, vec, wg, bg, wg, bg, vec], out_specs=(xb, xb),
        out_shape=(jax.ShapeDtypeStruct((T, B * W), BF16), jax.ShapeDtypeStruct((T, B * W), F32)),
        scratch_shapes=[pltpu.VMEM((T, W), F32), pltpu.VMEM((T, W), F32)],
        compiler_params=_params(("arbitrary",)),
    )(proj, proj, conv_w, conv_b, w_r, b_r, w_i, b_i, lam)


def _lru_bwd(proj, conv_w, conv_b, w_r, b_r, w_i, b_i, lam, hs, dout, *, name):
    T = proj.shape[0]
    B, W = LRU_BLOCKS, LRU_BLOCK

    def csum(x):
        return jnp.sum(x, axis=0, keepdims=True)

    def body(xb_ref, yb_ref, cw_ref, cb_ref, wr_ref, br_ref, wi_ref, bi_ref, lam_ref, hs_ref, do_ref,
             dxb_ref, dyb_ref, dcw_ref, dcb_ref, dwr_ref, dbr_ref, dwi_ref, dbi_ref, dlam_ref, a_s, b_s, g_s):
        xv, xc, r, i, sp, a, s = _lru_gates(xb_ref, cw_ref, cb_ref, wr_ref, br_ref, wi_ref, bi_ref, lam_ref)
        h = hs_ref[...]
        yb = yb_ref[...]
        dout = do_ref[...].astype(F32)
        dyb_ref[...] = (dout * h * _dgelu(yb)).astype(dyb_ref.dtype)
        a_s[...] = _shift_up(a, 1)
        b_s[...] = dout * _gelu(yb)
        _linear_scan(a_s, b_s, g_s, True)
        g = g_s[...]
        da = g * _shift_down(h, 1)
        ds = g * (i * xc)
        di = g * s * xc
        dxc = g * s * i
        dla = da * a - ds * (a * a) / s
        dr = dla * (-LRU_C * sp)
        dlam_ref[...] = csum(dla * r) * (LRU_C * _sigmoid(-lam_ref[...]))
        dpr = dr * r * (1.0 - r)
        dpi = di * i * (1.0 - i)
        dxc = dxc + _bdot(dpr, wr_ref[0], NT) + _bdot(dpi, wi_ref[0], NT)
        dwr_ref[0] = _bdot(xc, dpr, TN)
        dwi_ref[0] = _bdot(xc, dpi, TN)
        dbr_ref[0] = csum(dpr)
        dbi_ref[0] = csum(dpi)
        dcb_ref[...] = csum(dxc)
        dxb_ref[...] = _conv_bwd(xv, cw_ref, dxc, dcw_ref).astype(dxb_ref.dtype)

    xb, yb, cw, vec, wg, bg = _lru_specs(T)
    act = jax.ShapeDtypeStruct((T, B * W), BF16)
    return pl.pallas_call(
        body, name=name, grid=(B,), in_specs=[xb, yb, cw, vec, wg, bg, wg, bg, vec, xb, xb],
        out_specs=(xb, xb, cw, vec, wg, bg, wg, bg, vec),
        out_shape=(act, act, jax.ShapeDtypeStruct((CONV_WIDTH, B * W), F32), jax.ShapeDtypeStruct((1, B * W), F32),
                   jax.ShapeDtypeStruct((B, W, W), F32), jax.ShapeDtypeStruct((B, 1, W), F32),
                   jax.ShapeDtypeStruct((B, W, W), F32), jax.ShapeDtypeStruct((B, 1, W), F32),
                   jax.ShapeDtypeStruct((1, B * W), F32)),
        scratch_shapes=[pltpu.VMEM((T, W), F32)] * 3,
        compiler_params=_params(("arbitrary",)),
    )(proj, proj, conv_w, conv_b, w_r, b_r, w_i, b_i, lam, hs, dout)


MESH = pl.DeviceIdType.MESH
N_CHIPS = 4
AG_COPIES = 7


def _mesh_pos():
    return lax.axis_index("x"), lax.axis_index("y"), lax.axis_index("c")


def _hbm_specs(n):
    return [pl.BlockSpec(memory_space=pltpu.HBM)] * n


def _all_gather(shards, *, name):
    n = len(shards)

    def body(*refs):
        xs, outs = refs[:n], refs[n:2 * n]
        send_sems, recv_sems, local_sems = refs[2 * n:]
        x, y, c = _mesh_pos()
        me, sibling = (x, y, c), (x, y, 1 - c)
        chips = [(1 - x, y), (x, 1 - y), (1 - x, 1 - y)]

        def rows(t, px, py, pc):
            return outs[t].at[4 * px + 2 * py + pc]

        def copy(t, k, block, to, src=None):
            return pltpu.make_async_remote_copy(
                src_ref=rows(t, *block) if src is None else src, dst_ref=rows(t, *block),
                send_sem=send_sems.at[t * AG_COPIES + k], recv_sem=recv_sems.at[t * AG_COPIES + k],
                device_id=to, device_id_type=MESH)

        mine = [pltpu.make_async_copy(xs[t], rows(t, *me), local_sems.at[t]) for t in range(n)]
        for cp in mine:
            cp.start()
        first = []
        for t in range(n):
            first.append(copy(t, 0, me, sibling, src=xs[t]))
            first += [copy(t, 1 + j, me, (*chip, c), src=xs[t]) for j, chip in enumerate(chips)]
        for cp in first:
            cp.start()
        passed = []
        for j, chip in enumerate(chips):
            for t in range(n):
                copy(t, 1 + j, (*chip, c), me).wait_recv()
                cp = copy(t, 4 + j, (*chip, c), sibling)
                cp.start()
                passed.append(cp)
        for t in range(n):
            copy(t, 0, sibling, me).wait_recv()
        for j, chip in enumerate(chips):
            for t in range(n):
                copy(t, 4 + j, (*chip, 1 - c), me).wait_recv()
        for cp in first + passed:
            cp.wait_send()
        for cp in mine:
            cp.wait()

    return pl.pallas_call(
        body, name=name,
        out_shape=[jax.ShapeDtypeStruct((N_DEV,) + s.shape, s.dtype) for s in shards],
        in_specs=_hbm_specs(n), out_specs=_hbm_specs(n),
        scratch_shapes=[pltpu.SemaphoreType.DMA((n * AG_COPIES,)), pltpu.SemaphoreType.DMA((n * AG_COPIES,)),
                        pltpu.SemaphoreType.DMA((n,))],
    )(*shards)


SIDE_EFFECT = pltpu.SideEffectType.DATAFLOW_SIDE_EFFECTING


def _copies(plan, refs, send_sems, recv_sems):
    return [pltpu.make_async_remote_copy(src_ref=src, dst_ref=dst, send_sem=send_sems.at[k], recv_sem=recv_sems.at[k],
                                         device_id=to, device_id_type=MESH)
            for k, (src, dst, to) in enumerate(plan(refs))]


def _split_start(bufs, plan, n_copies, *, name, deps=()):
    n = len(bufs)

    def body(*refs):
        send_sems, recv_sems = refs[n + len(deps)], refs[n + len(deps) + 1]
        token = refs[-1]
        for cp in _copies(plan, refs[:n], send_sems, recv_sems):
            cp.start()
        token[...] = jnp.zeros_like(token)

    hbm, sem = pl.BlockSpec(memory_space=pltpu.HBM), pl.BlockSpec(memory_space=pltpu.SEMAPHORE)
    out = pl.pallas_call(
        body, name=name,
        out_shape=(pltpu.SemaphoreType.DMA((n_copies,)), pltpu.SemaphoreType.DMA((n_copies,)),
                   *[pltpu.HBM(b.shape, b.dtype) for b in bufs], jax.ShapeDtypeStruct(TOKEN_SHAPE, F32)),
        in_specs=[hbm] * n + [pl.BlockSpec(memory_space=pl.ANY)] * len(deps),
        out_specs=(sem, sem, *[hbm] * n, pl.BlockSpec(memory_space=pltpu.VMEM)),
        input_output_aliases={i: 2 + i for i in range(n)},
        compiler_params=pltpu.CompilerParams(has_side_effects=SIDE_EFFECT),
    )(*[pltpu.with_memory_space_constraint(b, pltpu.HBM) for b in bufs], *deps)
    return out[0], out[1], list(out[2:2 + n]), out[-1]


def _split_wait(send_sems, recv_sems, bufs, plan, after, *, name):
    n = len(bufs)

    def body(*refs):
        for cp in _copies(plan, refs[:n], refs[n], refs[n + 1]):
            cp.wait_send()
            cp.wait_recv()

    hbm, sem = pl.BlockSpec(memory_space=pltpu.HBM), pl.BlockSpec(memory_space=pltpu.SEMAPHORE)
    out = pl.pallas_call(
        body, name=name, out_shape=tuple(pltpu.HBM(b.shape, b.dtype) for b in bufs),
        in_specs=[hbm] * n + [sem, sem, pl.BlockSpec(memory_space=pl.ANY)], out_specs=tuple([hbm] * n),
        input_output_aliases={i: i for i in range(n)},
        compiler_params=pltpu.CompilerParams(has_side_effects=SIDE_EFFECT),
    )(*bufs, send_sems, recv_sems, after)
    return list(out)


def _plan_gather_a(n):
    def plan(refs):
        x, y, c = _mesh_pos()
        me = 4 * x + 2 * y + c
        peers = [(x, y, 1 - c), (1 - x, y, c), (x, 1 - y, c), (1 - x, 1 - y, c)]
        return [(refs[t], refs[n + t].at[me], to) for t in range(n) for to in peers]
    return plan


def _plan_gather_b(n):
    def plan(refs):
        x, y, c = _mesh_pos()
        rows = [4 * px + 2 * py + c for px, py in [(1 - x, y), (x, 1 - y), (1 - x, 1 - y)]]
        return [(refs[t].at[r], refs[t].at[r], (x, y, 1 - c)) for t in range(n) for r in rows]
    return plan


def _plan_scatter_pair(n):
    def plan(refs):
        x, y, c = _mesh_pos()
        return [(refs[t].at[2 * q + (1 - c)], refs[n + t].at[q], (x, y, 1 - c))
                for t in range(n) for q in range(N_CHIPS)]
    return plan


def _plan_scatter_chips(n):
    def plan(refs):
        x, y, c = _mesh_pos()
        chips = [(1 - x, y), (x, 1 - y), (1 - x, 1 - y)]
        return [(refs[t].at[2 * px + py], refs[n + t].at[j], (px, py, c))
                for t in range(n) for j, (px, py) in enumerate(chips)]
    return plan


PAIR_ROWS = (512, 256)


def _pair_add(g, a, core, *, name):
    _, R, C = g.shape
    tr = _tile(R, PAIR_ROWS)

    def body(c_ref, g_ref, a_ref, o_ref):
        o_ref[...] = (g_ref[...].astype(F32) + a_ref[...].astype(F32)).astype(o_ref.dtype)

    blk = pl.BlockSpec((1, tr, C), lambda q, i, c: (q, i, 0))
    return pl.pallas_call(
        body, name=name,
        grid_spec=pltpu.PrefetchScalarGridSpec(
            num_scalar_prefetch=1, grid=(N_CHIPS, R // tr),
            in_specs=[pl.BlockSpec((1, tr, C), lambda q, i, c: (2 * q + c[0], i, 0)), blk], out_specs=blk),
        out_shape=jax.ShapeDtypeStruct((N_CHIPS, R, C), BF16),
        compiler_params=_params(("parallel", "parallel")),
    )(core, g, a)


ADAM_ROWS = 256


def _adamw_sharded(w, m, v, s4, b3, chip, layer, prev, *, name):
    L, R, C = w.shape
    tr = _tile(R, (ADAM_ROWS,))
    if prev is None and L > 1:
        prev = tuple(lax.empty(w.shape, F32) for _ in range(4))
    n_prev = 0 if prev is None else 4

    def body(q_ref, w_ref, m_ref, v_ref, s_ref, b_ref, *rest):
        g_out, d_out, m_out, v_out = rest[n_prev:]
        g = s_ref[0].astype(F32)
        for j in range(N_CHIPS - 1):
            g = g + b_ref[j].astype(F32)
        mn = ADAM_B1 * m_ref[0] + (1.0 - ADAM_B1) * g
        vn = ADAM_B2 * v_ref[0] + (1.0 - ADAM_B2) * (g * g)
        g_out[0] = g
        d_out[0] = -ADAM_LR * ((mn / ADAM_C1) / (jnp.sqrt(vn / ADAM_C2) + ADAM_EPS) + ADAM_WD * w_ref[0])
        m_out[0] = mn
        v_out[0] = vn

    blk = pl.BlockSpec((1, tr, C), lambda i, q: (layer, i, 0))
    sh = jax.ShapeDtypeStruct((L, R, C), F32)
    return pl.pallas_call(
        body, name=name,
        grid_spec=pltpu.PrefetchScalarGridSpec(
            num_scalar_prefetch=1, grid=(R // tr,),
            in_specs=[blk, blk, blk, pl.BlockSpec((1, tr, C), lambda i, q: (q[0], i, 0)),
                      pl.BlockSpec((N_CHIPS - 1, tr, C), lambda i, q: (0, i, 0))]
            + [pl.BlockSpec(memory_space=pl.ANY)] * n_prev,
            out_specs=(blk,) * 4),
        out_shape=(sh,) * 4, input_output_aliases={6 + k: k for k in range(n_prev)},
        compiler_params=_params(("parallel",)),
    )(chip, w, m, v, s4, b3, *(prev or ()))


FWD_NAMES = ['x', 'norm1', 'norm2', 'final_norm', 'ret_w_in', 'ret_gn_gain', 'ret_w_out', 'gdn_w_in', 'gdn_conv_w',
             'gdn_a_log', 'gdn_dt_bias', 'gdn_norm_gain', 'gdn_w_out', 'gla_w_in', 'gla_w_gate_up', 'gla_gate_bias',
             'gla_norm_gain', 'gla_w_out', 'lru_w_in', 'lru_conv_w', 'lru_conv_b', 'lru_w_rgate', 'lru_b_rgate',
             'lru_w_igate', 'lru_b_igate', 'lru_lambda', 'lru_w_out', 'mlp_w_up', 'mlp_w_down']
WEIGHT_NAMES = FWD_NAMES[1:]
ARG_NAMES = FWD_NAMES + ['loss_target'] + ['m_' + n for n in WEIGHT_NAMES] + ['v_' + n for n in WEIGHT_NAMES]

MIXER_IN = ('ret_w_in', 'gdn_w_in', 'gla_w_in', 'lru_w_in')
MIXER_OUT = ('ret_w_out', 'gdn_w_out', 'gla_w_out', 'lru_w_out')
BIG_NAMES = MIXER_IN + MIXER_OUT + ('mlp_w_up', 'mlp_w_down')
SMALL = {'norm1': False, 'norm2': False, 'final_norm': False, 'ret_gn_gain': True, 'gdn_conv_w': True,
         'gdn_a_log': False, 'gdn_dt_bias': False, 'gdn_norm_gain': False, 'gla_w_gate_up': True,
         'gla_gate_bias': True, 'gla_norm_gain': True, 'lru_conv_w': True, 'lru_conv_b': True,
         'lru_w_rgate': False, 'lru_b_rgate': False, 'lru_w_igate': False, 'lru_b_igate': False, 'lru_lambda': True}
SMALL_NAMES = tuple(n for n in WEIGHT_NAMES if n in SMALL)
GDN_TAIL = 2 * GDN_HEADS


PACK_ROWS = 256


def _pack(arrs):
    rows = []
    for a in arrs:
        f = a.reshape(-1).astype(F32)
        rows.append(jnp.pad(f, (0, (-f.shape[0]) % LANE)).reshape(-1, LANE))
    buf = jnp.concatenate(rows, axis=0)
    return jnp.pad(buf, ((0, (-buf.shape[0]) % PACK_ROWS), (0, 0)))


def _unpack(buf, shapes, lead=()):
    out, r0 = [], 0
    for s in shapes:
        n = int(np.prod(s))
        nr = -(-n // LANE)
        blk = buf[..., r0:r0 + nr, :].reshape(lead + (nr * LANE,))[..., :n]
        out.append(blk.reshape(lead + tuple(s)))
        r0 += nr
    return out


def _full_cols(g):
    return jnp.transpose(g, (1, 0, 2)).reshape(g.shape[1], N_DEV * g.shape[2])


def _full_rows(g):
    return g.reshape(N_DEV * g.shape[1], g.shape[2])


def _blocks_cols(dw):
    r, c = dw.shape[0], dw.shape[1] // N_DEV
    return jnp.transpose(dw.reshape(r, N_DEV, c), (1, 0, 2))


def _blocks_rows(dw):
    return dw.reshape(N_DEV, dw.shape[0] // N_DEV, dw.shape[1])


def _pad_cols(a, n=LANE):
    return jnp.pad(a, ((0, 0), (0, n - a.shape[1])))


def _mixer_fwd(layer, hn, w_in, sm, tables, deps=()):
    tag = f"l{layer}"
    if layer == 0:
        proj = _matmul(hn, w_in, name=tag + "_in", deps=deps)
        gain = sm['ret_gn_gain'][0][:, None, :]
        og, st = _ret_fwd(proj, gain, tables, name=tag + "_ret_fwd")
        return og, dict(proj=proj, st=st, gain=gain)
    if layer == 1:
        w_main, w_tail = w_in[:, :4 * D_MODEL], _pad_cols(w_in[:, 4 * D_MODEL:])
        proj = _matmul(hn, w_main, name=tag + "_in", deps=deps)
        gates = _matmul(hn, w_tail, name=tag + "_in_tail")
        conv_w = sm['gdn_conv_w'][0]
        act = _conv_silu_fwd(proj, conv_w, name=tag + "_conv")
        a_log = sm['gdn_a_log'].reshape(GDN_HEADS, 1, 1)
        dt_bias = sm['gdn_dt_bias'].reshape(GDN_HEADS, 1, 1)
        gain = sm['gdn_norm_gain']
        og, st = _gdn_fwd(act, proj, gates, a_log, dt_bias, gain, name=tag + "_gdn_fwd")
        return og, dict(proj=proj, gates=gates, act=act, st=st, conv_w=conv_w, a_log=a_log, dt_bias=dt_bias,
                        gain=gain, w_main=w_main, w_tail=w_tail)
    if layer == 2:
        w_main, w_tail = w_in[:, :3 * D_MODEL], _pad_cols(w_in[:, 3 * D_MODEL:])
        proj = _matmul(hn, w_main, name=tag + "_in", deps=deps)
        glow = _matmul(hn, w_tail, name=tag + "_in_tail")
        wgu = jnp.pad(sm['gla_w_gate_up'][0], ((0, LANE - GLA_GATE_RANK), (0, 0)))
        glogit = _matmul(glow, wgu, name=tag + "_gate_up")
        bias = sm['gla_gate_bias']
        gain = sm['gla_norm_gain'][0][:, None, :]
        og, st = _gla_fwd(proj, glogit, bias, gain, name=tag + "_gla_fwd")
        return og, dict(proj=proj, glow=glow, glogit=glogit, wgu=wgu, bias=bias, gain=gain, st=st,
                        w_main=w_main, w_tail=w_tail)
    proj = _matmul(hn, w_in, name=tag + "_in", deps=deps)
    args = (proj, sm['lru_conv_w'][0], sm['lru_conv_b'], sm['lru_w_rgate'][0], sm['lru_b_rgate'][0][:, None, :],
            sm['lru_w_igate'][0], sm['lru_b_igate'][0][:, None, :], sm['lru_lambda'])
    og, hs = _lru_fwd(*args, name=tag + "_lru_fwd")
    return og, dict(args=args, hs=hs)


def _mixer_bwd(layer, hn, w_in, dog, sv, tables, on_dw):
    tag = f"l{layer}"
    if layer == 0:
        dq, dk, dv, dg, dgain = _ret_bwd(sv['proj'], sv['gain'], tables, sv['st'], dog, name=tag + "_ret_bwd")
        dproj = jnp.concatenate([dq, dk, dv, dg], axis=1)
        deps = on_dw(_matmul(hn, dproj, ta=True, out_dtype=BF16, name=tag + "_in_dw"))
        dhn = _matmul(dproj, w_in, tb=True, name=tag + "_in_dx", deps=deps)
        return dhn, {'ret_gn_gain': dgain[:, 0][None]}
    if layer == 1:
        dq, dk, dv, dz, dgates, dscal, dgain = _gdn_bwd(
            sv['act'], sv['proj'], sv['gates'], sv['a_log'], sv['dt_bias'], sv['gain'], sv['st'], dog,
            name=tag + "_gdn_bwd")
        dact = jnp.concatenate([dq, dk, dv], axis=1)
        dqkv, dconv = _conv_silu_bwd(sv['proj'], sv['conv_w'], dact, name=tag + "_conv_bwd")
        dmain = jnp.concatenate([dqkv, dz], axis=1)
        T = dmain.shape[0]
        dtail = _pad_cols(jnp.transpose(dgates[:, :, :2], (1, 2, 0)).reshape(T, GDN_TAIL))
        dw_main = _matmul(hn, dmain, ta=True, out_dtype=BF16, name=tag + "_in_dw")
        dw_tail = _matmul(hn, dtail, ta=True, out_dtype=BF16, name=tag + "_in_tail_dw")
        deps = on_dw(jnp.concatenate([dw_main, dw_tail[:, :GDN_TAIL]], axis=1))
        dhn = _matmul(dmain, sv['w_main'], tb=True, name=tag + "_in_dx", deps=deps)
        dhn = _matmul(dtail, sv['w_tail'], tb=True, epi="add", extra=dhn, name=tag + "_in_tail_dx")
        small = {'gdn_conv_w': dconv[None], 'gdn_a_log': dscal[:, 0, 0][None], 'gdn_dt_bias': dscal[:, 0, 1][None],
                 'gdn_norm_gain': jnp.sum(dgain[:, 0], axis=0)[None]}
        return dhn, small
    if layer == 2:
        dq, dk, dv, dr, dgl, dgain = _gla_bwd(sv['proj'], sv['glogit'], sv['bias'], sv['gain'], sv['st'], dog,
                                              name=tag + "_gla_bwd")
        dmain = jnp.concatenate([dq, dk, dv, dr], axis=1)
        dglow = _matmul(dgl, sv['wgu'], tb=True, name=tag + "_gate_up_dx")
        dwgu = _matmul(sv['glow'], dgl, ta=True, name=tag + "_gate_up_dw")
        dbias = _colsum(dgl, name=tag + "_gate_bias")
        dw_main = _matmul(hn, dmain, ta=True, out_dtype=BF16, name=tag + "_in_dw")
        dw_tail = _matmul(hn, dglow, ta=True, out_dtype=BF16, name=tag + "_in_tail_dw")
        deps = on_dw(jnp.concatenate([dw_main, dw_tail[:, :GLA_GATE_RANK]], axis=1))
        dhn = _matmul(dmain, sv['w_main'], tb=True, name=tag + "_in_dx", deps=deps)
        dhn = _matmul(dglow, sv['w_tail'], tb=True, epi="add", extra=dhn, name=tag + "_in_tail_dx")
        small = {'gla_w_gate_up': dwgu[:GLA_GATE_RANK][None], 'gla_gate_bias': dbias,
                 'gla_norm_gain': dgain[:, 0][None]}
        return dhn, small
    dxb, dyb, dcw, dcb, dwr, dbr, dwi, dbi, dlam = _lru_bwd(*sv['args'], sv['hs'], dog, name=tag + "_lru_bwd")
    dproj = jnp.concatenate([dxb, dyb], axis=1)
    deps = on_dw(_matmul(hn, dproj, ta=True, out_dtype=BF16, name=tag + "_in_dw"))
    dhn = _matmul(dproj, w_in, tb=True, name=tag + "_in_dx", deps=deps)
    small = {'lru_conv_w': dcw[None], 'lru_conv_b': dcb, 'lru_w_rgate': dwr[None], 'lru_b_rgate': dbr[:, 0][None],
             'lru_w_igate': dwi[None], 'lru_b_igate': dbi[:, 0][None], 'lru_lambda': dlam}
    return dhn, small


def _step(*args):
    assert len(args) == len(ARG_NAMES)
    p = dict(zip(ARG_NAMES, args))
    xi, yi, ci = _mesh_pos()
    dev = 4 * xi + 2 * yi + ci
    core = ci.astype(jnp.int32).reshape(1)
    chip = (2 * xi + yi).astype(jnp.int32).reshape(1)
    x = p['x'][0]
    target = p['loss_target'][0]
    T = x.shape[0]
    tables = _ret_tables(T)

    sharded_small = [n for n in SMALL_NAMES if SMALL[n]]
    gathered, = _all_gather([_pack([p[n] for n in sharded_small])], name="gather_small")
    gathers = {}
    token = gathered
    for layer in range(DEPTH):
        groups = {'a': [(p[MIXER_IN[layer]][0], _full_cols)],
                  'b': [(p[MIXER_OUT[layer]][0], _full_rows), (p['mlp_w_up'][layer], _full_cols),
                        (p['mlp_w_down'][layer], _full_rows)]}
        for key, members in groups.items():
            shards = [w.astype(BF16) for w, _ in members]
            n = len(shards)
            lands = [lax.empty((N_DEV,) + s.shape, BF16) for s in shards]
            send, recv, bufs, token = _split_start(shards + lands, _plan_gather_a(n), 4 * n,
                                                   name=f"gather_a_start_l{layer}{key}", deps=(token,))
            gathers[layer, key] = dict(n=n, h=(send, recv, bufs), full_of=[f for _, f in members])

    def gather_forward(layer, key, after):
        g = gathers[layer, key]
        n = g['n']
        send, recv, bufs = g['h']
        bufs = _split_wait(send, recv, bufs, _plan_gather_a(n), after, name=f"gather_a_wait_l{layer}{key}")
        send, recv, lands, tok = _split_start(bufs[n:], _plan_gather_b(n), 3 * n, name=f"gather_b_start_l{layer}{key}")
        g['h'], g['shards'] = (send, recv, lands), bufs[:n]
        return tok

    def gather_finish(layer, key, after):
        g = gathers[layer, key]
        send, recv, lands = g['h']
        lands = _split_wait(send, recv, lands, _plan_gather_b(g['n']), after, name=f"gather_b_wait_l{layer}{key}")
        return [full_of(lax.dynamic_update_slice(l, s[None], (dev, 0, 0)))
                for l, s, full_of in zip(lands, g['shards'], g['full_of'])]

    w_in_next, = gather_finish(0, 'a', gather_forward(0, 'a', token))
    parts = _unpack(gathered, [p[n].shape for n in sharded_small], lead=(N_DEV,))
    sm = {n: p[n] for n in SMALL_NAMES if not SMALL[n]}
    for n, blk in zip(sharded_small, parts):
        full = jnp.moveaxis(blk, 0, -2)
        sm[n] = full.reshape(full.shape[:-2] + (N_DEV * full.shape[-1],))

    saved = []
    big = {}
    for layer in range(DEPTH):
        w_in = w_in_next
        tag = f"l{layer}"
        hn = _rmsnorm_fwd(x, sm['norm1'][layer][None], name=tag + "_norm1")
        if layer == 0:
            og, sv = _mixer_fwd(layer, hn, w_in, sm, tables)
            w_out, w_up, w_down = gather_finish(layer, 'b', gather_forward(layer, 'b', og))
        else:
            og, sv = _mixer_fwd(layer, hn, w_in, sm, tables, deps=(gather_forward(layer, 'b', hn),))
            w_out, w_up, w_down = gather_finish(layer, 'b', og)
        big[layer] = (w_in, w_out, w_up, w_down)
        deps = (gather_forward(layer + 1, 'a', og),) if layer + 1 < DEPTH else ()
        x_mid = _matmul(og, w_out, epi="add", extra=x, name=tag + "_out", deps=deps)
        hn2 = _rmsnorm_fwd(x_mid, sm['norm2'][layer][None], name=tag + "_norm2")
        u, a = _matmul(hn2, w_up, epi="relu2", name=tag + "_up")
        x_new = _matmul(a, w_down, epi="add", extra=x_mid, name=tag + "_down")
        if layer + 1 < DEPTH:
            w_in_next, = gather_finish(layer + 1, 'a', x_new)
        saved.append(dict(x=x, hn=hn, og=og, sv=sv, x_mid=x_mid, hn2=hn2, u=u, a=a))
        x = x_new
    dy, loss_part = _final_loss(x, sm['final_norm'][None], target, name="final_loss")
    loss = lax.psum(loss_part[0, 0], ("x", "y", "c"))
    dx, dxb, dfinal = _rmsnorm_bwd(x, sm['final_norm'][None], dy, jnp.zeros_like(x), name="final_norm_bwd")

    outs = {}
    small_grads = {'final_norm': dfinal[0]}
    dnorm1, dnorm2 = [None] * DEPTH, [None] * DEPTH

    def scatter_start(items, deps, tag):
        blocks = [b for _, _, b in items]
        n = len(blocks)
        lands = [lax.empty((N_CHIPS,) + b.shape[1:], BF16) for b in blocks]
        send, recv, bufs, tok = _split_start(blocks + lands, _plan_scatter_pair(n), N_CHIPS * n,
                                             name=f"scatter_pair_start_{tag}", deps=deps)
        return dict(items=items, n=n, tag=tag, h=(send, recv, bufs)), tok

    def scatter_forward(g, after):
        n, tag = g['n'], g['tag']
        send, recv, bufs = g['h']
        bufs = _split_wait(send, recv, bufs, _plan_scatter_pair(n), after, name=f"scatter_pair_wait_{tag}")
        sums = [_pair_add(b, a_, core, name=f"pair_add_{tag}_{i}") for i, (b, a_) in enumerate(zip(bufs[:n], bufs[n:]))]
        lands = [lax.empty((N_CHIPS - 1,) + s_.shape[1:], BF16) for s_ in sums]
        send, recv, bufs, tok = _split_start(sums + lands, _plan_scatter_chips(n), (N_CHIPS - 1) * n,
                                             name=f"scatter_chips_start_{tag}")
        g['h'] = (send, recv, bufs)
        return tok

    def scatter_finish(g, after):
        n, tag = g['n'], g['tag']
        send, recv, bufs = g['h']
        bufs = _split_wait(send, recv, bufs, _plan_scatter_chips(n), after, name=f"scatter_chips_wait_{tag}")
        for i, (wname, idx, _) in enumerate(g['items']):
            outs[wname] = _adamw_sharded(p[wname], p['m_' + wname], p['v_' + wname], bufs[i], bufs[n + i], chip,
                                         idx, outs.get(wname), name=f"adamw_{tag}_{i}")

    older = []
    for layer in reversed(range(DEPTH)):
        w_in, w_out, w_up, w_down = big[layer]
        s = saved[layer]
        tag = f"l{layer}"
        du = _matmul(dxb, w_down, tb=True, epi="drelu2", extra=s['u'], out_dtype=BF16, name=tag + "_down_dx")
        dw_down = _matmul(s['a'], dxb, ta=True, out_dtype=BF16, name=tag + "_down_dw")
        dhn2 = _matmul(du, w_up, tb=True, name=tag + "_up_dx")
        dw_up = _matmul(s['hn2'], du, ta=True, out_dtype=BF16, name=tag + "_up_dw")
        dx, dxb, dn2 = _rmsnorm_bwd(s['x_mid'], sm['norm2'][layer][None], dhn2, dx, name=tag + "_norm2_bwd")
        mlp_group, tok = scatter_start([('mlp_w_up', layer, _blocks_cols(dw_up)),
                                        ('mlp_w_down', layer, _blocks_rows(dw_down))], (), f"mlp_l{layer}")
        dw_out = _matmul(s['og'], dxb, ta=True, out_dtype=BF16, name=tag + "_out_dw", deps=(tok,))
        tok = scatter_forward(mlp_group, dw_out)
        dog = _matmul(dxb, w_out, tb=True, name=tag + "_out_dx", deps=(tok,))
        started = []

        def on_dw(dw_in):
            group, tok_ = scatter_start([(MIXER_IN[layer], 0, _blocks_cols(dw_in)),
                                         (MIXER_OUT[layer], 0, _blocks_rows(dw_out))], (), f"mix_l{layer}")
            started.append(group)
            return (tok_,)

        dhn, sg = _mixer_bwd(layer, s['hn'], w_in, dog, s['sv'], tables, on_dw)
        mixer_group, = started
        small_grads.update(sg)
        if layer > 0:
            tok = scatter_forward(mixer_group, dhn)
            dx, dxb, dn1 = _rmsnorm_bwd(s['x'], sm['norm1'][layer][None], dhn, dx, name=tag + "_norm1_bwd", deps=(tok,))
        else:
            dx, dxb, dn1 = _rmsnorm_bwd(s['x'], sm['norm1'][layer][None], dhn, dx, name=tag + "_norm1_bwd")
        dnorm1[layer], dnorm2[layer] = dn1[0], dn2[0]
        for g in older:
            scatter_finish(g, dx)
        older = [mlp_group, mixer_group]
    small_grads['norm1'] = jnp.stack(dnorm1)
    small_grads['norm2'] = jnp.stack(dnorm2)

    full_shapes = [small_grads[n].shape for n in SMALL_NAMES]
    all_parts, = _all_gather([_pack([small_grads[n] for n in SMALL_NAMES])], name="gather_small_grads")
    last_token = scatter_forward(mixer_group, all_parts)
    summed_buf = _sum_parts(all_parts, name="sum_small_grads", deps=(last_token,))
    for g in older:
        scatter_finish(g, summed_buf)
    summed = _unpack(summed_buf, full_shapes)
    local_g = []
    for n, g in zip(SMALL_NAMES, summed):
        if SMALL[n]:
            width = p[n].shape[-1]
            g = lax.dynamic_slice_in_dim(g, dev * width, width, axis=g.ndim - 1)
        local_g.append(g.reshape(p[n].shape))
    res = _adamw(_pack([p[n] for n in SMALL_NAMES]), _pack([p['m_' + n] for n in SMALL_NAMES]),
                 _pack([p['v_' + n] for n in SMALL_NAMES]), [_pack(local_g)], name="adamw_small")
    local_shapes = [p[n].shape for n in SMALL_NAMES]
    unpacked = [_unpack(r, local_shapes) for r in res]
    for i, n in enumerate(SMALL_NAMES):
        outs[n] = tuple(unpacked[k][i] for k in range(4))

    result = [loss, dx[None]]
    for k in range(4):
        result += [outs[n][k] for n in WEIGHT_NAMES]
    return tuple(result)


def kernel(x, norm1, norm2, final_norm, ret_w_in, ret_gn_gain, ret_w_out, gdn_w_in, gdn_conv_w, gdn_a_log, gdn_dt_bias, gdn_norm_gain, gdn_w_out, gla_w_in, gla_w_gate_up, gla_gate_bias, gla_norm_gain, gla_w_out, lru_w_in, lru_conv_w, lru_conv_b, lru_w_rgate, lru_b_rgate, lru_w_igate, lru_b_igate, lru_lambda, lru_w_out, mlp_w_up, mlp_w_down, loss_target, m_norm1, m_norm2, m_final_norm, m_ret_w_in, m_ret_gn_gain, m_ret_w_out, m_gdn_w_in, m_gdn_conv_w, m_gdn_a_log, m_gdn_dt_bias, m_gdn_norm_gain, m_gdn_w_out, m_gla_w_in, m_gla_w_gate_up, m_gla_gate_bias, m_gla_norm_gain, m_gla_w_out, m_lru_w_in, m_lru_conv_w, m_lru_conv_b, m_lru_w_rgate, m_lru_b_rgate, m_lru_w_igate, m_lru_b_igate, m_lru_lambda, m_lru_w_out, m_mlp_w_up, m_mlp_w_down, v_norm1, v_norm2, v_final_norm, v_ret_w_in, v_ret_gn_gain, v_ret_w_out, v_gdn_w_in, v_gdn_conv_w, v_gdn_a_log, v_gdn_dt_bias, v_gdn_norm_gain, v_gdn_w_out, v_gla_w_in, v_gla_w_gate_up, v_gla_gate_bias, v_gla_norm_gain, v_gla_w_out, v_lru_w_in, v_lru_conv_w, v_lru_conv_b, v_lru_w_rgate, v_lru_b_rgate, v_lru_w_igate, v_lru_b_igate, v_lru_lambda, v_lru_w_out, v_mlp_w_up, v_mlp_w_down):
    return _step(x, norm1, norm2, final_norm, ret_w_in, ret_gn_gain, ret_w_out, gdn_w_in, gdn_conv_w, gdn_a_log, gdn_dt_bias, gdn_norm_gain, gdn_w_out, gla_w_in, gla_w_gate_up, gla_gate_bias, gla_norm_gain, gla_w_out, lru_w_in, lru_conv_w, lru_conv_b, lru_w_rgate, lru_b_rgate, lru_w_igate, lru_b_igate, lru_lambda, lru_w_out, mlp_w_up, mlp_w_down, loss_target, m_norm1, m_norm2, m_final_norm, m_ret_w_in, m_ret_gn_gain, m_ret_w_out, m_gdn_w_in, m_gdn_conv_w, m_gdn_a_log, m_gdn_dt_bias, m_gdn_norm_gain, m_gdn_w_out, m_gla_w_in, m_gla_w_gate_up, m_gla_gate_bias, m_gla_norm_gain, m_gla_w_out, m_lru_w_in, m_lru_conv_w, m_lru_conv_b, m_lru_w_rgate, m_lru_b_rgate, m_lru_w_igate, m_lru_b_igate, m_lru_lambda, m_lru_w_out, m_mlp_w_up, m_mlp_w_down, v_norm1, v_norm2, v_final_norm, v_ret_w_in, v_ret_gn_gain, v_ret_w_out, v_gdn_w_in, v_gdn_conv_w, v_gdn_a_log, v_gdn_dt_bias, v_gdn_norm_gain, v_gdn_w_out, v_gla_w_in, v_gla_w_gate_up, v_gla_gate_bias, v_gla_norm_gain, v_gla_w_out, v_lru_w_in, v_lru_conv_w, v_lru_conv_b, v_lru_w_rgate, v_lru_b_rgate, v_lru_w_igate, v_lru_b_igate, v_lru_lambda, v_lru_w_out, v_mlp_w_up, v_mlp_w_down)
```

```python
import functools
import math

import numpy as np
import jax
import jax.numpy as jnp
from jax import lax
from jax.experimental import pallas as pl
from jax.experimental.pallas import tpu as pltpu

F32 = jnp.float32
BF16 = jnp.bfloat16

D_MODEL = 2048
DEPTH = 4
CHUNK = 64
D_FF = 4 * D_MODEL
NORM_EPS = 1e-6
N_DEV = 8

RET_HEADS, RET_DK, RET_DV = 8, 256, 512
RET_HB = 4
GDN_HEADS, GDN_DK, GDN_DV = 16, 128, 128
GDN_HB = 8
GDN_QKV = GDN_HEADS * (2 * GDN_DK + GDN_DV)
CONV_WIDTH = 4
GLA_HEADS, GLA_DK, GLA_DV = 4, 256, 512
GLA_HB = 4
GLA_GATE_RANK = 16
GLA_TAU = 16.0
LRU_WIDTH, LRU_BLOCKS, LRU_BLOCK = 2048, 16, 128
LRU_C = 8.0
ROPE_BASE = 10000.0

ADAM_LR, ADAM_B1, ADAM_B2, ADAM_EPS, ADAM_WD, ADAM_STEP = 0.001, 0.9, 0.999, 1e-08, 0.01, 10

LANE = 128
VMEM_LIMIT = 48 * 1024 * 1024

NN = (((1,), (0,)), ((), ()))
NT = (((1,), (1,)), ((), ()))
TN = (((0,), (0,)), ((), ()))


def _params(sem=None):
    return pltpu.CompilerParams(dimension_semantics=sem, vmem_limit_bytes=VMEM_LIMIT)


def _bdot(a, b, dn=NN):
    return lax.dot_general(a.astype(BF16), b.astype(BF16), dn, preferred_element_type=F32)


def _split(x):
    hi = x.astype(BF16)
    lo = (x - hi.astype(F32)).astype(BF16)
    return hi, lo


def _fdot(a, b, dn=NN):
    a1, a2 = _split(a)
    b1, b2 = _split(b)
    d = functools.partial(lax.dot_general, dimension_numbers=dn, preferred_element_type=F32)
    return d(a1, b1) + (d(a1, b2) + d(a2, b1))


def _sigmoid(x):
    return 1.0 / (1.0 + jnp.exp(-x))


def _softplus(x):
    return jnp.maximum(x, 0.0) + jnp.log(1.0 + jnp.exp(-jnp.abs(x)))


def _silu(x):
    return x * _sigmoid(x)


def _dsilu(x):
    s = _sigmoid(x)
    return s * (1.0 + x * (1.0 - s))


GELU_C = math.sqrt(2.0 / math.pi)


def _gelu(x):
    return 0.5 * x * (1.0 + jnp.tanh(GELU_C * (x + 0.044715 * x * x * x)))


def _dgelu(x):
    t = jnp.tanh(GELU_C * (x + 0.044715 * x * x * x))
    return 0.5 * (1.0 + t) + 0.5 * x * (1.0 - t * t) * GELU_C * (1.0 + 3.0 * 0.044715 * x * x)


def _expm1(x):
    poly = x * (1.0 + x * 0.5 * (1.0 + x * (1.0 / 3.0) * (1.0 + x * 0.25 * (1.0 + x * 0.2))))
    return jnp.where(jnp.abs(x) < 0.05, poly, jnp.exp(x) - 1.0)


def _iota2(shape, axis):
    return lax.broadcasted_iota(jnp.int32, shape, axis)


def _col_to_row(col):
    n = col.shape[0]
    eye = _iota2((n, n), 0) == _iota2((n, n), 1)
    return jnp.sum(jnp.where(eye, col, 0.0), axis=0, keepdims=True)


def _row_to_col(row):
    n = row.shape[1]
    eye = _iota2((n, n), 0) == _iota2((n, n), 1)
    return jnp.sum(jnp.where(eye, row, 0.0), axis=1, keepdims=True)


def _pick_row(x, r):
    rows = _iota2(x.shape, 0)
    return jnp.sum(jnp.where(rows == r, x, 0.0), axis=0, keepdims=True)


def _shift_down(x, s):
    if s == 0:
        return x
    y = pltpu.roll(x, s, 0)
    return jnp.where(_iota2(x.shape, 0) < s, 0.0, y)


def _shift_up(x, s):
    if s == 0:
        return x
    n = x.shape[0]
    y = pltpu.roll(x, n - s, 0)
    return jnp.where(_iota2(x.shape, 0) >= n - s, 0.0, y)


def _tile(dim, prefs):
    for p in prefs:
        if dim % p == 0:
            return p
    return dim


TOKEN_SHAPE = (8, LANE)


def _dep_specs(deps):
    return [pl.BlockSpec(TOKEN_SHAPE, lambda *_: (0, 0)) for _ in deps]


def _matmul(a, b, *, ta=False, tb=False, epi="none", extra=None, out_dtype=F32, name, deps=()):
    if ta:
        K, M = a.shape
    else:
        M, K = a.shape
    if tb:
        N, K2 = b.shape
    else:
        K2, N = b.shape
    assert K == K2, (a.shape, b.shape, ta, tb)
    if K <= 2048:
        tk = K
        if N <= 2048:
            tm, tn = _tile(M, (512, 256, 128)), N
        else:
            tm, tn = _tile(M, (1024, 512, 256, 128)), _tile(N, (512, 256, 128))
    else:
        tm, tn, tk = (_tile(d, (1024, 512, 256, 128)) for d in (M, N, K))
    nk = K // tk
    dn = (((0 if ta else 1,), (1 if tb else 0,)), ((), ()))
    n_extra = 0 if extra is None else 1
    n_out = 2 if epi == "relu2" else 1

    def body(*refs):
        a_ref, b_ref = refs[0], refs[1]
        e_ref = refs[2] if n_extra else None
        outs = refs[2 + n_extra + len(deps):2 + n_extra + len(deps) + n_out]

        def finish(r):
            if epi == "none":
                outs[0][...] = r.astype(outs[0].dtype)
            elif epi == "add":
                outs[0][...] = (r + e_ref[...]).astype(outs[0].dtype)
            elif epi == "relu2":
                outs[0][...] = r
                p = jnp.maximum(r, 0.0)
                outs[1][...] = (p * p).astype(outs[1].dtype)
            elif epi == "drelu2":
                outs[0][...] = (r * 2.0 * jnp.maximum(e_ref[...], 0.0)).astype(outs[0].dtype)

        def product():
            return lax.dot_general(a_ref[...].astype(BF16), b_ref[...].astype(BF16), dn, preferred_element_type=F32)

        if nk == 1:
            finish(product())
            return
        acc = refs[-1]
        k = pl.program_id(2)

        @pl.when(k == 0)
        def _():
            acc[...] = jnp.zeros_like(acc)

        acc[...] += product()

        @pl.when(k == nk - 1)
        def _():
            finish(acc[...])

    a_spec = pl.BlockSpec((tk, tm), lambda i, j, k: (k, i)) if ta else pl.BlockSpec((tm, tk), lambda i, j, k: (i, k))
    b_spec = pl.BlockSpec((tn, tk), lambda i, j, k: (j, k)) if tb else pl.BlockSpec((tk, tn), lambda i, j, k: (k, j))
    o_spec = pl.BlockSpec((tm, tn), lambda i, j, k: (i, j))
    in_specs = [a_spec, b_spec] + ([o_spec] if n_extra else []) + _dep_specs(deps)
    if epi == "relu2":
        out_shape = (jax.ShapeDtypeStruct((M, N), F32), jax.ShapeDtypeStruct((M, N), BF16))
        out_specs = (o_spec, o_spec)
    else:
        out_shape = jax.ShapeDtypeStruct((M, N), out_dtype)
        out_specs = o_spec
    args = (a, b) + ((extra,) if n_extra else ()) + tuple(deps)
    return pl.pallas_call(
        body, name=name, grid=(M // tm, N // tn, nk), in_specs=in_specs, out_specs=out_specs,
        out_shape=out_shape, scratch_shapes=[pltpu.VMEM((tm, tn), F32)] if nk > 1 else [],
        compiler_params=_params(("parallel", "parallel", "arbitrary")),
    )(*args)


ROW_BLOCK = 256


def _rmsnorm_fwd(x, g, *, name, deps=()):
    T, D = x.shape
    tr = _tile(T, (ROW_BLOCK, 128, 64))

    def body(x_ref, g_ref, *rest):
        o_ref = rest[-1]
        xv = x_ref[...]
        r = lax.rsqrt(jnp.mean(xv * xv, axis=-1, keepdims=True) + NORM_EPS)
        o_ref[...] = (xv * r * g_ref[...]).astype(o_ref.dtype)

    return pl.pallas_call(
        body, name=name, grid=(T // tr,),
        in_specs=[pl.BlockSpec((tr, D), lambda i: (i, 0)), pl.BlockSpec((1, D), lambda i: (0, 0))] + _dep_specs(deps),
        out_specs=pl.BlockSpec((tr, D), lambda i: (i, 0)),
        out_shape=jax.ShapeDtypeStruct((T, D), BF16), compiler_params=_params(("parallel",)),
    )(x, g, *deps)


def _rmsnorm_bwd(x, g, dy, dres, *, name, deps=()):
    T, D = x.shape
    tr = _tile(T, (ROW_BLOCK, 128, 64))

    def body(x_ref, g_ref, dy_ref, dres_ref, *rest):
        dx_ref, dxb_ref, dg_ref = rest[len(deps):]
        i = pl.program_id(0)
        xv = x_ref[...]
        r = lax.rsqrt(jnp.mean(xv * xv, axis=-1, keepdims=True) + NORM_EPS)
        xh = xv * r
        dyv = dy_ref[...].astype(F32)
        dxh = dyv * g_ref[...]
        dx = dres_ref[...] + r * (dxh - xh * jnp.mean(dxh * xh, axis=-1, keepdims=True))
        dx_ref[...] = dx
        dxb_ref[...] = dx.astype(dxb_ref.dtype)

        @pl.when(i == 0)
        def _():
            dg_ref[...] = jnp.zeros_like(dg_ref)

        dg_ref[...] += jnp.sum(dyv * xh, axis=0, keepdims=True)

    blk = pl.BlockSpec((tr, D), lambda i: (i, 0))
    vec = pl.BlockSpec((1, D), lambda i: (0, 0))
    return pl.pallas_call(
        body, name=name, grid=(T // tr,), in_specs=[blk, vec, blk, blk] + _dep_specs(deps), out_specs=(blk, blk, vec),
        out_shape=(jax.ShapeDtypeStruct((T, D), F32), jax.ShapeDtypeStruct((T, D), BF16),
                   jax.ShapeDtypeStruct((1, D), F32)),
        compiler_params=_params(("arbitrary",)),
    )(x, g, dy, dres, *deps)


def _final_loss(x, g, target, *, name):
    T, D = x.shape
    tr = _tile(T, (ROW_BLOCK, 128, 64))

    def body(x_ref, g_ref, t_ref, dy_ref, l_ref):
        i = pl.program_id(0)
        xv = x_ref[...]
        r = lax.rsqrt(jnp.mean(xv * xv, axis=-1, keepdims=True) + NORM_EPS)
        err = xv * r * g_ref[...] - t_ref[...]
        dy_ref[...] = err * (1.0 / D)

        @pl.when(i == 0)
        def _():
            l_ref[...] = jnp.zeros_like(l_ref)

        part = 0.5 * jnp.sum(jnp.mean(err * err, axis=-1, keepdims=True), axis=0, keepdims=True)
        l_ref[...] += jnp.broadcast_to(part, l_ref.shape)

    blk = pl.BlockSpec((tr, D), lambda i: (i, 0))
    vec = pl.BlockSpec((1, D), lambda i: (0, 0))
    return pl.pallas_call(
        body, name=name, grid=(T // tr,), in_specs=[blk, vec, blk],
        out_specs=(blk, pl.BlockSpec((1, LANE), lambda i: (0, 0))),
        out_shape=(jax.ShapeDtypeStruct((T, D), F32), jax.ShapeDtypeStruct((1, LANE), F32)),
        compiler_params=_params(("arbitrary",)),
    )(x, g, target)


def _colsum(x, *, name):
    T, C = x.shape
    tc = _tile(C, (512, 256, 128))

    def body(x_ref, o_ref):
        o_ref[...] = jnp.sum(x_ref[...], axis=0, keepdims=True)

    return pl.pallas_call(
        body, name=name, grid=(C // tc,), in_specs=[pl.BlockSpec((T, tc), lambda j: (0, j))],
        out_specs=pl.BlockSpec((1, tc), lambda j: (0, j)),
        out_shape=jax.ShapeDtypeStruct((1, C), F32), compiler_params=_params(("parallel",)),
    )(x)


ADAM_C1 = 1.0 - ADAM_B1 ** ADAM_STEP
ADAM_C2 = 1.0 - ADAM_B2 ** ADAM_STEP


def _adamw(w, m, v, grads, *, name):
    R, C = w.shape
    tr = _tile(R, (256, 128, 64, 32, 16, 8))
    n_g = len(grads)

    def body(*refs):
        w_ref, m_ref, v_ref = refs[:3]
        g_refs = refs[3:3 + n_g]
        g_out, d_out, m_out, v_out = refs[3 + n_g:]
        g = g_refs[0][...].astype(F32)
        for r in g_refs[1:]:
            g = g + r[...].astype(F32)
        mn = ADAM_B1 * m_ref[...] + (1.0 - ADAM_B1) * g
        vn = ADAM_B2 * v_ref[...] + (1.0 - ADAM_B2) * (g * g)
        m_hat = mn / ADAM_C1
        v_hat = vn / ADAM_C2
        g_out[...] = g
        d_out[...] = -ADAM_LR * (m_hat / (jnp.sqrt(v_hat) + ADAM_EPS) + ADAM_WD * w_ref[...])
        m_out[...] = mn
        v_out[...] = vn

    blk = pl.BlockSpec((tr, C), lambda i: (i, 0))
    sh = jax.ShapeDtypeStruct((R, C), F32)
    return pl.pallas_call(
        body, name=name, grid=(R // tr,), in_specs=[blk] * (3 + n_g), out_specs=(blk,) * 4,
        out_shape=(sh,) * 4, compiler_params=_params(("parallel",)),
    )(w, m, v, *grads)


def _sum_parts(parts, *, name, deps=()):
    P, R, C = parts.shape
    tr = _tile(R, (256, 128, 64, 32, 16, 8))

    def body(p_ref, *rest):
        o_ref = rest[-1]
        s = p_ref[0].astype(F32)
        for i in range(1, P):
            s = s + p_ref[i].astype(F32)
        o_ref[...] = s

    return pl.pallas_call(
        body, name=name, grid=(R // tr,),
        in_specs=[pl.BlockSpec((P, tr, C), lambda i: (0, i, 0))] + _dep_specs(deps),
        out_specs=pl.BlockSpec((tr, C), lambda i: (i, 0)),
        out_shape=jax.ShapeDtypeStruct((R, C), F32), compiler_params=_params(("parallel",)),
    )(parts, *deps)


def _ret_tables(T):
    H, C = RET_HEADS, CHUNK
    log_gamma = jnp.log1p(-jnp.exp2(-5.0 - jnp.arange(H, dtype=F32)))
    pos = jnp.arange(C, dtype=F32)
    dist = jnp.abs(pos[:, None] - pos[None, :])
    dm = jnp.exp(log_gamma[:, None, None] * dist)
    qdec = jnp.exp(log_gamma[:, None] * (pos + 1.0))[:, :, None]
    kdec = jnp.exp(log_gamma[:, None] * (C - 1.0 - pos))[:, :, None]
    cdec = jnp.exp(log_gamma * C)[:, None, None]
    inv = ROPE_BASE ** (-jnp.arange(0, RET_DK, 2, dtype=F32) / RET_DK)
    ang = jnp.arange(T, dtype=F32)[:, None] * inv[None, :]
    return dm, qdec, kdec, cdec, jnp.cos(ang), jnp.sin(ang)


def _rot(x, cos, sin):
    h = x.shape[1] // 2
    x1, x2 = x[:, :h], x[:, h:]
    return jnp.concatenate([x1 * cos - x2 * sin, x1 * sin + x2 * cos], axis=1)


def _unrot(dy, cos, sin):
    h = dy.shape[1] // 2
    d1, d2 = dy[:, :h], dy[:, h:]
    return jnp.concatenate([d1 * cos + d2 * sin, d2 * cos - d1 * sin], axis=1)


def _ret_specs(N, rev):
    H, C, DK, DV = RET_HEADS, CHUNK, RET_DK, RET_DV
    cn = (lambda n: N - 1 - n) if rev else (lambda n: n)
    HB, G = RET_HB, H // RET_HB
    q = pl.BlockSpec((C, HB * DK), lambda h, n: (cn(n), h))
    k = pl.BlockSpec((C, HB * DK), lambda h, n: (cn(n), G + h))
    v = pl.BlockSpec((C, HB * DV), lambda h, n: (cn(n), G + h))
    g = pl.BlockSpec((C, HB * DV), lambda h, n: (cn(n), 2 * G + h))
    cs = pl.BlockSpec((C, DK // 2), lambda h, n: (cn(n), 0))
    dm = pl.BlockSpec((HB, C, C), lambda h, n: (h, 0, 0))
    dec = pl.BlockSpec((HB, C, 1), lambda h, n: (h, 0, 0))
    cd = pl.BlockSpec((HB, 1, 1), lambda h, n: (h, 0, 0))
    gain = pl.BlockSpec((HB, 1, DV), lambda h, n: (h, 0, 0))
    st = pl.BlockSpec((HB, 1, DK, DV), lambda h, n: (h, cn(n), 0, 0))
    ov = pl.BlockSpec((C, HB * DV), lambda h, n: (cn(n), h))
    return q, k, v, g, cs, dm, dec, cd, gain, st, ov


def _ret_fwd(proj, gain, tables, *, name):
    T = proj.shape[0]
    H, C, DK, DV = RET_HEADS, CHUNK, RET_DK, RET_DV
    N = T // C
    dm_t, qdec_t, kdec_t, cdec_t, cos_t, sin_t = tables

    def body(q_ref, k_ref, v_ref, g_ref, cos_ref, sin_ref, dm_ref, qd_ref, kd_ref, cd_ref, gain_ref,
             og_ref, st_ref, s_acc):
        n = pl.program_id(1)

        @pl.when(n == 0)
        def _():
            s_acc[...] = jnp.zeros_like(s_acc)

        cos, sin = cos_ref[...], sin_ref[...]

        def head(hb):
            kc, vc = pl.ds(hb * DK, DK), pl.ds(hb * DV, DV)
            qr = _rot(q_ref[:, kc], cos, sin)
            kr = _rot(k_ref[:, kc], cos, sin) * (DK ** -0.5)
            v = v_ref[:, vc]
            sp = s_acc[hb]
            st_ref[hb, 0] = sp.astype(st_ref.dtype)
            scores = _bdot(qr, kr, NT) * dm_ref[hb]
            inter = _bdot(qr * qd_ref[hb], sp)
            s_acc[hb] = sp * cd_ref[hb] + _bdot(kr * kd_ref[hb], v, TN)
            yield
            o = _bdot(scores, v) + inter
            yield
            oc = o - jnp.mean(o, axis=-1, keepdims=True)
            rstd = lax.rsqrt(jnp.mean(oc * oc, axis=-1, keepdims=True) + NORM_EPS)
            og_ref[:, vc] = (oc * rstd * gain_ref[hb] * _silu(g_ref[:, vc])).astype(og_ref.dtype)

        _interleave([head(hb) for hb in range(RET_HB)])

    q, k, v, g, cs, dm, dec, cd, gn, st, ov = _ret_specs(N, False)
    return pl.pallas_call(
        body, name=name, grid=(H // RET_HB, N),
        in_specs=[q, k, v, g, cs, cs, dm, dec, dec, cd, gn], out_specs=(ov, st),
        out_shape=(jax.ShapeDtypeStruct((T, H * DV), BF16), jax.ShapeDtypeStruct((H, N, DK, DV), BF16)),
        scratch_shapes=[pltpu.VMEM((RET_HB, DK, DV), F32)],
        compiler_params=_params(("arbitrary", "arbitrary")),
    )(proj, proj, proj, proj, cos_t, sin_t, dm_t, qdec_t, kdec_t, cdec_t, gain)


def _ret_bwd(proj, gain, tables, states, dog, *, name):
    T = proj.shape[0]
    H, C, DK, DV = RET_HEADS, CHUNK, RET_DK, RET_DV
    N = T // C
    dm_t, qdec_t, kdec_t, cdec_t, cos_t, sin_t = tables

    def body(q_ref, k_ref, v_ref, g_ref, cos_ref, sin_ref, dm_ref, qd_ref, kd_ref, cd_ref, gain_ref,
             st_ref, dog_ref, dq_ref, dk_ref, dv_ref, dg_ref, dgain_ref, ds_acc):
        n = pl.program_id(1)

        @pl.when(n == 0)
        def _():
            ds_acc[...] = jnp.zeros_like(ds_acc)
            dgain_ref[...] = jnp.zeros_like(dgain_ref)

        cos, sin = cos_ref[...], sin_ref[...]
        scale = DK ** -0.5

        def head(hb):
            kc, vc = pl.ds(hb * DK, DK), pl.ds(hb * DV, DV)
            qr = _rot(q_ref[:, kc], cos, sin)
            kr = _rot(k_ref[:, kc], cos, sin) * scale
            v = v_ref[:, vc]
            g = g_ref[:, vc]
            sp = st_ref[hb, 0]
            dm = dm_ref[hb]
            qd, kd = qd_ref[hb], kd_ref[hb]
            gain_v = gain_ref[hb]
            scores = _bdot(qr, kr, NT) * dm
            inter = _bdot(qr * qd, sp)
            yield
            o = _bdot(scores, v) + inter
            yield
            oc = o - jnp.mean(o, axis=-1, keepdims=True)
            rstd = lax.rsqrt(jnp.mean(oc * oc, axis=-1, keepdims=True) + NORM_EPS)
            oh = oc * rstd
            dy = dog_ref[:, vc].astype(F32)
            dg_ref[:, vc] = (dy * oh * gain_v * _dsilu(g)).astype(dg_ref.dtype)
            dnorm = dy * _silu(g)
            dgain_ref[hb] += jnp.sum(dnorm * oh, axis=0, keepdims=True)
            doh = dnorm * gain_v
            do = rstd * (doh - jnp.mean(doh, axis=-1, keepdims=True)
                         - oh * jnp.mean(doh * oh, axis=-1, keepdims=True))
            dsn = ds_acc[hb]
            dp = _bdot(do, v, NT) * dm
            dq_inter = _bdot(do, sp, NT) * qd
            dk_inter = _bdot(v, dsn, NT) * kd
            dv_ref[:, vc] = (_bdot(scores, do, TN) + _bdot(kr * kd, dsn)).astype(dv_ref.dtype)
            ds_acc[hb] = dsn * cd_ref[hb] + _bdot(qr * qd, do, TN)
            yield
            dqr = _bdot(dp, kr) + dq_inter
            dkr = _bdot(dp, qr, TN) + dk_inter
            yield
            dq_ref[:, kc] = _unrot(dqr, cos, sin).astype(dq_ref.dtype)
            dk_ref[:, kc] = _unrot(dkr * scale, cos, sin).astype(dk_ref.dtype)

        _interleave([head(hb) for hb in range(RET_HB)])

    q, k, v, g, cs, dm, dec, cd, gn, st, ov = _ret_specs(N, True)
    return pl.pallas_call(
        body, name=name, grid=(H // RET_HB, N),
        in_specs=[q, k, v, g, cs, cs, dm, dec, dec, cd, gn, st, ov],
        out_specs=(q, q, ov, ov, gn),
        out_shape=(jax.ShapeDtypeStruct((T, H * DK), BF16), jax.ShapeDtypeStruct((T, H * DK), BF16),
                   jax.ShapeDtypeStruct((T, H * DV), BF16), jax.ShapeDtypeStruct((T, H * DV), BF16),
                   jax.ShapeDtypeStruct((H, 1, DV), F32)),
        scratch_shapes=[pltpu.VMEM((RET_HB, DK, DV), F32)],
        compiler_params=_params(("arbitrary", "arbitrary")),
    )(proj, proj, proj, proj, cos_t, sin_t, dm_t, qdec_t, kdec_t, cdec_t, gain, states, dog)


def _gla_specs(N, rev):
    H, C, DK, DV = GLA_HEADS, CHUNK, GLA_DK, GLA_DV
    cn = (lambda n: N - 1 - n) if rev else (lambda n: n)
    HB, G = GLA_HB, H // GLA_HB
    q = pl.BlockSpec((C, HB * DK), lambda h, n: (cn(n), h))
    k = pl.BlockSpec((C, HB * DK), lambda h, n: (cn(n), G + h))
    v = pl.BlockSpec((C, HB * DV), lambda h, n: (cn(n), G + h))
    r = pl.BlockSpec((C, HB * DV), lambda h, n: (cn(n), 2 * G + h))
    bias = pl.BlockSpec((1, HB * DK), lambda h, n: (0, h))
    gain = pl.BlockSpec((HB, 1, DV), lambda h, n: (h, 0, 0))
    st = pl.BlockSpec((HB, 1, DV, DK), lambda h, n: (h, cn(n), 0, 0))
    ov = pl.BlockSpec((C, HB * DV), lambda h, n: (cn(n), h))
    return q, k, v, r, bias, gain, st, ov


def _gla_chunk(q, k, v, gl_raw, bias):
    C, DK = q.shape
    gl = gl_raw + bias
    la = -_softplus(-gl) * (1.0 / GLA_TAU)
    lower = _iota2((C, C), 0) >= _iota2((C, C), 1)
    cum = _fdot(jnp.where(lower, 1.0, 0.0), la)
    yield
    ref = _pick_row(cum, C // 2 - 1)
    clast = _pick_row(cum, C - 1)
    fw, bw = jnp.exp(cum - ref), jnp.exp(ref - cum)
    qs = q * (DK ** -0.5)
    s_lo = _bdot(qs * fw, k * bw, NT)
    s_up = _bdot(qs * bw, k * fw, NT)
    yield
    scores = jnp.where(lower, s_lo, s_up)
    return gl, cum, clast, fw, bw, qs, k, v, scores, lower


def _gla_fwd(proj, glogit, bias, gain, *, name):
    T = proj.shape[0]
    H, C, DK, DV = GLA_HEADS, CHUNK, GLA_DK, GLA_DV
    N = T // C

    def body(q_ref, k_ref, v_ref, r_ref, gl_ref, bias_ref, gain_ref, og_ref, st_ref, s_acc):
        n = pl.program_id(1)

        @pl.when(n == 0)
        def _():
            s_acc[...] = jnp.zeros_like(s_acc)

        def head(hb):
            kc, vc = pl.ds(hb * DK, DK), pl.ds(hb * DV, DV)
            gl, cum, clast, fw, bw, qs, k, v, scores, lower = yield from _gla_chunk(
                q_ref[:, kc], k_ref[:, kc], v_ref[:, vc], gl_ref[:, kc], bias_ref[:, kc])
            sp = s_acc[hb]
            st_ref[hb, 0] = sp
            o = _bdot(scores, v) + _bdot(qs * jnp.exp(cum), sp, NT)
            s_acc[hb] = sp * jnp.exp(clast) + _bdot(v, k * jnp.exp(clast - cum), TN)
            yield
            rstd = lax.rsqrt(jnp.mean(o * o, axis=-1, keepdims=True) + NORM_EPS)
            og_ref[:, vc] = (o * rstd * gain_ref[hb] * _silu(r_ref[:, vc])).astype(og_ref.dtype)

        _interleave([head(hb) for hb in range(GLA_HB)])

    q, k, v, r, bias_s, gn, st, ov = _gla_specs(N, False)
    return pl.pallas_call(
        body, name=name, grid=(H // GLA_HB, N), in_specs=[q, k, v, r, q, bias_s, gn], out_specs=(ov, st),
        out_shape=(jax.ShapeDtypeStruct((T, H * DV), BF16), jax.ShapeDtypeStruct((H, N, DV, DK), F32)),
        scratch_shapes=[pltpu.VMEM((GLA_HB, DV, DK), F32)],
        compiler_params=_params(("arbitrary", "arbitrary")),
    )(proj, proj, proj, proj, glogit, bias, gain)


def _gla_bwd(proj, glogit, bias, gain, states, dog, *, name):
    T = proj.shape[0]
    H, C, DK, DV = GLA_HEADS, CHUNK, GLA_DK, GLA_DV
    N = T // C

    def body(q_ref, k_ref, v_ref, r_ref, gl_ref, bias_ref, gain_ref, st_ref, dog_ref,
             dq_ref, dk_ref, dv_ref, dr_ref, dgl_ref, dgain_ref, ds_acc):
        n = pl.program_id(1)

        @pl.when(n == 0)
        def _():
            ds_acc[...] = jnp.zeros_like(ds_acc)
            dgain_ref[...] = jnp.zeros_like(dgain_ref)

        def head(hb):
            kc, vc = pl.ds(hb * DK, DK), pl.ds(hb * DV, DV)
            gl, cum, clast, fw, bw, qs, k, v, scores, lower = yield from _gla_chunk(
                q_ref[:, kc], k_ref[:, kc], v_ref[:, vc], gl_ref[:, kc], bias_ref[:, kc])
            sp = st_ref[hb, 0]
            ecum, e2, cdec = jnp.exp(cum), jnp.exp(clast - cum), jnp.exp(clast)
            q_in, k_end = qs * ecum, k * e2
            o = _bdot(scores, v) + _bdot(q_in, sp, NT)
            yield
            rstd = lax.rsqrt(jnp.mean(o * o, axis=-1, keepdims=True) + NORM_EPS)
            oh = o * rstd
            r = r_ref[:, vc]
            gain_v = gain_ref[hb]
            dy = dog_ref[:, vc].astype(F32)
            dr_ref[:, vc] = (dy * oh * gain_v * _dsilu(r)).astype(dr_ref.dtype)
            dnorm = dy * _silu(r)
            dgain_ref[hb] += jnp.sum(dnorm * oh, axis=0, keepdims=True)
            doh = dnorm * gain_v
            do = rstd * (doh - oh * jnp.mean(doh * oh, axis=-1, keepdims=True))
            dsn = ds_acc[hb]
            dq_in = _bdot(do, sp)
            dk_end = _bdot(v, dsn)
            dv_ref[:, vc] = (_bdot(k_end, dsn, NT) + _bdot(scores, do, TN)).astype(dv_ref.dtype)
            dcdec = jnp.sum(dsn * sp, axis=0, keepdims=True)
            ds_acc[hb] = dsn * cdec + _bdot(do, q_in, TN)
            dsc = _bdot(do, v, NT)
            yield
            ds_lo = jnp.where(lower, dsc, 0.0)
            ds_up = jnp.where(lower, 0.0, dsc)
            qf, kb, qb, kf = qs * fw, k * bw, qs * bw, k * fw
            dqf, dkb = _bdot(ds_lo, kb), _bdot(ds_lo, qf, TN)
            dqb, dkf = _bdot(ds_up, kf), _bdot(ds_up, qb, TN)
            yield
            dq_ref[:, kc] = ((dqf * fw + dqb * bw + dq_in * ecum) * (DK ** -0.5)).astype(dq_ref.dtype)
            dk_ref[:, kc] = (dkb * bw + dkf * fw + dk_end * e2).astype(dk_ref.dtype)
            dz = (dqf * qs + dkf * k) * fw - (dqb * qs + dkb * k) * bw
            kk = dk_end * k_end
            dcum = dz + dq_in * q_in - kk
            rows = _iota2((C, DK), 0)
            dcum = dcum + jnp.where(rows == C // 2 - 1, -jnp.sum(dz, axis=0, keepdims=True), 0.0)
            dcum = dcum + jnp.where(rows == C - 1, jnp.sum(kk, axis=0, keepdims=True) + dcdec * cdec, 0.0)
            upper = _iota2((C, C), 0) <= _iota2((C, C), 1)
            dla = _fdot(jnp.where(upper, 1.0, 0.0), dcum)
            yield
            dgl_ref[:, kc] = dla * (1.0 / GLA_TAU) * _sigmoid(-gl)

        _interleave([head(hb) for hb in range(GLA_HB)])

    q, k, v, r, bias_s, gn, st, ov = _gla_specs(N, True)
    return pl.pallas_call(
        body, name=name, grid=(H // GLA_HB, N), in_specs=[q, k, v, r, q, bias_s, gn, st, ov],
        out_specs=(q, q, ov, ov, q, gn),
        out_shape=(jax.ShapeDtypeStruct((T, H * DK), BF16), jax.ShapeDtypeStruct((T, H * DK), BF16),
                   jax.ShapeDtypeStruct((T, H * DV), BF16), jax.ShapeDtypeStruct((T, H * DV), BF16),
                   jax.ShapeDtypeStruct((T, H * DK), F32), jax.ShapeDtypeStruct((H, 1, DV), F32)),
        scratch_shapes=[pltpu.VMEM((GLA_HB, DV, DK), F32)],
        compiler_params=_params(("arbitrary", "arbitrary")),
    )(proj, proj, proj, proj, glogit, bias, gain, states, dog)


def _conv(xv, w_ref):
    out = _shift_down(xv, CONV_WIDTH - 1) * w_ref[0:1, :]
    for tap in range(1, CONV_WIDTH):
        out = out + _shift_down(xv, CONV_WIDTH - 1 - tap) * w_ref[tap:tap + 1, :]
    return out


def _conv_bwd(xv, w_ref, dpre, dw_ref):
    dx = None
    for tap in range(CONV_WIDTH):
        s = CONV_WIDTH - 1 - tap
        t = _shift_up(dpre, s) * w_ref[tap:tap + 1, :]
        dx = t if dx is None else dx + t
        dw_ref[tap:tap + 1, :] = jnp.sum(dpre * _shift_down(xv, s), axis=0, keepdims=True)
    return dx


CONV_COLS = 256


def _conv_silu_fwd(x, w, *, name):
    T = x.shape[0]
    n = w.shape[1]

    def body(x_ref, w_ref, o_ref):
        o_ref[...] = _silu(_conv(x_ref[...], w_ref))

    return pl.pallas_call(
        body, name=name, grid=(n // CONV_COLS,),
        in_specs=[pl.BlockSpec((T, CONV_COLS), lambda j: (0, j)), pl.BlockSpec((CONV_WIDTH, CONV_COLS), lambda j: (0, j))],
        out_specs=pl.BlockSpec((T, CONV_COLS), lambda j: (0, j)),
        out_shape=jax.ShapeDtypeStruct((T, n), F32), compiler_params=_params(("parallel",)),
    )(x, w)


def _conv_silu_bwd(x, w, dact, *, name):
    T = x.shape[0]
    n = w.shape[1]

    def body(x_ref, w_ref, da_ref, dx_ref, dw_ref):
        xv = x_ref[...]
        dpre = da_ref[...] * _dsilu(_conv(xv, w_ref))
        dx_ref[...] = _conv_bwd(xv, w_ref, dpre, dw_ref).astype(dx_ref.dtype)

    blk = pl.BlockSpec((T, CONV_COLS), lambda j: (0, j))
    wb = pl.BlockSpec((CONV_WIDTH, CONV_COLS), lambda j: (0, j))
    return pl.pallas_call(
        body, name=name, grid=(n // CONV_COLS,), in_specs=[blk, wb, blk], out_specs=(blk, wb),
        out_shape=(jax.ShapeDtypeStruct((T, n), BF16), jax.ShapeDtypeStruct((CONV_WIDTH, n), F32)),
        compiler_params=_params(("parallel",)),
    )(x, w, dact)


def _interleave(gens):
    results = [None] * len(gens)
    live = list(range(len(gens)))
    while live:
        for i in list(live):
            try:
                next(gens[i])
            except StopIteration as done:
                results[i] = done.value
                live.remove(i)
    return results


def _unit_lower_inverse(a):
    n = a.shape[0]
    eye = jnp.where(_iota2((n, n), 0) == _iota2((n, n), 1), 1.0, 0.0)
    p = -a
    t = eye + p
    for _ in range(5):
        p = _fdot(p, p)
        yield
        t = t + _fdot(t, p)
        yield
    return t


def _gdn_specs(N, rev):
    H, C, DK, DV = GDN_HEADS, CHUNK, GDN_DK, GDN_DV
    cn = (lambda n: N - 1 - n) if rev else (lambda n: n)
    HB, G = GDN_HB, H // GDN_HB
    q = pl.BlockSpec((C, HB * DK), lambda h, n: (cn(n), h))
    k = pl.BlockSpec((C, HB * DK), lambda h, n: (cn(n), G + h))
    v = pl.BlockSpec((C, HB * DV), lambda h, n: (cn(n), 2 * G + h))
    z = pl.BlockSpec((C, HB * DV), lambda h, n: (cn(n), 3 * G + h))
    gates = pl.BlockSpec((C, LANE), lambda h, n: (cn(n), 0))
    sc = pl.BlockSpec((HB, 1, 1), lambda h, n: (h, 0, 0))
    gain = pl.BlockSpec((1, DV), lambda h, n: (0, 0))
    st = pl.BlockSpec((HB, 1, DK, DV), lambda h, n: (h, cn(n), 0, 0))
    return q, k, v, z, gates, sc, gain, st


def _gdn_chunk(q_ref, k_ref, v_ref, gates_ref, alog_ref, dtb_ref, h):
    H, C, DK, DV = GDN_HEADS, CHUNK, GDN_DK, GDN_DV
    gates = gates_ref[...]
    lane = _iota2(gates.shape, 1)
    bl = jnp.sum(jnp.where(lane == h, gates, 0.0), axis=1, keepdims=True)
    al = jnp.sum(jnp.where(lane == H + h, gates, 0.0), axis=1, keepdims=True)
    beta = _sigmoid(bl)
    ea = jnp.exp(alog_ref[...])
    xs = al + dtb_ref[...]
    la = -ea * _softplus(xs)
    ii, jj = _iota2((C, C), 0), _iota2((C, C), 1)
    strict = ii > jj
    cum_col = jnp.sum(jnp.where(ii >= jj, _col_to_row(la), 0.0), axis=1, keepdims=True)
    cum_row = jnp.sum(jnp.where(ii <= jj, la, 0.0), axis=0, keepdims=True)
    q, k, v = q_ref[...], k_ref[...], v_ref[...]
    rq = lax.rsqrt(jnp.sum(q * q, axis=-1, keepdims=True) + NORM_EPS)
    rk = lax.rsqrt(jnp.sum(k * k, axis=-1, keepdims=True) + NORM_EPS)
    qn = q * rq * (DK ** -0.5)
    kn = k * rk
    rel = jnp.where(strict, jnp.exp(jnp.where(strict, cum_col - cum_row, 0.0)), 0.0)
    rg = rel * _bdot(kn, kn, NT)
    yield
    a = beta * rg
    tm = yield from _unit_lower_inverse(a)
    e_col = jnp.exp(cum_col)
    clast = _pick_row(cum_col, C - 1)
    rhs = jnp.concatenate([beta * v, (beta * e_col) * kn], axis=1)
    sol = _fdot(tm, rhs)
    yield
    u, w = sol[:, :DV], sol[:, DV:]
    dd = jnp.exp(clast - cum_col)
    ke = kn * dd
    g = jnp.exp(clast)
    eye = jnp.where(_iota2((DK, DK), 0) == _iota2((DK, DK), 1), 1.0, 0.0)
    trans = g * eye - _bdot(ke, w, TN)
    inject = _bdot(ke, u, TN)
    yield
    return dict(beta=beta, ea=ea, xs=xs, la=la, strict=strict, ii=ii, jj=jj, q=q, k=k, v=v, rq=rq, rk=rk,
                qn=qn, kn=kn, rel=rel, rg=rg, a=a, tm=tm, e_col=e_col, sol=sol, u=u, w=w, dd=dd, ke=ke,
                g=g, eye=eye, trans=trans, inject=inject)


def _gdn_fwd(act, proj, gates, a_log, dt_bias, gain, *, name):
    T = act.shape[0]
    H, C, DK, DV = GDN_HEADS, CHUNK, GDN_DK, GDN_DV
    N = T // C

    def body(q_ref, k_ref, v_ref, z_ref, gates_ref, alog_ref, dtb_ref, gain_ref, og_ref, st_ref, s_acc):
        hg, n = pl.program_id(0), pl.program_id(1)

        @pl.when(n == 0)
        def _():
            s_acc[...] = jnp.zeros_like(s_acc)

        def head(hb):
            cols = pl.ds(hb * DK, DK)
            c = yield from _gdn_chunk(q_ref.at[:, cols], k_ref.at[:, cols], v_ref.at[:, cols], gates_ref,
                                      alog_ref.at[hb], dtb_ref.at[hb], hg * GDN_HB + hb)
            sp = s_acc[hb]
            st_ref[hb, 0] = sp
            snew = _bdot(c["trans"], sp) + c["inject"]
            yield
            s_acc[hb] = snew
            o = _bdot(c["qn"], snew)
            yield
            rstd = lax.rsqrt(jnp.mean(o * o, axis=-1, keepdims=True) + NORM_EPS)
            og_ref[:, cols] = (o * rstd * gain_ref[...] * _silu(z_ref[:, cols])).astype(og_ref.dtype)

        _interleave([head(hb) for hb in range(GDN_HB)])

    q, k, v, z, gt, sc, gn, st = _gdn_specs(N, False)
    return pl.pallas_call(
        body, name=name, grid=(H // GDN_HB, N), in_specs=[q, k, v, z, gt, sc, sc, gn], out_specs=(q, st),
        out_shape=(jax.ShapeDtypeStruct((T, H * DV), BF16), jax.ShapeDtypeStruct((H, N, DK, DV), F32)),
        scratch_shapes=[pltpu.VMEM((GDN_HB, DK, DV), F32)],
        compiler_params=_params(("arbitrary", "arbitrary")),
    )(act, act, act, proj, gates, a_log, dt_bias, gain)


def _gdn_bwd(act, proj, gates, a_log, dt_bias, gain, states, dog, *, name):
    T = act.shape[0]
    H, C, DK, DV = GDN_HEADS, CHUNK, GDN_DK, GDN_DV
    N = T // C

    def rsum(x):
        return jnp.sum(x, axis=1, keepdims=True)

    def body(q_ref, k_ref, v_ref, z_ref, gates_ref, alog_ref, dtb_ref, gain_ref, st_ref, dog_ref,
             dq_ref, dk_ref, dv_ref, dz_ref, dgates_ref, dscal_ref, dgain_ref, ds_acc):
        hg, n = pl.program_id(0), pl.program_id(1)

        @pl.when(n == 0)
        def _():
            ds_acc[...] = jnp.zeros_like(ds_acc)
            dgain_ref[...] = jnp.zeros_like(dgain_ref)
            dscal_ref[...] = jnp.zeros_like(dscal_ref)

        _interleave([one_head(hb, hg * GDN_HB + hb, q_ref, k_ref, v_ref, z_ref, gates_ref, alog_ref, dtb_ref, gain_ref,
                              st_ref, dog_ref, dq_ref, dk_ref, dv_ref, dz_ref, dgates_ref, dscal_ref, dgain_ref, ds_acc)
                     for hb in range(GDN_HB)])

    def one_head(hb, h, q_ref, k_ref, v_ref, z_ref, gates_ref, alog_ref, dtb_ref, gain_ref, st_ref, dog_ref,
                 dq_ref, dk_ref, dv_ref, dz_ref, dgates_ref, dscal_ref, dgain_ref, ds_acc):
        cols = pl.ds(hb * DK, DK)
        c = yield from _gdn_chunk(q_ref.at[:, cols], k_ref.at[:, cols], v_ref.at[:, cols], gates_ref,
                                  alog_ref.at[hb], dtb_ref.at[hb], h)
        beta, kn, qn, v, ke, u, w, dd, e_col = c["beta"], c["kn"], c["qn"], c["v"], c["ke"], c["u"], c["w"], c["dd"], c["e_col"]
        sp = st_ref[hb, 0]
        snew = _bdot(c["trans"], sp) + c["inject"]
        yield
        o = _bdot(qn, snew)
        yield
        rstd = lax.rsqrt(jnp.mean(o * o, axis=-1, keepdims=True) + NORM_EPS)
        oh = o * rstd
        z = z_ref[:, cols]
        gain_v = gain_ref[...]
        dy = dog_ref[:, cols].astype(F32)
        dz_ref[:, cols] = (dy * oh * gain_v * _dsilu(z)).astype(dz_ref.dtype)
        dnorm = dy * _silu(z)
        dgain_ref[hb] += jnp.sum(dnorm * oh, axis=0, keepdims=True)
        doh = dnorm * gain_v
        do = rstd * (doh - oh * jnp.mean(doh * oh, axis=-1, keepdims=True))

        dstot = ds_acc[hb] + _bdot(qn, do, TN)
        dqn = _bdot(do, snew, NT)
        yield
        dtrans = _bdot(dstot, sp, NT)
        ds_acc[hb] = _bdot(c["trans"], dstot, TN)
        yield
        dg = jnp.sum(jnp.sum(dtrans * c["eye"], axis=1, keepdims=True), axis=0, keepdims=True)
        m = -dtrans
        dke = _bdot(w, m, NT) + _bdot(u, dstot, NT)
        dw = _bdot(ke, m)
        du = _bdot(ke, dstot)
        yield
        drhs = _fdot(c["tm"], jnp.concatenate([du, dw], axis=1), TN)
        yield
        da = jnp.where(c["strict"], -_fdot(drhs, c["sol"], NT), 0.0)
        yield
        drhs_u, drhs_w = drhs[:, :DV], drhs[:, DV:]
        rwk = rsum(drhs_w * kn)
        dbeta = rsum(da * c["rg"]) + rsum(drhs_u * v) + rwk * e_col
        dgm = da * beta * c["rel"]
        dkn = _bdot(dgm, kn) + _bdot(dgm, kn, TN) + (beta * e_col) * drhs_w + dd * dke
        yield
        dv_ref[:, cols] = beta * drhs_u
        r_ = da * c["a"]
        ddd = rsum(dke * kn)
        dc = rsum(r_) - _row_to_col(jnp.sum(r_, axis=0, keepdims=True)) + beta * rwk * e_col - ddd * dd
        dclast = jnp.sum(ddd * dd, axis=0, keepdims=True) + dg * c["g"]
        dc = dc + jnp.where(_iota2((C, 1), 0) == C - 1, dclast, 0.0)
        dla = jnp.sum(jnp.where(c["ii"] <= c["jj"], _col_to_row(dc), 0.0), axis=1, keepdims=True)
        dalog = jnp.sum(dla * c["la"], axis=0, keepdims=True)
        dxs = dla * (-c["ea"]) * _sigmoid(c["xs"])
        ddtb = jnp.sum(dxs, axis=0, keepdims=True)
        dbl = dbeta * beta * (1.0 - beta)
        lane = _iota2((C, LANE), 1)
        dgates_ref[hb] = jnp.where(lane == 0, dbl, jnp.where(lane == 1, dxs, 0.0))
        lane8 = _iota2((8, LANE), 1)
        dscal_ref[hb] += jnp.where(lane8 == 0, dalog, jnp.where(lane8 == 1, ddtb, 0.0))
        dk_ref[:, cols] = c["rk"] * (dkn - kn * rsum(dkn * kn))
        qh = c["q"] * c["rq"]
        dqs = dqn * (DK ** -0.5)
        dq_ref[:, cols] = c["rq"] * (dqs - qh * rsum(dqs * qh))

    q, k, v, z, gt, sc, gn, st = _gdn_specs(N, True)
    dgt = pl.BlockSpec((GDN_HB, C, LANE), lambda h, n: (h, N - 1 - n, 0))
    dsc = pl.BlockSpec((GDN_HB, 8, LANE), lambda h, n: (h, 0, 0))
    dgn = pl.BlockSpec((GDN_HB, 1, DV), lambda h, n: (h, 0, 0))
    sh = jax.ShapeDtypeStruct((T, H * DK), F32)
    return pl.pallas_call(
        body, name=name, grid=(H // GDN_HB, N), in_specs=[q, k, v, z, gt, sc, sc, gn, st, q],
        out_specs=(q, q, q, q, dgt, dsc, dgn),
        out_shape=(sh, sh, sh, jax.ShapeDtypeStruct((T, H * DV), BF16),
                   jax.ShapeDtypeStruct((H, T, LANE), F32), jax.ShapeDtypeStruct((H, 8, LANE), F32),
                   jax.ShapeDtypeStruct((H, 1, DV), F32)),
        scratch_shapes=[pltpu.VMEM((GDN_HB, DK, DV), F32)],
        compiler_params=_params(("arbitrary", "arbitrary")),
    )(act, act, act, proj, gates, a_log, dt_bias, gain, states, dog)


SUBLANES = 8


def _linear_scan(a_ref, b_ref, h_ref, reverse):
    T, W = a_ref.shape
    nb = T // SUBLANES
    row = _iota2((SUBLANES, W), 0)

    def blk(bi, carry):
        bb = (nb - 1 - bi) if reverse else bi
        off = pl.multiple_of(bb * SUBLANES, SUBLANES)
        a = a_ref[pl.ds(off, SUBLANES), :]
        b = b_ref[pl.ds(off, SUBLANES), :]
        for d in (1, 2, 4):
            if reverse:
                edge = row >= SUBLANES - d
                a_sh = jnp.where(edge, 1.0, pltpu.roll(a, SUBLANES - d, 0))
                b_sh = jnp.where(edge, 0.0, pltpu.roll(b, SUBLANES - d, 0))
            else:
                edge = row < d
                a_sh = jnp.where(edge, 1.0, pltpu.roll(a, d, 0))
                b_sh = jnp.where(edge, 0.0, pltpu.roll(b, d, 0))
            b = a * b_sh + b
            a = a * a_sh
        h = a * carry + b
        h_ref[pl.ds(off, SUBLANES), :] = h
        return h[0:1, :] if reverse else h[SUBLANES - 1:SUBLANES, :]

    lax.fori_loop(0, nb, blk, jnp.zeros((1, W), F32))


def _lru_specs(T):
    B, W = LRU_BLOCKS, LRU_BLOCK
    xb = pl.BlockSpec((T, W), lambda j: (0, j))
    yb = pl.BlockSpec((T, W), lambda j: (0, B + j))
    cw = pl.BlockSpec((CONV_WIDTH, W), lambda j: (0, j))
    vec = pl.BlockSpec((1, W), lambda j: (0, j))
    wg = pl.BlockSpec((1, W, W), lambda j: (j, 0, 0))
    bg = pl.BlockSpec((1, 1, W), lambda j: (j, 0, 0))
    return xb, yb, cw, vec, wg, bg


def _lru_gates(xb_ref, cw_ref, cb_ref, wr_ref, br_ref, wi_ref, bi_ref, lam_ref):
    xv = xb_ref[...]
    xc = _conv(xv, cw_ref) + cb_ref[...]
    r = _sigmoid(_bdot(xc, wr_ref[0]) + br_ref[0])
    i = _sigmoid(_bdot(xc, wi_ref[0]) + bi_ref[0])
    sp = _softplus(-lam_ref[...])
    la = -LRU_C * sp * r
    a = jnp.exp(la)
    s = jnp.sqrt(-_expm1(2.0 * la))
    return xv, xc, r, i, sp, a, s


def _lru_fwd(proj, conv_w, conv_b, w_r, b_r, w_i, b_i, lam, *, name):
    T = proj.shape[0]
    B, W = LRU_BLOCKS, LRU_BLOCK

    def body(xb_ref, yb_ref, cw_ref, cb_ref, wr_ref, br_ref, wi_ref, bi_ref, lam_ref, og_ref, hs_ref, a_s, u_s):
        xv, xc, r, i, sp, a, s = _lru_gates(xb_ref, cw_ref, cb_ref, wr_ref, br_ref, wi_ref, bi_ref, lam_ref)
        a_s[...] = a
        u_s[...] = s * (i * xc)
        _linear_scan(a_s, u_s, hs_ref, False)
        og_ref[...] = (hs_ref[...] * _gelu(yb_ref[...])).astype(og_ref.dtype)

    xb, yb, cw, vec, wg, bg = _lru_specs(T)
    return pl.pallas_call(
        body, name=name, grid=(B,), in_specs=[xb, yb, cw, vec, wg, bg, wg, bg, vec], out_specs=(xb, xb),
        out_shape=(jax.ShapeDtypeStruct((T, B * W), BF16), jax.ShapeDtypeStruct((T, B * W), F32)),
        scratch_shapes=[pltpu.VMEM((T, W), F32), pltpu.VMEM((T, W), F32)],
        compiler_params=_params(("arbitrary",)),
    )(proj, proj, conv_w, conv_b, w_r, b_r, w_i, b_i, lam)


def _lru_bwd(proj, conv_w, conv_b, w_r, b_r, w_i, b_i, lam, hs, dout, *, name):
    T = proj.shape[0]
    B, W = LRU_BLOCKS, LRU_BLOCK

    def csum(x):
        return jnp.sum(x, axis=0, keepdims=True)

    def body(xb_ref, yb_ref, cw_ref, cb_ref, wr_ref, br_ref, wi_ref, bi_ref, lam_ref, hs_ref, do_ref,
             dxb_ref, dyb_ref, dcw_ref, dcb_ref, dwr_ref, dbr_ref, dwi_ref, dbi_ref, dlam_ref, a_s, b_s, g_s):
        xv, xc, r, i, sp, a, s = _lru_gates(xb_ref, cw_ref, cb_ref, wr_ref, br_ref, wi_ref, bi_ref, lam_ref)
        h = hs_ref[...]
        yb = yb_ref[...]
        dout = do_ref[...].astype(F32)
        dyb_ref[...] = (dout * h * _dgelu(yb)).astype(dyb_ref.dtype)
        a_s[...] = _shift_up(a, 1)
        b_s[...] = dout * _gelu(yb)
        _linear_scan(a_s, b_s, g_s, True)
        g = g_s[...]
        da = g * _shift_down(h, 1)
        ds = g * (i * xc)
        di = g * s * xc
        dxc = g * s * i
        dla = da * a - ds * (a * a) / s
        dr = dla * (-LRU_C * sp)
        dlam_ref[...] = csum(dla * r) * (LRU_C * _sigmoid(-lam_ref[...]))
        dpr = dr * r * (1.0 - r)
        dpi = di * i * (1.0 - i)
        dxc = dxc + _bdot(dpr, wr_ref[0], NT) + _bdot(dpi, wi_ref[0], NT)
        dwr_ref[0] = _bdot(xc, dpr, TN)
        dwi_ref[0] = _bdot(xc, dpi, TN)
        dbr_ref[0] = csum(dpr)
        dbi_ref[0] = csum(dpi)
        dcb_ref[...] = csum(dxc)
        dxb_ref[...] = _conv_bwd(xv, cw_ref, dxc, dcw_ref).astype(dxb_ref.dtype)

    xb, yb, cw, vec, wg, bg = _lru_specs(T)
    act = jax.ShapeDtypeStruct((T, B * W), BF16)
    return pl.pallas_call(
        body, name=name, grid=(B,), in_specs=[xb, yb, cw, vec, wg, bg, wg, bg, vec, xb, xb],
        out_specs=(xb, xb, cw, vec, wg, bg, wg, bg, vec),
        out_shape=(act, act, jax.ShapeDtypeStruct((CONV_WIDTH, B * W), F32), jax.ShapeDtypeStruct((1, B * W), F32),
                   jax.ShapeDtypeStruct((B, W, W), F32), jax.ShapeDtypeStruct((B, 1, W), F32),
                   jax.ShapeDtypeStruct((B, W, W), F32), jax.ShapeDtypeStruct((B, 1, W), F32),
                   jax.ShapeDtypeStruct((1, B * W), F32)),
        scratch_shapes=[pltpu.VMEM((T, W), F32)] * 3,
        compiler_params=_params(("arbitrary",)),
    )(proj, proj, conv_w, conv_b, w_r, b_r, w_i, b_i, lam, hs, dout)


MESH = pl.DeviceIdType.MESH
N_CHIPS = 4
AG_COPIES = 7


def _mesh_pos():
    return lax.axis_index("x"), lax.axis_index("y"), lax.axis_index("c")


def _hbm_specs(n):
    return [pl.BlockSpec(memory_space=pltpu.HBM)] * n


def _all_gather(shards, *, name):
    n = len(shards)

    def body(*refs):
        xs, outs = refs[:n], refs[n:2 * n]
        send_sems, recv_sems, local_sems = refs[2 * n:]
        x, y, c = _mesh_pos()
        me, sibling = (x, y, c), (x, y, 1 - c)
        chips = [(1 - x, y), (x, 1 - y), (1 - x, 1 - y)]

        def rows(t, px, py, pc):
            return outs[t].at[4 * px + 2 * py + pc]

        def copy(t, k, block, to, src=None):
            return pltpu.make_async_remote_copy(
                src_ref=rows(t, *block) if src is None else src, dst_ref=rows(t, *block),
                send_sem=send_sems.at[t * AG_COPIES + k], recv_sem=recv_sems.at[t * AG_COPIES + k],
                device_id=to, device_id_type=MESH)

        mine = [pltpu.make_async_copy(xs[t], rows(t, *me), local_sems.at[t]) for t in range(n)]
        for cp in mine:
            cp.start()
        first = []
        for t in range(n):
            first.append(copy(t, 0, me, sibling, src=xs[t]))
            first += [copy(t, 1 + j, me, (*chip, c), src=xs[t]) for j, chip in enumerate(chips)]
        for cp in first:
            cp.start()
        passed = []
        for j, chip in enumerate(chips):
            for t in range(n):
                copy(t, 1 + j, (*chip, c), me).wait_recv()
                cp = copy(t, 4 + j, (*chip, c), sibling)
                cp.start()
                passed.append(cp)
        for t in range(n):
            copy(t, 0, sibling, me).wait_recv()
        for j, chip in enumerate(chips):
            for t in range(n):
                copy(t, 4 + j, (*chip, 1 - c), me).wait_recv()
        for cp in first + passed:
            cp.wait_send()
        for cp in mine:
            cp.wait()

    return pl.pallas_call(
        body, name=name,
        out_shape=[jax.ShapeDtypeStruct((N_DEV,) + s.shape, s.dtype) for s in shards],
        in_specs=_hbm_specs(n), out_specs=_hbm_specs(n),
        scratch_shapes=[pltpu.SemaphoreType.DMA((n * AG_COPIES,)), pltpu.SemaphoreType.DMA((n * AG_COPIES,)),
                        pltpu.SemaphoreType.DMA((n,))],
    )(*shards)


SIDE_EFFECT = pltpu.SideEffectType.DATAFLOW_SIDE_EFFECTING


def _copies(plan, refs, send_sems, recv_sems):
    return [pltpu.make_async_remote_copy(src_ref=src, dst_ref=dst, send_sem=send_sems.at[k], recv_sem=recv_sems.at[k],
                                         device_id=to, device_id_type=MESH)
            for k, (src, dst, to) in enumerate(plan(refs))]


def _split_start(bufs, plan, n_copies, *, name, deps=()):
    n = len(bufs)

    def body(*refs):
        send_sems, recv_sems = refs[n + len(deps)], refs[n + len(deps) + 1]
        token = refs[-1]
        for cp in _copies(plan, refs[:n], send_sems, recv_sems):
            cp.start()
        token[...] = jnp.zeros_like(token)

    hbm, sem = pl.BlockSpec(memory_space=pltpu.HBM), pl.BlockSpec(memory_space=pltpu.SEMAPHORE)
    out = pl.pallas_call(
        body, name=name,
        out_shape=(pltpu.SemaphoreType.DMA((n_copies,)), pltpu.SemaphoreType.DMA((n_copies,)),
                   *[pltpu.HBM(b.shape, b.dtype) for b in bufs], jax.ShapeDtypeStruct(TOKEN_SHAPE, F32)),
        in_specs=[hbm] * n + [pl.BlockSpec(memory_space=pl.ANY)] * len(deps),
        out_specs=(sem, sem, *[hbm] * n, pl.BlockSpec(memory_space=pltpu.VMEM)),
        input_output_aliases={i: 2 + i for i in range(n)},
        compiler_params=pltpu.CompilerParams(has_side_effects=SIDE_EFFECT),
    )(*[pltpu.with_memory_space_constraint(b, pltpu.HBM) for b in bufs], *deps)
    return out[0], out[1], list(out[2:2 + n]), out[-1]


def _split_wait(send_sems, recv_sems, bufs, plan, after, *, name):
    n = len(bufs)

    def body(*refs):
        for cp in _copies(plan, refs[:n], refs[n], refs[n + 1]):
            cp.wait_send()
            cp.wait_recv()

    hbm, sem = pl.BlockSpec(memory_space=pltpu.HBM), pl.BlockSpec(memory_space=pltpu.SEMAPHORE)
    out = pl.pallas_call(
        body, name=name, out_shape=tuple(pltpu.HBM(b.shape, b.dtype) for b in bufs),
        in_specs=[hbm] * n + [sem, sem, pl.BlockSpec(memory_space=pl.ANY)], out_specs=tuple([hbm] * n),
        input_output_aliases={i: i for i in range(n)},
        compiler_params=pltpu.CompilerParams(has_side_effects=SIDE_EFFECT),
    )(*bufs, send_sems, recv_sems, after)
    return list(out)


def _block(ref, in_cols, d):
    if not in_cols:
        return ref.at[d]
    c = ref.shape[1] // N_DEV
    return ref.at[:, pl.ds(pl.multiple_of(d * c, LANE), c)]


def _plan_gather_a(in_cols):
    n = len(in_cols)

    def plan(refs):
        x, y, c = _mesh_pos()
        me = 4 * x + 2 * y + c
        peers = [(x, y, 1 - c), (1 - x, y, c), (x, 1 - y, c), (1 - x, 1 - y, c)]
        return [(refs[t], _block(refs[n + t], in_cols[t], me), to) for t in range(n) for to in peers]
    return plan


def _plan_gather_b(in_cols):
    n = len(in_cols)

    def plan(refs):
        x, y, c = _mesh_pos()
        ds = [4 * px + 2 * py + c for px, py in [(1 - x, y), (x, 1 - y), (1 - x, 1 - y)]]
        return [(_block(refs[t], in_cols[t], d), _block(refs[t], in_cols[t], d), (x, y, 1 - c))
                for t in range(n) for d in ds]
    return plan


def _plan_scatter_pair(in_cols):
    n = len(in_cols)

    def plan(refs):
        x, y, c = _mesh_pos()
        return [(_block(refs[t], in_cols[t], 2 * q + (1 - c)), refs[n + t].at[q], (x, y, 1 - c))
                for t in range(n) for q in range(N_CHIPS)]
    return plan


def _plan_scatter_chips(n):
    def plan(refs):
        x, y, c = _mesh_pos()
        chips = [(1 - x, y), (x, 1 - y), (1 - x, 1 - y)]
        return [(refs[t].at[2 * px + py], refs[n + t].at[j], (px, py, c))
                for t in range(n) for j, (px, py) in enumerate(chips)]
    return plan


PAIR_ROWS = (512, 256)


def _pair_add(g, a, core, in_cols, *, name):
    _, R, C = a.shape
    tr = _tile(R, PAIR_ROWS)

    def body(c_ref, g_ref, a_ref, o_ref):
        gv = g_ref[...] if in_cols else g_ref[0]
        o_ref[0] = (gv.astype(F32) + a_ref[0].astype(F32)).astype(o_ref.dtype)

    blk = pl.BlockSpec((1, tr, C), lambda q, i, c: (q, i, 0))
    if in_cols:
        g_spec = pl.BlockSpec((tr, C), lambda q, i, c: (i, 2 * q + c[0]))
    else:
        g_spec = pl.BlockSpec((1, tr, C), lambda q, i, c: (2 * q + c[0], i, 0))
    return pl.pallas_call(
        body, name=name,
        grid_spec=pltpu.PrefetchScalarGridSpec(
            num_scalar_prefetch=1, grid=(N_CHIPS, R // tr), in_specs=[g_spec, blk], out_specs=blk),
        out_shape=jax.ShapeDtypeStruct((N_CHIPS, R, C), BF16),
        compiler_params=_params(("parallel", "parallel")),
    )(core, g, a)


ADAM_ROWS = 256


def _adamw_sharded(w, m, v, s4, b3, chip, layer, prev, *, name):
    L, R, C = w.shape
    tr = _tile(R, (ADAM_ROWS,))
    if prev is None and L > 1:
        prev = tuple(lax.empty(w.shape, F32) for _ in range(4))
    n_prev = 0 if prev is None else 4

    def body(q_ref, w_ref, m_ref, v_ref, s_ref, b_ref, *rest):
        g_out, d_out, m_out, v_out = rest[n_prev:]
        g = s_ref[0].astype(F32)
        for j in range(N_CHIPS - 1):
            g = g + b_ref[j].astype(F32)
        mn = ADAM_B1 * m_ref[0] + (1.0 - ADAM_B1) * g
        vn = ADAM_B2 * v_ref[0] + (1.0 - ADAM_B2) * (g * g)
        g_out[0] = g
        d_out[0] = -ADAM_LR * ((mn / ADAM_C1) / (jnp.sqrt(vn / ADAM_C2) + ADAM_EPS) + ADAM_WD * w_ref[0])
        m_out[0] = mn
        v_out[0] = vn

    blk = pl.BlockSpec((1, tr, C), lambda i, q: (layer, i, 0))
    sh = jax.ShapeDtypeStruct((L, R, C), F32)
    return pl.pallas_call(
        body, name=name,
        grid_spec=pltpu.PrefetchScalarGridSpec(
            num_scalar_prefetch=1, grid=(R // tr,),
            in_specs=[blk, blk, blk, pl.BlockSpec((1, tr, C), lambda i, q: (q[0], i, 0)),
                      pl.BlockSpec((N_CHIPS - 1, tr, C), lambda i, q: (0, i, 0))]
            + [pl.BlockSpec(memory_space=pl.ANY)] * n_prev,
            out_specs=(blk,) * 4),
        out_shape=(sh,) * 4, input_output_aliases={6 + k: k for k in range(n_prev)},
        compiler_params=_params(("parallel",)),
    )(chip, w, m, v, s4, b3, *(prev or ()))


FWD_NAMES = ['x', 'norm1', 'norm2', 'final_norm', 'ret_w_in', 'ret_gn_gain', 'ret_w_out', 'gdn_w_in', 'gdn_conv_w',
             'gdn_a_log', 'gdn_dt_bias', 'gdn_norm_gain', 'gdn_w_out', 'gla_w_in', 'gla_w_gate_up', 'gla_gate_bias',
             'gla_norm_gain', 'gla_w_out', 'lru_w_in', 'lru_conv_w', 'lru_conv_b', 'lru_w_rgate', 'lru_b_rgate',
             'lru_w_igate', 'lru_b_igate', 'lru_lambda', 'lru_w_out', 'mlp_w_up', 'mlp_w_down']
WEIGHT_NAMES = FWD_NAMES[1:]
ARG_NAMES = FWD_NAMES + ['loss_target'] + ['m_' + n for n in WEIGHT_NAMES] + ['v_' + n for n in WEIGHT_NAMES]

MIXER_IN = ('ret_w_in', 'gdn_w_in', 'gla_w_in', 'lru_w_in')
MIXER_OUT = ('ret_w_out', 'gdn_w_out', 'gla_w_out', 'lru_w_out')
BIG_NAMES = MIXER_IN + MIXER_OUT + ('mlp_w_up', 'mlp_w_down')
SMALL = {'norm1': False, 'norm2': False, 'final_norm': False, 'ret_gn_gain': True, 'gdn_conv_w': True,
         'gdn_a_log': False, 'gdn_dt_bias': False, 'gdn_norm_gain': False, 'gla_w_gate_up': True,
         'gla_gate_bias': True, 'gla_norm_gain': True, 'lru_conv_w': True, 'lru_conv_b': True,
         'lru_w_rgate': False, 'lru_b_rgate': False, 'lru_w_igate': False, 'lru_b_igate': False, 'lru_lambda': True}
SMALL_NAMES = tuple(n for n in WEIGHT_NAMES if n in SMALL)
GDN_TAIL = 2 * GDN_HEADS


PACK_ROWS = 256


def _pack(arrs):
    rows = []
    for a in arrs:
        f = a.reshape(-1).astype(F32)
        rows.append(jnp.pad(f, (0, (-f.shape[0]) % LANE)).reshape(-1, LANE))
    buf = jnp.concatenate(rows, axis=0)
    return jnp.pad(buf, ((0, (-buf.shape[0]) % PACK_ROWS), (0, 0)))


def _unpack(buf, shapes, lead=()):
    out, r0 = [], 0
    for s in shapes:
        n = int(np.prod(s))
        nr = -(-n // LANE)
        blk = buf[..., r0:r0 + nr, :].reshape(lead + (nr * LANE,))[..., :n]
        out.append(blk.reshape(lead + tuple(s)))
        r0 += nr
    return out


def _full_cols(g):
    return jnp.transpose(g, (1, 0, 2)).reshape(g.shape[1], N_DEV * g.shape[2])


def _full_rows(g):
    return g.reshape(N_DEV * g.shape[1], g.shape[2])


def _blocks_cols(dw):
    r, c = dw.shape[0], dw.shape[1] // N_DEV
    return jnp.transpose(dw.reshape(r, N_DEV, c), (1, 0, 2))


def _blocks_rows(dw):
    return dw.reshape(N_DEV, dw.shape[0] // N_DEV, dw.shape[1])


def _pad_cols(a, n=LANE):
    return jnp.pad(a, ((0, 0), (0, n - a.shape[1])))


def _mixer_fwd(layer, hn, w_in, sm, tables, deps=()):
    tag = f"l{layer}"
    if layer == 0:
        proj = _matmul(hn, w_in, name=tag + "_in", deps=deps)
        gain = sm['ret_gn_gain'][0][:, None, :]
        og, st = _ret_fwd(proj, gain, tables, name=tag + "_ret_fwd")
        return og, dict(proj=proj, st=st, gain=gain)
    if layer == 1:
        w_main, w_tail = w_in[:, :4 * D_MODEL], _pad_cols(w_in[:, 4 * D_MODEL:])
        proj = _matmul(hn, w_main, name=tag + "_in", deps=deps)
        gates = _matmul(hn, w_tail, name=tag + "_in_tail")
        conv_w = sm['gdn_conv_w'][0]
        act = _conv_silu_fwd(proj, conv_w, name=tag + "_conv")
        a_log = sm['gdn_a_log'].reshape(GDN_HEADS, 1, 1)
        dt_bias = sm['gdn_dt_bias'].reshape(GDN_HEADS, 1, 1)
        gain = sm['gdn_norm_gain']
        og, st = _gdn_fwd(act, proj, gates, a_log, dt_bias, gain, name=tag + "_gdn_fwd")
        return og, dict(proj=proj, gates=gates, act=act, st=st, conv_w=conv_w, a_log=a_log, dt_bias=dt_bias,
                        gain=gain, w_main=w_main, w_tail=w_tail)
    if layer == 2:
        w_main, w_tail = w_in[:, :3 * D_MODEL], _pad_cols(w_in[:, 3 * D_MODEL:])
        proj = _matmul(hn, w_main, name=tag + "_in", deps=deps)
        glow = _matmul(hn, w_tail, name=tag + "_in_tail")
        wgu = jnp.pad(sm['gla_w_gate_up'][0], ((0, LANE - GLA_GATE_RANK), (0, 0)))
        glogit = _matmul(glow, wgu, name=tag + "_gate_up")
        bias = sm['gla_gate_bias']
        gain = sm['gla_norm_gain'][0][:, None, :]
        og, st = _gla_fwd(proj, glogit, bias, gain, name=tag + "_gla_fwd")
        return og, dict(proj=proj, glow=glow, glogit=glogit, wgu=wgu, bias=bias, gain=gain, st=st,
                        w_main=w_main, w_tail=w_tail)
    proj = _matmul(hn, w_in, name=tag + "_in", deps=deps)
    args = (proj, sm['lru_conv_w'][0], sm['lru_conv_b'], sm['lru_w_rgate'][0], sm['lru_b_rgate'][0][:, None, :],
            sm['lru_w_igate'][0], sm['lru_b_igate'][0][:, None, :], sm['lru_lambda'])
    og, hs = _lru_fwd(*args, name=tag + "_lru_fwd")
    return og, dict(args=args, hs=hs)


def _mixer_bwd(layer, hn, w_in, dog, sv, tables, on_dw):
    tag = f"l{layer}"
    if layer == 0:
        dq, dk, dv, dg, dgain = _ret_bwd(sv['proj'], sv['gain'], tables, sv['st'], dog, name=tag + "_ret_bwd")
        dproj = jnp.concatenate([dq, dk, dv, dg], axis=1)
        deps = on_dw(_matmul(hn, dproj, ta=True, out_dtype=BF16, name=tag + "_in_dw"))
        dhn = _matmul(dproj, w_in, tb=True, name=tag + "_in_dx", deps=deps)
        return dhn, {'ret_gn_gain': dgain[:, 0][None]}
    if layer == 1:
        dq, dk, dv, dz, dgates, dscal, dgain = _gdn_bwd(
            sv['act'], sv['proj'], sv['gates'], sv['a_log'], sv['dt_bias'], sv['gain'], sv['st'], dog,
            name=tag + "_gdn_bwd")
        dact = jnp.concatenate([dq, dk, dv], axis=1)
        dqkv, dconv = _conv_silu_bwd(sv['proj'], sv['conv_w'], dact, name=tag + "_conv_bwd")
        dmain = jnp.concatenate([dqkv, dz], axis=1)
        T = dmain.shape[0]
        dtail = _pad_cols(jnp.transpose(dgates[:, :, :2], (1, 2, 0)).reshape(T, GDN_TAIL))
        dw_main = _matmul(hn, dmain, ta=True, out_dtype=BF16, name=tag + "_in_dw")
        dw_tail = _matmul(hn, dtail, ta=True, out_dtype=BF16, name=tag + "_in_tail_dw")
        deps = on_dw(jnp.concatenate([dw_main, dw_tail[:, :GDN_TAIL]], axis=1))
        dhn = _matmul(dmain, sv['w_main'], tb=True, name=tag + "_in_dx", deps=deps)
        dhn = _matmul(dtail, sv['w_tail'], tb=True, epi="add", extra=dhn, name=tag + "_in_tail_dx")
        small = {'gdn_conv_w': dconv[None], 'gdn_a_log': dscal[:, 0, 0][None], 'gdn_dt_bias': dscal[:, 0, 1][None],
                 'gdn_norm_gain': jnp.sum(dgain[:, 0], axis=0)[None]}
        return dhn, small
    if layer == 2:
        dq, dk, dv, dr, dgl, dgain = _gla_bwd(sv['proj'], sv['glogit'], sv['bias'], sv['gain'], sv['st'], dog,
                                              name=tag + "_gla_bwd")
        dmain = jnp.concatenate([dq, dk, dv, dr], axis=1)
        dglow = _matmul(dgl, sv['wgu'], tb=True, name=tag + "_gate_up_dx")
        dwgu = _matmul(sv['glow'], dgl, ta=True, name=tag + "_gate_up_dw")
        dbias = _colsum(dgl, name=tag + "_gate_bias")
        dw_main = _matmul(hn, dmain, ta=True, out_dtype=BF16, name=tag + "_in_dw")
        dw_tail = _matmul(hn, dglow, ta=True, out_dtype=BF16, name=tag + "_in_tail_dw")
        deps = on_dw(jnp.concatenate([dw_main, dw_tail[:, :GLA_GATE_RANK]], axis=1))
        dhn = _matmul(dmain, sv['w_main'], tb=True, name=tag + "_in_dx", deps=deps)
        dhn = _matmul(dglow, sv['w_tail'], tb=True, epi="add", extra=dhn, name=tag + "_in_tail_dx")
        small = {'gla_w_gate_up': dwgu[:GLA_GATE_RANK][None], 'gla_gate_bias': dbias,
                 'gla_norm_gain': dgain[:, 0][None]}
        return dhn, small
    dxb, dyb, dcw, dcb, dwr, dbr, dwi, dbi, dlam = _lru_bwd(*sv['args'], sv['hs'], dog, name=tag + "_lru_bwd")
    dproj = jnp.concatenate([dxb, dyb], axis=1)
    deps = on_dw(_matmul(hn, dproj, ta=True, out_dtype=BF16, name=tag + "_in_dw"))
    dhn = _matmul(dproj, w_in, tb=True, name=tag + "_in_dx", deps=deps)
    small = {'lru_conv_w': dcw[None], 'lru_conv_b': dcb, 'lru_w_rgate': dwr[None], 'lru_b_rgate': dbr[:, 0][None],
             'lru_w_igate': dwi[None], 'lru_b_igate': dbi[:, 0][None], 'lru_lambda': dlam}
    return dhn, small


def _step(*args):
    assert len(args) == len(ARG_NAMES)
    p = dict(zip(ARG_NAMES, args))
    xi, yi, ci = _mesh_pos()
    dev = 4 * xi + 2 * yi + ci
    core = ci.astype(jnp.int32).reshape(1)
    chip = (2 * xi + yi).astype(jnp.int32).reshape(1)
    x = p['x'][0]
    target = p['loss_target'][0]
    T = x.shape[0]
    tables = _ret_tables(T)

    sharded_small = [n for n in SMALL_NAMES if SMALL[n]]
    gathered, = _all_gather([_pack([p[n] for n in sharded_small])], name="gather_small")
    def in_cols_of(w):
        return w.shape[1] % LANE == 0

    gathers = {}
    token = gathered
    for layer in range(DEPTH):
        w_in_shard = p[MIXER_IN[layer]][0]
        groups = {'a': [(w_in_shard, in_cols_of(w_in_shard), _full_cols)],
                  'b': [(p[MIXER_OUT[layer]][0], False, _full_rows), (p['mlp_w_up'][layer], True, None),
                        (p['mlp_w_down'][layer], False, _full_rows)]}
        for key, members in groups.items():
            shards = [w.astype(BF16) for w, _, _ in members]
            in_cols = [ic for _, ic, _ in members]
            n = len(shards)
            lands = [lax.empty((s.shape[0], N_DEV * s.shape[1]) if ic else (N_DEV,) + s.shape, BF16)
                     for s, ic in zip(shards, in_cols)]
            send, recv, bufs, token = _split_start(shards + lands, _plan_gather_a(in_cols), 4 * n,
                                                   name=f"gather_a_start_l{layer}{key}", deps=(token,))
            gathers[layer, key] = dict(n=n, h=(send, recv, bufs), in_cols=in_cols, full_of=[f for _, _, f in members])

    def gather_forward(layer, key, after):
        g = gathers[layer, key]
        n = g['n']
        send, recv, bufs = g['h']
        bufs = _split_wait(send, recv, bufs, _plan_gather_a(g['in_cols']), after, name=f"gather_a_wait_l{layer}{key}")
        send, recv, lands, tok = _split_start(bufs[n:], _plan_gather_b(g['in_cols']), 3 * n,
                                              name=f"gather_b_start_l{layer}{key}")
        g['h'], g['shards'] = (send, recv, lands), bufs[:n]
        return tok

    def gather_finish(layer, key, after):
        g = gathers[layer, key]
        send, recv, lands = g['h']
        lands = _split_wait(send, recv, lands, _plan_gather_b(g['in_cols']), after, name=f"gather_b_wait_l{layer}{key}")
        full = []
        for l, s, ic, full_of in zip(lands, g['shards'], g['in_cols'], g['full_of']):
            if ic:
                full.append(lax.dynamic_update_slice(l, s, (0, dev * s.shape[1])))
            else:
                full.append(full_of(lax.dynamic_update_slice(l, s[None], (dev, 0, 0))))
        return full

    w_in_next, = gather_finish(0, 'a', gather_forward(0, 'a', token))
    parts = _unpack(gathered, [p[n].shape for n in sharded_small], lead=(N_DEV,))
    sm = {n: p[n] for n in SMALL_NAMES if not SMALL[n]}
    for n, blk in zip(sharded_small, parts):
        full = jnp.moveaxis(blk, 0, -2)
        sm[n] = full.reshape(full.shape[:-2] + (N_DEV * full.shape[-1],))

    saved = []
    big = {}
    for layer in range(DEPTH):
        w_in = w_in_next
        tag = f"l{layer}"
        hn = _rmsnorm_fwd(x, sm['norm1'][layer][None], name=tag + "_norm1")
        if layer == 0:
            og, sv = _mixer_fwd(layer, hn, w_in, sm, tables)
            w_out, w_up, w_down = gather_finish(layer, 'b', gather_forward(layer, 'b', og))
        else:
            og, sv = _mixer_fwd(layer, hn, w_in, sm, tables, deps=(gather_forward(layer, 'b', hn),))
            w_out, w_up, w_down = gather_finish(layer, 'b', og)
        big[layer] = (w_in, w_out, w_up, w_down)
        deps = (gather_forward(layer + 1, 'a', og),) if layer + 1 < DEPTH else ()
        x_mid = _matmul(og, w_out, epi="add", extra=x, name=tag + "_out", deps=deps)
        hn2 = _rmsnorm_fwd(x_mid, sm['norm2'][layer][None], name=tag + "_norm2")
        u, a = _matmul(hn2, w_up, epi="relu2", name=tag + "_up")
        x_new = _matmul(a, w_down, epi="add", extra=x_mid, name=tag + "_down")
        if layer + 1 < DEPTH:
            w_in_next, = gather_finish(layer + 1, 'a', x_new)
        saved.append(dict(x=x, hn=hn, og=og, sv=sv, x_mid=x_mid, hn2=hn2, u=u, a=a))
        x = x_new
    dy, loss_part = _final_loss(x, sm['final_norm'][None], target, name="final_loss")
    loss = lax.psum(loss_part[0, 0], ("x", "y", "c"))
    dx, dxb, dfinal = _rmsnorm_bwd(x, sm['final_norm'][None], dy, jnp.zeros_like(x), name="final_norm_bwd")

    outs = {}
    small_grads = {'final_norm': dfinal[0]}
    dnorm1, dnorm2 = [None] * DEPTH, [None] * DEPTH

    def scatter_start(items, deps, tag):
        grads, in_cols = [], []
        for _, _, dw, on in items:
            ic = on == 'cols' and (dw.shape[1] // N_DEV) % LANE == 0
            in_cols.append(ic)
            grads.append(dw if ic else (_blocks_cols(dw) if on == 'cols' else _blocks_rows(dw)))
        n = len(grads)
        lands = [lax.empty((N_CHIPS, dw.shape[0], dw.shape[1] // N_DEV) if ic else (N_CHIPS,) + dw.shape[1:], BF16)
                 for dw, ic in zip(grads, in_cols)]
        send, recv, bufs, tok = _split_start(grads + lands, _plan_scatter_pair(in_cols), N_CHIPS * n,
                                             name=f"scatter_pair_start_{tag}", deps=deps)
        return dict(items=items, n=n, tag=tag, in_cols=in_cols, h=(send, recv, bufs)), tok

    def scatter_forward(g, after):
        n, tag = g['n'], g['tag']
        send, recv, bufs = g['h']
        bufs = _split_wait(send, recv, bufs, _plan_scatter_pair(g['in_cols']), after, name=f"scatter_pair_wait_{tag}")
        sums = [_pair_add(b, a_, core, ic, name=f"pair_add_{tag}_{i}")
                for i, (b, a_, ic) in enumerate(zip(bufs[:n], bufs[n:], g['in_cols']))]
        lands = [lax.empty((N_CHIPS - 1,) + s_.shape[1:], BF16) for s_ in sums]
        send, recv, bufs, tok = _split_start(sums + lands, _plan_scatter_chips(n), (N_CHIPS - 1) * n,
                                             name=f"scatter_chips_start_{tag}")
        g['h'] = (send, recv, bufs)
        return tok

    def scatter_finish(g, after):
        n, tag = g['n'], g['tag']
        send, recv, bufs = g['h']
        bufs = _split_wait(send, recv, bufs, _plan_scatter_chips(n), after, name=f"scatter_chips_wait_{tag}")
        for i, (wname, idx, _, _) in enumerate(g['items']):
            outs[wname] = _adamw_sharded(p[wname], p['m_' + wname], p['v_' + wname], bufs[i], bufs[n + i], chip,
                                         idx, outs.get(wname), name=f"adamw_{tag}_{i}")

    older = []
    for layer in reversed(range(DEPTH)):
        w_in, w_out, w_up, w_down = big[layer]
        s = saved[layer]
        tag = f"l{layer}"
        du = _matmul(dxb, w_down, tb=True, epi="drelu2", extra=s['u'], out_dtype=BF16, name=tag + "_down_dx")
        dw_down = _matmul(s['a'], dxb, ta=True, out_dtype=BF16, name=tag + "_down_dw")
        dhn2 = _matmul(du, w_up, tb=True, name=tag + "_up_dx")
        dw_up = _matmul(s['hn2'], du, ta=True, out_dtype=BF16, name=tag + "_up_dw")
        dx, dxb, dn2 = _rmsnorm_bwd(s['x_mid'], sm['norm2'][layer][None], dhn2, dx, name=tag + "_norm2_bwd")
        mlp_group, tok = scatter_start([('mlp_w_up', layer, dw_up, 'cols'),
                                        ('mlp_w_down', layer, dw_down, 'rows')], (), f"mlp_l{layer}")
        dw_out = _matmul(s['og'], dxb, ta=True, out_dtype=BF16, name=tag + "_out_dw", deps=(tok,))
        tok = scatter_forward(mlp_group, dw_out)
        dog = _matmul(dxb, w_out, tb=True, name=tag + "_out_dx", deps=(tok,))
        started = []

        def on_dw(dw_in):
            group, tok_ = scatter_start([(MIXER_IN[layer], 0, dw_in, 'cols'),
                                         (MIXER_OUT[layer], 0, dw_out, 'rows')], (), f"mix_l{layer}")
            started.append(group)
            return (tok_,)

        dhn, sg = _mixer_bwd(layer, s['hn'], w_in, dog, s['sv'], tables, on_dw)
        mixer_group, = started
        small_grads.update(sg)
        if layer > 0:
            tok = scatter_forward(mixer_group, dhn)
            dx, dxb, dn1 = _rmsnorm_bwd(s['x'], sm['norm1'][layer][None], dhn, dx, name=tag + "_norm1_bwd", deps=(tok,))
        else:
            dx, dxb, dn1 = _rmsnorm_bwd(s['x'], sm['norm1'][layer][None], dhn, dx, name=tag + "_norm1_bwd")
        dnorm1[layer], dnorm2[layer] = dn1[0], dn2[0]
        for g in older:
            scatter_finish(g, dx)
        older = [mlp_group, mixer_group]
    small_grads['norm1'] = jnp.stack(dnorm1)
    small_grads['norm2'] = jnp.stack(dnorm2)

    full_shapes = [small_grads[n].shape for n in SMALL_NAMES]
    all_parts, = _all_gather([_pack([small_grads[n] for n in SMALL_NAMES])], name="gather_small_grads")
    last_token = scatter_forward(mixer_group, all_parts)
    summed_buf = _sum_parts(all_parts, name="sum_small_grads", deps=(last_token,))
    for g in older:
        scatter_finish(g, summed_buf)
    summed = _unpack(summed_buf, full_shapes)
    local_g = []
    for n, g in zip(SMALL_NAMES, summed):
        if SMALL[n]:
            width = p[n].shape[-1]
            g = lax.dynamic_slice_in_dim(g, dev * width, width, axis=g.ndim - 1)
        local_g.append(g.reshape(p[n].shape))
    res = _adamw(_pack([p[n] for n in SMALL_NAMES]), _pack([p['m_' + n] for n in SMALL_NAMES]),
                 _pack([p['v_' + n] for n in SMALL_NAMES]), [_pack(local_g)], name="adamw_small")
    local_shapes = [p[n].shape for n in SMALL_NAMES]
    unpacked = [_unpack(r, local_shapes) for r in res]
    for i, n in enumerate(SMALL_NAMES):
        outs[n] = tuple(unpacked[k][i] for k in range(4))

    result = [loss, dx[None]]
    for k in range(4):
        result += [outs[n][k] for n in WEIGHT_NAMES]
    return tuple(result)


def kernel(x, norm1, norm2, final_norm, ret_w_in, ret_gn_gain, ret_w_out, gdn_w_in, gdn_conv_w, gdn_a_log, gdn_dt_bias, gdn_norm_gain, gdn_w_out, gla_w_in, gla_w_gate_up, gla_gate_bias, gla_norm_gain, gla_w_out, lru_w_in, lru_conv_w, lru_conv_b, lru_w_rgate, lru_b_rgate, lru_w_igate, lru_b_igate, lru_lambda, lru_w_out, mlp_w_up, mlp_w_down, loss_target, m_norm1, m_norm2, m_final_norm, m_ret_w_in, m_ret_gn_gain, m_ret_w_out, m_gdn_w_in, m_gdn_conv_w, m_gdn_a_log, m_gdn_dt_bias, m_gdn_norm_gain, m_gdn_w_out, m_gla_w_in, m_gla_w_gate_up, m_gla_gate_bias, m_gla_norm_gain, m_gla_w_out, m_lru_w_in, m_lru_conv_w, m_lru_conv_b, m_lru_w_rgate, m_lru_b_rgate, m_lru_w_igate, m_lru_b_igate, m_lru_lambda, m_lru_w_out, m_mlp_w_up, m_mlp_w_down, v_norm1, v_norm2, v_final_norm, v_ret_w_in, v_ret_gn_gain, v_ret_w_out, v_gdn_w_in, v_gdn_conv_w, v_gdn_a_log, v_gdn_dt_bias, v_gdn_norm_gain, v_gdn_w_out, v_gla_w_in, v_gla_w_gate_up, v_gla_gate_bias, v_gla_norm_gain, v_gla_w_out, v_lru_w_in, v_lru_conv_w, v_lru_conv_b, v_lru_w_rgate, v_lru_b_rgate, v_lru_w_igate, v_lru_b_igate, v_lru_lambda, v_lru_w_out, v_mlp_w_up, v_mlp_w_down):
    return _step(x, norm1, norm2, final_norm, ret_w_in, ret_gn_gain, ret_w_out, gdn_w_in, gdn_conv_w, gdn_a_log, gdn_dt_bias, gdn_norm_gain, gdn_w_out, gla_w_in, gla_w_gate_up, gla_gate_bias, gla_norm_gain, gla_w_out, lru_w_in, lru_conv_w, lru_conv_b, lru_w_rgate, lru_b_rgate, lru_w_igate, lru_b_igate, lru_lambda, lru_w_out, mlp_w_up, mlp_w_down, loss_target, m_norm1, m_norm2, m_final_norm, m_ret_w_in, m_ret_gn_gain, m_ret_w_out, m_gdn_w_in, m_gdn_conv_w, m_gdn_a_log, m_gdn_dt_bias, m_gdn_norm_gain, m_gdn_w_out, m_gla_w_in, m_gla_w_gate_up, m_gla_gate_bias, m_gla_norm_gain, m_gla_w_out, m_lru_w_in, m_lru_conv_w, m_lru_conv_b, m_lru_w_rgate, m_lru_b_rgate, m_lru_w_igate, m_lru_b_igate, m_lru_lambda, m_lru_w_out, m_mlp_w_up, m_mlp_w_down, v_norm1, v_norm2, v_final_norm, v_ret_w_in, v_ret_gn_gain, v_ret_w_out, v_gdn_w_in, v_gdn_conv_w, v_gdn_a_log, v_gdn_dt_bias, v_gdn_norm_gain, v_gdn_w_out, v_gla_w_in, v_gla_w_gate_up, v_gla_gate_bias, v_gla_norm_gain, v_gla_w_out, v_lru_w_in, v_lru_conv_w, v_lru_conv_b, v_lru_w_rgate, v_lru_b_rgate, v_lru_w_igate, v_lru_b_igate, v_lru_lambda, v_lru_w_out, v_mlp_w_up, v_mlp_w_down)
```

```python
import functools
import math

import numpy as np
import jax
import jax.numpy as jnp
from jax import lax
from jax.experimental import pallas as pl
from jax.experimental.pallas import tpu as pltpu

F32 = jnp.float32
BF16 = jnp.bfloat16

D_MODEL = 2048
DEPTH = 4
CHUNK = 64
D_FF = 4 * D_MODEL
NORM_EPS = 1e-6
N_DEV = 8

RET_HEADS, RET_DK, RET_DV = 8, 256, 512
RET_HB = 4
GDN_HEADS, GDN_DK, GDN_DV = 16, 128, 128
GDN_HB = 8
GDN_QKV = GDN_HEADS * (2 * GDN_DK + GDN_DV)
CONV_WIDTH = 4
GLA_HEADS, GLA_DK, GLA_DV = 4, 256, 512
GLA_HB = 4
GLA_GATE_RANK = 16
GLA_TAU = 16.0
LRU_WIDTH, LRU_BLOCKS, LRU_BLOCK = 2048, 16, 128
LRU_C = 8.0
ROPE_BASE = 10000.0

ADAM_LR, ADAM_B1, ADAM_B2, ADAM_EPS, ADAM_WD, ADAM_STEP = 0.001, 0.9, 0.999, 1e-08, 0.01, 10

LANE = 128
VMEM_LIMIT = 48 * 1024 * 1024

NN = (((1,), (0,)), ((), ()))
NT = (((1,), (1,)), ((), ()))
TN = (((0,), (0,)), ((), ()))


def _params(sem=None):
    return pltpu.CompilerParams(dimension_semantics=sem, vmem_limit_bytes=VMEM_LIMIT)


def _bdot(a, b, dn=NN):
    return lax.dot_general(a.astype(BF16), b.astype(BF16), dn, preferred_element_type=F32)


def _split(x):
    hi = x.astype(BF16)
    lo = (x - hi.astype(F32)).astype(BF16)
    return hi, lo


def _fdot(a, b, dn=NN):
    a1, a2 = _split(a)
    b1, b2 = _split(b)
    d = functools.partial(lax.dot_general, dimension_numbers=dn, preferred_element_type=F32)
    return d(a1, b1) + (d(a1, b2) + d(a2, b1))


def _sigmoid(x):
    return 1.0 / (1.0 + jnp.exp(-x))


def _softplus(x):
    return jnp.maximum(x, 0.0) + jnp.log(1.0 + jnp.exp(-jnp.abs(x)))


def _silu(x):
    return x * _sigmoid(x)


def _dsilu(x):
    s = _sigmoid(x)
    return s * (1.0 + x * (1.0 - s))


GELU_C = math.sqrt(2.0 / math.pi)


def _gelu(x):
    return 0.5 * x * (1.0 + jnp.tanh(GELU_C * (x + 0.044715 * x * x * x)))


def _dgelu(x):
    t = jnp.tanh(GELU_C * (x + 0.044715 * x * x * x))
    return 0.5 * (1.0 + t) + 0.5 * x * (1.0 - t * t) * GELU_C * (1.0 + 3.0 * 0.044715 * x * x)


def _expm1(x):
    poly = x * (1.0 + x * 0.5 * (1.0 + x * (1.0 / 3.0) * (1.0 + x * 0.25 * (1.0 + x * 0.2))))
    return jnp.where(jnp.abs(x) < 0.05, poly, jnp.exp(x) - 1.0)


def _iota2(shape, axis):
    return lax.broadcasted_iota(jnp.int32, shape, axis)


def _col_to_row(col):
    n = col.shape[0]
    eye = _iota2((n, n), 0) == _iota2((n, n), 1)
    return jnp.sum(jnp.where(eye, col, 0.0), axis=0, keepdims=True)


def _row_to_col(row):
    n = row.shape[1]
    eye = _iota2((n, n), 0) == _iota2((n, n), 1)
    return jnp.sum(jnp.where(eye, row, 0.0), axis=1, keepdims=True)


def _pick_row(x, r):
    rows = _iota2(x.shape, 0)
    return jnp.sum(jnp.where(rows == r, x, 0.0), axis=0, keepdims=True)


def _shift_down(x, s):
    if s == 0:
        return x
    y = pltpu.roll(x, s, 0)
    return jnp.where(_iota2(x.shape, 0) < s, 0.0, y)


def _shift_up(x, s):
    if s == 0:
        return x
    n = x.shape[0]
    y = pltpu.roll(x, n - s, 0)
    return jnp.where(_iota2(x.shape, 0) >= n - s, 0.0, y)


def _tile(dim, prefs):
    for p in prefs:
        if dim % p == 0:
            return p
    return dim


TOKEN_SHAPE = (8, LANE)


def _dep_specs(deps):
    return [pl.BlockSpec(TOKEN_SHAPE, lambda *_: (0, 0)) for _ in deps]


def _matmul(a, b, *, ta=False, tb=False, epi="none", extra=None, out_dtype=F32, name, deps=()):
    if ta:
        K, M = a.shape
    else:
        M, K = a.shape
    if tb:
        N, K2 = b.shape
    else:
        K2, N = b.shape
    assert K == K2, (a.shape, b.shape, ta, tb)
    if K <= 2048:
        tk = K
        if N <= 2048:
            tm, tn = _tile(M, (512, 256, 128)), N
        else:
            tm, tn = _tile(M, (1024, 512, 256, 128)), _tile(N, (512, 256, 128))
    else:
        tm, tn, tk = (_tile(d, (1024, 512, 256, 128)) for d in (M, N, K))
    nk = K // tk
    dn = (((0 if ta else 1,), (1 if tb else 0,)), ((), ()))
    n_extra = 0 if extra is None else 1
    n_out = 2 if epi == "relu2" else 1

    def body(*refs):
        a_ref, b_ref = refs[0], refs[1]
        e_ref = refs[2] if n_extra else None
        outs = refs[2 + n_extra + len(deps):2 + n_extra + len(deps) + n_out]

        def finish(r):
            if epi == "none":
                outs[0][...] = r.astype(outs[0].dtype)
            elif epi == "add":
                outs[0][...] = (r + e_ref[...]).astype(outs[0].dtype)
            elif epi == "relu2":
                outs[0][...] = r
                p = jnp.maximum(r, 0.0)
                outs[1][...] = (p * p).astype(outs[1].dtype)
            elif epi == "drelu2":
                outs[0][...] = (r * 2.0 * jnp.maximum(e_ref[...], 0.0)).astype(outs[0].dtype)

        def product():
            return lax.dot_general(a_ref[...].astype(BF16), b_ref[...].astype(BF16), dn, preferred_element_type=F32)

        if nk == 1:
            finish(product())
            return
        acc = refs[-1]
        k = pl.program_id(2)

        @pl.when(k == 0)
        def _():
            acc[...] = jnp.zeros_like(acc)

        acc[...] += product()

        @pl.when(k == nk - 1)
        def _():
            finish(acc[...])

    a_spec = pl.BlockSpec((tk, tm), lambda i, j, k: (k, i)) if ta else pl.BlockSpec((tm, tk), lambda i, j, k: (i, k))
    b_spec = pl.BlockSpec((tn, tk), lambda i, j, k: (j, k)) if tb else pl.BlockSpec((tk, tn), lambda i, j, k: (k, j))
    o_spec = pl.BlockSpec((tm, tn), lambda i, j, k: (i, j))
    in_specs = [a_spec, b_spec] + ([o_spec] if n_extra else []) + _dep_specs(deps)
    if epi == "relu2":
        out_shape = (jax.ShapeDtypeStruct((M, N), F32), jax.ShapeDtypeStruct((M, N), BF16))
        out_specs = (o_spec, o_spec)
    else:
        out_shape = jax.ShapeDtypeStruct((M, N), out_dtype)
        out_specs = o_spec
    args = (a, b) + ((extra,) if n_extra else ()) + tuple(deps)
    return pl.pallas_call(
        body, name=name, grid=(M // tm, N // tn, nk), in_specs=in_specs, out_specs=out_specs,
        out_shape=out_shape, scratch_shapes=[pltpu.VMEM((tm, tn), F32)] if nk > 1 else [],
        compiler_params=_params(("parallel", "parallel", "arbitrary")),
    )(*args)


ROW_BLOCK = 256


def _rmsnorm_fwd(x, g, *, name, deps=()):
    T, D = x.shape
    tr = _tile(T, (ROW_BLOCK, 128, 64))

    def body(x_ref, g_ref, *rest):
        o_ref = rest[-1]
        xv = x_ref[...]
        r = lax.rsqrt(jnp.mean(xv * xv, axis=-1, keepdims=True) + NORM_EPS)
        o_ref[...] = (xv * r * g_ref[...]).astype(o_ref.dtype)

    return pl.pallas_call(
        body, name=name, grid=(T // tr,),
        in_specs=[pl.BlockSpec((tr, D), lambda i: (i, 0)), pl.BlockSpec((1, D), lambda i: (0, 0))] + _dep_specs(deps),
        out_specs=pl.BlockSpec((tr, D), lambda i: (i, 0)),
        out_shape=jax.ShapeDtypeStruct((T, D), BF16), compiler_params=_params(("parallel",)),
    )(x, g, *deps)


def _rmsnorm_bwd(x, g, dy, dres, *, name, deps=()):
    T, D = x.shape
    tr = _tile(T, (ROW_BLOCK, 128, 64))

    def body(x_ref, g_ref, dy_ref, dres_ref, *rest):
        dx_ref, dxb_ref, dg_ref = rest[len(deps):]
        i = pl.program_id(0)
        xv = x_ref[...]
        r = lax.rsqrt(jnp.mean(xv * xv, axis=-1, keepdims=True) + NORM_EPS)
        xh = xv * r
        dyv = dy_ref[...].astype(F32)
        dxh = dyv * g_ref[...]
        dx = dres_ref[...] + r * (dxh - xh * jnp.mean(dxh * xh, axis=-1, keepdims=True))
        dx_ref[...] = dx
        dxb_ref[...] = dx.astype(dxb_ref.dtype)

        @pl.when(i == 0)
        def _():
            dg_ref[...] = jnp.zeros_like(dg_ref)

        dg_ref[...] += jnp.sum(dyv * xh, axis=0, keepdims=True)

    blk = pl.BlockSpec((tr, D), lambda i: (i, 0))
    vec = pl.BlockSpec((1, D), lambda i: (0, 0))
    return pl.pallas_call(
        body, name=name, grid=(T // tr,), in_specs=[blk, vec, blk, blk] + _dep_specs(deps), out_specs=(blk, blk, vec),
        out_shape=(jax.ShapeDtypeStruct((T, D), F32), jax.ShapeDtypeStruct((T, D), BF16),
                   jax.ShapeDtypeStruct((1, D), F32)),
        compiler_params=_params(("arbitrary",)),
    )(x, g, dy, dres, *deps)


def _final_loss(x, g, target, *, name):
    T, D = x.shape
    tr = _tile(T, (ROW_BLOCK, 128, 64))

    def body(x_ref, g_ref, t_ref, dy_ref, l_ref):
        i = pl.program_id(0)
        xv = x_ref[...]
        r = lax.rsqrt(jnp.mean(xv * xv, axis=-1, keepdims=True) + NORM_EPS)
        err = xv * r * g_ref[...] - t_ref[...]
        dy_ref[...] = err * (1.0 / D)

        @pl.when(i == 0)
        def _():
            l_ref[...] = jnp.zeros_like(l_ref)

        part = 0.5 * jnp.sum(jnp.mean(err * err, axis=-1, keepdims=True), axis=0, keepdims=True)
        l_ref[...] += jnp.broadcast_to(part, l_ref.shape)

    blk = pl.BlockSpec((tr, D), lambda i: (i, 0))
    vec = pl.BlockSpec((1, D), lambda i: (0, 0))
    return pl.pallas_call(
        body, name=name, grid=(T // tr,), in_specs=[blk, vec, blk],
        out_specs=(blk, pl.BlockSpec((1, LANE), lambda i: (0, 0))),
        out_shape=(jax.ShapeDtypeStruct((T, D), F32), jax.ShapeDtypeStruct((1, LANE), F32)),
        compiler_params=_params(("arbitrary",)),
    )(x, g, target)


def _colsum(x, *, name):
    T, C = x.shape
    tc = _tile(C, (512, 256, 128))

    def body(x_ref, o_ref):
        o_ref[...] = jnp.sum(x_ref[...], axis=0, keepdims=True)

    return pl.pallas_call(
        body, name=name, grid=(C // tc,), in_specs=[pl.BlockSpec((T, tc), lambda j: (0, j))],
        out_specs=pl.BlockSpec((1, tc), lambda j: (0, j)),
        out_shape=jax.ShapeDtypeStruct((1, C), F32), compiler_params=_params(("parallel",)),
    )(x)


ADAM_C1 = 1.0 - ADAM_B1 ** ADAM_STEP
ADAM_C2 = 1.0 - ADAM_B2 ** ADAM_STEP


def _adamw(w, m, v, grads, *, name):
    R, C = w.shape
    tr = _tile(R, (256, 128, 64, 32, 16, 8))
    n_g = len(grads)

    def body(*refs):
        w_ref, m_ref, v_ref = refs[:3]
        g_refs = refs[3:3 + n_g]
        g_out, d_out, m_out, v_out = refs[3 + n_g:]
        g = g_refs[0][...].astype(F32)
        for r in g_refs[1:]:
            g = g + r[...].astype(F32)
        mn = ADAM_B1 * m_ref[...] + (1.0 - ADAM_B1) * g
        vn = ADAM_B2 * v_ref[...] + (1.0 - ADAM_B2) * (g * g)
        m_hat = mn / ADAM_C1
        v_hat = vn / ADAM_C2
        g_out[...] = g
        d_out[...] = -ADAM_LR * (m_hat / (jnp.sqrt(v_hat) + ADAM_EPS) + ADAM_WD * w_ref[...])
        m_out[...] = mn
        v_out[...] = vn

    blk = pl.BlockSpec((tr, C), lambda i: (i, 0))
    sh = jax.ShapeDtypeStruct((R, C), F32)
    return pl.pallas_call(
        body, name=name, grid=(R // tr,), in_specs=[blk] * (3 + n_g), out_specs=(blk,) * 4,
        out_shape=(sh,) * 4, compiler_params=_params(("parallel",)),
    )(w, m, v, *grads)


def _sum_parts(parts, *, name, deps=()):
    P, R, C = parts.shape
    tr = _tile(R, (256, 128, 64, 32, 16, 8))

    def body(p_ref, *rest):
        o_ref = rest[-1]
        s = p_ref[0].astype(F32)
        for i in range(1, P):
            s = s + p_ref[i].astype(F32)
        o_ref[...] = s

    return pl.pallas_call(
        body, name=name, grid=(R // tr,),
        in_specs=[pl.BlockSpec((P, tr, C), lambda i: (0, i, 0))] + _dep_specs(deps),
        out_specs=pl.BlockSpec((tr, C), lambda i: (i, 0)),
        out_shape=jax.ShapeDtypeStruct((R, C), F32), compiler_params=_params(("parallel",)),
    )(parts, *deps)


def _ret_tables(T):
    H, C = RET_HEADS, CHUNK
    log_gamma = jnp.log1p(-jnp.exp2(-5.0 - jnp.arange(H, dtype=F32)))
    pos = jnp.arange(C, dtype=F32)
    dist = jnp.abs(pos[:, None] - pos[None, :])
    dm = jnp.exp(log_gamma[:, None, None] * dist)
    qdec = jnp.exp(log_gamma[:, None] * (pos + 1.0))[:, :, None]
    kdec = jnp.exp(log_gamma[:, None] * (C - 1.0 - pos))[:, :, None]
    cdec = jnp.exp(log_gamma * C)[:, None, None]
    inv = ROPE_BASE ** (-jnp.arange(0, RET_DK, 2, dtype=F32) / RET_DK)
    ang = jnp.arange(T, dtype=F32)[:, None] * inv[None, :]
    return dm, qdec, kdec, cdec, jnp.cos(ang), jnp.sin(ang)


def _rot(x, cos, sin):
    h = x.shape[1] // 2
    x1, x2 = x[:, :h], x[:, h:]
    return jnp.concatenate([x1 * cos - x2 * sin, x1 * sin + x2 * cos], axis=1)


def _unrot(dy, cos, sin):
    h = dy.shape[1] // 2
    d1, d2 = dy[:, :h], dy[:, h:]
    return jnp.concatenate([d1 * cos + d2 * sin, d2 * cos - d1 * sin], axis=1)


def _ret_specs(N, rev):
    H, C, DK, DV = RET_HEADS, CHUNK, RET_DK, RET_DV
    cn = (lambda n: N - 1 - n) if rev else (lambda n: n)
    HB, G = RET_HB, H // RET_HB
    q = pl.BlockSpec((C, HB * DK), lambda h, n: (cn(n), h))
    k = pl.BlockSpec((C, HB * DK), lambda h, n: (cn(n), G + h))
    v = pl.BlockSpec((C, HB * DV), lambda h, n: (cn(n), G + h))
    g = pl.BlockSpec((C, HB * DV), lambda h, n: (cn(n), 2 * G + h))
    cs = pl.BlockSpec((C, DK // 2), lambda h, n: (cn(n), 0))
    dm = pl.BlockSpec((HB, C, C), lambda h, n: (h, 0, 0))
    dec = pl.BlockSpec((HB, C, 1), lambda h, n: (h, 0, 0))
    cd = pl.BlockSpec((HB, 1, 1), lambda h, n: (h, 0, 0))
    gain = pl.BlockSpec((HB, 1, DV), lambda h, n: (h, 0, 0))
    st = pl.BlockSpec((HB, 1, DK, DV), lambda h, n: (h, cn(n), 0, 0))
    ov = pl.BlockSpec((C, HB * DV), lambda h, n: (cn(n), h))
    return q, k, v, g, cs, dm, dec, cd, gain, st, ov


def _ret_fwd(proj, gain, tables, *, name):
    T = proj.shape[0]
    H, C, DK, DV = RET_HEADS, CHUNK, RET_DK, RET_DV
    N = T // C
    dm_t, qdec_t, kdec_t, cdec_t, cos_t, sin_t = tables

    def body(q_ref, k_ref, v_ref, g_ref, cos_ref, sin_ref, dm_ref, qd_ref, kd_ref, cd_ref, gain_ref,
             og_ref, st_ref, s_acc):
        n = pl.program_id(1)

        @pl.when(n == 0)
        def _():
            s_acc[...] = jnp.zeros_like(s_acc)

        cos, sin = cos_ref[...], sin_ref[...]

        def head(hb):
            kc, vc = pl.ds(hb * DK, DK), pl.ds(hb * DV, DV)
            qr = _rot(q_ref[:, kc], cos, sin)
            kr = _rot(k_ref[:, kc], cos, sin) * (DK ** -0.5)
            v = v_ref[:, vc]
            sp = s_acc[hb]
            st_ref[hb, 0] = sp.astype(st_ref.dtype)
            scores = _bdot(qr, kr, NT) * dm_ref[hb]
            inter = _bdot(qr * qd_ref[hb], sp)
            s_acc[hb] = sp * cd_ref[hb] + _bdot(kr * kd_ref[hb], v, TN)
            yield
            o = _bdot(scores, v) + inter
            yield
            oc = o - jnp.mean(o, axis=-1, keepdims=True)
            rstd = lax.rsqrt(jnp.mean(oc * oc, axis=-1, keepdims=True) + NORM_EPS)
            og_ref[:, vc] = (oc * rstd * gain_ref[hb] * _silu(g_ref[:, vc])).astype(og_ref.dtype)

        _interleave([head(hb) for hb in range(RET_HB)])

    q, k, v, g, cs, dm, dec, cd, gn, st, ov = _ret_specs(N, False)
    return pl.pallas_call(
        body, name=name, grid=(H // RET_HB, N),
        in_specs=[q, k, v, g, cs, cs, dm, dec, dec, cd, gn], out_specs=(ov, st),
        out_shape=(jax.ShapeDtypeStruct((T, H * DV), BF16), jax.ShapeDtypeStruct((H, N, DK, DV), BF16)),
        scratch_shapes=[pltpu.VMEM((RET_HB, DK, DV), F32)],
        compiler_params=_params(("arbitrary", "arbitrary")),
    )(proj, proj, proj, proj, cos_t, sin_t, dm_t, qdec_t, kdec_t, cdec_t, gain)


def _ret_bwd(proj, gain, tables, states, dog, *, name):
    T = proj.shape[0]
    H, C, DK, DV = RET_HEADS, CHUNK, RET_DK, RET_DV
    N = T // C
    dm_t, qdec_t, kdec_t, cdec_t, cos_t, sin_t = tables

    def body(q_ref, k_ref, v_ref, g_ref, cos_ref, sin_ref, dm_ref, qd_ref, kd_ref, cd_ref, gain_ref,
             st_ref, dog_ref, dq_ref, dk_ref, dv_ref, dg_ref, dgain_ref, ds_acc):
        n = pl.program_id(1)

        @pl.when(n == 0)
        def _():
            ds_acc[...] = jnp.zeros_like(ds_acc)
            dgain_ref[...] = jnp.zeros_like(dgain_ref)

        cos, sin = cos_ref[...], sin_ref[...]
        scale = DK ** -0.5

        def head(hb):
            kc, vc = pl.ds(hb * DK, DK), pl.ds(hb * DV, DV)
            qr = _rot(q_ref[:, kc], cos, sin)
            kr = _rot(k_ref[:, kc], cos, sin) * scale
            v = v_ref[:, vc]
            g = g_ref[:, vc]
            sp = st_ref[hb, 0]
            dm = dm_ref[hb]
            qd, kd = qd_ref[hb], kd_ref[hb]
            gain_v = gain_ref[hb]
            scores = _bdot(qr, kr, NT) * dm
            inter = _bdot(qr * qd, sp)
            yield
            o = _bdot(scores, v) + inter
            yield
            oc = o - jnp.mean(o, axis=-1, keepdims=True)
            rstd = lax.rsqrt(jnp.mean(oc * oc, axis=-1, keepdims=True) + NORM_EPS)
            oh = oc * rstd
            dy = dog_ref[:, vc].astype(F32)
            dg_ref[:, vc] = (dy * oh * gain_v * _dsilu(g)).astype(dg_ref.dtype)
            dnorm = dy * _silu(g)
            dgain_ref[hb] += jnp.sum(dnorm * oh, axis=0, keepdims=True)
            doh = dnorm * gain_v
            do = rstd * (doh - jnp.mean(doh, axis=-1, keepdims=True)
                         - oh * jnp.mean(doh * oh, axis=-1, keepdims=True))
            dsn = ds_acc[hb]
            dp = _bdot(do, v, NT) * dm
            dq_inter = _bdot(do, sp, NT) * qd
            dk_inter = _bdot(v, dsn, NT) * kd
            dv_ref[:, vc] = (_bdot(scores, do, TN) + _bdot(kr * kd, dsn)).astype(dv_ref.dtype)
            ds_acc[hb] = dsn * cd_ref[hb] + _bdot(qr * qd, do, TN)
            yield
            dqr = _bdot(dp, kr) + dq_inter
            dkr = _bdot(dp, qr, TN) + dk_inter
            yield
            dq_ref[:, kc] = _unrot(dqr, cos, sin).astype(dq_ref.dtype)
            dk_ref[:, kc] = _unrot(dkr * scale, cos, sin).astype(dk_ref.dtype)

        _interleave([head(hb) for hb in range(RET_HB)])

    q, k, v, g, cs, dm, dec, cd, gn, st, ov = _ret_specs(N, True)
    return pl.pallas_call(
        body, name=name, grid=(H // RET_HB, N),
        in_specs=[q, k, v, g, cs, cs, dm, dec, dec, cd, gn, st, ov],
        out_specs=(q, q, ov, ov, gn),
        out_shape=(jax.ShapeDtypeStruct((T, H * DK), BF16), jax.ShapeDtypeStruct((T, H * DK), BF16),
                   jax.ShapeDtypeStruct((T, H * DV), BF16), jax.ShapeDtypeStruct((T, H * DV), BF16),
                   jax.ShapeDtypeStruct((H, 1, DV), F32)),
        scratch_shapes=[pltpu.VMEM((RET_HB, DK, DV), F32)],
        compiler_params=_params(("arbitrary", "arbitrary")),
    )(proj, proj, proj, proj, cos_t, sin_t, dm_t, qdec_t, kdec_t, cdec_t, gain, states, dog)


def _gla_specs(N, rev):
    H, C, DK, DV = GLA_HEADS, CHUNK, GLA_DK, GLA_DV
    cn = (lambda n: N - 1 - n) if rev else (lambda n: n)
    HB, G = GLA_HB, H // GLA_HB
    q = pl.BlockSpec((C, HB * DK), lambda h, n: (cn(n), h))
    k = pl.BlockSpec((C, HB * DK), lambda h, n: (cn(n), G + h))
    v = pl.BlockSpec((C, HB * DV), lambda h, n: (cn(n), G + h))
    r = pl.BlockSpec((C, HB * DV), lambda h, n: (cn(n), 2 * G + h))
    bias = pl.BlockSpec((1, HB * DK), lambda h, n: (0, h))
    gain = pl.BlockSpec((HB, 1, DV), lambda h, n: (h, 0, 0))
    st = pl.BlockSpec((HB, 1, DV, DK), lambda h, n: (h, cn(n), 0, 0))
    ov = pl.BlockSpec((C, HB * DV), lambda h, n: (cn(n), h))
    return q, k, v, r, bias, gain, st, ov


def _gla_chunk(q, k, v, gl_raw, bias):
    C, DK = q.shape
    gl = gl_raw + bias
    la = -_softplus(-gl) * (1.0 / GLA_TAU)
    lower = _iota2((C, C), 0) >= _iota2((C, C), 1)
    cum = _fdot(jnp.where(lower, 1.0, 0.0), la)
    yield
    ref = _pick_row(cum, C // 2 - 1)
    clast = _pick_row(cum, C - 1)
    fw, bw = jnp.exp(cum - ref), jnp.exp(ref - cum)
    qs = q * (DK ** -0.5)
    s_lo = _bdot(qs * fw, k * bw, NT)
    s_up = _bdot(qs * bw, k * fw, NT)
    yield
    scores = jnp.where(lower, s_lo, s_up)
    return gl, cum, clast, fw, bw, qs, k, v, scores, lower


def _gla_fwd(proj, glogit, bias, gain, *, name):
    T = proj.shape[0]
    H, C, DK, DV = GLA_HEADS, CHUNK, GLA_DK, GLA_DV
    N = T // C

    def body(q_ref, k_ref, v_ref, r_ref, gl_ref, bias_ref, gain_ref, og_ref, st_ref, s_acc):
        n = pl.program_id(1)

        @pl.when(n == 0)
        def _():
            s_acc[...] = jnp.zeros_like(s_acc)

        def head(hb):
            kc, vc = pl.ds(hb * DK, DK), pl.ds(hb * DV, DV)
            gl, cum, clast, fw, bw, qs, k, v, scores, lower = yield from _gla_chunk(
                q_ref[:, kc], k_ref[:, kc], v_ref[:, vc], gl_ref[:, kc], bias_ref[:, kc])
            sp = s_acc[hb]
            st_ref[hb, 0] = sp
            o = _bdot(scores, v) + _bdot(qs * jnp.exp(cum), sp, NT)
            s_acc[hb] = sp * jnp.exp(clast) + _bdot(v, k * jnp.exp(clast - cum), TN)
            yield
            rstd = lax.rsqrt(jnp.mean(o * o, axis=-1, keepdims=True) + NORM_EPS)
            og_ref[:, vc] = (o * rstd * gain_ref[hb] * _silu(r_ref[:, vc])).astype(og_ref.dtype)

        _interleave([head(hb) for hb in range(GLA_HB)])

    q, k, v, r, bias_s, gn, st, ov = _gla_specs(N, False)
    return pl.pallas_call(
        body, name=name, grid=(H // GLA_HB, N), in_specs=[q, k, v, r, q, bias_s, gn], out_specs=(ov, st),
        out_shape=(jax.ShapeDtypeStruct((T, H * DV), BF16), jax.ShapeDtypeStruct((H, N, DV, DK), F32)),
        scratch_shapes=[pltpu.VMEM((GLA_HB, DV, DK), F32)],
        compiler_params=_params(("arbitrary", "arbitrary")),
    )(proj, proj, proj, proj, glogit, bias, gain)


def _gla_bwd(proj, glogit, bias, gain, states, dog, *, name):
    T = proj.shape[0]
    H, C, DK, DV = GLA_HEADS, CHUNK, GLA_DK, GLA_DV
    N = T // C

    def body(q_ref, k_ref, v_ref, r_ref, gl_ref, bias_ref, gain_ref, st_ref, dog_ref,
             dq_ref, dk_ref, dv_ref, dr_ref, dgl_ref, dgain_ref, ds_acc):
        n = pl.program_id(1)

        @pl.when(n == 0)
        def _():
            ds_acc[...] = jnp.zeros_like(ds_acc)
            dgain_ref[...] = jnp.zeros_like(dgain_ref)

        def head(hb):
            kc, vc = pl.ds(hb * DK, DK), pl.ds(hb * DV, DV)
            gl, cum, clast, fw, bw, qs, k, v, scores, lower = yield from _gla_chunk(
                q_ref[:, kc], k_ref[:, kc], v_ref[:, vc], gl_ref[:, kc], bias_ref[:, kc])
            sp = st_ref[hb, 0]
            ecum, e2, cdec = jnp.exp(cum), jnp.exp(clast - cum), jnp.exp(clast)
            q_in, k_end = qs * ecum, k * e2
            o = _bdot(scores, v) + _bdot(q_in, sp, NT)
            yield
            rstd = lax.rsqrt(jnp.mean(o * o, axis=-1, keepdims=True) + NORM_EPS)
            oh = o * rstd
            r = r_ref[:, vc]
            gain_v = gain_ref[hb]
            dy = dog_ref[:, vc].astype(F32)
            dr_ref[:, vc] = (dy * oh * gain_v * _dsilu(r)).astype(dr_ref.dtype)
            dnorm = dy * _silu(r)
            dgain_ref[hb] += jnp.sum(dnorm * oh, axis=0, keepdims=True)
            doh = dnorm * gain_v
            do = rstd * (doh - oh * jnp.mean(doh * oh, axis=-1, keepdims=True))
            dsn = ds_acc[hb]
            dq_in = _bdot(do, sp)
            dk_end = _bdot(v, dsn)
            dv_ref[:, vc] = (_bdot(k_end, dsn, NT) + _bdot(scores, do, TN)).astype(dv_ref.dtype)
            dcdec = jnp.sum(dsn * sp, axis=0, keepdims=True)
            ds_acc[hb] = dsn * cdec + _bdot(do, q_in, TN)
            dsc = _bdot(do, v, NT)
            yield
            ds_lo = jnp.where(lower, dsc, 0.0)
            ds_up = jnp.where(lower, 0.0, dsc)
            qf, kb, qb, kf = qs * fw, k * bw, qs * bw, k * fw
            dqf, dkb = _bdot(ds_lo, kb), _bdot(ds_lo, qf, TN)
            dqb, dkf = _bdot(ds_up, kf), _bdot(ds_up, qb, TN)
            yield
            dq_ref[:, kc] = ((dqf * fw + dqb * bw + dq_in * ecum) * (DK ** -0.5)).astype(dq_ref.dtype)
            dk_ref[:, kc] = (dkb * bw + dkf * fw + dk_end * e2).astype(dk_ref.dtype)
            dz = (dqf * qs + dkf * k) * fw - (dqb * qs + dkb * k) * bw
            kk = dk_end * k_end
            dcum = dz + dq_in * q_in - kk
            rows = _iota2((C, DK), 0)
            dcum = dcum + jnp.where(rows == C // 2 - 1, -jnp.sum(dz, axis=0, keepdims=True), 0.0)
            dcum = dcum + jnp.where(rows == C - 1, jnp.sum(kk, axis=0, keepdims=True) + dcdec * cdec, 0.0)
            upper = _iota2((C, C), 0) <= _iota2((C, C), 1)
            dla = _fdot(jnp.where(upper, 1.0, 0.0), dcum)
            yield
            dgl_ref[:, kc] = dla * (1.0 / GLA_TAU) * _sigmoid(-gl)

        _interleave([head(hb) for hb in range(GLA_HB)])

    q, k, v, r, bias_s, gn, st, ov = _gla_specs(N, True)
    return pl.pallas_call(
        body, name=name, grid=(H // GLA_HB, N), in_specs=[q, k, v, r, q, bias_s, gn, st, ov],
        out_specs=(q, q, ov, ov, q, gn),
        out_shape=(jax.ShapeDtypeStruct((T, H * DK), BF16), jax.ShapeDtypeStruct((T, H * DK), BF16),
                   jax.ShapeDtypeStruct((T, H * DV), BF16), jax.ShapeDtypeStruct((T, H * DV), BF16),
                   jax.ShapeDtypeStruct((T, H * DK), F32), jax.ShapeDtypeStruct((H, 1, DV), F32)),
        scratch_shapes=[pltpu.VMEM((GLA_HB, DV, DK), F32)],
        compiler_params=_params(("arbitrary", "arbitrary")),
    )(proj, proj, proj, proj, glogit, bias, gain, states, dog)


def _conv(xv, w_ref):
    out = _shift_down(xv, CONV_WIDTH - 1) * w_ref[0:1, :]
    for tap in range(1, CONV_WIDTH):
        out = out + _shift_down(xv, CONV_WIDTH - 1 - tap) * w_ref[tap:tap + 1, :]
    return out


def _conv_bwd(xv, w_ref, dpre, dw_ref):
    dx = None
    for tap in range(CONV_WIDTH):
        s = CONV_WIDTH - 1 - tap
        t = _shift_up(dpre, s) * w_ref[tap:tap + 1, :]
        dx = t if dx is None else dx + t
        dw_ref[tap:tap + 1, :] = jnp.sum(dpre * _shift_down(xv, s), axis=0, keepdims=True)
    return dx


CONV_COLS = 256


def _conv_silu_fwd(x, w, *, name):
    T = x.shape[0]
    n = w.shape[1]

    def body(x_ref, w_ref, o_ref):
        o_ref[...] = _silu(_conv(x_ref[...], w_ref))

    return pl.pallas_call(
        body, name=name, grid=(n // CONV_COLS,),
        in_specs=[pl.BlockSpec((T, CONV_COLS), lambda j: (0, j)), pl.BlockSpec((CONV_WIDTH, CONV_COLS), lambda j: (0, j))],
        out_specs=pl.BlockSpec((T, CONV_COLS), lambda j: (0, j)),
        out_shape=jax.ShapeDtypeStruct((T, n), F32), compiler_params=_params(("parallel",)),
    )(x, w)


def _conv_silu_bwd(x, w, dact, *, name):
    T = x.shape[0]
    n = w.shape[1]

    def body(x_ref, w_ref, da_ref, dx_ref, dw_ref):
        xv = x_ref[...]
        dpre = da_ref[...] * _dsilu(_conv(xv, w_ref))
        dx_ref[...] = _conv_bwd(xv, w_ref, dpre, dw_ref).astype(dx_ref.dtype)

    blk = pl.BlockSpec((T, CONV_COLS), lambda j: (0, j))
    wb = pl.BlockSpec((CONV_WIDTH, CONV_COLS), lambda j: (0, j))
    return pl.pallas_call(
        body, name=name, grid=(n // CONV_COLS,), in_specs=[blk, wb, blk], out_specs=(blk, wb),
        out_shape=(jax.ShapeDtypeStruct((T, n), BF16), jax.ShapeDtypeStruct((CONV_WIDTH, n), F32)),
        compiler_params=_params(("parallel",)),
    )(x, w, dact)


def _interleave(gens):
    results = [None] * len(gens)
    live = list(range(len(gens)))
    while live:
        for i in list(live):
            try:
                next(gens[i])
            except StopIteration as done:
                results[i] = done.value
                live.remove(i)
    return results


def _unit_lower_inverse(a):
    n = a.shape[0]
    eye = jnp.where(_iota2((n, n), 0) == _iota2((n, n), 1), 1.0, 0.0)
    p = -a
    t = eye + p
    for _ in range(5):
        p = _fdot(p, p)
        yield
        t = t + _fdot(t, p)
        yield
    return t


def _gdn_specs(N, rev):
    H, C, DK, DV = GDN_HEADS, CHUNK, GDN_DK, GDN_DV
    cn = (lambda n: N - 1 - n) if rev else (lambda n: n)
    HB, G = GDN_HB, H // GDN_HB
    q = pl.BlockSpec((C, HB * DK), lambda h, n: (cn(n), h))
    k = pl.BlockSpec((C, HB * DK), lambda h, n: (cn(n), G + h))
    v = pl.BlockSpec((C, HB * DV), lambda h, n: (cn(n), 2 * G + h))
    z = pl.BlockSpec((C, HB * DV), lambda h, n: (cn(n), 3 * G + h))
    gates = pl.BlockSpec((C, LANE), lambda h, n: (cn(n), 0))
    sc = pl.BlockSpec((HB, 1, 1), lambda h, n: (h, 0, 0))
    gain = pl.BlockSpec((1, DV), lambda h, n: (0, 0))
    st = pl.BlockSpec((HB, 1, DK, DV), lambda h, n: (h, cn(n), 0, 0))
    return q, k, v, z, gates, sc, gain, st


def _gdn_chunk(q_ref, k_ref, v_ref, gates_ref, alog_ref, dtb_ref, h):
    H, C, DK, DV = GDN_HEADS, CHUNK, GDN_DK, GDN_DV
    gates = gates_ref[...]
    lane = _iota2(gates.shape, 1)
    bl = jnp.sum(jnp.where(lane == h, gates, 0.0), axis=1, keepdims=True)
    al = jnp.sum(jnp.where(lane == H + h, gates, 0.0), axis=1, keepdims=True)
    beta = _sigmoid(bl)
    ea = jnp.exp(alog_ref[...])
    xs = al + dtb_ref[...]
    la = -ea * _softplus(xs)
    ii, jj = _iota2((C, C), 0), _iota2((C, C), 1)
    strict = ii > jj
    cum_col = jnp.sum(jnp.where(ii >= jj, _col_to_row(la), 0.0), axis=1, keepdims=True)
    cum_row = jnp.sum(jnp.where(ii <= jj, la, 0.0), axis=0, keepdims=True)
    q, k, v = q_ref[...], k_ref[...], v_ref[...]
    rq = lax.rsqrt(jnp.sum(q * q, axis=-1, keepdims=True) + NORM_EPS)
    rk = lax.rsqrt(jnp.sum(k * k, axis=-1, keepdims=True) + NORM_EPS)
    qn = q * rq * (DK ** -0.5)
    kn = k * rk
    rel = jnp.where(strict, jnp.exp(jnp.where(strict, cum_col - cum_row, 0.0)), 0.0)
    rg = rel * _bdot(kn, kn, NT)
    yield
    a = beta * rg
    tm = yield from _unit_lower_inverse(a)
    e_col = jnp.exp(cum_col)
    clast = _pick_row(cum_col, C - 1)
    rhs = jnp.concatenate([beta * v, (beta * e_col) * kn], axis=1)
    sol = _fdot(tm, rhs)
    yield
    u, w = sol[:, :DV], sol[:, DV:]
    dd = jnp.exp(clast - cum_col)
    ke = kn * dd
    g = jnp.exp(clast)
    eye = jnp.where(_iota2((DK, DK), 0) == _iota2((DK, DK), 1), 1.0, 0.0)
    trans = g * eye - _bdot(ke, w, TN)
    inject = _bdot(ke, u, TN)
    yield
    return dict(beta=beta, ea=ea, xs=xs, la=la, strict=strict, ii=ii, jj=jj, q=q, k=k, v=v, rq=rq, rk=rk,
                qn=qn, kn=kn, rel=rel, rg=rg, a=a, tm=tm, e_col=e_col, sol=sol, u=u, w=w, dd=dd, ke=ke,
                g=g, eye=eye, trans=trans, inject=inject)


def _gdn_fwd(act, proj, gates, a_log, dt_bias, gain, *, name):
    T = act.shape[0]
    H, C, DK, DV = GDN_HEADS, CHUNK, GDN_DK, GDN_DV
    N = T // C

    def body(q_ref, k_ref, v_ref, z_ref, gates_ref, alog_ref, dtb_ref, gain_ref, og_ref, st_ref, s_acc):
        hg, n = pl.program_id(0), pl.program_id(1)

        @pl.when(n == 0)
        def _():
            s_acc[...] = jnp.zeros_like(s_acc)

        def head(hb):
            cols = pl.ds(hb * DK, DK)
            c = yield from _gdn_chunk(q_ref.at[:, cols], k_ref.at[:, cols], v_ref.at[:, cols], gates_ref,
                                      alog_ref.at[hb], dtb_ref.at[hb], hg * GDN_HB + hb)
            sp = s_acc[hb]
            st_ref[hb, 0] = sp
            snew = _bdot(c["trans"], sp) + c["inject"]
            yield
            s_acc[hb] = snew
            o = _bdot(c["qn"], snew)
            yield
            rstd = lax.rsqrt(jnp.mean(o * o, axis=-1, keepdims=True) + NORM_EPS)
            og_ref[:, cols] = (o * rstd * gain_ref[...] * _silu(z_ref[:, cols])).astype(og_ref.dtype)

        _interleave([head(hb) for hb in range(GDN_HB)])

    q, k, v, z, gt, sc, gn, st = _gdn_specs(N, False)
    return pl.pallas_call(
        body, name=name, grid=(H // GDN_HB, N), in_specs=[q, k, v, z, gt, sc, sc, gn], out_specs=(q, st),
        out_shape=(jax.ShapeDtypeStruct((T, H * DV), BF16), jax.ShapeDtypeStruct((H, N, DK, DV), F32)),
        scratch_shapes=[pltpu.VMEM((GDN_HB, DK, DV), F32)],
        compiler_params=_params(("arbitrary", "arbitrary")),
    )(act, act, act, proj, gates, a_log, dt_bias, gain)


def _gdn_bwd(act, proj, gates, a_log, dt_bias, gain, states, dog, *, name):
    T = act.shape[0]
    H, C, DK, DV = GDN_HEADS, CHUNK, GDN_DK, GDN_DV
    N = T // C

    def rsum(x):
        return jnp.sum(x, axis=1, keepdims=True)

    def body(q_ref, k_ref, v_ref, z_ref, gates_ref, alog_ref, dtb_ref, gain_ref, st_ref, dog_ref,
             dq_ref, dk_ref, dv_ref, dz_ref, dgates_ref, dscal_ref, dgain_ref, ds_acc):
        hg, n = pl.program_id(0), pl.program_id(1)

        @pl.when(n == 0)
        def _():
            ds_acc[...] = jnp.zeros_like(ds_acc)
            dgain_ref[...] = jnp.zeros_like(dgain_ref)
            dscal_ref[...] = jnp.zeros_like(dscal_ref)

        _interleave([one_head(hb, hg * GDN_HB + hb, q_ref, k_ref, v_ref, z_ref, gates_ref, alog_ref, dtb_ref, gain_ref,
                              st_ref, dog_ref, dq_ref, dk_ref, dv_ref, dz_ref, dgates_ref, dscal_ref, dgain_ref, ds_acc)
                     for hb in range(GDN_HB)])

    def one_head(hb, h, q_ref, k_ref, v_ref, z_ref, gates_ref, alog_ref, dtb_ref, gain_ref, st_ref, dog_ref,
                 dq_ref, dk_ref, dv_ref, dz_ref, dgates_ref, dscal_ref, dgain_ref, ds_acc):
        cols = pl.ds(hb * DK, DK)
        c = yield from _gdn_chunk(q_ref.at[:, cols], k_ref.at[:, cols], v_ref.at[:, cols], gates_ref,
                                  alog_ref.at[hb], dtb_ref.at[hb], h)
        beta, kn, qn, v, ke, u, w, dd, e_col = c["beta"], c["kn"], c["qn"], c["v"], c["ke"], c["u"], c["w"], c["dd"], c["e_col"]
        sp = st_ref[hb, 0]
        snew = _bdot(c["trans"], sp) + c["inject"]
        yield
        o = _bdot(qn, snew)
        yield
        rstd = lax.rsqrt(jnp.mean(o * o, axis=-1, keepdims=True) + NORM_EPS)
        oh = o * rstd
        z = z_ref[:, cols]
        gain_v = gain_ref[...]
        dy = dog_ref[:, cols].astype(F32)
        dz_ref[:, cols] = (dy * oh * gain_v * _dsilu(z)).astype(dz_ref.dtype)
        dnorm = dy * _silu(z)
        dgain_ref[hb] += jnp.sum(dnorm * oh, axis=0, keepdims=True)
        doh = dnorm * gain_v
        do = rstd * (doh - oh * jnp.mean(doh * oh, axis=-1, keepdims=True))

        dstot = ds_acc[hb] + _bdot(qn, do, TN)
        dqn = _bdot(do, snew, NT)
        yield
        dtrans = _bdot(dstot, sp, NT)
        ds_acc[hb] = _bdot(c["trans"], dstot, TN)
        yield
        dg = jnp.sum(jnp.sum(dtrans * c["eye"], axis=1, keepdims=True), axis=0, keepdims=True)
        m = -dtrans
        dke = _bdot(w, m, NT) + _bdot(u, dstot, NT)
        dw = _bdot(ke, m)
        du = _bdot(ke, dstot)
        yield
        drhs = _fdot(c["tm"], jnp.concatenate([du, dw], axis=1), TN)
        yield
        da = jnp.where(c["strict"], -_fdot(drhs, c["sol"], NT), 0.0)
        yield
        drhs_u, drhs_w = drhs[:, :DV], drhs[:, DV:]
        rwk = rsum(drhs_w * kn)
        dbeta = rsum(da * c["rg"]) + rsum(drhs_u * v) + rwk * e_col
        dgm = da * beta * c["rel"]
        dkn = _bdot(dgm, kn) + _bdot(dgm, kn, TN) + (beta * e_col) * drhs_w + dd * dke
        yield
        dv_ref[:, cols] = beta * drhs_u
        r_ = da * c["a"]
        ddd = rsum(dke * kn)
        dc = rsum(r_) - _row_to_col(jnp.sum(r_, axis=0, keepdims=True)) + beta * rwk * e_col - ddd * dd
        dclast = jnp.sum(ddd * dd, axis=0, keepdims=True) + dg * c["g"]
        dc = dc + jnp.where(_iota2((C, 1), 0) == C - 1, dclast, 0.0)
        dla = jnp.sum(jnp.where(c["ii"] <= c["jj"], _col_to_row(dc), 0.0), axis=1, keepdims=True)
        dalog = jnp.sum(dla * c["la"], axis=0, keepdims=True)
        dxs = dla * (-c["ea"]) * _sigmoid(c["xs"])
        ddtb = jnp.sum(dxs, axis=0, keepdims=True)
        dbl = dbeta * beta * (1.0 - beta)
        lane = _iota2((C, LANE), 1)
        dgates_ref[hb] = jnp.where(lane == 0, dbl, jnp.where(lane == 1, dxs, 0.0))
        lane8 = _iota2((8, LANE), 1)
        dscal_ref[hb] += jnp.where(lane8 == 0, dalog, jnp.where(lane8 == 1, ddtb, 0.0))
        dk_ref[:, cols] = c["rk"] * (dkn - kn * rsum(dkn * kn))
        qh = c["q"] * c["rq"]
        dqs = dqn * (DK ** -0.5)
        dq_ref[:, cols] = c["rq"] * (dqs - qh * rsum(dqs * qh))

    q, k, v, z, gt, sc, gn, st = _gdn_specs(N, True)
    dgt = pl.BlockSpec((GDN_HB, C, LANE), lambda h, n: (h, N - 1 - n, 0))
    dsc = pl.BlockSpec((GDN_HB, 8, LANE), lambda h, n: (h, 0, 0))
    dgn = pl.BlockSpec((GDN_HB, 1, DV), lambda h, n: (h, 0, 0))
    sh = jax.ShapeDtypeStruct((T, H * DK), F32)
    return pl.pallas_call(
        body, name=name, grid=(H // GDN_HB, N), in_specs=[q, k, v, z, gt, sc, sc, gn, st, q],
        out_specs=(q, q, q, q, dgt, dsc, dgn),
        out_shape=(sh, sh, sh, jax.ShapeDtypeStruct((T, H * DV), BF16),
                   jax.ShapeDtypeStruct((H, T, LANE), F32), jax.ShapeDtypeStruct((H, 8, LANE), F32),
                   jax.ShapeDtypeStruct((H, 1, DV), F32)),
        scratch_shapes=[pltpu.VMEM((GDN_HB, DK, DV), F32)],
        compiler_params=_params(("arbitrary", "arbitrary")),
    )(act, act, act, proj, gates, a_log, dt_bias, gain, states, dog)


SUBLANES = 8


def _linear_scan(a_ref, b_ref, h_ref, reverse):
    T, W = a_ref.shape
    nb = T // SUBLANES
    row = _iota2((SUBLANES, W), 0)

    def blk(bi, carry):
        bb = (nb - 1 - bi) if reverse else bi
        off = pl.multiple_of(bb * SUBLANES, SUBLANES)
        a = a_ref[pl.ds(off, SUBLANES), :]
        b = b_ref[pl.ds(off, SUBLANES), :]
        for d in (1, 2, 4):
            if reverse:
                edge = row >= SUBLANES - d
                a_sh = jnp.where(edge, 1.0, pltpu.roll(a, SUBLANES - d, 0))
                b_sh = jnp.where(edge, 0.0, pltpu.roll(b, SUBLANES - d, 0))
            else:
                edge = row < d
                a_sh = jnp.where(edge, 1.0, pltpu.roll(a, d, 0))
                b_sh = jnp.where(edge, 0.0, pltpu.roll(b, d, 0))
            b = a * b_sh + b
            a = a * a_sh
        h = a * carry + b
        h_ref[pl.ds(off, SUBLANES), :] = h
        return h[0:1, :] if reverse else h[SUBLANES - 1:SUBLANES, :]

    lax.fori_loop(0, nb, blk, jnp.zeros((1, W), F32))


def _lru_specs(T):
    B, W = LRU_BLOCKS, LRU_BLOCK
    xb = pl.BlockSpec((T, W), lambda j: (0, j))
    yb = pl.BlockSpec((T, W), lambda j: (0, B + j))
    cw = pl.BlockSpec((CONV_WIDTH, W), lambda j: (0, j))
    vec = pl.BlockSpec((1, W), lambda j: (0, j))
    wg = pl.BlockSpec((1, W, W), lambda j: (j, 0, 0))
    bg = pl.BlockSpec((1, 1, W), lambda j: (j, 0, 0))
    return xb, yb, cw, vec, wg, bg


def _lru_gates(xb_ref, cw_ref, cb_ref, wr_ref, br_ref, wi_ref, bi_ref, lam_ref):
    xv = xb_ref[...]
    xc = _conv(xv, cw_ref) + cb_ref[...]
    r = _sigmoid(_bdot(xc, wr_ref[0]) + br_ref[0])
    i = _sigmoid(_bdot(xc, wi_ref[0]) + bi_ref[0])
    sp = _softplus(-lam_ref[...])
    la = -LRU_C * sp * r
    a = jnp.exp(la)
    s = jnp.sqrt(-_expm1(2.0 * la))
    return xv, xc, r, i, sp, a, s


def _lru_fwd(proj, conv_w, conv_b, w_r, b_r, w_i, b_i, lam, *, name):
    T = proj.shape[0]
    B, W = LRU_BLOCKS, LRU_BLOCK

    def body(xb_ref, yb_ref, cw_ref, cb_ref, wr_ref, br_ref, wi_ref, bi_ref, lam_ref, og_ref, hs_ref, a_s, u_s):
        xv, xc, r, i, sp, a, s = _lru_gates(xb_ref, cw_ref, cb_ref, wr_ref, br_ref, wi_ref, bi_ref, lam_ref)
        a_s[...] = a
        u_s[...] = s * (i * xc)
        _linear_scan(a_s, u_s, hs_ref, False)
        og_ref[...] = (hs_ref[...] * _gelu(yb_ref[...])).astype(og_ref.dtype)

    xb, yb, cw, vec, wg, bg = _lru_specs(T)
    return pl.pallas_call(
        body, name=name, grid=(B,), in_specs=[xb, yb, cw, vec, wg, bg, wg, bg, vec], out_specs=(xb, xb),
        out_shape=(jax.ShapeDtypeStruct((T, B * W), BF16), jax.ShapeDtypeStruct((T, B * W), F32)),
        scratch_shapes=[pltpu.VMEM((T, W), F32), pltpu.VMEM((T, W), F32)],
        compiler_params=_params(("arbitrary",)),
    )(proj, proj, conv_w, conv_b, w_r, b_r, w_i, b_i, lam)


def _lru_bwd(proj, conv_w, conv_b, w_r, b_r, w_i, b_i, lam, hs, dout, *, name):
    T = proj.shape[0]
    B, W = LRU_BLOCKS, LRU_BLOCK

    def csum(x):
        return jnp.sum(x, axis=0, keepdims=True)

    def body(xb_ref, yb_ref, cw_ref, cb_ref, wr_ref, br_ref, wi_ref, bi_ref, lam_ref, hs_ref, do_ref,
             dxb_ref, dyb_ref, dcw_ref, dcb_ref, dwr_ref, dbr_ref, dwi_ref, dbi_ref, dlam_ref, a_s, b_s, g_s):
        xv, xc, r, i, sp, a, s = _lru_gates(xb_ref, cw_ref, cb_ref, wr_ref, br_ref, wi_ref, bi_ref, lam_ref)
        h = hs_ref[...]
        yb = yb_ref[...]
        dout = do_ref[...].astype(F32)
        dyb_ref[...] = (dout * h * _dgelu(yb)).astype(dyb_ref.dtype)
        a_s[...] = _shift_up(a, 1)
        b_s[...] = dout * _gelu(yb)
        _linear_scan(a_s, b_s, g_s, True)
        g = g_s[...]
        da = g * _shift_down(h, 1)
        ds = g * (i * xc)
        di = g * s * xc
        dxc = g * s * i
        dla = da * a - ds * (a * a) / s
        dr = dla * (-LRU_C * sp)
        dlam_ref[...] = csum(dla * r) * (LRU_C * _sigmoid(-lam_ref[...]))
        dpr = dr * r * (1.0 - r)
        dpi = di * i * (1.0 - i)
        dxc = dxc + _bdot(dpr, wr_ref[0], NT) + _bdot(dpi, wi_ref[0], NT)
        dwr_ref[0] = _bdot(xc, dpr, TN)
        dwi_ref[0] = _bdot(xc, dpi, TN)
        dbr_ref[0] = csum(dpr)
        dbi_ref[0] = csum(dpi)
        dcb_ref[...] = csum(dxc)
        dxb_ref[...] = _conv_bwd(xv, cw_ref, dxc, dcw_ref).astype(dxb_ref.dtype)

    xb, yb, cw, vec, wg, bg = _lru_specs(T)
    act = jax.ShapeDtypeStruct((T, B * W), BF16)
    return pl.pallas_call(
        body, name=name, grid=(B,), in_specs=[xb, yb, cw, vec, wg, bg, wg, bg, vec, xb, xb],
        out_specs=(xb, xb, cw, vec, wg, bg, wg, bg, vec),
        out_shape=(act, act, jax.ShapeDtypeStruct((CONV_WIDTH, B * W), F32), jax.ShapeDtypeStruct((1, B * W), F32),
                   jax.ShapeDtypeStruct((B, W, W), F32), jax.ShapeDtypeStruct((B, 1, W), F32),
                   jax.ShapeDtypeStruct((B, W, W), F32), jax.ShapeDtypeStruct((B, 1, W), F32),
                   jax.ShapeDtypeStruct((1, B * W), F32)),
        scratch_shapes=[pltpu.VMEM((T, W), F32)] * 3,
        compiler_params=_params(("arbitrary",)),
    )(proj, proj, conv_w, conv_b, w_r, b_r, w_i, b_i, lam, hs, dout)


MESH = pl.DeviceIdType.MESH
N_CHIPS = 4
AG_COPIES = 7


def _mesh_pos():
    return lax.axis_index("x"), lax.axis_index("y"), lax.axis_index("c")


def _hbm_specs(n):
    return [pl.BlockSpec(memory_space=pltpu.HBM)] * n


def _all_gather(shards, *, name):
    n = len(shards)

    def body(*refs):
        xs, outs = refs[:n], refs[n:2 * n]
        send_sems, recv_sems, local_sems = refs[2 * n:]
        x, y, c = _mesh_pos()
        me, sibling = (x, y, c), (x, y, 1 - c)
        chips = [(1 - x, y), (x, 1 - y), (1 - x, 1 - y)]

        def rows(t, px, py, pc):
            return outs[t].at[4 * px + 2 * py + pc]

        def copy(t, k, block, to, src=None):
            return pltpu.make_async_remote_copy(
                src_ref=rows(t, *block) if src is None else src, dst_ref=rows(t, *block),
                send_sem=send_sems.at[t * AG_COPIES + k], recv_sem=recv_sems.at[t * AG_COPIES + k],
                device_id=to, device_id_type=MESH)

        mine = [pltpu.make_async_copy(xs[t], rows(t, *me), local_sems.at[t]) for t in range(n)]
        for cp in mine:
            cp.start()
        first = []
        for t in range(n):
            first.append(copy(t, 0, me, sibling, src=xs[t]))
            first += [copy(t, 1 + j, me, (*chip, c), src=xs[t]) for j, chip in enumerate(chips)]
        for cp in first:
            cp.start()
        passed = []
        for j, chip in enumerate(chips):
            for t in range(n):
                copy(t, 1 + j, (*chip, c), me).wait_recv()
                cp = copy(t, 4 + j, (*chip, c), sibling)
                cp.start()
                passed.append(cp)
        for t in range(n):
            copy(t, 0, sibling, me).wait_recv()
        for j, chip in enumerate(chips):
            for t in range(n):
                copy(t, 4 + j, (*chip, 1 - c), me).wait_recv()
        for cp in first + passed:
            cp.wait_send()
        for cp in mine:
            cp.wait()

    return pl.pallas_call(
        body, name=name,
        out_shape=[jax.ShapeDtypeStruct((N_DEV,) + s.shape, s.dtype) for s in shards],
        in_specs=_hbm_specs(n), out_specs=_hbm_specs(n),
        scratch_shapes=[pltpu.SemaphoreType.DMA((n * AG_COPIES,)), pltpu.SemaphoreType.DMA((n * AG_COPIES,)),
                        pltpu.SemaphoreType.DMA((n,))],
    )(*shards)


SIDE_EFFECT = pltpu.SideEffectType.DATAFLOW_SIDE_EFFECTING


def _copies(plan, refs, send_sems, recv_sems):
    return [pltpu.make_async_remote_copy(src_ref=src, dst_ref=dst, send_sem=send_sems.at[k], recv_sem=recv_sems.at[k],
                                         device_id=to, device_id_type=MESH)
            for k, (src, dst, to) in enumerate(plan(refs))]


def _split_start(bufs, plan, n_copies, *, name, deps=()):
    n = len(bufs)

    def body(*refs):
        send_sems, recv_sems = refs[n + len(deps)], refs[n + len(deps) + 1]
        token = refs[-1]
        for cp in _copies(plan, refs[:n], send_sems, recv_sems):
            cp.start()
        token[...] = jnp.zeros_like(token)

    hbm, sem = pl.BlockSpec(memory_space=pltpu.HBM), pl.BlockSpec(memory_space=pltpu.SEMAPHORE)
    out = pl.pallas_call(
        body, name=name,
        out_shape=(pltpu.SemaphoreType.DMA((n_copies,)), pltpu.SemaphoreType.DMA((n_copies,)),
                   *[pltpu.HBM(b.shape, b.dtype) for b in bufs], jax.ShapeDtypeStruct(TOKEN_SHAPE, F32)),
        in_specs=[hbm] * n + [pl.BlockSpec(memory_space=pl.ANY)] * len(deps),
        out_specs=(sem, sem, *[hbm] * n, pl.BlockSpec(memory_space=pltpu.VMEM)),
        input_output_aliases={i: 2 + i for i in range(n)},
        compiler_params=pltpu.CompilerParams(has_side_effects=SIDE_EFFECT),
    )(*[pltpu.with_memory_space_constraint(b, pltpu.HBM) for b in bufs], *deps)
    return out[0], out[1], list(out[2:2 + n]), out[-1]


def _split_wait(send_sems, recv_sems, bufs, plan, after, *, name):
    n = len(bufs)

    def body(*refs):
        for cp in _copies(plan, refs[:n], refs[n], refs[n + 1]):
            cp.wait_send()
            cp.wait_recv()

    hbm, sem = pl.BlockSpec(memory_space=pltpu.HBM), pl.BlockSpec(memory_space=pltpu.SEMAPHORE)
    out = pl.pallas_call(
        body, name=name, out_shape=tuple(pltpu.HBM(b.shape, b.dtype) for b in bufs),
        in_specs=[hbm] * n + [sem, sem, pl.BlockSpec(memory_space=pl.ANY)], out_specs=tuple([hbm] * n),
        input_output_aliases={i: i for i in range(n)},
        compiler_params=pltpu.CompilerParams(has_side_effects=SIDE_EFFECT),
    )(*bufs, send_sems, recv_sems, after)
    return list(out)


def _block(ref, in_cols, d):
    if not in_cols:
        return ref.at[d]
    c = ref.shape[1] // N_DEV
    return ref.at[:, pl.ds(pl.multiple_of(d * c, LANE), c)]


def _plan_gather_a(in_cols):
    n = len(in_cols)

    def plan(refs):
        x, y, c = _mesh_pos()
        me = 4 * x + 2 * y + c
        peers = [(x, y, 1 - c), (1 - x, y, c), (x, 1 - y, c), (1 - x, 1 - y, c)]
        return [(refs[t], _block(refs[n + t], in_cols[t], me), to) for t in range(n) for to in peers]
    return plan


def _plan_gather_b(in_cols):
    n = len(in_cols)

    def plan(refs):
        x, y, c = _mesh_pos()
        ds = [4 * px + 2 * py + c for px, py in [(1 - x, y), (x, 1 - y), (1 - x, 1 - y)]]
        return [(_block(refs[t], in_cols[t], d), _block(refs[t], in_cols[t], d), (x, y, 1 - c))
                for t in range(n) for d in ds]
    return plan


def _plan_scatter_pair(in_cols):
    n = len(in_cols)

    def plan(refs):
        x, y, c = _mesh_pos()
        return [(_block(refs[t], in_cols[t], 2 * q + (1 - c)), refs[n + t].at[q], (x, y, 1 - c))
                for t in range(n) for q in range(N_CHIPS)]
    return plan


def _plan_scatter_chips(n):
    def plan(refs):
        x, y, c = _mesh_pos()
        chips = [(1 - x, y), (x, 1 - y), (1 - x, 1 - y)]
        return [(refs[t].at[2 * px + py], refs[n + t].at[j], (px, py, c))
                for t in range(n) for j, (px, py) in enumerate(chips)]
    return plan


def _insert_block(land, shard, device, in_cols, *, name):
    r, c = shard.shape
    tr = _tile(r, (512, 256))

    def body(d_ref, s_ref, land_ref, o_ref):
        if in_cols:
            o_ref[...] = s_ref[...]
        else:
            o_ref[0] = s_ref[...]

    if in_cols:
        o_spec = pl.BlockSpec((tr, c), lambda i, d: (i, d[0]))
    else:
        o_spec = pl.BlockSpec((1, tr, c), lambda i, d: (d[0], i, 0))
    return pl.pallas_call(
        body, name=name,
        grid_spec=pltpu.PrefetchScalarGridSpec(
            num_scalar_prefetch=1, grid=(r // tr,),
            in_specs=[pl.BlockSpec((tr, c), lambda i, d: (i, 0)), pl.BlockSpec(memory_space=pl.ANY)],
            out_specs=o_spec),
        out_shape=jax.ShapeDtypeStruct(land.shape, land.dtype), input_output_aliases={2: 0},
        compiler_params=_params(("parallel",)),
    )(device, shard, land)


PAIR_ROWS = (512, 256)


def _pair_add(g, a, core, in_cols, *, name):
    _, R, C = a.shape
    tr = _tile(R, PAIR_ROWS)

    def body(c_ref, g_ref, a_ref, o_ref):
        gv = g_ref[...] if in_cols else g_ref[0]
        o_ref[0] = (gv.astype(F32) + a_ref[0].astype(F32)).astype(o_ref.dtype)

    blk = pl.BlockSpec((1, tr, C), lambda q, i, c: (q, i, 0))
    if in_cols:
        g_spec = pl.BlockSpec((tr, C), lambda q, i, c: (i, 2 * q + c[0]))
    else:
        g_spec = pl.BlockSpec((1, tr, C), lambda q, i, c: (2 * q + c[0], i, 0))
    return pl.pallas_call(
        body, name=name,
        grid_spec=pltpu.PrefetchScalarGridSpec(
            num_scalar_prefetch=1, grid=(N_CHIPS, R // tr), in_specs=[g_spec, blk], out_specs=blk),
        out_shape=jax.ShapeDtypeStruct((N_CHIPS, R, C), BF16),
        compiler_params=_params(("parallel", "parallel")),
    )(core, g, a)


ADAM_ROWS = 256


def _adamw_sharded(w, m, v, s4, b3, chip, layer, prev, *, name):
    L, R, C = w.shape
    tr = _tile(R, (ADAM_ROWS,))
    if prev is None and L > 1:
        prev = tuple(lax.empty(w.shape, F32) for _ in range(4))
    n_prev = 0 if prev is None else 4

    def body(q_ref, w_ref, m_ref, v_ref, s_ref, b_ref, *rest):
        g_out, d_out, m_out, v_out = rest[n_prev:]
        g = s_ref[0].astype(F32)
        for j in range(N_CHIPS - 1):
            g = g + b_ref[j].astype(F32)
        mn = ADAM_B1 * m_ref[0] + (1.0 - ADAM_B1) * g
        vn = ADAM_B2 * v_ref[0] + (1.0 - ADAM_B2) * (g * g)
        g_out[0] = g
        d_out[0] = -ADAM_LR * ((mn / ADAM_C1) / (jnp.sqrt(vn / ADAM_C2) + ADAM_EPS) + ADAM_WD * w_ref[0])
        m_out[0] = mn
        v_out[0] = vn

    blk = pl.BlockSpec((1, tr, C), lambda i, q: (layer, i, 0))
    sh = jax.ShapeDtypeStruct((L, R, C), F32)
    return pl.pallas_call(
        body, name=name,
        grid_spec=pltpu.PrefetchScalarGridSpec(
            num_scalar_prefetch=1, grid=(R // tr,),
            in_specs=[blk, blk, blk, pl.BlockSpec((1, tr, C), lambda i, q: (q[0], i, 0)),
                      pl.BlockSpec((N_CHIPS - 1, tr, C), lambda i, q: (0, i, 0))]
            + [pl.BlockSpec(memory_space=pl.ANY)] * n_prev,
            out_specs=(blk,) * 4),
        out_shape=(sh,) * 4, input_output_aliases={6 + k: k for k in range(n_prev)},
        compiler_params=_params(("parallel",)),
    )(chip, w, m, v, s4, b3, *(prev or ()))


FWD_NAMES = ['x', 'norm1', 'norm2', 'final_norm', 'ret_w_in', 'ret_gn_gain', 'ret_w_out', 'gdn_w_in', 'gdn_conv_w',
             'gdn_a_log', 'gdn_dt_bias', 'gdn_norm_gain', 'gdn_w_out', 'gla_w_in', 'gla_w_gate_up', 'gla_gate_bias',
             'gla_norm_gain', 'gla_w_out', 'lru_w_in', 'lru_conv_w', 'lru_conv_b', 'lru_w_rgate', 'lru_b_rgate',
             'lru_w_igate', 'lru_b_igate', 'lru_lambda', 'lru_w_out', 'mlp_w_up', 'mlp_w_down']
WEIGHT_NAMES = FWD_NAMES[1:]
ARG_NAMES = FWD_NAMES + ['loss_target'] + ['m_' + n for n in WEIGHT_NAMES] + ['v_' + n for n in WEIGHT_NAMES]

MIXER_IN = ('ret_w_in', 'gdn_w_in', 'gla_w_in', 'lru_w_in')
MIXER_OUT = ('ret_w_out', 'gdn_w_out', 'gla_w_out', 'lru_w_out')
BIG_NAMES = MIXER_IN + MIXER_OUT + ('mlp_w_up', 'mlp_w_down')
SMALL = {'norm1': False, 'norm2': False, 'final_norm': False, 'ret_gn_gain': True, 'gdn_conv_w': True,
         'gdn_a_log': False, 'gdn_dt_bias': False, 'gdn_norm_gain': False, 'gla_w_gate_up': True,
         'gla_gate_bias': True, 'gla_norm_gain': True, 'lru_conv_w': True, 'lru_conv_b': True,
         'lru_w_rgate': False, 'lru_b_rgate': False, 'lru_w_igate': False, 'lru_b_igate': False, 'lru_lambda': True}
SMALL_NAMES = tuple(n for n in WEIGHT_NAMES if n in SMALL)
EARLY_SMALL = tuple(n for n in SMALL_NAMES if n.startswith(('gdn_', 'gla_', 'lru_')))
LATE_SMALL = tuple(n for n in SMALL_NAMES if n not in EARLY_SMALL)
GDN_TAIL = 2 * GDN_HEADS


PACK_ROWS = 256


PACK_TILE = SUBLANES * LANE


def _pack(arrs):
    rows = []
    for a in arrs:
        f = a.reshape(-1).astype(F32)
        rows.append(jnp.pad(f, (0, (-f.shape[0]) % PACK_TILE)).reshape(-1, LANE))
    fill = (-sum(r.shape[0] for r in rows)) % PACK_ROWS
    if fill:
        rows.append(jnp.zeros((fill, LANE), F32))
    return jnp.concatenate(rows, axis=0)


def _unpack(buf, shapes, lead=()):
    out, r0 = [], 0
    for s in shapes:
        n = int(np.prod(s))
        nr = -(-n // PACK_TILE) * SUBLANES
        blk = buf[..., r0:r0 + nr, :].reshape(lead + (nr * LANE,))[..., :n]
        out.append(blk.reshape(lead + tuple(s)))
        r0 += nr
    return out


def _full_cols(g):
    return jnp.transpose(g, (1, 0, 2)).reshape(g.shape[1], N_DEV * g.shape[2])


def _full_rows(g):
    return g.reshape(N_DEV * g.shape[1], g.shape[2])


def _blocks_cols(dw):
    r, c = dw.shape[0], dw.shape[1] // N_DEV
    return jnp.transpose(dw.reshape(r, N_DEV, c), (1, 0, 2))


def _blocks_rows(dw):
    return dw.reshape(N_DEV, dw.shape[0] // N_DEV, dw.shape[1])


def _pad_cols(a, n=LANE):
    return jnp.pad(a, ((0, 0), (0, n - a.shape[1])))


def _mixer_fwd(layer, hn, w_in, sm, tables, deps=()):
    tag = f"l{layer}"
    if layer == 0:
        proj = _matmul(hn, w_in, name=tag + "_in", deps=deps)
        gain = sm['ret_gn_gain'][0][:, None, :]
        og, st = _ret_fwd(proj, gain, tables, name=tag + "_ret_fwd")
        return og, dict(proj=proj, st=st, gain=gain)
    if layer == 1:
        w_main, w_tail = w_in[:, :4 * D_MODEL], _pad_cols(w_in[:, 4 * D_MODEL:])
        proj = _matmul(hn, w_main, name=tag + "_in", deps=deps)
        gates = _matmul(hn, w_tail, name=tag + "_in_tail")
        conv_w = sm['gdn_conv_w'][0]
        act = _conv_silu_fwd(proj, conv_w, name=tag + "_conv")
        a_log = sm['gdn_a_log'].reshape(GDN_HEADS, 1, 1)
        dt_bias = sm['gdn_dt_bias'].reshape(GDN_HEADS, 1, 1)
        gain = sm['gdn_norm_gain']
        og, st = _gdn_fwd(act, proj, gates, a_log, dt_bias, gain, name=tag + "_gdn_fwd")
        return og, dict(proj=proj, gates=gates, act=act, st=st, conv_w=conv_w, a_log=a_log, dt_bias=dt_bias,
                        gain=gain, w_main=w_main, w_tail=w_tail)
    if layer == 2:
        w_main, w_tail = w_in[:, :3 * D_MODEL], _pad_cols(w_in[:, 3 * D_MODEL:])
        proj = _matmul(hn, w_main, name=tag + "_in", deps=deps)
        glow = _matmul(hn, w_tail, name=tag + "_in_tail")
        wgu = jnp.pad(sm['gla_w_gate_up'][0], ((0, LANE - GLA_GATE_RANK), (0, 0)))
        glogit = _matmul(glow, wgu, name=tag + "_gate_up")
        bias = sm['gla_gate_bias']
        gain = sm['gla_norm_gain'][0][:, None, :]
        og, st = _gla_fwd(proj, glogit, bias, gain, name=tag + "_gla_fwd")
        return og, dict(proj=proj, glow=glow, glogit=glogit, wgu=wgu, bias=bias, gain=gain, st=st,
                        w_main=w_main, w_tail=w_tail)
    proj = _matmul(hn, w_in, name=tag + "_in", deps=deps)
    args = (proj, sm['lru_conv_w'][0], sm['lru_conv_b'], sm['lru_w_rgate'][0], sm['lru_b_rgate'][0][:, None, :],
            sm['lru_w_igate'][0], sm['lru_b_igate'][0][:, None, :], sm['lru_lambda'])
    og, hs = _lru_fwd(*args, name=tag + "_lru_fwd")
    return og, dict(args=args, hs=hs)


def _mixer_bwd(layer, hn, w_in, dog, sv, tables, on_dw):
    tag = f"l{layer}"
    if layer == 0:
        dq, dk, dv, dg, dgain = _ret_bwd(sv['proj'], sv['gain'], tables, sv['st'], dog, name=tag + "_ret_bwd")
        dproj = jnp.concatenate([dq, dk, dv, dg], axis=1)
        deps = on_dw(_matmul(hn, dproj, ta=True, out_dtype=BF16, name=tag + "_in_dw"))
        dhn = _matmul(dproj, w_in, tb=True, name=tag + "_in_dx", deps=deps)
        return dhn, {'ret_gn_gain': dgain[:, 0][None]}
    if layer == 1:
        dq, dk, dv, dz, dgates, dscal, dgain = _gdn_bwd(
            sv['act'], sv['proj'], sv['gates'], sv['a_log'], sv['dt_bias'], sv['gain'], sv['st'], dog,
            name=tag + "_gdn_bwd")
        dact = jnp.concatenate([dq, dk, dv], axis=1)
        dqkv, dconv = _conv_silu_bwd(sv['proj'], sv['conv_w'], dact, name=tag + "_conv_bwd")
        dmain = jnp.concatenate([dqkv, dz], axis=1)
        T = dmain.shape[0]
        dtail = _pad_cols(jnp.transpose(dgates[:, :, :2], (1, 2, 0)).reshape(T, GDN_TAIL))
        dw_main = _matmul(hn, dmain, ta=True, out_dtype=BF16, name=tag + "_in_dw")
        dw_tail = _matmul(hn, dtail, ta=True, out_dtype=BF16, name=tag + "_in_tail_dw")
        deps = on_dw(jnp.concatenate([dw_main, dw_tail[:, :GDN_TAIL]], axis=1))
        dhn = _matmul(dmain, sv['w_main'], tb=True, name=tag + "_in_dx", deps=deps)
        dhn = _matmul(dtail, sv['w_tail'], tb=True, epi="add", extra=dhn, name=tag + "_in_tail_dx")
        small = {'gdn_conv_w': dconv[None], 'gdn_a_log': dscal[:, 0, 0][None], 'gdn_dt_bias': dscal[:, 0, 1][None],
                 'gdn_norm_gain': jnp.sum(dgain[:, 0], axis=0)[None]}
        return dhn, small
    if layer == 2:
        dq, dk, dv, dr, dgl, dgain = _gla_bwd(sv['proj'], sv['glogit'], sv['bias'], sv['gain'], sv['st'], dog,
                                              name=tag + "_gla_bwd")
        dmain = jnp.concatenate([dq, dk, dv, dr], axis=1)
        dglow = _matmul(dgl, sv['wgu'], tb=True, name=tag + "_gate_up_dx")
        dwgu = _matmul(sv['glow'], dgl, ta=True, name=tag + "_gate_up_dw")
        dbias = _colsum(dgl, name=tag + "_gate_bias")
        dw_main = _matmul(hn, dmain, ta=True, out_dtype=BF16, name=tag + "_in_dw")
        dw_tail = _matmul(hn, dglow, ta=True, out_dtype=BF16, name=tag + "_in_tail_dw")
        deps = on_dw(jnp.concatenate([dw_main, dw_tail[:, :GLA_GATE_RANK]], axis=1))
        dhn = _matmul(dmain, sv['w_main'], tb=True, name=tag + "_in_dx", deps=deps)
        dhn = _matmul(dglow, sv['w_tail'], tb=True, epi="add", extra=dhn, name=tag + "_in_tail_dx")
        small = {'gla_w_gate_up': dwgu[:GLA_GATE_RANK][None], 'gla_gate_bias': dbias,
                 'gla_norm_gain': dgain[:, 0][None]}
        return dhn, small
    dxb, dyb, dcw, dcb, dwr, dbr, dwi, dbi, dlam = _lru_bwd(*sv['args'], sv['hs'], dog, name=tag + "_lru_bwd")
    dproj = jnp.concatenate([dxb, dyb], axis=1)
    deps = on_dw(_matmul(hn, dproj, ta=True, out_dtype=BF16, name=tag + "_in_dw"))
    dhn = _matmul(dproj, w_in, tb=True, name=tag + "_in_dx", deps=deps)
    small = {'lru_conv_w': dcw[None], 'lru_conv_b': dcb, 'lru_w_rgate': dwr[None], 'lru_b_rgate': dbr[:, 0][None],
             'lru_w_igate': dwi[None], 'lru_b_igate': dbi[:, 0][None], 'lru_lambda': dlam}
    return dhn, small


def _step(*args):
    assert len(args) == len(ARG_NAMES)
    p = dict(zip(ARG_NAMES, args))
    xi, yi, ci = _mesh_pos()
    dev = 4 * xi + 2 * yi + ci
    device = dev.astype(jnp.int32).reshape(1)
    core = ci.astype(jnp.int32).reshape(1)
    chip = (2 * xi + yi).astype(jnp.int32).reshape(1)
    x = p['x'][0]
    target = p['loss_target'][0]
    T = x.shape[0]
    tables = _ret_tables(T)

    sharded_small = [n for n in SMALL_NAMES if SMALL[n]]
    gathered, = _all_gather([_pack([p[n] for n in sharded_small])], name="gather_small")
    def in_cols_of(w):
        return w.shape[1] % LANE == 0

    gathers = {}
    token = gathered
    for layer in range(DEPTH):
        w_in_shard = p[MIXER_IN[layer]][0]
        groups = {'a': [(w_in_shard, in_cols_of(w_in_shard), _full_cols)],
                  'b': [(p[MIXER_OUT[layer]][0], False, _full_rows)],
                  'c': [(p['mlp_w_up'][layer], True, None), (p['mlp_w_down'][layer], False, _full_rows)]}
        for key, members in groups.items():
            shards = [w.astype(BF16) for w, _, _ in members]
            in_cols = [ic for _, ic, _ in members]
            n = len(shards)
            lands = [_insert_block(lax.empty((s.shape[0], N_DEV * s.shape[1]) if ic else (N_DEV,) + s.shape, BF16),
                                   s, device, ic, name=f"own_l{layer}{key}{i}")
                     for i, (s, ic) in enumerate(zip(shards, in_cols))]
            send, recv, bufs, token = _split_start(shards + lands, _plan_gather_a(in_cols), 4 * n,
                                                   name=f"gather_a_start_l{layer}{key}", deps=(token,))
            gathers[layer, key] = dict(n=n, h=(send, recv, bufs), in_cols=in_cols, full_of=[f for _, _, f in members])

    def gather_forward(layer, key, after):
        g = gathers[layer, key]
        n = g['n']
        send, recv, bufs = g['h']
        bufs = _split_wait(send, recv, bufs, _plan_gather_a(g['in_cols']), after, name=f"gather_a_wait_l{layer}{key}")
        send, recv, lands, tok = _split_start(bufs[n:], _plan_gather_b(g['in_cols']), 3 * n,
                                              name=f"gather_b_start_l{layer}{key}")
        g['h'] = (send, recv, lands)
        return tok

    def gather_finish(layer, key, after):
        g = gathers[layer, key]
        send, recv, lands = g['h']
        lands = _split_wait(send, recv, lands, _plan_gather_b(g['in_cols']), after, name=f"gather_b_wait_l{layer}{key}")
        return [l if ic else full_of(l) for l, ic, full_of in zip(lands, g['in_cols'], g['full_of'])]

    w_in_next, = gather_finish(0, 'a', gather_forward(0, 'a', token))
    parts = _unpack(gathered, [p[n].shape for n in sharded_small], lead=(N_DEV,))
    sm = {n: p[n] for n in SMALL_NAMES if not SMALL[n]}
    for n, blk in zip(sharded_small, parts):
        full = jnp.moveaxis(blk, 0, -2)
        sm[n] = full.reshape(full.shape[:-2] + (N_DEV * full.shape[-1],))

    saved = []
    big = {}
    for layer in range(DEPTH):
        w_in = w_in_next
        tag = f"l{layer}"
        hn = _rmsnorm_fwd(x, sm['norm1'][layer][None], name=tag + "_norm1")
        og, sv = _mixer_fwd(layer, hn, w_in, sm, tables)
        tok_b = gather_forward(layer, 'b', og)
        tok_c = gather_forward(layer, 'c', tok_b)
        w_out, = gather_finish(layer, 'b', tok_c)
        x_mid = _matmul(og, w_out, epi="add", extra=x, name=tag + "_out")
        hn2 = _rmsnorm_fwd(x_mid, sm['norm2'][layer][None], name=tag + "_norm2")
        w_up, w_down = gather_finish(layer, 'c', hn2)
        big[layer] = (w_in, w_out, w_up, w_down)
        u, a = _matmul(hn2, w_up, epi="relu2", name=tag + "_up")
        deps = (gather_forward(layer + 1, 'a', u),) if layer + 1 < DEPTH else ()
        x_new = _matmul(a, w_down, epi="add", extra=x_mid, name=tag + "_down", deps=deps)
        if layer + 1 < DEPTH:
            w_in_next, = gather_finish(layer + 1, 'a', x_new)
        saved.append(dict(x=x, hn=hn, og=og, sv=sv, x_mid=x_mid, hn2=hn2, u=u, a=a))
        x = x_new
    dy, loss_part = _final_loss(x, sm['final_norm'][None], target, name="final_loss")
    loss = lax.psum(loss_part[0, 0], ("x", "y", "c"))
    dx, dxb, dfinal = _rmsnorm_bwd(x, sm['final_norm'][None], dy, jnp.zeros_like(x), name="final_norm_bwd")

    outs = {}
    small_grads = {'final_norm': dfinal[0]}
    dnorm1, dnorm2 = [None] * DEPTH, [None] * DEPTH

    def scatter_start(items, deps, tag):
        grads, in_cols = [], []
        for _, _, dw, on in items:
            ic = on == 'cols' and (dw.shape[1] // N_DEV) % LANE == 0
            in_cols.append(ic)
            grads.append(dw if ic else (_blocks_cols(dw) if on == 'cols' else _blocks_rows(dw)))
        n = len(grads)
        lands = [lax.empty((N_CHIPS, dw.shape[0], dw.shape[1] // N_DEV) if ic else (N_CHIPS,) + dw.shape[1:], BF16)
                 for dw, ic in zip(grads, in_cols)]
        send, recv, bufs, tok = _split_start(grads + lands, _plan_scatter_pair(in_cols), N_CHIPS * n,
                                             name=f"scatter_pair_start_{tag}", deps=deps)
        return dict(items=items, n=n, tag=tag, in_cols=in_cols, h=(send, recv, bufs)), tok

    def scatter_forward(g, after):
        n, tag = g['n'], g['tag']
        send, recv, bufs = g['h']
        bufs = _split_wait(send, recv, bufs, _plan_scatter_pair(g['in_cols']), after, name=f"scatter_pair_wait_{tag}")
        sums = [_pair_add(b, a_, core, ic, name=f"pair_add_{tag}_{i}")
                for i, (b, a_, ic) in enumerate(zip(bufs[:n], bufs[n:], g['in_cols']))]
        lands = [lax.empty((N_CHIPS - 1,) + s_.shape[1:], BF16) for s_ in sums]
        send, recv, bufs, tok = _split_start(sums + lands, _plan_scatter_chips(n), (N_CHIPS - 1) * n,
                                             name=f"scatter_chips_start_{tag}")
        g['h'] = (send, recv, bufs)
        return tok

    def scatter_finish(g, after):
        n, tag = g['n'], g['tag']
        send, recv, bufs = g['h']
        bufs = _split_wait(send, recv, bufs, _plan_scatter_chips(n), after, name=f"scatter_chips_wait_{tag}")
        for i, (wname, idx, _, _) in enumerate(g['items']):
            outs[wname] = _adamw_sharded(p[wname], p['m_' + wname], p['v_' + wname], bufs[i], bufs[n + i], chip,
                                         idx, outs.get(wname), name=f"adamw_{tag}_{i}")

    older = []
    for layer in reversed(range(DEPTH)):
        w_in, w_out, w_up, w_down = big[layer]
        s = saved[layer]
        tag = f"l{layer}"
        du = _matmul(dxb, w_down, tb=True, epi="drelu2", extra=s['u'], out_dtype=BF16, name=tag + "_down_dx")
        dw_down = _matmul(s['a'], dxb, ta=True, out_dtype=BF16, name=tag + "_down_dw")
        dhn2 = _matmul(du, w_up, tb=True, name=tag + "_up_dx")
        dw_up = _matmul(s['hn2'], du, ta=True, out_dtype=BF16, name=tag + "_up_dw")
        dx, dxb, dn2 = _rmsnorm_bwd(s['x_mid'], sm['norm2'][layer][None], dhn2, dx, name=tag + "_norm2_bwd")
        mlp_group, tok = scatter_start([('mlp_w_up', layer, dw_up, 'cols'),
                                        ('mlp_w_down', layer, dw_down, 'rows')], (), f"mlp_l{layer}")
        dw_out = _matmul(s['og'], dxb, ta=True, out_dtype=BF16, name=tag + "_out_dw", deps=(tok,))
        tok = scatter_forward(mlp_group, dw_out)
        dog = _matmul(dxb, w_out, tb=True, name=tag + "_out_dx", deps=(tok,))
        started = []

        def on_dw(dw_in):
            group, tok_ = scatter_start([(MIXER_IN[layer], 0, dw_in, 'cols'),
                                         (MIXER_OUT[layer], 0, dw_out, 'rows')], (), f"mix_l{layer}")
            started.append(group)
            return (tok_,)

        dhn, sg = _mixer_bwd(layer, s['hn'], w_in, dog, s['sv'], tables, on_dw)
        mixer_group, = started
        small_grads.update(sg)
        if layer > 0:
            tok = scatter_forward(mixer_group, dhn)
            dx, dxb, dn1 = _rmsnorm_bwd(s['x'], sm['norm1'][layer][None], dhn, dx, name=tag + "_norm1_bwd", deps=(tok,))
        else:
            tok = gather_forward('small_grads', 'early', dhn)
            dx, dxb, dn1 = _rmsnorm_bwd(s['x'], sm['norm1'][layer][None], dhn, dx, name=tag + "_norm1_bwd", deps=(tok,))
        dnorm1[layer], dnorm2[layer] = dn1[0], dn2[0]
        for g in older:
            scatter_finish(g, dx)
        older = [mlp_group, mixer_group]
        if layer == 1:
            packed = _pack([small_grads[n] for n in EARLY_SMALL])
            land = _insert_block(lax.empty((N_DEV,) + packed.shape, F32), packed, device, False, name="own_small_grads")
            send, recv, bufs, _ = _split_start([packed, land], _plan_gather_a([False]), 4,
                                               name="gather_a_start_small_grads", deps=(dx,))
            gathers['small_grads', 'early'] = dict(n=1, h=(send, recv, bufs), in_cols=[False], full_of=[lambda l: l])
    small_grads['norm1'] = jnp.stack(dnorm1)
    small_grads['norm2'] = jnp.stack(dnorm2)

    late_parts, = _all_gather([_pack([small_grads[n] for n in LATE_SMALL])], name="gather_small_grads")
    last_token = scatter_forward(mixer_group, late_parts)
    early_parts, = gather_finish('small_grads', 'early', late_parts)
    early_sum = _sum_parts(early_parts, name="sum_small_grads_early", deps=(last_token,))
    late_sum = _sum_parts(late_parts, name="sum_small_grads_late")
    for g in older:
        scatter_finish(g, late_sum)
    by_name = dict(zip(EARLY_SMALL, _unpack(early_sum, [small_grads[n].shape for n in EARLY_SMALL])))
    by_name.update(zip(LATE_SMALL, _unpack(late_sum, [small_grads[n].shape for n in LATE_SMALL])))
    summed = [by_name[n] for n in SMALL_NAMES]
    local_g = []
    for n, g in zip(SMALL_NAMES, summed):
        if SMALL[n]:
            width = p[n].shape[-1]
            g = lax.dynamic_slice_in_dim(g, dev * width, width, axis=g.ndim - 1)
        local_g.append(g.reshape(p[n].shape))
    res = _adamw(_pack([p[n] for n in SMALL_NAMES]), _pack([p['m_' + n] for n in SMALL_NAMES]),
                 _pack([p['v_' + n] for n in SMALL_NAMES]), [_pack(local_g)], name="adamw_small")
    local_shapes = [p[n].shape for n in SMALL_NAMES]
    unpacked = [_unpack(r, local_shapes) for r in res]
    for i, n in enumerate(SMALL_NAMES):
        outs[n] = tuple(unpacked[k][i] for k in range(4))

    result = [loss, dx[None]]
    for k in range(4):
        result += [outs[n][k] for n in WEIGHT_NAMES]
    return tuple(result)


def kernel(x, norm1, norm2, final_norm, ret_w_in, ret_gn_gain, ret_w_out, gdn_w_in, gdn_conv_w, gdn_a_log, gdn_dt_bias, gdn_norm_gain, gdn_w_out, gla_w_in, gla_w_gate_up, gla_gate_bias, gla_norm_gain, gla_w_out, lru_w_in, lru_conv_w, lru_conv_b, lru_w_rgate, lru_b_rgate, lru_w_igate, lru_b_igate, lru_lambda, lru_w_out, mlp_w_up, mlp_w_down, loss_target, m_norm1, m_norm2, m_final_norm, m_ret_w_in, m_ret_gn_gain, m_ret_w_out, m_gdn_w_in, m_gdn_conv_w, m_gdn_a_log, m_gdn_dt_bias, m_gdn_norm_gain, m_gdn_w_out, m_gla_w_in, m_gla_w_gate_up, m_gla_gate_bias, m_gla_norm_gain, m_gla_w_out, m_lru_w_in, m_lru_conv_w, m_lru_conv_b, m_lru_w_rgate, m_lru_b_rgate, m_lru_w_igate, m_lru_b_igate, m_lru_lambda, m_lru_w_out, m_mlp_w_up, m_mlp_w_down, v_norm1, v_norm2, v_final_norm, v_ret_w_in, v_ret_gn_gain, v_ret_w_out, v_gdn_w_in, v_gdn_conv_w, v_gdn_a_log, v_gdn_dt_bias, v_gdn_norm_gain, v_gdn_w_out, v_gla_w_in, v_gla_w_gate_up, v_gla_gate_bias, v_gla_norm_gain, v_gla_w_out, v_lru_w_in, v_lru_conv_w, v_lru_conv_b, v_lru_w_rgate, v_lru_b_rgate, v_lru_w_igate, v_lru_b_igate, v_lru_lambda, v_lru_w_out, v_mlp_w_up, v_mlp_w_down):
    return _step(x, norm1, norm2, final_norm, ret_w_in, ret_gn_gain, ret_w_out, gdn_w_in, gdn_conv_w, gdn_a_log, gdn_dt_bias, gdn_norm_gain, gdn_w_out, gla_w_in, gla_w_gate_up, gla_gate_bias, gla_norm_gain, gla_w_out, lru_w_in, lru_conv_w, lru_conv_b, lru_w_rgate, lru_b_rgate, lru_w_igate, lru_b_igate, lru_lambda, lru_w_out, mlp_w_up, mlp_w_down, loss_target, m_norm1, m_norm2, m_final_norm, m_ret_w_in, m_ret_gn_gain, m_ret_w_out, m_gdn_w_in, m_gdn_conv_w, m_gdn_a_log, m_gdn_dt_bias, m_gdn_norm_gain, m_gdn_w_out, m_gla_w_in, m_gla_w_gate_up, m_gla_gate_bias, m_gla_norm_gain, m_gla_w_out, m_lru_w_in, m_lru_conv_w, m_lru_conv_b, m_lru_w_rgate, m_lru_b_rgate, m_lru_w_igate, m_lru_b_igate, m_lru_lambda, m_lru_w_out, m_mlp_w_up, m_mlp_w_down, v_norm1, v_norm2, v_final_norm, v_ret_w_in, v_ret_gn_gain, v_ret_w_out, v_gdn_w_in, v_gdn_conv_w, v_gdn_a_log, v_gdn_dt_bias, v_gdn_norm_gain, v_gdn_w_out, v_gla_w_in, v_gla_w_gate_up, v_gla_gate_bias, v_gla_norm_gain, v_gla_w_out, v_lru_w_in, v_lru_conv_w, v_lru_conv_b, v_lru_w_rgate, v_lru_b_rgate, v_lru_w_igate, v_lru_b_igate, v_lru_lambda, v_lru_w_out, v_mlp_w_up, v_mlp_w_down)
```

```python
import functools
import math

import numpy as np
import jax
import jax.numpy as jnp
from jax import lax
from jax.experimental import pallas as pl
from jax.experimental.pallas import tpu as pltpu

F32 = jnp.float32
BF16 = jnp.bfloat16

D_MODEL = 2048
DEPTH = 4
CHUNK = 64
D_FF = 4 * D_MODEL
NORM_EPS = 1e-6
N_DEV = 8

RET_HEADS, RET_DK, RET_DV = 8, 256, 512
RET_HB = 4
GDN_HEADS, GDN_DK, GDN_DV = 16, 128, 128
GDN_HB = 8
GDN_QKV = GDN_HEADS * (2 * GDN_DK + GDN_DV)
CONV_WIDTH = 4
GLA_HEADS, GLA_DK, GLA_DV = 4, 256, 512
GLA_HB = 4
GLA_GATE_RANK = 16
GLA_TAU = 16.0
LRU_WIDTH, LRU_BLOCKS, LRU_BLOCK = 2048, 16, 128
LRU_C = 8.0
ROPE_BASE = 10000.0

ADAM_LR, ADAM_B1, ADAM_B2, ADAM_EPS, ADAM_WD, ADAM_STEP = 0.001, 0.9, 0.999, 1e-08, 0.01, 10

LANE = 128
VMEM_LIMIT = 48 * 1024 * 1024

NN = (((1,), (0,)), ((), ()))
NT = (((1,), (1,)), ((), ()))
TN = (((0,), (0,)), ((), ()))


def _params(sem=None):
    return pltpu.CompilerParams(dimension_semantics=sem, vmem_limit_bytes=VMEM_LIMIT)


def _bdot(a, b, dn=NN):
    return lax.dot_general(a.astype(BF16), b.astype(BF16), dn, preferred_element_type=F32)


def _split(x):
    hi = x.astype(BF16)
    lo = (x - hi.astype(F32)).astype(BF16)
    return hi, lo


def _fdot(a, b, dn=NN):
    a1, a2 = _split(a)
    b1, b2 = _split(b)
    d = functools.partial(lax.dot_general, dimension_numbers=dn, preferred_element_type=F32)
    return d(a1, b1) + (d(a1, b2) + d(a2, b1))


def _sigmoid(x):
    return 1.0 / (1.0 + jnp.exp(-x))


def _softplus(x):
    return jnp.maximum(x, 0.0) + jnp.log(1.0 + jnp.exp(-jnp.abs(x)))


def _silu(x):
    return x * _sigmoid(x)


def _dsilu(x):
    s = _sigmoid(x)
    return s * (1.0 + x * (1.0 - s))


GELU_C = math.sqrt(2.0 / math.pi)


def _gelu(x):
    return 0.5 * x * (1.0 + jnp.tanh(GELU_C * (x + 0.044715 * x * x * x)))


def _dgelu(x):
    t = jnp.tanh(GELU_C * (x + 0.044715 * x * x * x))
    return 0.5 * (1.0 + t) + 0.5 * x * (1.0 - t * t) * GELU_C * (1.0 + 3.0 * 0.044715 * x * x)


def _expm1(x):
    poly = x * (1.0 + x * 0.5 * (1.0 + x * (1.0 / 3.0) * (1.0 + x * 0.25 * (1.0 + x * 0.2))))
    return jnp.where(jnp.abs(x) < 0.05, poly, jnp.exp(x) - 1.0)


def _iota2(shape, axis):
    return lax.broadcasted_iota(jnp.int32, shape, axis)


def _col_to_row(col):
    n = col.shape[0]
    eye = _iota2((n, n), 0) == _iota2((n, n), 1)
    return jnp.sum(jnp.where(eye, col, 0.0), axis=0, keepdims=True)


def _row_to_col(row):
    n = row.shape[1]
    eye = _iota2((n, n), 0) == _iota2((n, n), 1)
    return jnp.sum(jnp.where(eye, row, 0.0), axis=1, keepdims=True)


def _pick_row(x, r):
    rows = _iota2(x.shape, 0)
    return jnp.sum(jnp.where(rows == r, x, 0.0), axis=0, keepdims=True)


def _shift_down(x, s):
    if s == 0:
        return x
    y = pltpu.roll(x, s, 0)
    return jnp.where(_iota2(x.shape, 0) < s, 0.0, y)


def _shift_up(x, s):
    if s == 0:
        return x
    n = x.shape[0]
    y = pltpu.roll(x, n - s, 0)
    return jnp.where(_iota2(x.shape, 0) >= n - s, 0.0, y)


def _tile(dim, prefs):
    for p in prefs:
        if dim % p == 0:
            return p
    return dim


TOKEN_SHAPE = (8, LANE)


def _dep_specs(deps):
    return [pl.BlockSpec(TOKEN_SHAPE, lambda *_: (0, 0)) for _ in deps]


def _matmul(a, b, *, ta=False, tb=False, epi="none", extra=None, out_dtype=F32, name, deps=()):
    if ta:
        K, M = a.shape
    else:
        M, K = a.shape
    if tb:
        N, K2 = b.shape
    else:
        K2, N = b.shape
    assert K == K2, (a.shape, b.shape, ta, tb)
    if K <= 2048:
        tk = K
        if N <= 2048:
            tm, tn = _tile(M, (512, 256, 128)), N
        else:
            tm, tn = _tile(M, (1024, 512, 256, 128)), _tile(N, (512, 256, 128))
    else:
        tm, tn, tk = (_tile(d, (1024, 512, 256, 128)) for d in (M, N, K))
    nk = K // tk
    dn = (((0 if ta else 1,), (1 if tb else 0,)), ((), ()))
    n_extra = 0 if extra is None else 1
    n_out = 2 if epi == "relu2" else 1

    def body(*refs):
        a_ref, b_ref = refs[0], refs[1]
        e_ref = refs[2] if n_extra else None
        outs = refs[2 + n_extra + len(deps):2 + n_extra + len(deps) + n_out]

        def finish(r):
            if epi == "none":
                outs[0][...] = r.astype(outs[0].dtype)
            elif epi == "add":
                outs[0][...] = (r + e_ref[...]).astype(outs[0].dtype)
            elif epi == "relu2":
                outs[0][...] = r
                p = jnp.maximum(r, 0.0)
                outs[1][...] = (p * p).astype(outs[1].dtype)
            elif epi == "drelu2":
                outs[0][...] = (r * 2.0 * jnp.maximum(e_ref[...], 0.0)).astype(outs[0].dtype)

        def product():
            return lax.dot_general(a_ref[...].astype(BF16), b_ref[...].astype(BF16), dn, preferred_element_type=F32)

        if nk == 1:
            finish(product())
            return
        acc = refs[-1]
        k = pl.program_id(2)

        @pl.when(k == 0)
        def _():
            acc[...] = jnp.zeros_like(acc)

        acc[...] += product()

        @pl.when(k == nk - 1)
        def _():
            finish(acc[...])

    a_spec = pl.BlockSpec((tk, tm), lambda i, j, k: (k, i)) if ta else pl.BlockSpec((tm, tk), lambda i, j, k: (i, k))
    b_spec = pl.BlockSpec((tn, tk), lambda i, j, k: (j, k)) if tb else pl.BlockSpec((tk, tn), lambda i, j, k: (k, j))
    o_spec = pl.BlockSpec((tm, tn), lambda i, j, k: (i, j))
    in_specs = [a_spec, b_spec] + ([o_spec] if n_extra else []) + _dep_specs(deps)
    if epi == "relu2":
        out_shape = (jax.ShapeDtypeStruct((M, N), F32), jax.ShapeDtypeStruct((M, N), BF16))
        out_specs = (o_spec, o_spec)
    else:
        out_shape = jax.ShapeDtypeStruct((M, N), out_dtype)
        out_specs = o_spec
    args = (a, b) + ((extra,) if n_extra else ()) + tuple(deps)
    return pl.pallas_call(
        body, name=name, grid=(M // tm, N // tn, nk), in_specs=in_specs, out_specs=out_specs,
        out_shape=out_shape, scratch_shapes=[pltpu.VMEM((tm, tn), F32)] if nk > 1 else [],
        compiler_params=_params(("parallel", "parallel", "arbitrary")),
    )(*args)


ROW_BLOCK = 256


def _rmsnorm_fwd(x, g, *, name, deps=()):
    T, D = x.shape
    tr = _tile(T, (ROW_BLOCK, 128, 64))

    def body(x_ref, g_ref, *rest):
        o_ref = rest[-1]
        xv = x_ref[...]
        r = lax.rsqrt(jnp.mean(xv * xv, axis=-1, keepdims=True) + NORM_EPS)
        o_ref[...] = (xv * r * g_ref[...]).astype(o_ref.dtype)

    return pl.pallas_call(
        body, name=name, grid=(T // tr,),
        in_specs=[pl.BlockSpec((tr, D), lambda i: (i, 0)), pl.BlockSpec((1, D), lambda i: (0, 0))] + _dep_specs(deps),
        out_specs=pl.BlockSpec((tr, D), lambda i: (i, 0)),
        out_shape=jax.ShapeDtypeStruct((T, D), BF16), compiler_params=_params(("parallel",)),
    )(x, g, *deps)


def _rmsnorm_bwd(x, g, dy, dres, *, name, deps=()):
    T, D = x.shape
    tr = _tile(T, (ROW_BLOCK, 128, 64))

    def body(x_ref, g_ref, dy_ref, dres_ref, *rest):
        dx_ref, dxb_ref, dg_ref = rest[len(deps):]
        i = pl.program_id(0)
        xv = x_ref[...]
        r = lax.rsqrt(jnp.mean(xv * xv, axis=-1, keepdims=True) + NORM_EPS)
        xh = xv * r
        dyv = dy_ref[...].astype(F32)
        dxh = dyv * g_ref[...]
        dx = dres_ref[...] + r * (dxh - xh * jnp.mean(dxh * xh, axis=-1, keepdims=True))
        dx_ref[...] = dx
        dxb_ref[...] = dx.astype(dxb_ref.dtype)

        @pl.when(i == 0)
        def _():
            dg_ref[...] = jnp.zeros_like(dg_ref)

        dg_ref[...] += jnp.sum(dyv * xh, axis=0, keepdims=True)

    blk = pl.BlockSpec((tr, D), lambda i: (i, 0))
    vec = pl.BlockSpec((1, D), lambda i: (0, 0))
    return pl.pallas_call(
        body, name=name, grid=(T // tr,), in_specs=[blk, vec, blk, blk] + _dep_specs(deps), out_specs=(blk, blk, vec),
        out_shape=(jax.ShapeDtypeStruct((T, D), F32), jax.ShapeDtypeStruct((T, D), BF16),
                   jax.ShapeDtypeStruct((1, D), F32)),
        compiler_params=_params(("arbitrary",)),
    )(x, g, dy, dres, *deps)


def _final_loss(x, g, target, *, name):
    T, D = x.shape
    tr = _tile(T, (ROW_BLOCK, 128, 64))

    def body(x_ref, g_ref, t_ref, dy_ref, l_ref):
        i = pl.program_id(0)
        xv = x_ref[...]
        r = lax.rsqrt(jnp.mean(xv * xv, axis=-1, keepdims=True) + NORM_EPS)
        err = xv * r * g_ref[...] - t_ref[...]
        dy_ref[...] = err * (1.0 / D)

        @pl.when(i == 0)
        def _():
            l_ref[...] = jnp.zeros_like(l_ref)

        part = 0.5 * jnp.sum(jnp.mean(err * err, axis=-1, keepdims=True), axis=0, keepdims=True)
        l_ref[...] += jnp.broadcast_to(part, l_ref.shape)

    blk = pl.BlockSpec((tr, D), lambda i: (i, 0))
    vec = pl.BlockSpec((1, D), lambda i: (0, 0))
    return pl.pallas_call(
        body, name=name, grid=(T // tr,), in_specs=[blk, vec, blk],
        out_specs=(blk, pl.BlockSpec((1, LANE), lambda i: (0, 0))),
        out_shape=(jax.ShapeDtypeStruct((T, D), F32), jax.ShapeDtypeStruct((1, LANE), F32)),
        compiler_params=_params(("arbitrary",)),
    )(x, g, target)


def _colsum(x, *, name):
    T, C = x.shape
    tc = _tile(C, (512, 256, 128))

    def body(x_ref, o_ref):
        o_ref[...] = jnp.sum(x_ref[...], axis=0, keepdims=True)

    return pl.pallas_call(
        body, name=name, grid=(C // tc,), in_specs=[pl.BlockSpec((T, tc), lambda j: (0, j))],
        out_specs=pl.BlockSpec((1, tc), lambda j: (0, j)),
        out_shape=jax.ShapeDtypeStruct((1, C), F32), compiler_params=_params(("parallel",)),
    )(x)


ADAM_C1 = 1.0 - ADAM_B1 ** ADAM_STEP
ADAM_C2 = 1.0 - ADAM_B2 ** ADAM_STEP


def _adamw(w, m, v, grads, *, name):
    R, C = w.shape
    tr = _tile(R, (256, 128, 64, 32, 16, 8))
    n_g = len(grads)

    def body(*refs):
        w_ref, m_ref, v_ref = refs[:3]
        g_refs = refs[3:3 + n_g]
        g_out, d_out, m_out, v_out = refs[3 + n_g:]
        g = g_refs[0][...].astype(F32)
        for r in g_refs[1:]:
            g = g + r[...].astype(F32)
        mn = ADAM_B1 * m_ref[...] + (1.0 - ADAM_B1) * g
        vn = ADAM_B2 * v_ref[...] + (1.0 - ADAM_B2) * (g * g)
        m_hat = mn / ADAM_C1
        v_hat = vn / ADAM_C2
        g_out[...] = g
        d_out[...] = -ADAM_LR * (m_hat / (jnp.sqrt(v_hat) + ADAM_EPS) + ADAM_WD * w_ref[...])
        m_out[...] = mn
        v_out[...] = vn

    blk = pl.BlockSpec((tr, C), lambda i: (i, 0))
    sh = jax.ShapeDtypeStruct((R, C), F32)
    return pl.pallas_call(
        body, name=name, grid=(R // tr,), in_specs=[blk] * (3 + n_g), out_specs=(blk,) * 4,
        out_shape=(sh,) * 4, compiler_params=_params(("parallel",)),
    )(w, m, v, *grads)


def _sum_parts(parts, *, name, deps=()):
    P, R, C = parts.shape
    tr = _tile(R, (256, 128, 64, 32, 16, 8))

    def body(p_ref, *rest):
        o_ref = rest[-1]
        s = p_ref[0].astype(F32)
        for i in range(1, P):
            s = s + p_ref[i].astype(F32)
        o_ref[...] = s

    return pl.pallas_call(
        body, name=name, grid=(R // tr,),
        in_specs=[pl.BlockSpec((P, tr, C), lambda i: (0, i, 0))] + _dep_specs(deps),
        out_specs=pl.BlockSpec((tr, C), lambda i: (i, 0)),
        out_shape=jax.ShapeDtypeStruct((R, C), F32), compiler_params=_params(("parallel",)),
    )(parts, *deps)


def _ret_tables(T):
    H, C = RET_HEADS, CHUNK
    log_gamma = jnp.log1p(-jnp.exp2(-5.0 - jnp.arange(H, dtype=F32)))
    pos = jnp.arange(C, dtype=F32)
    dist = jnp.abs(pos[:, None] - pos[None, :])
    dm = jnp.exp(log_gamma[:, None, None] * dist)
    qdec = jnp.exp(log_gamma[:, None] * (pos + 1.0))[:, :, None]
    kdec = jnp.exp(log_gamma[:, None] * (C - 1.0 - pos))[:, :, None]
    cdec = jnp.exp(log_gamma * C)[:, None, None]
    inv = ROPE_BASE ** (-jnp.arange(0, RET_DK, 2, dtype=F32) / RET_DK)
    ang = jnp.arange(T, dtype=F32)[:, None] * inv[None, :]
    return dm, qdec, kdec, cdec, jnp.cos(ang), jnp.sin(ang)


def _rot(x, cos, sin):
    h = x.shape[1] // 2
    x1, x2 = x[:, :h], x[:, h:]
    return jnp.concatenate([x1 * cos - x2 * sin, x1 * sin + x2 * cos], axis=1)


def _unrot(dy, cos, sin):
    h = dy.shape[1] // 2
    d1, d2 = dy[:, :h], dy[:, h:]
    return jnp.concatenate([d1 * cos + d2 * sin, d2 * cos - d1 * sin], axis=1)


def _ret_specs(N, rev):
    H, C, DK, DV = RET_HEADS, CHUNK, RET_DK, RET_DV
    cn = (lambda n: N - 1 - n) if rev else (lambda n: n)
    HB, G = RET_HB, H // RET_HB
    q = pl.BlockSpec((C, HB * DK), lambda h, n: (cn(n), h))
    k = pl.BlockSpec((C, HB * DK), lambda h, n: (cn(n), G + h))
    v = pl.BlockSpec((C, HB * DV), lambda h, n: (cn(n), G + h))
    g = pl.BlockSpec((C, HB * DV), lambda h, n: (cn(n), 2 * G + h))
    cs = pl.BlockSpec((C, DK // 2), lambda h, n: (cn(n), 0))
    dm = pl.BlockSpec((HB, C, C), lambda h, n: (h, 0, 0))
    dec = pl.BlockSpec((HB, C, 1), lambda h, n: (h, 0, 0))
    cd = pl.BlockSpec((HB, 1, 1), lambda h, n: (h, 0, 0))
    gain = pl.BlockSpec((HB, 1, DV), lambda h, n: (h, 0, 0))
    st = pl.BlockSpec((HB, 1, DK, DV), lambda h, n: (h, cn(n), 0, 0))
    ov = pl.BlockSpec((C, HB * DV), lambda h, n: (cn(n), h))
    return q, k, v, g, cs, dm, dec, cd, gain, st, ov


def _ret_fwd(proj, gain, tables, *, name):
    T = proj.shape[0]
    H, C, DK, DV = RET_HEADS, CHUNK, RET_DK, RET_DV
    N = T // C
    dm_t, qdec_t, kdec_t, cdec_t, cos_t, sin_t = tables

    def body(q_ref, k_ref, v_ref, g_ref, cos_ref, sin_ref, dm_ref, qd_ref, kd_ref, cd_ref, gain_ref,
             og_ref, st_ref, s_acc):
        n = pl.program_id(1)

        @pl.when(n == 0)
        def _():
            s_acc[...] = jnp.zeros_like(s_acc)

        cos, sin = cos_ref[...], sin_ref[...]

        def head(hb):
            kc, vc = pl.ds(hb * DK, DK), pl.ds(hb * DV, DV)
            qr = _rot(q_ref[:, kc], cos, sin)
            kr = _rot(k_ref[:, kc], cos, sin) * (DK ** -0.5)
            v = v_ref[:, vc]
            sp = s_acc[hb]
            st_ref[hb, 0] = sp.astype(st_ref.dtype)
            scores = _bdot(qr, kr, NT) * dm_ref[hb]
            inter = _bdot(qr * qd_ref[hb], sp)
            s_acc[hb] = sp * cd_ref[hb] + _bdot(kr * kd_ref[hb], v, TN)
            yield
            o = _bdot(scores, v) + inter
            yield
            oc = o - jnp.mean(o, axis=-1, keepdims=True)
            rstd = lax.rsqrt(jnp.mean(oc * oc, axis=-1, keepdims=True) + NORM_EPS)
            og_ref[:, vc] = (oc * rstd * gain_ref[hb] * _silu(g_ref[:, vc])).astype(og_ref.dtype)

        _interleave([head(hb) for hb in range(RET_HB)])

    q, k, v, g, cs, dm, dec, cd, gn, st, ov = _ret_specs(N, False)
    return pl.pallas_call(
        body, name=name, grid=(H // RET_HB, N),
        in_specs=[q, k, v, g, cs, cs, dm, dec, dec, cd, gn], out_specs=(ov, st),
        out_shape=(jax.ShapeDtypeStruct((T, H * DV), BF16), jax.ShapeDtypeStruct((H, N, DK, DV), BF16)),
        scratch_shapes=[pltpu.VMEM((RET_HB, DK, DV), F32)],
        compiler_params=_params(("arbitrary", "arbitrary")),
    )(proj, proj, proj, proj, cos_t, sin_t, dm_t, qdec_t, kdec_t, cdec_t, gain)


def _ret_bwd(proj, gain, tables, states, dog, *, name):
    T = proj.shape[0]
    H, C, DK, DV = RET_HEADS, CHUNK, RET_DK, RET_DV
    N = T // C
    dm_t, qdec_t, kdec_t, cdec_t, cos_t, sin_t = tables

    def body(q_ref, k_ref, v_ref, g_ref, cos_ref, sin_ref, dm_ref, qd_ref, kd_ref, cd_ref, gain_ref,
             st_ref, dog_ref, dq_ref, dk_ref, dv_ref, dg_ref, dgain_ref, ds_acc):
        n = pl.program_id(1)

        @pl.when(n == 0)
        def _():
            ds_acc[...] = jnp.zeros_like(ds_acc)
            dgain_ref[...] = jnp.zeros_like(dgain_ref)

        cos, sin = cos_ref[...], sin_ref[...]
        scale = DK ** -0.5

        def head(hb):
            kc, vc = pl.ds(hb * DK, DK), pl.ds(hb * DV, DV)
            qr = _rot(q_ref[:, kc], cos, sin)
            kr = _rot(k_ref[:, kc], cos, sin) * scale
            v = v_ref[:, vc]
            g = g_ref[:, vc]
            sp = st_ref[hb, 0]
            dm = dm_ref[hb]
            qd, kd = qd_ref[hb], kd_ref[hb]
            gain_v = gain_ref[hb]
            scores = _bdot(qr, kr, NT) * dm
            inter = _bdot(qr * qd, sp)
            yield
            o = _bdot(scores, v) + inter
            yield
            oc = o - jnp.mean(o, axis=-1, keepdims=True)
            rstd = lax.rsqrt(jnp.mean(oc * oc, axis=-1, keepdims=True) + NORM_EPS)
            oh = oc * rstd
            dy = dog_ref[:, vc].astype(F32)
            dg_ref[:, vc] = (dy * oh * gain_v * _dsilu(g)).astype(dg_ref.dtype)
            dnorm = dy * _silu(g)
            dgain_ref[hb] += jnp.sum(dnorm * oh, axis=0, keepdims=True)
            doh = dnorm * gain_v
            do = rstd * (doh - jnp.mean(doh, axis=-1, keepdims=True)
                         - oh * jnp.mean(doh * oh, axis=-1, keepdims=True))
            dsn = ds_acc[hb]
            dp = _bdot(do, v, NT) * dm
            dq_inter = _bdot(do, sp, NT) * qd
            dk_inter = _bdot(v, dsn, NT) * kd
            dv_ref[:, vc] = (_bdot(scores, do, TN) + _bdot(kr * kd, dsn)).astype(dv_ref.dtype)
            ds_acc[hb] = dsn * cd_ref[hb] + _bdot(qr * qd, do, TN)
            yield
            dqr = _bdot(dp, kr) + dq_inter
            dkr = _bdot(dp, qr, TN) + dk_inter
            yield
            dq_ref[:, kc] = _unrot(dqr, cos, sin).astype(dq_ref.dtype)
            dk_ref[:, kc] = _unrot(dkr * scale, cos, sin).astype(dk_ref.dtype)

        _interleave([head(hb) for hb in range(RET_HB)])

    q, k, v, g, cs, dm, dec, cd, gn, st, ov = _ret_specs(N, True)
    return pl.pallas_call(
        body, name=name, grid=(H // RET_HB, N),
        in_specs=[q, k, v, g, cs, cs, dm, dec, dec, cd, gn, st, ov],
        out_specs=(q, q, ov, ov, gn),
        out_shape=(jax.ShapeDtypeStruct((T, H * DK), BF16), jax.ShapeDtypeStruct((T, H * DK), BF16),
                   jax.ShapeDtypeStruct((T, H * DV), BF16), jax.ShapeDtypeStruct((T, H * DV), BF16),
                   jax.ShapeDtypeStruct((H, 1, DV), F32)),
        scratch_shapes=[pltpu.VMEM((RET_HB, DK, DV), F32)],
        compiler_params=_params(("arbitrary", "arbitrary")),
    )(proj, proj, proj, proj, cos_t, sin_t, dm_t, qdec_t, kdec_t, cdec_t, gain, states, dog)


def _gla_specs(N, rev):
    H, C, DK, DV = GLA_HEADS, CHUNK, GLA_DK, GLA_DV
    cn = (lambda n: N - 1 - n) if rev else (lambda n: n)
    HB, G = GLA_HB, H // GLA_HB
    q = pl.BlockSpec((C, HB * DK), lambda h, n: (cn(n), h))
    k = pl.BlockSpec((C, HB * DK), lambda h, n: (cn(n), G + h))
    v = pl.BlockSpec((C, HB * DV), lambda h, n: (cn(n), G + h))
    r = pl.BlockSpec((C, HB * DV), lambda h, n: (cn(n), 2 * G + h))
    bias = pl.BlockSpec((1, HB * DK), lambda h, n: (0, h))
    gain = pl.BlockSpec((HB, 1, DV), lambda h, n: (h, 0, 0))
    st = pl.BlockSpec((HB, 1, DV, DK), lambda h, n: (h, cn(n), 0, 0))
    ov = pl.BlockSpec((C, HB * DV), lambda h, n: (cn(n), h))
    return q, k, v, r, bias, gain, st, ov


def _gla_chunk(q, k, v, gl_raw, bias):
    C, DK = q.shape
    gl = gl_raw + bias
    la = -_softplus(-gl) * (1.0 / GLA_TAU)
    lower = _iota2((C, C), 0) >= _iota2((C, C), 1)
    cum = _fdot(jnp.where(lower, 1.0, 0.0), la)
    yield
    ref = _pick_row(cum, C // 2 - 1)
    clast = _pick_row(cum, C - 1)
    fw, bw = jnp.exp(cum - ref), jnp.exp(ref - cum)
    qs = q * (DK ** -0.5)
    s_lo = _bdot(qs * fw, k * bw, NT)
    s_up = _bdot(qs * bw, k * fw, NT)
    yield
    scores = jnp.where(lower, s_lo, s_up)
    return gl, cum, clast, fw, bw, qs, k, v, scores, lower


def _gla_fwd(proj, glogit, bias, gain, *, name):
    T = proj.shape[0]
    H, C, DK, DV = GLA_HEADS, CHUNK, GLA_DK, GLA_DV
    N = T // C

    def body(q_ref, k_ref, v_ref, r_ref, gl_ref, bias_ref, gain_ref, og_ref, st_ref, s_acc):
        n = pl.program_id(1)

        @pl.when(n == 0)
        def _():
            s_acc[...] = jnp.zeros_like(s_acc)

        def head(hb):
            kc, vc = pl.ds(hb * DK, DK), pl.ds(hb * DV, DV)
            gl, cum, clast, fw, bw, qs, k, v, scores, lower = yield from _gla_chunk(
                q_ref[:, kc], k_ref[:, kc], v_ref[:, vc], gl_ref[:, kc], bias_ref[:, kc])
            sp = s_acc[hb]
            st_ref[hb, 0] = sp
            o = _bdot(scores, v) + _bdot(qs * jnp.exp(cum), sp, NT)
            s_acc[hb] = sp * jnp.exp(clast) + _bdot(v, k * jnp.exp(clast - cum), TN)
            yield
            rstd = lax.rsqrt(jnp.mean(o * o, axis=-1, keepdims=True) + NORM_EPS)
            og_ref[:, vc] = (o * rstd * gain_ref[hb] * _silu(r_ref[:, vc])).astype(og_ref.dtype)

        _interleave([head(hb) for hb in range(GLA_HB)])

    q, k, v, r, bias_s, gn, st, ov = _gla_specs(N, False)
    return pl.pallas_call(
        body, name=name, grid=(H // GLA_HB, N), in_specs=[q, k, v, r, q, bias_s, gn], out_specs=(ov, st),
        out_shape=(jax.ShapeDtypeStruct((T, H * DV), BF16), jax.ShapeDtypeStruct((H, N, DV, DK), F32)),
        scratch_shapes=[pltpu.VMEM((GLA_HB, DV, DK), F32)],
        compiler_params=_params(("arbitrary", "arbitrary")),
    )(proj, proj, proj, proj, glogit, bias, gain)


def _gla_bwd(proj, glogit, bias, gain, states, dog, *, name):
    T = proj.shape[0]
    H, C, DK, DV = GLA_HEADS, CHUNK, GLA_DK, GLA_DV
    N = T // C

    def body(q_ref, k_ref, v_ref, r_ref, gl_ref, bias_ref, gain_ref, st_ref, dog_ref,
             dq_ref, dk_ref, dv_ref, dr_ref, dgl_ref, dgain_ref, ds_acc):
        n = pl.program_id(1)

        @pl.when(n == 0)
        def _():
            ds_acc[...] = jnp.zeros_like(ds_acc)
            dgain_ref[...] = jnp.zeros_like(dgain_ref)

        def head(hb):
            kc, vc = pl.ds(hb * DK, DK), pl.ds(hb * DV, DV)
            gl, cum, clast, fw, bw, qs, k, v, scores, lower = yield from _gla_chunk(
                q_ref[:, kc], k_ref[:, kc], v_ref[:, vc], gl_ref[:, kc], bias_ref[:, kc])
            sp = st_ref[hb, 0]
            ecum, e2, cdec = jnp.exp(cum), jnp.exp(clast - cum), jnp.exp(clast)
            q_in, k_end = qs * ecum, k * e2
            o = _bdot(scores, v) + _bdot(q_in, sp, NT)
            yield
            rstd = lax.rsqrt(jnp.mean(o * o, axis=-1, keepdims=True) + NORM_EPS)
            oh = o * rstd
            r = r_ref[:, vc]
            gain_v = gain_ref[hb]
            dy = dog_ref[:, vc].astype(F32)
            dr_ref[:, vc] = (dy * oh * gain_v * _dsilu(r)).astype(dr_ref.dtype)
            dnorm = dy * _silu(r)
            dgain_ref[hb] += jnp.sum(dnorm * oh, axis=0, keepdims=True)
            doh = dnorm * gain_v
            do = rstd * (doh - oh * jnp.mean(doh * oh, axis=-1, keepdims=True))
            dsn = ds_acc[hb]
            dq_in = _bdot(do, sp)
            dk_end = _bdot(v, dsn)
            dv_ref[:, vc] = (_bdot(k_end, dsn, NT) + _bdot(scores, do, TN)).astype(dv_ref.dtype)
            dcdec = jnp.sum(dsn * sp, axis=0, keepdims=True)
            ds_acc[hb] = dsn * cdec + _bdot(do, q_in, TN)
            dsc = _bdot(do, v, NT)
            yield
            ds_lo = jnp.where(lower, dsc, 0.0)
            ds_up = jnp.where(lower, 0.0, dsc)
            qf, kb, qb, kf = qs * fw, k * bw, qs * bw, k * fw
            dqf, dkb = _bdot(ds_lo, kb), _bdot(ds_lo, qf, TN)
            dqb, dkf = _bdot(ds_up, kf), _bdot(ds_up, qb, TN)
            yield
            dq_ref[:, kc] = ((dqf * fw + dqb * bw + dq_in * ecum) * (DK ** -0.5)).astype(dq_ref.dtype)
            dk_ref[:, kc] = (dkb * bw + dkf * fw + dk_end * e2).astype(dk_ref.dtype)
            dz = (dqf * qs + dkf * k) * fw - (dqb * qs + dkb * k) * bw
            kk = dk_end * k_end
            dcum = dz + dq_in * q_in - kk
            rows = _iota2((C, DK), 0)
            dcum = dcum + jnp.where(rows == C // 2 - 1, -jnp.sum(dz, axis=0, keepdims=True), 0.0)
            dcum = dcum + jnp.where(rows == C - 1, jnp.sum(kk, axis=0, keepdims=True) + dcdec * cdec, 0.0)
            upper = _iota2((C, C), 0) <= _iota2((C, C), 1)
            dla = _fdot(jnp.where(upper, 1.0, 0.0), dcum)
            yield
            dgl_ref[:, kc] = dla * (1.0 / GLA_TAU) * _sigmoid(-gl)

        _interleave([head(hb) for hb in range(GLA_HB)])

    q, k, v, r, bias_s, gn, st, ov = _gla_specs(N, True)
    return pl.pallas_call(
        body, name=name, grid=(H // GLA_HB, N), in_specs=[q, k, v, r, q, bias_s, gn, st, ov],
        out_specs=(q, q, ov, ov, q, gn),
        out_shape=(jax.ShapeDtypeStruct((T, H * DK), BF16), jax.ShapeDtypeStruct((T, H * DK), BF16),
                   jax.ShapeDtypeStruct((T, H * DV), BF16), jax.ShapeDtypeStruct((T, H * DV), BF16),
                   jax.ShapeDtypeStruct((T, H * DK), F32), jax.ShapeDtypeStruct((H, 1, DV), F32)),
        scratch_shapes=[pltpu.VMEM((GLA_HB, DV, DK), F32)],
        compiler_params=_params(("arbitrary", "arbitrary")),
    )(proj, proj, proj, proj, glogit, bias, gain, states, dog)


def _conv(xv, w_ref):
    out = _shift_down(xv, CONV_WIDTH - 1) * w_ref[0:1, :]
    for tap in range(1, CONV_WIDTH):
        out = out + _shift_down(xv, CONV_WIDTH - 1 - tap) * w_ref[tap:tap + 1, :]
    return out


def _conv_bwd(xv, w_ref, dpre, dw_ref):
    dx = None
    for tap in range(CONV_WIDTH):
        s = CONV_WIDTH - 1 - tap
        t = _shift_up(dpre, s) * w_ref[tap:tap + 1, :]
        dx = t if dx is None else dx + t
        dw_ref[tap:tap + 1, :] = jnp.sum(dpre * _shift_down(xv, s), axis=0, keepdims=True)
    return dx


CONV_COLS = 256


def _conv_silu_fwd(x, w, *, name):
    T = x.shape[0]
    n = w.shape[1]

    def body(x_ref, w_ref, o_ref):
        o_ref[...] = _silu(_conv(x_ref[...], w_ref))

    return pl.pallas_call(
        body, name=name, grid=(n // CONV_COLS,),
        in_specs=[pl.BlockSpec((T, CONV_COLS), lambda j: (0, j)), pl.BlockSpec((CONV_WIDTH, CONV_COLS), lambda j: (0, j))],
        out_specs=pl.BlockSpec((T, CONV_COLS), lambda j: (0, j)),
        out_shape=jax.ShapeDtypeStruct((T, n), F32), compiler_params=_params(("parallel",)),
    )(x, w)


def _conv_silu_bwd(x, w, dact, *, name):
    T = x.shape[0]
    n = w.shape[1]

    def body(x_ref, w_ref, da_ref, dx_ref, dw_ref):
        xv = x_ref[...]
        dpre = da_ref[...] * _dsilu(_conv(xv, w_ref))
        dx_ref[...] = _conv_bwd(xv, w_ref, dpre, dw_ref).astype(dx_ref.dtype)

    blk = pl.BlockSpec((T, CONV_COLS), lambda j: (0, j))
    wb = pl.BlockSpec((CONV_WIDTH, CONV_COLS), lambda j: (0, j))
    return pl.pallas_call(
        body, name=name, grid=(n // CONV_COLS,), in_specs=[blk, wb, blk], out_specs=(blk, wb),
        out_shape=(jax.ShapeDtypeStruct((T, n), BF16), jax.ShapeDtypeStruct((CONV_WIDTH, n), F32)),
        compiler_params=_params(("parallel",)),
    )(x, w, dact)


def _interleave(gens):
    results = [None] * len(gens)
    live = list(range(len(gens)))
    while live:
        for i in list(live):
            try:
                next(gens[i])
            except StopIteration as done:
                results[i] = done.value
                live.remove(i)
    return results


def _unit_lower_inverse(a):
    n = a.shape[0]
    eye = jnp.where(_iota2((n, n), 0) == _iota2((n, n), 1), 1.0, 0.0)
    p = -a
    t = eye + p
    for _ in range(5):
        p = _fdot(p, p)
        yield
        t = t + _fdot(t, p)
        yield
    return t


def _gdn_specs(N, rev):
    H, C, DK, DV = GDN_HEADS, CHUNK, GDN_DK, GDN_DV
    cn = (lambda n: N - 1 - n) if rev else (lambda n: n)
    HB, G = GDN_HB, H // GDN_HB
    q = pl.BlockSpec((C, HB * DK), lambda h, n: (cn(n), h))
    k = pl.BlockSpec((C, HB * DK), lambda h, n: (cn(n), G + h))
    v = pl.BlockSpec((C, HB * DV), lambda h, n: (cn(n), 2 * G + h))
    z = pl.BlockSpec((C, HB * DV), lambda h, n: (cn(n), 3 * G + h))
    gates = pl.BlockSpec((C, LANE), lambda h, n: (cn(n), 0))
    sc = pl.BlockSpec((HB, 1, 1), lambda h, n: (h, 0, 0))
    gain = pl.BlockSpec((1, DV), lambda h, n: (0, 0))
    st = pl.BlockSpec((HB, 1, DK, DV), lambda h, n: (h, cn(n), 0, 0))
    return q, k, v, z, gates, sc, gain, st


def _gdn_solved_specs(N, rev):
    C = CHUNK
    cn = (lambda n: N - 1 - n) if rev else (lambda n: n)
    return (pl.BlockSpec((GDN_HB, 1, C, C), lambda h, n: (h, cn(n), 0, 0)),
            pl.BlockSpec((GDN_HB, 1, C, GDN_DV + GDN_DK), lambda h, n: (h, cn(n), 0, 0)))


def _gdn_chunk(q_ref, k_ref, v_ref, gates_ref, alog_ref, dtb_ref, h, solved=None):
    H, C, DK, DV = GDN_HEADS, CHUNK, GDN_DK, GDN_DV
    gates = gates_ref[...]
    lane = _iota2(gates.shape, 1)
    bl = jnp.sum(jnp.where(lane == h, gates, 0.0), axis=1, keepdims=True)
    al = jnp.sum(jnp.where(lane == H + h, gates, 0.0), axis=1, keepdims=True)
    beta = _sigmoid(bl)
    ea = jnp.exp(alog_ref[...])
    xs = al + dtb_ref[...]
    la = -ea * _softplus(xs)
    ii, jj = _iota2((C, C), 0), _iota2((C, C), 1)
    strict = ii > jj
    cum_col = jnp.sum(jnp.where(ii >= jj, _col_to_row(la), 0.0), axis=1, keepdims=True)
    cum_row = jnp.sum(jnp.where(ii <= jj, la, 0.0), axis=0, keepdims=True)
    q, k, v = q_ref[...], k_ref[...], v_ref[...]
    rq = lax.rsqrt(jnp.sum(q * q, axis=-1, keepdims=True) + NORM_EPS)
    rk = lax.rsqrt(jnp.sum(k * k, axis=-1, keepdims=True) + NORM_EPS)
    qn = q * rq * (DK ** -0.5)
    kn = k * rk
    rel = jnp.where(strict, jnp.exp(jnp.where(strict, cum_col - cum_row, 0.0)), 0.0)
    rg = rel * _bdot(kn, kn, NT)
    yield
    a = beta * rg
    e_col = jnp.exp(cum_col)
    clast = _pick_row(cum_col, C - 1)
    if solved is None:
        tm = yield from _unit_lower_inverse(a)
        rhs = jnp.concatenate([beta * v, (beta * e_col) * kn], axis=1)
        sol = _fdot(tm, rhs)
        yield
    else:
        tm, sol = solved
    u, w = sol[:, :DV], sol[:, DV:]
    dd = jnp.exp(clast - cum_col)
    ke = kn * dd
    g = jnp.exp(clast)
    eye = jnp.where(_iota2((DK, DK), 0) == _iota2((DK, DK), 1), 1.0, 0.0)
    trans = g * eye - _bdot(ke, w, TN)
    inject = _bdot(ke, u, TN)
    yield
    return dict(beta=beta, ea=ea, xs=xs, la=la, strict=strict, ii=ii, jj=jj, q=q, k=k, v=v, rq=rq, rk=rk,
                qn=qn, kn=kn, rel=rel, rg=rg, a=a, tm=tm, e_col=e_col, sol=sol, u=u, w=w, dd=dd, ke=ke,
                g=g, eye=eye, trans=trans, inject=inject)


def _gdn_fwd(act, proj, gates, a_log, dt_bias, gain, *, name):
    T = act.shape[0]
    H, C, DK, DV = GDN_HEADS, CHUNK, GDN_DK, GDN_DV
    N = T // C

    def body(q_ref, k_ref, v_ref, z_ref, gates_ref, alog_ref, dtb_ref, gain_ref, og_ref, st_ref, tm_ref, sol_ref,
             s_acc):
        hg, n = pl.program_id(0), pl.program_id(1)

        @pl.when(n == 0)
        def _():
            s_acc[...] = jnp.zeros_like(s_acc)

        def head(hb):
            cols = pl.ds(hb * DK, DK)
            c = yield from _gdn_chunk(q_ref.at[:, cols], k_ref.at[:, cols], v_ref.at[:, cols], gates_ref,
                                      alog_ref.at[hb], dtb_ref.at[hb], hg * GDN_HB + hb)
            tm_ref[hb, 0] = c["tm"]
            sol_ref[hb, 0] = c["sol"]
            sp = s_acc[hb]
            st_ref[hb, 0] = sp
            snew = _bdot(c["trans"], sp) + c["inject"]
            yield
            s_acc[hb] = snew
            o = _bdot(c["qn"], snew)
            yield
            rstd = lax.rsqrt(jnp.mean(o * o, axis=-1, keepdims=True) + NORM_EPS)
            og_ref[:, cols] = (o * rstd * gain_ref[...] * _silu(z_ref[:, cols])).astype(og_ref.dtype)

        _interleave([head(hb) for hb in range(GDN_HB)])

    q, k, v, z, gt, sc, gn, st = _gdn_specs(N, False)
    tm_s, sol_s = _gdn_solved_specs(N, False)
    return pl.pallas_call(
        body, name=name, grid=(H // GDN_HB, N), in_specs=[q, k, v, z, gt, sc, sc, gn],
        out_specs=(q, st, tm_s, sol_s),
        out_shape=(jax.ShapeDtypeStruct((T, H * DV), BF16), jax.ShapeDtypeStruct((H, N, DK, DV), F32),
                   jax.ShapeDtypeStruct((H, N, C, C), F32), jax.ShapeDtypeStruct((H, N, C, DV + DK), F32)),
        scratch_shapes=[pltpu.VMEM((GDN_HB, DK, DV), F32)],
        compiler_params=_params(("arbitrary", "arbitrary")),
    )(act, act, act, proj, gates, a_log, dt_bias, gain)


def _gdn_bwd(act, proj, gates, a_log, dt_bias, gain, states, solved, dog, *, name):
    T = act.shape[0]
    H, C, DK, DV = GDN_HEADS, CHUNK, GDN_DK, GDN_DV
    N = T // C

    def rsum(x):
        return jnp.sum(x, axis=1, keepdims=True)

    def body(q_ref, k_ref, v_ref, z_ref, gates_ref, alog_ref, dtb_ref, gain_ref, st_ref, tm_ref, sol_ref, dog_ref,
             dq_ref, dk_ref, dv_ref, dz_ref, dgates_ref, dscal_ref, dgain_ref, ds_acc):
        hg, n = pl.program_id(0), pl.program_id(1)

        @pl.when(n == 0)
        def _():
            ds_acc[...] = jnp.zeros_like(ds_acc)
            dgain_ref[...] = jnp.zeros_like(dgain_ref)
            dscal_ref[...] = jnp.zeros_like(dscal_ref)

        _interleave([one_head(hb, hg * GDN_HB + hb, q_ref, k_ref, v_ref, z_ref, gates_ref, alog_ref, dtb_ref, gain_ref,
                              st_ref, tm_ref, sol_ref, dog_ref, dq_ref, dk_ref, dv_ref, dz_ref, dgates_ref, dscal_ref,
                              dgain_ref, ds_acc)
                     for hb in range(GDN_HB)])

    def one_head(hb, h, q_ref, k_ref, v_ref, z_ref, gates_ref, alog_ref, dtb_ref, gain_ref, st_ref, tm_ref, sol_ref,
                 dog_ref, dq_ref, dk_ref, dv_ref, dz_ref, dgates_ref, dscal_ref, dgain_ref, ds_acc):
        cols = pl.ds(hb * DK, DK)
        c = yield from _gdn_chunk(q_ref.at[:, cols], k_ref.at[:, cols], v_ref.at[:, cols], gates_ref,
                                  alog_ref.at[hb], dtb_ref.at[hb], h, solved=(tm_ref[hb, 0], sol_ref[hb, 0]))
        beta, kn, qn, v, ke, u, w, dd, e_col = c["beta"], c["kn"], c["qn"], c["v"], c["ke"], c["u"], c["w"], c["dd"], c["e_col"]
        sp = st_ref[hb, 0]
        snew = _bdot(c["trans"], sp) + c["inject"]
        yield
        o = _bdot(qn, snew)
        yield
        rstd = lax.rsqrt(jnp.mean(o * o, axis=-1, keepdims=True) + NORM_EPS)
        oh = o * rstd
        z = z_ref[:, cols]
        gain_v = gain_ref[...]
        dy = dog_ref[:, cols].astype(F32)
        dz_ref[:, cols] = (dy * oh * gain_v * _dsilu(z)).astype(dz_ref.dtype)
        dnorm = dy * _silu(z)
        dgain_ref[hb] += jnp.sum(dnorm * oh, axis=0, keepdims=True)
        doh = dnorm * gain_v
        do = rstd * (doh - oh * jnp.mean(doh * oh, axis=-1, keepdims=True))

        dstot = ds_acc[hb] + _bdot(qn, do, TN)
        dqn = _bdot(do, snew, NT)
        yield
        dtrans = _bdot(dstot, sp, NT)
        ds_acc[hb] = _bdot(c["trans"], dstot, TN)
        yield
        dg = jnp.sum(jnp.sum(dtrans * c["eye"], axis=1, keepdims=True), axis=0, keepdims=True)
        m = -dtrans
        dke = _bdot(w, m, NT) + _bdot(u, dstot, NT)
        dw = _bdot(ke, m)
        du = _bdot(ke, dstot)
        yield
        drhs = _fdot(c["tm"], jnp.concatenate([du, dw], axis=1), TN)
        yield
        da = jnp.where(c["strict"], -_fdot(drhs, c["sol"], NT), 0.0)
        yield
        drhs_u, drhs_w = drhs[:, :DV], drhs[:, DV:]
        rwk = rsum(drhs_w * kn)
        dbeta = rsum(da * c["rg"]) + rsum(drhs_u * v) + rwk * e_col
        dgm = da * beta * c["rel"]
        dkn = _bdot(dgm, kn) + _bdot(dgm, kn, TN) + (beta * e_col) * drhs_w + dd * dke
        yield
        dv_ref[:, cols] = beta * drhs_u
        r_ = da * c["a"]
        ddd = rsum(dke * kn)
        dc = rsum(r_) - _row_to_col(jnp.sum(r_, axis=0, keepdims=True)) + beta * rwk * e_col - ddd * dd
        dclast = jnp.sum(ddd * dd, axis=0, keepdims=True) + dg * c["g"]
        dc = dc + jnp.where(_iota2((C, 1), 0) == C - 1, dclast, 0.0)
        dla = jnp.sum(jnp.where(c["ii"] <= c["jj"], _col_to_row(dc), 0.0), axis=1, keepdims=True)
        dalog = jnp.sum(dla * c["la"], axis=0, keepdims=True)
        dxs = dla * (-c["ea"]) * _sigmoid(c["xs"])
        ddtb = jnp.sum(dxs, axis=0, keepdims=True)
        dbl = dbeta * beta * (1.0 - beta)
        lane = _iota2((C, LANE), 1)
        dgates_ref[hb] = jnp.where(lane == 0, dbl, jnp.where(lane == 1, dxs, 0.0))
        lane8 = _iota2((8, LANE), 1)
        dscal_ref[hb] += jnp.where(lane8 == 0, dalog, jnp.where(lane8 == 1, ddtb, 0.0))
        dk_ref[:, cols] = c["rk"] * (dkn - kn * rsum(dkn * kn))
        qh = c["q"] * c["rq"]
        dqs = dqn * (DK ** -0.5)
        dq_ref[:, cols] = c["rq"] * (dqs - qh * rsum(dqs * qh))

    q, k, v, z, gt, sc, gn, st = _gdn_specs(N, True)
    dgt = pl.BlockSpec((GDN_HB, C, LANE), lambda h, n: (h, N - 1 - n, 0))
    dsc = pl.BlockSpec((GDN_HB, 8, LANE), lambda h, n: (h, 0, 0))
    dgn = pl.BlockSpec((GDN_HB, 1, DV), lambda h, n: (h, 0, 0))
    sh = jax.ShapeDtypeStruct((T, H * DK), F32)
    tm_s, sol_s = _gdn_solved_specs(N, True)
    return pl.pallas_call(
        body, name=name, grid=(H // GDN_HB, N), in_specs=[q, k, v, z, gt, sc, sc, gn, st, tm_s, sol_s, q],
        out_specs=(q, q, q, q, dgt, dsc, dgn),
        out_shape=(sh, sh, sh, jax.ShapeDtypeStruct((T, H * DV), BF16),
                   jax.ShapeDtypeStruct((H, T, LANE), F32), jax.ShapeDtypeStruct((H, 8, LANE), F32),
                   jax.ShapeDtypeStruct((H, 1, DV), F32)),
        scratch_shapes=[pltpu.VMEM((GDN_HB, DK, DV), F32)],
        compiler_params=_params(("arbitrary", "arbitrary")),
    )(act, act, act, proj, gates, a_log, dt_bias, gain, states, *solved, dog)


SUBLANES = 8


def _linear_scan(a_ref, b_ref, h_ref, reverse):
    T, W = a_ref.shape
    nb = T // SUBLANES
    row = _iota2((SUBLANES, W), 0)

    def blk(bi, carry):
        bb = (nb - 1 - bi) if reverse else bi
        off = pl.multiple_of(bb * SUBLANES, SUBLANES)
        a = a_ref[pl.ds(off, SUBLANES), :]
        b = b_ref[pl.ds(off, SUBLANES), :]
        for d in (1, 2, 4):
            if reverse:
                edge = row >= SUBLANES - d
                a_sh = jnp.where(edge, 1.0, pltpu.roll(a, SUBLANES - d, 0))
                b_sh = jnp.where(edge, 0.0, pltpu.roll(b, SUBLANES - d, 0))
            else:
                edge = row < d
                a_sh = jnp.where(edge, 1.0, pltpu.roll(a, d, 0))
                b_sh = jnp.where(edge, 0.0, pltpu.roll(b, d, 0))
            b = a * b_sh + b
            a = a * a_sh
        h = a * carry + b
        h_ref[pl.ds(off, SUBLANES), :] = h
        return h[0:1, :] if reverse else h[SUBLANES - 1:SUBLANES, :]

    lax.fori_loop(0, nb, blk, jnp.zeros((1, W), F32))


def _lru_specs(T):
    B, W = LRU_BLOCKS, LRU_BLOCK
    xb = pl.BlockSpec((T, W), lambda j: (0, j))
    yb = pl.BlockSpec((T, W), lambda j: (0, B + j))
    cw = pl.BlockSpec((CONV_WIDTH, W), lambda j: (0, j))
    vec = pl.BlockSpec((1, W), lambda j: (0, j))
    wg = pl.BlockSpec((1, W, W), lambda j: (j, 0, 0))
    bg = pl.BlockSpec((1, 1, W), lambda j: (j, 0, 0))
    return xb, yb, cw, vec, wg, bg


def _lru_gates(xb_ref, cw_ref, cb_ref, wr_ref, br_ref, wi_ref, bi_ref, lam_ref):
    xv = xb_ref[...]
    xc = _conv(xv, cw_ref) + cb_ref[...]
    r = _sigmoid(_bdot(xc, wr_ref[0]) + br_ref[0])
    i = _sigmoid(_bdot(xc, wi_ref[0]) + bi_ref[0])
    sp = _softplus(-lam_ref[...])
    la = -LRU_C * sp * r
    a = jnp.exp(la)
    s = jnp.sqrt(-_expm1(2.0 * la))
    return xv, xc, r, i, sp, a, s


def _lru_fwd(proj, conv_w, conv_b, w_r, b_r, w_i, b_i, lam, *, name):
    T = proj.shape[0]
    B, W = LRU_BLOCKS, LRU_BLOCK

    def body(xb_ref, yb_ref, cw_ref, cb_ref, wr_ref, br_ref, wi_ref, bi_ref, lam_ref, og_ref, hs_ref, a_s, u_s):
        xv, xc, r, i, sp, a, s = _lru_gates(xb_ref, cw_ref, cb_ref, wr_ref, br_ref, wi_ref, bi_ref, lam_ref)
        a_s[...] = a
        u_s[...] = s * (i * xc)
        _linear_scan(a_s, u_s, hs_ref, False)
        og_ref[...] = (hs_ref[...] * _gelu(yb_ref[...])).astype(og_ref.dtype)

    xb, yb, cw, vec, wg, bg = _lru_specs(T)
    return pl.pallas_call(
        body, name=name, grid=(B,), in_specs=[xb, yb, cw, vec, wg, bg, wg, bg, vec], out_specs=(xb, xb),
        out_shape=(jax.ShapeDtypeStruct((T, B * W), BF16), jax.ShapeDtypeStruct((T, B * W), F32)),
        scratch_shapes=[pltpu.VMEM((T, W), F32), pltpu.VMEM((T, W), F32)],
        compiler_params=_params(("arbitrary",)),
    )(proj, proj, conv_w, conv_b, w_r, b_r, w_i, b_i, lam)


def _lru_bwd(proj, conv_w, conv_b, w_r, b_r, w_i, b_i, lam, hs, dout, *, name):
    T = proj.shape[0]
    B, W = LRU_BLOCKS, LRU_BLOCK

    def csum(x):
        return jnp.sum(x, axis=0, keepdims=True)

    def body(xb_ref, yb_ref, cw_ref, cb_ref, wr_ref, br_ref, wi_ref, bi_ref, lam_ref, hs_ref, do_ref,
             dxb_ref, dyb_ref, dcw_ref, dcb_ref, dwr_ref, dbr_ref, dwi_ref, dbi_ref, dlam_ref, a_s, b_s, g_s):
        xv, xc, r, i, sp, a, s = _lru_gates(xb_ref, cw_ref, cb_ref, wr_ref, br_ref, wi_ref, bi_ref, lam_ref)
        h = hs_ref[...]
        yb = yb_ref[...]
        dout = do_ref[...].astype(F32)
        dyb_ref[...] = (dout * h * _dgelu(yb)).astype(dyb_ref.dtype)
        a_s[...] = _shift_up(a, 1)
        b_s[...] = dout * _gelu(yb)
        _linear_scan(a_s, b_s, g_s, True)
        g = g_s[...]
        da = g * _shift_down(h, 1)
        ds = g * (i * xc)
        di = g * s * xc
        dxc = g * s * i
        dla = da * a - ds * (a * a) / s
        dr = dla * (-LRU_C * sp)
        dlam_ref[...] = csum(dla * r) * (LRU_C * _sigmoid(-lam_ref[...]))
        dpr = dr * r * (1.0 - r)
        dpi = di * i * (1.0 - i)
        dxc = dxc + _bdot(dpr, wr_ref[0], NT) + _bdot(dpi, wi_ref[0], NT)
        dwr_ref[0] = _bdot(xc, dpr, TN)
        dwi_ref[0] = _bdot(xc, dpi, TN)
        dbr_ref[0] = csum(dpr)
        dbi_ref[0] = csum(dpi)
        dcb_ref[...] = csum(dxc)
        dxb_ref[...] = _conv_bwd(xv, cw_ref, dxc, dcw_ref).astype(dxb_ref.dtype)

    xb, yb, cw, vec, wg, bg = _lru_specs(T)
    act = jax.ShapeDtypeStruct((T, B * W), BF16)
    return pl.pallas_call(
        body, name=name, grid=(B,), in_specs=[xb, yb, cw, vec, wg, bg, wg, bg, vec, xb, xb],
        out_specs=(xb, xb, cw, vec, wg, bg, wg, bg, vec),
        out_shape=(act, act, jax.ShapeDtypeStruct((CONV_WIDTH, B * W), F32), jax.ShapeDtypeStruct((1, B * W), F32),
                   jax.ShapeDtypeStruct((B, W, W), F32), jax.ShapeDtypeStruct((B, 1, W), F32),
                   jax.ShapeDtypeStruct((B, W, W), F32), jax.ShapeDtypeStruct((B, 1, W), F32),
                   jax.ShapeDtypeStruct((1, B * W), F32)),
        scratch_shapes=[pltpu.VMEM((T, W), F32)] * 3,
        compiler_params=_params(("arbitrary",)),
    )(proj, proj, conv_w, conv_b, w_r, b_r, w_i, b_i, lam, hs, dout)


MESH = pl.DeviceIdType.MESH
N_CHIPS = 4
AG_COPIES = 7


def _mesh_pos():
    return lax.axis_index("x"), lax.axis_index("y"), lax.axis_index("c")


def _hbm_specs(n):
    return [pl.BlockSpec(memory_space=pltpu.HBM)] * n


def _all_gather(shards, *, name):
    n = len(shards)

    def body(*refs):
        xs, outs = refs[:n], refs[n:2 * n]
        send_sems, recv_sems, local_sems = refs[2 * n:]
        x, y, c = _mesh_pos()
        me, sibling = (x, y, c), (x, y, 1 - c)
        chips = [(1 - x, y), (x, 1 - y), (1 - x, 1 - y)]

        def rows(t, px, py, pc):
            return outs[t].at[4 * px + 2 * py + pc]

        def copy(t, k, block, to, src=None):
            return pltpu.make_async_remote_copy(
                src_ref=rows(t, *block) if src is None else src, dst_ref=rows(t, *block),
                send_sem=send_sems.at[t * AG_COPIES + k], recv_sem=recv_sems.at[t * AG_COPIES + k],
                device_id=to, device_id_type=MESH)

        mine = [pltpu.make_async_copy(xs[t], rows(t, *me), local_sems.at[t]) for t in range(n)]
        for cp in mine:
            cp.start()
        first = []
        for t in range(n):
            first.append(copy(t, 0, me, sibling, src=xs[t]))
            first += [copy(t, 1 + j, me, (*chip, c), src=xs[t]) for j, chip in enumerate(chips)]
        for cp in first:
            cp.start()
        passed = []
        for j, chip in enumerate(chips):
            for t in range(n):
                copy(t, 1 + j, (*chip, c), me).wait_recv()
                cp = copy(t, 4 + j, (*chip, c), sibling)
                cp.start()
                passed.append(cp)
        for t in range(n):
            copy(t, 0, sibling, me).wait_recv()
        for j, chip in enumerate(chips):
            for t in range(n):
                copy(t, 4 + j, (*chip, 1 - c), me).wait_recv()
        for cp in first + passed:
            cp.wait_send()
        for cp in mine:
            cp.wait()

    return pl.pallas_call(
        body, name=name,
        out_shape=[jax.ShapeDtypeStruct((N_DEV,) + s.shape, s.dtype) for s in shards],
        in_specs=_hbm_specs(n), out_specs=_hbm_specs(n),
        scratch_shapes=[pltpu.SemaphoreType.DMA((n * AG_COPIES,)), pltpu.SemaphoreType.DMA((n * AG_COPIES,)),
                        pltpu.SemaphoreType.DMA((n,))],
    )(*shards)


SIDE_EFFECT = pltpu.SideEffectType.DATAFLOW_SIDE_EFFECTING


def _copies(plan, refs, send_sems, recv_sems):
    return [pltpu.make_async_remote_copy(src_ref=src, dst_ref=dst, send_sem=send_sems.at[k], recv_sem=recv_sems.at[k],
                                         device_id=to, device_id_type=MESH)
            for k, (src, dst, to) in enumerate(plan(refs))]


def _split_start(bufs, plan, n_copies, *, name, deps=()):
    n = len(bufs)

    def body(*refs):
        send_sems, recv_sems = refs[n + len(deps)], refs[n + len(deps) + 1]
        token = refs[-1]
        for cp in _copies(plan, refs[:n], send_sems, recv_sems):
            cp.start()
        token[...] = jnp.zeros_like(token)

    hbm, sem = pl.BlockSpec(memory_space=pltpu.HBM), pl.BlockSpec(memory_space=pltpu.SEMAPHORE)
    out = pl.pallas_call(
        body, name=name,
        out_shape=(pltpu.SemaphoreType.DMA((n_copies,)), pltpu.SemaphoreType.DMA((n_copies,)),
                   *[pltpu.HBM(b.shape, b.dtype) for b in bufs], jax.ShapeDtypeStruct(TOKEN_SHAPE, F32)),
        in_specs=[hbm] * n + [pl.BlockSpec(memory_space=pl.ANY)] * len(deps),
        out_specs=(sem, sem, *[hbm] * n, pl.BlockSpec(memory_space=pltpu.VMEM)),
        input_output_aliases={i: 2 + i for i in range(n)},
        compiler_params=pltpu.CompilerParams(has_side_effects=SIDE_EFFECT),
    )(*[pltpu.with_memory_space_constraint(b, pltpu.HBM) for b in bufs], *deps)
    return out[0], out[1], list(out[2:2 + n]), out[-1]


def _split_wait(send_sems, recv_sems, bufs, plan, after, *, name):
    n = len(bufs)

    def body(*refs):
        for cp in _copies(plan, refs[:n], refs[n], refs[n + 1]):
            cp.wait_send()
            cp.wait_recv()

    hbm, sem = pl.BlockSpec(memory_space=pltpu.HBM), pl.BlockSpec(memory_space=pltpu.SEMAPHORE)
    out = pl.pallas_call(
        body, name=name, out_shape=tuple(pltpu.HBM(b.shape, b.dtype) for b in bufs),
        in_specs=[hbm] * n + [sem, sem, pl.BlockSpec(memory_space=pl.ANY)], out_specs=tuple([hbm] * n),
        input_output_aliases={i: i for i in range(n)},
        compiler_params=pltpu.CompilerParams(has_side_effects=SIDE_EFFECT),
    )(*bufs, send_sems, recv_sems, after)
    return list(out)


def _block(ref, in_cols, d):
    if not in_cols:
        return ref.at[d]
    c = ref.shape[1] // N_DEV
    return ref.at[:, pl.ds(pl.multiple_of(d * c, LANE), c)]


def _plan_gather_a(in_cols):
    n = len(in_cols)

    def plan(refs):
        x, y, c = _mesh_pos()
        me = 4 * x + 2 * y + c
        peers = [(x, y, 1 - c), (1 - x, y, c), (x, 1 - y, c), (1 - x, 1 - y, c)]
        return [(refs[t], _block(refs[n + t], in_cols[t], me), to) for t in range(n) for to in peers]
    return plan


def _plan_gather_b(in_cols):
    n = len(in_cols)

    def plan(refs):
        x, y, c = _mesh_pos()
        ds = [4 * px + 2 * py + c for px, py in [(1 - x, y), (x, 1 - y), (1 - x, 1 - y)]]
        return [(_block(refs[t], in_cols[t], d), _block(refs[t], in_cols[t], d), (x, y, 1 - c))
                for t in range(n) for d in ds]
    return plan


def _plan_scatter_pair(in_cols):
    n = len(in_cols)

    def plan(refs):
        x, y, c = _mesh_pos()
        return [(_block(refs[t], in_cols[t], 2 * q + (1 - c)), refs[n + t].at[q], (x, y, 1 - c))
                for t in range(n) for q in range(N_CHIPS)]
    return plan


def _plan_scatter_chips(n):
    def plan(refs):
        x, y, c = _mesh_pos()
        chips = [(1 - x, y), (x, 1 - y), (1 - x, 1 - y)]
        return [(refs[t].at[2 * px + py], refs[n + t].at[j], (px, py, c))
                for t in range(n) for j, (px, py) in enumerate(chips)]
    return plan


def _insert_block(land, shard, device, in_cols, *, name):
    r, c = shard.shape
    tr = _tile(r, (512, 256))

    def body(d_ref, s_ref, land_ref, o_ref):
        if in_cols:
            o_ref[...] = s_ref[...]
        else:
            o_ref[0] = s_ref[...]

    if in_cols:
        o_spec = pl.BlockSpec((tr, c), lambda i, d: (i, d[0]))
    else:
        o_spec = pl.BlockSpec((1, tr, c), lambda i, d: (d[0], i, 0))
    return pl.pallas_call(
        body, name=name,
        grid_spec=pltpu.PrefetchScalarGridSpec(
            num_scalar_prefetch=1, grid=(r // tr,),
            in_specs=[pl.BlockSpec((tr, c), lambda i, d: (i, 0)), pl.BlockSpec(memory_space=pl.ANY)],
            out_specs=o_spec),
        out_shape=jax.ShapeDtypeStruct(land.shape, land.dtype), input_output_aliases={2: 0},
        compiler_params=_params(("parallel",)),
    )(device, shard, land)


PAIR_ROWS = (512, 256)


def _pair_add(g, a, core, in_cols, *, name):
    _, R, C = a.shape
    tr = _tile(R, PAIR_ROWS)

    def body(c_ref, g_ref, a_ref, o_ref):
        gv = g_ref[...] if in_cols else g_ref[0]
        o_ref[0] = (gv.astype(F32) + a_ref[0].astype(F32)).astype(o_ref.dtype)

    blk = pl.BlockSpec((1, tr, C), lambda q, i, c: (q, i, 0))
    if in_cols:
        g_spec = pl.BlockSpec((tr, C), lambda q, i, c: (i, 2 * q + c[0]))
    else:
        g_spec = pl.BlockSpec((1, tr, C), lambda q, i, c: (2 * q + c[0], i, 0))
    return pl.pallas_call(
        body, name=name,
        grid_spec=pltpu.PrefetchScalarGridSpec(
            num_scalar_prefetch=1, grid=(N_CHIPS, R // tr), in_specs=[g_spec, blk], out_specs=blk),
        out_shape=jax.ShapeDtypeStruct((N_CHIPS, R, C), BF16),
        compiler_params=_params(("parallel", "parallel")),
    )(core, g, a)


ADAM_ROWS = 256


def _adamw_sharded(w, m, v, s4, b3, chip, layer, prev, *, name):
    L, R, C = w.shape
    tr = _tile(R, (ADAM_ROWS,))
    if prev is None and L > 1:
        prev = tuple(lax.empty(w.shape, F32) for _ in range(4))
    n_prev = 0 if prev is None else 4

    def body(q_ref, w_ref, m_ref, v_ref, s_ref, b_ref, *rest):
        g_out, d_out, m_out, v_out = rest[n_prev:]
        g = s_ref[0].astype(F32)
        for j in range(N_CHIPS - 1):
            g = g + b_ref[j].astype(F32)
        mn = ADAM_B1 * m_ref[0] + (1.0 - ADAM_B1) * g
        vn = ADAM_B2 * v_ref[0] + (1.0 - ADAM_B2) * (g * g)
        g_out[0] = g
        d_out[0] = -ADAM_LR * ((mn / ADAM_C1) / (jnp.sqrt(vn / ADAM_C2) + ADAM_EPS) + ADAM_WD * w_ref[0])
        m_out[0] = mn
        v_out[0] = vn

    blk = pl.BlockSpec((1, tr, C), lambda i, q: (layer, i, 0))
    sh = jax.ShapeDtypeStruct((L, R, C), F32)
    return pl.pallas_call(
        body, name=name,
        grid_spec=pltpu.PrefetchScalarGridSpec(
            num_scalar_prefetch=1, grid=(R // tr,),
            in_specs=[blk, blk, blk, pl.BlockSpec((1, tr, C), lambda i, q: (q[0], i, 0)),
                      pl.BlockSpec((N_CHIPS - 1, tr, C), lambda i, q: (0, i, 0))]
            + [pl.BlockSpec(memory_space=pl.ANY)] * n_prev,
            out_specs=(blk,) * 4),
        out_shape=(sh,) * 4, input_output_aliases={6 + k: k for k in range(n_prev)},
        compiler_params=_params(("parallel",)),
    )(chip, w, m, v, s4, b3, *(prev or ()))


FWD_NAMES = ['x', 'norm1', 'norm2', 'final_norm', 'ret_w_in', 'ret_gn_gain', 'ret_w_out', 'gdn_w_in', 'gdn_conv_w',
             'gdn_a_log', 'gdn_dt_bias', 'gdn_norm_gain', 'gdn_w_out', 'gla_w_in', 'gla_w_gate_up', 'gla_gate_bias',
             'gla_norm_gain', 'gla_w_out', 'lru_w_in', 'lru_conv_w', 'lru_conv_b', 'lru_w_rgate', 'lru_b_rgate',
             'lru_w_igate', 'lru_b_igate', 'lru_lambda', 'lru_w_out', 'mlp_w_up', 'mlp_w_down']
WEIGHT_NAMES = FWD_NAMES[1:]
ARG_NAMES = FWD_NAMES + ['loss_target'] + ['m_' + n for n in WEIGHT_NAMES] + ['v_' + n for n in WEIGHT_NAMES]

MIXER_IN = ('ret_w_in', 'gdn_w_in', 'gla_w_in', 'lru_w_in')
MIXER_OUT = ('ret_w_out', 'gdn_w_out', 'gla_w_out', 'lru_w_out')
BIG_NAMES = MIXER_IN + MIXER_OUT + ('mlp_w_up', 'mlp_w_down')
SMALL = {'norm1': False, 'norm2': False, 'final_norm': False, 'ret_gn_gain': True, 'gdn_conv_w': True,
         'gdn_a_log': False, 'gdn_dt_bias': False, 'gdn_norm_gain': False, 'gla_w_gate_up': True,
         'gla_gate_bias': True, 'gla_norm_gain': True, 'lru_conv_w': True, 'lru_conv_b': True,
         'lru_w_rgate': False, 'lru_b_rgate': False, 'lru_w_igate': False, 'lru_b_igate': False, 'lru_lambda': True}
SMALL_NAMES = tuple(n for n in WEIGHT_NAMES if n in SMALL)
MEDIUM = ('lru_w_rgate', 'lru_w_igate')
EARLY_SMALL = tuple(n for n in SMALL_NAMES if n.startswith(('gdn_', 'gla_', 'lru_')) and n not in MEDIUM)
LATE_SMALL = tuple(n for n in SMALL_NAMES if n not in EARLY_SMALL + MEDIUM)
GDN_TAIL = 2 * GDN_HEADS


PACK_ROWS = 256


PACK_TILE = SUBLANES * LANE


def _pack(arrs):
    rows = []
    for a in arrs:
        f = a.reshape(-1).astype(F32)
        rows.append(jnp.pad(f, (0, (-f.shape[0]) % PACK_TILE)).reshape(-1, LANE))
    fill = (-sum(r.shape[0] for r in rows)) % PACK_ROWS
    if fill:
        rows.append(jnp.zeros((fill, LANE), F32))
    return jnp.concatenate(rows, axis=0)


def _unpack(buf, shapes, lead=()):
    out, r0 = [], 0
    for s in shapes:
        n = int(np.prod(s))
        nr = -(-n // PACK_TILE) * SUBLANES
        blk = buf[..., r0:r0 + nr, :].reshape(lead + (nr * LANE,))[..., :n]
        out.append(blk.reshape(lead + tuple(s)))
        r0 += nr
    return out


def _full_cols(g):
    return jnp.transpose(g, (1, 0, 2)).reshape(g.shape[1], N_DEV * g.shape[2])


def _full_rows(g):
    return g.reshape(N_DEV * g.shape[1], g.shape[2])


def _blocks_cols(dw):
    r, c = dw.shape[0], dw.shape[1] // N_DEV
    return jnp.transpose(dw.reshape(r, N_DEV, c), (1, 0, 2))


def _blocks_rows(dw):
    return dw.reshape(N_DEV, dw.shape[0] // N_DEV, dw.shape[1])


def _pad_cols(a, n=LANE):
    return jnp.pad(a, ((0, 0), (0, n - a.shape[1])))


def _mixer_fwd(layer, hn, w_in, sm, tables, deps=()):
    tag = f"l{layer}"
    if layer == 0:
        proj = _matmul(hn, w_in, name=tag + "_in", deps=deps)
        gain = sm['ret_gn_gain'][0][:, None, :]
        og, st = _ret_fwd(proj, gain, tables, name=tag + "_ret_fwd")
        return og, dict(proj=proj, st=st, gain=gain)
    if layer == 1:
        w_main, w_tail = w_in[:, :4 * D_MODEL], _pad_cols(w_in[:, 4 * D_MODEL:])
        proj = _matmul(hn, w_main, name=tag + "_in", deps=deps)
        gates = _matmul(hn, w_tail, name=tag + "_in_tail")
        conv_w = sm['gdn_conv_w'][0]
        act = _conv_silu_fwd(proj, conv_w, name=tag + "_conv")
        a_log = sm['gdn_a_log'].reshape(GDN_HEADS, 1, 1)
        dt_bias = sm['gdn_dt_bias'].reshape(GDN_HEADS, 1, 1)
        gain = sm['gdn_norm_gain']
        og, st, tm, sol = _gdn_fwd(act, proj, gates, a_log, dt_bias, gain, name=tag + "_gdn_fwd")
        return og, dict(proj=proj, gates=gates, act=act, st=st, solved=(tm, sol), conv_w=conv_w, a_log=a_log, dt_bias=dt_bias,
                        gain=gain, w_main=w_main, w_tail=w_tail)
    if layer == 2:
        w_main, w_tail = w_in[:, :3 * D_MODEL], _pad_cols(w_in[:, 3 * D_MODEL:])
        proj = _matmul(hn, w_main, name=tag + "_in", deps=deps)
        glow = _matmul(hn, w_tail, name=tag + "_in_tail")
        wgu = jnp.pad(sm['gla_w_gate_up'][0], ((0, LANE - GLA_GATE_RANK), (0, 0)))
        glogit = _matmul(glow, wgu, name=tag + "_gate_up")
        bias = sm['gla_gate_bias']
        gain = sm['gla_norm_gain'][0][:, None, :]
        og, st = _gla_fwd(proj, glogit, bias, gain, name=tag + "_gla_fwd")
        return og, dict(proj=proj, glow=glow, glogit=glogit, wgu=wgu, bias=bias, gain=gain, st=st,
                        w_main=w_main, w_tail=w_tail)
    proj = _matmul(hn, w_in, name=tag + "_in", deps=deps)
    args = (proj, sm['lru_conv_w'][0], sm['lru_conv_b'], sm['lru_w_rgate'][0], sm['lru_b_rgate'][0][:, None, :],
            sm['lru_w_igate'][0], sm['lru_b_igate'][0][:, None, :], sm['lru_lambda'])
    og, hs = _lru_fwd(*args, name=tag + "_lru_fwd")
    return og, dict(args=args, hs=hs)


def _mixer_bwd(layer, hn, w_in, dog, sv, tables, on_dw):
    tag = f"l{layer}"
    if layer == 0:
        dq, dk, dv, dg, dgain = _ret_bwd(sv['proj'], sv['gain'], tables, sv['st'], dog, name=tag + "_ret_bwd")
        dproj = jnp.concatenate([dq, dk, dv, dg], axis=1)
        deps = on_dw(_matmul(hn, dproj, ta=True, out_dtype=BF16, name=tag + "_in_dw"))
        dhn = _matmul(dproj, w_in, tb=True, name=tag + "_in_dx", deps=deps)
        return dhn, {'ret_gn_gain': dgain[:, 0][None]}
    if layer == 1:
        dq, dk, dv, dz, dgates, dscal, dgain = _gdn_bwd(
            sv['act'], sv['proj'], sv['gates'], sv['a_log'], sv['dt_bias'], sv['gain'], sv['st'], sv['solved'], dog,
            name=tag + "_gdn_bwd")
        dact = jnp.concatenate([dq, dk, dv], axis=1)
        dqkv, dconv = _conv_silu_bwd(sv['proj'], sv['conv_w'], dact, name=tag + "_conv_bwd")
        dmain = jnp.concatenate([dqkv, dz], axis=1)
        T = dmain.shape[0]
        dtail = _pad_cols(jnp.transpose(dgates[:, :, :2], (1, 2, 0)).reshape(T, GDN_TAIL))
        dw_main = _matmul(hn, dmain, ta=True, out_dtype=BF16, name=tag + "_in_dw")
        dw_tail = _matmul(hn, dtail, ta=True, out_dtype=BF16, name=tag + "_in_tail_dw")
        deps = on_dw(jnp.concatenate([dw_main, dw_tail[:, :GDN_TAIL]], axis=1))
        dhn = _matmul(dmain, sv['w_main'], tb=True, name=tag + "_in_dx", deps=deps)
        dhn = _matmul(dtail, sv['w_tail'], tb=True, epi="add", extra=dhn, name=tag + "_in_tail_dx")
        small = {'gdn_conv_w': dconv[None], 'gdn_a_log': dscal[:, 0, 0][None], 'gdn_dt_bias': dscal[:, 0, 1][None],
                 'gdn_norm_gain': jnp.sum(dgain[:, 0], axis=0)[None]}
        return dhn, small
    if layer == 2:
        dq, dk, dv, dr, dgl, dgain = _gla_bwd(sv['proj'], sv['glogit'], sv['bias'], sv['gain'], sv['st'], dog,
                                              name=tag + "_gla_bwd")
        dmain = jnp.concatenate([dq, dk, dv, dr], axis=1)
        dglow = _matmul(dgl, sv['wgu'], tb=True, name=tag + "_gate_up_dx")
        dwgu = _matmul(sv['glow'], dgl, ta=True, name=tag + "_gate_up_dw")
        dbias = _colsum(dgl, name=tag + "_gate_bias")
        dw_main = _matmul(hn, dmain, ta=True, out_dtype=BF16, name=tag + "_in_dw")
        dw_tail = _matmul(hn, dglow, ta=True, out_dtype=BF16, name=tag + "_in_tail_dw")
        deps = on_dw(jnp.concatenate([dw_main, dw_tail[:, :GLA_GATE_RANK]], axis=1))
        dhn = _matmul(dmain, sv['w_main'], tb=True, name=tag + "_in_dx", deps=deps)
        dhn = _matmul(dglow, sv['w_tail'], tb=True, epi="add", extra=dhn, name=tag + "_in_tail_dx")
        small = {'gla_w_gate_up': dwgu[:GLA_GATE_RANK][None], 'gla_gate_bias': dbias,
                 'gla_norm_gain': dgain[:, 0][None]}
        return dhn, small
    dxb, dyb, dcw, dcb, dwr, dbr, dwi, dbi, dlam = _lru_bwd(*sv['args'], sv['hs'], dog, name=tag + "_lru_bwd")
    dproj = jnp.concatenate([dxb, dyb], axis=1)
    deps = on_dw(_matmul(hn, dproj, ta=True, out_dtype=BF16, name=tag + "_in_dw"))
    dhn = _matmul(dproj, w_in, tb=True, name=tag + "_in_dx", deps=deps)
    small = {'lru_conv_w': dcw[None], 'lru_conv_b': dcb, 'lru_w_rgate': dwr[None], 'lru_b_rgate': dbr[:, 0][None],
             'lru_w_igate': dwi[None], 'lru_b_igate': dbi[:, 0][None], 'lru_lambda': dlam}
    return dhn, small


def _step(*args):
    assert len(args) == len(ARG_NAMES)
    p = dict(zip(ARG_NAMES, args))
    xi, yi, ci = _mesh_pos()
    dev = 4 * xi + 2 * yi + ci
    device = dev.astype(jnp.int32).reshape(1)
    core = ci.astype(jnp.int32).reshape(1)
    chip = (2 * xi + yi).astype(jnp.int32).reshape(1)
    x = p['x'][0]
    target = p['loss_target'][0]
    T = x.shape[0]
    tables = _ret_tables(T)

    sharded_small = [n for n in SMALL_NAMES if SMALL[n]]
    gathered, = _all_gather([_pack([p[n] for n in sharded_small])], name="gather_small")
    def in_cols_of(w):
        return w.shape[1] % LANE == 0

    gathers = {}
    token = gathered
    for layer in range(DEPTH):
        w_in_shard = p[MIXER_IN[layer]][0]
        groups = {'a': [(w_in_shard, in_cols_of(w_in_shard), _full_cols)],
                  'b': [(p[MIXER_OUT[layer]][0], False, _full_rows)],
                  'c': [(p['mlp_w_up'][layer], True, None), (p['mlp_w_down'][layer], False, _full_rows)]}
        for key, members in groups.items():
            shards = [w.astype(BF16) for w, _, _ in members]
            in_cols = [ic for _, ic, _ in members]
            n = len(shards)
            lands = [_insert_block(lax.empty((s.shape[0], N_DEV * s.shape[1]) if ic else (N_DEV,) + s.shape, BF16),
                                   s, device, ic, name=f"own_l{layer}{key}{i}")
                     for i, (s, ic) in enumerate(zip(shards, in_cols))]
            send, recv, bufs, token = _split_start(shards + lands, _plan_gather_a(in_cols), 4 * n,
                                                   name=f"gather_a_start_l{layer}{key}", deps=(token,))
            gathers[layer, key] = dict(n=n, h=(send, recv, bufs), in_cols=in_cols, full_of=[f for _, _, f in members])

    def gather_forward(layer, key, after):
        g = gathers[layer, key]
        n = g['n']
        send, recv, bufs = g['h']
        bufs = _split_wait(send, recv, bufs, _plan_gather_a(g['in_cols']), after, name=f"gather_a_wait_l{layer}{key}")
        send, recv, lands, tok = _split_start(bufs[n:], _plan_gather_b(g['in_cols']), 3 * n,
                                              name=f"gather_b_start_l{layer}{key}")
        g['h'] = (send, recv, lands)
        return tok

    def gather_finish(layer, key, after):
        g = gathers[layer, key]
        send, recv, lands = g['h']
        lands = _split_wait(send, recv, lands, _plan_gather_b(g['in_cols']), after, name=f"gather_b_wait_l{layer}{key}")
        return [l if ic else full_of(l) for l, ic, full_of in zip(lands, g['in_cols'], g['full_of'])]

    w_in_next, = gather_finish(0, 'a', gather_forward(0, 'a', token))
    parts = _unpack(gathered, [p[n].shape for n in sharded_small], lead=(N_DEV,))
    sm = {n: p[n] for n in SMALL_NAMES if not SMALL[n]}
    for n, blk in zip(sharded_small, parts):
        full = jnp.moveaxis(blk, 0, -2)
        sm[n] = full.reshape(full.shape[:-2] + (N_DEV * full.shape[-1],))

    saved = []
    big = {}
    for layer in range(DEPTH):
        w_in = w_in_next
        tag = f"l{layer}"
        hn = _rmsnorm_fwd(x, sm['norm1'][layer][None], name=tag + "_norm1")
        og, sv = _mixer_fwd(layer, hn, w_in, sm, tables)
        tok_b = gather_forward(layer, 'b', og)
        tok_c = gather_forward(layer, 'c', tok_b)
        w_out, = gather_finish(layer, 'b', tok_c)
        x_mid = _matmul(og, w_out, epi="add", extra=x, name=tag + "_out")
        hn2 = _rmsnorm_fwd(x_mid, sm['norm2'][layer][None], name=tag + "_norm2")
        w_up, w_down = gather_finish(layer, 'c', hn2)
        big[layer] = (w_in, w_out, w_up, w_down)
        u, a = _matmul(hn2, w_up, epi="relu2", name=tag + "_up")
        deps = (gather_forward(layer + 1, 'a', u),) if layer + 1 < DEPTH else ()
        x_new = _matmul(a, w_down, epi="add", extra=x_mid, name=tag + "_down", deps=deps)
        if layer + 1 < DEPTH:
            w_in_next, = gather_finish(layer + 1, 'a', x_new)
        saved.append(dict(x=x, hn=hn, og=og, sv=sv, x_mid=x_mid, hn2=hn2, u=u, a=a))
        x = x_new
    dy, loss_part = _final_loss(x, sm['final_norm'][None], target, name="final_loss")
    loss = lax.psum(loss_part[0, 0], ("x", "y", "c"))
    dx, dxb, dfinal = _rmsnorm_bwd(x, sm['final_norm'][None], dy, jnp.zeros_like(x), name="final_norm_bwd")

    outs = {}
    small_grads = {'final_norm': dfinal[0]}
    dnorm1, dnorm2 = [None] * DEPTH, [None] * DEPTH

    def scatter_start(items, deps, tag):
        grads, in_cols = [], []
        for _, _, dw, on in items:
            ic = on == 'cols' and (dw.shape[1] // N_DEV) % LANE == 0
            in_cols.append(ic)
            grads.append(dw if ic else (_blocks_cols(dw) if on == 'cols' else _blocks_rows(dw)))
        n = len(grads)
        lands = [lax.empty((N_CHIPS, dw.shape[0], dw.shape[1] // N_DEV) if ic else (N_CHIPS,) + dw.shape[1:], BF16)
                 for dw, ic in zip(grads, in_cols)]
        send, recv, bufs, tok = _split_start(grads + lands, _plan_scatter_pair(in_cols), N_CHIPS * n,
                                             name=f"scatter_pair_start_{tag}", deps=deps)
        return dict(items=items, n=n, tag=tag, in_cols=in_cols, h=(send, recv, bufs)), tok

    def scatter_forward(g, after):
        n, tag = g['n'], g['tag']
        send, recv, bufs = g['h']
        bufs = _split_wait(send, recv, bufs, _plan_scatter_pair(g['in_cols']), after, name=f"scatter_pair_wait_{tag}")
        sums = [_pair_add(b, a_, core, ic, name=f"pair_add_{tag}_{i}")
                for i, (b, a_, ic) in enumerate(zip(bufs[:n], bufs[n:], g['in_cols']))]
        lands = [lax.empty((N_CHIPS - 1,) + s_.shape[1:], BF16) for s_ in sums]
        send, recv, bufs, tok = _split_start(sums + lands, _plan_scatter_chips(n), (N_CHIPS - 1) * n,
                                             name=f"scatter_chips_start_{tag}")
        g['h'] = (send, recv, bufs)
        return tok

    def scatter_finish(g, after):
        n, tag = g['n'], g['tag']
        send, recv, bufs = g['h']
        bufs = _split_wait(send, recv, bufs, _plan_scatter_chips(n), after, name=f"scatter_chips_wait_{tag}")
        for i, (wname, idx, _, _) in enumerate(g['items']):
            outs[wname] = _adamw_sharded(p[wname], p['m_' + wname], p['v_' + wname], bufs[i], bufs[n + i], chip,
                                         idx, outs.get(wname), name=f"adamw_{tag}_{i}")

    older = []
    for layer in reversed(range(DEPTH)):
        w_in, w_out, w_up, w_down = big[layer]
        s = saved[layer]
        tag = f"l{layer}"
        du = _matmul(dxb, w_down, tb=True, epi="drelu2", extra=s['u'], out_dtype=BF16, name=tag + "_down_dx")
        dw_down = _matmul(s['a'], dxb, ta=True, out_dtype=BF16, name=tag + "_down_dw")
        dhn2 = _matmul(du, w_up, tb=True, name=tag + "_up_dx")
        dw_up = _matmul(s['hn2'], du, ta=True, out_dtype=BF16, name=tag + "_up_dw")
        dx, dxb, dn2 = _rmsnorm_bwd(s['x_mid'], sm['norm2'][layer][None], dhn2, dx, name=tag + "_norm2_bwd")
        mlp_group, tok = scatter_start([('mlp_w_up', layer, dw_up, 'cols'),
                                        ('mlp_w_down', layer, dw_down, 'rows')], (), f"mlp_l{layer}")
        dw_out = _matmul(s['og'], dxb, ta=True, out_dtype=BF16, name=tag + "_out_dw", deps=(tok,))
        tok = scatter_forward(mlp_group, dw_out)
        dog = _matmul(dxb, w_out, tb=True, name=tag + "_out_dx", deps=(tok,))
        started = []

        def on_dw(dw_in):
            group, tok_ = scatter_start([(MIXER_IN[layer], 0, dw_in, 'cols'),
                                         (MIXER_OUT[layer], 0, dw_out, 'rows')], (), f"mix_l{layer}")
            started.append(group)
            return (tok_,)

        dhn, sg = _mixer_bwd(layer, s['hn'], w_in, dog, s['sv'], tables, on_dw)
        mixer_group, = started
        small_grads.update(sg)
        if layer > 0:
            tok = scatter_forward(mixer_group, dhn)
            dx, dxb, dn1 = _rmsnorm_bwd(s['x'], sm['norm1'][layer][None], dhn, dx, name=tag + "_norm1_bwd", deps=(tok,))
        else:
            tok = gather_forward('small_grads', 'early', dhn)
            dx, dxb, dn1 = _rmsnorm_bwd(s['x'], sm['norm1'][layer][None], dhn, dx, name=tag + "_norm1_bwd", deps=(tok,))
        dnorm1[layer], dnorm2[layer] = dn1[0], dn2[0]
        for g in older:
            scatter_finish(g, dx)
        older = [mlp_group, mixer_group]
        if layer == 1:
            parts = [_pack([small_grads[n] for n in EARLY_SMALL])] + [small_grads[n].reshape(-1, LANE) for n in MEDIUM]
            lands = [_insert_block(lax.empty((N_DEV,) + s_.shape, F32), s_, device, False, name=f"own_small_grads{i}")
                     for i, s_ in enumerate(parts)]
            n_parts = len(parts)
            send, recv, bufs, _ = _split_start(parts + lands, _plan_gather_a([False] * n_parts), 4 * n_parts,
                                               name="gather_a_start_small_grads", deps=(dx,))
            gathers['small_grads', 'early'] = dict(n=n_parts, h=(send, recv, bufs), in_cols=[False] * n_parts,
                                                   full_of=[lambda l: l] * n_parts)
    small_grads['norm1'] = jnp.stack(dnorm1)
    small_grads['norm2'] = jnp.stack(dnorm2)

    late_parts, = _all_gather([_pack([small_grads[n] for n in LATE_SMALL])], name="gather_small_grads")
    last_token = scatter_forward(mixer_group, late_parts)
    early_parts, *medium_parts = gather_finish('small_grads', 'early', late_parts)
    early_sum = _sum_parts(early_parts, name="sum_small_grads_early", deps=(last_token,))
    late_sum = _sum_parts(late_parts, name="sum_small_grads_late")
    for g in older:
        scatter_finish(g, late_sum)
    for n, part in zip(MEDIUM, medium_parts):
        g = _sum_parts(part, name=f"sum_{n}")
        res = _adamw(p[n].reshape(-1, LANE), p['m_' + n].reshape(-1, LANE), p['v_' + n].reshape(-1, LANE), [g],
                     name=f"adamw_{n}")
        outs[n] = tuple(r.reshape(p[n].shape) for r in res)
    by_name = dict(zip(EARLY_SMALL, _unpack(early_sum, [small_grads[n].shape for n in EARLY_SMALL])))
    by_name.update(zip(LATE_SMALL, _unpack(late_sum, [small_grads[n].shape for n in LATE_SMALL])))
    packed_names = EARLY_SMALL + LATE_SMALL
    local_g = []
    for n in packed_names:
        g = by_name[n]
        if SMALL[n]:
            width = p[n].shape[-1]
            g = lax.dynamic_slice_in_dim(g, dev * width, width, axis=g.ndim - 1)
        local_g.append(g.reshape(p[n].shape))
    res = _adamw(_pack([p[n] for n in packed_names]), _pack([p['m_' + n] for n in packed_names]),
                 _pack([p['v_' + n] for n in packed_names]), [_pack(local_g)], name="adamw_small")
    local_shapes = [p[n].shape for n in packed_names]
    unpacked = [_unpack(r, local_shapes) for r in res]
    for i, n in enumerate(packed_names):
        outs[n] = tuple(unpacked[k][i] for k in range(4))

    result = [loss, dx[None]]
    for k in range(4):
        result += [outs[n][k] for n in WEIGHT_NAMES]
    return tuple(result)


def kernel(x, norm1, norm2, final_norm, ret_w_in, ret_gn_gain, ret_w_out, gdn_w_in, gdn_conv_w, gdn_a_log, gdn_dt_bias, gdn_norm_gain, gdn_w_out, gla_w_in, gla_w_gate_up, gla_gate_bias, gla_norm_gain, gla_w_out, lru_w_in, lru_conv_w, lru_conv_b, lru_w_rgate, lru_b_rgate, lru_w_igate, lru_b_igate, lru_lambda, lru_w_out, mlp_w_up, mlp_w_down, loss_target, m_norm1, m_norm2, m_final_norm, m_ret_w_in, m_ret_gn_gain, m_ret_w_out, m_gdn_w_in, m_gdn_conv_w, m_gdn_a_log, m_gdn_dt_bias, m_gdn_norm_gain, m_gdn_w_out, m_gla_w_in, m_gla_w_gate_up, m_gla_gate_bias, m_gla_norm_gain, m_gla_w_out, m_lru_w_in, m_lru_conv_w, m_lru_conv_b, m_lru_w_rgate, m_lru_b_rgate, m_lru_w_igate, m_lru_b_igate, m_lru_lambda, m_lru_w_out, m_mlp_w_up, m_mlp_w_down, v_norm1, v_norm2, v_final_norm, v_ret_w_in, v_ret_gn_gain, v_ret_w_out, v_gdn_w_in, v_gdn_conv_w, v_gdn_a_log, v_gdn_dt_bias, v_gdn_norm_gain, v_gdn_w_out, v_gla_w_in, v_gla_w_gate_up, v_gla_gate_bias, v_gla_norm_gain, v_gla_w_out, v_lru_w_in, v_lru_conv_w, v_lru_conv_b, v_lru_w_rgate, v_lru_b_rgate, v_lru_w_igate, v_lru_b_igate, v_lru_lambda, v_lru_w_out, v_mlp_w_up, v_mlp_w_down):
    return _step(x, norm1, norm2, final_norm, ret_w_in, ret_gn_gain, ret_w_out, gdn_w_in, gdn_conv_w, gdn_a_log, gdn_dt_bias, gdn_norm_gain, gdn_w_out, gla_w_in, gla_w_gate_up, gla_gate_bias, gla_norm_gain, gla_w_out, lru_w_in, lru_conv_w, lru_conv_b, lru_w_rgate, lru_b_rgate, lru_w_igate, lru_b_igate, lru_lambda, lru_w_out, mlp_w_up, mlp_w_down, loss_target, m_norm1, m_norm2, m_final_norm, m_ret_w_in, m_ret_gn_gain, m_ret_w_out, m_gdn_w_in, m_gdn_conv_w, m_gdn_a_log, m_gdn_dt_bias, m_gdn_norm_gain, m_gdn_w_out, m_gla_w_in, m_gla_w_gate_up, m_gla_gate_bias, m_gla_norm_gain, m_gla_w_out, m_lru_w_in, m_lru_conv_w, m_lru_conv_b, m_lru_w_rgate, m_lru_b_rgate, m_lru_w_igate, m_lru_b_igate, m_lru_lambda, m_lru_w_out, m_mlp_w_up, m_mlp_w_down, v_norm1, v_norm2, v_final_norm, v_ret_w_in, v_ret_gn_gain, v_ret_w_out, v_gdn_w_in, v_gdn_conv_w, v_gdn_a_log, v_gdn_dt_bias, v_gdn_norm_gain, v_gdn_w_out, v_gla_w_in, v_gla_w_gate_up, v_gla_gate_bias, v_gla_norm_gain, v_gla_w_out, v_lru_w_in, v_lru_conv_w, v_lru_conv_b, v_lru_w_rgate, v_lru_b_rgate, v_lru_w_igate, v_lru_b_igate, v_lru_lambda, v_lru_w_out, v_mlp_w_up, v_mlp_w_down)
```

```python
import functools
import math

import numpy as np
import jax
import jax.numpy as jnp
from jax import lax
from jax.experimental import pallas as pl
from jax.experimental.pallas import tpu as pltpu

F32 = jnp.float32
BF16 = jnp.bfloat16

D_MODEL = 2048
DEPTH = 4
CHUNK = 64
D_FF = 4 * D_MODEL
NORM_EPS = 1e-6
N_DEV = 8

RET_HEADS, RET_DK, RET_DV = 8, 256, 512
RET_HB = 4
GDN_HEADS, GDN_DK, GDN_DV = 16, 128, 128
GDN_HB = 8
GDN_QKV = GDN_HEADS * (2 * GDN_DK + GDN_DV)
CONV_WIDTH = 4
GLA_HEADS, GLA_DK, GLA_DV = 4, 256, 512
GLA_HB = 4
GLA_GATE_RANK = 16
GLA_TAU = 16.0
LRU_WIDTH, LRU_BLOCKS, LRU_BLOCK = 2048, 16, 128
LRU_C = 8.0
ROPE_BASE = 10000.0

ADAM_LR, ADAM_B1, ADAM_B2, ADAM_EPS, ADAM_WD, ADAM_STEP = 0.001, 0.9, 0.999, 1e-08, 0.01, 10

LANE = 128
VMEM_LIMIT = 48 * 1024 * 1024

NN = (((1,), (0,)), ((), ()))
NT = (((1,), (1,)), ((), ()))
TN = (((0,), (0,)), ((), ()))


def _params(sem=None):
    return pltpu.CompilerParams(dimension_semantics=sem, vmem_limit_bytes=VMEM_LIMIT)


def _bdot(a, b, dn=NN):
    return lax.dot_general(a.astype(BF16), b.astype(BF16), dn, preferred_element_type=F32)


def _split(x):
    hi = x.astype(BF16)
    lo = (x - hi.astype(F32)).astype(BF16)
    return hi, lo


def _fdot(a, b, dn=NN):
    a1, a2 = _split(a)
    b1, b2 = _split(b)
    d = functools.partial(lax.dot_general, dimension_numbers=dn, preferred_element_type=F32)
    return d(a1, b1) + (d(a1, b2) + d(a2, b1))


def _sigmoid(x):
    return 1.0 / (1.0 + jnp.exp(-x))


def _softplus(x):
    return jnp.maximum(x, 0.0) + jnp.log(1.0 + jnp.exp(-jnp.abs(x)))


def _silu(x):
    return x * _sigmoid(x)


def _dsilu(x):
    s = _sigmoid(x)
    return s * (1.0 + x * (1.0 - s))


GELU_C = math.sqrt(2.0 / math.pi)


def _gelu(x):
    return 0.5 * x * (1.0 + jnp.tanh(GELU_C * (x + 0.044715 * x * x * x)))


def _dgelu(x):
    t = jnp.tanh(GELU_C * (x + 0.044715 * x * x * x))
    return 0.5 * (1.0 + t) + 0.5 * x * (1.0 - t * t) * GELU_C * (1.0 + 3.0 * 0.044715 * x * x)


def _expm1(x):
    poly = x * (1.0 + x * 0.5 * (1.0 + x * (1.0 / 3.0) * (1.0 + x * 0.25 * (1.0 + x * 0.2))))
    return jnp.where(jnp.abs(x) < 0.05, poly, jnp.exp(x) - 1.0)


def _iota2(shape, axis):
    return lax.broadcasted_iota(jnp.int32, shape, axis)


def _col_to_row(col):
    n = col.shape[0]
    eye = _iota2((n, n), 0) == _iota2((n, n), 1)
    return jnp.sum(jnp.where(eye, col, 0.0), axis=0, keepdims=True)


def _row_to_col(row):
    n = row.shape[1]
    eye = _iota2((n, n), 0) == _iota2((n, n), 1)
    return jnp.sum(jnp.where(eye, row, 0.0), axis=1, keepdims=True)


def _pick_row(x, r):
    rows = _iota2(x.shape, 0)
    return jnp.sum(jnp.where(rows == r, x, 0.0), axis=0, keepdims=True)


def _shift_down(x, s):
    if s == 0:
        return x
    y = pltpu.roll(x, s, 0)
    return jnp.where(_iota2(x.shape, 0) < s, 0.0, y)


def _shift_up(x, s):
    if s == 0:
        return x
    n = x.shape[0]
    y = pltpu.roll(x, n - s, 0)
    return jnp.where(_iota2(x.shape, 0) >= n - s, 0.0, y)


def _tile(dim, prefs):
    for p in prefs:
        if dim % p == 0:
            return p
    return dim


TOKEN_SHAPE = (8, LANE)


def _dep_specs(deps):
    return [pl.BlockSpec(TOKEN_SHAPE, lambda *_: (0, 0)) for _ in deps]


def _matmul(a, b, *, ta=False, tb=False, epi="none", extra=None, out_dtype=F32, name, deps=(), b_cols=None):
    if ta:
        K, M = a.shape
    else:
        M, K = a.shape
    if tb:
        N, K2 = b.shape[0], b_cols or b.shape[1]
    else:
        K2, N = b.shape[0], b_cols or b.shape[1]
    assert K == K2, (a.shape, b.shape, ta, tb)
    if K <= 2048:
        tk = K
        if N <= 2048:
            tm, tn = _tile(M, (512, 256, 128)), N
        else:
            tm, tn = _tile(M, (1024, 512, 256, 128)), _tile(N, (512, 256, 128))
    else:
        tm, tn, tk = (_tile(d, (1024, 512, 256, 128)) for d in (M, N, K))
    nk = K // tk
    dn = (((0 if ta else 1,), (1 if tb else 0,)), ((), ()))
    n_extra = 0 if extra is None else 1
    n_out = 2 if epi == "relu2" else 1

    def body(*refs):
        a_ref, b_ref = refs[0], refs[1]
        e_ref = refs[2] if n_extra else None
        outs = refs[2 + n_extra + len(deps):2 + n_extra + len(deps) + n_out]

        def finish(r):
            if epi == "none":
                outs[0][...] = r.astype(outs[0].dtype)
            elif epi == "add":
                outs[0][...] = (r + e_ref[...]).astype(outs[0].dtype)
            elif epi == "relu2":
                outs[0][...] = r
                p = jnp.maximum(r, 0.0)
                outs[1][...] = (p * p).astype(outs[1].dtype)
            elif epi == "drelu2":
                outs[0][...] = (r * 2.0 * jnp.maximum(e_ref[...], 0.0)).astype(outs[0].dtype)

        def product():
            return lax.dot_general(a_ref[...].astype(BF16), b_ref[...].astype(BF16), dn, preferred_element_type=F32)

        if nk == 1:
            finish(product())
            return
        acc = refs[-1]
        k = pl.program_id(2)

        @pl.when(k == 0)
        def _():
            acc[...] = jnp.zeros_like(acc)

        acc[...] += product()

        @pl.when(k == nk - 1)
        def _():
            finish(acc[...])

    a_spec = pl.BlockSpec((tk, tm), lambda i, j, k: (k, i)) if ta else pl.BlockSpec((tm, tk), lambda i, j, k: (i, k))
    b_spec = pl.BlockSpec((tn, tk), lambda i, j, k: (j, k)) if tb else pl.BlockSpec((tk, tn), lambda i, j, k: (k, j))
    o_spec = pl.BlockSpec((tm, tn), lambda i, j, k: (i, j))
    in_specs = [a_spec, b_spec] + ([o_spec] if n_extra else []) + _dep_specs(deps)
    if epi == "relu2":
        out_shape = (jax.ShapeDtypeStruct((M, N), F32), jax.ShapeDtypeStruct((M, N), BF16))
        out_specs = (o_spec, o_spec)
    else:
        out_shape = jax.ShapeDtypeStruct((M, N), out_dtype)
        out_specs = o_spec
    args = (a, b) + ((extra,) if n_extra else ()) + tuple(deps)
    return pl.pallas_call(
        body, name=name, grid=(M // tm, N // tn, nk), in_specs=in_specs, out_specs=out_specs,
        out_shape=out_shape, scratch_shapes=[pltpu.VMEM((tm, tn), F32)] if nk > 1 else [],
        compiler_params=_params(("parallel", "parallel", "arbitrary")),
    )(*args)


ROW_BLOCK = 256


def _rmsnorm_fwd(x, g, *, name, deps=()):
    T, D = x.shape
    tr = _tile(T, (ROW_BLOCK, 128, 64))

    def body(x_ref, g_ref, *rest):
        o_ref = rest[-1]
        xv = x_ref[...]
        r = lax.rsqrt(jnp.mean(xv * xv, axis=-1, keepdims=True) + NORM_EPS)
        o_ref[...] = (xv * r * g_ref[...]).astype(o_ref.dtype)

    return pl.pallas_call(
        body, name=name, grid=(T // tr,),
        in_specs=[pl.BlockSpec((tr, D), lambda i: (i, 0)), pl.BlockSpec((1, D), lambda i: (0, 0))] + _dep_specs(deps),
        out_specs=pl.BlockSpec((tr, D), lambda i: (i, 0)),
        out_shape=jax.ShapeDtypeStruct((T, D), BF16), compiler_params=_params(("parallel",)),
    )(x, g, *deps)


def _rmsnorm_bwd(x, g, dy, dres, *, name, deps=()):
    T, D = x.shape
    tr = _tile(T, (ROW_BLOCK, 128, 64))

    def body(x_ref, g_ref, dy_ref, dres_ref, *rest):
        dx_ref, dxb_ref, dg_ref = rest[len(deps):]
        i = pl.program_id(0)
        xv = x_ref[...]
        r = lax.rsqrt(jnp.mean(xv * xv, axis=-1, keepdims=True) + NORM_EPS)
        xh = xv * r
        dyv = dy_ref[...].astype(F32)
        dxh = dyv * g_ref[...]
        dx = dres_ref[...] + r * (dxh - xh * jnp.mean(dxh * xh, axis=-1, keepdims=True))
        dx_ref[...] = dx
        dxb_ref[...] = dx.astype(dxb_ref.dtype)

        @pl.when(i == 0)
        def _():
            dg_ref[...] = jnp.zeros_like(dg_ref)

        dg_ref[...] += jnp.sum(dyv * xh, axis=0, keepdims=True)

    blk = pl.BlockSpec((tr, D), lambda i: (i, 0))
    vec = pl.BlockSpec((1, D), lambda i: (0, 0))
    return pl.pallas_call(
        body, name=name, grid=(T // tr,), in_specs=[blk, vec, blk, blk] + _dep_specs(deps), out_specs=(blk, blk, vec),
        out_shape=(jax.ShapeDtypeStruct((T, D), F32), jax.ShapeDtypeStruct((T, D), BF16),
                   jax.ShapeDtypeStruct((1, D), F32)),
        compiler_params=_params(("arbitrary",)),
    )(x, g, dy, dres, *deps)


def _final_loss_bwd(x, g, target, *, name):
    T, D = x.shape
    tr = _tile(T, (ROW_BLOCK, 128, 64))

    def body(x_ref, g_ref, t_ref, dx_ref, dxb_ref, dg_ref, l_ref):
        i = pl.program_id(0)
        xv = x_ref[...]
        r = lax.rsqrt(jnp.mean(xv * xv, axis=-1, keepdims=True) + NORM_EPS)
        xh = xv * r
        gv = g_ref[...]
        err = xh * gv - t_ref[...]
        dy = err * (1.0 / D)
        dxh = dy * gv
        dx = r * (dxh - xh * jnp.mean(dxh * xh, axis=-1, keepdims=True))
        dx_ref[...] = dx
        dxb_ref[...] = dx.astype(dxb_ref.dtype)

        @pl.when(i == 0)
        def _():
            l_ref[...] = jnp.zeros_like(l_ref)
            dg_ref[...] = jnp.zeros_like(dg_ref)

        dg_ref[...] += jnp.sum(dy * xh, axis=0, keepdims=True)
        part = 0.5 * jnp.sum(jnp.mean(err * err, axis=-1, keepdims=True), axis=0, keepdims=True)
        l_ref[...] += jnp.broadcast_to(part, l_ref.shape)

    blk = pl.BlockSpec((tr, D), lambda i: (i, 0))
    vec = pl.BlockSpec((1, D), lambda i: (0, 0))
    return pl.pallas_call(
        body, name=name, grid=(T // tr,), in_specs=[blk, vec, blk],
        out_specs=(blk, blk, vec, pl.BlockSpec((1, LANE), lambda i: (0, 0))),
        out_shape=(jax.ShapeDtypeStruct((T, D), F32), jax.ShapeDtypeStruct((T, D), BF16),
                   jax.ShapeDtypeStruct((1, D), F32), jax.ShapeDtypeStruct((1, LANE), F32)),
        compiler_params=_params(("arbitrary",)),
    )(x, g, target)


def _colsum(x, *, name):
    T, C = x.shape
    tc = _tile(C, (512, 256, 128))

    def body(x_ref, o_ref):
        o_ref[...] = jnp.sum(x_ref[...], axis=0, keepdims=True)

    return pl.pallas_call(
        body, name=name, grid=(C // tc,), in_specs=[pl.BlockSpec((T, tc), lambda j: (0, j))],
        out_specs=pl.BlockSpec((1, tc), lambda j: (0, j)),
        out_shape=jax.ShapeDtypeStruct((1, C), F32), compiler_params=_params(("parallel",)),
    )(x)


ADAM_C1 = 1.0 - ADAM_B1 ** ADAM_STEP
ADAM_C2 = 1.0 - ADAM_B2 ** ADAM_STEP


def _adamw(w, m, v, grads, *, name):
    R, C = w.shape
    tr = _tile(R, (256, 128, 64, 32, 16, 8))
    n_g = len(grads)

    def body(*refs):
        w_ref, m_ref, v_ref = refs[:3]
        g_refs = refs[3:3 + n_g]
        g_out, d_out, m_out, v_out = refs[3 + n_g:]
        g = g_refs[0][...].astype(F32)
        for r in g_refs[1:]:
            g = g + r[...].astype(F32)
        mn = ADAM_B1 * m_ref[...] + (1.0 - ADAM_B1) * g
        vn = ADAM_B2 * v_ref[...] + (1.0 - ADAM_B2) * (g * g)
        m_hat = mn / ADAM_C1
        v_hat = vn / ADAM_C2
        g_out[...] = g
        d_out[...] = -ADAM_LR * (m_hat / (jnp.sqrt(v_hat) + ADAM_EPS) + ADAM_WD * w_ref[...])
        m_out[...] = mn
        v_out[...] = vn

    blk = pl.BlockSpec((tr, C), lambda i: (i, 0))
    sh = jax.ShapeDtypeStruct((R, C), F32)
    return pl.pallas_call(
        body, name=name, grid=(R // tr,), in_specs=[blk] * (3 + n_g), out_specs=(blk,) * 4,
        out_shape=(sh,) * 4, compiler_params=_params(("parallel",)),
    )(w, m, v, *grads)


def _sum_parts(parts, *, name, deps=()):
    P, R, C = parts.shape
    tr = _tile(R, (256, 128, 64, 32, 16, 8))

    def body(p_ref, *rest):
        o_ref = rest[-1]
        s = p_ref[0].astype(F32)
        for i in range(1, P):
            s = s + p_ref[i].astype(F32)
        o_ref[...] = s

    return pl.pallas_call(
        body, name=name, grid=(R // tr,),
        in_specs=[pl.BlockSpec((P, tr, C), lambda i: (0, i, 0))] + _dep_specs(deps),
        out_specs=pl.BlockSpec((tr, C), lambda i: (i, 0)),
        out_shape=jax.ShapeDtypeStruct((R, C), F32), compiler_params=_params(("parallel",)),
    )(parts, *deps)


def _ret_tables(T):
    H, C = RET_HEADS, CHUNK
    log_gamma = jnp.log1p(-jnp.exp2(-5.0 - jnp.arange(H, dtype=F32)))
    pos = jnp.arange(C, dtype=F32)
    dist = jnp.abs(pos[:, None] - pos[None, :])
    dm = jnp.exp(log_gamma[:, None, None] * dist)
    qdec = jnp.exp(log_gamma[:, None] * (pos + 1.0))[:, :, None]
    kdec = jnp.exp(log_gamma[:, None] * (C - 1.0 - pos))[:, :, None]
    cdec = jnp.exp(log_gamma * C)[:, None, None]
    inv = ROPE_BASE ** (-jnp.arange(0, RET_DK, 2, dtype=F32) / RET_DK)
    ang = jnp.arange(T, dtype=F32)[:, None] * inv[None, :]
    return dm, qdec, kdec, cdec, jnp.cos(ang), jnp.sin(ang)


def _rot(x, cos, sin):
    h = x.shape[1] // 2
    x1, x2 = x[:, :h], x[:, h:]
    return jnp.concatenate([x1 * cos - x2 * sin, x1 * sin + x2 * cos], axis=1)


def _unrot(dy, cos, sin):
    h = dy.shape[1] // 2
    d1, d2 = dy[:, :h], dy[:, h:]
    return jnp.concatenate([d1 * cos + d2 * sin, d2 * cos - d1 * sin], axis=1)


def _ret_specs(N, rev):
    H, C, DK, DV = RET_HEADS, CHUNK, RET_DK, RET_DV
    cn = (lambda n: N - 1 - n) if rev else (lambda n: n)
    HB, G = RET_HB, H // RET_HB
    q = pl.BlockSpec((C, HB * DK), lambda h, n: (cn(n), h))
    k = pl.BlockSpec((C, HB * DK), lambda h, n: (cn(n), G + h))
    v = pl.BlockSpec((C, HB * DV), lambda h, n: (cn(n), G + h))
    g = pl.BlockSpec((C, HB * DV), lambda h, n: (cn(n), 2 * G + h))
    cs = pl.BlockSpec((C, DK // 2), lambda h, n: (cn(n), 0))
    dm = pl.BlockSpec((HB, C, C), lambda h, n: (h, 0, 0))
    dec = pl.BlockSpec((HB, C, 1), lambda h, n: (h, 0, 0))
    cd = pl.BlockSpec((HB, 1, 1), lambda h, n: (h, 0, 0))
    gain = pl.BlockSpec((HB, 1, DV), lambda h, n: (h, 0, 0))
    st = pl.BlockSpec((HB, 1, DK, DV), lambda h, n: (h, cn(n), 0, 0))
    ov = pl.BlockSpec((C, HB * DV), lambda h, n: (cn(n), h))
    return q, k, v, g, cs, dm, dec, cd, gain, st, ov


def _ret_fwd(proj, gain, tables, *, name):
    T = proj.shape[0]
    H, C, DK, DV = RET_HEADS, CHUNK, RET_DK, RET_DV
    N = T // C
    dm_t, qdec_t, kdec_t, cdec_t, cos_t, sin_t = tables

    def body(q_ref, k_ref, v_ref, g_ref, cos_ref, sin_ref, dm_ref, qd_ref, kd_ref, cd_ref, gain_ref,
             og_ref, st_ref, s_acc):
        n = pl.program_id(1)

        @pl.when(n == 0)
        def _():
            s_acc[...] = jnp.zeros_like(s_acc)

        cos, sin = cos_ref[...], sin_ref[...]

        def head(hb):
            kc, vc = pl.ds(hb * DK, DK), pl.ds(hb * DV, DV)
            qr = _rot(q_ref[:, kc], cos, sin)
            kr = _rot(k_ref[:, kc], cos, sin) * (DK ** -0.5)
            v = v_ref[:, vc]
            sp = s_acc[hb]
            st_ref[hb, 0] = sp.astype(st_ref.dtype)
            scores = _bdot(qr, kr, NT) * dm_ref[hb]
            inter = _bdot(qr * qd_ref[hb], sp)
            s_acc[hb] = sp * cd_ref[hb] + _bdot(kr * kd_ref[hb], v, TN)
            yield
            o = _bdot(scores, v) + inter
            yield
            oc = o - jnp.mean(o, axis=-1, keepdims=True)
            rstd = lax.rsqrt(jnp.mean(oc * oc, axis=-1, keepdims=True) + NORM_EPS)
            og_ref[:, vc] = (oc * rstd * gain_ref[hb] * _silu(g_ref[:, vc])).astype(og_ref.dtype)

        _interleave([head(hb) for hb in range(RET_HB)])

    q, k, v, g, cs, dm, dec, cd, gn, st, ov = _ret_specs(N, False)
    return pl.pallas_call(
        body, name=name, grid=(H // RET_HB, N),
        in_specs=[q, k, v, g, cs, cs, dm, dec, dec, cd, gn], out_specs=(ov, st),
        out_shape=(jax.ShapeDtypeStruct((T, H * DV), BF16), jax.ShapeDtypeStruct((H, N, DK, DV), BF16)),
        scratch_shapes=[pltpu.VMEM((RET_HB, DK, DV), F32)],
        compiler_params=_params(("arbitrary", "arbitrary")),
    )(proj, proj, proj, proj, cos_t, sin_t, dm_t, qdec_t, kdec_t, cdec_t, gain)


def _ret_bwd(proj, gain, tables, states, dog, *, name):
    T = proj.shape[0]
    H, C, DK, DV = RET_HEADS, CHUNK, RET_DK, RET_DV
    N = T // C
    dm_t, qdec_t, kdec_t, cdec_t, cos_t, sin_t = tables

    def body(q_ref, k_ref, v_ref, g_ref, cos_ref, sin_ref, dm_ref, qd_ref, kd_ref, cd_ref, gain_ref,
             st_ref, dog_ref, dq_ref, dk_ref, dv_ref, dg_ref, dgain_ref, ds_acc):
        n = pl.program_id(1)

        @pl.when(n == 0)
        def _():
            ds_acc[...] = jnp.zeros_like(ds_acc)
            dgain_ref[...] = jnp.zeros_like(dgain_ref)

        cos, sin = cos_ref[...], sin_ref[...]
        scale = DK ** -0.5

        def head(hb):
            kc, vc = pl.ds(hb * DK, DK), pl.ds(hb * DV, DV)
            qr = _rot(q_ref[:, kc], cos, sin)
            kr = _rot(k_ref[:, kc], cos, sin) * scale
            v = v_ref[:, vc]
            g = g_ref[:, vc]
            sp = st_ref[hb, 0]
            dm = dm_ref[hb]
            qd, kd = qd_ref[hb], kd_ref[hb]
            gain_v = gain_ref[hb]
            scores = _bdot(qr, kr, NT) * dm
            inter = _bdot(qr * qd, sp)
            yield
            o = _bdot(scores, v) + inter
            yield
            oc = o - jnp.mean(o, axis=-1, keepdims=True)
            rstd = lax.rsqrt(jnp.mean(oc * oc, axis=-1, keepdims=True) + NORM_EPS)
            oh = oc * rstd
            dy = dog_ref[:, vc].astype(F32)
            dg_ref[:, vc] = (dy * oh * gain_v * _dsilu(g)).astype(dg_ref.dtype)
            dnorm = dy * _silu(g)
            dgain_ref[hb] += jnp.sum(dnorm * oh, axis=0, keepdims=True)
            doh = dnorm * gain_v
            do = rstd * (doh - jnp.mean(doh, axis=-1, keepdims=True)
                         - oh * jnp.mean(doh * oh, axis=-1, keepdims=True))
            dsn = ds_acc[hb]
            dp = _bdot(do, v, NT) * dm
            dq_inter = _bdot(do, sp, NT) * qd
            dk_inter = _bdot(v, dsn, NT) * kd
            dv_ref[:, vc] = (_bdot(scores, do, TN) + _bdot(kr * kd, dsn)).astype(dv_ref.dtype)
            ds_acc[hb] = dsn * cd_ref[hb] + _bdot(qr * qd, do, TN)
            yield
            dqr = _bdot(dp, kr) + dq_inter
            dkr = _bdot(dp, qr, TN) + dk_inter
            yield
            dq_ref[:, kc] = _unrot(dqr, cos, sin).astype(dq_ref.dtype)
            dk_ref[:, kc] = _unrot(dkr * scale, cos, sin).astype(dk_ref.dtype)

        _interleave([head(hb) for hb in range(RET_HB)])

    q, k, v, g, cs, dm, dec, cd, gn, st, ov = _ret_specs(N, True)
    return pl.pallas_call(
        body, name=name, grid=(H // RET_HB, N),
        in_specs=[q, k, v, g, cs, cs, dm, dec, dec, cd, gn, st, ov],
        out_specs=(q, q, ov, ov, gn),
        out_shape=(jax.ShapeDtypeStruct((T, H * DK), BF16), jax.ShapeDtypeStruct((T, H * DK), BF16),
                   jax.ShapeDtypeStruct((T, H * DV), BF16), jax.ShapeDtypeStruct((T, H * DV), BF16),
                   jax.ShapeDtypeStruct((H, 1, DV), F32)),
        scratch_shapes=[pltpu.VMEM((RET_HB, DK, DV), F32)],
        compiler_params=_params(("arbitrary", "arbitrary")),
    )(proj, proj, proj, proj, cos_t, sin_t, dm_t, qdec_t, kdec_t, cdec_t, gain, states, dog)


def _gla_specs(N, rev):
    H, C, DK, DV = GLA_HEADS, CHUNK, GLA_DK, GLA_DV
    cn = (lambda n: N - 1 - n) if rev else (lambda n: n)
    HB, G = GLA_HB, H // GLA_HB
    q = pl.BlockSpec((C, HB * DK), lambda h, n: (cn(n), h))
    k = pl.BlockSpec((C, HB * DK), lambda h, n: (cn(n), G + h))
    v = pl.BlockSpec((C, HB * DV), lambda h, n: (cn(n), G + h))
    r = pl.BlockSpec((C, HB * DV), lambda h, n: (cn(n), 2 * G + h))
    bias = pl.BlockSpec((1, HB * DK), lambda h, n: (0, h))
    gain = pl.BlockSpec((HB, 1, DV), lambda h, n: (h, 0, 0))
    st = pl.BlockSpec((HB, 1, DV, DK), lambda h, n: (h, cn(n), 0, 0))
    ov = pl.BlockSpec((C, HB * DV), lambda h, n: (cn(n), h))
    return q, k, v, r, bias, gain, st, ov


def _gla_chunk(q, k, v, gl_raw, bias):
    C, DK = q.shape
    gl = gl_raw + bias
    la = -_softplus(-gl) * (1.0 / GLA_TAU)
    lower = _iota2((C, C), 0) >= _iota2((C, C), 1)
    cum = _fdot(jnp.where(lower, 1.0, 0.0), la)
    yield
    ref = _pick_row(cum, C // 2 - 1)
    clast = _pick_row(cum, C - 1)
    fw, bw = jnp.exp(cum - ref), jnp.exp(ref - cum)
    qs = q * (DK ** -0.5)
    s_lo = _bdot(qs * fw, k * bw, NT)
    s_up = _bdot(qs * bw, k * fw, NT)
    yield
    scores = jnp.where(lower, s_lo, s_up)
    return gl, cum, clast, fw, bw, qs, k, v, scores, lower


def _gla_fwd(proj, glogit, bias, gain, *, name):
    T = proj.shape[0]
    H, C, DK, DV = GLA_HEADS, CHUNK, GLA_DK, GLA_DV
    N = T // C

    def body(q_ref, k_ref, v_ref, r_ref, gl_ref, bias_ref, gain_ref, og_ref, st_ref, s_acc):
        n = pl.program_id(1)

        @pl.when(n == 0)
        def _():
            s_acc[...] = jnp.zeros_like(s_acc)

        def head(hb):
            kc, vc = pl.ds(hb * DK, DK), pl.ds(hb * DV, DV)
            gl, cum, clast, fw, bw, qs, k, v, scores, lower = yield from _gla_chunk(
                q_ref[:, kc], k_ref[:, kc], v_ref[:, vc], gl_ref[:, kc], bias_ref[:, kc])
            sp = s_acc[hb]
            st_ref[hb, 0] = sp
            o = _bdot(scores, v) + _bdot(qs * jnp.exp(cum), sp, NT)
            s_acc[hb] = sp * jnp.exp(clast) + _bdot(v, k * jnp.exp(clast - cum), TN)
            yield
            rstd = lax.rsqrt(jnp.mean(o * o, axis=-1, keepdims=True) + NORM_EPS)
            og_ref[:, vc] = (o * rstd * gain_ref[hb] * _silu(r_ref[:, vc])).astype(og_ref.dtype)

        _interleave([head(hb) for hb in range(GLA_HB)])

    q, k, v, r, bias_s, gn, st, ov = _gla_specs(N, False)
    return pl.pallas_call(
        body, name=name, grid=(H // GLA_HB, N), in_specs=[q, k, v, r, q, bias_s, gn], out_specs=(ov, st),
        out_shape=(jax.ShapeDtypeStruct((T, H * DV), BF16), jax.ShapeDtypeStruct((H, N, DV, DK), F32)),
        scratch_shapes=[pltpu.VMEM((GLA_HB, DV, DK), F32)],
        compiler_params=_params(("arbitrary", "arbitrary")),
    )(proj, proj, proj, proj, glogit, bias, gain)


def _gla_bwd(proj, glogit, bias, gain, states, dog, *, name):
    T = proj.shape[0]
    H, C, DK, DV = GLA_HEADS, CHUNK, GLA_DK, GLA_DV
    N = T // C

    def body(q_ref, k_ref, v_ref, r_ref, gl_ref, bias_ref, gain_ref, st_ref, dog_ref,
             dq_ref, dk_ref, dv_ref, dr_ref, dgl_ref, dgain_ref, ds_acc):
        n = pl.program_id(1)

        @pl.when(n == 0)
        def _():
            ds_acc[...] = jnp.zeros_like(ds_acc)
            dgain_ref[...] = jnp.zeros_like(dgain_ref)

        def head(hb):
            kc, vc = pl.ds(hb * DK, DK), pl.ds(hb * DV, DV)
            gl, cum, clast, fw, bw, qs, k, v, scores, lower = yield from _gla_chunk(
                q_ref[:, kc], k_ref[:, kc], v_ref[:, vc], gl_ref[:, kc], bias_ref[:, kc])
            sp = st_ref[hb, 0]
            ecum, e2, cdec = jnp.exp(cum), jnp.exp(clast - cum), jnp.exp(clast)
            q_in, k_end = qs * ecum, k * e2
            o = _bdot(scores, v) + _bdot(q_in, sp, NT)
            yield
            rstd = lax.rsqrt(jnp.mean(o * o, axis=-1, keepdims=True) + NORM_EPS)
            oh = o * rstd
            r = r_ref[:, vc]
            gain_v = gain_ref[hb]
            dy = dog_ref[:, vc].astype(F32)
            dr_ref[:, vc] = (dy * oh * gain_v * _dsilu(r)).astype(dr_ref.dtype)
            dnorm = dy * _silu(r)
            dgain_ref[hb] += jnp.sum(dnorm * oh, axis=0, keepdims=True)
            doh = dnorm * gain_v
            do = rstd * (doh - oh * jnp.mean(doh * oh, axis=-1, keepdims=True))
            dsn = ds_acc[hb]
            dq_in = _bdot(do, sp)
            dk_end = _bdot(v, dsn)
            dv_ref[:, vc] = (_bdot(k_end, dsn, NT) + _bdot(scores, do, TN)).astype(dv_ref.dtype)
            dcdec = jnp.sum(dsn * sp, axis=0, keepdims=True)
            ds_acc[hb] = dsn * cdec + _bdot(do, q_in, TN)
            dsc = _bdot(do, v, NT)
            yield
            ds_lo = jnp.where(lower, dsc, 0.0)
            ds_up = jnp.where(lower, 0.0, dsc)
            qf, kb, qb, kf = qs * fw, k * bw, qs * bw, k * fw
            dqf, dkb = _bdot(ds_lo, kb), _bdot(ds_lo, qf, TN)
            dqb, dkf = _bdot(ds_up, kf), _bdot(ds_up, qb, TN)
            yield
            dq_ref[:, kc] = ((dqf * fw + dqb * bw + dq_in * ecum) * (DK ** -0.5)).astype(dq_ref.dtype)
            dk_ref[:, kc] = (dkb * bw + dkf * fw + dk_end * e2).astype(dk_ref.dtype)
            dz = (dqf * qs + dkf * k) * fw - (dqb * qs + dkb * k) * bw
            kk = dk_end * k_end
            dcum = dz + dq_in * q_in - kk
            rows = _iota2((C, DK), 0)
            dcum = dcum + jnp.where(rows == C // 2 - 1, -jnp.sum(dz, axis=0, keepdims=True), 0.0)
            dcum = dcum + jnp.where(rows == C - 1, jnp.sum(kk, axis=0, keepdims=True) + dcdec * cdec, 0.0)
            upper = _iota2((C, C), 0) <= _iota2((C, C), 1)
            dla = _fdot(jnp.where(upper, 1.0, 0.0), dcum)
            yield
            dgl_ref[:, kc] = dla * (1.0 / GLA_TAU) * _sigmoid(-gl)

        _interleave([head(hb) for hb in range(GLA_HB)])

    q, k, v, r, bias_s, gn, st, ov = _gla_specs(N, True)
    return pl.pallas_call(
        body, name=name, grid=(H // GLA_HB, N), in_specs=[q, k, v, r, q, bias_s, gn, st, ov],
        out_specs=(q, q, ov, ov, q, gn),
        out_shape=(jax.ShapeDtypeStruct((T, H * DK), BF16), jax.ShapeDtypeStruct((T, H * DK), BF16),
                   jax.ShapeDtypeStruct((T, H * DV), BF16), jax.ShapeDtypeStruct((T, H * DV), BF16),
                   jax.ShapeDtypeStruct((T, H * DK), F32), jax.ShapeDtypeStruct((H, 1, DV), F32)),
        scratch_shapes=[pltpu.VMEM((GLA_HB, DV, DK), F32)],
        compiler_params=_params(("arbitrary", "arbitrary")),
    )(proj, proj, proj, proj, glogit, bias, gain, states, dog)


def _conv(xv, w_ref):
    out = _shift_down(xv, CONV_WIDTH - 1) * w_ref[0:1, :]
    for tap in range(1, CONV_WIDTH):
        out = out + _shift_down(xv, CONV_WIDTH - 1 - tap) * w_ref[tap:tap + 1, :]
    return out


def _conv_bwd(xv, w_ref, dpre, dw_ref):
    dx = None
    for tap in range(CONV_WIDTH):
        s = CONV_WIDTH - 1 - tap
        t = _shift_up(dpre, s) * w_ref[tap:tap + 1, :]
        dx = t if dx is None else dx + t
        dw_ref[tap:tap + 1, :] = jnp.sum(dpre * _shift_down(xv, s), axis=0, keepdims=True)
    return dx


CONV_COLS = 256


def _conv_silu_fwd(x, w, *, name):
    T = x.shape[0]
    n = w.shape[1]

    def body(x_ref, w_ref, o_ref):
        o_ref[...] = _silu(_conv(x_ref[...], w_ref))

    return pl.pallas_call(
        body, name=name, grid=(n // CONV_COLS,),
        in_specs=[pl.BlockSpec((T, CONV_COLS), lambda j: (0, j)), pl.BlockSpec((CONV_WIDTH, CONV_COLS), lambda j: (0, j))],
        out_specs=pl.BlockSpec((T, CONV_COLS), lambda j: (0, j)),
        out_shape=jax.ShapeDtypeStruct((T, n), F32), compiler_params=_params(("parallel",)),
    )(x, w)


def _conv_silu_bwd(x, w, dact, *, name):
    T = x.shape[0]
    n = w.shape[1]

    def body(x_ref, w_ref, da_ref, dx_ref, dw_ref):
        xv = x_ref[...]
        dpre = da_ref[...] * _dsilu(_conv(xv, w_ref))
        dx_ref[...] = _conv_bwd(xv, w_ref, dpre, dw_ref).astype(dx_ref.dtype)

    blk = pl.BlockSpec((T, CONV_COLS), lambda j: (0, j))
    wb = pl.BlockSpec((CONV_WIDTH, CONV_COLS), lambda j: (0, j))
    return pl.pallas_call(
        body, name=name, grid=(n // CONV_COLS,), in_specs=[blk, wb, blk], out_specs=(blk, wb),
        out_shape=(jax.ShapeDtypeStruct((T, n), BF16), jax.ShapeDtypeStruct((CONV_WIDTH, n), F32)),
        compiler_params=_params(("parallel",)),
    )(x, w, dact)


def _interleave(gens):
    results = [None] * len(gens)
    live = list(range(len(gens)))
    while live:
        for i in list(live):
            try:
                next(gens[i])
            except StopIteration as done:
                results[i] = done.value
                live.remove(i)
    return results


def _unit_lower_inverse(a):
    n = a.shape[0]
    eye = jnp.where(_iota2((n, n), 0) == _iota2((n, n), 1), 1.0, 0.0)
    p = -a
    t = eye + p
    for _ in range(5):
        p = _fdot(p, p)
        yield
        t = t + _fdot(t, p)
        yield
    return t


def _gdn_specs(N, rev):
    H, C, DK, DV = GDN_HEADS, CHUNK, GDN_DK, GDN_DV
    cn = (lambda n: N - 1 - n) if rev else (lambda n: n)
    HB, G = GDN_HB, H // GDN_HB
    q = pl.BlockSpec((C, HB * DK), lambda h, n: (cn(n), h))
    k = pl.BlockSpec((C, HB * DK), lambda h, n: (cn(n), G + h))
    v = pl.BlockSpec((C, HB * DV), lambda h, n: (cn(n), 2 * G + h))
    z = pl.BlockSpec((C, HB * DV), lambda h, n: (cn(n), 3 * G + h))
    gates = pl.BlockSpec((C, LANE), lambda h, n: (cn(n), 0))
    sc = pl.BlockSpec((HB, 1, 1), lambda h, n: (h, 0, 0))
    gain = pl.BlockSpec((1, DV), lambda h, n: (0, 0))
    st = pl.BlockSpec((HB, 1, DK, DV), lambda h, n: (h, cn(n), 0, 0))
    return q, k, v, z, gates, sc, gain, st


def _gdn_solved_specs(N, rev):
    C = CHUNK
    cn = (lambda n: N - 1 - n) if rev else (lambda n: n)
    return (pl.BlockSpec((GDN_HB, 1, C, C), lambda h, n: (h, cn(n), 0, 0)),
            pl.BlockSpec((GDN_HB, 1, C, GDN_DV + GDN_DK), lambda h, n: (h, cn(n), 0, 0)))


def _gdn_chunk(q_ref, k_ref, v_ref, gates_ref, alog_ref, dtb_ref, h, solved=None):
    H, C, DK, DV = GDN_HEADS, CHUNK, GDN_DK, GDN_DV
    gates = gates_ref[...]
    lane = _iota2(gates.shape, 1)
    bl = jnp.sum(jnp.where(lane == h, gates, 0.0), axis=1, keepdims=True)
    al = jnp.sum(jnp.where(lane == H + h, gates, 0.0), axis=1, keepdims=True)
    beta = _sigmoid(bl)
    ea = jnp.exp(alog_ref[...])
    xs = al + dtb_ref[...]
    la = -ea * _softplus(xs)
    ii, jj = _iota2((C, C), 0), _iota2((C, C), 1)
    strict = ii > jj
    cum_col = jnp.sum(jnp.where(ii >= jj, _col_to_row(la), 0.0), axis=1, keepdims=True)
    cum_row = jnp.sum(jnp.where(ii <= jj, la, 0.0), axis=0, keepdims=True)
    q, k, v = q_ref[...], k_ref[...], v_ref[...]
    rq = lax.rsqrt(jnp.sum(q * q, axis=-1, keepdims=True) + NORM_EPS)
    rk = lax.rsqrt(jnp.sum(k * k, axis=-1, keepdims=True) + NORM_EPS)
    qn = q * rq * (DK ** -0.5)
    kn = k * rk
    rel = jnp.where(strict, jnp.exp(jnp.where(strict, cum_col - cum_row, 0.0)), 0.0)
    rg = rel * _bdot(kn, kn, NT)
    yield
    a = beta * rg
    e_col = jnp.exp(cum_col)
    clast = _pick_row(cum_col, C - 1)
    if solved is None:
        tm = yield from _unit_lower_inverse(a)
        rhs = jnp.concatenate([beta * v, (beta * e_col) * kn], axis=1)
        sol = _fdot(tm, rhs)
        yield
    else:
        tm, sol = solved
    u, w = sol[:, :DV], sol[:, DV:]
    dd = jnp.exp(clast - cum_col)
    ke = kn * dd
    g = jnp.exp(clast)
    eye = jnp.where(_iota2((DK, DK), 0) == _iota2((DK, DK), 1), 1.0, 0.0)
    trans = g * eye - _bdot(ke, w, TN)
    inject = _bdot(ke, u, TN)
    yield
    return dict(beta=beta, ea=ea, xs=xs, la=la, strict=strict, ii=ii, jj=jj, q=q, k=k, v=v, rq=rq, rk=rk,
                qn=qn, kn=kn, rel=rel, rg=rg, a=a, tm=tm, e_col=e_col, sol=sol, u=u, w=w, dd=dd, ke=ke,
                g=g, eye=eye, trans=trans, inject=inject)


def _gdn_fwd(act, proj, gates, a_log, dt_bias, gain, *, name):
    T = act.shape[0]
    H, C, DK, DV = GDN_HEADS, CHUNK, GDN_DK, GDN_DV
    N = T // C

    def body(q_ref, k_ref, v_ref, z_ref, gates_ref, alog_ref, dtb_ref, gain_ref, og_ref, st_ref, tm_ref, sol_ref,
             s_acc):
        hg, n = pl.program_id(0), pl.program_id(1)

        @pl.when(n == 0)
        def _():
            s_acc[...] = jnp.zeros_like(s_acc)

        def head(hb):
            cols = pl.ds(hb * DK, DK)
            c = yield from _gdn_chunk(q_ref.at[:, cols], k_ref.at[:, cols], v_ref.at[:, cols], gates_ref,
                                      alog_ref.at[hb], dtb_ref.at[hb], hg * GDN_HB + hb)
            tm_ref[hb, 0] = c["tm"]
            sol_ref[hb, 0] = c["sol"]
            sp = s_acc[hb]
            st_ref[hb, 0] = sp
            snew = _bdot(c["trans"], sp) + c["inject"]
            yield
            s_acc[hb] = snew
            o = _bdot(c["qn"], snew)
            yield
            rstd = lax.rsqrt(jnp.mean(o * o, axis=-1, keepdims=True) + NORM_EPS)
            og_ref[:, cols] = (o * rstd * gain_ref[...] * _silu(z_ref[:, cols])).astype(og_ref.dtype)

        _interleave([head(hb) for hb in range(GDN_HB)])

    q, k, v, z, gt, sc, gn, st = _gdn_specs(N, False)
    tm_s, sol_s = _gdn_solved_specs(N, False)
    return pl.pallas_call(
        body, name=name, grid=(H // GDN_HB, N), in_specs=[q, k, v, z, gt, sc, sc, gn],
        out_specs=(q, st, tm_s, sol_s),
        out_shape=(jax.ShapeDtypeStruct((T, H * DV), BF16), jax.ShapeDtypeStruct((H, N, DK, DV), F32),
                   jax.ShapeDtypeStruct((H, N, C, C), F32), jax.ShapeDtypeStruct((H, N, C, DV + DK), F32)),
        scratch_shapes=[pltpu.VMEM((GDN_HB, DK, DV), F32)],
        compiler_params=_params(("arbitrary", "arbitrary")),
    )(act, act, act, proj, gates, a_log, dt_bias, gain)


def _gdn_bwd(act, proj, gates, a_log, dt_bias, gain, states, solved, dog, *, name):
    T = act.shape[0]
    H, C, DK, DV = GDN_HEADS, CHUNK, GDN_DK, GDN_DV
    N = T // C

    def rsum(x):
        return jnp.sum(x, axis=1, keepdims=True)

    def body(q_ref, k_ref, v_ref, z_ref, gates_ref, alog_ref, dtb_ref, gain_ref, st_ref, tm_ref, sol_ref, dog_ref,
             dq_ref, dk_ref, dv_ref, dz_ref, dgates_ref, dscal_ref, dgain_ref, ds_acc):
        hg, n = pl.program_id(0), pl.program_id(1)

        @pl.when(n == 0)
        def _():
            ds_acc[...] = jnp.zeros_like(ds_acc)
            dgain_ref[...] = jnp.zeros_like(dgain_ref)
            dscal_ref[...] = jnp.zeros_like(dscal_ref)

        _interleave([one_head(hb, hg * GDN_HB + hb, q_ref, k_ref, v_ref, z_ref, gates_ref, alog_ref, dtb_ref, gain_ref,
                              st_ref, tm_ref, sol_ref, dog_ref, dq_ref, dk_ref, dv_ref, dz_ref, dgates_ref, dscal_ref,
                              dgain_ref, ds_acc)
                     for hb in range(GDN_HB)])

    def one_head(hb, h, q_ref, k_ref, v_ref, z_ref, gates_ref, alog_ref, dtb_ref, gain_ref, st_ref, tm_ref, sol_ref,
                 dog_ref, dq_ref, dk_ref, dv_ref, dz_ref, dgates_ref, dscal_ref, dgain_ref, ds_acc):
        cols = pl.ds(hb * DK, DK)
        c = yield from _gdn_chunk(q_ref.at[:, cols], k_ref.at[:, cols], v_ref.at[:, cols], gates_ref,
                                  alog_ref.at[hb], dtb_ref.at[hb], h, solved=(tm_ref[hb, 0], sol_ref[hb, 0]))
        beta, kn, qn, v, ke, u, w, dd, e_col = c["beta"], c["kn"], c["qn"], c["v"], c["ke"], c["u"], c["w"], c["dd"], c["e_col"]
        sp = st_ref[hb, 0]
        snew = _bdot(c["trans"], sp) + c["inject"]
        yield
        o = _bdot(qn, snew)
        yield
        rstd = lax.rsqrt(jnp.mean(o * o, axis=-1, keepdims=True) + NORM_EPS)
        oh = o * rstd
        z = z_ref[:, cols]
        gain_v = gain_ref[...]
        dy = dog_ref[:, cols].astype(F32)
        dz_ref[:, cols] = (dy * oh * gain_v * _dsilu(z)).astype(dz_ref.dtype)
        dnorm = dy * _silu(z)
        dgain_ref[hb] += jnp.sum(dnorm * oh, axis=0, keepdims=True)
        doh = dnorm * gain_v
        do = rstd * (doh - oh * jnp.mean(doh * oh, axis=-1, keepdims=True))

        dstot = ds_acc[hb] + _bdot(qn, do, TN)
        dqn = _bdot(do, snew, NT)
        yield
        dtrans = _bdot(dstot, sp, NT)
        ds_acc[hb] = _bdot(c["trans"], dstot, TN)
        yield
        dg = jnp.sum(jnp.sum(dtrans * c["eye"], axis=1, keepdims=True), axis=0, keepdims=True)
        m = -dtrans
        dke = _bdot(w, m, NT) + _bdot(u, dstot, NT)
        dw = _bdot(ke, m)
        du = _bdot(ke, dstot)
        yield
        drhs = _fdot(c["tm"], jnp.concatenate([du, dw], axis=1), TN)
        yield
        da = jnp.where(c["strict"], -_fdot(drhs, c["sol"], NT), 0.0)
        yield
        drhs_u, drhs_w = drhs[:, :DV], drhs[:, DV:]
        rwk = rsum(drhs_w * kn)
        dbeta = rsum(da * c["rg"]) + rsum(drhs_u * v) + rwk * e_col
        dgm = da * beta * c["rel"]
        dkn = _bdot(dgm, kn) + _bdot(dgm, kn, TN) + (beta * e_col) * drhs_w + dd * dke
        yield
        dv_ref[:, cols] = beta * drhs_u
        r_ = da * c["a"]
        ddd = rsum(dke * kn)
        dc = rsum(r_) - _row_to_col(jnp.sum(r_, axis=0, keepdims=True)) + beta * rwk * e_col - ddd * dd
        dclast = jnp.sum(ddd * dd, axis=0, keepdims=True) + dg * c["g"]
        dc = dc + jnp.where(_iota2((C, 1), 0) == C - 1, dclast, 0.0)
        dla = jnp.sum(jnp.where(c["ii"] <= c["jj"], _col_to_row(dc), 0.0), axis=1, keepdims=True)
        dalog = jnp.sum(dla * c["la"], axis=0, keepdims=True)
        dxs = dla * (-c["ea"]) * _sigmoid(c["xs"])
        ddtb = jnp.sum(dxs, axis=0, keepdims=True)
        dbl = dbeta * beta * (1.0 - beta)
        lane = _iota2((C, LANE), 1)
        dgates_ref[hb] = jnp.where(lane == 0, dbl, jnp.where(lane == 1, dxs, 0.0))
        lane8 = _iota2((8, LANE), 1)
        dscal_ref[hb] += jnp.where(lane8 == 0, dalog, jnp.where(lane8 == 1, ddtb, 0.0))
        dk_ref[:, cols] = c["rk"] * (dkn - kn * rsum(dkn * kn))
        qh = c["q"] * c["rq"]
        dqs = dqn * (DK ** -0.5)
        dq_ref[:, cols] = c["rq"] * (dqs - qh * rsum(dqs * qh))

    q, k, v, z, gt, sc, gn, st = _gdn_specs(N, True)
    dgt = pl.BlockSpec((GDN_HB, C, LANE), lambda h, n: (h, N - 1 - n, 0))
    dsc = pl.BlockSpec((GDN_HB, 8, LANE), lambda h, n: (h, 0, 0))
    dgn = pl.BlockSpec((GDN_HB, 1, DV), lambda h, n: (h, 0, 0))
    sh = jax.ShapeDtypeStruct((T, H * DK), F32)
    tm_s, sol_s = _gdn_solved_specs(N, True)
    return pl.pallas_call(
        body, name=name, grid=(H // GDN_HB, N), in_specs=[q, k, v, z, gt, sc, sc, gn, st, tm_s, sol_s, q],
        out_specs=(q, q, q, q, dgt, dsc, dgn),
        out_shape=(sh, sh, sh, jax.ShapeDtypeStruct((T, H * DV), BF16),
                   jax.ShapeDtypeStruct((H, T, LANE), F32), jax.ShapeDtypeStruct((H, 8, LANE), F32),
                   jax.ShapeDtypeStruct((H, 1, DV), F32)),
        scratch_shapes=[pltpu.VMEM((GDN_HB, DK, DV), F32)],
        compiler_params=_params(("arbitrary", "arbitrary")),
    )(act, act, act, proj, gates, a_log, dt_bias, gain, states, *solved, dog)


SUBLANES = 8


def _linear_scan(a_ref, b_ref, h_ref, reverse):
    T, W = a_ref.shape
    nb = T // SUBLANES
    row = _iota2((SUBLANES, W), 0)

    def blk(bi, carry):
        bb = (nb - 1 - bi) if reverse else bi
        off = pl.multiple_of(bb * SUBLANES, SUBLANES)
        a = a_ref[pl.ds(off, SUBLANES), :]
        b = b_ref[pl.ds(off, SUBLANES), :]
        for d in (1, 2, 4):
            if reverse:
                edge = row >= SUBLANES - d
                a_sh = jnp.where(edge, 1.0, pltpu.roll(a, SUBLANES - d, 0))
                b_sh = jnp.where(edge, 0.0, pltpu.roll(b, SUBLANES - d, 0))
            else:
                edge = row < d
                a_sh = jnp.where(edge, 1.0, pltpu.roll(a, d, 0))
                b_sh = jnp.where(edge, 0.0, pltpu.roll(b, d, 0))
            b = a * b_sh + b
            a = a * a_sh
        h = a * carry + b
        h_ref[pl.ds(off, SUBLANES), :] = h
        return h[0:1, :] if reverse else h[SUBLANES - 1:SUBLANES, :]

    lax.fori_loop(0, nb, blk, jnp.zeros((1, W), F32))


def _lru_specs(T):
    B, W = LRU_BLOCKS, LRU_BLOCK
    xb = pl.BlockSpec((T, W), lambda j: (0, j))
    yb = pl.BlockSpec((T, W), lambda j: (0, B + j))
    cw = pl.BlockSpec((CONV_WIDTH, W), lambda j: (0, j))
    vec = pl.BlockSpec((1, W), lambda j: (0, j))
    wg = pl.BlockSpec((1, W, W), lambda j: (j, 0, 0))
    bg = pl.BlockSpec((1, 1, W), lambda j: (j, 0, 0))
    return xb, yb, cw, vec, wg, bg


def _lru_gates(xb_ref, cw_ref, cb_ref, wr_ref, br_ref, wi_ref, bi_ref, lam_ref):
    xv = xb_ref[...]
    xc = _conv(xv, cw_ref) + cb_ref[...]
    r = _sigmoid(_bdot(xc, wr_ref[0]) + br_ref[0])
    i = _sigmoid(_bdot(xc, wi_ref[0]) + bi_ref[0])
    sp = _softplus(-lam_ref[...])
    la = -LRU_C * sp * r
    a = jnp.exp(la)
    s = jnp.sqrt(-_expm1(2.0 * la))
    return xv, xc, r, i, sp, a, s


def _lru_fwd(proj, conv_w, conv_b, w_r, b_r, w_i, b_i, lam, *, name):
    T = proj.shape[0]
    B, W = LRU_BLOCKS, LRU_BLOCK

    def body(xb_ref, yb_ref, cw_ref, cb_ref, wr_ref, br_ref, wi_ref, bi_ref, lam_ref, og_ref, hs_ref, a_s, u_s):
        xv, xc, r, i, sp, a, s = _lru_gates(xb_ref, cw_ref, cb_ref, wr_ref, br_ref, wi_ref, bi_ref, lam_ref)
        a_s[...] = a
        u_s[...] = s * (i * xc)
        _linear_scan(a_s, u_s, hs_ref, False)
        og_ref[...] = (hs_ref[...] * _gelu(yb_ref[...])).astype(og_ref.dtype)

    xb, yb, cw, vec, wg, bg = _lru_specs(T)
    return pl.pallas_call(
        body, name=name, grid=(B,), in_specs=[xb, yb, cw, vec, wg, bg, wg, bg, vec], out_specs=(xb, xb),
        out_shape=(jax.ShapeDtypeStruct((T, B * W), BF16), jax.ShapeDtypeStruct((T, B * W), F32)),
        scratch_shapes=[pltpu.VMEM((T, W), F32), pltpu.VMEM((T, W), F32)],
        compiler_params=_params(("arbitrary",)),
    )(proj, proj, conv_w, conv_b, w_r, b_r, w_i, b_i, lam)


def _lru_bwd(proj, conv_w, conv_b, w_r, b_r, w_i, b_i, lam, hs, dout, *, name):
    T = proj.shape[0]
    B, W = LRU_BLOCKS, LRU_BLOCK

    def csum(x):
        return jnp.sum(x, axis=0, keepdims=True)

    def body(xb_ref, yb_ref, cw_ref, cb_ref, wr_ref, br_ref, wi_ref, bi_ref, lam_ref, hs_ref, do_ref,
             dxb_ref, dyb_ref, dcw_ref, dcb_ref, dwr_ref, dbr_ref, dwi_ref, dbi_ref, dlam_ref, a_s, b_s, g_s):
        xv, xc, r, i, sp, a, s = _lru_gates(xb_ref, cw_ref, cb_ref, wr_ref, br_ref, wi_ref, bi_ref, lam_ref)
        h = hs_ref[...]
        yb = yb_ref[...]
        dout = do_ref[...].astype(F32)
        dyb_ref[...] = (dout * h * _dgelu(yb)).astype(dyb_ref.dtype)
        a_s[...] = _shift_up(a, 1)
        b_s[...] = dout * _gelu(yb)
        _linear_scan(a_s, b_s, g_s, True)
        g = g_s[...]
        da = g * _shift_down(h, 1)
        ds = g * (i * xc)
        di = g * s * xc
        dxc = g * s * i
        dla = da * a - ds * (a * a) / s
        dr = dla * (-LRU_C * sp)
        dlam_ref[...] = csum(dla * r) * (LRU_C * _sigmoid(-lam_ref[...]))
        dpr = dr * r * (1.0 - r)
        dpi = di * i * (1.0 - i)
        dxc = dxc + _bdot(dpr, wr_ref[0], NT) + _bdot(dpi, wi_ref[0], NT)
        dwr_ref[0] = _bdot(xc, dpr, TN)
        dwi_ref[0] = _bdot(xc, dpi, TN)
        dbr_ref[0] = csum(dpr)
        dbi_ref[0] = csum(dpi)
        dcb_ref[...] = csum(dxc)
        dxb_ref[...] = _conv_bwd(xv, cw_ref, dxc, dcw_ref).astype(dxb_ref.dtype)

    xb, yb, cw, vec, wg, bg = _lru_specs(T)
    act = jax.ShapeDtypeStruct((T, B * W), BF16)
    return pl.pallas_call(
        body, name=name, grid=(B,), in_specs=[xb, yb, cw, vec, wg, bg, wg, bg, vec, xb, xb],
        out_specs=(xb, xb, cw, vec, wg, bg, wg, bg, vec),
        out_shape=(act, act, jax.ShapeDtypeStruct((CONV_WIDTH, B * W), F32), jax.ShapeDtypeStruct((1, B * W), F32),
                   jax.ShapeDtypeStruct((B, W, W), F32), jax.ShapeDtypeStruct((B, 1, W), F32),
                   jax.ShapeDtypeStruct((B, W, W), F32), jax.ShapeDtypeStruct((B, 1, W), F32),
                   jax.ShapeDtypeStruct((1, B * W), F32)),
        scratch_shapes=[pltpu.VMEM((T, W), F32)] * 3,
        compiler_params=_params(("arbitrary",)),
    )(proj, proj, conv_w, conv_b, w_r, b_r, w_i, b_i, lam, hs, dout)


MESH = pl.DeviceIdType.MESH
N_CHIPS = 4
AG_COPIES = 7


def _mesh_pos():
    return lax.axis_index("x"), lax.axis_index("y"), lax.axis_index("c")


def _hbm_specs(n):
    return [pl.BlockSpec(memory_space=pltpu.HBM)] * n


def _all_gather(shards, *, name):
    n = len(shards)

    def body(*refs):
        xs, outs = refs[:n], refs[n:2 * n]
        send_sems, recv_sems, local_sems = refs[2 * n:]
        x, y, c = _mesh_pos()
        me, sibling = (x, y, c), (x, y, 1 - c)
        chips = [(1 - x, y), (x, 1 - y), (1 - x, 1 - y)]

        def rows(t, px, py, pc):
            return outs[t].at[4 * px + 2 * py + pc]

        def copy(t, k, block, to, src=None):
            return pltpu.make_async_remote_copy(
                src_ref=rows(t, *block) if src is None else src, dst_ref=rows(t, *block),
                send_sem=send_sems.at[t * AG_COPIES + k], recv_sem=recv_sems.at[t * AG_COPIES + k],
                device_id=to, device_id_type=MESH)

        mine = [pltpu.make_async_copy(xs[t], rows(t, *me), local_sems.at[t]) for t in range(n)]
        for cp in mine:
            cp.start()
        first = []
        for t in range(n):
            first.append(copy(t, 0, me, sibling, src=xs[t]))
            first += [copy(t, 1 + j, me, (*chip, c), src=xs[t]) for j, chip in enumerate(chips)]
        for cp in first:
            cp.start()
        passed = []
        for j, chip in enumerate(chips):
            for t in range(n):
                copy(t, 1 + j, (*chip, c), me).wait_recv()
                cp = copy(t, 4 + j, (*chip, c), sibling)
                cp.start()
                passed.append(cp)
        for t in range(n):
            copy(t, 0, sibling, me).wait_recv()
        for j, chip in enumerate(chips):
            for t in range(n):
                copy(t, 4 + j, (*chip, 1 - c), me).wait_recv()
        for cp in first + passed:
            cp.wait_send()
        for cp in mine:
            cp.wait()

    return pl.pallas_call(
        body, name=name,
        out_shape=[jax.ShapeDtypeStruct((N_DEV,) + s.shape, s.dtype) for s in shards],
        in_specs=_hbm_specs(n), out_specs=_hbm_specs(n),
        scratch_shapes=[pltpu.SemaphoreType.DMA((n * AG_COPIES,)), pltpu.SemaphoreType.DMA((n * AG_COPIES,)),
                        pltpu.SemaphoreType.DMA((n,))],
    )(*shards)


SIDE_EFFECT = pltpu.SideEffectType.DATAFLOW_SIDE_EFFECTING


def _copies(plan, refs, send_sems, recv_sems):
    return [pltpu.make_async_remote_copy(src_ref=src, dst_ref=dst, send_sem=send_sems.at[k], recv_sem=recv_sems.at[k],
                                         device_id=to, device_id_type=MESH)
            for k, (src, dst, to) in enumerate(plan(refs))]


def _split_start(bufs, plan, n_copies, *, name, deps=()):
    n = len(bufs)

    def body(*refs):
        send_sems, recv_sems = refs[n + len(deps)], refs[n + len(deps) + 1]
        token = refs[-1]
        for cp in _copies(plan, refs[:n], send_sems, recv_sems):
            cp.start()
        token[...] = jnp.zeros_like(token)

    hbm, sem = pl.BlockSpec(memory_space=pltpu.HBM), pl.BlockSpec(memory_space=pltpu.SEMAPHORE)
    out = pl.pallas_call(
        body, name=name,
        out_shape=(pltpu.SemaphoreType.DMA((n_copies,)), pltpu.SemaphoreType.DMA((n_copies,)),
                   *[pltpu.HBM(b.shape, b.dtype) for b in bufs], jax.ShapeDtypeStruct(TOKEN_SHAPE, F32)),
        in_specs=[hbm] * n + [pl.BlockSpec(memory_space=pl.ANY)] * len(deps),
        out_specs=(sem, sem, *[hbm] * n, pl.BlockSpec(memory_space=pltpu.VMEM)),
        input_output_aliases={i: 2 + i for i in range(n)},
        compiler_params=pltpu.CompilerParams(has_side_effects=SIDE_EFFECT),
    )(*[pltpu.with_memory_space_constraint(b, pltpu.HBM) for b in bufs], *deps)
    return out[0], out[1], list(out[2:2 + n]), out[-1]


def _split_wait(send_sems, recv_sems, bufs, plan, after, *, name):
    n = len(bufs)

    def body(*refs):
        for cp in _copies(plan, refs[:n], refs[n], refs[n + 1]):
            cp.wait_send()
            cp.wait_recv()

    hbm, sem = pl.BlockSpec(memory_space=pltpu.HBM), pl.BlockSpec(memory_space=pltpu.SEMAPHORE)
    out = pl.pallas_call(
        body, name=name, out_shape=tuple(pltpu.HBM(b.shape, b.dtype) for b in bufs),
        in_specs=[hbm] * n + [sem, sem, pl.BlockSpec(memory_space=pl.ANY)], out_specs=tuple([hbm] * n),
        input_output_aliases={i: i for i in range(n)},
        compiler_params=pltpu.CompilerParams(has_side_effects=SIDE_EFFECT),
    )(*bufs, send_sems, recv_sems, after)
    return list(out)


def _block(ref, in_cols, d):
    if not in_cols:
        return ref.at[d]
    c = ref.shape[1] // N_DEV
    return ref.at[:, pl.ds(pl.multiple_of(d * c, LANE), c)]


def _plan_gather_a(in_cols):
    n = len(in_cols)

    def plan(refs):
        x, y, c = _mesh_pos()
        me = 4 * x + 2 * y + c
        peers = [(x, y, 1 - c), (1 - x, y, c), (x, 1 - y, c), (1 - x, 1 - y, c)]
        return [(refs[t], _block(refs[n + t], in_cols[t], me), to) for t in range(n) for to in peers]
    return plan


def _plan_gather_b(in_cols):
    n = len(in_cols)

    def plan(refs):
        x, y, c = _mesh_pos()
        ds = [4 * px + 2 * py + c for px, py in [(1 - x, y), (x, 1 - y), (1 - x, 1 - y)]]
        return [(_block(refs[t], in_cols[t], d), _block(refs[t], in_cols[t], d), (x, y, 1 - c))
                for t in range(n) for d in ds]
    return plan


def _plan_scatter_pair(in_cols):
    n = len(in_cols)

    def plan(refs):
        x, y, c = _mesh_pos()
        return [(_block(refs[t], in_cols[t], 2 * q + (1 - c)), refs[n + t].at[q], (x, y, 1 - c))
                for t in range(n) for q in range(N_CHIPS)]
    return plan


def _plan_scatter_chips(n):
    def plan(refs):
        x, y, c = _mesh_pos()
        chips = [(1 - x, y), (x, 1 - y), (1 - x, 1 - y)]
        return [(refs[t].at[2 * px + py], refs[n + t].at[j], (px, py, c))
                for t in range(n) for j, (px, py) in enumerate(chips)]
    return plan


def _insert_block(land, shard, device, in_cols, *, name):
    r, c = shard.shape
    tr = _tile(r, (512, 256))

    def body(d_ref, s_ref, land_ref, o_ref):
        if in_cols:
            o_ref[...] = s_ref[...]
        else:
            o_ref[0] = s_ref[...]

    if in_cols:
        o_spec = pl.BlockSpec((tr, c), lambda i, d: (i, d[0]))
    else:
        o_spec = pl.BlockSpec((1, tr, c), lambda i, d: (d[0], i, 0))
    return pl.pallas_call(
        body, name=name,
        grid_spec=pltpu.PrefetchScalarGridSpec(
            num_scalar_prefetch=1, grid=(r // tr,),
            in_specs=[pl.BlockSpec((tr, c), lambda i, d: (i, 0)), pl.BlockSpec(memory_space=pl.ANY)],
            out_specs=o_spec),
        out_shape=jax.ShapeDtypeStruct(land.shape, land.dtype), input_output_aliases={2: 0},
        compiler_params=_params(("parallel",)),
    )(device, shard, land)


PAIR_ROWS = (512, 256)


def _pair_add(g, a, core, in_cols, *, name):
    _, R, C = a.shape
    tr = _tile(R, PAIR_ROWS)

    def body(c_ref, g_ref, a_ref, o_ref):
        gv = g_ref[...] if in_cols else g_ref[0]
        o_ref[0] = (gv.astype(F32) + a_ref[0].astype(F32)).astype(o_ref.dtype)

    blk = pl.BlockSpec((1, tr, C), lambda q, i, c: (q, i, 0))
    if in_cols:
        g_spec = pl.BlockSpec((tr, C), lambda q, i, c: (i, 2 * q + c[0]))
    else:
        g_spec = pl.BlockSpec((1, tr, C), lambda q, i, c: (2 * q + c[0], i, 0))
    return pl.pallas_call(
        body, name=name,
        grid_spec=pltpu.PrefetchScalarGridSpec(
            num_scalar_prefetch=1, grid=(N_CHIPS, R // tr), in_specs=[g_spec, blk], out_specs=blk),
        out_shape=jax.ShapeDtypeStruct((N_CHIPS, R, C), BF16),
        compiler_params=_params(("parallel", "parallel")),
    )(core, g, a)


ADAM_ROWS = 256


def _adamw_sharded(w, m, v, s4, b3, chip, layer, prev, *, name):
    L, R, C = w.shape
    tr = _tile(R, (ADAM_ROWS,))
    if prev is None and L > 1:
        prev = tuple(lax.empty(w.shape, F32) for _ in range(4))
    n_prev = 0 if prev is None else 4

    def body(q_ref, w_ref, m_ref, v_ref, s_ref, b_ref, *rest):
        g_out, d_out, m_out, v_out = rest[n_prev:]
        g = s_ref[0].astype(F32)
        for j in range(N_CHIPS - 1):
            g = g + b_ref[j].astype(F32)
        mn = ADAM_B1 * m_ref[0] + (1.0 - ADAM_B1) * g
        vn = ADAM_B2 * v_ref[0] + (1.0 - ADAM_B2) * (g * g)
        g_out[0] = g
        d_out[0] = -ADAM_LR * ((mn / ADAM_C1) / (jnp.sqrt(vn / ADAM_C2) + ADAM_EPS) + ADAM_WD * w_ref[0])
        m_out[0] = mn
        v_out[0] = vn

    blk = pl.BlockSpec((1, tr, C), lambda i, q: (layer, i, 0))
    sh = jax.ShapeDtypeStruct((L, R, C), F32)
    return pl.pallas_call(
        body, name=name,
        grid_spec=pltpu.PrefetchScalarGridSpec(
            num_scalar_prefetch=1, grid=(R // tr,),
            in_specs=[blk, blk, blk, pl.BlockSpec((1, tr, C), lambda i, q: (q[0], i, 0)),
                      pl.BlockSpec((N_CHIPS - 1, tr, C), lambda i, q: (0, i, 0))]
            + [pl.BlockSpec(memory_space=pl.ANY)] * n_prev,
            out_specs=(blk,) * 4),
        out_shape=(sh,) * 4, input_output_aliases={6 + k: k for k in range(n_prev)},
        compiler_params=_params(("parallel",)),
    )(chip, w, m, v, s4, b3, *(prev or ()))


FWD_NAMES = ['x', 'norm1', 'norm2', 'final_norm', 'ret_w_in', 'ret_gn_gain', 'ret_w_out', 'gdn_w_in', 'gdn_conv_w',
             'gdn_a_log', 'gdn_dt_bias', 'gdn_norm_gain', 'gdn_w_out', 'gla_w_in', 'gla_w_gate_up', 'gla_gate_bias',
             'gla_norm_gain', 'gla_w_out', 'lru_w_in', 'lru_conv_w', 'lru_conv_b', 'lru_w_rgate', 'lru_b_rgate',
             'lru_w_igate', 'lru_b_igate', 'lru_lambda', 'lru_w_out', 'mlp_w_up', 'mlp_w_down']
WEIGHT_NAMES = FWD_NAMES[1:]
ARG_NAMES = FWD_NAMES + ['loss_target'] + ['m_' + n for n in WEIGHT_NAMES] + ['v_' + n for n in WEIGHT_NAMES]

MIXER_IN = ('ret_w_in', 'gdn_w_in', 'gla_w_in', 'lru_w_in')
MIXER_OUT = ('ret_w_out', 'gdn_w_out', 'gla_w_out', 'lru_w_out')
BIG_NAMES = MIXER_IN + MIXER_OUT + ('mlp_w_up', 'mlp_w_down')
SMALL = {'norm1': False, 'norm2': False, 'final_norm': False, 'ret_gn_gain': True, 'gdn_conv_w': True,
         'gdn_a_log': False, 'gdn_dt_bias': False, 'gdn_norm_gain': False, 'gla_w_gate_up': True,
         'gla_gate_bias': True, 'gla_norm_gain': True, 'lru_conv_w': True, 'lru_conv_b': True,
         'lru_w_rgate': False, 'lru_b_rgate': False, 'lru_w_igate': False, 'lru_b_igate': False, 'lru_lambda': True}
SMALL_NAMES = tuple(n for n in WEIGHT_NAMES if n in SMALL)
MEDIUM = ('lru_w_rgate', 'lru_w_igate')
EARLY_SMALL = tuple(n for n in SMALL_NAMES if n.startswith(('gdn_', 'gla_', 'lru_')) and n not in MEDIUM)
LATE_SMALL = tuple(n for n in SMALL_NAMES if n not in EARLY_SMALL + MEDIUM)
GDN_TAIL = 2 * GDN_HEADS
GDN_MAIN = 4 * D_MODEL
GLA_MAIN = 3 * D_MODEL


PACK_ROWS = 256


PACK_TILE = SUBLANES * LANE


def _pack(arrs):
    rows = []
    for a in arrs:
        f = a.reshape(-1).astype(F32)
        rows.append(jnp.pad(f, (0, (-f.shape[0]) % PACK_TILE)).reshape(-1, LANE))
    fill = (-sum(r.shape[0] for r in rows)) % PACK_ROWS
    if fill:
        rows.append(jnp.zeros((fill, LANE), F32))
    return jnp.concatenate(rows, axis=0)


def _unpack(buf, shapes, lead=()):
    out, r0 = [], 0
    for s in shapes:
        n = int(np.prod(s))
        nr = -(-n // PACK_TILE) * SUBLANES
        blk = buf[..., r0:r0 + nr, :].reshape(lead + (nr * LANE,))[..., :n]
        out.append(blk.reshape(lead + tuple(s)))
        r0 += nr
    return out


def _full_cols(g):
    return jnp.transpose(g, (1, 0, 2)).reshape(g.shape[1], N_DEV * g.shape[2])


def _full_rows(g):
    return g.reshape(N_DEV * g.shape[1], g.shape[2])


def _blocks_cols(dw):
    r, c = dw.shape[0], dw.shape[1] // N_DEV
    return jnp.transpose(dw.reshape(r, N_DEV, c), (1, 0, 2))


def _blocks_rows(dw):
    return dw.reshape(N_DEV, dw.shape[0] // N_DEV, dw.shape[1])


def _pad_cols(a, n=LANE):
    return jnp.pad(a, ((0, 0), (0, n - a.shape[1])))


def _mixer_fwd(layer, hn, w_in, sm, tables, deps=()):
    tag = f"l{layer}"
    if layer == 0:
        proj = _matmul(hn, w_in, name=tag + "_in", deps=deps)
        gain = sm['ret_gn_gain'][0][:, None, :]
        og, st = _ret_fwd(proj, gain, tables, name=tag + "_ret_fwd")
        return og, dict(proj=proj, st=st, gain=gain)
    if layer == 1:
        w_main, w_tail = w_in, _pad_cols(w_in[:, GDN_MAIN:])
        proj = _matmul(hn, w_main, b_cols=GDN_MAIN, name=tag + "_in", deps=deps)
        gates = _matmul(hn, w_tail, name=tag + "_in_tail")
        conv_w = sm['gdn_conv_w'][0]
        act = _conv_silu_fwd(proj, conv_w, name=tag + "_conv")
        a_log = sm['gdn_a_log'].reshape(GDN_HEADS, 1, 1)
        dt_bias = sm['gdn_dt_bias'].reshape(GDN_HEADS, 1, 1)
        gain = sm['gdn_norm_gain']
        og, st, tm, sol = _gdn_fwd(act, proj, gates, a_log, dt_bias, gain, name=tag + "_gdn_fwd")
        return og, dict(proj=proj, gates=gates, act=act, st=st, solved=(tm, sol), conv_w=conv_w, a_log=a_log, dt_bias=dt_bias,
                        gain=gain, w_main=w_main, w_tail=w_tail)
    if layer == 2:
        w_main, w_tail = w_in, _pad_cols(w_in[:, GLA_MAIN:])
        proj = _matmul(hn, w_main, b_cols=GLA_MAIN, name=tag + "_in", deps=deps)
        glow = _matmul(hn, w_tail, name=tag + "_in_tail")
        wgu = jnp.pad(sm['gla_w_gate_up'][0], ((0, LANE - GLA_GATE_RANK), (0, 0)))
        glogit = _matmul(glow, wgu, name=tag + "_gate_up")
        bias = sm['gla_gate_bias']
        gain = sm['gla_norm_gain'][0][:, None, :]
        og, st = _gla_fwd(proj, glogit, bias, gain, name=tag + "_gla_fwd")
        return og, dict(proj=proj, glow=glow, glogit=glogit, wgu=wgu, bias=bias, gain=gain, st=st,
                        w_main=w_main, w_tail=w_tail)
    proj = _matmul(hn, w_in, name=tag + "_in", deps=deps)
    args = (proj, sm['lru_conv_w'][0], sm['lru_conv_b'], sm['lru_w_rgate'][0], sm['lru_b_rgate'][0][:, None, :],
            sm['lru_w_igate'][0], sm['lru_b_igate'][0][:, None, :], sm['lru_lambda'])
    og, hs = _lru_fwd(*args, name=tag + "_lru_fwd")
    return og, dict(args=args, hs=hs)


def _mixer_bwd(layer, hn, w_in, dog, sv, tables, on_dw):
    tag = f"l{layer}"
    if layer == 0:
        dq, dk, dv, dg, dgain = _ret_bwd(sv['proj'], sv['gain'], tables, sv['st'], dog, name=tag + "_ret_bwd")
        dproj = jnp.concatenate([dq, dk, dv, dg], axis=1)
        deps = on_dw(_matmul(hn, dproj, ta=True, out_dtype=BF16, name=tag + "_in_dw"))
        dhn = _matmul(dproj, w_in, tb=True, name=tag + "_in_dx", deps=deps)
        return dhn, {'ret_gn_gain': dgain[:, 0][None]}
    if layer == 1:
        dq, dk, dv, dz, dgates, dscal, dgain = _gdn_bwd(
            sv['act'], sv['proj'], sv['gates'], sv['a_log'], sv['dt_bias'], sv['gain'], sv['st'], sv['solved'], dog,
            name=tag + "_gdn_bwd")
        dact = jnp.concatenate([dq, dk, dv], axis=1)
        dqkv, dconv = _conv_silu_bwd(sv['proj'], sv['conv_w'], dact, name=tag + "_conv_bwd")
        dmain = jnp.concatenate([dqkv, dz], axis=1)
        T = dmain.shape[0]
        dtail = _pad_cols(jnp.transpose(dgates[:, :, :2], (1, 2, 0)).reshape(T, GDN_TAIL))
        dw_main = _matmul(hn, dmain, ta=True, out_dtype=BF16, name=tag + "_in_dw")
        dw_tail = _matmul(hn, dtail, ta=True, out_dtype=BF16, name=tag + "_in_tail_dw")
        deps = on_dw(jnp.concatenate([dw_main, dw_tail[:, :GDN_TAIL]], axis=1))
        dhn = _matmul(dmain, sv['w_main'], tb=True, b_cols=GDN_MAIN, name=tag + "_in_dx", deps=deps)
        dhn = _matmul(dtail, sv['w_tail'], tb=True, epi="add", extra=dhn, name=tag + "_in_tail_dx")
        small = {'gdn_conv_w': dconv[None], 'gdn_a_log': dscal[:, 0, 0][None], 'gdn_dt_bias': dscal[:, 0, 1][None],
                 'gdn_norm_gain': jnp.sum(dgain[:, 0], axis=0)[None]}
        return dhn, small
    if layer == 2:
        dq, dk, dv, dr, dgl, dgain = _gla_bwd(sv['proj'], sv['glogit'], sv['bias'], sv['gain'], sv['st'], dog,
                                              name=tag + "_gla_bwd")
        dmain = jnp.concatenate([dq, dk, dv, dr], axis=1)
        dglow = _matmul(dgl, sv['wgu'], tb=True, name=tag + "_gate_up_dx")
        dwgu = _matmul(sv['glow'], dgl, ta=True, name=tag + "_gate_up_dw")
        dbias = _colsum(dgl, name=tag + "_gate_bias")
        dw_main = _matmul(hn, dmain, ta=True, out_dtype=BF16, name=tag + "_in_dw")
        dw_tail = _matmul(hn, dglow, ta=True, out_dtype=BF16, name=tag + "_in_tail_dw")
        deps = on_dw(jnp.concatenate([dw_main, dw_tail[:, :GLA_GATE_RANK]], axis=1))
        dhn = _matmul(dmain, sv['w_main'], tb=True, b_cols=GLA_MAIN, name=tag + "_in_dx", deps=deps)
        dhn = _matmul(dglow, sv['w_tail'], tb=True, epi="add", extra=dhn, name=tag + "_in_tail_dx")
        small = {'gla_w_gate_up': dwgu[:GLA_GATE_RANK][None], 'gla_gate_bias': dbias,
                 'gla_norm_gain': dgain[:, 0][None]}
        return dhn, small
    dxb, dyb, dcw, dcb, dwr, dbr, dwi, dbi, dlam = _lru_bwd(*sv['args'], sv['hs'], dog, name=tag + "_lru_bwd")
    dproj = jnp.concatenate([dxb, dyb], axis=1)
    deps = on_dw(_matmul(hn, dproj, ta=True, out_dtype=BF16, name=tag + "_in_dw"))
    dhn = _matmul(dproj, w_in, tb=True, name=tag + "_in_dx", deps=deps)
    small = {'lru_conv_w': dcw[None], 'lru_conv_b': dcb, 'lru_w_rgate': dwr[None], 'lru_b_rgate': dbr[:, 0][None],
             'lru_w_igate': dwi[None], 'lru_b_igate': dbi[:, 0][None], 'lru_lambda': dlam}
    return dhn, small


def _step(*args):
    assert len(args) == len(ARG_NAMES)
    p = dict(zip(ARG_NAMES, args))
    xi, yi, ci = _mesh_pos()
    dev = 4 * xi + 2 * yi + ci
    device = dev.astype(jnp.int32).reshape(1)
    core = ci.astype(jnp.int32).reshape(1)
    chip = (2 * xi + yi).astype(jnp.int32).reshape(1)
    x = p['x'][0]
    target = p['loss_target'][0]
    T = x.shape[0]
    tables = _ret_tables(T)

    sharded_small = [n for n in SMALL_NAMES if SMALL[n]]
    gathered, = _all_gather([_pack([p[n] for n in sharded_small])], name="gather_small")
    def in_cols_of(w):
        return w.shape[1] % LANE == 0

    gathers = {}
    token = gathered
    for layer in range(DEPTH):
        w_in_shard = p[MIXER_IN[layer]][0]
        groups = {'a': [(w_in_shard, in_cols_of(w_in_shard), _full_cols)],
                  'b': [(p[MIXER_OUT[layer]][0], False, _full_rows)],
                  'c': [(p['mlp_w_up'][layer], True, None), (p['mlp_w_down'][layer], False, _full_rows)]}
        for key, members in groups.items():
            shards = [w.astype(BF16) for w, _, _ in members]
            in_cols = [ic for _, ic, _ in members]
            n = len(shards)
            lands = [_insert_block(lax.empty((s.shape[0], N_DEV * s.shape[1]) if ic else (N_DEV,) + s.shape, BF16),
                                   s, device, ic, name=f"own_l{layer}{key}{i}")
                     for i, (s, ic) in enumerate(zip(shards, in_cols))]
            send, recv, bufs, token = _split_start(shards + lands, _plan_gather_a(in_cols), 4 * n,
                                                   name=f"gather_a_start_l{layer}{key}", deps=(token,))
            gathers[layer, key] = dict(n=n, h=(send, recv, bufs), in_cols=in_cols, full_of=[f for _, _, f in members])

    def gather_forward(layer, key, after):
        g = gathers[layer, key]
        n = g['n']
        send, recv, bufs = g['h']
        bufs = _split_wait(send, recv, bufs, _plan_gather_a(g['in_cols']), after, name=f"gather_a_wait_l{layer}{key}")
        send, recv, lands, tok = _split_start(bufs[n:], _plan_gather_b(g['in_cols']), 3 * n,
                                              name=f"gather_b_start_l{layer}{key}")
        g['h'] = (send, recv, lands)
        return tok

    def gather_finish(layer, key, after):
        g = gathers[layer, key]
        send, recv, lands = g['h']
        lands = _split_wait(send, recv, lands, _plan_gather_b(g['in_cols']), after, name=f"gather_b_wait_l{layer}{key}")
        return [l if ic else full_of(l) for l, ic, full_of in zip(lands, g['in_cols'], g['full_of'])]

    w_in_next, = gather_finish(0, 'a', gather_forward(0, 'a', token))
    parts = _unpack(gathered, [p[n].shape for n in sharded_small], lead=(N_DEV,))
    sm = {n: p[n] for n in SMALL_NAMES if not SMALL[n]}
    for n, blk in zip(sharded_small, parts):
        full = jnp.moveaxis(blk, 0, -2)
        sm[n] = full.reshape(full.shape[:-2] + (N_DEV * full.shape[-1],))

    saved = []
    big = {}
    for layer in range(DEPTH):
        w_in = w_in_next
        tag = f"l{layer}"
        hn = _rmsnorm_fwd(x, sm['norm1'][layer][None], name=tag + "_norm1")
        og, sv = _mixer_fwd(layer, hn, w_in, sm, tables)
        tok_b = gather_forward(layer, 'b', og)
        tok_c = gather_forward(layer, 'c', tok_b)
        w_out, = gather_finish(layer, 'b', tok_c)
        x_mid = _matmul(og, w_out, epi="add", extra=x, name=tag + "_out")
        hn2 = _rmsnorm_fwd(x_mid, sm['norm2'][layer][None], name=tag + "_norm2")
        w_up, w_down = gather_finish(layer, 'c', hn2)
        big[layer] = (w_in, w_out, w_up, w_down)
        u, a = _matmul(hn2, w_up, epi="relu2", name=tag + "_up")
        deps = (gather_forward(layer + 1, 'a', u),) if layer + 1 < DEPTH else ()
        x_new = _matmul(a, w_down, epi="add", extra=x_mid, name=tag + "_down", deps=deps)
        if layer + 1 < DEPTH:
            w_in_next, = gather_finish(layer + 1, 'a', x_new)
        saved.append(dict(x=x, hn=hn, og=og, sv=sv, x_mid=x_mid, hn2=hn2, u=u, a=a))
        x = x_new
    dx, dxb, dfinal, loss_part = _final_loss_bwd(x, sm['final_norm'][None], target, name="final_loss_bwd")
    loss = lax.psum(loss_part[0, 0], ("x", "y", "c"))

    outs = {}
    small_grads = {'final_norm': dfinal[0]}
    dnorm1, dnorm2 = [None] * DEPTH, [None] * DEPTH

    def scatter_start(items, deps, tag):
        grads, in_cols = [], []
        for _, _, dw, on in items:
            ic = on == 'cols' and (dw.shape[1] // N_DEV) % LANE == 0
            in_cols.append(ic)
            grads.append(dw if ic else (_blocks_cols(dw) if on == 'cols' else _blocks_rows(dw)))
        n = len(grads)
        lands = [lax.empty((N_CHIPS, dw.shape[0], dw.shape[1] // N_DEV) if ic else (N_CHIPS,) + dw.shape[1:], BF16)
                 for dw, ic in zip(grads, in_cols)]
        send, recv, bufs, tok = _split_start(grads + lands, _plan_scatter_pair(in_cols), N_CHIPS * n,
                                             name=f"scatter_pair_start_{tag}", deps=deps)
        return dict(items=items, n=n, tag=tag, in_cols=in_cols, h=(send, recv, bufs)), tok

    def scatter_forward(g, after):
        n, tag = g['n'], g['tag']
        send, recv, bufs = g['h']
        bufs = _split_wait(send, recv, bufs, _plan_scatter_pair(g['in_cols']), after, name=f"scatter_pair_wait_{tag}")
        sums = [_pair_add(b, a_, core, ic, name=f"pair_add_{tag}_{i}")
                for i, (b, a_, ic) in enumerate(zip(bufs[:n], bufs[n:], g['in_cols']))]
        lands = [lax.empty((N_CHIPS - 1,) + s_.shape[1:], BF16) for s_ in sums]
        send, recv, bufs, tok = _split_start(sums + lands, _plan_scatter_chips(n), (N_CHIPS - 1) * n,
                                             name=f"scatter_chips_start_{tag}")
        g['h'] = (send, recv, bufs)
        return tok

    def scatter_finish(g, after):
        n, tag = g['n'], g['tag']
        send, recv, bufs = g['h']
        bufs = _split_wait(send, recv, bufs, _plan_scatter_chips(n), after, name=f"scatter_chips_wait_{tag}")
        for i, (wname, idx, _, _) in enumerate(g['items']):
            outs[wname] = _adamw_sharded(p[wname], p['m_' + wname], p['v_' + wname], bufs[i], bufs[n + i], chip,
                                         idx, outs.get(wname), name=f"adamw_{tag}_{i}")

    older = []
    for layer in reversed(range(DEPTH)):
        w_in, w_out, w_up, w_down = big[layer]
        s = saved[layer]
        tag = f"l{layer}"
        du = _matmul(dxb, w_down, tb=True, epi="drelu2", extra=s['u'], out_dtype=BF16, name=tag + "_down_dx")
        dw_down = _matmul(s['a'], dxb, ta=True, out_dtype=BF16, name=tag + "_down_dw")
        dhn2 = _matmul(du, w_up, tb=True, name=tag + "_up_dx")
        dw_up = _matmul(s['hn2'], du, ta=True, out_dtype=BF16, name=tag + "_up_dw")
        dx, dxb, dn2 = _rmsnorm_bwd(s['x_mid'], sm['norm2'][layer][None], dhn2, dx, name=tag + "_norm2_bwd")
        mlp_group, tok = scatter_start([('mlp_w_up', layer, dw_up, 'cols'),
                                        ('mlp_w_down', layer, dw_down, 'rows')], (), f"mlp_l{layer}")
        dog = _matmul(dxb, w_out, tb=True, name=tag + "_out_dx", deps=(tok,))
        dw_out = _matmul(s['og'], dxb, ta=True, out_dtype=BF16, name=tag + "_out_dw")
        mlp_token = scatter_forward(mlp_group, dw_out)
        started = []

        def on_dw(dw_in):
            group, tok_ = scatter_start([(MIXER_IN[layer], 0, dw_in, 'cols'),
                                         (MIXER_OUT[layer], 0, dw_out, 'rows')], (mlp_token,), f"mix_l{layer}")
            started.append(group)
            return (tok_,)

        dhn, sg = _mixer_bwd(layer, s['hn'], w_in, dog, s['sv'], tables, on_dw)
        mixer_group, = started
        small_grads.update(sg)
        if layer > 0:
            tok = scatter_forward(mixer_group, dhn)
            dx, dxb, dn1 = _rmsnorm_bwd(s['x'], sm['norm1'][layer][None], dhn, dx, name=tag + "_norm1_bwd", deps=(tok,))
        else:
            dx, dxb, dn1 = _rmsnorm_bwd(s['x'], sm['norm1'][layer][None], dhn, dx, name=tag + "_norm1_bwd")
        dnorm1[layer], dnorm2[layer] = dn1[0], dn2[0]
        for g in older:
            scatter_finish(g, dx)
        older = [mlp_group, mixer_group]
        if layer == 1:
            parts = [_pack([small_grads[n] for n in EARLY_SMALL])] + [small_grads[n].reshape(-1, LANE) for n in MEDIUM]
            lands = [_insert_block(lax.empty((N_DEV,) + s_.shape, F32), s_, device, False, name=f"own_small_grads{i}")
                     for i, s_ in enumerate(parts)]
            n_parts = len(parts)
            send, recv, bufs, _ = _split_start(parts + lands, _plan_gather_a([False] * n_parts), 4 * n_parts,
                                               name="gather_a_start_small_grads", deps=(dx,))
            gathers['small_grads', 'early'] = dict(n=n_parts, h=(send, recv, bufs), in_cols=[False] * n_parts,
                                                   full_of=[lambda l: l] * n_parts)
    small_grads['norm1'] = jnp.stack(dnorm1)
    small_grads['norm2'] = jnp.stack(dnorm2)

    late_parts, = _all_gather([_pack([small_grads[n] for n in LATE_SMALL])], name="gather_small_grads")
    early_token = gather_forward('small_grads', 'early', late_parts)
    last_token = scatter_forward(mixer_group, early_token)
    early_parts, *medium_parts = gather_finish('small_grads', 'early', last_token)
    early_sum = _sum_parts(early_parts, name="sum_small_grads_early", deps=(last_token,))
    late_sum = _sum_parts(late_parts, name="sum_small_grads_late")
    for g in older:
        scatter_finish(g, late_sum)
    for n, part in zip(MEDIUM, medium_parts):
        g = _sum_parts(part, name=f"sum_{n}")
        res = _adamw(p[n].reshape(-1, LANE), p['m_' + n].reshape(-1, LANE), p['v_' + n].reshape(-1, LANE), [g],
                     name=f"adamw_{n}")
        outs[n] = tuple(r.reshape(p[n].shape) for r in res)
    by_name = dict(zip(EARLY_SMALL, _unpack(early_sum, [small_grads[n].shape for n in EARLY_SMALL])))
    by_name.update(zip(LATE_SMALL, _unpack(late_sum, [small_grads[n].shape for n in LATE_SMALL])))
    packed_names = EARLY_SMALL + LATE_SMALL
    local_g = []
    for n in packed_names:
        g = by_name[n]
        if SMALL[n]:
            width = p[n].shape[-1]
            g = lax.dynamic_slice_in_dim(g, dev * width, width, axis=g.ndim - 1)
        local_g.append(g.reshape(p[n].shape))
    res = _adamw(_pack([p[n] for n in packed_names]), _pack([p['m_' + n] for n in packed_names]),
                 _pack([p['v_' + n] for n in packed_names]), [_pack(local_g)], name="adamw_small")
    local_shapes = [p[n].shape for n in packed_names]
    unpacked = [_unpack(r, local_shapes) for r in res]
    for i, n in enumerate(packed_names):
        outs[n] = tuple(unpacked[k][i] for k in range(4))

    result = [loss, dx[None]]
    for k in range(4):
        result += [outs[n][k] for n in WEIGHT_NAMES]
    return tuple(result)


def kernel(x, norm1, norm2, final_norm, ret_w_in, ret_gn_gain, ret_w_out, gdn_w_in, gdn_conv_w, gdn_a_log, gdn_dt_bias, gdn_norm_gain, gdn_w_out, gla_w_in, gla_w_gate_up, gla_gate_bias, gla_norm_gain, gla_w_out, lru_w_in, lru_conv_w, lru_conv_b, lru_w_rgate, lru_b_rgate, lru_w_igate, lru_b_igate, lru_lambda, lru_w_out, mlp_w_up, mlp_w_down, loss_target, m_norm1, m_norm2, m_final_norm, m_ret_w_in, m_ret_gn_gain, m_ret_w_out, m_gdn_w_in, m_gdn_conv_w, m_gdn_a_log, m_gdn_dt_bias, m_gdn_norm_gain, m_gdn_w_out, m_gla_w_in, m_gla_w_gate_up, m_gla_gate_bias, m_gla_norm_gain, m_gla_w_out, m_lru_w_in, m_lru_conv_w, m_lru_conv_b, m_lru_w_rgate, m_lru_b_rgate, m_lru_w_igate, m_lru_b_igate, m_lru_lambda, m_lru_w_out, m_mlp_w_up, m_mlp_w_down, v_norm1, v_norm2, v_final_norm, v_ret_w_in, v_ret_gn_gain, v_ret_w_out, v_gdn_w_in, v_gdn_conv_w, v_gdn_a_log, v_gdn_dt_bias, v_gdn_norm_gain, v_gdn_w_out, v_gla_w_in, v_gla_w_gate_up, v_gla_gate_bias, v_gla_norm_gain, v_gla_w_out, v_lru_w_in, v_lru_conv_w, v_lru_conv_b, v_lru_w_rgate, v_lru_b_rgate, v_lru_w_igate, v_lru_b_igate, v_lru_lambda, v_lru_w_out, v_mlp_w_up, v_mlp_w_down):
    return _step(x, norm1, norm2, final_norm, ret_w_in, ret_gn_gain, ret_w_out, gdn_w_in, gdn_conv_w, gdn_a_log, gdn_dt_bias, gdn_norm_gain, gdn_w_out, gla_w_in, gla_w_gate_up, gla_gate_bias, gla_norm_gain, gla_w_out, lru_w_in, lru_conv_w, lru_conv_b, lru_w_rgate, lru_b_rgate, lru_w_igate, lru_b_igate, lru_lambda, lru_w_out, mlp_w_up, mlp_w_down, loss_target, m_norm1, m_norm2, m_final_norm, m_ret_w_in, m_ret_gn_gain, m_ret_w_out, m_gdn_w_in, m_gdn_conv_w, m_gdn_a_log, m_gdn_dt_bias, m_gdn_norm_gain, m_gdn_w_out, m_gla_w_in, m_gla_w_gate_up, m_gla_gate_bias, m_gla_norm_gain, m_gla_w_out, m_lru_w_in, m_lru_conv_w, m_lru_conv_b, m_lru_w_rgate, m_lru_b_rgate, m_lru_w_igate, m_lru_b_igate, m_lru_lambda, m_lru_w_out, m_mlp_w_up, m_mlp_w_down, v_norm1, v_norm2, v_final_norm, v_ret_w_in, v_ret_gn_gain, v_ret_w_out, v_gdn_w_in, v_gdn_conv_w, v_gdn_a_log, v_gdn_dt_bias, v_gdn_norm_gain, v_gdn_w_out, v_gla_w_in, v_gla_w_gate_up, v_gla_gate_bias, v_gla_norm_gain, v_gla_w_out, v_lru_w_in, v_lru_conv_w, v_lru_conv_b, v_lru_w_rgate, v_lru_b_rgate, v_lru_w_igate, v_lru_b_igate, v_lru_lambda, v_lru_w_out, v_mlp_w_up, v_mlp_w_down)
```

```python
import functools
import math

import numpy as np
import jax
import jax.numpy as jnp
from jax import lax
from jax.experimental import pallas as pl
from jax.experimental.pallas import tpu as pltpu

F32 = jnp.float32
BF16 = jnp.bfloat16

D_MODEL = 2048
DEPTH = 4
CHUNK = 64
D_FF = 4 * D_MODEL
NORM_EPS = 1e-6
N_DEV = 8

RET_HEADS, RET_DK, RET_DV = 8, 256, 512
RET_HB = 4
GDN_HEADS, GDN_DK, GDN_DV = 16, 128, 128
GDN_HB = 8
GDN_QKV = GDN_HEADS * (2 * GDN_DK + GDN_DV)
CONV_WIDTH = 4
GLA_HEADS, GLA_DK, GLA_DV = 4, 256, 512
GLA_HB = 4
GLA_GATE_RANK = 16
GLA_TAU = 16.0
LRU_WIDTH, LRU_BLOCKS, LRU_BLOCK = 2048, 16, 128
LRU_C = 8.0
ROPE_BASE = 10000.0

ADAM_LR, ADAM_B1, ADAM_B2, ADAM_EPS, ADAM_WD, ADAM_STEP = 0.001, 0.9, 0.999, 1e-08, 0.01, 10

LANE = 128
VMEM_LIMIT = 48 * 1024 * 1024

NN = (((1,), (0,)), ((), ()))
NT = (((1,), (1,)), ((), ()))
TN = (((0,), (0,)), ((), ()))


def _params(sem=None):
    return pltpu.CompilerParams(dimension_semantics=sem, vmem_limit_bytes=VMEM_LIMIT)


def _bdot(a, b, dn=NN):
    return lax.dot_general(a.astype(BF16), b.astype(BF16), dn, preferred_element_type=F32)


def _split(x):
    hi = x.astype(BF16)
    lo = (x - hi.astype(F32)).astype(BF16)
    return hi, lo


def _fdot(a, b, dn=NN):
    a1, a2 = _split(a)
    b1, b2 = _split(b)
    d = functools.partial(lax.dot_general, dimension_numbers=dn, preferred_element_type=F32)
    return d(a1, b1) + (d(a1, b2) + d(a2, b1))


def _sigmoid(x):
    return 1.0 / (1.0 + jnp.exp(-x))


def _softplus(x):
    return jnp.maximum(x, 0.0) + jnp.log(1.0 + jnp.exp(-jnp.abs(x)))


def _silu(x):
    return x * _sigmoid(x)


def _dsilu(x):
    s = _sigmoid(x)
    return s * (1.0 + x * (1.0 - s))


GELU_C = math.sqrt(2.0 / math.pi)


def _gelu(x):
    return 0.5 * x * (1.0 + jnp.tanh(GELU_C * (x + 0.044715 * x * x * x)))


def _dgelu(x):
    t = jnp.tanh(GELU_C * (x + 0.044715 * x * x * x))
    return 0.5 * (1.0 + t) + 0.5 * x * (1.0 - t * t) * GELU_C * (1.0 + 3.0 * 0.044715 * x * x)


def _expm1(x):
    poly = x * (1.0 + x * 0.5 * (1.0 + x * (1.0 / 3.0) * (1.0 + x * 0.25 * (1.0 + x * 0.2))))
    return jnp.where(jnp.abs(x) < 0.05, poly, jnp.exp(x) - 1.0)


def _iota2(shape, axis):
    return lax.broadcasted_iota(jnp.int32, shape, axis)


def _col_to_row(col):
    n = col.shape[0]
    eye = _iota2((n, n), 0) == _iota2((n, n), 1)
    return jnp.sum(jnp.where(eye, col, 0.0), axis=0, keepdims=True)


def _row_to_col(row):
    n = row.shape[1]
    eye = _iota2((n, n), 0) == _iota2((n, n), 1)
    return jnp.sum(jnp.where(eye, row, 0.0), axis=1, keepdims=True)


def _pick_row(x, r):
    rows = _iota2(x.shape, 0)
    return jnp.sum(jnp.where(rows == r, x, 0.0), axis=0, keepdims=True)


def _shift_down(x, s):
    if s == 0:
        return x
    y = pltpu.roll(x, s, 0)
    return jnp.where(_iota2(x.shape, 0) < s, 0.0, y)


def _shift_up(x, s):
    if s == 0:
        return x
    n = x.shape[0]
    y = pltpu.roll(x, n - s, 0)
    return jnp.where(_iota2(x.shape, 0) >= n - s, 0.0, y)


def _tile(dim, prefs):
    for p in prefs:
        if dim % p == 0:
            return p
    return dim


TOKEN_SHAPE = (8, LANE)


def _dep_specs(deps):
    return [pl.BlockSpec(TOKEN_SHAPE, lambda *_: (0, 0)) for _ in deps]


def _matmul(a, b, *, ta=False, tb=False, epi="none", extra=None, out_dtype=F32, name, deps=(), b_cols=None):
    if ta:
        K, M = a.shape
    else:
        M, K = a.shape
    if tb:
        N, K2 = b.shape[0], b_cols or b.shape[1]
    else:
        K2, N = b.shape[0], b_cols or b.shape[1]
    assert K == K2, (a.shape, b.shape, ta, tb)
    if K <= 2048:
        tk = K
        if N <= 2048:
            tm, tn = _tile(M, (512, 256, 128)), N
        else:
            tm, tn = _tile(M, (1024, 512, 256, 128)), _tile(N, (512, 256, 128))
    else:
        tm, tn, tk = (_tile(d, (1024, 512, 256, 128)) for d in (M, N, K))
    nk = K // tk
    dn = (((0 if ta else 1,), (1 if tb else 0,)), ((), ()))
    n_extra = 0 if extra is None else 1
    n_out = 2 if epi == "relu2" else 1

    def body(*refs):
        a_ref, b_ref = refs[0], refs[1]
        e_ref = refs[2] if n_extra else None
        outs = refs[2 + n_extra + len(deps):2 + n_extra + len(deps) + n_out]

        def finish(r):
            if epi == "none":
                outs[0][...] = r.astype(outs[0].dtype)
            elif epi == "add":
                outs[0][...] = (r + e_ref[...]).astype(outs[0].dtype)
            elif epi == "relu2":
                outs[0][...] = r
                p = jnp.maximum(r, 0.0)
                outs[1][...] = (p * p).astype(outs[1].dtype)
            elif epi == "drelu2":
                outs[0][...] = (r * 2.0 * jnp.maximum(e_ref[...], 0.0)).astype(outs[0].dtype)

        def product():
            return lax.dot_general(a_ref[...].astype(BF16), b_ref[...].astype(BF16), dn, preferred_element_type=F32)

        if nk == 1:
            finish(product())
            return
        acc = refs[-1]
        k = pl.program_id(2)

        @pl.when(k == 0)
        def _():
            acc[...] = jnp.zeros_like(acc)

        acc[...] += product()

        @pl.when(k == nk - 1)
        def _():
            finish(acc[...])

    a_spec = pl.BlockSpec((tk, tm), lambda i, j, k: (k, i)) if ta else pl.BlockSpec((tm, tk), lambda i, j, k: (i, k))
    b_spec = pl.BlockSpec((tn, tk), lambda i, j, k: (j, k)) if tb else pl.BlockSpec((tk, tn), lambda i, j, k: (k, j))
    o_spec = pl.BlockSpec((tm, tn), lambda i, j, k: (i, j))
    in_specs = [a_spec, b_spec] + ([o_spec] if n_extra else []) + _dep_specs(deps)
    if epi == "relu2":
        out_shape = (jax.ShapeDtypeStruct((M, N), F32), jax.ShapeDtypeStruct((M, N), BF16))
        out_specs = (o_spec, o_spec)
    else:
        out_shape = jax.ShapeDtypeStruct((M, N), out_dtype)
        out_specs = o_spec
    args = (a, b) + ((extra,) if n_extra else ()) + tuple(deps)
    return pl.pallas_call(
        body, name=name, grid=(M // tm, N // tn, nk), in_specs=in_specs, out_specs=out_specs,
        out_shape=out_shape, scratch_shapes=[pltpu.VMEM((tm, tn), F32)] if nk > 1 else [],
        compiler_params=_params(("parallel", "parallel", "arbitrary")),
    )(*args)


ROW_BLOCK = 256


def _rmsnorm_fwd(x, g, *, name, deps=()):
    T, D = x.shape
    tr = _tile(T, (ROW_BLOCK, 128, 64))

    def body(x_ref, g_ref, *rest):
        o_ref = rest[-1]
        xv = x_ref[...]
        r = lax.rsqrt(jnp.mean(xv * xv, axis=-1, keepdims=True) + NORM_EPS)
        o_ref[...] = (xv * r * g_ref[...]).astype(o_ref.dtype)

    return pl.pallas_call(
        body, name=name, grid=(T // tr,),
        in_specs=[pl.BlockSpec((tr, D), lambda i: (i, 0)), pl.BlockSpec((1, D), lambda i: (0, 0))] + _dep_specs(deps),
        out_specs=pl.BlockSpec((tr, D), lambda i: (i, 0)),
        out_shape=jax.ShapeDtypeStruct((T, D), BF16), compiler_params=_params(("parallel",)),
    )(x, g, *deps)


def _rmsnorm_bwd(x, g, dy, dres, *, name, deps=()):
    T, D = x.shape
    tr = _tile(T, (ROW_BLOCK, 128, 64))

    def body(x_ref, g_ref, dy_ref, dres_ref, *rest):
        dx_ref, dxb_ref, dg_ref = rest[len(deps):]
        i = pl.program_id(0)
        xv = x_ref[...]
        r = lax.rsqrt(jnp.mean(xv * xv, axis=-1, keepdims=True) + NORM_EPS)
        xh = xv * r
        dyv = dy_ref[...].astype(F32)
        dxh = dyv * g_ref[...]
        dx = dres_ref[...] + r * (dxh - xh * jnp.mean(dxh * xh, axis=-1, keepdims=True))
        dx_ref[...] = dx
        dxb_ref[...] = dx.astype(dxb_ref.dtype)

        @pl.when(i == 0)
        def _():
            dg_ref[...] = jnp.zeros_like(dg_ref)

        dg_ref[...] += jnp.sum(dyv * xh, axis=0, keepdims=True)

    blk = pl.BlockSpec((tr, D), lambda i: (i, 0))
    vec = pl.BlockSpec((1, D), lambda i: (0, 0))
    return pl.pallas_call(
        body, name=name, grid=(T // tr,), in_specs=[blk, vec, blk, blk] + _dep_specs(deps), out_specs=(blk, blk, vec),
        out_shape=(jax.ShapeDtypeStruct((T, D), F32), jax.ShapeDtypeStruct((T, D), BF16),
                   jax.ShapeDtypeStruct((1, D), F32)),
        compiler_params=_params(("arbitrary",)),
    )(x, g, dy, dres, *deps)


def _final_loss_bwd(x, g, target, *, name):
    T, D = x.shape
    tr = _tile(T, (ROW_BLOCK, 128, 64))

    def body(x_ref, g_ref, t_ref, dx_ref, dxb_ref, dg_ref, l_ref):
        i = pl.program_id(0)
        xv = x_ref[...]
        r = lax.rsqrt(jnp.mean(xv * xv, axis=-1, keepdims=True) + NORM_EPS)
        xh = xv * r
        gv = g_ref[...]
        err = xh * gv - t_ref[...]
        dy = err * (1.0 / D)
        dxh = dy * gv
        dx = r * (dxh - xh * jnp.mean(dxh * xh, axis=-1, keepdims=True))
        dx_ref[...] = dx
        dxb_ref[...] = dx.astype(dxb_ref.dtype)

        @pl.when(i == 0)
        def _():
            l_ref[...] = jnp.zeros_like(l_ref)
            dg_ref[...] = jnp.zeros_like(dg_ref)

        dg_ref[...] += jnp.sum(dy * xh, axis=0, keepdims=True)
        part = 0.5 * jnp.sum(jnp.mean(err * err, axis=-1, keepdims=True), axis=0, keepdims=True)
        l_ref[...] += jnp.broadcast_to(part, l_ref.shape)

    blk = pl.BlockSpec((tr, D), lambda i: (i, 0))
    vec = pl.BlockSpec((1, D), lambda i: (0, 0))
    return pl.pallas_call(
        body, name=name, grid=(T // tr,), in_specs=[blk, vec, blk],
        out_specs=(blk, blk, vec, pl.BlockSpec((1, LANE), lambda i: (0, 0))),
        out_shape=(jax.ShapeDtypeStruct((T, D), F32), jax.ShapeDtypeStruct((T, D), BF16),
                   jax.ShapeDtypeStruct((1, D), F32), jax.ShapeDtypeStruct((1, LANE), F32)),
        compiler_params=_params(("arbitrary",)),
    )(x, g, target)


def _colsum(x, *, name):
    T, C = x.shape
    tc = _tile(C, (512, 256, 128))

    def body(x_ref, o_ref):
        o_ref[...] = jnp.sum(x_ref[...], axis=0, keepdims=True)

    return pl.pallas_call(
        body, name=name, grid=(C // tc,), in_specs=[pl.BlockSpec((T, tc), lambda j: (0, j))],
        out_specs=pl.BlockSpec((1, tc), lambda j: (0, j)),
        out_shape=jax.ShapeDtypeStruct((1, C), F32), compiler_params=_params(("parallel",)),
    )(x)


ADAM_C1 = 1.0 - ADAM_B1 ** ADAM_STEP
ADAM_C2 = 1.0 - ADAM_B2 ** ADAM_STEP


def _adamw(w, m, v, grads, *, name):
    R, C = w.shape
    tr = _tile(R, (256, 128, 64, 32, 16, 8))
    n_g = len(grads)

    def body(*refs):
        w_ref, m_ref, v_ref = refs[:3]
        g_refs = refs[3:3 + n_g]
        g_out, d_out, m_out, v_out = refs[3 + n_g:]
        g = g_refs[0][...].astype(F32)
        for r in g_refs[1:]:
            g = g + r[...].astype(F32)
        mn = ADAM_B1 * m_ref[...] + (1.0 - ADAM_B1) * g
        vn = ADAM_B2 * v_ref[...] + (1.0 - ADAM_B2) * (g * g)
        m_hat = mn / ADAM_C1
        v_hat = vn / ADAM_C2
        g_out[...] = g
        d_out[...] = -ADAM_LR * (m_hat / (jnp.sqrt(v_hat) + ADAM_EPS) + ADAM_WD * w_ref[...])
        m_out[...] = mn
        v_out[...] = vn

    blk = pl.BlockSpec((tr, C), lambda i: (i, 0))
    sh = jax.ShapeDtypeStruct((R, C), F32)
    return pl.pallas_call(
        body, name=name, grid=(R // tr,), in_specs=[blk] * (3 + n_g), out_specs=(blk,) * 4,
        out_shape=(sh,) * 4, compiler_params=_params(("parallel",)),
    )(w, m, v, *grads)


def _sum_parts(parts, *, name, deps=()):
    P, R, C = parts.shape
    tr = _tile(R, (256, 128, 64, 32, 16, 8))

    def body(p_ref, *rest):
        o_ref = rest[-1]
        s = p_ref[0].astype(F32)
        for i in range(1, P):
            s = s + p_ref[i].astype(F32)
        o_ref[...] = s

    return pl.pallas_call(
        body, name=name, grid=(R // tr,),
        in_specs=[pl.BlockSpec((P, tr, C), lambda i: (0, i, 0))] + _dep_specs(deps),
        out_specs=pl.BlockSpec((tr, C), lambda i: (i, 0)),
        out_shape=jax.ShapeDtypeStruct((R, C), F32), compiler_params=_params(("parallel",)),
    )(parts, *deps)


def _ret_tables(T):
    H, C = RET_HEADS, CHUNK
    log_gamma = jnp.log1p(-jnp.exp2(-5.0 - jnp.arange(H, dtype=F32)))
    pos = jnp.arange(C, dtype=F32)
    dist = jnp.abs(pos[:, None] - pos[None, :])
    dm = jnp.exp(log_gamma[:, None, None] * dist)
    qdec = jnp.exp(log_gamma[:, None] * (pos + 1.0))[:, :, None]
    kdec = jnp.exp(log_gamma[:, None] * (C - 1.0 - pos))[:, :, None]
    cdec = jnp.exp(log_gamma * C)[:, None, None]
    inv = ROPE_BASE ** (-jnp.arange(0, RET_DK, 2, dtype=F32) / RET_DK)
    ang = jnp.arange(T, dtype=F32)[:, None] * inv[None, :]
    return dm, qdec, kdec, cdec, jnp.cos(ang), jnp.sin(ang)


def _rot(x, cos, sin):
    h = x.shape[1] // 2
    x1, x2 = x[:, :h], x[:, h:]
    return jnp.concatenate([x1 * cos - x2 * sin, x1 * sin + x2 * cos], axis=1)


def _unrot(dy, cos, sin):
    h = dy.shape[1] // 2
    d1, d2 = dy[:, :h], dy[:, h:]
    return jnp.concatenate([d1 * cos + d2 * sin, d2 * cos - d1 * sin], axis=1)


def _ret_specs(N, rev):
    H, C, DK, DV = RET_HEADS, CHUNK, RET_DK, RET_DV
    cn = (lambda n: N - 1 - n) if rev else (lambda n: n)
    HB, G = RET_HB, H // RET_HB
    q = pl.BlockSpec((C, HB * DK), lambda h, n: (cn(n), h))
    k = pl.BlockSpec((C, HB * DK), lambda h, n: (cn(n), G + h))
    v = pl.BlockSpec((C, HB * DV), lambda h, n: (cn(n), G + h))
    g = pl.BlockSpec((C, HB * DV), lambda h, n: (cn(n), 2 * G + h))
    cs = pl.BlockSpec((C, DK // 2), lambda h, n: (cn(n), 0))
    dm = pl.BlockSpec((HB, C, C), lambda h, n: (h, 0, 0))
    dec = pl.BlockSpec((HB, C, 1), lambda h, n: (h, 0, 0))
    cd = pl.BlockSpec((HB, 1, 1), lambda h, n: (h, 0, 0))
    gain = pl.BlockSpec((HB, 1, DV), lambda h, n: (h, 0, 0))
    st = pl.BlockSpec((HB, 1, DK, DV), lambda h, n: (h, cn(n), 0, 0))
    ov = pl.BlockSpec((C, HB * DV), lambda h, n: (cn(n), h))
    return q, k, v, g, cs, dm, dec, cd, gain, st, ov


def _ret_fwd(proj, gain, tables, *, name):
    T = proj.shape[0]
    H, C, DK, DV = RET_HEADS, CHUNK, RET_DK, RET_DV
    N = T // C
    dm_t, qdec_t, kdec_t, cdec_t, cos_t, sin_t = tables

    def body(q_ref, k_ref, v_ref, g_ref, cos_ref, sin_ref, dm_ref, qd_ref, kd_ref, cd_ref, gain_ref,
             og_ref, st_ref, s_acc):
        n = pl.program_id(1)

        @pl.when(n == 0)
        def _():
            s_acc[...] = jnp.zeros_like(s_acc)

        cos, sin = cos_ref[...], sin_ref[...]

        def head(hb):
            kc, vc = pl.ds(hb * DK, DK), pl.ds(hb * DV, DV)
            qr = _rot(q_ref[:, kc], cos, sin)
            kr = _rot(k_ref[:, kc], cos, sin) * (DK ** -0.5)
            v = v_ref[:, vc]
            sp = s_acc[hb]
            st_ref[hb, 0] = sp.astype(st_ref.dtype)
            scores = _bdot(qr, kr, NT) * dm_ref[hb]
            inter = _bdot(qr * qd_ref[hb], sp)
            s_acc[hb] = sp * cd_ref[hb] + _bdot(kr * kd_ref[hb], v, TN)
            yield
            o = _bdot(scores, v) + inter
            yield
            oc = o - jnp.mean(o, axis=-1, keepdims=True)
            rstd = lax.rsqrt(jnp.mean(oc * oc, axis=-1, keepdims=True) + NORM_EPS)
            og_ref[:, vc] = (oc * rstd * gain_ref[hb] * _silu(g_ref[:, vc])).astype(og_ref.dtype)

        _interleave([head(hb) for hb in range(RET_HB)])

    q, k, v, g, cs, dm, dec, cd, gn, st, ov = _ret_specs(N, False)
    return pl.pallas_call(
        body, name=name, grid=(H // RET_HB, N),
        in_specs=[q, k, v, g, cs, cs, dm, dec, dec, cd, gn], out_specs=(ov, st),
        out_shape=(jax.ShapeDtypeStruct((T, H * DV), BF16), jax.ShapeDtypeStruct((H, N, DK, DV), BF16)),
        scratch_shapes=[pltpu.VMEM((RET_HB, DK, DV), F32)],
        compiler_params=_params(("arbitrary", "arbitrary")),
    )(proj, proj, proj, proj, cos_t, sin_t, dm_t, qdec_t, kdec_t, cdec_t, gain)


def _ret_bwd(proj, gain, tables, states, dog, *, name):
    T = proj.shape[0]
    H, C, DK, DV = RET_HEADS, CHUNK, RET_DK, RET_DV
    N = T // C
    dm_t, qdec_t, kdec_t, cdec_t, cos_t, sin_t = tables

    def body(q_ref, k_ref, v_ref, g_ref, cos_ref, sin_ref, dm_ref, qd_ref, kd_ref, cd_ref, gain_ref,
             st_ref, dog_ref, dq_ref, dk_ref, dv_ref, dg_ref, dgain_ref, ds_acc):
        n = pl.program_id(1)

        @pl.when(n == 0)
        def _():
            ds_acc[...] = jnp.zeros_like(ds_acc)
            dgain_ref[...] = jnp.zeros_like(dgain_ref)

        cos, sin = cos_ref[...], sin_ref[...]
        scale = DK ** -0.5

        def head(hb):
            kc, vc = pl.ds(hb * DK, DK), pl.ds(hb * DV, DV)
            qr = _rot(q_ref[:, kc], cos, sin)
            kr = _rot(k_ref[:, kc], cos, sin) * scale
            v = v_ref[:, vc]
            g = g_ref[:, vc]
            sp = st_ref[hb, 0]
            dm = dm_ref[hb]
            qd, kd = qd_ref[hb], kd_ref[hb]
            gain_v = gain_ref[hb]
            scores = _bdot(qr, kr, NT) * dm
            inter = _bdot(qr * qd, sp)
            yield
            o = _bdot(scores, v) + inter
            yield
            oc = o - jnp.mean(o, axis=-1, keepdims=True)
            rstd = lax.rsqrt(jnp.mean(oc * oc, axis=-1, keepdims=True) + NORM_EPS)
            oh = oc * rstd
            dy = dog_ref[:, vc].astype(F32)
            dg_ref[:, vc] = (dy * oh * gain_v * _dsilu(g)).astype(dg_ref.dtype)
            dnorm = dy * _silu(g)
            dgain_ref[hb] += jnp.sum(dnorm * oh, axis=0, keepdims=True)
            doh = dnorm * gain_v
            do = rstd * (doh - jnp.mean(doh, axis=-1, keepdims=True)
                         - oh * jnp.mean(doh * oh, axis=-1, keepdims=True))
            dsn = ds_acc[hb]
            dp = _bdot(do, v, NT) * dm
            dq_inter = _bdot(do, sp, NT) * qd
            dk_inter = _bdot(v, dsn, NT) * kd
            dv_ref[:, vc] = (_bdot(scores, do, TN) + _bdot(kr * kd, dsn)).astype(dv_ref.dtype)
            ds_acc[hb] = dsn * cd_ref[hb] + _bdot(qr * qd, do, TN)
            yield
            dqr = _bdot(dp, kr) + dq_inter
            dkr = _bdot(dp, qr, TN) + dk_inter
            yield
            dq_ref[:, kc] = _unrot(dqr, cos, sin).astype(dq_ref.dtype)
            dk_ref[:, kc] = _unrot(dkr * scale, cos, sin).astype(dk_ref.dtype)

        _interleave([head(hb) for hb in range(RET_HB)])

    q, k, v, g, cs, dm, dec, cd, gn, st, ov = _ret_specs(N, True)
    return pl.pallas_call(
        body, name=name, grid=(H // RET_HB, N),
        in_specs=[q, k, v, g, cs, cs, dm, dec, dec, cd, gn, st, ov],
        out_specs=(q, q, ov, ov, gn),
        out_shape=(jax.ShapeDtypeStruct((T, H * DK), BF16), jax.ShapeDtypeStruct((T, H * DK), BF16),
                   jax.ShapeDtypeStruct((T, H * DV), BF16), jax.ShapeDtypeStruct((T, H * DV), BF16),
                   jax.ShapeDtypeStruct((H, 1, DV), F32)),
        scratch_shapes=[pltpu.VMEM((RET_HB, DK, DV), F32)],
        compiler_params=_params(("arbitrary", "arbitrary")),
    )(proj, proj, proj, proj, cos_t, sin_t, dm_t, qdec_t, kdec_t, cdec_t, gain, states, dog)


def _gla_specs(N, rev):
    H, C, DK, DV = GLA_HEADS, CHUNK, GLA_DK, GLA_DV
    cn = (lambda n: N - 1 - n) if rev else (lambda n: n)
    HB, G = GLA_HB, H // GLA_HB
    q = pl.BlockSpec((C, HB * DK), lambda h, n: (cn(n), h))
    k = pl.BlockSpec((C, HB * DK), lambda h, n: (cn(n), G + h))
    v = pl.BlockSpec((C, HB * DV), lambda h, n: (cn(n), G + h))
    r = pl.BlockSpec((C, HB * DV), lambda h, n: (cn(n), 2 * G + h))
    bias = pl.BlockSpec((1, HB * DK), lambda h, n: (0, h))
    gain = pl.BlockSpec((HB, 1, DV), lambda h, n: (h, 0, 0))
    st = pl.BlockSpec((HB, 1, DV, DK), lambda h, n: (h, cn(n), 0, 0))
    ov = pl.BlockSpec((C, HB * DV), lambda h, n: (cn(n), h))
    return q, k, v, r, bias, gain, st, ov


def _gla_chunk(q, k, v, gl_raw, bias):
    C, DK = q.shape
    gl = gl_raw + bias
    la = -_softplus(-gl) * (1.0 / GLA_TAU)
    lower = _iota2((C, C), 0) >= _iota2((C, C), 1)
    cum = _fdot(jnp.where(lower, 1.0, 0.0), la)
    yield
    ref = _pick_row(cum, C // 2 - 1)
    clast = _pick_row(cum, C - 1)
    fw, bw = jnp.exp(cum - ref), jnp.exp(ref - cum)
    qs = q * (DK ** -0.5)
    s_lo = _bdot(qs * fw, k * bw, NT)
    s_up = _bdot(qs * bw, k * fw, NT)
    yield
    scores = jnp.where(lower, s_lo, s_up)
    return gl, cum, clast, fw, bw, qs, k, v, scores, lower


def _gla_fwd(proj, glogit, bias, gain, *, name):
    T = proj.shape[0]
    H, C, DK, DV = GLA_HEADS, CHUNK, GLA_DK, GLA_DV
    N = T // C

    def body(q_ref, k_ref, v_ref, r_ref, gl_ref, bias_ref, gain_ref, og_ref, st_ref, s_acc):
        n = pl.program_id(1)

        @pl.when(n == 0)
        def _():
            s_acc[...] = jnp.zeros_like(s_acc)

        def head(hb):
            kc, vc = pl.ds(hb * DK, DK), pl.ds(hb * DV, DV)
            gl, cum, clast, fw, bw, qs, k, v, scores, lower = yield from _gla_chunk(
                q_ref[:, kc], k_ref[:, kc], v_ref[:, vc], gl_ref[:, kc], bias_ref[:, kc])
            sp = s_acc[hb]
            st_ref[hb, 0] = sp
            o = _bdot(scores, v) + _bdot(qs * jnp.exp(cum), sp, NT)
            s_acc[hb] = sp * jnp.exp(clast) + _bdot(v, k * jnp.exp(clast - cum), TN)
            yield
            rstd = lax.rsqrt(jnp.mean(o * o, axis=-1, keepdims=True) + NORM_EPS)
            og_ref[:, vc] = (o * rstd * gain_ref[hb] * _silu(r_ref[:, vc])).astype(og_ref.dtype)

        _interleave([head(hb) for hb in range(GLA_HB)])

    q, k, v, r, bias_s, gn, st, ov = _gla_specs(N, False)
    return pl.pallas_call(
        body, name=name, grid=(H // GLA_HB, N), in_specs=[q, k, v, r, q, bias_s, gn], out_specs=(ov, st),
        out_shape=(jax.ShapeDtypeStruct((T, H * DV), BF16), jax.ShapeDtypeStruct((H, N, DV, DK), F32)),
        scratch_shapes=[pltpu.VMEM((GLA_HB, DV, DK), F32)],
        compiler_params=_params(("arbitrary", "arbitrary")),
    )(proj, proj, proj, proj, glogit, bias, gain)


def _gla_bwd(proj, glogit, bias, gain, states, dog, *, name):
    T = proj.shape[0]
    H, C, DK, DV = GLA_HEADS, CHUNK, GLA_DK, GLA_DV
    N = T // C

    def body(q_ref, k_ref, v_ref, r_ref, gl_ref, bias_ref, gain_ref, st_ref, dog_ref,
             dq_ref, dk_ref, dv_ref, dr_ref, dgl_ref, dgain_ref, ds_acc):
        n = pl.program_id(1)

        @pl.when(n == 0)
        def _():
            ds_acc[...] = jnp.zeros_like(ds_acc)
            dgain_ref[...] = jnp.zeros_like(dgain_ref)

        def head(hb):
            kc, vc = pl.ds(hb * DK, DK), pl.ds(hb * DV, DV)
            gl, cum, clast, fw, bw, qs, k, v, scores, lower = yield from _gla_chunk(
                q_ref[:, kc], k_ref[:, kc], v_ref[:, vc], gl_ref[:, kc], bias_ref[:, kc])
            sp = st_ref[hb, 0]
            ecum, e2, cdec = jnp.exp(cum), jnp.exp(clast - cum), jnp.exp(clast)
            q_in, k_end = qs * ecum, k * e2
            o = _bdot(scores, v) + _bdot(q_in, sp, NT)
            yield
            rstd = lax.rsqrt(jnp.mean(o * o, axis=-1, keepdims=True) + NORM_EPS)
            oh = o * rstd
            r = r_ref[:, vc]
            gain_v = gain_ref[hb]
            dy = dog_ref[:, vc].astype(F32)
            dr_ref[:, vc] = (dy * oh * gain_v * _dsilu(r)).astype(dr_ref.dtype)
            dnorm = dy * _silu(r)
            dgain_ref[hb] += jnp.sum(dnorm * oh, axis=0, keepdims=True)
            doh = dnorm * gain_v
            do = rstd * (doh - oh * jnp.mean(doh * oh, axis=-1, keepdims=True))
            dsn = ds_acc[hb]
            dq_in = _bdot(do, sp)
            dk_end = _bdot(v, dsn)
            dv_ref[:, vc] = (_bdot(k_end, dsn, NT) + _bdot(scores, do, TN)).astype(dv_ref.dtype)
            dcdec = jnp.sum(dsn * sp, axis=0, keepdims=True)
            ds_acc[hb] = dsn * cdec + _bdot(do, q_in, TN)
            dsc = _bdot(do, v, NT)
            yield
            ds_lo = jnp.where(lower, dsc, 0.0)
            ds_up = jnp.where(lower, 0.0, dsc)
            qf, kb, qb, kf = qs * fw, k * bw, qs * bw, k * fw
            dqf, dkb = _bdot(ds_lo, kb), _bdot(ds_lo, qf, TN)
            dqb, dkf = _bdot(ds_up, kf), _bdot(ds_up, qb, TN)
            yield
            dq_ref[:, kc] = ((dqf * fw + dqb * bw + dq_in * ecum) * (DK ** -0.5)).astype(dq_ref.dtype)
            dk_ref[:, kc] = (dkb * bw + dkf * fw + dk_end * e2).astype(dk_ref.dtype)
            dz = (dqf * qs + dkf * k) * fw - (dqb * qs + dkb * k) * bw
            kk = dk_end * k_end
            dcum = dz + dq_in * q_in - kk
            rows = _iota2((C, DK), 0)
            dcum = dcum + jnp.where(rows == C // 2 - 1, -jnp.sum(dz, axis=0, keepdims=True), 0.0)
            dcum = dcum + jnp.where(rows == C - 1, jnp.sum(kk, axis=0, keepdims=True) + dcdec * cdec, 0.0)
            upper = _iota2((C, C), 0) <= _iota2((C, C), 1)
            dla = _fdot(jnp.where(upper, 1.0, 0.0), dcum)
            yield
            dgl_ref[:, kc] = dla * (1.0 / GLA_TAU) * _sigmoid(-gl)

        _interleave([head(hb) for hb in range(GLA_HB)])

    q, k, v, r, bias_s, gn, st, ov = _gla_specs(N, True)
    return pl.pallas_call(
        body, name=name, grid=(H // GLA_HB, N), in_specs=[q, k, v, r, q, bias_s, gn, st, ov],
        out_specs=(q, q, ov, ov, q, gn),
        out_shape=(jax.ShapeDtypeStruct((T, H * DK), BF16), jax.ShapeDtypeStruct((T, H * DK), BF16),
                   jax.ShapeDtypeStruct((T, H * DV), BF16), jax.ShapeDtypeStruct((T, H * DV), BF16),
                   jax.ShapeDtypeStruct((T, H * DK), F32), jax.ShapeDtypeStruct((H, 1, DV), F32)),
        scratch_shapes=[pltpu.VMEM((GLA_HB, DV, DK), F32)],
        compiler_params=_params(("arbitrary", "arbitrary")),
    )(proj, proj, proj, proj, glogit, bias, gain, states, dog)


def _conv(xv, w_ref):
    out = _shift_down(xv, CONV_WIDTH - 1) * w_ref[0:1, :]
    for tap in range(1, CONV_WIDTH):
        out = out + _shift_down(xv, CONV_WIDTH - 1 - tap) * w_ref[tap:tap + 1, :]
    return out


def _conv_bwd(xv, w_ref, dpre, dw_ref):
    dx = None
    for tap in range(CONV_WIDTH):
        s = CONV_WIDTH - 1 - tap
        t = _shift_up(dpre, s) * w_ref[tap:tap + 1, :]
        dx = t if dx is None else dx + t
        dw_ref[tap:tap + 1, :] = jnp.sum(dpre * _shift_down(xv, s), axis=0, keepdims=True)
    return dx


CONV_COLS = 256


def _conv_silu_fwd(x, w, *, name):
    T = x.shape[0]
    n = w.shape[1]

    def body(x_ref, w_ref, o_ref):
        o_ref[...] = _silu(_conv(x_ref[...], w_ref))

    return pl.pallas_call(
        body, name=name, grid=(n // CONV_COLS,),
        in_specs=[pl.BlockSpec((T, CONV_COLS), lambda j: (0, j)), pl.BlockSpec((CONV_WIDTH, CONV_COLS), lambda j: (0, j))],
        out_specs=pl.BlockSpec((T, CONV_COLS), lambda j: (0, j)),
        out_shape=jax.ShapeDtypeStruct((T, n), F32), compiler_params=_params(("parallel",)),
    )(x, w)


def _conv_silu_bwd(x, w, dact, *, name):
    T = x.shape[0]
    n = w.shape[1]

    def body(x_ref, w_ref, da_ref, dx_ref, dw_ref):
        xv = x_ref[...]
        dpre = da_ref[...] * _dsilu(_conv(xv, w_ref))
        dx_ref[...] = _conv_bwd(xv, w_ref, dpre, dw_ref).astype(dx_ref.dtype)

    blk = pl.BlockSpec((T, CONV_COLS), lambda j: (0, j))
    wb = pl.BlockSpec((CONV_WIDTH, CONV_COLS), lambda j: (0, j))
    return pl.pallas_call(
        body, name=name, grid=(n // CONV_COLS,), in_specs=[blk, wb, blk], out_specs=(blk, wb),
        out_shape=(jax.ShapeDtypeStruct((T, n), BF16), jax.ShapeDtypeStruct((CONV_WIDTH, n), F32)),
        compiler_params=_params(("parallel",)),
    )(x, w, dact)


def _interleave(gens):
    results = [None] * len(gens)
    live = list(range(len(gens)))
    while live:
        for i in list(live):
            try:
                next(gens[i])
            except StopIteration as done:
                results[i] = done.value
                live.remove(i)
    return results


def _unit_lower_inverse(a):
    n = a.shape[0]
    eye = jnp.where(_iota2((n, n), 0) == _iota2((n, n), 1), 1.0, 0.0)
    p = -a
    t = eye + p
    for _ in range(5):
        p = _fdot(p, p)
        yield
        t = t + _fdot(t, p)
        yield
    return t


def _gdn_specs(N, rev):
    H, C, DK, DV = GDN_HEADS, CHUNK, GDN_DK, GDN_DV
    cn = (lambda n: N - 1 - n) if rev else (lambda n: n)
    HB, G = GDN_HB, H // GDN_HB
    q = pl.BlockSpec((C, HB * DK), lambda h, n: (cn(n), h))
    k = pl.BlockSpec((C, HB * DK), lambda h, n: (cn(n), G + h))
    v = pl.BlockSpec((C, HB * DV), lambda h, n: (cn(n), 2 * G + h))
    z = pl.BlockSpec((C, HB * DV), lambda h, n: (cn(n), 3 * G + h))
    gates = pl.BlockSpec((C, LANE), lambda h, n: (cn(n), 0))
    sc = pl.BlockSpec((HB, 1, 1), lambda h, n: (h, 0, 0))
    gain = pl.BlockSpec((1, DV), lambda h, n: (0, 0))
    st = pl.BlockSpec((HB, 1, DK, DV), lambda h, n: (h, cn(n), 0, 0))
    return q, k, v, z, gates, sc, gain, st


def _gdn_solved_specs(N, rev):
    C = CHUNK
    cn = (lambda n: N - 1 - n) if rev else (lambda n: n)
    return (pl.BlockSpec((GDN_HB, 1, C, C), lambda h, n: (h, cn(n), 0, 0)),
            pl.BlockSpec((GDN_HB, 1, C, GDN_DV + GDN_DK), lambda h, n: (h, cn(n), 0, 0)))


def _gdn_chunk(q_ref, k_ref, v_ref, gates_ref, alog_ref, dtb_ref, h, solved=None):
    H, C, DK, DV = GDN_HEADS, CHUNK, GDN_DK, GDN_DV
    gates = gates_ref[...]
    lane = _iota2(gates.shape, 1)
    bl = jnp.sum(jnp.where(lane == h, gates, 0.0), axis=1, keepdims=True)
    al = jnp.sum(jnp.where(lane == H + h, gates, 0.0), axis=1, keepdims=True)
    beta = _sigmoid(bl)
    ea = jnp.exp(alog_ref[...])
    xs = al + dtb_ref[...]
    la = -ea * _softplus(xs)
    ii, jj = _iota2((C, C), 0), _iota2((C, C), 1)
    strict = ii > jj
    cum_col = jnp.sum(jnp.where(ii >= jj, _col_to_row(la), 0.0), axis=1, keepdims=True)
    cum_row = jnp.sum(jnp.where(ii <= jj, la, 0.0), axis=0, keepdims=True)
    q, k, v = q_ref[...], k_ref[...], v_ref[...]
    rq = lax.rsqrt(jnp.sum(q * q, axis=-1, keepdims=True) + NORM_EPS)
    rk = lax.rsqrt(jnp.sum(k * k, axis=-1, keepdims=True) + NORM_EPS)
    qn = q * rq * (DK ** -0.5)
    kn = k * rk
    rel = jnp.where(strict, jnp.exp(jnp.where(strict, cum_col - cum_row, 0.0)), 0.0)
    rg = rel * _bdot(kn, kn, NT)
    yield
    a = beta * rg
    e_col = jnp.exp(cum_col)
    clast = _pick_row(cum_col, C - 1)
    if solved is None:
        tm = yield from _unit_lower_inverse(a)
        rhs = jnp.concatenate([beta * v, (beta * e_col) * kn], axis=1)
        sol = _fdot(tm, rhs)
        yield
    else:
        tm, sol = solved
    u, w = sol[:, :DV], sol[:, DV:]
    dd = jnp.exp(clast - cum_col)
    ke = kn * dd
    g = jnp.exp(clast)
    eye = jnp.where(_iota2((DK, DK), 0) == _iota2((DK, DK), 1), 1.0, 0.0)
    trans = g * eye - _bdot(ke, w, TN)
    inject = _bdot(ke, u, TN)
    yield
    return dict(beta=beta, ea=ea, xs=xs, la=la, strict=strict, ii=ii, jj=jj, q=q, k=k, v=v, rq=rq, rk=rk,
                qn=qn, kn=kn, rel=rel, rg=rg, a=a, tm=tm, e_col=e_col, sol=sol, u=u, w=w, dd=dd, ke=ke,
                g=g, eye=eye, trans=trans, inject=inject)


def _gdn_fwd(act, proj, gates, a_log, dt_bias, gain, *, name):
    T = act.shape[0]
    H, C, DK, DV = GDN_HEADS, CHUNK, GDN_DK, GDN_DV
    N = T // C

    def body(q_ref, k_ref, v_ref, z_ref, gates_ref, alog_ref, dtb_ref, gain_ref, og_ref, st_ref, tm_ref, sol_ref,
             s_acc):
        hg, n = pl.program_id(0), pl.program_id(1)

        @pl.when(n == 0)
        def _():
            s_acc[...] = jnp.zeros_like(s_acc)

        def head(hb):
            cols = pl.ds(hb * DK, DK)
            c = yield from _gdn_chunk(q_ref.at[:, cols], k_ref.at[:, cols], v_ref.at[:, cols], gates_ref,
                                      alog_ref.at[hb], dtb_ref.at[hb], hg * GDN_HB + hb)
            tm_ref[hb, 0] = c["tm"]
            sol_ref[hb, 0] = c["sol"]
            sp = s_acc[hb]
            st_ref[hb, 0] = sp
            snew = _bdot(c["trans"], sp) + c["inject"]
            yield
            s_acc[hb] = snew
            o = _bdot(c["qn"], snew)
            yield
            rstd = lax.rsqrt(jnp.mean(o * o, axis=-1, keepdims=True) + NORM_EPS)
            og_ref[:, cols] = (o * rstd * gain_ref[...] * _silu(z_ref[:, cols])).astype(og_ref.dtype)

        _interleave([head(hb) for hb in range(GDN_HB)])

    q, k, v, z, gt, sc, gn, st = _gdn_specs(N, False)
    tm_s, sol_s = _gdn_solved_specs(N, False)
    return pl.pallas_call(
        body, name=name, grid=(H // GDN_HB, N), in_specs=[q, k, v, z, gt, sc, sc, gn],
        out_specs=(q, st, tm_s, sol_s),
        out_shape=(jax.ShapeDtypeStruct((T, H * DV), BF16), jax.ShapeDtypeStruct((H, N, DK, DV), F32),
                   jax.ShapeDtypeStruct((H, N, C, C), F32), jax.ShapeDtypeStruct((H, N, C, DV + DK), F32)),
        scratch_shapes=[pltpu.VMEM((GDN_HB, DK, DV), F32)],
        compiler_params=_params(("arbitrary", "arbitrary")),
    )(act, act, act, proj, gates, a_log, dt_bias, gain)


def _gdn_bwd(act, proj, gates, a_log, dt_bias, gain, states, solved, dog, *, name):
    T = act.shape[0]
    H, C, DK, DV = GDN_HEADS, CHUNK, GDN_DK, GDN_DV
    N = T // C

    def rsum(x):
        return jnp.sum(x, axis=1, keepdims=True)

    def body(q_ref, k_ref, v_ref, z_ref, gates_ref, alog_ref, dtb_ref, gain_ref, st_ref, tm_ref, sol_ref, dog_ref,
             dq_ref, dk_ref, dv_ref, dz_ref, dgates_ref, dscal_ref, dgain_ref, ds_acc):
        hg, n = pl.program_id(0), pl.program_id(1)

        @pl.when(n == 0)
        def _():
            ds_acc[...] = jnp.zeros_like(ds_acc)
            dgain_ref[...] = jnp.zeros_like(dgain_ref)
            dscal_ref[...] = jnp.zeros_like(dscal_ref)

        _interleave([one_head(hb, hg * GDN_HB + hb, q_ref, k_ref, v_ref, z_ref, gates_ref, alog_ref, dtb_ref, gain_ref,
                              st_ref, tm_ref, sol_ref, dog_ref, dq_ref, dk_ref, dv_ref, dz_ref, dgates_ref, dscal_ref,
                              dgain_ref, ds_acc)
                     for hb in range(GDN_HB)])

    def one_head(hb, h, q_ref, k_ref, v_ref, z_ref, gates_ref, alog_ref, dtb_ref, gain_ref, st_ref, tm_ref, sol_ref,
                 dog_ref, dq_ref, dk_ref, dv_ref, dz_ref, dgates_ref, dscal_ref, dgain_ref, ds_acc):
        cols = pl.ds(hb * DK, DK)
        c = yield from _gdn_chunk(q_ref.at[:, cols], k_ref.at[:, cols], v_ref.at[:, cols], gates_ref,
                                  alog_ref.at[hb], dtb_ref.at[hb], h, solved=(tm_ref[hb, 0], sol_ref[hb, 0]))
        beta, kn, qn, v, ke, u, w, dd, e_col = c["beta"], c["kn"], c["qn"], c["v"], c["ke"], c["u"], c["w"], c["dd"], c["e_col"]
        sp = st_ref[hb, 0]
        snew = _bdot(c["trans"], sp) + c["inject"]
        yield
        o = _bdot(qn, snew)
        yield
        rstd = lax.rsqrt(jnp.mean(o * o, axis=-1, keepdims=True) + NORM_EPS)
        oh = o * rstd
        z = z_ref[:, cols]
        gain_v = gain_ref[...]
        dy = dog_ref[:, cols].astype(F32)
        dz_ref[:, cols] = (dy * oh * gain_v * _dsilu(z)).astype(dz_ref.dtype)
        dnorm = dy * _silu(z)
        dgain_ref[hb] += jnp.sum(dnorm * oh, axis=0, keepdims=True)
        doh = dnorm * gain_v
        do = rstd * (doh - oh * jnp.mean(doh * oh, axis=-1, keepdims=True))

        dstot = ds_acc[hb] + _bdot(qn, do, TN)
        dqn = _bdot(do, snew, NT)
        yield
        dtrans = _bdot(dstot, sp, NT)
        ds_acc[hb] = _bdot(c["trans"], dstot, TN)
        yield
        dg = jnp.sum(jnp.sum(dtrans * c["eye"], axis=1, keepdims=True), axis=0, keepdims=True)
        m = -dtrans
        dke = _bdot(w, m, NT) + _bdot(u, dstot, NT)
        dw = _bdot(ke, m)
        du = _bdot(ke, dstot)
        yield
        drhs = _fdot(c["tm"], jnp.concatenate([du, dw], axis=1), TN)
        yield
        da = jnp.where(c["strict"], -_fdot(drhs, c["sol"], NT), 0.0)
        yield
        drhs_u, drhs_w = drhs[:, :DV], drhs[:, DV:]
        rwk = rsum(drhs_w * kn)
        dbeta = rsum(da * c["rg"]) + rsum(drhs_u * v) + rwk * e_col
        dgm = da * beta * c["rel"]
        dkn = _bdot(dgm, kn) + _bdot(dgm, kn, TN) + (beta * e_col) * drhs_w + dd * dke
        yield
        dv_ref[:, cols] = beta * drhs_u
        r_ = da * c["a"]
        ddd = rsum(dke * kn)
        dc = rsum(r_) - _row_to_col(jnp.sum(r_, axis=0, keepdims=True)) + beta * rwk * e_col - ddd * dd
        dclast = jnp.sum(ddd * dd, axis=0, keepdims=True) + dg * c["g"]
        dc = dc + jnp.where(_iota2((C, 1), 0) == C - 1, dclast, 0.0)
        dla = jnp.sum(jnp.where(c["ii"] <= c["jj"], _col_to_row(dc), 0.0), axis=1, keepdims=True)
        dalog = jnp.sum(dla * c["la"], axis=0, keepdims=True)
        dxs = dla * (-c["ea"]) * _sigmoid(c["xs"])
        ddtb = jnp.sum(dxs, axis=0, keepdims=True)
        dbl = dbeta * beta * (1.0 - beta)
        lane = _iota2((C, LANE), 1)
        dgates_ref[hb] = jnp.where(lane == 0, dbl, jnp.where(lane == 1, dxs, 0.0))
        lane8 = _iota2((8, LANE), 1)
        dscal_ref[hb] += jnp.where(lane8 == 0, dalog, jnp.where(lane8 == 1, ddtb, 0.0))
        dk_ref[:, cols] = c["rk"] * (dkn - kn * rsum(dkn * kn))
        qh = c["q"] * c["rq"]
        dqs = dqn * (DK ** -0.5)
        dq_ref[:, cols] = c["rq"] * (dqs - qh * rsum(dqs * qh))

    q, k, v, z, gt, sc, gn, st = _gdn_specs(N, True)
    dgt = pl.BlockSpec((GDN_HB, C, LANE), lambda h, n: (h, N - 1 - n, 0))
    dsc = pl.BlockSpec((GDN_HB, 8, LANE), lambda h, n: (h, 0, 0))
    dgn = pl.BlockSpec((GDN_HB, 1, DV), lambda h, n: (h, 0, 0))
    sh = jax.ShapeDtypeStruct((T, H * DK), F32)
    tm_s, sol_s = _gdn_solved_specs(N, True)
    return pl.pallas_call(
        body, name=name, grid=(H // GDN_HB, N), in_specs=[q, k, v, z, gt, sc, sc, gn, st, tm_s, sol_s, q],
        out_specs=(q, q, q, q, dgt, dsc, dgn),
        out_shape=(sh, sh, sh, jax.ShapeDtypeStruct((T, H * DV), BF16),
                   jax.ShapeDtypeStruct((H, T, LANE), F32), jax.ShapeDtypeStruct((H, 8, LANE), F32),
                   jax.ShapeDtypeStruct((H, 1, DV), F32)),
        scratch_shapes=[pltpu.VMEM((GDN_HB, DK, DV), F32)],
        compiler_params=_params(("arbitrary", "arbitrary")),
    )(act, act, act, proj, gates, a_log, dt_bias, gain, states, *solved, dog)


SUBLANES = 8


def _linear_scan(a_ref, b_ref, h_ref, reverse):
    T, W = a_ref.shape
    nb = T // SUBLANES
    row = _iota2((SUBLANES, W), 0)

    def blk(bi, carry):
        bb = (nb - 1 - bi) if reverse else bi
        off = pl.multiple_of(bb * SUBLANES, SUBLANES)
        a = a_ref[pl.ds(off, SUBLANES), :]
        b = b_ref[pl.ds(off, SUBLANES), :]
        for d in (1, 2, 4):
            if reverse:
                edge = row >= SUBLANES - d
                a_sh = jnp.where(edge, 1.0, pltpu.roll(a, SUBLANES - d, 0))
                b_sh = jnp.where(edge, 0.0, pltpu.roll(b, SUBLANES - d, 0))
            else:
                edge = row < d
                a_sh = jnp.where(edge, 1.0, pltpu.roll(a, d, 0))
                b_sh = jnp.where(edge, 0.0, pltpu.roll(b, d, 0))
            b = a * b_sh + b
            a = a * a_sh
        h = a * carry + b
        h_ref[pl.ds(off, SUBLANES), :] = h
        return h[0:1, :] if reverse else h[SUBLANES - 1:SUBLANES, :]

    lax.fori_loop(0, nb, blk, jnp.zeros((1, W), F32))


def _lru_specs(T):
    B, W = LRU_BLOCKS, LRU_BLOCK
    xb = pl.BlockSpec((T, W), lambda j: (0, j))
    yb = pl.BlockSpec((T, W), lambda j: (0, B + j))
    cw = pl.BlockSpec((CONV_WIDTH, W), lambda j: (0, j))
    vec = pl.BlockSpec((1, W), lambda j: (0, j))
    wg = pl.BlockSpec((1, W, W), lambda j: (j, 0, 0))
    bg = pl.BlockSpec((1, 1, W), lambda j: (j, 0, 0))
    return xb, yb, cw, vec, wg, bg


def _lru_gates(xb_ref, cw_ref, cb_ref, wr_ref, br_ref, wi_ref, bi_ref, lam_ref):
    xv = xb_ref[...]
    xc = _conv(xv, cw_ref) + cb_ref[...]
    r = _sigmoid(_bdot(xc, wr_ref[0]) + br_ref[0])
    i = _sigmoid(_bdot(xc, wi_ref[0]) + bi_ref[0])
    sp = _softplus(-lam_ref[...])
    la = -LRU_C * sp * r
    a = jnp.exp(la)
    s = jnp.sqrt(-_expm1(2.0 * la))
    return xv, xc, r, i, sp, a, s


def _lru_fwd(proj, conv_w, conv_b, w_r, b_r, w_i, b_i, lam, *, name):
    T = proj.shape[0]
    B, W = LRU_BLOCKS, LRU_BLOCK

    def body(xb_ref, yb_ref, cw_ref, cb_ref, wr_ref, br_ref, wi_ref, bi_ref, lam_ref, og_ref, hs_ref, a_s, u_s):
        xv, xc, r, i, sp, a, s = _lru_gates(xb_ref, cw_ref, cb_ref, wr_ref, br_ref, wi_ref, bi_ref, lam_ref)
        a_s[...] = a
        u_s[...] = s * (i * xc)
        _linear_scan(a_s, u_s, hs_ref, False)
        og_ref[...] = (hs_ref[...] * _gelu(yb_ref[...])).astype(og_ref.dtype)

    xb, yb, cw, vec, wg, bg = _lru_specs(T)
    return pl.pallas_call(
        body, name=name, grid=(B,), in_specs=[xb, yb, cw, vec, wg, bg, wg, bg, vec], out_specs=(xb, xb),
        out_shape=(jax.ShapeDtypeStruct((T, B * W), BF16), jax.ShapeDtypeStruct((T, B * W), F32)),
        scratch_shapes=[pltpu.VMEM((T, W), F32), pltpu.VMEM((T, W), F32)],
        compiler_params=_params(("arbitrary",)),
    )(proj, proj, conv_w, conv_b, w_r, b_r, w_i, b_i, lam)


def _lru_bwd(proj, conv_w, conv_b, w_r, b_r, w_i, b_i, lam, hs, dout, *, name):
    T = proj.shape[0]
    B, W = LRU_BLOCKS, LRU_BLOCK

    def csum(x):
        return jnp.sum(x, axis=0, keepdims=True)

    def body(xb_ref, yb_ref, cw_ref, cb_ref, wr_ref, br_ref, wi_ref, bi_ref, lam_ref, hs_ref, do_ref,
             dxb_ref, dyb_ref, dcw_ref, dcb_ref, dwr_ref, dbr_ref, dwi_ref, dbi_ref, dlam_ref, a_s, b_s, g_s):
        xv, xc, r, i, sp, a, s = _lru_gates(xb_ref, cw_ref, cb_ref, wr_ref, br_ref, wi_ref, bi_ref, lam_ref)
        h = hs_ref[...]
        yb = yb_ref[...]
        dout = do_ref[...].astype(F32)
        dyb_ref[...] = (dout * h * _dgelu(yb)).astype(dyb_ref.dtype)
        a_s[...] = _shift_up(a, 1)
        b_s[...] = dout * _gelu(yb)
        _linear_scan(a_s, b_s, g_s, True)
        g = g_s[...]
        da = g * _shift_down(h, 1)
        ds = g * (i * xc)
        di = g * s * xc
        dxc = g * s * i
        dla = da * a - ds * (a * a) / s
        dr = dla * (-LRU_C * sp)
        dlam_ref[...] = csum(dla * r) * (LRU_C * _sigmoid(-lam_ref[...]))
        dpr = dr * r * (1.0 - r)
        dpi = di * i * (1.0 - i)
        dxc = dxc + _bdot(dpr, wr_ref[0], NT) + _bdot(dpi, wi_ref[0], NT)
        dwr_ref[0] = _bdot(xc, dpr, TN)
        dwi_ref[0] = _bdot(xc, dpi, TN)
        dbr_ref[0] = csum(dpr)
        dbi_ref[0] = csum(dpi)
        dcb_ref[...] = csum(dxc)
        dxb_ref[...] = _conv_bwd(xv, cw_ref, dxc, dcw_ref).astype(dxb_ref.dtype)

    xb, yb, cw, vec, wg, bg = _lru_specs(T)
    act = jax.ShapeDtypeStruct((T, B * W), BF16)
    return pl.pallas_call(
        body, name=name, grid=(B,), in_specs=[xb, yb, cw, vec, wg, bg, wg, bg, vec, xb, xb],
        out_specs=(xb, xb, cw, vec, wg, bg, wg, bg, vec),
        out_shape=(act, act, jax.ShapeDtypeStruct((CONV_WIDTH, B * W), F32), jax.ShapeDtypeStruct((1, B * W), F32),
                   jax.ShapeDtypeStruct((B, W, W), F32), jax.ShapeDtypeStruct((B, 1, W), F32),
                   jax.ShapeDtypeStruct((B, W, W), F32), jax.ShapeDtypeStruct((B, 1, W), F32),
                   jax.ShapeDtypeStruct((1, B * W), F32)),
        scratch_shapes=[pltpu.VMEM((T, W), F32)] * 3,
        compiler_params=_params(("arbitrary",)),
    )(proj, proj, conv_w, conv_b, w_r, b_r, w_i, b_i, lam, hs, dout)


MESH = pl.DeviceIdType.MESH
N_CHIPS = 4
AG_COPIES = 7


def _mesh_pos():
    return lax.axis_index("x"), lax.axis_index("y"), lax.axis_index("c")


def _hbm_specs(n):
    return [pl.BlockSpec(memory_space=pltpu.HBM)] * n


def _all_gather(shards, *, name):
    n = len(shards)

    def body(*refs):
        xs, outs = refs[:n], refs[n:2 * n]
        send_sems, recv_sems, local_sems = refs[2 * n:]
        x, y, c = _mesh_pos()
        me, sibling = (x, y, c), (x, y, 1 - c)
        chips = [(1 - x, y), (x, 1 - y), (1 - x, 1 - y)]

        def rows(t, px, py, pc):
            return outs[t].at[4 * px + 2 * py + pc]

        def copy(t, k, block, to, src=None):
            return pltpu.make_async_remote_copy(
                src_ref=rows(t, *block) if src is None else src, dst_ref=rows(t, *block),
                send_sem=send_sems.at[t * AG_COPIES + k], recv_sem=recv_sems.at[t * AG_COPIES + k],
                device_id=to, device_id_type=MESH)

        mine = [pltpu.make_async_copy(xs[t], rows(t, *me), local_sems.at[t]) for t in range(n)]
        for cp in mine:
            cp.start()
        first = []
        for t in range(n):
            first.append(copy(t, 0, me, sibling, src=xs[t]))
            first += [copy(t, 1 + j, me, (*chip, c), src=xs[t]) for j, chip in enumerate(chips)]
        for cp in first:
            cp.start()
        passed = []
        for j, chip in enumerate(chips):
            for t in range(n):
                copy(t, 1 + j, (*chip, c), me).wait_recv()
                cp = copy(t, 4 + j, (*chip, c), sibling)
                cp.start()
                passed.append(cp)
        for t in range(n):
            copy(t, 0, sibling, me).wait_recv()
        for j, chip in enumerate(chips):
            for t in range(n):
                copy(t, 4 + j, (*chip, 1 - c), me).wait_recv()
        for cp in first + passed:
            cp.wait_send()
        for cp in mine:
            cp.wait()

    return pl.pallas_call(
        body, name=name,
        out_shape=[jax.ShapeDtypeStruct((N_DEV,) + s.shape, s.dtype) for s in shards],
        in_specs=_hbm_specs(n), out_specs=_hbm_specs(n),
        scratch_shapes=[pltpu.SemaphoreType.DMA((n * AG_COPIES,)), pltpu.SemaphoreType.DMA((n * AG_COPIES,)),
                        pltpu.SemaphoreType.DMA((n,))],
    )(*shards)


SIDE_EFFECT = pltpu.SideEffectType.DATAFLOW_SIDE_EFFECTING


def _copies(plan, refs, send_sems, recv_sems):
    return [pltpu.make_async_remote_copy(src_ref=src, dst_ref=dst, send_sem=send_sems.at[k], recv_sem=recv_sems.at[k],
                                         device_id=to, device_id_type=MESH)
            for k, (src, dst, to) in enumerate(plan(refs))]


def _split_start(bufs, plan, n_copies, *, name, deps=()):
    n = len(bufs)

    def body(*refs):
        send_sems, recv_sems = refs[n + len(deps)], refs[n + len(deps) + 1]
        token = refs[-1]
        for cp in _copies(plan, refs[:n], send_sems, recv_sems):
            cp.start()
        token[...] = jnp.zeros_like(token)

    hbm, sem = pl.BlockSpec(memory_space=pltpu.HBM), pl.BlockSpec(memory_space=pltpu.SEMAPHORE)
    out = pl.pallas_call(
        body, name=name,
        out_shape=(pltpu.SemaphoreType.DMA((n_copies,)), pltpu.SemaphoreType.DMA((n_copies,)),
                   *[pltpu.HBM(b.shape, b.dtype) for b in bufs], jax.ShapeDtypeStruct(TOKEN_SHAPE, F32)),
        in_specs=[hbm] * n + [pl.BlockSpec(memory_space=pl.ANY)] * len(deps),
        out_specs=(sem, sem, *[hbm] * n, pl.BlockSpec(memory_space=pltpu.VMEM)),
        input_output_aliases={i: 2 + i for i in range(n)},
        compiler_params=pltpu.CompilerParams(has_side_effects=SIDE_EFFECT),
    )(*[pltpu.with_memory_space_constraint(b, pltpu.HBM) for b in bufs], *deps)
    return out[0], out[1], list(out[2:2 + n]), out[-1]


def _split_wait(send_sems, recv_sems, bufs, plan, after, *, name):
    n = len(bufs)

    def body(*refs):
        for cp in _copies(plan, refs[:n], refs[n], refs[n + 1]):
            cp.wait_send()
            cp.wait_recv()

    hbm, sem = pl.BlockSpec(memory_space=pltpu.HBM), pl.BlockSpec(memory_space=pltpu.SEMAPHORE)
    out = pl.pallas_call(
        body, name=name, out_shape=tuple(pltpu.HBM(b.shape, b.dtype) for b in bufs),
        in_specs=[hbm] * n + [sem, sem, pl.BlockSpec(memory_space=pl.ANY)], out_specs=tuple([hbm] * n),
        input_output_aliases={i: i for i in range(n)},
        compiler_params=pltpu.CompilerParams(has_side_effects=SIDE_EFFECT),
    )(*bufs, send_sems, recv_sems, after)
    return list(out)


def _block(ref, in_cols, d):
    if not in_cols:
        return ref.at[d]
    c = ref.shape[1] // N_DEV
    return ref.at[:, pl.ds(pl.multiple_of(d * c, LANE), c)]


def _plan_gather_a(in_cols):
    n = len(in_cols)

    def plan(refs):
        x, y, c = _mesh_pos()
        me = 4 * x + 2 * y + c
        peers = [(x, y, 1 - c), (1 - x, y, c), (x, 1 - y, c), (1 - x, 1 - y, c)]
        return [(refs[t], _block(refs[n + t], in_cols[t], me), to) for t in range(n) for to in peers]
    return plan


def _plan_gather_b(in_cols):
    n = len(in_cols)

    def plan(refs):
        x, y, c = _mesh_pos()
        ds = [4 * px + 2 * py + c for px, py in [(1 - x, y), (x, 1 - y), (1 - x, 1 - y)]]
        return [(_block(refs[t], in_cols[t], d), _block(refs[t], in_cols[t], d), (x, y, 1 - c))
                for t in range(n) for d in ds]
    return plan


def _plan_scatter_pair(in_cols):
    n = len(in_cols)

    def plan(refs):
        x, y, c = _mesh_pos()
        return [(_block(refs[t], in_cols[t], 2 * q + (1 - c)), refs[n + t].at[q], (x, y, 1 - c))
                for t in range(n) for q in range(N_CHIPS)]
    return plan


def _plan_scatter_chips(n):
    def plan(refs):
        x, y, c = _mesh_pos()
        chips = [(1 - x, y), (x, 1 - y), (1 - x, 1 - y)]
        return [(refs[t].at[2 * px + py], refs[n + t].at[j], (px, py, c))
                for t in range(n) for j, (px, py) in enumerate(chips)]
    return plan


def _insert_block(land, shard, device, in_cols, *, name):
    r, c = shard.shape
    tr = _tile(r, (512, 256))

    def body(d_ref, s_ref, land_ref, o_ref):
        if in_cols:
            o_ref[...] = s_ref[...]
        else:
            o_ref[0] = s_ref[...]

    if in_cols:
        o_spec = pl.BlockSpec((tr, c), lambda i, d: (i, d[0]))
    else:
        o_spec = pl.BlockSpec((1, tr, c), lambda i, d: (d[0], i, 0))
    return pl.pallas_call(
        body, name=name,
        grid_spec=pltpu.PrefetchScalarGridSpec(
            num_scalar_prefetch=1, grid=(r // tr,),
            in_specs=[pl.BlockSpec((tr, c), lambda i, d: (i, 0)), pl.BlockSpec(memory_space=pl.ANY)],
            out_specs=o_spec),
        out_shape=jax.ShapeDtypeStruct(land.shape, land.dtype), input_output_aliases={2: 0},
        compiler_params=_params(("parallel",)),
    )(device, shard, land)


PAIR_ROWS = (512, 256)


def _pair_add(g, a, core, in_cols, *, name):
    _, R, C = a.shape
    tr = _tile(R, PAIR_ROWS)

    def body(c_ref, g_ref, a_ref, o_ref):
        gv = g_ref[...] if in_cols else g_ref[0]
        o_ref[0] = (gv.astype(F32) + a_ref[0].astype(F32)).astype(o_ref.dtype)

    blk = pl.BlockSpec((1, tr, C), lambda q, i, c: (q, i, 0))
    if in_cols:
        g_spec = pl.BlockSpec((tr, C), lambda q, i, c: (i, 2 * q + c[0]))
    else:
        g_spec = pl.BlockSpec((1, tr, C), lambda q, i, c: (2 * q + c[0], i, 0))
    return pl.pallas_call(
        body, name=name,
        grid_spec=pltpu.PrefetchScalarGridSpec(
            num_scalar_prefetch=1, grid=(N_CHIPS, R // tr), in_specs=[g_spec, blk], out_specs=blk),
        out_shape=jax.ShapeDtypeStruct((N_CHIPS, R, C), BF16),
        compiler_params=_params(("parallel", "parallel")),
    )(core, g, a)


ADAM_ROWS = 256


def _adamw_sharded(w, m, v, s4, b3, chip, layer, prev, *, name):
    L, R, C = w.shape
    tr = _tile(R, (ADAM_ROWS,))
    if prev is None and L > 1:
        prev = tuple(lax.empty(w.shape, F32) for _ in range(4))
    n_prev = 0 if prev is None else 4

    def body(q_ref, w_ref, m_ref, v_ref, s_ref, b_ref, *rest):
        g_out, d_out, m_out, v_out = rest[n_prev:]
        g = s_ref[0].astype(F32)
        for j in range(N_CHIPS - 1):
            g = g + b_ref[j].astype(F32)
        mn = ADAM_B1 * m_ref[0] + (1.0 - ADAM_B1) * g
        vn = ADAM_B2 * v_ref[0] + (1.0 - ADAM_B2) * (g * g)
        g_out[0] = g
        d_out[0] = -ADAM_LR * ((mn / ADAM_C1) / (jnp.sqrt(vn / ADAM_C2) + ADAM_EPS) + ADAM_WD * w_ref[0])
        m_out[0] = mn
        v_out[0] = vn

    blk = pl.BlockSpec((1, tr, C), lambda i, q: (layer, i, 0))
    sh = jax.ShapeDtypeStruct((L, R, C), F32)
    return pl.pallas_call(
        body, name=name,
        grid_spec=pltpu.PrefetchScalarGridSpec(
            num_scalar_prefetch=1, grid=(R // tr,),
            in_specs=[blk, blk, blk, pl.BlockSpec((1, tr, C), lambda i, q: (q[0], i, 0)),
                      pl.BlockSpec((N_CHIPS - 1, tr, C), lambda i, q: (0, i, 0))]
            + [pl.BlockSpec(memory_space=pl.ANY)] * n_prev,
            out_specs=(blk,) * 4),
        out_shape=(sh,) * 4, input_output_aliases={6 + k: k for k in range(n_prev)},
        compiler_params=_params(("parallel",)),
    )(chip, w, m, v, s4, b3, *(prev or ()))


FWD_NAMES = ['x', 'norm1', 'norm2', 'final_norm', 'ret_w_in', 'ret_gn_gain', 'ret_w_out', 'gdn_w_in', 'gdn_conv_w',
             'gdn_a_log', 'gdn_dt_bias', 'gdn_norm_gain', 'gdn_w_out', 'gla_w_in', 'gla_w_gate_up', 'gla_gate_bias',
             'gla_norm_gain', 'gla_w_out', 'lru_w_in', 'lru_conv_w', 'lru_conv_b', 'lru_w_rgate', 'lru_b_rgate',
             'lru_w_igate', 'lru_b_igate', 'lru_lambda', 'lru_w_out', 'mlp_w_up', 'mlp_w_down']
WEIGHT_NAMES = FWD_NAMES[1:]
ARG_NAMES = FWD_NAMES + ['loss_target'] + ['m_' + n for n in WEIGHT_NAMES] + ['v_' + n for n in WEIGHT_NAMES]

MIXER_IN = ('ret_w_in', 'gdn_w_in', 'gla_w_in', 'lru_w_in')
MIXER_OUT = ('ret_w_out', 'gdn_w_out', 'gla_w_out', 'lru_w_out')
BIG_NAMES = MIXER_IN + MIXER_OUT + ('mlp_w_up', 'mlp_w_down')
SMALL = {'norm1': False, 'norm2': False, 'final_norm': False, 'ret_gn_gain': True, 'gdn_conv_w': True,
         'gdn_a_log': False, 'gdn_dt_bias': False, 'gdn_norm_gain': False, 'gla_w_gate_up': True,
         'gla_gate_bias': True, 'gla_norm_gain': True, 'lru_conv_w': True, 'lru_conv_b': True,
         'lru_w_rgate': False, 'lru_b_rgate': False, 'lru_w_igate': False, 'lru_b_igate': False, 'lru_lambda': True}
SMALL_NAMES = tuple(n for n in WEIGHT_NAMES if n in SMALL)
MEDIUM = ('lru_w_rgate', 'lru_w_igate')
EARLY_SMALL = tuple(n for n in SMALL_NAMES if n.startswith(('gdn_', 'gla_', 'lru_')) and n not in MEDIUM)
LATE_SMALL = tuple(n for n in SMALL_NAMES if n not in EARLY_SMALL + MEDIUM)
GDN_TAIL = 2 * GDN_HEADS
GDN_MAIN = 4 * D_MODEL
GLA_MAIN = 3 * D_MODEL


PACK_ROWS = 256


PACK_TILE = SUBLANES * LANE


def _pack(arrs):
    rows = []
    for a in arrs:
        f = a.reshape(-1).astype(F32)
        rows.append(jnp.pad(f, (0, (-f.shape[0]) % PACK_TILE)).reshape(-1, LANE))
    fill = (-sum(r.shape[0] for r in rows)) % PACK_ROWS
    if fill:
        rows.append(jnp.zeros((fill, LANE), F32))
    return jnp.concatenate(rows, axis=0)


def _unpack(buf, shapes, lead=()):
    out, r0 = [], 0
    for s in shapes:
        n = int(np.prod(s))
        nr = -(-n // PACK_TILE) * SUBLANES
        blk = buf[..., r0:r0 + nr, :].reshape(lead + (nr * LANE,))[..., :n]
        out.append(blk.reshape(lead + tuple(s)))
        r0 += nr
    return out


def _full_cols(g):
    return jnp.transpose(g, (1, 0, 2)).reshape(g.shape[1], N_DEV * g.shape[2])


def _full_rows(g):
    return g.reshape(N_DEV * g.shape[1], g.shape[2])


def _blocks_cols(dw):
    r, c = dw.shape[0], dw.shape[1] // N_DEV
    return jnp.transpose(dw.reshape(r, N_DEV, c), (1, 0, 2))


def _blocks_rows(dw):
    return dw.reshape(N_DEV, dw.shape[0] // N_DEV, dw.shape[1])


def _pad_cols(a, n=LANE):
    return jnp.pad(a, ((0, 0), (0, n - a.shape[1])))


def _mixer_fwd(layer, hn, w_in, sm, tables, deps=()):
    tag = f"l{layer}"
    if layer == 0:
        proj = _matmul(hn, w_in, name=tag + "_in", deps=deps)
        gain = sm['ret_gn_gain'][0][:, None, :]
        og, st = _ret_fwd(proj, gain, tables, name=tag + "_ret_fwd")
        return og, dict(proj=proj, st=st, gain=gain)
    if layer == 1:
        w_main, w_tail = w_in, _pad_cols(w_in[:, GDN_MAIN:])
        proj = _matmul(hn, w_main, b_cols=GDN_MAIN, name=tag + "_in", deps=deps)
        gates = _matmul(hn, w_tail, name=tag + "_in_tail")
        conv_w = sm['gdn_conv_w'][0]
        act = _conv_silu_fwd(proj, conv_w, name=tag + "_conv")
        a_log = sm['gdn_a_log'].reshape(GDN_HEADS, 1, 1)
        dt_bias = sm['gdn_dt_bias'].reshape(GDN_HEADS, 1, 1)
        gain = sm['gdn_norm_gain']
        og, st, tm, sol = _gdn_fwd(act, proj, gates, a_log, dt_bias, gain, name=tag + "_gdn_fwd")
        return og, dict(proj=proj, gates=gates, act=act, st=st, solved=(tm, sol), conv_w=conv_w, a_log=a_log, dt_bias=dt_bias,
                        gain=gain, w_main=w_main, w_tail=w_tail)
    if layer == 2:
        w_main, w_tail = w_in, _pad_cols(w_in[:, GLA_MAIN:])
        proj = _matmul(hn, w_main, b_cols=GLA_MAIN, name=tag + "_in", deps=deps)
        glow = _matmul(hn, w_tail, name=tag + "_in_tail")
        wgu = jnp.pad(sm['gla_w_gate_up'][0], ((0, LANE - GLA_GATE_RANK), (0, 0)))
        glogit = _matmul(glow, wgu, name=tag + "_gate_up")
        bias = sm['gla_gate_bias']
        gain = sm['gla_norm_gain'][0][:, None, :]
        og, st = _gla_fwd(proj, glogit, bias, gain, name=tag + "_gla_fwd")
        return og, dict(proj=proj, glow=glow, glogit=glogit, wgu=wgu, bias=bias, gain=gain, st=st,
                        w_main=w_main, w_tail=w_tail)
    proj = _matmul(hn, w_in, name=tag + "_in", deps=deps)
    args = (proj, sm['lru_conv_w'][0], sm['lru_conv_b'], sm['lru_w_rgate'][0], sm['lru_b_rgate'][0][:, None, :],
            sm['lru_w_igate'][0], sm['lru_b_igate'][0][:, None, :], sm['lru_lambda'])
    og, hs = _lru_fwd(*args, name=tag + "_lru_fwd")
    return og, dict(args=args, hs=hs)


def _mixer_bwd(layer, hn, w_in, dog, sv, tables, on_dw):
    tag = f"l{layer}"
    if layer == 0:
        dq, dk, dv, dg, dgain = _ret_bwd(sv['proj'], sv['gain'], tables, sv['st'], dog, name=tag + "_ret_bwd")
        dproj = jnp.concatenate([dq, dk, dv, dg], axis=1)
        deps = on_dw(_matmul(hn, dproj, ta=True, out_dtype=BF16, name=tag + "_in_dw"))
        dhn = _matmul(dproj, w_in, tb=True, name=tag + "_in_dx", deps=deps)
        return dhn, {'ret_gn_gain': dgain[:, 0][None]}
    if layer == 1:
        dq, dk, dv, dz, dgates, dscal, dgain = _gdn_bwd(
            sv['act'], sv['proj'], sv['gates'], sv['a_log'], sv['dt_bias'], sv['gain'], sv['st'], sv['solved'], dog,
            name=tag + "_gdn_bwd")
        dact = jnp.concatenate([dq, dk, dv], axis=1)
        dqkv, dconv = _conv_silu_bwd(sv['proj'], sv['conv_w'], dact, name=tag + "_conv_bwd")
        dmain = jnp.concatenate([dqkv, dz], axis=1)
        T = dmain.shape[0]
        dtail = _pad_cols(jnp.transpose(dgates[:, :, :2], (1, 2, 0)).reshape(T, GDN_TAIL))
        dw_main = _matmul(hn, dmain, ta=True, out_dtype=BF16, name=tag + "_in_dw")
        dw_tail = _matmul(hn, dtail, ta=True, out_dtype=BF16, name=tag + "_in_tail_dw")
        deps = on_dw(jnp.concatenate([dw_main, dw_tail[:, :GDN_TAIL]], axis=1))
        dhn = _matmul(dmain, sv['w_main'], tb=True, b_cols=GDN_MAIN, name=tag + "_in_dx", deps=deps)
        dhn = _matmul(dtail, sv['w_tail'], tb=True, epi="add", extra=dhn, name=tag + "_in_tail_dx")
        small = {'gdn_conv_w': dconv[None], 'gdn_a_log': dscal[:, 0, 0][None], 'gdn_dt_bias': dscal[:, 0, 1][None],
                 'gdn_norm_gain': jnp.sum(dgain[:, 0], axis=0)[None]}
        return dhn, small
    if layer == 2:
        dq, dk, dv, dr, dgl, dgain = _gla_bwd(sv['proj'], sv['glogit'], sv['bias'], sv['gain'], sv['st'], dog,
                                              name=tag + "_gla_bwd")
        dmain = jnp.concatenate([dq, dk, dv, dr], axis=1)
        dglow = _matmul(dgl, sv['wgu'], tb=True, name=tag + "_gate_up_dx")
        dwgu = _matmul(sv['glow'], dgl, ta=True, name=tag + "_gate_up_dw")
        dbias = _colsum(dgl, name=tag + "_gate_bias")
        dw_main = _matmul(hn, dmain, ta=True, out_dtype=BF16, name=tag + "_in_dw")
        dw_tail = _matmul(hn, dglow, ta=True, out_dtype=BF16, name=tag + "_in_tail_dw")
        deps = on_dw(jnp.concatenate([dw_main, dw_tail[:, :GLA_GATE_RANK]], axis=1))
        dhn = _matmul(dmain, sv['w_main'], tb=True, b_cols=GLA_MAIN, name=tag + "_in_dx", deps=deps)
        dhn = _matmul(dglow, sv['w_tail'], tb=True, epi="add", extra=dhn, name=tag + "_in_tail_dx")
        small = {'gla_w_gate_up': dwgu[:GLA_GATE_RANK][None], 'gla_gate_bias': dbias,
                 'gla_norm_gain': dgain[:, 0][None]}
        return dhn, small
    dxb, dyb, dcw, dcb, dwr, dbr, dwi, dbi, dlam = _lru_bwd(*sv['args'], sv['hs'], dog, name=tag + "_lru_bwd")
    dproj = jnp.concatenate([dxb, dyb], axis=1)
    deps = on_dw(_matmul(hn, dproj, ta=True, out_dtype=BF16, name=tag + "_in_dw"))
    dhn = _matmul(dproj, w_in, tb=True, name=tag + "_in_dx", deps=deps)
    small = {'lru_conv_w': dcw[None], 'lru_conv_b': dcb, 'lru_w_rgate': dwr[None], 'lru_b_rgate': dbr[:, 0][None],
             'lru_w_igate': dwi[None], 'lru_b_igate': dbi[:, 0][None], 'lru_lambda': dlam}
    return dhn, small


def _step(*args):
    assert len(args) == len(ARG_NAMES)
    p = dict(zip(ARG_NAMES, args))
    xi, yi, ci = _mesh_pos()
    dev = 4 * xi + 2 * yi + ci
    device = dev.astype(jnp.int32).reshape(1)
    core = ci.astype(jnp.int32).reshape(1)
    chip = (2 * xi + yi).astype(jnp.int32).reshape(1)
    x = p['x'][0]
    target = p['loss_target'][0]
    T = x.shape[0]
    tables = _ret_tables(T)

    sharded_small = [n for n in SMALL_NAMES if SMALL[n]]
    gathered, = _all_gather([_pack([p[n] for n in sharded_small])], name="gather_small")
    def in_cols_of(w):
        return w.shape[1] % LANE == 0

    gathers = {}
    token = gathered
    for layer in range(DEPTH):
        w_in_shard = p[MIXER_IN[layer]][0]
        groups = {'a': [(w_in_shard, in_cols_of(w_in_shard), _full_cols)],
                  'b': [(p[MIXER_OUT[layer]][0], False, _full_rows)],
                  'c': [(p['mlp_w_up'][layer], True, None), (p['mlp_w_down'][layer], False, _full_rows)]}
        for key, members in groups.items():
            shards = [w.astype(BF16) for w, _, _ in members]
            in_cols = [ic for _, ic, _ in members]
            n = len(shards)
            lands = [_insert_block(lax.empty((s.shape[0], N_DEV * s.shape[1]) if ic else (N_DEV,) + s.shape, BF16),
                                   s, device, ic, name=f"own_l{layer}{key}{i}")
                     for i, (s, ic) in enumerate(zip(shards, in_cols))]
            send, recv, bufs, token = _split_start(shards + lands, _plan_gather_a(in_cols), 4 * n,
                                                   name=f"gather_a_start_l{layer}{key}", deps=(token,))
            gathers[layer, key] = dict(n=n, h=(send, recv, bufs), in_cols=in_cols, full_of=[f for _, _, f in members])

    def gather_forward(layer, key, after):
        g = gathers[layer, key]
        n = g['n']
        send, recv, bufs = g['h']
        bufs = _split_wait(send, recv, bufs, _plan_gather_a(g['in_cols']), after, name=f"gather_a_wait_l{layer}{key}")
        send, recv, lands, tok = _split_start(bufs[n:], _plan_gather_b(g['in_cols']), 3 * n,
                                              name=f"gather_b_start_l{layer}{key}")
        g['h'] = (send, recv, lands)
        return tok

    def gather_finish(layer, key, after):
        g = gathers[layer, key]
        send, recv, lands = g['h']
        lands = _split_wait(send, recv, lands, _plan_gather_b(g['in_cols']), after, name=f"gather_b_wait_l{layer}{key}")
        return [l if ic else full_of(l) for l, ic, full_of in zip(lands, g['in_cols'], g['full_of'])]

    w_in_next, = gather_finish(0, 'a', gather_forward(0, 'a', token))
    parts = _unpack(gathered, [p[n].shape for n in sharded_small], lead=(N_DEV,))
    sm = {n: p[n] for n in SMALL_NAMES if not SMALL[n]}
    for n, blk in zip(sharded_small, parts):
        full = jnp.moveaxis(blk, 0, -2)
        sm[n] = full.reshape(full.shape[:-2] + (N_DEV * full.shape[-1],))

    saved = []
    big = {}
    for layer in range(DEPTH):
        w_in = w_in_next
        tag = f"l{layer}"
        hn = _rmsnorm_fwd(x, sm['norm1'][layer][None], name=tag + "_norm1")
        og, sv = _mixer_fwd(layer, hn, w_in, sm, tables)
        tok_b = gather_forward(layer, 'b', og)
        tok_c = gather_forward(layer, 'c', tok_b)
        w_out, = gather_finish(layer, 'b', tok_c)
        x_mid = _matmul(og, w_out, epi="add", extra=x, name=tag + "_out")
        hn2 = _rmsnorm_fwd(x_mid, sm['norm2'][layer][None], name=tag + "_norm2")
        w_up, w_down = gather_finish(layer, 'c', hn2)
        big[layer] = (w_in, w_out, w_up, w_down)
        u, a = _matmul(hn2, w_up, epi="relu2", name=tag + "_up")
        deps = (gather_forward(layer + 1, 'a', u),) if layer + 1 < DEPTH else ()
        x_new = _matmul(a, w_down, epi="add", extra=x_mid, name=tag + "_down", deps=deps)
        if layer + 1 < DEPTH:
            w_in_next, = gather_finish(layer + 1, 'a', x_new)
        saved.append(dict(x=x, hn=hn, og=og, sv=sv, x_mid=x_mid, hn2=hn2, u=u, a=a))
        x = x_new
    dx, dxb, dfinal, loss_part = _final_loss_bwd(x, sm['final_norm'][None], target, name="final_loss_bwd")
    loss = lax.psum(loss_part[0, 0], ("x", "y", "c"))

    outs = {}
    small_grads = {'final_norm': dfinal[0]}
    dnorm1, dnorm2 = [None] * DEPTH, [None] * DEPTH

    def scatter_start(items, deps, tag):
        grads, in_cols = [], []
        for _, _, dw, on in items:
            ic = on == 'cols' and (dw.shape[1] // N_DEV) % LANE == 0
            in_cols.append(ic)
            grads.append(dw if ic else (_blocks_cols(dw) if on == 'cols' else _blocks_rows(dw)))
        n = len(grads)
        lands = [lax.empty((N_CHIPS, dw.shape[0], dw.shape[1] // N_DEV) if ic else (N_CHIPS,) + dw.shape[1:], BF16)
                 for dw, ic in zip(grads, in_cols)]
        send, recv, bufs, tok = _split_start(grads + lands, _plan_scatter_pair(in_cols), N_CHIPS * n,
                                             name=f"scatter_pair_start_{tag}", deps=deps)
        return dict(items=items, n=n, tag=tag, in_cols=in_cols, h=(send, recv, bufs)), tok

    def scatter_forward(g, after):
        n, tag = g['n'], g['tag']
        send, recv, bufs = g['h']
        bufs = _split_wait(send, recv, bufs, _plan_scatter_pair(g['in_cols']), after, name=f"scatter_pair_wait_{tag}")
        sums = [_pair_add(b, a_, core, ic, name=f"pair_add_{tag}_{i}")
                for i, (b, a_, ic) in enumerate(zip(bufs[:n], bufs[n:], g['in_cols']))]
        lands = [lax.empty((N_CHIPS - 1,) + s_.shape[1:], BF16) for s_ in sums]
        send, recv, bufs, tok = _split_start(sums + lands, _plan_scatter_chips(n), (N_CHIPS - 1) * n,
                                             name=f"scatter_chips_start_{tag}")
        g['h'] = (send, recv, bufs)
        return tok

    def scatter_finish(g, after):
        n, tag = g['n'], g['tag']
        send, recv, bufs = g['h']
        bufs = _split_wait(send, recv, bufs, _plan_scatter_chips(n), after, name=f"scatter_chips_wait_{tag}")
        for i, (wname, idx, _, _) in enumerate(g['items']):
            outs[wname] = _adamw_sharded(p[wname], p['m_' + wname], p['v_' + wname], bufs[i], bufs[n + i], chip,
                                         idx, outs.get(wname), name=f"adamw_{tag}_{i}")

    older = []
    for layer in reversed(range(DEPTH)):
        w_in, w_out, w_up, w_down = big[layer]
        s = saved[layer]
        tag = f"l{layer}"
        du = _matmul(dxb, w_down, tb=True, epi="drelu2", extra=s['u'], out_dtype=BF16, name=tag + "_down_dx")
        dw_down = _matmul(s['a'], dxb, ta=True, out_dtype=BF16, name=tag + "_down_dw")
        dhn2 = _matmul(du, w_up, tb=True, name=tag + "_up_dx")
        dw_up = _matmul(s['hn2'], du, ta=True, out_dtype=BF16, name=tag + "_up_dw")
        mlp_group, tok = scatter_start([('mlp_w_up', layer, dw_up, 'cols'),
                                        ('mlp_w_down', layer, dw_down, 'rows')], (), f"mlp_l{layer}")
        dx, dxb, dn2 = _rmsnorm_bwd(s['x_mid'], sm['norm2'][layer][None], dhn2, dx, name=tag + "_norm2_bwd",
                                    deps=(tok,))
        dog = _matmul(dxb, w_out, tb=True, name=tag + "_out_dx")
        tok = scatter_forward(mlp_group, dog)
        dw_out = _matmul(s['og'], dxb, ta=True, out_dtype=BF16, name=tag + "_out_dw", deps=(tok,))
        started = []

        def on_dw(dw_in):
            group, tok_ = scatter_start([(MIXER_IN[layer], 0, dw_in, 'cols'),
                                         (MIXER_OUT[layer], 0, dw_out, 'rows')], (), f"mix_l{layer}")
            started.append(group)
            return (tok_,)

        dhn, sg = _mixer_bwd(layer, s['hn'], w_in, dog, s['sv'], tables, on_dw)
        mixer_group, = started
        small_grads.update(sg)
        if layer > 0:
            tok = scatter_forward(mixer_group, dhn)
            dx, dxb, dn1 = _rmsnorm_bwd(s['x'], sm['norm1'][layer][None], dhn, dx, name=tag + "_norm1_bwd", deps=(tok,))
        else:
            dx, dxb, dn1 = _rmsnorm_bwd(s['x'], sm['norm1'][layer][None], dhn, dx, name=tag + "_norm1_bwd")
        dnorm1[layer], dnorm2[layer] = dn1[0], dn2[0]
        for g in older:
            scatter_finish(g, dx)
        older = [mlp_group, mixer_group]
        if layer == 1:
            parts = [_pack([small_grads[n] for n in EARLY_SMALL])] + [small_grads[n].reshape(-1, LANE) for n in MEDIUM]
            lands = [_insert_block(lax.empty((N_DEV,) + s_.shape, F32), s_, device, False, name=f"own_small_grads{i}")
                     for i, s_ in enumerate(parts)]
            n_parts = len(parts)
            send, recv, bufs, _ = _split_start(parts + lands, _plan_gather_a([False] * n_parts), 4 * n_parts,
                                               name="gather_a_start_small_grads", deps=(dx,))
            gathers['small_grads', 'early'] = dict(n=n_parts, h=(send, recv, bufs), in_cols=[False] * n_parts,
                                                   full_of=[lambda l: l] * n_parts)
    small_grads['norm1'] = jnp.stack(dnorm1)
    small_grads['norm2'] = jnp.stack(dnorm2)

    late_parts, = _all_gather([_pack([small_grads[n] for n in LATE_SMALL])], name="gather_small_grads")
    early_token = gather_forward('small_grads', 'early', late_parts)
    last_token = scatter_forward(mixer_group, early_token)
    early_parts, *medium_parts = gather_finish('small_grads', 'early', last_token)
    early_sum = _sum_parts(early_parts, name="sum_small_grads_early", deps=(last_token,))
    late_sum = _sum_parts(late_parts, name="sum_small_grads_late")
    for g in older:
        scatter_finish(g, late_sum)
    for n, part in zip(MEDIUM, medium_parts):
        g = _sum_parts(part, name=f"sum_{n}")
        res = _adamw(p[n].reshape(-1, LANE), p['m_' + n].reshape(-1, LANE), p['v_' + n].reshape(-1, LANE), [g],
                     name=f"adamw_{n}")
        outs[n] = tuple(r.reshape(p[n].shape) for r in res)
    by_name = dict(zip(EARLY_SMALL, _unpack(early_sum, [small_grads[n].shape for n in EARLY_SMALL])))
    by_name.update(zip(LATE_SMALL, _unpack(late_sum, [small_grads[n].shape for n in LATE_SMALL])))
    packed_names = EARLY_SMALL + LATE_SMALL
    local_g = []
    for n in packed_names:
        g = by_name[n]
        if SMALL[n]:
            width = p[n].shape[-1]
            g = lax.dynamic_slice_in_dim(g, dev * width, width, axis=g.ndim - 1)
        local_g.append(g.reshape(p[n].shape))
    res = _adamw(_pack([p[n] for n in packed_names]), _pack([p['m_' + n] for n in packed_names]),
                 _pack([p['v_' + n] for n in packed_names]), [_pack(local_g)], name="adamw_small")
    local_shapes = [p[n].shape for n in packed_names]
    unpacked = [_unpack(r, local_shapes) for r in res]
    for i, n in enumerate(packed_names):
        outs[n] = tuple(unpacked[k][i] for k in range(4))

    result = [loss, dx[None]]
    for k in range(4):
        result += [outs[n][k] for n in WEIGHT_NAMES]
    return tuple(result)


def kernel(x, norm1, norm2, final_norm, ret_w_in, ret_gn_gain, ret_w_out, gdn_w_in, gdn_conv_w, gdn_a_log, gdn_dt_bias, gdn_norm_gain, gdn_w_out, gla_w_in, gla_w_gate_up, gla_gate_bias, gla_norm_gain, gla_w_out, lru_w_in, lru_conv_w, lru_conv_b, lru_w_rgate, lru_b_rgate, lru_w_igate, lru_b_igate, lru_lambda, lru_w_out, mlp_w_up, mlp_w_down, loss_target, m_norm1, m_norm2, m_final_norm, m_ret_w_in, m_ret_gn_gain, m_ret_w_out, m_gdn_w_in, m_gdn_conv_w, m_gdn_a_log, m_gdn_dt_bias, m_gdn_norm_gain, m_gdn_w_out, m_gla_w_in, m_gla_w_gate_up, m_gla_gate_bias, m_gla_norm_gain, m_gla_w_out, m_lru_w_in, m_lru_conv_w, m_lru_conv_b, m_lru_w_rgate, m_lru_b_rgate, m_lru_w_igate, m_lru_b_igate, m_lru_lambda, m_lru_w_out, m_mlp_w_up, m_mlp_w_down, v_norm1, v_norm2, v_final_norm, v_ret_w_in, v_ret_gn_gain, v_ret_w_out, v_gdn_w_in, v_gdn_conv_w, v_gdn_a_log, v_gdn_dt_bias, v_gdn_norm_gain, v_gdn_w_out, v_gla_w_in, v_gla_w_gate_up, v_gla_gate_bias, v_gla_norm_gain, v_gla_w_out, v_lru_w_in, v_lru_conv_w, v_lru_conv_b, v_lru_w_rgate, v_lru_b_rgate, v_lru_w_igate, v_lru_b_igate, v_lru_lambda, v_lru_w_out, v_mlp_w_up, v_mlp_w_down):
    return _step(x, norm1, norm2, final_norm, ret_w_in, ret_gn_gain, ret_w_out, gdn_w_in, gdn_conv_w, gdn_a_log, gdn_dt_bias, gdn_norm_gain, gdn_w_out, gla_w_in, gla_w_gate_up, gla_gate_bias, gla_norm_gain, gla_w_out, lru_w_in, lru_conv_w, lru_conv_b, lru_w_rgate, lru_b_rgate, lru_w_igate, lru_b_igate, lru_lambda, lru_w_out, mlp_w_up, mlp_w_down, loss_target, m_norm1, m_norm2, m_final_norm, m_ret_w_in, m_ret_gn_gain, m_ret_w_out, m_gdn_w_in, m_gdn_conv_w, m_gdn_a_log, m_gdn_dt_bias, m_gdn_norm_gain, m_gdn_w_out, m_gla_w_in, m_gla_w_gate_up, m_gla_gate_bias, m_gla_norm_gain, m_gla_w_out, m_lru_w_in, m_lru_conv_w, m_lru_conv_b, m_lru_w_rgate, m_lru_b_rgate, m_lru_w_igate, m_lru_b_igate, m_lru_lambda, m_lru_w_out, m_mlp_w_up, m_mlp_w_down, v_norm1, v_norm2, v_final_norm, v_ret_w_in, v_ret_gn_gain, v_ret_w_out, v_gdn_w_in, v_gdn_conv_w, v_gdn_a_log, v_gdn_dt_bias, v_gdn_norm_gain, v_gdn_w_out, v_gla_w_in, v_gla_w_gate_up, v_gla_gate_bias, v_gla_norm_gain, v_gla_w_out, v_lru_w_in, v_lru_conv_w, v_lru_conv_b, v_lru_w_rgate, v_lru_b_rgate, v_lru_w_igate, v_lru_b_igate, v_lru_lambda, v_lru_w_out, v_mlp_w_up, v_mlp_w_down)
```

```python
import functools
import math

import numpy as np
import jax
import jax.numpy as jnp
from jax import lax
from jax.experimental import pallas as pl
from jax.experimental.pallas import tpu as pltpu

F32 = jnp.float32
BF16 = jnp.bfloat16

D_MODEL = 2048
DEPTH = 4
CHUNK = 64
D_FF = 4 * D_MODEL
NORM_EPS = 1e-6
N_DEV = 8

RET_HEADS, RET_DK, RET_DV = 8, 256, 512
RET_HB = 4
GDN_HEADS, GDN_DK, GDN_DV = 16, 128, 128
GDN_HB = 8
GDN_QKV = GDN_HEADS * (2 * GDN_DK + GDN_DV)
CONV_WIDTH = 4
GLA_HEADS, GLA_DK, GLA_DV = 4, 256, 512
GLA_HB = 4
GLA_GATE_RANK = 16
GLA_TAU = 16.0
LRU_WIDTH, LRU_BLOCKS, LRU_BLOCK = 2048, 16, 128
LRU_C = 8.0
ROPE_BASE = 10000.0

ADAM_LR, ADAM_B1, ADAM_B2, ADAM_EPS, ADAM_WD, ADAM_STEP = 0.001, 0.9, 0.999, 1e-08, 0.01, 10

LANE = 128
VMEM_LIMIT = 48 * 1024 * 1024

NN = (((1,), (0,)), ((), ()))
NT = (((1,), (1,)), ((), ()))
TN = (((0,), (0,)), ((), ()))


def _params(sem=None):
    return pltpu.CompilerParams(dimension_semantics=sem, vmem_limit_bytes=VMEM_LIMIT)


def _bdot(a, b, dn=NN):
    return lax.dot_general(a.astype(BF16), b.astype(BF16), dn, preferred_element_type=F32)


def _split(x):
    hi = x.astype(BF16)
    lo = (x - hi.astype(F32)).astype(BF16)
    return hi, lo


def _fdot(a, b, dn=NN):
    a1, a2 = _split(a)
    b1, b2 = _split(b)
    d = functools.partial(lax.dot_general, dimension_numbers=dn, preferred_element_type=F32)
    return d(a1, b1) + (d(a1, b2) + d(a2, b1))


def _sigmoid(x):
    return 1.0 / (1.0 + jnp.exp(-x))


def _softplus(x):
    return jnp.maximum(x, 0.0) + jnp.log(1.0 + jnp.exp(-jnp.abs(x)))


def _silu(x):
    return x * _sigmoid(x)


def _dsilu(x):
    s = _sigmoid(x)
    return s * (1.0 + x * (1.0 - s))


GELU_C = math.sqrt(2.0 / math.pi)


def _gelu(x):
    return 0.5 * x * (1.0 + jnp.tanh(GELU_C * (x + 0.044715 * x * x * x)))


def _dgelu(x):
    t = jnp.tanh(GELU_C * (x + 0.044715 * x * x * x))
    return 0.5 * (1.0 + t) + 0.5 * x * (1.0 - t * t) * GELU_C * (1.0 + 3.0 * 0.044715 * x * x)


def _expm1(x):
    poly = x * (1.0 + x * 0.5 * (1.0 + x * (1.0 / 3.0) * (1.0 + x * 0.25 * (1.0 + x * 0.2))))
    return jnp.where(jnp.abs(x) < 0.05, poly, jnp.exp(x) - 1.0)


def _iota2(shape, axis):
    return lax.broadcasted_iota(jnp.int32, shape, axis)


def _col_to_row(col):
    n = col.shape[0]
    eye = _iota2((n, n), 0) == _iota2((n, n), 1)
    return jnp.sum(jnp.where(eye, col, 0.0), axis=0, keepdims=True)


def _row_to_col(row):
    n = row.shape[1]
    eye = _iota2((n, n), 0) == _iota2((n, n), 1)
    return jnp.sum(jnp.where(eye, row, 0.0), axis=1, keepdims=True)


def _pick_row(x, r):
    rows = _iota2(x.shape, 0)
    return jnp.sum(jnp.where(rows == r, x, 0.0), axis=0, keepdims=True)


def _shift_down(x, s):
    if s == 0:
        return x
    y = pltpu.roll(x, s, 0)
    return jnp.where(_iota2(x.shape, 0) < s, 0.0, y)


def _shift_up(x, s):
    if s == 0:
        return x
    n = x.shape[0]
    y = pltpu.roll(x, n - s, 0)
    return jnp.where(_iota2(x.shape, 0) >= n - s, 0.0, y)


def _tile(dim, prefs):
    for p in prefs:
        if dim % p == 0:
            return p
    return dim


TOKEN_SHAPE = (8, LANE)


def _dep_specs(deps):
    return [pl.BlockSpec(TOKEN_SHAPE, lambda *_: (0, 0)) for _ in deps]


def _matmul(a, b, *, ta=False, tb=False, epi="none", extra=None, out_dtype=F32, name, deps=(), b_cols=None):
    if ta:
        K, M = a.shape
    else:
        M, K = a.shape
    if tb:
        N, K2 = b.shape[0], b_cols or b.shape[1]
    else:
        K2, N = b.shape[0], b_cols or b.shape[1]
    assert K == K2, (a.shape, b.shape, ta, tb)
    if K <= 2048:
        tk = K
        if N <= 2048:
            tm, tn = _tile(M, (512, 256, 128)), N
        else:
            tm, tn = _tile(M, (1024, 512, 256, 128)), _tile(N, (512, 256, 128))
    else:
        tm, tn, tk = (_tile(d, (1024, 512, 256, 128)) for d in (M, N, K))
    nk = K // tk
    dn = (((0 if ta else 1,), (1 if tb else 0,)), ((), ()))
    n_extra = 0 if extra is None else 1
    n_out = 2 if epi == "relu2" else 1

    def body(*refs):
        a_ref, b_ref = refs[0], refs[1]
        e_ref = refs[2] if n_extra else None
        outs = refs[2 + n_extra + len(deps):2 + n_extra + len(deps) + n_out]

        def finish(r):
            if epi == "none":
                outs[0][...] = r.astype(outs[0].dtype)
            elif epi == "add":
                outs[0][...] = (r + e_ref[...]).astype(outs[0].dtype)
            elif epi == "relu2":
                outs[0][...] = r
                p = jnp.maximum(r, 0.0)
                outs[1][...] = (p * p).astype(outs[1].dtype)
            elif epi == "drelu2":
                outs[0][...] = (r * 2.0 * jnp.maximum(e_ref[...], 0.0)).astype(outs[0].dtype)

        def product():
            return lax.dot_general(a_ref[...].astype(BF16), b_ref[...].astype(BF16), dn, preferred_element_type=F32)

        if nk == 1:
            finish(product())
            return
        acc = refs[-1]
        k = pl.program_id(2)

        @pl.when(k == 0)
        def _():
            acc[...] = jnp.zeros_like(acc)

        acc[...] += product()

        @pl.when(k == nk - 1)
        def _():
            finish(acc[...])

    a_spec = pl.BlockSpec((tk, tm), lambda i, j, k: (k, i)) if ta else pl.BlockSpec((tm, tk), lambda i, j, k: (i, k))
    b_spec = pl.BlockSpec((tn, tk), lambda i, j, k: (j, k)) if tb else pl.BlockSpec((tk, tn), lambda i, j, k: (k, j))
    o_spec = pl.BlockSpec((tm, tn), lambda i, j, k: (i, j))
    in_specs = [a_spec, b_spec] + ([o_spec] if n_extra else []) + _dep_specs(deps)
    if epi == "relu2":
        out_shape = (jax.ShapeDtypeStruct((M, N), F32), jax.ShapeDtypeStruct((M, N), BF16))
        out_specs = (o_spec, o_spec)
    else:
        out_shape = jax.ShapeDtypeStruct((M, N), out_dtype)
        out_specs = o_spec
    args = (a, b) + ((extra,) if n_extra else ()) + tuple(deps)
    return pl.pallas_call(
        body, name=name, grid=(M // tm, N // tn, nk), in_specs=in_specs, out_specs=out_specs,
        out_shape=out_shape, scratch_shapes=[pltpu.VMEM((tm, tn), F32)] if nk > 1 else [],
        compiler_params=_params(("parallel", "parallel", "arbitrary")),
    )(*args)


ROW_BLOCK = 256


def _rmsnorm_fwd(x, g, *, name, deps=()):
    T, D = x.shape
    tr = _tile(T, (ROW_BLOCK, 128, 64))

    def body(x_ref, g_ref, *rest):
        o_ref = rest[-1]
        xv = x_ref[...]
        r = lax.rsqrt(jnp.mean(xv * xv, axis=-1, keepdims=True) + NORM_EPS)
        o_ref[...] = (xv * r * g_ref[...]).astype(o_ref.dtype)

    return pl.pallas_call(
        body, name=name, grid=(T // tr,),
        in_specs=[pl.BlockSpec((tr, D), lambda i: (i, 0)), pl.BlockSpec((1, D), lambda i: (0, 0))] + _dep_specs(deps),
        out_specs=pl.BlockSpec((tr, D), lambda i: (i, 0)),
        out_shape=jax.ShapeDtypeStruct((T, D), BF16), compiler_params=_params(("parallel",)),
    )(x, g, *deps)


def _rmsnorm_bwd(x, g, dy, dres, *, name, deps=()):
    T, D = x.shape
    tr = _tile(T, (ROW_BLOCK, 128, 64))

    def body(x_ref, g_ref, dy_ref, dres_ref, *rest):
        dx_ref, dxb_ref, dg_ref = rest[len(deps):]
        i = pl.program_id(0)
        xv = x_ref[...]
        r = lax.rsqrt(jnp.mean(xv * xv, axis=-1, keepdims=True) + NORM_EPS)
        xh = xv * r
        dyv = dy_ref[...].astype(F32)
        dxh = dyv * g_ref[...]
        dx = dres_ref[...] + r * (dxh - xh * jnp.mean(dxh * xh, axis=-1, keepdims=True))
        dx_ref[...] = dx
        dxb_ref[...] = dx.astype(dxb_ref.dtype)

        @pl.when(i == 0)
        def _():
            dg_ref[...] = jnp.zeros_like(dg_ref)

        dg_ref[...] += jnp.sum(dyv * xh, axis=0, keepdims=True)

    blk = pl.BlockSpec((tr, D), lambda i: (i, 0))
    vec = pl.BlockSpec((1, D), lambda i: (0, 0))
    return pl.pallas_call(
        body, name=name, grid=(T // tr,), in_specs=[blk, vec, blk, blk] + _dep_specs(deps), out_specs=(blk, blk, vec),
        out_shape=(jax.ShapeDtypeStruct((T, D), F32), jax.ShapeDtypeStruct((T, D), BF16),
                   jax.ShapeDtypeStruct((1, D), F32)),
        compiler_params=_params(("arbitrary",)),
    )(x, g, dy, dres, *deps)


def _final_loss_bwd(x, g, target, *, name):
    T, D = x.shape
    tr = _tile(T, (ROW_BLOCK, 128, 64))

    def body(x_ref, g_ref, t_ref, dx_ref, dxb_ref, dg_ref, l_ref):
        i = pl.program_id(0)
        xv = x_ref[...]
        r = lax.rsqrt(jnp.mean(xv * xv, axis=-1, keepdims=True) + NORM_EPS)
        xh = xv * r
        gv = g_ref[...]
        err = xh * gv - t_ref[...]
        dy = err * (1.0 / D)
        dxh = dy * gv
        dx = r * (dxh - xh * jnp.mean(dxh * xh, axis=-1, keepdims=True))
        dx_ref[...] = dx
        dxb_ref[...] = dx.astype(dxb_ref.dtype)

        @pl.when(i == 0)
        def _():
            l_ref[...] = jnp.zeros_like(l_ref)
            dg_ref[...] = jnp.zeros_like(dg_ref)

        dg_ref[...] += jnp.sum(dy * xh, axis=0, keepdims=True)
        part = 0.5 * jnp.sum(jnp.mean(err * err, axis=-1, keepdims=True), axis=0, keepdims=True)
        l_ref[...] += jnp.broadcast_to(part, l_ref.shape)

    blk = pl.BlockSpec((tr, D), lambda i: (i, 0))
    vec = pl.BlockSpec((1, D), lambda i: (0, 0))
    return pl.pallas_call(
        body, name=name, grid=(T // tr,), in_specs=[blk, vec, blk],
        out_specs=(blk, blk, vec, pl.BlockSpec((1, LANE), lambda i: (0, 0))),
        out_shape=(jax.ShapeDtypeStruct((T, D), F32), jax.ShapeDtypeStruct((T, D), BF16),
                   jax.ShapeDtypeStruct((1, D), F32), jax.ShapeDtypeStruct((1, LANE), F32)),
        compiler_params=_params(("arbitrary",)),
    )(x, g, target)


def _colsum(x, *, name):
    T, C = x.shape
    tc = _tile(C, (512, 256, 128))

    def body(x_ref, o_ref):
        o_ref[...] = jnp.sum(x_ref[...], axis=0, keepdims=True)

    return pl.pallas_call(
        body, name=name, grid=(C // tc,), in_specs=[pl.BlockSpec((T, tc), lambda j: (0, j))],
        out_specs=pl.BlockSpec((1, tc), lambda j: (0, j)),
        out_shape=jax.ShapeDtypeStruct((1, C), F32), compiler_params=_params(("parallel",)),
    )(x)


ADAM_C1 = 1.0 - ADAM_B1 ** ADAM_STEP
ADAM_C2 = 1.0 - ADAM_B2 ** ADAM_STEP


def _adamw(w, m, v, grads, *, name):
    R, C = w.shape
    tr = _tile(R, (256, 128, 64, 32, 16, 8))
    n_g = len(grads)

    def body(*refs):
        w_ref, m_ref, v_ref = refs[:3]
        g_refs = refs[3:3 + n_g]
        g_out, d_out, m_out, v_out = refs[3 + n_g:]
        g = g_refs[0][...].astype(F32)
        for r in g_refs[1:]:
            g = g + r[...].astype(F32)
        mn = ADAM_B1 * m_ref[...] + (1.0 - ADAM_B1) * g
        vn = ADAM_B2 * v_ref[...] + (1.0 - ADAM_B2) * (g * g)
        m_hat = mn / ADAM_C1
        v_hat = vn / ADAM_C2
        g_out[...] = g
        d_out[...] = -ADAM_LR * (m_hat / (jnp.sqrt(v_hat) + ADAM_EPS) + ADAM_WD * w_ref[...])
        m_out[...] = mn
        v_out[...] = vn

    blk = pl.BlockSpec((tr, C), lambda i: (i, 0))
    sh = jax.ShapeDtypeStruct((R, C), F32)
    return pl.pallas_call(
        body, name=name, grid=(R // tr,), in_specs=[blk] * (3 + n_g), out_specs=(blk,) * 4,
        out_shape=(sh,) * 4, compiler_params=_params(("parallel",)),
    )(w, m, v, *grads)


def _sum_parts(parts, *, name, deps=()):
    P, R, C = parts.shape
    tr = _tile(R, (256, 128, 64, 32, 16, 8))

    def body(p_ref, *rest):
        o_ref = rest[-1]
        s = p_ref[0].astype(F32)
        for i in range(1, P):
            s = s + p_ref[i].astype(F32)
        o_ref[...] = s

    return pl.pallas_call(
        body, name=name, grid=(R // tr,),
        in_specs=[pl.BlockSpec((P, tr, C), lambda i: (0, i, 0))] + _dep_specs(deps),
        out_specs=pl.BlockSpec((tr, C), lambda i: (i, 0)),
        out_shape=jax.ShapeDtypeStruct((R, C), F32), compiler_params=_params(("parallel",)),
    )(parts, *deps)


def _ret_tables(T):
    H, C = RET_HEADS, CHUNK
    log_gamma = jnp.log1p(-jnp.exp2(-5.0 - jnp.arange(H, dtype=F32)))
    pos = jnp.arange(C, dtype=F32)
    dist = jnp.abs(pos[:, None] - pos[None, :])
    dm = jnp.exp(log_gamma[:, None, None] * dist)
    qdec = jnp.exp(log_gamma[:, None] * (pos + 1.0))[:, :, None]
    kdec = jnp.exp(log_gamma[:, None] * (C - 1.0 - pos))[:, :, None]
    cdec = jnp.exp(log_gamma * C)[:, None, None]
    inv = ROPE_BASE ** (-jnp.arange(0, RET_DK, 2, dtype=F32) / RET_DK)
    ang = jnp.arange(T, dtype=F32)[:, None] * inv[None, :]
    return dm, qdec, kdec, cdec, jnp.cos(ang), jnp.sin(ang)


def _rot(x, cos, sin):
    h = x.shape[1] // 2
    x1, x2 = x[:, :h], x[:, h:]
    return jnp.concatenate([x1 * cos - x2 * sin, x1 * sin + x2 * cos], axis=1)


def _unrot(dy, cos, sin):
    h = dy.shape[1] // 2
    d1, d2 = dy[:, :h], dy[:, h:]
    return jnp.concatenate([d1 * cos + d2 * sin, d2 * cos - d1 * sin], axis=1)


def _ret_specs(N, rev):
    H, C, DK, DV = RET_HEADS, CHUNK, RET_DK, RET_DV
    cn = (lambda n: N - 1 - n) if rev else (lambda n: n)
    HB, G = RET_HB, H // RET_HB
    q = pl.BlockSpec((C, HB * DK), lambda h, n: (cn(n), h))
    k = pl.BlockSpec((C, HB * DK), lambda h, n: (cn(n), G + h))
    v = pl.BlockSpec((C, HB * DV), lambda h, n: (cn(n), G + h))
    g = pl.BlockSpec((C, HB * DV), lambda h, n: (cn(n), 2 * G + h))
    cs = pl.BlockSpec((C, DK // 2), lambda h, n: (cn(n), 0))
    dm = pl.BlockSpec((HB, C, C), lambda h, n: (h, 0, 0))
    dec = pl.BlockSpec((HB, C, 1), lambda h, n: (h, 0, 0))
    cd = pl.BlockSpec((HB, 1, 1), lambda h, n: (h, 0, 0))
    gain = pl.BlockSpec((HB, 1, DV), lambda h, n: (h, 0, 0))
    st = pl.BlockSpec((HB, 1, DK, DV), lambda h, n: (h, cn(n), 0, 0))
    ov = pl.BlockSpec((C, HB * DV), lambda h, n: (cn(n), h))
    return q, k, v, g, cs, dm, dec, cd, gain, st, ov


def _ret_fwd(proj, gain, tables, *, name):
    T = proj.shape[0]
    H, C, DK, DV = RET_HEADS, CHUNK, RET_DK, RET_DV
    N = T // C
    dm_t, qdec_t, kdec_t, cdec_t, cos_t, sin_t = tables

    def body(q_ref, k_ref, v_ref, g_ref, cos_ref, sin_ref, dm_ref, qd_ref, kd_ref, cd_ref, gain_ref,
             og_ref, st_ref, s_acc):
        n = pl.program_id(1)

        @pl.when(n == 0)
        def _():
            s_acc[...] = jnp.zeros_like(s_acc)

        cos, sin = cos_ref[...], sin_ref[...]

        def head(hb):
            kc, vc = pl.ds(hb * DK, DK), pl.ds(hb * DV, DV)
            qr = _rot(q_ref[:, kc], cos, sin)
            kr = _rot(k_ref[:, kc], cos, sin) * (DK ** -0.5)
            v = v_ref[:, vc]
            sp = s_acc[hb]
            st_ref[hb, 0] = sp.astype(st_ref.dtype)
            scores = _bdot(qr, kr, NT) * dm_ref[hb]
            inter = _bdot(qr * qd_ref[hb], sp)
            s_acc[hb] = sp * cd_ref[hb] + _bdot(kr * kd_ref[hb], v, TN)
            yield
            o = _bdot(scores, v) + inter
            yield
            oc = o - jnp.mean(o, axis=-1, keepdims=True)
            rstd = lax.rsqrt(jnp.mean(oc * oc, axis=-1, keepdims=True) + NORM_EPS)
            og_ref[:, vc] = (oc * rstd * gain_ref[hb] * _silu(g_ref[:, vc])).astype(og_ref.dtype)

        _interleave([head(hb) for hb in range(RET_HB)])

    q, k, v, g, cs, dm, dec, cd, gn, st, ov = _ret_specs(N, False)
    return pl.pallas_call(
        body, name=name, grid=(H // RET_HB, N),
        in_specs=[q, k, v, g, cs, cs, dm, dec, dec, cd, gn], out_specs=(ov, st),
        out_shape=(jax.ShapeDtypeStruct((T, H * DV), BF16), jax.ShapeDtypeStruct((H, N, DK, DV), BF16)),
        scratch_shapes=[pltpu.VMEM((RET_HB, DK, DV), F32)],
        compiler_params=_params(("arbitrary", "arbitrary")),
    )(proj, proj, proj, proj, cos_t, sin_t, dm_t, qdec_t, kdec_t, cdec_t, gain)


def _ret_bwd(proj, gain, tables, states, dog, *, name):
    T = proj.shape[0]
    H, C, DK, DV = RET_HEADS, CHUNK, RET_DK, RET_DV
    N = T // C
    dm_t, qdec_t, kdec_t, cdec_t, cos_t, sin_t = tables

    def body(q_ref, k_ref, v_ref, g_ref, cos_ref, sin_ref, dm_ref, qd_ref, kd_ref, cd_ref, gain_ref,
             st_ref, dog_ref, dq_ref, dk_ref, dv_ref, dg_ref, dgain_ref, ds_acc):
        n = pl.program_id(1)

        @pl.when(n == 0)
        def _():
            ds_acc[...] = jnp.zeros_like(ds_acc)
            dgain_ref[...] = jnp.zeros_like(dgain_ref)

        cos, sin = cos_ref[...], sin_ref[...]
        scale = DK ** -0.5

        def head(hb):
            kc, vc = pl.ds(hb * DK, DK), pl.ds(hb * DV, DV)
            qr = _rot(q_ref[:, kc], cos, sin)
            kr = _rot(k_ref[:, kc], cos, sin) * scale
            v = v_ref[:, vc]
            g = g_ref[:, vc]
            sp = st_ref[hb, 0]
            dm = dm_ref[hb]
            qd, kd = qd_ref[hb], kd_ref[hb]
            gain_v = gain_ref[hb]
            scores = _bdot(qr, kr, NT) * dm
            inter = _bdot(qr * qd, sp)
            yield
            o = _bdot(scores, v) + inter
            yield
            oc = o - jnp.mean(o, axis=-1, keepdims=True)
            rstd = lax.rsqrt(jnp.mean(oc * oc, axis=-1, keepdims=True) + NORM_EPS)
            oh = oc * rstd
            dy = dog_ref[:, vc].astype(F32)
            dg_ref[:, vc] = (dy * oh * gain_v * _dsilu(g)).astype(dg_ref.dtype)
            dnorm = dy * _silu(g)
            dgain_ref[hb] += jnp.sum(dnorm * oh, axis=0, keepdims=True)
            doh = dnorm * gain_v
            do = rstd * (doh - jnp.mean(doh, axis=-1, keepdims=True)
                         - oh * jnp.mean(doh * oh, axis=-1, keepdims=True))
            dsn = ds_acc[hb]
            dp = _bdot(do, v, NT) * dm
            dq_inter = _bdot(do, sp, NT) * qd
            dk_inter = _bdot(v, dsn, NT) * kd
            dv_ref[:, vc] = (_bdot(scores, do, TN) + _bdot(kr * kd, dsn)).astype(dv_ref.dtype)
            ds_acc[hb] = dsn * cd_ref[hb] + _bdot(qr * qd, do, TN)
            yield
            dqr = _bdot(dp, kr) + dq_inter
            dkr = _bdot(dp, qr, TN) + dk_inter
            yield
            dq_ref[:, kc] = _unrot(dqr, cos, sin).astype(dq_ref.dtype)
            dk_ref[:, kc] = _unrot(dkr * scale, cos, sin).astype(dk_ref.dtype)

        _interleave([head(hb) for hb in range(RET_HB)])

    q, k, v, g, cs, dm, dec, cd, gn, st, ov = _ret_specs(N, True)
    return pl.pallas_call(
        body, name=name, grid=(H // RET_HB, N),
        in_specs=[q, k, v, g, cs, cs, dm, dec, dec, cd, gn, st, ov],
        out_specs=(q, q, ov, ov, gn),
        out_shape=(jax.ShapeDtypeStruct((T, H * DK), BF16), jax.ShapeDtypeStruct((T, H * DK), BF16),
                   jax.ShapeDtypeStruct((T, H * DV), BF16), jax.ShapeDtypeStruct((T, H * DV), BF16),
                   jax.ShapeDtypeStruct((H, 1, DV), F32)),
        scratch_shapes=[pltpu.VMEM((RET_HB, DK, DV), F32)],
        compiler_params=_params(("arbitrary", "arbitrary")),
    )(proj, proj, proj, proj, cos_t, sin_t, dm_t, qdec_t, kdec_t, cdec_t, gain, states, dog)


def _gla_specs(N, rev):
    H, C, DK, DV = GLA_HEADS, CHUNK, GLA_DK, GLA_DV
    cn = (lambda n: N - 1 - n) if rev else (lambda n: n)
    HB, G = GLA_HB, H // GLA_HB
    q = pl.BlockSpec((C, HB * DK), lambda h, n: (cn(n), h))
    k = pl.BlockSpec((C, HB * DK), lambda h, n: (cn(n), G + h))
    v = pl.BlockSpec((C, HB * DV), lambda h, n: (cn(n), G + h))
    r = pl.BlockSpec((C, HB * DV), lambda h, n: (cn(n), 2 * G + h))
    bias = pl.BlockSpec((1, HB * DK), lambda h, n: (0, h))
    gain = pl.BlockSpec((HB, 1, DV), lambda h, n: (h, 0, 0))
    st = pl.BlockSpec((HB, 1, DV, DK), lambda h, n: (h, cn(n), 0, 0))
    ov = pl.BlockSpec((C, HB * DV), lambda h, n: (cn(n), h))
    return q, k, v, r, bias, gain, st, ov


def _gla_chunk(q, k, v, gl_raw, bias):
    C, DK = q.shape
    gl = gl_raw + bias
    la = -_softplus(-gl) * (1.0 / GLA_TAU)
    lower = _iota2((C, C), 0) >= _iota2((C, C), 1)
    cum = _fdot(jnp.where(lower, 1.0, 0.0), la)
    yield
    ref = _pick_row(cum, C // 2 - 1)
    clast = _pick_row(cum, C - 1)
    fw, bw = jnp.exp(cum - ref), jnp.exp(ref - cum)
    qs = q * (DK ** -0.5)
    s_lo = _bdot(qs * fw, k * bw, NT)
    s_up = _bdot(qs * bw, k * fw, NT)
    yield
    scores = jnp.where(lower, s_lo, s_up)
    return gl, cum, clast, fw, bw, qs, k, v, scores, lower


def _gla_fwd(proj, glogit, bias, gain, *, name):
    T = proj.shape[0]
    H, C, DK, DV = GLA_HEADS, CHUNK, GLA_DK, GLA_DV
    N = T // C

    def body(q_ref, k_ref, v_ref, r_ref, gl_ref, bias_ref, gain_ref, og_ref, st_ref, s_acc):
        n = pl.program_id(1)

        @pl.when(n == 0)
        def _():
            s_acc[...] = jnp.zeros_like(s_acc)

        def head(hb):
            kc, vc = pl.ds(hb * DK, DK), pl.ds(hb * DV, DV)
            gl, cum, clast, fw, bw, qs, k, v, scores, lower = yield from _gla_chunk(
                q_ref[:, kc], k_ref[:, kc], v_ref[:, vc], gl_ref[:, kc], bias_ref[:, kc])
            sp = s_acc[hb]
            st_ref[hb, 0] = sp
            o = _bdot(scores, v) + _bdot(qs * jnp.exp(cum), sp, NT)
            s_acc[hb] = sp * jnp.exp(clast) + _bdot(v, k * jnp.exp(clast - cum), TN)
            yield
            rstd = lax.rsqrt(jnp.mean(o * o, axis=-1, keepdims=True) + NORM_EPS)
            og_ref[:, vc] = (o * rstd * gain_ref[hb] * _silu(r_ref[:, vc])).astype(og_ref.dtype)

        _interleave([head(hb) for hb in range(GLA_HB)])

    q, k, v, r, bias_s, gn, st, ov = _gla_specs(N, False)
    return pl.pallas_call(
        body, name=name, grid=(H // GLA_HB, N), in_specs=[q, k, v, r, q, bias_s, gn], out_specs=(ov, st),
        out_shape=(jax.ShapeDtypeStruct((T, H * DV), BF16), jax.ShapeDtypeStruct((H, N, DV, DK), F32)),
        scratch_shapes=[pltpu.VMEM((GLA_HB, DV, DK), F32)],
        compiler_params=_params(("arbitrary", "arbitrary")),
    )(proj, proj, proj, proj, glogit, bias, gain)


def _gla_bwd(proj, glogit, bias, gain, states, dog, *, name):
    T = proj.shape[0]
    H, C, DK, DV = GLA_HEADS, CHUNK, GLA_DK, GLA_DV
    N = T // C

    def body(q_ref, k_ref, v_ref, r_ref, gl_ref, bias_ref, gain_ref, st_ref, dog_ref,
             dq_ref, dk_ref, dv_ref, dr_ref, dgl_ref, dgain_ref, ds_acc):
        n = pl.program_id(1)

        @pl.when(n == 0)
        def _():
            ds_acc[...] = jnp.zeros_like(ds_acc)
            dgain_ref[...] = jnp.zeros_like(dgain_ref)

        def head(hb):
            kc, vc = pl.ds(hb * DK, DK), pl.ds(hb * DV, DV)
            gl, cum, clast, fw, bw, qs, k, v, scores, lower = yield from _gla_chunk(
                q_ref[:, kc], k_ref[:, kc], v_ref[:, vc], gl_ref[:, kc], bias_ref[:, kc])
            sp = st_ref[hb, 0]
            ecum, e2, cdec = jnp.exp(cum), jnp.exp(clast - cum), jnp.exp(clast)
            q_in, k_end = qs * ecum, k * e2
            o = _bdot(scores, v) + _bdot(q_in, sp, NT)
            yield
            rstd = lax.rsqrt(jnp.mean(o * o, axis=-1, keepdims=True) + NORM_EPS)
            oh = o * rstd
            r = r_ref[:, vc]
            gain_v = gain_ref[hb]
            dy = dog_ref[:, vc].astype(F32)
            dr_ref[:, vc] = (dy * oh * gain_v * _dsilu(r)).astype(dr_ref.dtype)
            dnorm = dy * _silu(r)
            dgain_ref[hb] += jnp.sum(dnorm * oh, axis=0, keepdims=True)
            doh = dnorm * gain_v
            do = rstd * (doh - oh * jnp.mean(doh * oh, axis=-1, keepdims=True))
            dsn = ds_acc[hb]
            dq_in = _bdot(do, sp)
            dk_end = _bdot(v, dsn)
            dv_ref[:, vc] = (_bdot(k_end, dsn, NT) + _bdot(scores, do, TN)).astype(dv_ref.dtype)
            dcdec = jnp.sum(dsn * sp, axis=0, keepdims=True)
            ds_acc[hb] = dsn * cdec + _bdot(do, q_in, TN)
            dsc = _bdot(do, v, NT)
            yield
            ds_lo = jnp.where(lower, dsc, 0.0)
            ds_up = jnp.where(lower, 0.0, dsc)
            qf, kb, qb, kf = qs * fw, k * bw, qs * bw, k * fw
            dqf, dkb = _bdot(ds_lo, kb), _bdot(ds_lo, qf, TN)
            dqb, dkf = _bdot(ds_up, kf), _bdot(ds_up, qb, TN)
            yield
            dq_ref[:, kc] = ((dqf * fw + dqb * bw + dq_in * ecum) * (DK ** -0.5)).astype(dq_ref.dtype)
            dk_ref[:, kc] = (dkb * bw + dkf * fw + dk_end * e2).astype(dk_ref.dtype)
            dz = (dqf * qs + dkf * k) * fw - (dqb * qs + dkb * k) * bw
            kk = dk_end * k_end
            dcum = dz + dq_in * q_in - kk
            rows = _iota2((C, DK), 0)
            dcum = dcum + jnp.where(rows == C // 2 - 1, -jnp.sum(dz, axis=0, keepdims=True), 0.0)
            dcum = dcum + jnp.where(rows == C - 1, jnp.sum(kk, axis=0, keepdims=True) + dcdec * cdec, 0.0)
            upper = _iota2((C, C), 0) <= _iota2((C, C), 1)
            dla = _fdot(jnp.where(upper, 1.0, 0.0), dcum)
            yield
            dgl_ref[:, kc] = dla * (1.0 / GLA_TAU) * _sigmoid(-gl)

        _interleave([head(hb) for hb in range(GLA_HB)])

    q, k, v, r, bias_s, gn, st, ov = _gla_specs(N, True)
    return pl.pallas_call(
        body, name=name, grid=(H // GLA_HB, N), in_specs=[q, k, v, r, q, bias_s, gn, st, ov],
        out_specs=(q, q, ov, ov, q, gn),
        out_shape=(jax.ShapeDtypeStruct((T, H * DK), BF16), jax.ShapeDtypeStruct((T, H * DK), BF16),
                   jax.ShapeDtypeStruct((T, H * DV), BF16), jax.ShapeDtypeStruct((T, H * DV), BF16),
                   jax.ShapeDtypeStruct((T, H * DK), F32), jax.ShapeDtypeStruct((H, 1, DV), F32)),
        scratch_shapes=[pltpu.VMEM((GLA_HB, DV, DK), F32)],
        compiler_params=_params(("arbitrary", "arbitrary")),
    )(proj, proj, proj, proj, glogit, bias, gain, states, dog)


def _conv(xv, w_ref):
    out = _shift_down(xv, CONV_WIDTH - 1) * w_ref[0:1, :]
    for tap in range(1, CONV_WIDTH):
        out = out + _shift_down(xv, CONV_WIDTH - 1 - tap) * w_ref[tap:tap + 1, :]
    return out


def _conv_bwd(xv, w_ref, dpre, dw_ref):
    dx = None
    for tap in range(CONV_WIDTH):
        s = CONV_WIDTH - 1 - tap
        t = _shift_up(dpre, s) * w_ref[tap:tap + 1, :]
        dx = t if dx is None else dx + t
        dw_ref[tap:tap + 1, :] = jnp.sum(dpre * _shift_down(xv, s), axis=0, keepdims=True)
    return dx


CONV_COLS = 256


def _conv_silu_fwd(x, w, *, name):
    T = x.shape[0]
    n = w.shape[1]

    def body(x_ref, w_ref, o_ref):
        o_ref[...] = _silu(_conv(x_ref[...], w_ref))

    return pl.pallas_call(
        body, name=name, grid=(n // CONV_COLS,),
        in_specs=[pl.BlockSpec((T, CONV_COLS), lambda j: (0, j)), pl.BlockSpec((CONV_WIDTH, CONV_COLS), lambda j: (0, j))],
        out_specs=pl.BlockSpec((T, CONV_COLS), lambda j: (0, j)),
        out_shape=jax.ShapeDtypeStruct((T, n), F32), compiler_params=_params(("parallel",)),
    )(x, w)


def _conv_silu_bwd(x, w, dact, *, name):
    T = x.shape[0]
    n = w.shape[1]

    def body(x_ref, w_ref, da_ref, dx_ref, dw_ref):
        xv = x_ref[...]
        dpre = da_ref[...] * _dsilu(_conv(xv, w_ref))
        dx_ref[...] = _conv_bwd(xv, w_ref, dpre, dw_ref).astype(dx_ref.dtype)

    blk = pl.BlockSpec((T, CONV_COLS), lambda j: (0, j))
    wb = pl.BlockSpec((CONV_WIDTH, CONV_COLS), lambda j: (0, j))
    return pl.pallas_call(
        body, name=name, grid=(n // CONV_COLS,), in_specs=[blk, wb, blk], out_specs=(blk, wb),
        out_shape=(jax.ShapeDtypeStruct((T, n), BF16), jax.ShapeDtypeStruct((CONV_WIDTH, n), F32)),
        compiler_params=_params(("parallel",)),
    )(x, w, dact)


def _interleave(gens):
    results = [None] * len(gens)
    live = list(range(len(gens)))
    while live:
        for i in list(live):
            try:
                next(gens[i])
            except StopIteration as done:
                results[i] = done.value
                live.remove(i)
    return results


def _unit_lower_inverse(a):
    n = a.shape[0]
    eye = jnp.where(_iota2((n, n), 0) == _iota2((n, n), 1), 1.0, 0.0)
    p = -a
    t = eye + p
    for _ in range(5):
        p = _fdot(p, p)
        yield
        t = t + _fdot(t, p)
        yield
    return t


def _gdn_specs(N, rev):
    H, C, DK, DV = GDN_HEADS, CHUNK, GDN_DK, GDN_DV
    cn = (lambda n: N - 1 - n) if rev else (lambda n: n)
    HB, G = GDN_HB, H // GDN_HB
    q = pl.BlockSpec((C, HB * DK), lambda h, n: (cn(n), h))
    k = pl.BlockSpec((C, HB * DK), lambda h, n: (cn(n), G + h))
    v = pl.BlockSpec((C, HB * DV), lambda h, n: (cn(n), 2 * G + h))
    z = pl.BlockSpec((C, HB * DV), lambda h, n: (cn(n), 3 * G + h))
    gates = pl.BlockSpec((C, LANE), lambda h, n: (cn(n), 0))
    sc = pl.BlockSpec((HB, 1, 1), lambda h, n: (h, 0, 0))
    gain = pl.BlockSpec((1, DV), lambda h, n: (0, 0))
    st = pl.BlockSpec((HB, 1, DK, DV), lambda h, n: (h, cn(n), 0, 0))
    return q, k, v, z, gates, sc, gain, st


def _gdn_solved_specs(N, rev):
    C = CHUNK
    cn = (lambda n: N - 1 - n) if rev else (lambda n: n)
    return (pl.BlockSpec((GDN_HB, 1, C, C), lambda h, n: (h, cn(n), 0, 0)),
            pl.BlockSpec((GDN_HB, 1, C, GDN_DV + GDN_DK), lambda h, n: (h, cn(n), 0, 0)))


def _gdn_chunk(q_ref, k_ref, v_ref, gates_ref, alog_ref, dtb_ref, h, solved=None):
    H, C, DK, DV = GDN_HEADS, CHUNK, GDN_DK, GDN_DV
    gates = gates_ref[...]
    lane = _iota2(gates.shape, 1)
    bl = jnp.sum(jnp.where(lane == h, gates, 0.0), axis=1, keepdims=True)
    al = jnp.sum(jnp.where(lane == H + h, gates, 0.0), axis=1, keepdims=True)
    beta = _sigmoid(bl)
    ea = jnp.exp(alog_ref[...])
    xs = al + dtb_ref[...]
    la = -ea * _softplus(xs)
    ii, jj = _iota2((C, C), 0), _iota2((C, C), 1)
    strict = ii > jj
    cum_col = jnp.sum(jnp.where(ii >= jj, _col_to_row(la), 0.0), axis=1, keepdims=True)
    cum_row = jnp.sum(jnp.where(ii <= jj, la, 0.0), axis=0, keepdims=True)
    q, k, v = q_ref[...], k_ref[...], v_ref[...]
    rq = lax.rsqrt(jnp.sum(q * q, axis=-1, keepdims=True) + NORM_EPS)
    rk = lax.rsqrt(jnp.sum(k * k, axis=-1, keepdims=True) + NORM_EPS)
    qn = q * rq * (DK ** -0.5)
    kn = k * rk
    rel = jnp.where(strict, jnp.exp(jnp.where(strict, cum_col - cum_row, 0.0)), 0.0)
    rg = rel * _bdot(kn, kn, NT)
    yield
    a = beta * rg
    e_col = jnp.exp(cum_col)
    clast = _pick_row(cum_col, C - 1)
    if solved is None:
        tm = yield from _unit_lower_inverse(a)
        rhs = jnp.concatenate([beta * v, (beta * e_col) * kn], axis=1)
        sol = _fdot(tm, rhs)
        yield
    else:
        tm, sol = solved
    u, w = sol[:, :DV], sol[:, DV:]
    dd = jnp.exp(clast - cum_col)
    ke = kn * dd
    g = jnp.exp(clast)
    eye = jnp.where(_iota2((DK, DK), 0) == _iota2((DK, DK), 1), 1.0, 0.0)
    trans = g * eye - _bdot(ke, w, TN)
    inject = _bdot(ke, u, TN)
    yield
    return dict(beta=beta, ea=ea, xs=xs, la=la, strict=strict, ii=ii, jj=jj, q=q, k=k, v=v, rq=rq, rk=rk,
                qn=qn, kn=kn, rel=rel, rg=rg, a=a, tm=tm, e_col=e_col, sol=sol, u=u, w=w, dd=dd, ke=ke,
                g=g, eye=eye, trans=trans, inject=inject)


def _gdn_fwd(act, proj, gates, a_log, dt_bias, gain, *, name):
    T = act.shape[0]
    H, C, DK, DV = GDN_HEADS, CHUNK, GDN_DK, GDN_DV
    N = T // C

    def body(q_ref, k_ref, v_ref, z_ref, gates_ref, alog_ref, dtb_ref, gain_ref, og_ref, st_ref, tm_ref, sol_ref,
             s_acc):
        hg, n = pl.program_id(0), pl.program_id(1)

        @pl.when(n == 0)
        def _():
            s_acc[...] = jnp.zeros_like(s_acc)

        def head(hb):
            cols = pl.ds(hb * DK, DK)
            c = yield from _gdn_chunk(q_ref.at[:, cols], k_ref.at[:, cols], v_ref.at[:, cols], gates_ref,
                                      alog_ref.at[hb], dtb_ref.at[hb], hg * GDN_HB + hb)
            tm_ref[hb, 0] = c["tm"]
            sol_ref[hb, 0] = c["sol"]
            sp = s_acc[hb]
            st_ref[hb, 0] = sp
            snew = _bdot(c["trans"], sp) + c["inject"]
            yield
            s_acc[hb] = snew
            o = _bdot(c["qn"], snew)
            yield
            rstd = lax.rsqrt(jnp.mean(o * o, axis=-1, keepdims=True) + NORM_EPS)
            og_ref[:, cols] = (o * rstd * gain_ref[...] * _silu(z_ref[:, cols])).astype(og_ref.dtype)

        _interleave([head(hb) for hb in range(GDN_HB)])

    q, k, v, z, gt, sc, gn, st = _gdn_specs(N, False)
    tm_s, sol_s = _gdn_solved_specs(N, False)
    return pl.pallas_call(
        body, name=name, grid=(H // GDN_HB, N), in_specs=[q, k, v, z, gt, sc, sc, gn],
        out_specs=(q, st, tm_s, sol_s),
        out_shape=(jax.ShapeDtypeStruct((T, H * DV), BF16), jax.ShapeDtypeStruct((H, N, DK, DV), F32),
                   jax.ShapeDtypeStruct((H, N, C, C), F32), jax.ShapeDtypeStruct((H, N, C, DV + DK), F32)),
        scratch_shapes=[pltpu.VMEM((GDN_HB, DK, DV), F32)],
        compiler_params=_params(("arbitrary", "arbitrary")),
    )(act, act, act, proj, gates, a_log, dt_bias, gain)


def _gdn_bwd(act, proj, gates, a_log, dt_bias, gain, states, solved, dog, *, name):
    T = act.shape[0]
    H, C, DK, DV = GDN_HEADS, CHUNK, GDN_DK, GDN_DV
    N = T // C

    def rsum(x):
        return jnp.sum(x, axis=1, keepdims=True)

    def body(q_ref, k_ref, v_ref, z_ref, gates_ref, alog_ref, dtb_ref, gain_ref, st_ref, tm_ref, sol_ref, dog_ref,
             dq_ref, dk_ref, dv_ref, dz_ref, dgates_ref, dscal_ref, dgain_ref, ds_acc):
        hg, n = pl.program_id(0), pl.program_id(1)

        @pl.when(n == 0)
        def _():
            ds_acc[...] = jnp.zeros_like(ds_acc)
            dgain_ref[...] = jnp.zeros_like(dgain_ref)
            dscal_ref[...] = jnp.zeros_like(dscal_ref)

        _interleave([one_head(hb, hg * GDN_HB + hb, q_ref, k_ref, v_ref, z_ref, gates_ref, alog_ref, dtb_ref, gain_ref,
                              st_ref, tm_ref, sol_ref, dog_ref, dq_ref, dk_ref, dv_ref, dz_ref, dgates_ref, dscal_ref,
                              dgain_ref, ds_acc)
                     for hb in range(GDN_HB)])

    def one_head(hb, h, q_ref, k_ref, v_ref, z_ref, gates_ref, alog_ref, dtb_ref, gain_ref, st_ref, tm_ref, sol_ref,
                 dog_ref, dq_ref, dk_ref, dv_ref, dz_ref, dgates_ref, dscal_ref, dgain_ref, ds_acc):
        cols = pl.ds(hb * DK, DK)
        c = yield from _gdn_chunk(q_ref.at[:, cols], k_ref.at[:, cols], v_ref.at[:, cols], gates_ref,
                                  alog_ref.at[hb], dtb_ref.at[hb], h, solved=(tm_ref[hb, 0], sol_ref[hb, 0]))
        beta, kn, qn, v, ke, u, w, dd, e_col = c["beta"], c["kn"], c["qn"], c["v"], c["ke"], c["u"], c["w"], c["dd"], c["e_col"]
        sp = st_ref[hb, 0]
        snew = _bdot(c["trans"], sp) + c["inject"]
        yield
        o = _bdot(qn, snew)
        yield
        rstd = lax.rsqrt(jnp.mean(o * o, axis=-1, keepdims=True) + NORM_EPS)
        oh = o * rstd
        z = z_ref[:, cols]
        gain_v = gain_ref[...]
        dy = dog_ref[:, cols].astype(F32)
        dz_ref[:, cols] = (dy * oh * gain_v * _dsilu(z)).astype(dz_ref.dtype)
        dnorm = dy * _silu(z)
        dgain_ref[hb] += jnp.sum(dnorm * oh, axis=0, keepdims=True)
        doh = dnorm * gain_v
        do = rstd * (doh - oh * jnp.mean(doh * oh, axis=-1, keepdims=True))

        dstot = ds_acc[hb] + _bdot(qn, do, TN)
        dqn = _bdot(do, snew, NT)
        yield
        dtrans = _bdot(dstot, sp, NT)
        ds_acc[hb] = _bdot(c["trans"], dstot, TN)
        yield
        dg = jnp.sum(jnp.sum(dtrans * c["eye"], axis=1, keepdims=True), axis=0, keepdims=True)
        m = -dtrans
        dke = _bdot(w, m, NT) + _bdot(u, dstot, NT)
        dw = _bdot(ke, m)
        du = _bdot(ke, dstot)
        yield
        drhs = _fdot(c["tm"], jnp.concatenate([du, dw], axis=1), TN)
        yield
        da = jnp.where(c["strict"], -_fdot(drhs, c["sol"], NT), 0.0)
        yield
        drhs_u, drhs_w = drhs[:, :DV], drhs[:, DV:]
        rwk = rsum(drhs_w * kn)
        dbeta = rsum(da * c["rg"]) + rsum(drhs_u * v) + rwk * e_col
        dgm = da * beta * c["rel"]
        dkn = _bdot(dgm, kn) + _bdot(dgm, kn, TN) + (beta * e_col) * drhs_w + dd * dke
        yield
        dv_ref[:, cols] = beta * drhs_u
        r_ = da * c["a"]
        ddd = rsum(dke * kn)
        dc = rsum(r_) - _row_to_col(jnp.sum(r_, axis=0, keepdims=True)) + beta * rwk * e_col - ddd * dd
        dclast = jnp.sum(ddd * dd, axis=0, keepdims=True) + dg * c["g"]
        dc = dc + jnp.where(_iota2((C, 1), 0) == C - 1, dclast, 0.0)
        dla = jnp.sum(jnp.where(c["ii"] <= c["jj"], _col_to_row(dc), 0.0), axis=1, keepdims=True)
        dalog = jnp.sum(dla * c["la"], axis=0, keepdims=True)
        dxs = dla * (-c["ea"]) * _sigmoid(c["xs"])
        ddtb = jnp.sum(dxs, axis=0, keepdims=True)
        dbl = dbeta * beta * (1.0 - beta)
        lane = _iota2((C, LANE), 1)
        dgates_ref[hb] = jnp.where(lane == 0, dbl, jnp.where(lane == 1, dxs, 0.0))
        lane8 = _iota2((8, LANE), 1)
        dscal_ref[hb] += jnp.where(lane8 == 0, dalog, jnp.where(lane8 == 1, ddtb, 0.0))
        dk_ref[:, cols] = c["rk"] * (dkn - kn * rsum(dkn * kn))
        qh = c["q"] * c["rq"]
        dqs = dqn * (DK ** -0.5)
        dq_ref[:, cols] = c["rq"] * (dqs - qh * rsum(dqs * qh))

    q, k, v, z, gt, sc, gn, st = _gdn_specs(N, True)
    dgt = pl.BlockSpec((GDN_HB, C, LANE), lambda h, n: (h, N - 1 - n, 0))
    dsc = pl.BlockSpec((GDN_HB, 8, LANE), lambda h, n: (h, 0, 0))
    dgn = pl.BlockSpec((GDN_HB, 1, DV), lambda h, n: (h, 0, 0))
    sh = jax.ShapeDtypeStruct((T, H * DK), F32)
    tm_s, sol_s = _gdn_solved_specs(N, True)
    return pl.pallas_call(
        body, name=name, grid=(H // GDN_HB, N), in_specs=[q, k, v, z, gt, sc, sc, gn, st, tm_s, sol_s, q],
        out_specs=(q, q, q, q, dgt, dsc, dgn),
        out_shape=(sh, sh, sh, jax.ShapeDtypeStruct((T, H * DV), BF16),
                   jax.ShapeDtypeStruct((H, T, LANE), F32), jax.ShapeDtypeStruct((H, 8, LANE), F32),
                   jax.ShapeDtypeStruct((H, 1, DV), F32)),
        scratch_shapes=[pltpu.VMEM((GDN_HB, DK, DV), F32)],
        compiler_params=_params(("arbitrary", "arbitrary")),
    )(act, act, act, proj, gates, a_log, dt_bias, gain, states, *solved, dog)


SUBLANES = 8


def _linear_scan(a_ref, b_ref, h_ref, reverse):
    T, W = a_ref.shape
    nb = T // SUBLANES
    row = _iota2((SUBLANES, W), 0)

    def blk(bi, carry):
        bb = (nb - 1 - bi) if reverse else bi
        off = pl.multiple_of(bb * SUBLANES, SUBLANES)
        a = a_ref[pl.ds(off, SUBLANES), :]
        b = b_ref[pl.ds(off, SUBLANES), :]
        for d in (1, 2, 4):
            if reverse:
                edge = row >= SUBLANES - d
                a_sh = jnp.where(edge, 1.0, pltpu.roll(a, SUBLANES - d, 0))
                b_sh = jnp.where(edge, 0.0, pltpu.roll(b, SUBLANES - d, 0))
            else:
                edge = row < d
                a_sh = jnp.where(edge, 1.0, pltpu.roll(a, d, 0))
                b_sh = jnp.where(edge, 0.0, pltpu.roll(b, d, 0))
            b = a * b_sh + b
            a = a * a_sh
        h = a * carry + b
        h_ref[pl.ds(off, SUBLANES), :] = h
        return h[0:1, :] if reverse else h[SUBLANES - 1:SUBLANES, :]

    lax.fori_loop(0, nb, blk, jnp.zeros((1, W), F32))


def _lru_specs(T):
    B, W = LRU_BLOCKS, LRU_BLOCK
    xb = pl.BlockSpec((T, W), lambda j: (0, j))
    yb = pl.BlockSpec((T, W), lambda j: (0, B + j))
    cw = pl.BlockSpec((CONV_WIDTH, W), lambda j: (0, j))
    vec = pl.BlockSpec((1, W), lambda j: (0, j))
    wg = pl.BlockSpec((1, W, W), lambda j: (j, 0, 0))
    bg = pl.BlockSpec((1, 1, W), lambda j: (j, 0, 0))
    return xb, yb, cw, vec, wg, bg


def _lru_gates(xb_ref, cw_ref, cb_ref, wr_ref, br_ref, wi_ref, bi_ref, lam_ref):
    xv = xb_ref[...]
    xc = _conv(xv, cw_ref) + cb_ref[...]
    r = _sigmoid(_bdot(xc, wr_ref[0]) + br_ref[0])
    i = _sigmoid(_bdot(xc, wi_ref[0]) + bi_ref[0])
    sp = _softplus(-lam_ref[...])
    la = -LRU_C * sp * r
    a = jnp.exp(la)
    s = jnp.sqrt(-_expm1(2.0 * la))
    return xv, xc, r, i, sp, a, s


def _lru_fwd(proj, conv_w, conv_b, w_r, b_r, w_i, b_i, lam, *, name):
    T = proj.shape[0]
    B, W = LRU_BLOCKS, LRU_BLOCK

    def body(xb_ref, yb_ref, cw_ref, cb_ref, wr_ref, br_ref, wi_ref, bi_ref, lam_ref, og_ref, hs_ref, a_s, u_s):
        xv, xc, r, i, sp, a, s = _lru_gates(xb_ref, cw_ref, cb_ref, wr_ref, br_ref, wi_ref, bi_ref, lam_ref)
        a_s[...] = a
        u_s[...] = s * (i * xc)
        _linear_scan(a_s, u_s, hs_ref, False)
        og_ref[...] = (hs_ref[...] * _gelu(yb_ref[...])).astype(og_ref.dtype)

    xb, yb, cw, vec, wg, bg = _lru_specs(T)
    return pl.pallas_call(
        body, name=name, grid=(B,), in_specs=[xb, yb, cw, vec, wg, bg, wg, bg, vec], out_specs=(xb, xb),
        out_shape=(jax.ShapeDtypeStruct((T, B * W), BF16), jax.ShapeDtypeStruct((T, B * W), F32)),
        scratch_shapes=[pltpu.VMEM((T, W), F32), pltpu.VMEM((T, W), F32)],
        compiler_params=_params(("arbitrary",)),
    )(proj, proj, conv_w, conv_b, w_r, b_r, w_i, b_i, lam)


def _lru_bwd(proj, conv_w, conv_b, w_r, b_r, w_i, b_i, lam, hs, dout, *, name):
    T = proj.shape[0]
    B, W = LRU_BLOCKS, LRU_BLOCK

    def csum(x):
        return jnp.sum(x, axis=0, keepdims=True)

    def body(xb_ref, yb_ref, cw_ref, cb_ref, wr_ref, br_ref, wi_ref, bi_ref, lam_ref, hs_ref, do_ref,
             dxb_ref, dyb_ref, dcw_ref, dcb_ref, dwr_ref, dbr_ref, dwi_ref, dbi_ref, dlam_ref, a_s, b_s, g_s):
        xv, xc, r, i, sp, a, s = _lru_gates(xb_ref, cw_ref, cb_ref, wr_ref, br_ref, wi_ref, bi_ref, lam_ref)
        h = hs_ref[...]
        yb = yb_ref[...]
        dout = do_ref[...].astype(F32)
        dyb_ref[...] = (dout * h * _dgelu(yb)).astype(dyb_ref.dtype)
        a_s[...] = _shift_up(a, 1)
        b_s[...] = dout * _gelu(yb)
        _linear_scan(a_s, b_s, g_s, True)
        g = g_s[...]
        da = g * _shift_down(h, 1)
        ds = g * (i * xc)
        di = g * s * xc
        dxc = g * s * i
        dla = da * a - ds * (a * a) / s
        dr = dla * (-LRU_C * sp)
        dlam_ref[...] = csum(dla * r) * (LRU_C * _sigmoid(-lam_ref[...]))
        dpr = dr * r * (1.0 - r)
        dpi = di * i * (1.0 - i)
        dxc = dxc + _bdot(dpr, wr_ref[0], NT) + _bdot(dpi, wi_ref[0], NT)
        dwr_ref[0] = _bdot(xc, dpr, TN)
        dwi_ref[0] = _bdot(xc, dpi, TN)
        dbr_ref[0] = csum(dpr)
        dbi_ref[0] = csum(dpi)
        dcb_ref[...] = csum(dxc)
        dxb_ref[...] = _conv_bwd(xv, cw_ref, dxc, dcw_ref).astype(dxb_ref.dtype)

    xb, yb, cw, vec, wg, bg = _lru_specs(T)
    act = jax.ShapeDtypeStruct((T, B * W), BF16)
    return pl.pallas_call(
        body, name=name, grid=(B,), in_specs=[xb, yb, cw, vec, wg, bg, wg, bg, vec, xb, xb],
        out_specs=(xb, xb, cw, vec, wg, bg, wg, bg, vec),
        out_shape=(act, act, jax.ShapeDtypeStruct((CONV_WIDTH, B * W), F32), jax.ShapeDtypeStruct((1, B * W), F32),
                   jax.ShapeDtypeStruct((B, W, W), F32), jax.ShapeDtypeStruct((B, 1, W), F32),
                   jax.ShapeDtypeStruct((B, W, W), F32), jax.ShapeDtypeStruct((B, 1, W), F32),
                   jax.ShapeDtypeStruct((1, B * W), F32)),
        scratch_shapes=[pltpu.VMEM((T, W), F32)] * 3,
        compiler_params=_params(("arbitrary",)),
    )(proj, proj, conv_w, conv_b, w_r, b_r, w_i, b_i, lam, hs, dout)


MESH = pl.DeviceIdType.MESH
N_CHIPS = 4
AG_COPIES = 7


def _mesh_pos():
    return lax.axis_index("x"), lax.axis_index("y"), lax.axis_index("c")


def _hbm_specs(n):
    return [pl.BlockSpec(memory_space=pltpu.HBM)] * n


def _all_gather(shards, *, name):
    n = len(shards)

    def body(*refs):
        xs, outs = refs[:n], refs[n:2 * n]
        send_sems, recv_sems, local_sems = refs[2 * n:]
        x, y, c = _mesh_pos()
        me, sibling = (x, y, c), (x, y, 1 - c)
        chips = [(1 - x, y), (x, 1 - y), (1 - x, 1 - y)]

        def rows(t, px, py, pc):
            return outs[t].at[4 * px + 2 * py + pc]

        def copy(t, k, block, to, src=None):
            return pltpu.make_async_remote_copy(
                src_ref=rows(t, *block) if src is None else src, dst_ref=rows(t, *block),
                send_sem=send_sems.at[t * AG_COPIES + k], recv_sem=recv_sems.at[t * AG_COPIES + k],
                device_id=to, device_id_type=MESH)

        mine = [pltpu.make_async_copy(xs[t], rows(t, *me), local_sems.at[t]) for t in range(n)]
        for cp in mine:
            cp.start()
        first = []
        for t in range(n):
            first.append(copy(t, 0, me, sibling, src=xs[t]))
            first += [copy(t, 1 + j, me, (*chip, c), src=xs[t]) for j, chip in enumerate(chips)]
        for cp in first:
            cp.start()
        passed = []
        for j, chip in enumerate(chips):
            for t in range(n):
                copy(t, 1 + j, (*chip, c), me).wait_recv()
                cp = copy(t, 4 + j, (*chip, c), sibling)
                cp.start()
                passed.append(cp)
        for t in range(n):
            copy(t, 0, sibling, me).wait_recv()
        for j, chip in enumerate(chips):
            for t in range(n):
                copy(t, 4 + j, (*chip, 1 - c), me).wait_recv()
        for cp in first + passed:
            cp.wait_send()
        for cp in mine:
            cp.wait()

    return pl.pallas_call(
        body, name=name,
        out_shape=[jax.ShapeDtypeStruct((N_DEV,) + s.shape, s.dtype) for s in shards],
        in_specs=_hbm_specs(n), out_specs=_hbm_specs(n),
        scratch_shapes=[pltpu.SemaphoreType.DMA((n * AG_COPIES,)), pltpu.SemaphoreType.DMA((n * AG_COPIES,)),
                        pltpu.SemaphoreType.DMA((n,))],
    )(*shards)


SIDE_EFFECT = pltpu.SideEffectType.DATAFLOW_SIDE_EFFECTING


def _copies(plan, refs, send_sems, recv_sems):
    return [pltpu.make_async_remote_copy(src_ref=src, dst_ref=dst, send_sem=send_sems.at[k], recv_sem=recv_sems.at[k],
                                         device_id=to, device_id_type=MESH)
            for k, (src, dst, to) in enumerate(plan(refs))]


def _split_start(bufs, plan, n_copies, *, name, deps=()):
    n = len(bufs)

    def body(*refs):
        send_sems, recv_sems = refs[n + len(deps)], refs[n + len(deps) + 1]
        token = refs[-1]
        for cp in _copies(plan, refs[:n], send_sems, recv_sems):
            cp.start()
        token[...] = jnp.zeros_like(token)

    hbm, sem = pl.BlockSpec(memory_space=pltpu.HBM), pl.BlockSpec(memory_space=pltpu.SEMAPHORE)
    out = pl.pallas_call(
        body, name=name,
        out_shape=(pltpu.SemaphoreType.DMA((n_copies,)), pltpu.SemaphoreType.DMA((n_copies,)),
                   *[pltpu.HBM(b.shape, b.dtype) for b in bufs], jax.ShapeDtypeStruct(TOKEN_SHAPE, F32)),
        in_specs=[hbm] * n + [pl.BlockSpec(memory_space=pl.ANY)] * len(deps),
        out_specs=(sem, sem, *[hbm] * n, pl.BlockSpec(memory_space=pltpu.VMEM)),
        input_output_aliases={i: 2 + i for i in range(n)},
        compiler_params=pltpu.CompilerParams(has_side_effects=SIDE_EFFECT),
    )(*[pltpu.with_memory_space_constraint(b, pltpu.HBM) for b in bufs], *deps)
    return out[0], out[1], list(out[2:2 + n]), out[-1]


def _split_wait(send_sems, recv_sems, bufs, plan, after, *, name):
    n = len(bufs)

    def body(*refs):
        for cp in _copies(plan, refs[:n], refs[n], refs[n + 1]):
            cp.wait_send()
            cp.wait_recv()

    hbm, sem = pl.BlockSpec(memory_space=pltpu.HBM), pl.BlockSpec(memory_space=pltpu.SEMAPHORE)
    out = pl.pallas_call(
        body, name=name, out_shape=tuple(pltpu.HBM(b.shape, b.dtype) for b in bufs),
        in_specs=[hbm] * n + [sem, sem, pl.BlockSpec(memory_space=pl.ANY)], out_specs=tuple([hbm] * n),
        input_output_aliases={i: i for i in range(n)},
        compiler_params=pltpu.CompilerParams(has_side_effects=SIDE_EFFECT),
    )(*bufs, send_sems, recv_sems, after)
    return list(out)


def _block(ref, in_cols, d):
    if not in_cols:
        return ref.at[d]
    c = ref.shape[1] // N_DEV
    return ref.at[:, pl.ds(pl.multiple_of(d * c, LANE), c)]


def _plan_gather_a(in_cols):
    n = len(in_cols)

    def plan(refs):
        x, y, c = _mesh_pos()
        me = 4 * x + 2 * y + c
        peers = [(x, y, 1 - c), (1 - x, y, c), (x, 1 - y, c), (1 - x, 1 - y, c)]
        return [(refs[t], _block(refs[n + t], in_cols[t], me), to) for t in range(n) for to in peers]
    return plan


def _plan_gather_b(in_cols):
    n = len(in_cols)

    def plan(refs):
        x, y, c = _mesh_pos()
        ds = [4 * px + 2 * py + c for px, py in [(1 - x, y), (x, 1 - y), (1 - x, 1 - y)]]
        return [(_block(refs[t], in_cols[t], d), _block(refs[t], in_cols[t], d), (x, y, 1 - c))
                for t in range(n) for d in ds]
    return plan


def _plan_scatter_pair(in_cols):
    n = len(in_cols)

    def plan(refs):
        x, y, c = _mesh_pos()
        return [(_block(refs[t], in_cols[t], 2 * q + (1 - c)), refs[n + t].at[q], (x, y, 1 - c))
                for t in range(n) for q in range(N_CHIPS)]
    return plan


def _plan_scatter_chips(n):
    def plan(refs):
        x, y, c = _mesh_pos()
        chips = [(1 - x, y), (x, 1 - y), (1 - x, 1 - y)]
        return [(refs[t].at[2 * px + py], refs[n + t].at[j], (px, py, c))
                for t in range(n) for j, (px, py) in enumerate(chips)]
    return plan


def _insert_block(land, shard, device, in_cols, *, name):
    r, c = shard.shape
    tr = _tile(r, (512, 256))

    def body(d_ref, s_ref, land_ref, o_ref):
        if in_cols:
            o_ref[...] = s_ref[...]
        else:
            o_ref[0] = s_ref[...]

    if in_cols:
        o_spec = pl.BlockSpec((tr, c), lambda i, d: (i, d[0]))
    else:
        o_spec = pl.BlockSpec((1, tr, c), lambda i, d: (d[0], i, 0))
    return pl.pallas_call(
        body, name=name,
        grid_spec=pltpu.PrefetchScalarGridSpec(
            num_scalar_prefetch=1, grid=(r // tr,),
            in_specs=[pl.BlockSpec((tr, c), lambda i, d: (i, 0)), pl.BlockSpec(memory_space=pl.ANY)],
            out_specs=o_spec),
        out_shape=jax.ShapeDtypeStruct(land.shape, land.dtype), input_output_aliases={2: 0},
        compiler_params=_params(("parallel",)),
    )(device, shard, land)


PAIR_ROWS = (512, 256)


def _pair_add(g, a, core, in_cols, *, name):
    _, R, C = a.shape
    tr = _tile(R, PAIR_ROWS)

    def body(c_ref, g_ref, a_ref, o_ref):
        gv = g_ref[...] if in_cols else g_ref[0]
        o_ref[0] = (gv.astype(F32) + a_ref[0].astype(F32)).astype(o_ref.dtype)

    blk = pl.BlockSpec((1, tr, C), lambda q, i, c: (q, i, 0))
    if in_cols:
        g_spec = pl.BlockSpec((tr, C), lambda q, i, c: (i, 2 * q + c[0]))
    else:
        g_spec = pl.BlockSpec((1, tr, C), lambda q, i, c: (2 * q + c[0], i, 0))
    return pl.pallas_call(
        body, name=name,
        grid_spec=pltpu.PrefetchScalarGridSpec(
            num_scalar_prefetch=1, grid=(N_CHIPS, R // tr), in_specs=[g_spec, blk], out_specs=blk),
        out_shape=jax.ShapeDtypeStruct((N_CHIPS, R, C), BF16),
        compiler_params=_params(("parallel", "parallel")),
    )(core, g, a)


ADAM_ROWS = 256


def _adamw_sharded(w, m, v, s4, b3, chip, layer, prev, *, name):
    L, R, C = w.shape
    tr = _tile(R, (ADAM_ROWS,))
    if prev is None and L > 1:
        prev = tuple(lax.empty(w.shape, F32) for _ in range(4))
    n_prev = 0 if prev is None else 4

    def body(q_ref, w_ref, m_ref, v_ref, s_ref, b_ref, *rest):
        g_out, d_out, m_out, v_out = rest[n_prev:]
        g = s_ref[0].astype(F32)
        for j in range(N_CHIPS - 1):
            g = g + b_ref[j].astype(F32)
        mn = ADAM_B1 * m_ref[0] + (1.0 - ADAM_B1) * g
        vn = ADAM_B2 * v_ref[0] + (1.0 - ADAM_B2) * (g * g)
        g_out[0] = g
        d_out[0] = -ADAM_LR * ((mn / ADAM_C1) / (jnp.sqrt(vn / ADAM_C2) + ADAM_EPS) + ADAM_WD * w_ref[0])
        m_out[0] = mn
        v_out[0] = vn

    blk = pl.BlockSpec((1, tr, C), lambda i, q: (layer, i, 0))
    sh = jax.ShapeDtypeStruct((L, R, C), F32)
    return pl.pallas_call(
        body, name=name,
        grid_spec=pltpu.PrefetchScalarGridSpec(
            num_scalar_prefetch=1, grid=(R // tr,),
            in_specs=[blk, blk, blk, pl.BlockSpec((1, tr, C), lambda i, q: (q[0], i, 0)),
                      pl.BlockSpec((N_CHIPS - 1, tr, C), lambda i, q: (0, i, 0))]
            + [pl.BlockSpec(memory_space=pl.ANY)] * n_prev,
            out_specs=(blk,) * 4),
        out_shape=(sh,) * 4, input_output_aliases={6 + k: k for k in range(n_prev)},
        compiler_params=_params(("parallel",)),
    )(chip, w, m, v, s4, b3, *(prev or ()))


FWD_NAMES = ['x', 'norm1', 'norm2', 'final_norm', 'ret_w_in', 'ret_gn_gain', 'ret_w_out', 'gdn_w_in', 'gdn_conv_w',
             'gdn_a_log', 'gdn_dt_bias', 'gdn_norm_gain', 'gdn_w_out', 'gla_w_in', 'gla_w_gate_up', 'gla_gate_bias',
             'gla_norm_gain', 'gla_w_out', 'lru_w_in', 'lru_conv_w', 'lru_conv_b', 'lru_w_rgate', 'lru_b_rgate',
             'lru_w_igate', 'lru_b_igate', 'lru_lambda', 'lru_w_out', 'mlp_w_up', 'mlp_w_down']
WEIGHT_NAMES = FWD_NAMES[1:]
ARG_NAMES = FWD_NAMES + ['loss_target'] + ['m_' + n for n in WEIGHT_NAMES] + ['v_' + n for n in WEIGHT_NAMES]

MIXER_IN = ('ret_w_in', 'gdn_w_in', 'gla_w_in', 'lru_w_in')
MIXER_OUT = ('ret_w_out', 'gdn_w_out', 'gla_w_out', 'lru_w_out')
BIG_NAMES = MIXER_IN + MIXER_OUT + ('mlp_w_up', 'mlp_w_down')
SMALL = {'norm1': False, 'norm2': False, 'final_norm': False, 'ret_gn_gain': True, 'gdn_conv_w': True,
         'gdn_a_log': False, 'gdn_dt_bias': False, 'gdn_norm_gain': False, 'gla_w_gate_up': True,
         'gla_gate_bias': True, 'gla_norm_gain': True, 'lru_conv_w': True, 'lru_conv_b': True,
         'lru_w_rgate': False, 'lru_b_rgate': False, 'lru_w_igate': False, 'lru_b_igate': False, 'lru_lambda': True}
SMALL_NAMES = tuple(n for n in WEIGHT_NAMES if n in SMALL)
MEDIUM = ('lru_w_rgate', 'lru_w_igate')
EARLY_SMALL = tuple(n for n in SMALL_NAMES if n.startswith(('gla_', 'lru_')) and n not in MEDIUM)
LATE_SMALL = tuple(n for n in SMALL_NAMES if n not in EARLY_SMALL + MEDIUM)
GDN_TAIL = 2 * GDN_HEADS
GDN_MAIN = 4 * D_MODEL
GLA_MAIN = 3 * D_MODEL


PACK_ROWS = 256


PACK_TILE = SUBLANES * LANE


def _pack(arrs):
    rows = []
    for a in arrs:
        f = a.reshape(-1).astype(F32)
        rows.append(jnp.pad(f, (0, (-f.shape[0]) % PACK_TILE)).reshape(-1, LANE))
    fill = (-sum(r.shape[0] for r in rows)) % PACK_ROWS
    if fill:
        rows.append(jnp.zeros((fill, LANE), F32))
    return jnp.concatenate(rows, axis=0)


def _unpack(buf, shapes, lead=()):
    out, r0 = [], 0
    for s in shapes:
        n = int(np.prod(s))
        nr = -(-n // PACK_TILE) * SUBLANES
        blk = buf[..., r0:r0 + nr, :].reshape(lead + (nr * LANE,))[..., :n]
        out.append(blk.reshape(lead + tuple(s)))
        r0 += nr
    return out


def _full_cols(g):
    return jnp.transpose(g, (1, 0, 2)).reshape(g.shape[1], N_DEV * g.shape[2])


def _full_rows(g):
    return g.reshape(N_DEV * g.shape[1], g.shape[2])


def _blocks_cols(dw):
    r, c = dw.shape[0], dw.shape[1] // N_DEV
    return jnp.transpose(dw.reshape(r, N_DEV, c), (1, 0, 2))


def _blocks_rows(dw):
    return dw.reshape(N_DEV, dw.shape[0] // N_DEV, dw.shape[1])


def _pad_cols(a, n=LANE):
    return jnp.pad(a, ((0, 0), (0, n - a.shape[1])))


def _mixer_fwd(layer, hn, w_in, sm, tables, deps=()):
    tag = f"l{layer}"
    if layer == 0:
        proj = _matmul(hn, w_in, name=tag + "_in", deps=deps)
        gain = sm['ret_gn_gain'][0][:, None, :]
        og, st = _ret_fwd(proj, gain, tables, name=tag + "_ret_fwd")
        return og, dict(proj=proj, st=st, gain=gain)
    if layer == 1:
        w_main, w_tail = w_in, _pad_cols(w_in[:, GDN_MAIN:])
        proj = _matmul(hn, w_main, b_cols=GDN_MAIN, name=tag + "_in", deps=deps)
        gates = _matmul(hn, w_tail, name=tag + "_in_tail")
        conv_w = sm['gdn_conv_w'][0]
        act = _conv_silu_fwd(proj, conv_w, name=tag + "_conv")
        a_log = sm['gdn_a_log'].reshape(GDN_HEADS, 1, 1)
        dt_bias = sm['gdn_dt_bias'].reshape(GDN_HEADS, 1, 1)
        gain = sm['gdn_norm_gain']
        og, st, tm, sol = _gdn_fwd(act, proj, gates, a_log, dt_bias, gain, name=tag + "_gdn_fwd")
        return og, dict(proj=proj, gates=gates, act=act, st=st, solved=(tm, sol), conv_w=conv_w, a_log=a_log, dt_bias=dt_bias,
                        gain=gain, w_main=w_main, w_tail=w_tail)
    if layer == 2:
        w_main, w_tail = w_in, _pad_cols(w_in[:, GLA_MAIN:])
        proj = _matmul(hn, w_main, b_cols=GLA_MAIN, name=tag + "_in", deps=deps)
        glow = _matmul(hn, w_tail, name=tag + "_in_tail")
        wgu = jnp.pad(sm['gla_w_gate_up'][0], ((0, LANE - GLA_GATE_RANK), (0, 0)))
        glogit = _matmul(glow, wgu, name=tag + "_gate_up")
        bias = sm['gla_gate_bias']
        gain = sm['gla_norm_gain'][0][:, None, :]
        og, st = _gla_fwd(proj, glogit, bias, gain, name=tag + "_gla_fwd")
        return og, dict(proj=proj, glow=glow, glogit=glogit, wgu=wgu, bias=bias, gain=gain, st=st,
                        w_main=w_main, w_tail=w_tail)
    proj = _matmul(hn, w_in, name=tag + "_in", deps=deps)
    args = (proj, sm['lru_conv_w'][0], sm['lru_conv_b'], sm['lru_w_rgate'][0], sm['lru_b_rgate'][0][:, None, :],
            sm['lru_w_igate'][0], sm['lru_b_igate'][0][:, None, :], sm['lru_lambda'])
    og, hs = _lru_fwd(*args, name=tag + "_lru_fwd")
    return og, dict(args=args, hs=hs)


def _mixer_bwd(layer, hn, w_in, dog, sv, tables, on_dw):
    tag = f"l{layer}"
    if layer == 0:
        dq, dk, dv, dg, dgain = _ret_bwd(sv['proj'], sv['gain'], tables, sv['st'], dog, name=tag + "_ret_bwd")
        dproj = jnp.concatenate([dq, dk, dv, dg], axis=1)
        deps = on_dw(_matmul(hn, dproj, ta=True, out_dtype=BF16, name=tag + "_in_dw"))
        dhn = _matmul(dproj, w_in, tb=True, name=tag + "_in_dx", deps=deps)
        return dhn, {'ret_gn_gain': dgain[:, 0][None]}
    if layer == 1:
        dq, dk, dv, dz, dgates, dscal, dgain = _gdn_bwd(
            sv['act'], sv['proj'], sv['gates'], sv['a_log'], sv['dt_bias'], sv['gain'], sv['st'], sv['solved'], dog,
            name=tag + "_gdn_bwd")
        dact = jnp.concatenate([dq, dk, dv], axis=1)
        dqkv, dconv = _conv_silu_bwd(sv['proj'], sv['conv_w'], dact, name=tag + "_conv_bwd")
        dmain = jnp.concatenate([dqkv, dz], axis=1)
        T = dmain.shape[0]
        dtail = _pad_cols(jnp.transpose(dgates[:, :, :2], (1, 2, 0)).reshape(T, GDN_TAIL))
        dw_main = _matmul(hn, dmain, ta=True, out_dtype=BF16, name=tag + "_in_dw")
        dw_tail = _matmul(hn, dtail, ta=True, out_dtype=BF16, name=tag + "_in_tail_dw")
        deps = on_dw(jnp.concatenate([dw_main, dw_tail[:, :GDN_TAIL]], axis=1))
        dhn = _matmul(dmain, sv['w_main'], tb=True, b_cols=GDN_MAIN, name=tag + "_in_dx", deps=deps)
        dhn = _matmul(dtail, sv['w_tail'], tb=True, epi="add", extra=dhn, name=tag + "_in_tail_dx")
        small = {'gdn_conv_w': dconv[None], 'gdn_a_log': dscal[:, 0, 0][None], 'gdn_dt_bias': dscal[:, 0, 1][None],
                 'gdn_norm_gain': jnp.sum(dgain[:, 0], axis=0)[None]}
        return dhn, small
    if layer == 2:
        dq, dk, dv, dr, dgl, dgain = _gla_bwd(sv['proj'], sv['glogit'], sv['bias'], sv['gain'], sv['st'], dog,
                                              name=tag + "_gla_bwd")
        dmain = jnp.concatenate([dq, dk, dv, dr], axis=1)
        dglow = _matmul(dgl, sv['wgu'], tb=True, name=tag + "_gate_up_dx")
        dwgu = _matmul(sv['glow'], dgl, ta=True, name=tag + "_gate_up_dw")
        dbias = _colsum(dgl, name=tag + "_gate_bias")
        dw_main = _matmul(hn, dmain, ta=True, out_dtype=BF16, name=tag + "_in_dw")
        dw_tail = _matmul(hn, dglow, ta=True, out_dtype=BF16, name=tag + "_in_tail_dw")
        deps = on_dw(jnp.concatenate([dw_main, dw_tail[:, :GLA_GATE_RANK]], axis=1))
        dhn = _matmul(dmain, sv['w_main'], tb=True, b_cols=GLA_MAIN, name=tag + "_in_dx", deps=deps)
        dhn = _matmul(dglow, sv['w_tail'], tb=True, epi="add", extra=dhn, name=tag + "_in_tail_dx")
        small = {'gla_w_gate_up': dwgu[:GLA_GATE_RANK][None], 'gla_gate_bias': dbias,
                 'gla_norm_gain': dgain[:, 0][None]}
        return dhn, small
    dxb, dyb, dcw, dcb, dwr, dbr, dwi, dbi, dlam = _lru_bwd(*sv['args'], sv['hs'], dog, name=tag + "_lru_bwd")
    dproj = jnp.concatenate([dxb, dyb], axis=1)
    deps = on_dw(_matmul(hn, dproj, ta=True, out_dtype=BF16, name=tag + "_in_dw"))
    dhn = _matmul(dproj, w_in, tb=True, name=tag + "_in_dx", deps=deps)
    small = {'lru_conv_w': dcw[None], 'lru_conv_b': dcb, 'lru_w_rgate': dwr[None], 'lru_b_rgate': dbr[:, 0][None],
             'lru_w_igate': dwi[None], 'lru_b_igate': dbi[:, 0][None], 'lru_lambda': dlam}
    return dhn, small


def _step(*args):
    assert len(args) == len(ARG_NAMES)
    p = dict(zip(ARG_NAMES, args))
    xi, yi, ci = _mesh_pos()
    dev = 4 * xi + 2 * yi + ci
    device = dev.astype(jnp.int32).reshape(1)
    core = ci.astype(jnp.int32).reshape(1)
    chip = (2 * xi + yi).astype(jnp.int32).reshape(1)
    x = p['x'][0]
    target = p['loss_target'][0]
    T = x.shape[0]
    tables = _ret_tables(T)

    sharded_small = [n for n in SMALL_NAMES if SMALL[n]]
    gathered, = _all_gather([_pack([p[n] for n in sharded_small])], name="gather_small")
    def in_cols_of(w):
        return w.shape[1] % LANE == 0

    gathers = {}
    token = gathered
    for layer in range(DEPTH):
        w_in_shard = p[MIXER_IN[layer]][0]
        groups = {'a': [(w_in_shard, in_cols_of(w_in_shard), _full_cols)],
                  'b': [(p[MIXER_OUT[layer]][0], False, _full_rows)],
                  'c': [(p['mlp_w_up'][layer], True, None), (p['mlp_w_down'][layer], False, _full_rows)]}
        for key, members in groups.items():
            shards = [w.astype(BF16) for w, _, _ in members]
            in_cols = [ic for _, ic, _ in members]
            n = len(shards)
            lands = [_insert_block(lax.empty((s.shape[0], N_DEV * s.shape[1]) if ic else (N_DEV,) + s.shape, BF16),
                                   s, device, ic, name=f"own_l{layer}{key}{i}")
                     for i, (s, ic) in enumerate(zip(shards, in_cols))]
            send, recv, bufs, token = _split_start(shards + lands, _plan_gather_a(in_cols), 4 * n,
                                                   name=f"gather_a_start_l{layer}{key}", deps=(token,))
            gathers[layer, key] = dict(n=n, h=(send, recv, bufs), in_cols=in_cols, full_of=[f for _, _, f in members])

    def gather_forward(layer, key, after):
        g = gathers[layer, key]
        n = g['n']
        send, recv, bufs = g['h']
        bufs = _split_wait(send, recv, bufs, _plan_gather_a(g['in_cols']), after, name=f"gather_a_wait_l{layer}{key}")
        send, recv, lands, tok = _split_start(bufs[n:], _plan_gather_b(g['in_cols']), 3 * n,
                                              name=f"gather_b_start_l{layer}{key}")
        g['h'] = (send, recv, lands)
        return tok

    def gather_finish(layer, key, after):
        g = gathers[layer, key]
        send, recv, lands = g['h']
        lands = _split_wait(send, recv, lands, _plan_gather_b(g['in_cols']), after, name=f"gather_b_wait_l{layer}{key}")
        return [l if ic else full_of(l) for l, ic, full_of in zip(lands, g['in_cols'], g['full_of'])]

    w_in_next, = gather_finish(0, 'a', gather_forward(0, 'a', token))
    parts = _unpack(gathered, [p[n].shape for n in sharded_small], lead=(N_DEV,))
    sm = {n: p[n] for n in SMALL_NAMES if not SMALL[n]}
    for n, blk in zip(sharded_small, parts):
        full = jnp.moveaxis(blk, 0, -2)
        sm[n] = full.reshape(full.shape[:-2] + (N_DEV * full.shape[-1],))

    saved = []
    big = {}
    for layer in range(DEPTH):
        w_in = w_in_next
        tag = f"l{layer}"
        hn = _rmsnorm_fwd(x, sm['norm1'][layer][None], name=tag + "_norm1")
        og, sv = _mixer_fwd(layer, hn, w_in, sm, tables)
        tok_b = gather_forward(layer, 'b', og)
        tok_c = gather_forward(layer, 'c', tok_b)
        w_out, = gather_finish(layer, 'b', tok_c)
        x_mid = _matmul(og, w_out, epi="add", extra=x, name=tag + "_out")
        hn2 = _rmsnorm_fwd(x_mid, sm['norm2'][layer][None], name=tag + "_norm2")
        w_up, w_down = gather_finish(layer, 'c', hn2)
        big[layer] = (w_in, w_out, w_up, w_down)
        u, a = _matmul(hn2, w_up, epi="relu2", name=tag + "_up")
        deps = (gather_forward(layer + 1, 'a', u),) if layer + 1 < DEPTH else ()
        x_new = _matmul(a, w_down, epi="add", extra=x_mid, name=tag + "_down", deps=deps)
        if layer + 1 < DEPTH:
            w_in_next, = gather_finish(layer + 1, 'a', x_new)
        saved.append(dict(x=x, hn=hn, og=og, sv=sv, x_mid=x_mid, hn2=hn2, u=u, a=a))
        x = x_new
    dx, dxb, dfinal, loss_part = _final_loss_bwd(x, sm['final_norm'][None], target, name="final_loss_bwd")
    loss = lax.psum(loss_part[0, 0], ("x", "y", "c"))

    outs = {}
    small_grads = {'final_norm': dfinal[0]}
    dnorm1, dnorm2 = [None] * DEPTH, [None] * DEPTH

    def scatter_start(items, deps, tag):
        grads, in_cols = [], []
        for _, _, dw, on in items:
            ic = on == 'cols' and (dw.shape[1] // N_DEV) % LANE == 0
            in_cols.append(ic)
            grads.append(dw if ic else (_blocks_cols(dw) if on == 'cols' else _blocks_rows(dw)))
        n = len(grads)
        lands = [lax.empty((N_CHIPS, dw.shape[0], dw.shape[1] // N_DEV) if ic else (N_CHIPS,) + dw.shape[1:], BF16)
                 for dw, ic in zip(grads, in_cols)]
        send, recv, bufs, tok = _split_start(grads + lands, _plan_scatter_pair(in_cols), N_CHIPS * n,
                                             name=f"scatter_pair_start_{tag}", deps=deps)
        return dict(items=items, n=n, tag=tag, in_cols=in_cols, h=(send, recv, bufs)), tok

    def scatter_forward(g, after):
        n, tag = g['n'], g['tag']
        send, recv, bufs = g['h']
        bufs = _split_wait(send, recv, bufs, _plan_scatter_pair(g['in_cols']), after, name=f"scatter_pair_wait_{tag}")
        sums = [_pair_add(b, a_, core, ic, name=f"pair_add_{tag}_{i}")
                for i, (b, a_, ic) in enumerate(zip(bufs[:n], bufs[n:], g['in_cols']))]
        lands = [lax.empty((N_CHIPS - 1,) + s_.shape[1:], BF16) for s_ in sums]
        send, recv, bufs, tok = _split_start(sums + lands, _plan_scatter_chips(n), (N_CHIPS - 1) * n,
                                             name=f"scatter_chips_start_{tag}")
        g['h'] = (send, recv, bufs)
        return tok

    def scatter_finish(g, after):
        n, tag = g['n'], g['tag']
        send, recv, bufs = g['h']
        bufs = _split_wait(send, recv, bufs, _plan_scatter_chips(n), after, name=f"scatter_chips_wait_{tag}")
        for i, (wname, idx, _, _) in enumerate(g['items']):
            outs[wname] = _adamw_sharded(p[wname], p['m_' + wname], p['v_' + wname], bufs[i], bufs[n + i], chip,
                                         idx, outs.get(wname), name=f"adamw_{tag}_{i}")

    older = []
    next_deps = ()
    for layer in reversed(range(DEPTH)):
        w_in, w_out, w_up, w_down = big[layer]
        s = saved[layer]
        tag = f"l{layer}"
        du = _matmul(dxb, w_down, tb=True, epi="drelu2", extra=s['u'], out_dtype=BF16, name=tag + "_down_dx",
                     deps=next_deps)
        next_deps = ()
        dw_down = _matmul(s['a'], dxb, ta=True, out_dtype=BF16, name=tag + "_down_dw")
        dhn2 = _matmul(du, w_up, tb=True, name=tag + "_up_dx")
        dw_up = _matmul(s['hn2'], du, ta=True, out_dtype=BF16, name=tag + "_up_dw")
        mlp_group, tok = scatter_start([('mlp_w_up', layer, dw_up, 'cols'),
                                        ('mlp_w_down', layer, dw_down, 'rows')], (), f"mlp_l{layer}")
        dx, dxb, dn2 = _rmsnorm_bwd(s['x_mid'], sm['norm2'][layer][None], dhn2, dx, name=tag + "_norm2_bwd",
                                    deps=(tok,))
        dog = _matmul(dxb, w_out, tb=True, name=tag + "_out_dx")
        tok = scatter_forward(mlp_group, dog)
        dw_out = _matmul(s['og'], dxb, ta=True, out_dtype=BF16, name=tag + "_out_dw", deps=(tok,))
        started = []

        def on_dw(dw_in):
            group, tok_ = scatter_start([(MIXER_IN[layer], 0, dw_in, 'cols'),
                                         (MIXER_OUT[layer], 0, dw_out, 'rows')], (), f"mix_l{layer}")
            started.append(group)
            return (tok_,)

        dhn, sg = _mixer_bwd(layer, s['hn'], w_in, dog, s['sv'], tables, on_dw)
        mixer_group, = started
        small_grads.update(sg)
        if layer > 0:
            tok = scatter_forward(mixer_group, dhn)
            dx, dxb, dn1 = _rmsnorm_bwd(s['x'], sm['norm1'][layer][None], dhn, dx, name=tag + "_norm1_bwd", deps=(tok,))
        else:
            dx, dxb, dn1 = _rmsnorm_bwd(s['x'], sm['norm1'][layer][None], dhn, dx, name=tag + "_norm1_bwd")
        dnorm1[layer], dnorm2[layer] = dn1[0], dn2[0]
        for g in older:
            scatter_finish(g, dx)
        older = [mlp_group, mixer_group]
        if layer == 1:
            next_deps = (gather_forward('small_grads', 'early', dx),)
        if layer == 2:
            parts = [_pack([small_grads[n] for n in EARLY_SMALL])] + [small_grads[n].reshape(-1, LANE) for n in MEDIUM]
            lands = [_insert_block(lax.empty((N_DEV,) + s_.shape, F32), s_, device, False, name=f"own_small_grads{i}")
                     for i, s_ in enumerate(parts)]
            n_parts = len(parts)
            send, recv, bufs, _ = _split_start(parts + lands, _plan_gather_a([False] * n_parts), 4 * n_parts,
                                               name="gather_a_start_small_grads", deps=(dx,))
            gathers['small_grads', 'early'] = dict(n=n_parts, h=(send, recv, bufs), in_cols=[False] * n_parts,
                                                   full_of=[lambda l: l] * n_parts)
    small_grads['norm1'] = jnp.stack(dnorm1)
    small_grads['norm2'] = jnp.stack(dnorm2)

    late_parts, = _all_gather([_pack([small_grads[n] for n in LATE_SMALL])], name="gather_small_grads")
    last_token = scatter_forward(mixer_group, late_parts)
    early_parts, *medium_parts = gather_finish('small_grads', 'early', last_token)
    early_sum = _sum_parts(early_parts, name="sum_small_grads_early", deps=(last_token,))
    late_sum = _sum_parts(late_parts, name="sum_small_grads_late")
    for g in older:
        scatter_finish(g, late_sum)
    for n, part in zip(MEDIUM, medium_parts):
        g = _sum_parts(part, name=f"sum_{n}")
        res = _adamw(p[n].reshape(-1, LANE), p['m_' + n].reshape(-1, LANE), p['v_' + n].reshape(-1, LANE), [g],
                     name=f"adamw_{n}")
        outs[n] = tuple(r.reshape(p[n].shape) for r in res)
    by_name = dict(zip(EARLY_SMALL, _unpack(early_sum, [small_grads[n].shape for n in EARLY_SMALL])))
    by_name.update(zip(LATE_SMALL, _unpack(late_sum, [small_grads[n].shape for n in LATE_SMALL])))
    packed_names = EARLY_SMALL + LATE_SMALL
    local_g = []
    for n in packed_names:
        g = by_name[n]
        if SMALL[n]:
            width = p[n].shape[-1]
            g = lax.dynamic_slice_in_dim(g, dev * width, width, axis=g.ndim - 1)
        local_g.append(g.reshape(p[n].shape))
    res = _adamw(_pack([p[n] for n in packed_names]), _pack([p['m_' + n] for n in packed_names]),
                 _pack([p['v_' + n] for n in packed_names]), [_pack(local_g)], name="adamw_small")
    local_shapes = [p[n].shape for n in packed_names]
    unpacked = [_unpack(r, local_shapes) for r in res]
    for i, n in enumerate(packed_names):
        outs[n] = tuple(unpacked[k][i] for k in range(4))

    result = [loss, dx[None]]
    for k in range(4):
        result += [outs[n][k] for n in WEIGHT_NAMES]
    return tuple(result)


def kernel(x, norm1, norm2, final_norm, ret_w_in, ret_gn_gain, ret_w_out, gdn_w_in, gdn_conv_w, gdn_a_log, gdn_dt_bias, gdn_norm_gain, gdn_w_out, gla_w_in, gla_w_gate_up, gla_gate_bias, gla_norm_gain, gla_w_out, lru_w_in, lru_conv_w, lru_conv_b, lru_w_rgate, lru_b_rgate, lru_w_igate, lru_b_igate, lru_lambda, lru_w_out, mlp_w_up, mlp_w_down, loss_target, m_norm1, m_norm2, m_final_norm, m_ret_w_in, m_ret_gn_gain, m_ret_w_out, m_gdn_w_in, m_gdn_conv_w, m_gdn_a_log, m_gdn_dt_bias, m_gdn_norm_gain, m_gdn_w_out, m_gla_w_in, m_gla_w_gate_up, m_gla_gate_bias, m_gla_norm_gain, m_gla_w_out, m_lru_w_in, m_lru_conv_w, m_lru_conv_b, m_lru_w_rgate, m_lru_b_rgate, m_lru_w_igate, m_lru_b_igate, m_lru_lambda, m_lru_w_out, m_mlp_w_up, m_mlp_w_down, v_norm1, v_norm2, v_final_norm, v_ret_w_in, v_ret_gn_gain, v_ret_w_out, v_gdn_w_in, v_gdn_conv_w, v_gdn_a_log, v_gdn_dt_bias, v_gdn_norm_gain, v_gdn_w_out, v_gla_w_in, v_gla_w_gate_up, v_gla_gate_bias, v_gla_norm_gain, v_gla_w_out, v_lru_w_in, v_lru_conv_w, v_lru_conv_b, v_lru_w_rgate, v_lru_b_rgate, v_lru_w_igate, v_lru_b_igate, v_lru_lambda, v_lru_w_out, v_mlp_w_up, v_mlp_w_down):
    return _step(x, norm1, norm2, final_norm, ret_w_in, ret_gn_gain, ret_w_out, gdn_w_in, gdn_conv_w, gdn_a_log, gdn_dt_bias, gdn_norm_gain, gdn_w_out, gla_w_in, gla_w_gate_up, gla_gate_bias, gla_norm_gain, gla_w_out, lru_w_in, lru_conv_w, lru_conv_b, lru_w_rgate, lru_b_rgate, lru_w_igate, lru_b_igate, lru_lambda, lru_w_out, mlp_w_up, mlp_w_down, loss_target, m_norm1, m_norm2, m_final_norm, m_ret_w_in, m_ret_gn_gain, m_ret_w_out, m_gdn_w_in, m_gdn_conv_w, m_gdn_a_log, m_gdn_dt_bias, m_gdn_norm_gain, m_gdn_w_out, m_gla_w_in, m_gla_w_gate_up, m_gla_gate_bias, m_gla_norm_gain, m_gla_w_out, m_lru_w_in, m_lru_conv_w, m_lru_conv_b, m_lru_w_rgate, m_lru_b_rgate, m_lru_w_igate, m_lru_b_igate, m_lru_lambda, m_lru_w_out, m_mlp_w_up, m_mlp_w_down, v_norm1, v_norm2, v_final_norm, v_ret_w_in, v_ret_gn_gain, v_ret_w_out, v_gdn_w_in, v_gdn_conv_w, v_gdn_a_log, v_gdn_dt_bias, v_gdn_norm_gain, v_gdn_w_out, v_gla_w_in, v_gla_w_gate_up, v_gla_gate_bias, v_gla_norm_gain, v_gla_w_out, v_lru_w_in, v_lru_conv_w, v_lru_conv_b, v_lru_w_rgate, v_lru_b_rgate, v_lru_w_igate, v_lru_b_igate, v_lru_lambda, v_lru_w_out, v_mlp_w_up, v_mlp_w_down)
```

```python
import functools
import math

import numpy as np
import jax
import jax.numpy as jnp
from jax import lax
from jax.experimental import pallas as pl
from jax.experimental.pallas import tpu as pltpu

F32 = jnp.float32
BF16 = jnp.bfloat16

D_MODEL = 2048
DEPTH = 4
CHUNK = 64
D_FF = 4 * D_MODEL
NORM_EPS = 1e-6
N_DEV = 8

RET_HEADS, RET_DK, RET_DV = 8, 256, 512
RET_HB = 4
GDN_HEADS, GDN_DK, GDN_DV = 16, 128, 128
GDN_HB = 8
GDN_QKV = GDN_HEADS * (2 * GDN_DK + GDN_DV)
CONV_WIDTH = 4
GLA_HEADS, GLA_DK, GLA_DV = 4, 256, 512
GLA_HB = 4
GLA_GATE_RANK = 16
GLA_TAU = 16.0
LRU_WIDTH, LRU_BLOCKS, LRU_BLOCK = 2048, 16, 128
LRU_C = 8.0
ROPE_BASE = 10000.0

ADAM_LR, ADAM_B1, ADAM_B2, ADAM_EPS, ADAM_WD, ADAM_STEP = 0.001, 0.9, 0.999, 1e-08, 0.01, 10

LANE = 128
VMEM_LIMIT = 48 * 1024 * 1024

NN = (((1,), (0,)), ((), ()))
NT = (((1,), (1,)), ((), ()))
TN = (((0,), (0,)), ((), ()))


def _params(sem=None):
    return pltpu.CompilerParams(dimension_semantics=sem, vmem_limit_bytes=VMEM_LIMIT)


def _bdot(a, b, dn=NN):
    return lax.dot_general(a.astype(BF16), b.astype(BF16), dn, preferred_element_type=F32)


def _split(x):
    hi = x.astype(BF16)
    lo = (x - hi.astype(F32)).astype(BF16)
    return hi, lo


def _fdot(a, b, dn=NN):
    a1, a2 = _split(a)
    b1, b2 = _split(b)
    d = functools.partial(lax.dot_general, dimension_numbers=dn, preferred_element_type=F32)
    return d(a1, b1) + (d(a1, b2) + d(a2, b1))


def _sigmoid(x):
    return 1.0 / (1.0 + jnp.exp(-x))


def _softplus(x):
    return jnp.maximum(x, 0.0) + jnp.log(1.0 + jnp.exp(-jnp.abs(x)))


def _silu(x):
    return x * _sigmoid(x)


def _dsilu(x):
    s = _sigmoid(x)
    return s * (1.0 + x * (1.0 - s))


GELU_C = math.sqrt(2.0 / math.pi)


def _gelu(x):
    return 0.5 * x * (1.0 + jnp.tanh(GELU_C * (x + 0.044715 * x * x * x)))


def _dgelu(x):
    t = jnp.tanh(GELU_C * (x + 0.044715 * x * x * x))
    return 0.5 * (1.0 + t) + 0.5 * x * (1.0 - t * t) * GELU_C * (1.0 + 3.0 * 0.044715 * x * x)


def _expm1(x):
    poly = x * (1.0 + x * 0.5 * (1.0 + x * (1.0 / 3.0) * (1.0 + x * 0.25 * (1.0 + x * 0.2))))
    return jnp.where(jnp.abs(x) < 0.05, poly, jnp.exp(x) - 1.0)


def _iota2(shape, axis):
    return lax.broadcasted_iota(jnp.int32, shape, axis)


def _col_to_row(col):
    n = col.shape[0]
    eye = _iota2((n, n), 0) == _iota2((n, n), 1)
    return jnp.sum(jnp.where(eye, col, 0.0), axis=0, keepdims=True)


def _row_to_col(row):
    n = row.shape[1]
    eye = _iota2((n, n), 0) == _iota2((n, n), 1)
    return jnp.sum(jnp.where(eye, row, 0.0), axis=1, keepdims=True)


def _pick_row(x, r):
    rows = _iota2(x.shape, 0)
    return jnp.sum(jnp.where(rows == r, x, 0.0), axis=0, keepdims=True)


def _shift_down(x, s):
    if s == 0:
        return x
    y = pltpu.roll(x, s, 0)
    return jnp.where(_iota2(x.shape, 0) < s, 0.0, y)


def _shift_up(x, s):
    if s == 0:
        return x
    n = x.shape[0]
    y = pltpu.roll(x, n - s, 0)
    return jnp.where(_iota2(x.shape, 0) >= n - s, 0.0, y)


def _tile(dim, prefs):
    for p in prefs:
        if dim % p == 0:
            return p
    return dim


TOKEN_SHAPE = (8, LANE)


def _dep_specs(deps):
    return [pl.BlockSpec(TOKEN_SHAPE, lambda *_: (0, 0)) for _ in deps]


def _matmul(a, b, *, ta=False, tb=False, epi="none", extra=None, out_dtype=F32, name, deps=(), b_cols=None):
    if ta:
        K, M = a.shape
    else:
        M, K = a.shape
    if tb:
        N, K2 = b.shape[0], b_cols or b.shape[1]
    else:
        K2, N = b.shape[0], b_cols or b.shape[1]
    assert K == K2, (a.shape, b.shape, ta, tb)
    if K <= 2048:
        tk = K
        if N <= 2048:
            tm, tn = _tile(M, (512, 256, 128)), N
        else:
            tm, tn = _tile(M, (1024, 512, 256, 128)), _tile(N, (512, 256, 128))
    else:
        tm, tn, tk = (_tile(d, (1024, 512, 256, 128)) for d in (M, N, K))
    nk = K // tk
    dn = (((0 if ta else 1,), (1 if tb else 0,)), ((), ()))
    n_extra = 0 if extra is None else 1
    n_out = 2 if epi == "relu2" else 1

    def body(*refs):
        a_ref, b_ref = refs[0], refs[1]
        e_ref = refs[2] if n_extra else None
        outs = refs[2 + n_extra + len(deps):2 + n_extra + len(deps) + n_out]

        def finish(r):
            if epi == "none":
                outs[0][...] = r.astype(outs[0].dtype)
            elif epi == "add":
                outs[0][...] = (r + e_ref[...]).astype(outs[0].dtype)
            elif epi == "relu2":
                outs[0][...] = r
                p = jnp.maximum(r, 0.0)
                outs[1][...] = (p * p).astype(outs[1].dtype)
            elif epi == "drelu2":
                outs[0][...] = (r * 2.0 * jnp.maximum(e_ref[...], 0.0)).astype(outs[0].dtype)

        def product():
            return lax.dot_general(a_ref[...].astype(BF16), b_ref[...].astype(BF16), dn, preferred_element_type=F32)

        if nk == 1:
            finish(product())
            return
        acc = refs[-1]
        k = pl.program_id(2)

        @pl.when(k == 0)
        def _():
            acc[...] = jnp.zeros_like(acc)

        acc[...] += product()

        @pl.when(k == nk - 1)
        def _():
            finish(acc[...])

    a_spec = pl.BlockSpec((tk, tm), lambda i, j, k: (k, i)) if ta else pl.BlockSpec((tm, tk), lambda i, j, k: (i, k))
    b_spec = pl.BlockSpec((tn, tk), lambda i, j, k: (j, k)) if tb else pl.BlockSpec((tk, tn), lambda i, j, k: (k, j))
    o_spec = pl.BlockSpec((tm, tn), lambda i, j, k: (i, j))
    in_specs = [a_spec, b_spec] + ([o_spec] if n_extra else []) + _dep_specs(deps)
    if epi == "relu2":
        out_shape = (jax.ShapeDtypeStruct((M, N), F32), jax.ShapeDtypeStruct((M, N), BF16))
        out_specs = (o_spec, o_spec)
    else:
        out_shape = jax.ShapeDtypeStruct((M, N), out_dtype)
        out_specs = o_spec
    args = (a, b) + ((extra,) if n_extra else ()) + tuple(deps)
    return pl.pallas_call(
        body, name=name, grid=(M // tm, N // tn, nk), in_specs=in_specs, out_specs=out_specs,
        out_shape=out_shape, scratch_shapes=[pltpu.VMEM((tm, tn), F32)] if nk > 1 else [],
        compiler_params=_params(("parallel", "parallel", "arbitrary")),
    )(*args)


ROW_BLOCK = 256


def _rmsnorm_fwd(x, g, *, name, deps=()):
    T, D = x.shape
    tr = _tile(T, (ROW_BLOCK, 128, 64))

    def body(x_ref, g_ref, *rest):
        o_ref = rest[-1]
        xv = x_ref[...]
        r = lax.rsqrt(jnp.mean(xv * xv, axis=-1, keepdims=True) + NORM_EPS)
        o_ref[...] = (xv * r * g_ref[...]).astype(o_ref.dtype)

    return pl.pallas_call(
        body, name=name, grid=(T // tr,),
        in_specs=[pl.BlockSpec((tr, D), lambda i: (i, 0)), pl.BlockSpec((1, D), lambda i: (0, 0))] + _dep_specs(deps),
        out_specs=pl.BlockSpec((tr, D), lambda i: (i, 0)),
        out_shape=jax.ShapeDtypeStruct((T, D), BF16), compiler_params=_params(("parallel",)),
    )(x, g, *deps)


def _rmsnorm_bwd(x, g, dy, dres, *, name, deps=()):
    T, D = x.shape
    tr = _tile(T, (ROW_BLOCK, 128, 64))

    def body(x_ref, g_ref, dy_ref, dres_ref, *rest):
        dx_ref, dxb_ref, dg_ref = rest[len(deps):]
        i = pl.program_id(0)
        xv = x_ref[...]
        r = lax.rsqrt(jnp.mean(xv * xv, axis=-1, keepdims=True) + NORM_EPS)
        xh = xv * r
        dyv = dy_ref[...].astype(F32)
        dxh = dyv * g_ref[...]
        dx = dres_ref[...] + r * (dxh - xh * jnp.mean(dxh * xh, axis=-1, keepdims=True))
        dx_ref[...] = dx
        dxb_ref[...] = dx.astype(dxb_ref.dtype)

        @pl.when(i == 0)
        def _():
            dg_ref[...] = jnp.zeros_like(dg_ref)

        dg_ref[...] += jnp.sum(dyv * xh, axis=0, keepdims=True)

    blk = pl.BlockSpec((tr, D), lambda i: (i, 0))
    vec = pl.BlockSpec((1, D), lambda i: (0, 0))
    return pl.pallas_call(
        body, name=name, grid=(T // tr,), in_specs=[blk, vec, blk, blk] + _dep_specs(deps), out_specs=(blk, blk, vec),
        out_shape=(jax.ShapeDtypeStruct((T, D), F32), jax.ShapeDtypeStruct((T, D), BF16),
                   jax.ShapeDtypeStruct((1, D), F32)),
        compiler_params=_params(("arbitrary",)),
    )(x, g, dy, dres, *deps)


def _final_loss_bwd(x, g, target, *, name):
    T, D = x.shape
    tr = _tile(T, (ROW_BLOCK, 128, 64))

    def body(x_ref, g_ref, t_ref, dx_ref, dxb_ref, dg_ref, l_ref):
        i = pl.program_id(0)
        xv = x_ref[...]
        r = lax.rsqrt(jnp.mean(xv * xv, axis=-1, keepdims=True) + NORM_EPS)
        xh = xv * r
        gv = g_ref[...]
        err = xh * gv - t_ref[...]
        dy = err * (1.0 / D)
        dxh = dy * gv
        dx = r * (dxh - xh * jnp.mean(dxh * xh, axis=-1, keepdims=True))
        dx_ref[...] = dx
        dxb_ref[...] = dx.astype(dxb_ref.dtype)

        @pl.when(i == 0)
        def _():
            l_ref[...] = jnp.zeros_like(l_ref)
            dg_ref[...] = jnp.zeros_like(dg_ref)

        dg_ref[...] += jnp.sum(dy * xh, axis=0, keepdims=True)
        part = 0.5 * jnp.sum(jnp.mean(err * err, axis=-1, keepdims=True), axis=0, keepdims=True)
        l_ref[...] += jnp.broadcast_to(part, l_ref.shape)

    blk = pl.BlockSpec((tr, D), lambda i: (i, 0))
    vec = pl.BlockSpec((1, D), lambda i: (0, 0))
    return pl.pallas_call(
        body, name=name, grid=(T // tr,), in_specs=[blk, vec, blk],
        out_specs=(blk, blk, vec, pl.BlockSpec((1, LANE), lambda i: (0, 0))),
        out_shape=(jax.ShapeDtypeStruct((T, D), F32), jax.ShapeDtypeStruct((T, D), BF16),
                   jax.ShapeDtypeStruct((1, D), F32), jax.ShapeDtypeStruct((1, LANE), F32)),
        compiler_params=_params(("arbitrary",)),
    )(x, g, target)


def _colsum(x, *, name):
    T, C = x.shape
    tc = _tile(C, (512, 256, 128))

    def body(x_ref, o_ref):
        o_ref[...] = jnp.sum(x_ref[...], axis=0, keepdims=True)

    return pl.pallas_call(
        body, name=name, grid=(C // tc,), in_specs=[pl.BlockSpec((T, tc), lambda j: (0, j))],
        out_specs=pl.BlockSpec((1, tc), lambda j: (0, j)),
        out_shape=jax.ShapeDtypeStruct((1, C), F32), compiler_params=_params(("parallel",)),
    )(x)


ADAM_C1 = 1.0 - ADAM_B1 ** ADAM_STEP
ADAM_C2 = 1.0 - ADAM_B2 ** ADAM_STEP


def _adamw(w, m, v, grads, *, name):
    R, C = w.shape
    tr = _tile(R, (256, 128, 64, 32, 16, 8))
    n_g = len(grads)

    def body(*refs):
        w_ref, m_ref, v_ref = refs[:3]
        g_refs = refs[3:3 + n_g]
        g_out, d_out, m_out, v_out = refs[3 + n_g:]
        g = g_refs[0][...].astype(F32)
        for r in g_refs[1:]:
            g = g + r[...].astype(F32)
        mn = ADAM_B1 * m_ref[...] + (1.0 - ADAM_B1) * g
        vn = ADAM_B2 * v_ref[...] + (1.0 - ADAM_B2) * (g * g)
        m_hat = mn / ADAM_C1
        v_hat = vn / ADAM_C2
        g_out[...] = g
        d_out[...] = -ADAM_LR * (m_hat / (jnp.sqrt(v_hat) + ADAM_EPS) + ADAM_WD * w_ref[...])
        m_out[...] = mn
        v_out[...] = vn

    blk = pl.BlockSpec((tr, C), lambda i: (i, 0))
    sh = jax.ShapeDtypeStruct((R, C), F32)
    return pl.pallas_call(
        body, name=name, grid=(R // tr,), in_specs=[blk] * (3 + n_g), out_specs=(blk,) * 4,
        out_shape=(sh,) * 4, compiler_params=_params(("parallel",)),
    )(w, m, v, *grads)


def _sum_parts(parts, *, name, deps=()):
    P, R, C = parts.shape
    tr = _tile(R, (256, 128, 64, 32, 16, 8))

    def body(p_ref, *rest):
        o_ref = rest[-1]
        s = p_ref[0].astype(F32)
        for i in range(1, P):
            s = s + p_ref[i].astype(F32)
        o_ref[...] = s

    return pl.pallas_call(
        body, name=name, grid=(R // tr,),
        in_specs=[pl.BlockSpec((P, tr, C), lambda i: (0, i, 0))] + _dep_specs(deps),
        out_specs=pl.BlockSpec((tr, C), lambda i: (i, 0)),
        out_shape=jax.ShapeDtypeStruct((R, C), F32), compiler_params=_params(("parallel",)),
    )(parts, *deps)


def _ret_tables(T):
    H, C = RET_HEADS, CHUNK
    log_gamma = jnp.log1p(-jnp.exp2(-5.0 - jnp.arange(H, dtype=F32)))
    pos = jnp.arange(C, dtype=F32)
    dist = jnp.abs(pos[:, None] - pos[None, :])
    dm = jnp.exp(log_gamma[:, None, None] * dist)
    qdec = jnp.exp(log_gamma[:, None] * (pos + 1.0))[:, :, None]
    kdec = jnp.exp(log_gamma[:, None] * (C - 1.0 - pos))[:, :, None]
    cdec = jnp.exp(log_gamma * C)[:, None, None]
    inv = ROPE_BASE ** (-jnp.arange(0, RET_DK, 2, dtype=F32) / RET_DK)
    ang = jnp.arange(T, dtype=F32)[:, None] * inv[None, :]
    return dm, qdec, kdec, cdec, jnp.cos(ang), jnp.sin(ang)


def _rot(x, cos, sin):
    h = x.shape[1] // 2
    x1, x2 = x[:, :h], x[:, h:]
    return jnp.concatenate([x1 * cos - x2 * sin, x1 * sin + x2 * cos], axis=1)


def _unrot(dy, cos, sin):
    h = dy.shape[1] // 2
    d1, d2 = dy[:, :h], dy[:, h:]
    return jnp.concatenate([d1 * cos + d2 * sin, d2 * cos - d1 * sin], axis=1)


def _ret_specs(N, rev):
    H, C, DK, DV = RET_HEADS, CHUNK, RET_DK, RET_DV
    cn = (lambda n: N - 1 - n) if rev else (lambda n: n)
    HB, G = RET_HB, H // RET_HB
    q = pl.BlockSpec((C, HB * DK), lambda h, n: (cn(n), h))
    k = pl.BlockSpec((C, HB * DK), lambda h, n: (cn(n), G + h))
    v = pl.BlockSpec((C, HB * DV), lambda h, n: (cn(n), G + h))
    g = pl.BlockSpec((C, HB * DV), lambda h, n: (cn(n), 2 * G + h))
    cs = pl.BlockSpec((C, DK // 2), lambda h, n: (cn(n), 0))
    dm = pl.BlockSpec((HB, C, C), lambda h, n: (h, 0, 0))
    dec = pl.BlockSpec((HB, C, 1), lambda h, n: (h, 0, 0))
    cd = pl.BlockSpec((HB, 1, 1), lambda h, n: (h, 0, 0))
    gain = pl.BlockSpec((HB, 1, DV), lambda h, n: (h, 0, 0))
    st = pl.BlockSpec((HB, 1, DK, DV), lambda h, n: (h, cn(n), 0, 0))
    ov = pl.BlockSpec((C, HB * DV), lambda h, n: (cn(n), h))
    return q, k, v, g, cs, dm, dec, cd, gain, st, ov


def _ret_fwd(proj, gain, tables, *, name):
    T = proj.shape[0]
    H, C, DK, DV = RET_HEADS, CHUNK, RET_DK, RET_DV
    N = T // C
    dm_t, qdec_t, kdec_t, cdec_t, cos_t, sin_t = tables

    def body(q_ref, k_ref, v_ref, g_ref, cos_ref, sin_ref, dm_ref, qd_ref, kd_ref, cd_ref, gain_ref,
             og_ref, st_ref, s_acc):
        n = pl.program_id(1)

        @pl.when(n == 0)
        def _():
            s_acc[...] = jnp.zeros_like(s_acc)

        cos, sin = cos_ref[...], sin_ref[...]

        def head(hb):
            kc, vc = pl.ds(hb * DK, DK), pl.ds(hb * DV, DV)
            qr = _rot(q_ref[:, kc], cos, sin)
            kr = _rot(k_ref[:, kc], cos, sin) * (DK ** -0.5)
            v = v_ref[:, vc]
            sp = s_acc[hb]
            st_ref[hb, 0] = sp.astype(st_ref.dtype)
            scores = _bdot(qr, kr, NT) * dm_ref[hb]
            inter = _bdot(qr * qd_ref[hb], sp)
            s_acc[hb] = sp * cd_ref[hb] + _bdot(kr * kd_ref[hb], v, TN)
            yield
            o = _bdot(scores, v) + inter
            yield
            oc = o - jnp.mean(o, axis=-1, keepdims=True)
            rstd = lax.rsqrt(jnp.mean(oc * oc, axis=-1, keepdims=True) + NORM_EPS)
            og_ref[:, vc] = (oc * rstd * gain_ref[hb] * _silu(g_ref[:, vc])).astype(og_ref.dtype)

        _interleave([head(hb) for hb in range(RET_HB)])

    q, k, v, g, cs, dm, dec, cd, gn, st, ov = _ret_specs(N, False)
    return pl.pallas_call(
        body, name=name, grid=(H // RET_HB, N),
        in_specs=[q, k, v, g, cs, cs, dm, dec, dec, cd, gn], out_specs=(ov, st),
        out_shape=(jax.ShapeDtypeStruct((T, H * DV), BF16), jax.ShapeDtypeStruct((H, N, DK, DV), BF16)),
        scratch_shapes=[pltpu.VMEM((RET_HB, DK, DV), F32)],
        compiler_params=_params(("arbitrary", "arbitrary")),
    )(proj, proj, proj, proj, cos_t, sin_t, dm_t, qdec_t, kdec_t, cdec_t, gain)


def _ret_bwd(proj, gain, tables, states, dog, *, name):
    T = proj.shape[0]
    H, C, DK, DV = RET_HEADS, CHUNK, RET_DK, RET_DV
    N = T // C
    dm_t, qdec_t, kdec_t, cdec_t, cos_t, sin_t = tables

    def body(q_ref, k_ref, v_ref, g_ref, cos_ref, sin_ref, dm_ref, qd_ref, kd_ref, cd_ref, gain_ref,
             st_ref, dog_ref, dq_ref, dk_ref, dv_ref, dg_ref, dgain_ref, ds_acc):
        n = pl.program_id(1)

        @pl.when(n == 0)
        def _():
            ds_acc[...] = jnp.zeros_like(ds_acc)
            dgain_ref[...] = jnp.zeros_like(dgain_ref)

        cos, sin = cos_ref[...], sin_ref[...]
        scale = DK ** -0.5

        def head(hb):
            kc, vc = pl.ds(hb * DK, DK), pl.ds(hb * DV, DV)
            qr = _rot(q_ref[:, kc], cos, sin)
            kr = _rot(k_ref[:, kc], cos, sin) * scale
            v = v_ref[:, vc]
            g = g_ref[:, vc]
            sp = st_ref[hb, 0]
            dm = dm_ref[hb]
            qd, kd = qd_ref[hb], kd_ref[hb]
            gain_v = gain_ref[hb]
            scores = _bdot(qr, kr, NT) * dm
            inter = _bdot(qr * qd, sp)
            yield
            o = _bdot(scores, v) + inter
            yield
            oc = o - jnp.mean(o, axis=-1, keepdims=True)
            rstd = lax.rsqrt(jnp.mean(oc * oc, axis=-1, keepdims=True) + NORM_EPS)
            oh = oc * rstd
            dy = dog_ref[:, vc].astype(F32)
            dg_ref[:, vc] = (dy * oh * gain_v * _dsilu(g)).astype(dg_ref.dtype)
            dnorm = dy * _silu(g)
            dgain_ref[hb] += jnp.sum(dnorm * oh, axis=0, keepdims=True)
            doh = dnorm * gain_v
            do = rstd * (doh - jnp.mean(doh, axis=-1, keepdims=True)
                         - oh * jnp.mean(doh * oh, axis=-1, keepdims=True))
            dsn = ds_acc[hb]
            dp = _bdot(do, v, NT) * dm
            dq_inter = _bdot(do, sp, NT) * qd
            dk_inter = _bdot(v, dsn, NT) * kd
            dv_ref[:, vc] = (_bdot(scores, do, TN) + _bdot(kr * kd, dsn)).astype(dv_ref.dtype)
            ds_acc[hb] = dsn * cd_ref[hb] + _bdot(qr * qd, do, TN)
            yield
            dqr = _bdot(dp, kr) + dq_inter
            dkr = _bdot(dp, qr, TN) + dk_inter
            yield
            dq_ref[:, kc] = _unrot(dqr, cos, sin).astype(dq_ref.dtype)
            dk_ref[:, kc] = _unrot(dkr * scale, cos, sin).astype(dk_ref.dtype)

        _interleave([head(hb) for hb in range(RET_HB)])

    q, k, v, g, cs, dm, dec, cd, gn, st, ov = _ret_specs(N, True)
    return pl.pallas_call(
        body, name=name, grid=(H // RET_HB, N),
        in_specs=[q, k, v, g, cs, cs, dm, dec, dec, cd, gn, st, ov],
        out_specs=(q, q, ov, ov, gn),
        out_shape=(jax.ShapeDtypeStruct((T, H * DK), BF16), jax.ShapeDtypeStruct((T, H * DK), BF16),
                   jax.ShapeDtypeStruct((T, H * DV), BF16), jax.ShapeDtypeStruct((T, H * DV), BF16),
                   jax.ShapeDtypeStruct((H, 1, DV), F32)),
        scratch_shapes=[pltpu.VMEM((RET_HB, DK, DV), F32)],
        compiler_params=_params(("arbitrary", "arbitrary")),
    )(proj, proj, proj, proj, cos_t, sin_t, dm_t, qdec_t, kdec_t, cdec_t, gain, states, dog)


def _gla_specs(N, rev):
    H, C, DK, DV = GLA_HEADS, CHUNK, GLA_DK, GLA_DV
    cn = (lambda n: N - 1 - n) if rev else (lambda n: n)
    HB, G = GLA_HB, H // GLA_HB
    q = pl.BlockSpec((C, HB * DK), lambda h, n: (cn(n), h))
    k = pl.BlockSpec((C, HB * DK), lambda h, n: (cn(n), G + h))
    v = pl.BlockSpec((C, HB * DV), lambda h, n: (cn(n), G + h))
    r = pl.BlockSpec((C, HB * DV), lambda h, n: (cn(n), 2 * G + h))
    bias = pl.BlockSpec((1, HB * DK), lambda h, n: (0, h))
    gain = pl.BlockSpec((HB, 1, DV), lambda h, n: (h, 0, 0))
    st = pl.BlockSpec((HB, 1, DV, DK), lambda h, n: (h, cn(n), 0, 0))
    ov = pl.BlockSpec((C, HB * DV), lambda h, n: (cn(n), h))
    return q, k, v, r, bias, gain, st, ov


def _gla_chunk(q, k, v, gl_raw, bias):
    C, DK = q.shape
    gl = gl_raw + bias
    la = -_softplus(-gl) * (1.0 / GLA_TAU)
    lower = _iota2((C, C), 0) >= _iota2((C, C), 1)
    cum = _fdot(jnp.where(lower, 1.0, 0.0), la)
    yield
    ref = _pick_row(cum, C // 2 - 1)
    clast = _pick_row(cum, C - 1)
    fw, bw = jnp.exp(cum - ref), jnp.exp(ref - cum)
    qs = q * (DK ** -0.5)
    s_lo = _bdot(qs * fw, k * bw, NT)
    s_up = _bdot(qs * bw, k * fw, NT)
    yield
    scores = jnp.where(lower, s_lo, s_up)
    return gl, cum, clast, fw, bw, qs, k, v, scores, lower


def _gla_fwd(proj, glogit, bias, gain, *, name):
    T = proj.shape[0]
    H, C, DK, DV = GLA_HEADS, CHUNK, GLA_DK, GLA_DV
    N = T // C

    def body(q_ref, k_ref, v_ref, r_ref, gl_ref, bias_ref, gain_ref, og_ref, st_ref, s_acc):
        n = pl.program_id(1)

        @pl.when(n == 0)
        def _():
            s_acc[...] = jnp.zeros_like(s_acc)

        def head(hb):
            kc, vc = pl.ds(hb * DK, DK), pl.ds(hb * DV, DV)
            gl, cum, clast, fw, bw, qs, k, v, scores, lower = yield from _gla_chunk(
                q_ref[:, kc], k_ref[:, kc], v_ref[:, vc], gl_ref[:, kc], bias_ref[:, kc])
            sp = s_acc[hb]
            st_ref[hb, 0] = sp
            o = _bdot(scores, v) + _bdot(qs * jnp.exp(cum), sp, NT)
            s_acc[hb] = sp * jnp.exp(clast) + _bdot(v, k * jnp.exp(clast - cum), TN)
            yield
            rstd = lax.rsqrt(jnp.mean(o * o, axis=-1, keepdims=True) + NORM_EPS)
            og_ref[:, vc] = (o * rstd * gain_ref[hb] * _silu(r_ref[:, vc])).astype(og_ref.dtype)

        _interleave([head(hb) for hb in range(GLA_HB)])

    q, k, v, r, bias_s, gn, st, ov = _gla_specs(N, False)
    return pl.pallas_call(
        body, name=name, grid=(H // GLA_HB, N), in_specs=[q, k, v, r, q, bias_s, gn], out_specs=(ov, st),
        out_shape=(jax.ShapeDtypeStruct((T, H * DV), BF16), jax.ShapeDtypeStruct((H, N, DV, DK), F32)),
        scratch_shapes=[pltpu.VMEM((GLA_HB, DV, DK), F32)],
        compiler_params=_params(("arbitrary", "arbitrary")),
    )(proj, proj, proj, proj, glogit, bias, gain)


def _gla_bwd(proj, glogit, bias, gain, states, dog, *, name):
    T = proj.shape[0]
    H, C, DK, DV = GLA_HEADS, CHUNK, GLA_DK, GLA_DV
    N = T // C

    def body(q_ref, k_ref, v_ref, r_ref, gl_ref, bias_ref, gain_ref, st_ref, dog_ref,
             dq_ref, dk_ref, dv_ref, dr_ref, dgl_ref, dgain_ref, ds_acc):
        n = pl.program_id(1)

        @pl.when(n == 0)
        def _():
            ds_acc[...] = jnp.zeros_like(ds_acc)
            dgain_ref[...] = jnp.zeros_like(dgain_ref)

        def head(hb):
            kc, vc = pl.ds(hb * DK, DK), pl.ds(hb * DV, DV)
            gl, cum, clast, fw, bw, qs, k, v, scores, lower = yield from _gla_chunk(
                q_ref[:, kc], k_ref[:, kc], v_ref[:, vc], gl_ref[:, kc], bias_ref[:, kc])
            sp = st_ref[hb, 0]
            ecum, e2, cdec = jnp.exp(cum), jnp.exp(clast - cum), jnp.exp(clast)
            q_in, k_end = qs * ecum, k * e2
            o = _bdot(scores, v) + _bdot(q_in, sp, NT)
            yield
            rstd = lax.rsqrt(jnp.mean(o * o, axis=-1, keepdims=True) + NORM_EPS)
            oh = o * rstd
            r = r_ref[:, vc]
            gain_v = gain_ref[hb]
            dy = dog_ref[:, vc].astype(F32)
            dr_ref[:, vc] = (dy * oh * gain_v * _dsilu(r)).astype(dr_ref.dtype)
            dnorm = dy * _silu(r)
            dgain_ref[hb] += jnp.sum(dnorm * oh, axis=0, keepdims=True)
            doh = dnorm * gain_v
            do = rstd * (doh - oh * jnp.mean(doh * oh, axis=-1, keepdims=True))
            dsn = ds_acc[hb]
            dq_in = _bdot(do, sp)
            dk_end = _bdot(v, dsn)
            dv_ref[:, vc] = (_bdot(k_end, dsn, NT) + _bdot(scores, do, TN)).astype(dv_ref.dtype)
            dcdec = jnp.sum(dsn * sp, axis=0, keepdims=True)
            ds_acc[hb] = dsn * cdec + _bdot(do, q_in, TN)
            dsc = _bdot(do, v, NT)
            yield
            ds_lo = jnp.where(lower, dsc, 0.0)
            ds_up = jnp.where(lower, 0.0, dsc)
            qf, kb, qb, kf = qs * fw, k * bw, qs * bw, k * fw
            dqf, dkb = _bdot(ds_lo, kb), _bdot(ds_lo, qf, TN)
            dqb, dkf = _bdot(ds_up, kf), _bdot(ds_up, qb, TN)
            yield
            dq_ref[:, kc] = ((dqf * fw + dqb * bw + dq_in * ecum) * (DK ** -0.5)).astype(dq_ref.dtype)
            dk_ref[:, kc] = (dkb * bw + dkf * fw + dk_end * e2).astype(dk_ref.dtype)
            dz = (dqf * qs + dkf * k) * fw - (dqb * qs + dkb * k) * bw
            kk = dk_end * k_end
            dcum = dz + dq_in * q_in - kk
            rows = _iota2((C, DK), 0)
            dcum = dcum + jnp.where(rows == C // 2 - 1, -jnp.sum(dz, axis=0, keepdims=True), 0.0)
            dcum = dcum + jnp.where(rows == C - 1, jnp.sum(kk, axis=0, keepdims=True) + dcdec * cdec, 0.0)
            upper = _iota2((C, C), 0) <= _iota2((C, C), 1)
            dla = _fdot(jnp.where(upper, 1.0, 0.0), dcum)
            yield
            dgl_ref[:, kc] = dla * (1.0 / GLA_TAU) * _sigmoid(-gl)

        _interleave([head(hb) for hb in range(GLA_HB)])

    q, k, v, r, bias_s, gn, st, ov = _gla_specs(N, True)
    return pl.pallas_call(
        body, name=name, grid=(H // GLA_HB, N), in_specs=[q, k, v, r, q, bias_s, gn, st, ov],
        out_specs=(q, q, ov, ov, q, gn),
        out_shape=(jax.ShapeDtypeStruct((T, H * DK), BF16), jax.ShapeDtypeStruct((T, H * DK), BF16),
                   jax.ShapeDtypeStruct((T, H * DV), BF16), jax.ShapeDtypeStruct((T, H * DV), BF16),
                   jax.ShapeDtypeStruct((T, H * DK), F32), jax.ShapeDtypeStruct((H, 1, DV), F32)),
        scratch_shapes=[pltpu.VMEM((GLA_HB, DV, DK), F32)],
        compiler_params=_params(("arbitrary", "arbitrary")),
    )(proj, proj, proj, proj, glogit, bias, gain, states, dog)


def _conv(xv, w_ref):
    out = _shift_down(xv, CONV_WIDTH - 1) * w_ref[0:1, :]
    for tap in range(1, CONV_WIDTH):
        out = out + _shift_down(xv, CONV_WIDTH - 1 - tap) * w_ref[tap:tap + 1, :]
    return out


def _conv_bwd(xv, w_ref, dpre, dw_ref):
    dx = None
    for tap in range(CONV_WIDTH):
        s = CONV_WIDTH - 1 - tap
        t = _shift_up(dpre, s) * w_ref[tap:tap + 1, :]
        dx = t if dx is None else dx + t
        dw_ref[tap:tap + 1, :] = jnp.sum(dpre * _shift_down(xv, s), axis=0, keepdims=True)
    return dx


CONV_COLS = 256


def _conv_silu_fwd(x, w, *, name):
    T = x.shape[0]
    n = w.shape[1]

    def body(x_ref, w_ref, o_ref):
        o_ref[...] = _silu(_conv(x_ref[...], w_ref))

    return pl.pallas_call(
        body, name=name, grid=(n // CONV_COLS,),
        in_specs=[pl.BlockSpec((T, CONV_COLS), lambda j: (0, j)), pl.BlockSpec((CONV_WIDTH, CONV_COLS), lambda j: (0, j))],
        out_specs=pl.BlockSpec((T, CONV_COLS), lambda j: (0, j)),
        out_shape=jax.ShapeDtypeStruct((T, n), F32), compiler_params=_params(("parallel",)),
    )(x, w)


def _conv_silu_bwd(x, w, dact, *, name):
    T = x.shape[0]
    n = w.shape[1]

    def body(x_ref, w_ref, da_ref, dx_ref, dw_ref):
        xv = x_ref[...]
        dpre = da_ref[...] * _dsilu(_conv(xv, w_ref))
        dx_ref[...] = _conv_bwd(xv, w_ref, dpre, dw_ref).astype(dx_ref.dtype)

    blk = pl.BlockSpec((T, CONV_COLS), lambda j: (0, j))
    wb = pl.BlockSpec((CONV_WIDTH, CONV_COLS), lambda j: (0, j))
    return pl.pallas_call(
        body, name=name, grid=(n // CONV_COLS,), in_specs=[blk, wb, blk], out_specs=(blk, wb),
        out_shape=(jax.ShapeDtypeStruct((T, n), BF16), jax.ShapeDtypeStruct((CONV_WIDTH, n), F32)),
        compiler_params=_params(("parallel",)),
    )(x, w, dact)


def _interleave(gens):
    results = [None] * len(gens)
    live = list(range(len(gens)))
    while live:
        for i in list(live):
            try:
                next(gens[i])
            except StopIteration as done:
                results[i] = done.value
                live.remove(i)
    return results


def _unit_lower_inverse(a):
    n = a.shape[0]
    eye = jnp.where(_iota2((n, n), 0) == _iota2((n, n), 1), 1.0, 0.0)
    p = -a
    t = eye + p
    for _ in range(5):
        p = _fdot(p, p)
        yield
        t = t + _fdot(t, p)
        yield
    return t


def _gdn_specs(N, rev):
    H, C, DK, DV = GDN_HEADS, CHUNK, GDN_DK, GDN_DV
    cn = (lambda n: N - 1 - n) if rev else (lambda n: n)
    HB, G = GDN_HB, H // GDN_HB
    q = pl.BlockSpec((C, HB * DK), lambda h, n: (cn(n), h))
    k = pl.BlockSpec((C, HB * DK), lambda h, n: (cn(n), G + h))
    v = pl.BlockSpec((C, HB * DV), lambda h, n: (cn(n), 2 * G + h))
    z = pl.BlockSpec((C, HB * DV), lambda h, n: (cn(n), 3 * G + h))
    gates = pl.BlockSpec((C, LANE), lambda h, n: (cn(n), 0))
    sc = pl.BlockSpec((HB, 1, 1), lambda h, n: (h, 0, 0))
    gain = pl.BlockSpec((1, DV), lambda h, n: (0, 0))
    st = pl.BlockSpec((HB, 1, DK, DV), lambda h, n: (h, cn(n), 0, 0))
    return q, k, v, z, gates, sc, gain, st


def _gdn_solved_specs(N, rev):
    C = CHUNK
    cn = (lambda n: N - 1 - n) if rev else (lambda n: n)
    return (pl.BlockSpec((GDN_HB, 1, C, C), lambda h, n: (h, cn(n), 0, 0)),
            pl.BlockSpec((GDN_HB, 1, C, GDN_DV + GDN_DK), lambda h, n: (h, cn(n), 0, 0)))


def _gdn_chunk(q_ref, k_ref, v_ref, gates_ref, alog_ref, dtb_ref, h, solved=None):
    H, C, DK, DV = GDN_HEADS, CHUNK, GDN_DK, GDN_DV
    gates = gates_ref[...]
    lane = _iota2(gates.shape, 1)
    bl = jnp.sum(jnp.where(lane == h, gates, 0.0), axis=1, keepdims=True)
    al = jnp.sum(jnp.where(lane == H + h, gates, 0.0), axis=1, keepdims=True)
    beta = _sigmoid(bl)
    ea = jnp.exp(alog_ref[...])
    xs = al + dtb_ref[...]
    la = -ea * _softplus(xs)
    ii, jj = _iota2((C, C), 0), _iota2((C, C), 1)
    strict = ii > jj
    cum_col = jnp.sum(jnp.where(ii >= jj, _col_to_row(la), 0.0), axis=1, keepdims=True)
    cum_row = jnp.sum(jnp.where(ii <= jj, la, 0.0), axis=0, keepdims=True)
    q, k, v = q_ref[...], k_ref[...], v_ref[...]
    rq = lax.rsqrt(jnp.sum(q * q, axis=-1, keepdims=True) + NORM_EPS)
    rk = lax.rsqrt(jnp.sum(k * k, axis=-1, keepdims=True) + NORM_EPS)
    qn = q * rq * (DK ** -0.5)
    kn = k * rk
    rel = jnp.where(strict, jnp.exp(jnp.where(strict, cum_col - cum_row, 0.0)), 0.0)
    rg = rel * _bdot(kn, kn, NT)
    yield
    a = beta * rg
    e_col = jnp.exp(cum_col)
    clast = _pick_row(cum_col, C - 1)
    if solved is None:
        tm = yield from _unit_lower_inverse(a)
        rhs = jnp.concatenate([beta * v, (beta * e_col) * kn], axis=1)
        sol = _fdot(tm, rhs)
        yield
    else:
        tm, sol = solved
    u, w = sol[:, :DV], sol[:, DV:]
    dd = jnp.exp(clast - cum_col)
    ke = kn * dd
    g = jnp.exp(clast)
    eye = jnp.where(_iota2((DK, DK), 0) == _iota2((DK, DK), 1), 1.0, 0.0)
    trans = g * eye - _bdot(ke, w, TN)
    inject = _bdot(ke, u, TN)
    yield
    return dict(beta=beta, ea=ea, xs=xs, la=la, strict=strict, ii=ii, jj=jj, q=q, k=k, v=v, rq=rq, rk=rk,
                qn=qn, kn=kn, rel=rel, rg=rg, a=a, tm=tm, e_col=e_col, sol=sol, u=u, w=w, dd=dd, ke=ke,
                g=g, eye=eye, trans=trans, inject=inject)


def _gdn_fwd(act, proj, gates, a_log, dt_bias, gain, *, name):
    T = act.shape[0]
    H, C, DK, DV = GDN_HEADS, CHUNK, GDN_DK, GDN_DV
    N = T // C

    def body(q_ref, k_ref, v_ref, z_ref, gates_ref, alog_ref, dtb_ref, gain_ref, og_ref, st_ref, tm_ref, sol_ref,
             s_acc):
        hg, n = pl.program_id(0), pl.program_id(1)

        @pl.when(n == 0)
        def _():
            s_acc[...] = jnp.zeros_like(s_acc)

        def head(hb):
            cols = pl.ds(hb * DK, DK)
            c = yield from _gdn_chunk(q_ref.at[:, cols], k_ref.at[:, cols], v_ref.at[:, cols], gates_ref,
                                      alog_ref.at[hb], dtb_ref.at[hb], hg * GDN_HB + hb)
            tm_ref[hb, 0] = c["tm"]
            sol_ref[hb, 0] = c["sol"]
            sp = s_acc[hb]
            st_ref[hb, 0] = sp
            snew = _bdot(c["trans"], sp) + c["inject"]
            yield
            s_acc[hb] = snew
            o = _bdot(c["qn"], snew)
            yield
            rstd = lax.rsqrt(jnp.mean(o * o, axis=-1, keepdims=True) + NORM_EPS)
            og_ref[:, cols] = (o * rstd * gain_ref[...] * _silu(z_ref[:, cols])).astype(og_ref.dtype)

        _interleave([head(hb) for hb in range(GDN_HB)])

    q, k, v, z, gt, sc, gn, st = _gdn_specs(N, False)
    tm_s, sol_s = _gdn_solved_specs(N, False)
    return pl.pallas_call(
        body, name=name, grid=(H // GDN_HB, N), in_specs=[q, k, v, z, gt, sc, sc, gn],
        out_specs=(q, st, tm_s, sol_s),
        out_shape=(jax.ShapeDtypeStruct((T, H * DV), BF16), jax.ShapeDtypeStruct((H, N, DK, DV), F32),
                   jax.ShapeDtypeStruct((H, N, C, C), F32), jax.ShapeDtypeStruct((H, N, C, DV + DK), F32)),
        scratch_shapes=[pltpu.VMEM((GDN_HB, DK, DV), F32)],
        compiler_params=_params(("arbitrary", "arbitrary")),
    )(act, act, act, proj, gates, a_log, dt_bias, gain)


def _gdn_bwd(act, proj, gates, a_log, dt_bias, gain, states, solved, dog, *, name):
    T = act.shape[0]
    H, C, DK, DV = GDN_HEADS, CHUNK, GDN_DK, GDN_DV
    N = T // C

    def rsum(x):
        return jnp.sum(x, axis=1, keepdims=True)

    def body(q_ref, k_ref, v_ref, z_ref, gates_ref, alog_ref, dtb_ref, gain_ref, st_ref, tm_ref, sol_ref, dog_ref,
             dq_ref, dk_ref, dv_ref, dz_ref, dgates_ref, dscal_ref, dgain_ref, ds_acc):
        hg, n = pl.program_id(0), pl.program_id(1)

        @pl.when(n == 0)
        def _():
            ds_acc[...] = jnp.zeros_like(ds_acc)
            dgain_ref[...] = jnp.zeros_like(dgain_ref)
            dscal_ref[...] = jnp.zeros_like(dscal_ref)

        _interleave([one_head(hb, hg * GDN_HB + hb, q_ref, k_ref, v_ref, z_ref, gates_ref, alog_ref, dtb_ref, gain_ref,
                              st_ref, tm_ref, sol_ref, dog_ref, dq_ref, dk_ref, dv_ref, dz_ref, dgates_ref, dscal_ref,
                              dgain_ref, ds_acc)
                     for hb in range(GDN_HB)])

    def one_head(hb, h, q_ref, k_ref, v_ref, z_ref, gates_ref, alog_ref, dtb_ref, gain_ref, st_ref, tm_ref, sol_ref,
                 dog_ref, dq_ref, dk_ref, dv_ref, dz_ref, dgates_ref, dscal_ref, dgain_ref, ds_acc):
        cols = pl.ds(hb * DK, DK)
        c = yield from _gdn_chunk(q_ref.at[:, cols], k_ref.at[:, cols], v_ref.at[:, cols], gates_ref,
                                  alog_ref.at[hb], dtb_ref.at[hb], h, solved=(tm_ref[hb, 0], sol_ref[hb, 0]))
        beta, kn, qn, v, ke, u, w, dd, e_col = c["beta"], c["kn"], c["qn"], c["v"], c["ke"], c["u"], c["w"], c["dd"], c["e_col"]
        sp = st_ref[hb, 0]
        snew = _bdot(c["trans"], sp) + c["inject"]
        yield
        o = _bdot(qn, snew)
        yield
        rstd = lax.rsqrt(jnp.mean(o * o, axis=-1, keepdims=True) + NORM_EPS)
        oh = o * rstd
        z = z_ref[:, cols]
        gain_v = gain_ref[...]
        dy = dog_ref[:, cols].astype(F32)
        dz_ref[:, cols] = (dy * oh * gain_v * _dsilu(z)).astype(dz_ref.dtype)
        dnorm = dy * _silu(z)
        dgain_ref[hb] += jnp.sum(dnorm * oh, axis=0, keepdims=True)
        doh = dnorm * gain_v
        do = rstd * (doh - oh * jnp.mean(doh * oh, axis=-1, keepdims=True))

        dstot = ds_acc[hb] + _bdot(qn, do, TN)
        dqn = _bdot(do, snew, NT)
        yield
        dtrans = _bdot(dstot, sp, NT)
        ds_acc[hb] = _bdot(c["trans"], dstot, TN)
        yield
        dg = jnp.sum(jnp.sum(dtrans * c["eye"], axis=1, keepdims=True), axis=0, keepdims=True)
        m = -dtrans
        dke = _bdot(w, m, NT) + _bdot(u, dstot, NT)
        dw = _bdot(ke, m)
        du = _bdot(ke, dstot)
        yield
        drhs = _fdot(c["tm"], jnp.concatenate([du, dw], axis=1), TN)
        yield
        da = jnp.where(c["strict"], -_fdot(drhs, c["sol"], NT), 0.0)
        yield
        drhs_u, drhs_w = drhs[:, :DV], drhs[:, DV:]
        rwk = rsum(drhs_w * kn)
        dbeta = rsum(da * c["rg"]) + rsum(drhs_u * v) + rwk * e_col
        dgm = da * beta * c["rel"]
        dkn = _bdot(dgm, kn) + _bdot(dgm, kn, TN) + (beta * e_col) * drhs_w + dd * dke
        yield
        dv_ref[:, cols] = beta * drhs_u
        r_ = da * c["a"]
        ddd = rsum(dke * kn)
        dc = rsum(r_) - _row_to_col(jnp.sum(r_, axis=0, keepdims=True)) + beta * rwk * e_col - ddd * dd
        dclast = jnp.sum(ddd * dd, axis=0, keepdims=True) + dg * c["g"]
        dc = dc + jnp.where(_iota2((C, 1), 0) == C - 1, dclast, 0.0)
        dla = jnp.sum(jnp.where(c["ii"] <= c["jj"], _col_to_row(dc), 0.0), axis=1, keepdims=True)
        dalog = jnp.sum(dla * c["la"], axis=0, keepdims=True)
        dxs = dla * (-c["ea"]) * _sigmoid(c["xs"])
        ddtb = jnp.sum(dxs, axis=0, keepdims=True)
        dbl = dbeta * beta * (1.0 - beta)
        lane = _iota2((C, LANE), 1)
        dgates_ref[hb] = jnp.where(lane == 0, dbl, jnp.where(lane == 1, dxs, 0.0))
        lane8 = _iota2((8, LANE), 1)
        dscal_ref[hb] += jnp.where(lane8 == 0, dalog, jnp.where(lane8 == 1, ddtb, 0.0))
        dk_ref[:, cols] = c["rk"] * (dkn - kn * rsum(dkn * kn))
        qh = c["q"] * c["rq"]
        dqs = dqn * (DK ** -0.5)
        dq_ref[:, cols] = c["rq"] * (dqs - qh * rsum(dqs * qh))

    q, k, v, z, gt, sc, gn, st = _gdn_specs(N, True)
    dgt = pl.BlockSpec((GDN_HB, C, LANE), lambda h, n: (h, N - 1 - n, 0))
    dsc = pl.BlockSpec((GDN_HB, 8, LANE), lambda h, n: (h, 0, 0))
    dgn = pl.BlockSpec((GDN_HB, 1, DV), lambda h, n: (h, 0, 0))
    sh = jax.ShapeDtypeStruct((T, H * DK), F32)
    tm_s, sol_s = _gdn_solved_specs(N, True)
    return pl.pallas_call(
        body, name=name, grid=(H // GDN_HB, N), in_specs=[q, k, v, z, gt, sc, sc, gn, st, tm_s, sol_s, q],
        out_specs=(q, q, q, q, dgt, dsc, dgn),
        out_shape=(sh, sh, sh, jax.ShapeDtypeStruct((T, H * DV), BF16),
                   jax.ShapeDtypeStruct((H, T, LANE), F32), jax.ShapeDtypeStruct((H, 8, LANE), F32),
                   jax.ShapeDtypeStruct((H, 1, DV), F32)),
        scratch_shapes=[pltpu.VMEM((GDN_HB, DK, DV), F32)],
        compiler_params=_params(("arbitrary", "arbitrary")),
    )(act, act, act, proj, gates, a_log, dt_bias, gain, states, *solved, dog)


SUBLANES = 8


def _linear_scan(a_ref, b_ref, h_ref, reverse):
    T, W = a_ref.shape
    nb = T // SUBLANES
    row = _iota2((SUBLANES, W), 0)

    def blk(bi, carry):
        bb = (nb - 1 - bi) if reverse else bi
        off = pl.multiple_of(bb * SUBLANES, SUBLANES)
        a = a_ref[pl.ds(off, SUBLANES), :]
        b = b_ref[pl.ds(off, SUBLANES), :]
        for d in (1, 2, 4):
            if reverse:
                edge = row >= SUBLANES - d
                a_sh = jnp.where(edge, 1.0, pltpu.roll(a, SUBLANES - d, 0))
                b_sh = jnp.where(edge, 0.0, pltpu.roll(b, SUBLANES - d, 0))
            else:
                edge = row < d
                a_sh = jnp.where(edge, 1.0, pltpu.roll(a, d, 0))
                b_sh = jnp.where(edge, 0.0, pltpu.roll(b, d, 0))
            b = a * b_sh + b
            a = a * a_sh
        h = a * carry + b
        h_ref[pl.ds(off, SUBLANES), :] = h
        return h[0:1, :] if reverse else h[SUBLANES - 1:SUBLANES, :]

    lax.fori_loop(0, nb, blk, jnp.zeros((1, W), F32))


def _lru_specs(T):
    B, W = LRU_BLOCKS, LRU_BLOCK
    xb = pl.BlockSpec((T, W), lambda j: (0, j))
    yb = pl.BlockSpec((T, W), lambda j: (0, B + j))
    cw = pl.BlockSpec((CONV_WIDTH, W), lambda j: (0, j))
    vec = pl.BlockSpec((1, W), lambda j: (0, j))
    wg = pl.BlockSpec((1, W, W), lambda j: (j, 0, 0))
    bg = pl.BlockSpec((1, 1, W), lambda j: (j, 0, 0))
    return xb, yb, cw, vec, wg, bg


def _lru_gates(xb_ref, cw_ref, cb_ref, wr_ref, br_ref, wi_ref, bi_ref, lam_ref):
    xv = xb_ref[...]
    xc = _conv(xv, cw_ref) + cb_ref[...]
    r = _sigmoid(_bdot(xc, wr_ref[0]) + br_ref[0])
    i = _sigmoid(_bdot(xc, wi_ref[0]) + bi_ref[0])
    sp = _softplus(-lam_ref[...])
    la = -LRU_C * sp * r
    a = jnp.exp(la)
    s = jnp.sqrt(-_expm1(2.0 * la))
    return xv, xc, r, i, sp, a, s


def _lru_fwd(proj, conv_w, conv_b, w_r, b_r, w_i, b_i, lam, *, name):
    T = proj.shape[0]
    B, W = LRU_BLOCKS, LRU_BLOCK

    def body(xb_ref, yb_ref, cw_ref, cb_ref, wr_ref, br_ref, wi_ref, bi_ref, lam_ref, og_ref, hs_ref, a_s, u_s):
        xv, xc, r, i, sp, a, s = _lru_gates(xb_ref, cw_ref, cb_ref, wr_ref, br_ref, wi_ref, bi_ref, lam_ref)
        a_s[...] = a
        u_s[...] = s * (i * xc)
        _linear_scan(a_s, u_s, hs_ref, False)
        og_ref[...] = (hs_ref[...] * _gelu(yb_ref[...])).astype(og_ref.dtype)

    xb, yb, cw, vec, wg, bg = _lru_specs(T)
    return pl.pallas_call(
        body, name=name, grid=(B,), in_specs=[xb, yb, cw, vec, wg, bg, wg, bg, vec], out_specs=(xb, xb),
        out_shape=(jax.ShapeDtypeStruct((T, B * W), BF16), jax.ShapeDtypeStruct((T, B * W), F32)),
        scratch_shapes=[pltpu.VMEM((T, W), F32), pltpu.VMEM((T, W), F32)],
        compiler_params=_params(("arbitrary",)),
    )(proj, proj, conv_w, conv_b, w_r, b_r, w_i, b_i, lam)


def _lru_bwd(proj, conv_w, conv_b, w_r, b_r, w_i, b_i, lam, hs, dout, *, name):
    T = proj.shape[0]
    B, W = LRU_BLOCKS, LRU_BLOCK

    def csum(x):
        return jnp.sum(x, axis=0, keepdims=True)

    def body(xb_ref, yb_ref, cw_ref, cb_ref, wr_ref, br_ref, wi_ref, bi_ref, lam_ref, hs_ref, do_ref,
             dxb_ref, dyb_ref, dcw_ref, dcb_ref, dwr_ref, dbr_ref, dwi_ref, dbi_ref, dlam_ref, a_s, b_s, g_s):
        xv, xc, r, i, sp, a, s = _lru_gates(xb_ref, cw_ref, cb_ref, wr_ref, br_ref, wi_ref, bi_ref, lam_ref)
        h = hs_ref[...]
        yb = yb_ref[...]
        dout = do_ref[...].astype(F32)
        dyb_ref[...] = (dout * h * _dgelu(yb)).astype(dyb_ref.dtype)
        a_s[...] = _shift_up(a, 1)
        b_s[...] = dout * _gelu(yb)
        _linear_scan(a_s, b_s, g_s, True)
        g = g_s[...]
        da = g * _shift_down(h, 1)
        ds = g * (i * xc)
        di = g * s * xc
        dxc = g * s * i
        dla = da * a - ds * (a * a) / s
        dr = dla * (-LRU_C * sp)
        dlam_ref[...] = csum(dla * r) * (LRU_C * _sigmoid(-lam_ref[...]))
        dpr = dr * r * (1.0 - r)
        dpi = di * i * (1.0 - i)
        dxc = dxc + _bdot(dpr, wr_ref[0], NT) + _bdot(dpi, wi_ref[0], NT)
        dwr_ref[0] = _bdot(xc, dpr, TN)
        dwi_ref[0] = _bdot(xc, dpi, TN)
        dbr_ref[0] = csum(dpr)
        dbi_ref[0] = csum(dpi)
        dcb_ref[...] = csum(dxc)
        dxb_ref[...] = _conv_bwd(xv, cw_ref, dxc, dcw_ref).astype(dxb_ref.dtype)

    xb, yb, cw, vec, wg, bg = _lru_specs(T)
    act = jax.ShapeDtypeStruct((T, B * W), BF16)
    return pl.pallas_call(
        body, name=name, grid=(B,), in_specs=[xb, yb, cw, vec, wg, bg, wg, bg, vec, xb, xb],
        out_specs=(xb, xb, cw, vec, wg, bg, wg, bg, vec),
        out_shape=(act, act, jax.ShapeDtypeStruct((CONV_WIDTH, B * W), F32), jax.ShapeDtypeStruct((1, B * W), F32),
                   jax.ShapeDtypeStruct((B, W, W), F32), jax.ShapeDtypeStruct((B, 1, W), F32),
                   jax.ShapeDtypeStruct((B, W, W), F32), jax.ShapeDtypeStruct((B, 1, W), F32),
                   jax.ShapeDtypeStruct((1, B * W), F32)),
        scratch_shapes=[pltpu.VMEM((T, W), F32)] * 3,
        compiler_params=_params(("arbitrary",)),
    )(proj, proj, conv_w, conv_b, w_r, b_r, w_i, b_i, lam, hs, dout)


MESH = pl.DeviceIdType.MESH
N_CHIPS = 4
AG_COPIES = 7


def _mesh_pos():
    return lax.axis_index("x"), lax.axis_index("y"), lax.axis_index("c")


def _hbm_specs(n):
    return [pl.BlockSpec(memory_space=pltpu.HBM)] * n


def _all_gather(shards, *, name):
    n = len(shards)

    def body(*refs):
        xs, outs = refs[:n], refs[n:2 * n]
        send_sems, recv_sems, local_sems = refs[2 * n:]
        x, y, c = _mesh_pos()
        me, sibling = (x, y, c), (x, y, 1 - c)
        chips = [(1 - x, y), (x, 1 - y), (1 - x, 1 - y)]

        def rows(t, px, py, pc):
            return outs[t].at[4 * px + 2 * py + pc]

        def copy(t, k, block, to, src=None):
            return pltpu.make_async_remote_copy(
                src_ref=rows(t, *block) if src is None else src, dst_ref=rows(t, *block),
                send_sem=send_sems.at[t * AG_COPIES + k], recv_sem=recv_sems.at[t * AG_COPIES + k],
                device_id=to, device_id_type=MESH)

        mine = [pltpu.make_async_copy(xs[t], rows(t, *me), local_sems.at[t]) for t in range(n)]
        for cp in mine:
            cp.start()
        first = []
        for t in range(n):
            first.append(copy(t, 0, me, sibling, src=xs[t]))
            first += [copy(t, 1 + j, me, (*chip, c), src=xs[t]) for j, chip in enumerate(chips)]
        for cp in first:
            cp.start()
        passed = []
        for j, chip in enumerate(chips):
            for t in range(n):
                copy(t, 1 + j, (*chip, c), me).wait_recv()
                cp = copy(t, 4 + j, (*chip, c), sibling)
                cp.start()
                passed.append(cp)
        for t in range(n):
            copy(t, 0, sibling, me).wait_recv()
        for j, chip in enumerate(chips):
            for t in range(n):
                copy(t, 4 + j, (*chip, 1 - c), me).wait_recv()
        for cp in first + passed:
            cp.wait_send()
        for cp in mine:
            cp.wait()

    return pl.pallas_call(
        body, name=name,
        out_shape=[jax.ShapeDtypeStruct((N_DEV,) + s.shape, s.dtype) for s in shards],
        in_specs=_hbm_specs(n), out_specs=_hbm_specs(n),
        scratch_shapes=[pltpu.SemaphoreType.DMA((n * AG_COPIES,)), pltpu.SemaphoreType.DMA((n * AG_COPIES,)),
                        pltpu.SemaphoreType.DMA((n,))],
    )(*shards)


SIDE_EFFECT = pltpu.SideEffectType.DATAFLOW_SIDE_EFFECTING


def _copies(plan, refs, send_sems, recv_sems):
    return [pltpu.make_async_remote_copy(src_ref=src, dst_ref=dst, send_sem=send_sems.at[k], recv_sem=recv_sems.at[k],
                                         device_id=to, device_id_type=MESH)
            for k, (src, dst, to) in enumerate(plan(refs))]


def _split_start(bufs, plan, n_copies, *, name, deps=()):
    n = len(bufs)

    def body(*refs):
        send_sems, recv_sems = refs[n + len(deps)], refs[n + len(deps) + 1]
        token = refs[-1]
        for cp in _copies(plan, refs[:n], send_sems, recv_sems):
            cp.start()
        token[...] = jnp.zeros_like(token)

    hbm, sem = pl.BlockSpec(memory_space=pltpu.HBM), pl.BlockSpec(memory_space=pltpu.SEMAPHORE)
    out = pl.pallas_call(
        body, name=name,
        out_shape=(pltpu.SemaphoreType.DMA((n_copies,)), pltpu.SemaphoreType.DMA((n_copies,)),
                   *[pltpu.HBM(b.shape, b.dtype) for b in bufs], jax.ShapeDtypeStruct(TOKEN_SHAPE, F32)),
        in_specs=[hbm] * n + [pl.BlockSpec(memory_space=pl.ANY)] * len(deps),
        out_specs=(sem, sem, *[hbm] * n, pl.BlockSpec(memory_space=pltpu.VMEM)),
        input_output_aliases={i: 2 + i for i in range(n)},
        compiler_params=pltpu.CompilerParams(has_side_effects=SIDE_EFFECT),
    )(*[pltpu.with_memory_space_constraint(b, pltpu.HBM) for b in bufs], *deps)
    return out[0], out[1], list(out[2:2 + n]), out[-1]


def _split_wait(send_sems, recv_sems, bufs, plan, after, *, name):
    n = len(bufs)

    def body(*refs):
        for cp in _copies(plan, refs[:n], refs[n], refs[n + 1]):
            cp.wait_send()
            cp.wait_recv()

    hbm, sem = pl.BlockSpec(memory_space=pltpu.HBM), pl.BlockSpec(memory_space=pltpu.SEMAPHORE)
    out = pl.pallas_call(
        body, name=name, out_shape=tuple(pltpu.HBM(b.shape, b.dtype) for b in bufs),
        in_specs=[hbm] * n + [sem, sem, pl.BlockSpec(memory_space=pl.ANY)], out_specs=tuple([hbm] * n),
        input_output_aliases={i: i for i in range(n)},
        compiler_params=pltpu.CompilerParams(has_side_effects=SIDE_EFFECT),
    )(*bufs, send_sems, recv_sems, after)
    return list(out)


def _block(ref, in_cols, d):
    if not in_cols:
        return ref.at[d]
    c = ref.shape[1] // N_DEV
    return ref.at[:, pl.ds(pl.multiple_of(d * c, LANE), c)]


def _plan_gather_a(in_cols):
    n = len(in_cols)

    def plan(refs):
        x, y, c = _mesh_pos()
        me = 4 * x + 2 * y + c
        peers = [(x, y, 1 - c), (1 - x, y, c), (x, 1 - y, c), (1 - x, 1 - y, c)]
        return [(refs[t], _block(refs[n + t], in_cols[t], me), to) for t in range(n) for to in peers]
    return plan


def _plan_gather_b(in_cols):
    n = len(in_cols)

    def plan(refs):
        x, y, c = _mesh_pos()
        ds = [4 * px + 2 * py + c for px, py in [(1 - x, y), (x, 1 - y), (1 - x, 1 - y)]]
        return [(_block(refs[t], in_cols[t], d), _block(refs[t], in_cols[t], d), (x, y, 1 - c))
                for t in range(n) for d in ds]
    return plan


def _plan_scatter_pair(in_cols):
    n = len(in_cols)

    def plan(refs):
        x, y, c = _mesh_pos()
        return [(_block(refs[t], in_cols[t], 2 * q + (1 - c)), refs[n + t].at[q], (x, y, 1 - c))
                for t in range(n) for q in range(N_CHIPS)]
    return plan


def _plan_scatter_chips(n):
    def plan(refs):
        x, y, c = _mesh_pos()
        chips = [(1 - x, y), (x, 1 - y), (1 - x, 1 - y)]
        return [(refs[t].at[2 * px + py], refs[n + t].at[j], (px, py, c))
                for t in range(n) for j, (px, py) in enumerate(chips)]
    return plan


def _insert_block(land, shard, device, in_cols, *, name):
    r, c = shard.shape
    tr = _tile(r, (512, 256))

    def body(d_ref, s_ref, land_ref, o_ref):
        if in_cols:
            o_ref[...] = s_ref[...]
        else:
            o_ref[0] = s_ref[...]

    if in_cols:
        o_spec = pl.BlockSpec((tr, c), lambda i, d: (i, d[0]))
    else:
        o_spec = pl.BlockSpec((1, tr, c), lambda i, d: (d[0], i, 0))
    return pl.pallas_call(
        body, name=name,
        grid_spec=pltpu.PrefetchScalarGridSpec(
            num_scalar_prefetch=1, grid=(r // tr,),
            in_specs=[pl.BlockSpec((tr, c), lambda i, d: (i, 0)), pl.BlockSpec(memory_space=pl.ANY)],
            out_specs=o_spec),
        out_shape=jax.ShapeDtypeStruct(land.shape, land.dtype), input_output_aliases={2: 0},
        compiler_params=_params(("parallel",)),
    )(device, shard, land)


PAIR_ROWS = (512, 256)


def _pair_add(g, a, core, in_cols, *, name):
    _, R, C = a.shape
    tr = _tile(R, PAIR_ROWS)

    def body(c_ref, g_ref, a_ref, o_ref):
        gv = g_ref[...] if in_cols else g_ref[0]
        o_ref[0] = (gv.astype(F32) + a_ref[0].astype(F32)).astype(o_ref.dtype)

    blk = pl.BlockSpec((1, tr, C), lambda q, i, c: (q, i, 0))
    if in_cols:
        g_spec = pl.BlockSpec((tr, C), lambda q, i, c: (i, 2 * q + c[0]))
    else:
        g_spec = pl.BlockSpec((1, tr, C), lambda q, i, c: (2 * q + c[0], i, 0))
    return pl.pallas_call(
        body, name=name,
        grid_spec=pltpu.PrefetchScalarGridSpec(
            num_scalar_prefetch=1, grid=(N_CHIPS, R // tr), in_specs=[g_spec, blk], out_specs=blk),
        out_shape=jax.ShapeDtypeStruct((N_CHIPS, R, C), BF16),
        compiler_params=_params(("parallel", "parallel")),
    )(core, g, a)


ADAM_ROWS = 256


def _adamw_sharded(w, m, v, s4, b3, chip, layer, prev, *, name):
    L, R, C = w.shape
    tr = _tile(R, (ADAM_ROWS,))
    if prev is None and L > 1:
        prev = tuple(lax.empty(w.shape, F32) for _ in range(4))
    n_prev = 0 if prev is None else 4

    def body(q_ref, w_ref, m_ref, v_ref, s_ref, b_ref, *rest):
        g_out, d_out, m_out, v_out = rest[n_prev:]
        g = s_ref[0].astype(F32)
        for j in range(N_CHIPS - 1):
            g = g + b_ref[j].astype(F32)
        mn = ADAM_B1 * m_ref[0] + (1.0 - ADAM_B1) * g
        vn = ADAM_B2 * v_ref[0] + (1.0 - ADAM_B2) * (g * g)
        g_out[0] = g
        d_out[0] = -ADAM_LR * ((mn / ADAM_C1) / (jnp.sqrt(vn / ADAM_C2) + ADAM_EPS) + ADAM_WD * w_ref[0])
        m_out[0] = mn
        v_out[0] = vn

    blk = pl.BlockSpec((1, tr, C), lambda i, q: (layer, i, 0))
    sh = jax.ShapeDtypeStruct((L, R, C), F32)
    return pl.pallas_call(
        body, name=name,
        grid_spec=pltpu.PrefetchScalarGridSpec(
            num_scalar_prefetch=1, grid=(R // tr,),
            in_specs=[blk, blk, blk, pl.BlockSpec((1, tr, C), lambda i, q: (q[0], i, 0)),
                      pl.BlockSpec((N_CHIPS - 1, tr, C), lambda i, q: (0, i, 0))]
            + [pl.BlockSpec(memory_space=pl.ANY)] * n_prev,
            out_specs=(blk,) * 4),
        out_shape=(sh,) * 4, input_output_aliases={6 + k: k for k in range(n_prev)},
        compiler_params=_params(("parallel",)),
    )(chip, w, m, v, s4, b3, *(prev or ()))


FWD_NAMES = ['x', 'norm1', 'norm2', 'final_norm', 'ret_w_in', 'ret_gn_gain', 'ret_w_out', 'gdn_w_in', 'gdn_conv_w',
             'gdn_a_log', 'gdn_dt_bias', 'gdn_norm_gain', 'gdn_w_out', 'gla_w_in', 'gla_w_gate_up', 'gla_gate_bias',
             'gla_norm_gain', 'gla_w_out', 'lru_w_in', 'lru_conv_w', 'lru_conv_b', 'lru_w_rgate', 'lru_b_rgate',
             'lru_w_igate', 'lru_b_igate', 'lru_lambda', 'lru_w_out', 'mlp_w_up', 'mlp_w_down']
WEIGHT_NAMES = FWD_NAMES[1:]
ARG_NAMES = FWD_NAMES + ['loss_target'] + ['m_' + n for n in WEIGHT_NAMES] + ['v_' + n for n in WEIGHT_NAMES]

MIXER_IN = ('ret_w_in', 'gdn_w_in', 'gla_w_in', 'lru_w_in')
MIXER_OUT = ('ret_w_out', 'gdn_w_out', 'gla_w_out', 'lru_w_out')
BIG_NAMES = MIXER_IN + MIXER_OUT + ('mlp_w_up', 'mlp_w_down')
SMALL = {'norm1': False, 'norm2': False, 'final_norm': False, 'ret_gn_gain': True, 'gdn_conv_w': True,
         'gdn_a_log': False, 'gdn_dt_bias': False, 'gdn_norm_gain': False, 'gla_w_gate_up': True,
         'gla_gate_bias': True, 'gla_norm_gain': True, 'lru_conv_w': True, 'lru_conv_b': True,
         'lru_w_rgate': False, 'lru_b_rgate': False, 'lru_w_igate': False, 'lru_b_igate': False, 'lru_lambda': True}
SMALL_NAMES = tuple(n for n in WEIGHT_NAMES if n in SMALL)
MEDIUM = ('lru_w_rgate', 'lru_w_igate')
EARLY_SMALL = tuple(n for n in SMALL_NAMES if n.startswith(('gdn_', 'gla_', 'lru_')) and n not in MEDIUM)
LATE_SMALL = tuple(n for n in SMALL_NAMES if n not in EARLY_SMALL + MEDIUM)
GDN_TAIL = 2 * GDN_HEADS
GDN_MAIN = 4 * D_MODEL
GLA_MAIN = 3 * D_MODEL


PACK_ROWS = 256


PACK_TILE = SUBLANES * LANE


def _pack(arrs):
    rows = []
    for a in arrs:
        f = a.reshape(-1).astype(F32)
        rows.append(jnp.pad(f, (0, (-f.shape[0]) % PACK_TILE)).reshape(-1, LANE))
    fill = (-sum(r.shape[0] for r in rows)) % PACK_ROWS
    if fill:
        rows.append(jnp.zeros((fill, LANE), F32))
    return jnp.concatenate(rows, axis=0)


def _unpack(buf, shapes, lead=()):
    out, r0 = [], 0
    for s in shapes:
        n = int(np.prod(s))
        nr = -(-n // PACK_TILE) * SUBLANES
        blk = buf[..., r0:r0 + nr, :].reshape(lead + (nr * LANE,))[..., :n]
        out.append(blk.reshape(lead + tuple(s)))
        r0 += nr
    return out


def _full_cols(g):
    return jnp.transpose(g, (1, 0, 2)).reshape(g.shape[1], N_DEV * g.shape[2])


def _full_rows(g):
    return g.reshape(N_DEV * g.shape[1], g.shape[2])


def _blocks_cols(dw):
    r, c = dw.shape[0], dw.shape[1] // N_DEV
    return jnp.transpose(dw.reshape(r, N_DEV, c), (1, 0, 2))


def _blocks_rows(dw):
    return dw.reshape(N_DEV, dw.shape[0] // N_DEV, dw.shape[1])


def _pad_cols(a, n=LANE):
    return jnp.pad(a, ((0, 0), (0, n - a.shape[1])))


def _mixer_fwd(layer, hn, w_in, sm, tables, deps=()):
    tag = f"l{layer}"
    if layer == 0:
        proj = _matmul(hn, w_in, name=tag + "_in", deps=deps)
        gain = sm['ret_gn_gain'][0][:, None, :]
        og, st = _ret_fwd(proj, gain, tables, name=tag + "_ret_fwd")
        return og, dict(proj=proj, st=st, gain=gain)
    if layer == 1:
        w_main, w_tail = w_in, _pad_cols(w_in[:, GDN_MAIN:])
        proj = _matmul(hn, w_main, b_cols=GDN_MAIN, name=tag + "_in", deps=deps)
        gates = _matmul(hn, w_tail, name=tag + "_in_tail")
        conv_w = sm['gdn_conv_w'][0]
        act = _conv_silu_fwd(proj, conv_w, name=tag + "_conv")
        a_log = sm['gdn_a_log'].reshape(GDN_HEADS, 1, 1)
        dt_bias = sm['gdn_dt_bias'].reshape(GDN_HEADS, 1, 1)
        gain = sm['gdn_norm_gain']
        og, st, tm, sol = _gdn_fwd(act, proj, gates, a_log, dt_bias, gain, name=tag + "_gdn_fwd")
        return og, dict(proj=proj, gates=gates, act=act, st=st, solved=(tm, sol), conv_w=conv_w, a_log=a_log, dt_bias=dt_bias,
                        gain=gain, w_main=w_main, w_tail=w_tail)
    if layer == 2:
        w_main, w_tail = w_in, _pad_cols(w_in[:, GLA_MAIN:])
        proj = _matmul(hn, w_main, b_cols=GLA_MAIN, name=tag + "_in", deps=deps)
        glow = _matmul(hn, w_tail, name=tag + "_in_tail")
        wgu = jnp.pad(sm['gla_w_gate_up'][0], ((0, LANE - GLA_GATE_RANK), (0, 0)))
        glogit = _matmul(glow, wgu, name=tag + "_gate_up")
        bias = sm['gla_gate_bias']
        gain = sm['gla_norm_gain'][0][:, None, :]
        og, st = _gla_fwd(proj, glogit, bias, gain, name=tag + "_gla_fwd")
        return og, dict(proj=proj, glow=glow, glogit=glogit, wgu=wgu, bias=bias, gain=gain, st=st,
                        w_main=w_main, w_tail=w_tail)
    proj = _matmul(hn, w_in, name=tag + "_in", deps=deps)
    args = (proj, sm['lru_conv_w'][0], sm['lru_conv_b'], sm['lru_w_rgate'][0], sm['lru_b_rgate'][0][:, None, :],
            sm['lru_w_igate'][0], sm['lru_b_igate'][0][:, None, :], sm['lru_lambda'])
    og, hs = _lru_fwd(*args, name=tag + "_lru_fwd")
    return og, dict(args=args, hs=hs)


def _mixer_bwd(layer, hn, w_in, dog, sv, tables, on_dw):
    tag = f"l{layer}"
    if layer == 0:
        dq, dk, dv, dg, dgain = _ret_bwd(sv['proj'], sv['gain'], tables, sv['st'], dog, name=tag + "_ret_bwd")
        dproj = jnp.concatenate([dq, dk, dv, dg], axis=1)
        deps = on_dw(_matmul(hn, dproj, ta=True, out_dtype=BF16, name=tag + "_in_dw"))
        dhn = _matmul(dproj, w_in, tb=True, name=tag + "_in_dx", deps=deps)
        return dhn, {'ret_gn_gain': dgain[:, 0][None]}
    if layer == 1:
        dq, dk, dv, dz, dgates, dscal, dgain = _gdn_bwd(
            sv['act'], sv['proj'], sv['gates'], sv['a_log'], sv['dt_bias'], sv['gain'], sv['st'], sv['solved'], dog,
            name=tag + "_gdn_bwd")
        dact = jnp.concatenate([dq, dk, dv], axis=1)
        dqkv, dconv = _conv_silu_bwd(sv['proj'], sv['conv_w'], dact, name=tag + "_conv_bwd")
        dmain = jnp.concatenate([dqkv, dz], axis=1)
        T = dmain.shape[0]
        dtail = _pad_cols(jnp.transpose(dgates[:, :, :2], (1, 2, 0)).reshape(T, GDN_TAIL))
        dw_main = _matmul(hn, dmain, ta=True, out_dtype=BF16, name=tag + "_in_dw")
        dw_tail = _matmul(hn, dtail, ta=True, out_dtype=BF16, name=tag + "_in_tail_dw")
        deps = on_dw(jnp.concatenate([dw_main, dw_tail[:, :GDN_TAIL]], axis=1))
        dhn = _matmul(dmain, sv['w_main'], tb=True, b_cols=GDN_MAIN, name=tag + "_in_dx", deps=deps)
        dhn = _matmul(dtail, sv['w_tail'], tb=True, epi="add", extra=dhn, name=tag + "_in_tail_dx")
        small = {'gdn_conv_w': dconv[None], 'gdn_a_log': dscal[:, 0, 0][None], 'gdn_dt_bias': dscal[:, 0, 1][None],
                 'gdn_norm_gain': jnp.sum(dgain[:, 0], axis=0)[None]}
        return dhn, small
    if layer == 2:
        dq, dk, dv, dr, dgl, dgain = _gla_bwd(sv['proj'], sv['glogit'], sv['bias'], sv['gain'], sv['st'], dog,
                                              name=tag + "_gla_bwd")
        dmain = jnp.concatenate([dq, dk, dv, dr], axis=1)
        dglow = _matmul(dgl, sv['wgu'], tb=True, name=tag + "_gate_up_dx")
        dwgu = _matmul(sv['glow'], dgl, ta=True, name=tag + "_gate_up_dw")
        dbias = _colsum(dgl, name=tag + "_gate_bias")
        dw_main = _matmul(hn, dmain, ta=True, out_dtype=BF16, name=tag + "_in_dw")
        dw_tail = _matmul(hn, dglow, ta=True, out_dtype=BF16, name=tag + "_in_tail_dw")
        deps = on_dw(jnp.concatenate([dw_main, dw_tail[:, :GLA_GATE_RANK]], axis=1))
        dhn = _matmul(dmain, sv['w_main'], tb=True, b_cols=GLA_MAIN, name=tag + "_in_dx", deps=deps)
        dhn = _matmul(dglow, sv['w_tail'], tb=True, epi="add", extra=dhn, name=tag + "_in_tail_dx")
        small = {'gla_w_gate_up': dwgu[:GLA_GATE_RANK][None], 'gla_gate_bias': dbias,
                 'gla_norm_gain': dgain[:, 0][None]}
        return dhn, small
    dxb, dyb, dcw, dcb, dwr, dbr, dwi, dbi, dlam = _lru_bwd(*sv['args'], sv['hs'], dog, name=tag + "_lru_bwd")
    dproj = jnp.concatenate([dxb, dyb], axis=1)
    deps = on_dw(_matmul(hn, dproj, ta=True, out_dtype=BF16, name=tag + "_in_dw"))
    dhn = _matmul(dproj, w_in, tb=True, name=tag + "_in_dx", deps=deps)
    small = {'lru_conv_w': dcw[None], 'lru_conv_b': dcb, 'lru_w_rgate': dwr[None], 'lru_b_rgate': dbr[:, 0][None],
             'lru_w_igate': dwi[None], 'lru_b_igate': dbi[:, 0][None], 'lru_lambda': dlam}
    return dhn, small


def _step(*args):
    assert len(args) == len(ARG_NAMES)
    p = dict(zip(ARG_NAMES, args))
    xi, yi, ci = _mesh_pos()
    dev = 4 * xi + 2 * yi + ci
    device = dev.astype(jnp.int32).reshape(1)
    core = ci.astype(jnp.int32).reshape(1)
    chip = (2 * xi + yi).astype(jnp.int32).reshape(1)
    x = p['x'][0]
    target = p['loss_target'][0]
    T = x.shape[0]
    tables = _ret_tables(T)

    sharded_small = [n for n in SMALL_NAMES if SMALL[n]]
    gathered, = _all_gather([_pack([p[n] for n in sharded_small])], name="gather_small")
    def in_cols_of(w):
        return w.shape[1] % LANE == 0

    gathers = {}
    token = gathered
    for layer in range(DEPTH):
        w_in_shard = p[MIXER_IN[layer]][0]
        groups = {'a': [(w_in_shard, in_cols_of(w_in_shard), _full_cols)],
                  'b': [(p[MIXER_OUT[layer]][0], False, _full_rows)],
                  'c': [(p['mlp_w_up'][layer], True, None), (p['mlp_w_down'][layer], False, _full_rows)]}
        for key, members in groups.items():
            shards = [w.astype(BF16) for w, _, _ in members]
            in_cols = [ic for _, ic, _ in members]
            n = len(shards)
            lands = [_insert_block(lax.empty((s.shape[0], N_DEV * s.shape[1]) if ic else (N_DEV,) + s.shape, BF16),
                                   s, device, ic, name=f"own_l{layer}{key}{i}")
                     for i, (s, ic) in enumerate(zip(shards, in_cols))]
            send, recv, bufs, token = _split_start(shards + lands, _plan_gather_a(in_cols), 4 * n,
                                                   name=f"gather_a_start_l{layer}{key}", deps=(token,))
            gathers[layer, key] = dict(n=n, h=(send, recv, bufs), in_cols=in_cols, full_of=[f for _, _, f in members])

    def gather_forward(layer, key, after):
        g = gathers[layer, key]
        n = g['n']
        send, recv, bufs = g['h']
        bufs = _split_wait(send, recv, bufs, _plan_gather_a(g['in_cols']), after, name=f"gather_a_wait_l{layer}{key}")
        send, recv, lands, tok = _split_start(bufs[n:], _plan_gather_b(g['in_cols']), 3 * n,
                                              name=f"gather_b_start_l{layer}{key}")
        g['h'] = (send, recv, lands)
        return tok

    def gather_finish(layer, key, after):
        g = gathers[layer, key]
        send, recv, lands = g['h']
        lands = _split_wait(send, recv, lands, _plan_gather_b(g['in_cols']), after, name=f"gather_b_wait_l{layer}{key}")
        return [l if ic else full_of(l) for l, ic, full_of in zip(lands, g['in_cols'], g['full_of'])]

    w_in_next, = gather_finish(0, 'a', gather_forward(0, 'a', token))
    parts = _unpack(gathered, [p[n].shape for n in sharded_small], lead=(N_DEV,))
    sm = {n: p[n] for n in SMALL_NAMES if not SMALL[n]}
    for n, blk in zip(sharded_small, parts):
        full = jnp.moveaxis(blk, 0, -2)
        sm[n] = full.reshape(full.shape[:-2] + (N_DEV * full.shape[-1],))

    saved = []
    big = {}
    for layer in range(DEPTH):
        w_in = w_in_next
        tag = f"l{layer}"
        hn = _rmsnorm_fwd(x, sm['norm1'][layer][None], name=tag + "_norm1")
        og, sv = _mixer_fwd(layer, hn, w_in, sm, tables)
        tok_b = gather_forward(layer, 'b', og)
        tok_c = gather_forward(layer, 'c', tok_b)
        w_out, = gather_finish(layer, 'b', tok_c)
        x_mid = _matmul(og, w_out, epi="add", extra=x, name=tag + "_out")
        hn2 = _rmsnorm_fwd(x_mid, sm['norm2'][layer][None], name=tag + "_norm2")
        w_up, w_down = gather_finish(layer, 'c', hn2)
        big[layer] = (w_in, w_out, w_up, w_down)
        u, a = _matmul(hn2, w_up, epi="relu2", name=tag + "_up")
        deps = (gather_forward(layer + 1, 'a', u),) if layer + 1 < DEPTH else ()
        x_new = _matmul(a, w_down, epi="add", extra=x_mid, name=tag + "_down", deps=deps)
        if layer + 1 < DEPTH:
            w_in_next, = gather_finish(layer + 1, 'a', x_new)
        saved.append(dict(x=x, hn=hn, og=og, sv=sv, x_mid=x_mid, hn2=hn2, u=u, a=a))
        x = x_new
    dx, dxb, dfinal, loss_part = _final_loss_bwd(x, sm['final_norm'][None], target, name="final_loss_bwd")
    loss = lax.psum(loss_part[0, 0], ("x", "y", "c"))

    outs = {}
    small_grads = {'final_norm': dfinal[0]}
    dnorm1, dnorm2 = [None] * DEPTH, [None] * DEPTH

    def scatter_start(items, deps, tag):
        grads, in_cols = [], []
        for _, _, dw, on in items:
            ic = on == 'cols' and (dw.shape[1] // N_DEV) % LANE == 0
            in_cols.append(ic)
            grads.append(dw if ic else (_blocks_cols(dw) if on == 'cols' else _blocks_rows(dw)))
        n = len(grads)
        lands = [lax.empty((N_CHIPS, dw.shape[0], dw.shape[1] // N_DEV) if ic else (N_CHIPS,) + dw.shape[1:], BF16)
                 for dw, ic in zip(grads, in_cols)]
        send, recv, bufs, tok = _split_start(grads + lands, _plan_scatter_pair(in_cols), N_CHIPS * n,
                                             name=f"scatter_pair_start_{tag}", deps=deps)
        return dict(items=items, n=n, tag=tag, in_cols=in_cols, h=(send, recv, bufs)), tok

    def scatter_forward(g, after):
        n, tag = g['n'], g['tag']
        send, recv, bufs = g['h']
        bufs = _split_wait(send, recv, bufs, _plan_scatter_pair(g['in_cols']), after, name=f"scatter_pair_wait_{tag}")
        sums = [_pair_add(b, a_, core, ic, name=f"pair_add_{tag}_{i}")
                for i, (b, a_, ic) in enumerate(zip(bufs[:n], bufs[n:], g['in_cols']))]
        lands = [lax.empty((N_CHIPS - 1,) + s_.shape[1:], BF16) for s_ in sums]
        send, recv, bufs, tok = _split_start(sums + lands, _plan_scatter_chips(n), (N_CHIPS - 1) * n,
                                             name=f"scatter_chips_start_{tag}")
        g['h'] = (send, recv, bufs)
        return tok

    def scatter_finish(g, after):
        n, tag = g['n'], g['tag']
        send, recv, bufs = g['h']
        bufs = _split_wait(send, recv, bufs, _plan_scatter_chips(n), after, name=f"scatter_chips_wait_{tag}")
        for i, (wname, idx, _, _) in enumerate(g['items']):
            outs[wname] = _adamw_sharded(p[wname], p['m_' + wname], p['v_' + wname], bufs[i], bufs[n + i], chip,
                                         idx, outs.get(wname), name=f"adamw_{tag}_{i}")

    older = []
    for layer in reversed(range(DEPTH)):
        w_in, w_out, w_up, w_down = big[layer]
        s = saved[layer]
        tag = f"l{layer}"
        du = _matmul(dxb, w_down, tb=True, epi="drelu2", extra=s['u'], out_dtype=BF16, name=tag + "_down_dx")
        dw_down = _matmul(s['a'], dxb, ta=True, out_dtype=BF16, name=tag + "_down_dw")
        dhn2 = _matmul(du, w_up, tb=True, name=tag + "_up_dx")
        dw_up = _matmul(s['hn2'], du, ta=True, out_dtype=BF16, name=tag + "_up_dw")
        mlp_group, tok = scatter_start([('mlp_w_up', layer, dw_up, 'cols'),
                                        ('mlp_w_down', layer, dw_down, 'rows')], (), f"mlp_l{layer}")
        dx, dxb, dn2 = _rmsnorm_bwd(s['x_mid'], sm['norm2'][layer][None], dhn2, dx, name=tag + "_norm2_bwd",
                                    deps=(tok,))
        dog = _matmul(dxb, w_out, tb=True, name=tag + "_out_dx")
        tok = scatter_forward(mlp_group, dog)
        dw_out = _matmul(s['og'], dxb, ta=True, out_dtype=BF16, name=tag + "_out_dw", deps=(tok,))
        started = []

        def on_dw(dw_in):
            group, tok_ = scatter_start([(MIXER_IN[layer], 0, dw_in, 'cols'),
                                         (MIXER_OUT[layer], 0, dw_out, 'rows')], (), f"mix_l{layer}")
            started.append(group)
            return (tok_,)

        dhn, sg = _mixer_bwd(layer, s['hn'], w_in, dog, s['sv'], tables, on_dw)
        mixer_group, = started
        small_grads.update(sg)
        if layer > 0:
            tok = scatter_forward(mixer_group, dhn)
            dx, dxb, dn1 = _rmsnorm_bwd(s['x'], sm['norm1'][layer][None], dhn, dx, name=tag + "_norm1_bwd", deps=(tok,))
        else:
            tok = gather_forward('small_grads', 'early', dhn)
            dx, dxb, dn1 = _rmsnorm_bwd(s['x'], sm['norm1'][layer][None], dhn, dx, name=tag + "_norm1_bwd", deps=(tok,))
        dnorm1[layer], dnorm2[layer] = dn1[0], dn2[0]
        for g in older:
            scatter_finish(g, dx)
        older = [mlp_group, mixer_group]
        if layer == 1:
            parts = [_pack([small_grads[n] for n in EARLY_SMALL])] + [small_grads[n].reshape(-1, LANE) for n in MEDIUM]
            lands = [_insert_block(lax.empty((N_DEV,) + s_.shape, F32), s_, device, False, name=f"own_small_grads{i}")
                     for i, s_ in enumerate(parts)]
            n_parts = len(parts)
            send, recv, bufs, _ = _split_start(parts + lands, _plan_gather_a([False] * n_parts), 4 * n_parts,
                                               name="gather_a_start_small_grads", deps=(dx,))
            gathers['small_grads', 'early'] = dict(n=n_parts, h=(send, recv, bufs), in_cols=[False] * n_parts,
                                                   full_of=[lambda l: l] * n_parts)
    small_grads['norm1'] = jnp.stack(dnorm1)
    small_grads['norm2'] = jnp.stack(dnorm2)

    late_parts, = _all_gather([_pack([small_grads[n] for n in LATE_SMALL])], name="gather_small_grads")
    last_token = scatter_forward(mixer_group, late_parts)
    early_parts, *medium_parts = gather_finish('small_grads', 'early', last_token)
    early_sum = _sum_parts(early_parts, name="sum_small_grads_early", deps=(last_token,))
    late_sum = _sum_parts(late_parts, name="sum_small_grads_late")
    for g in older:
        scatter_finish(g, late_sum)
    for n, part in zip(MEDIUM, medium_parts):
        g = _sum_parts(part, name=f"sum_{n}")
        res = _adamw(p[n].reshape(-1, LANE), p['m_' + n].reshape(-1, LANE), p['v_' + n].reshape(-1, LANE), [g],
                     name=f"adamw_{n}")
        outs[n] = tuple(r.reshape(p[n].shape) for r in res)
    by_name = dict(zip(EARLY_SMALL, _unpack(early_sum, [small_grads[n].shape for n in EARLY_SMALL])))
    by_name.update(zip(LATE_SMALL, _unpack(late_sum, [small_grads[n].shape for n in LATE_SMALL])))
    packed_names = EARLY_SMALL + LATE_SMALL
    local_g = []
    for n in packed_names:
        g = by_name[n]
        if SMALL[n]:
            width = p[n].shape[-1]
            g = lax.dynamic_slice_in_dim(g, dev * width, width, axis=g.ndim - 1)
        local_g.append(g.reshape(p[n].shape))
    res = _adamw(_pack([p[n] for n in packed_names]), _pack([p['m_' + n] for n in packed_names]),
                 _pack([p['v_' + n] for n in packed_names]), [_pack(local_g)], name="adamw_small")
    local_shapes = [p[n].shape for n in packed_names]
    unpacked = [_unpack(r, local_shapes) for r in res]
    for i, n in enumerate(packed_names):
        outs[n] = tuple(unpacked[k][i] for k in range(4))

    result = [loss, dx[None]]
    for k in range(4):
        result += [outs[n][k] for n in WEIGHT_NAMES]
    return tuple(result)


def kernel(x, norm1, norm2, final_norm, ret_w_in, ret_gn_gain, ret_w_out, gdn_w_in, gdn_conv_w, gdn_a_log, gdn_dt_bias, gdn_norm_gain, gdn_w_out, gla_w_in, gla_w_gate_up, gla_gate_bias, gla_norm_gain, gla_w_out, lru_w_in, lru_conv_w, lru_conv_b, lru_w_rgate, lru_b_rgate, lru_w_igate, lru_b_igate, lru_lambda, lru_w_out, mlp_w_up, mlp_w_down, loss_target, m_norm1, m_norm2, m_final_norm, m_ret_w_in, m_ret_gn_gain, m_ret_w_out, m_gdn_w_in, m_gdn_conv_w, m_gdn_a_log, m_gdn_dt_bias, m_gdn_norm_gain, m_gdn_w_out, m_gla_w_in, m_gla_w_gate_up, m_gla_gate_bias, m_gla_norm_gain, m_gla_w_out, m_lru_w_in, m_lru_conv_w, m_lru_conv_b, m_lru_w_rgate, m_lru_b_rgate, m_lru_w_igate, m_lru_b_igate, m_lru_lambda, m_lru_w_out, m_mlp_w_up, m_mlp_w_down, v_norm1, v_norm2, v_final_norm, v_ret_w_in, v_ret_gn_gain, v_ret_w_out, v_gdn_w_in, v_gdn_conv_w, v_gdn_a_log, v_gdn_dt_bias, v_gdn_norm_gain, v_gdn_w_out, v_gla_w_in, v_gla_w_gate_up, v_gla_gate_bias, v_gla_norm_gain, v_gla_w_out, v_lru_w_in, v_lru_conv_w, v_lru_conv_b, v_lru_w_rgate, v_lru_b_rgate, v_lru_w_igate, v_lru_b_igate, v_lru_lambda, v_lru_w_out, v_mlp_w_up, v_mlp_w_down):
    return _step(x, norm1, norm2, final_norm, ret_w_in, ret_gn_gain, ret_w_out, gdn_w_in, gdn_conv_w, gdn_a_log, gdn_dt_bias, gdn_norm_gain, gdn_w_out, gla_w_in, gla_w_gate_up, gla_gate_bias, gla_norm_gain, gla_w_out, lru_w_in, lru_conv_w, lru_conv_b, lru_w_rgate, lru_b_rgate, lru_w_igate, lru_b_igate, lru_lambda, lru_w_out, mlp_w_up, mlp_w_down, loss_target, m_norm1, m_norm2, m_final_norm, m_ret_w_in, m_ret_gn_gain, m_ret_w_out, m_gdn_w_in, m_gdn_conv_w, m_gdn_a_log, m_gdn_dt_bias, m_gdn_norm_gain, m_gdn_w_out, m_gla_w_in, m_gla_w_gate_up, m_gla_gate_bias, m_gla_norm_gain, m_gla_w_out, m_lru_w_in, m_lru_conv_w, m_lru_conv_b, m_lru_w_rgate, m_lru_b_rgate, m_lru_w_igate, m_lru_b_igate, m_lru_lambda, m_lru_w_out, m_mlp_w_up, m_mlp_w_down, v_norm1, v_norm2, v_final_norm, v_ret_w_in, v_ret_gn_gain, v_ret_w_out, v_gdn_w_in, v_gdn_conv_w, v_gdn_a_log, v_gdn_dt_bias, v_gdn_norm_gain, v_gdn_w_out, v_gla_w_in, v_gla_w_gate_up, v_gla_gate_bias, v_gla_norm_gain, v_gla_w_out, v_lru_w_in, v_lru_conv_w, v_lru_conv_b, v_lru_w_rgate, v_lru_b_rgate, v_lru_w_igate, v_lru_b_igate, v_lru_lambda, v_lru_w_out, v_mlp_w_up, v_mlp_w_down)
```

```python
import functools
import math

import numpy as np
import jax
import jax.numpy as jnp
from jax import lax
from jax.experimental import pallas as pl
from jax.experimental.pallas import tpu as pltpu

F32 = jnp.float32
BF16 = jnp.bfloat16

D_MODEL = 2048
DEPTH = 4
CHUNK = 64
D_FF = 4 * D_MODEL
NORM_EPS = 1e-6
N_DEV = 8

RET_HEADS, RET_DK, RET_DV = 8, 256, 512
RET_HB = 4
GDN_HEADS, GDN_DK, GDN_DV = 16, 128, 128
GDN_HB = 8
GDN_QKV = GDN_HEADS * (2 * GDN_DK + GDN_DV)
CONV_WIDTH = 4
GLA_HEADS, GLA_DK, GLA_DV = 4, 256, 512
GLA_HB = 4
GLA_GATE_RANK = 16
GLA_TAU = 16.0
LRU_WIDTH, LRU_BLOCKS, LRU_BLOCK = 2048, 16, 128
LRU_C = 8.0
ROPE_BASE = 10000.0

ADAM_LR, ADAM_B1, ADAM_B2, ADAM_EPS, ADAM_WD, ADAM_STEP = 0.001, 0.9, 0.999, 1e-08, 0.01, 10

LANE = 128
VMEM_LIMIT = 48 * 1024 * 1024

NN = (((1,), (0,)), ((), ()))
NT = (((1,), (1,)), ((), ()))
TN = (((0,), (0,)), ((), ()))


def _params(sem=None):
    return pltpu.CompilerParams(dimension_semantics=sem, vmem_limit_bytes=VMEM_LIMIT)


def _bdot(a, b, dn=NN):
    return lax.dot_general(a.astype(BF16), b.astype(BF16), dn, preferred_element_type=F32)


def _split(x):
    hi = x.astype(BF16)
    lo = (x - hi.astype(F32)).astype(BF16)
    return hi, lo


def _fdot(a, b, dn=NN):
    a1, a2 = _split(a)
    b1, b2 = _split(b)
    d = functools.partial(lax.dot_general, dimension_numbers=dn, preferred_element_type=F32)
    return d(a1, b1) + (d(a1, b2) + d(a2, b1))


def _sigmoid(x):
    return 1.0 / (1.0 + jnp.exp(-x))


def _softplus(x):
    return jnp.maximum(x, 0.0) + jnp.log(1.0 + jnp.exp(-jnp.abs(x)))


def _silu(x):
    return x * _sigmoid(x)


def _dsilu(x):
    s = _sigmoid(x)
    return s * (1.0 + x * (1.0 - s))


GELU_C = math.sqrt(2.0 / math.pi)


def _gelu(x):
    return 0.5 * x * (1.0 + jnp.tanh(GELU_C * (x + 0.044715 * x * x * x)))


def _dgelu(x):
    t = jnp.tanh(GELU_C * (x + 0.044715 * x * x * x))
    return 0.5 * (1.0 + t) + 0.5 * x * (1.0 - t * t) * GELU_C * (1.0 + 3.0 * 0.044715 * x * x)


def _expm1(x):
    poly = x * (1.0 + x * 0.5 * (1.0 + x * (1.0 / 3.0) * (1.0 + x * 0.25 * (1.0 + x * 0.2))))
    return jnp.where(jnp.abs(x) < 0.05, poly, jnp.exp(x) - 1.0)


def _iota2(shape, axis):
    return lax.broadcasted_iota(jnp.int32, shape, axis)


def _col_to_row(col):
    n = col.shape[0]
    eye = _iota2((n, n), 0) == _iota2((n, n), 1)
    return jnp.sum(jnp.where(eye, col, 0.0), axis=0, keepdims=True)


def _row_to_col(row):
    n = row.shape[1]
    eye = _iota2((n, n), 0) == _iota2((n, n), 1)
    return jnp.sum(jnp.where(eye, row, 0.0), axis=1, keepdims=True)


def _pick_row(x, r):
    rows = _iota2(x.shape, 0)
    return jnp.sum(jnp.where(rows == r, x, 0.0), axis=0, keepdims=True)


def _shift_down(x, s):
    if s == 0:
        return x
    y = pltpu.roll(x, s, 0)
    return jnp.where(_iota2(x.shape, 0) < s, 0.0, y)


def _shift_up(x, s):
    if s == 0:
        return x
    n = x.shape[0]
    y = pltpu.roll(x, n - s, 0)
    return jnp.where(_iota2(x.shape, 0) >= n - s, 0.0, y)


def _tile(dim, prefs):
    for p in prefs:
        if dim % p == 0:
            return p
    return dim


TOKEN_SHAPE = (8, LANE)


def _dep_specs(deps):
    return [pl.BlockSpec(TOKEN_SHAPE, lambda *_: (0, 0)) for _ in deps]


def _matmul(a, b, *, ta=False, tb=False, epi="none", extra=None, out_dtype=F32, name, deps=(), b_cols=None):
    if ta:
        K, M = a.shape
    else:
        M, K = a.shape
    if tb:
        N, K2 = b.shape[0], b_cols or b.shape[1]
    else:
        K2, N = b.shape[0], b_cols or b.shape[1]
    assert K == K2, (a.shape, b.shape, ta, tb)
    if K <= 2048:
        tk = K
        if N <= 2048:
            tm, tn = _tile(M, (512, 256, 128)), N
        else:
            tm, tn = _tile(M, (1024, 512, 256, 128)), _tile(N, (512, 256, 128))
    else:
        tm, tn, tk = (_tile(d, (1024, 512, 256, 128)) for d in (M, N, K))
    nk = K // tk
    dn = (((0 if ta else 1,), (1 if tb else 0,)), ((), ()))
    n_extra = 0 if extra is None else 1
    n_out = 2 if epi == "relu2" else 1

    def body(*refs):
        a_ref, b_ref = refs[0], refs[1]
        e_ref = refs[2] if n_extra else None
        outs = refs[2 + n_extra + len(deps):2 + n_extra + len(deps) + n_out]

        def finish(r):
            if epi == "none":
                outs[0][...] = r.astype(outs[0].dtype)
            elif epi == "add":
                outs[0][...] = (r + e_ref[...]).astype(outs[0].dtype)
            elif epi == "relu2":
                outs[0][...] = r
                p = jnp.maximum(r, 0.0)
                outs[1][...] = (p * p).astype(outs[1].dtype)
            elif epi == "drelu2":
                outs[0][...] = (r * 2.0 * jnp.maximum(e_ref[...], 0.0)).astype(outs[0].dtype)

        def product():
            return lax.dot_general(a_ref[...].astype(BF16), b_ref[...].astype(BF16), dn, preferred_element_type=F32)

        if nk == 1:
            finish(product())
            return
        acc = refs[-1]
        k = pl.program_id(2)

        @pl.when(k == 0)
        def _():
            acc[...] = jnp.zeros_like(acc)

        acc[...] += product()

        @pl.when(k == nk - 1)
        def _():
            finish(acc[...])

    a_spec = pl.BlockSpec((tk, tm), lambda i, j, k: (k, i)) if ta else pl.BlockSpec((tm, tk), lambda i, j, k: (i, k))
    b_spec = pl.BlockSpec((tn, tk), lambda i, j, k: (j, k)) if tb else pl.BlockSpec((tk, tn), lambda i, j, k: (k, j))
    o_spec = pl.BlockSpec((tm, tn), lambda i, j, k: (i, j))
    in_specs = [a_spec, b_spec] + ([o_spec] if n_extra else []) + _dep_specs(deps)
    if epi == "relu2":
        out_shape = (jax.ShapeDtypeStruct((M, N), F32), jax.ShapeDtypeStruct((M, N), BF16))
        out_specs = (o_spec, o_spec)
    else:
        out_shape = jax.ShapeDtypeStruct((M, N), out_dtype)
        out_specs = o_spec
    args = (a, b) + ((extra,) if n_extra else ()) + tuple(deps)
    return pl.pallas_call(
        body, name=name, grid=(M // tm, N // tn, nk), in_specs=in_specs, out_specs=out_specs,
        out_shape=out_shape, scratch_shapes=[pltpu.VMEM((tm, tn), F32)] if nk > 1 else [],
        compiler_params=_params(("parallel", "parallel", "arbitrary")),
    )(*args)


ROW_BLOCK = 256


def _rmsnorm_fwd(x, g, *, name, deps=()):
    T, D = x.shape
    tr = _tile(T, (ROW_BLOCK, 128, 64))

    def body(x_ref, g_ref, *rest):
        o_ref = rest[-1]
        xv = x_ref[...]
        r = lax.rsqrt(jnp.mean(xv * xv, axis=-1, keepdims=True) + NORM_EPS)
        o_ref[...] = (xv * r * g_ref[...]).astype(o_ref.dtype)

    return pl.pallas_call(
        body, name=name, grid=(T // tr,),
        in_specs=[pl.BlockSpec((tr, D), lambda i: (i, 0)), pl.BlockSpec((1, D), lambda i: (0, 0))] + _dep_specs(deps),
        out_specs=pl.BlockSpec((tr, D), lambda i: (i, 0)),
        out_shape=jax.ShapeDtypeStruct((T, D), BF16), compiler_params=_params(("parallel",)),
    )(x, g, *deps)


def _rmsnorm_bwd(x, g, dy, dres, *, name, deps=()):
    T, D = x.shape
    tr = _tile(T, (ROW_BLOCK, 128, 64))

    def body(x_ref, g_ref, dy_ref, dres_ref, *rest):
        dx_ref, dxb_ref, dg_ref = rest[len(deps):]
        i = pl.program_id(0)
        xv = x_ref[...]
        r = lax.rsqrt(jnp.mean(xv * xv, axis=-1, keepdims=True) + NORM_EPS)
        xh = xv * r
        dyv = dy_ref[...].astype(F32)
        dxh = dyv * g_ref[...]
        dx = dres_ref[...] + r * (dxh - xh * jnp.mean(dxh * xh, axis=-1, keepdims=True))
        dx_ref[...] = dx
        dxb_ref[...] = dx.astype(dxb_ref.dtype)

        @pl.when(i == 0)
        def _():
            dg_ref[...] = jnp.zeros_like(dg_ref)

        dg_ref[...] += jnp.sum(dyv * xh, axis=0, keepdims=True)

    blk = pl.BlockSpec((tr, D), lambda i: (i, 0))
    vec = pl.BlockSpec((1, D), lambda i: (0, 0))
    return pl.pallas_call(
        body, name=name, grid=(T // tr,), in_specs=[blk, vec, blk, blk] + _dep_specs(deps), out_specs=(blk, blk, vec),
        out_shape=(jax.ShapeDtypeStruct((T, D), F32), jax.ShapeDtypeStruct((T, D), BF16),
                   jax.ShapeDtypeStruct((1, D), F32)),
        compiler_params=_params(("arbitrary",)),
    )(x, g, dy, dres, *deps)


def _final_loss_bwd(x, g, target, *, name):
    T, D = x.shape
    tr = _tile(T, (ROW_BLOCK, 128, 64))

    def body(x_ref, g_ref, t_ref, dx_ref, dxb_ref, dg_ref, l_ref):
        i = pl.program_id(0)
        xv = x_ref[...]
        r = lax.rsqrt(jnp.mean(xv * xv, axis=-1, keepdims=True) + NORM_EPS)
        xh = xv * r
        gv = g_ref[...]
        err = xh * gv - t_ref[...]
        dy = err * (1.0 / D)
        dxh = dy * gv
        dx = r * (dxh - xh * jnp.mean(dxh * xh, axis=-1, keepdims=True))
        dx_ref[...] = dx
        dxb_ref[...] = dx.astype(dxb_ref.dtype)

        @pl.when(i == 0)
        def _():
            l_ref[...] = jnp.zeros_like(l_ref)
            dg_ref[...] = jnp.zeros_like(dg_ref)

        dg_ref[...] += jnp.sum(dy * xh, axis=0, keepdims=True)
        part = 0.5 * jnp.sum(jnp.mean(err * err, axis=-1, keepdims=True), axis=0, keepdims=True)
        l_ref[...] += jnp.broadcast_to(part, l_ref.shape)

    blk = pl.BlockSpec((tr, D), lambda i: (i, 0))
    vec = pl.BlockSpec((1, D), lambda i: (0, 0))
    return pl.pallas_call(
        body, name=name, grid=(T // tr,), in_specs=[blk, vec, blk],
        out_specs=(blk, blk, vec, pl.BlockSpec((1, LANE), lambda i: (0, 0))),
        out_shape=(jax.ShapeDtypeStruct((T, D), F32), jax.ShapeDtypeStruct((T, D), BF16),
                   jax.ShapeDtypeStruct((1, D), F32), jax.ShapeDtypeStruct((1, LANE), F32)),
        compiler_params=_params(("arbitrary",)),
    )(x, g, target)


def _colsum(x, *, name):
    T, C = x.shape
    tc = _tile(C, (512, 256, 128))

    def body(x_ref, o_ref):
        o_ref[...] = jnp.sum(x_ref[...], axis=0, keepdims=True)

    return pl.pallas_call(
        body, name=name, grid=(C // tc,), in_specs=[pl.BlockSpec((T, tc), lambda j: (0, j))],
        out_specs=pl.BlockSpec((1, tc), lambda j: (0, j)),
        out_shape=jax.ShapeDtypeStruct((1, C), F32), compiler_params=_params(("parallel",)),
    )(x)


ADAM_C1 = 1.0 - ADAM_B1 ** ADAM_STEP
ADAM_C2 = 1.0 - ADAM_B2 ** ADAM_STEP


def _adamw(w, m, v, grads, *, name):
    R, C = w.shape
    tr = _tile(R, (256, 128, 64, 32, 16, 8))
    n_g = len(grads)

    def body(*refs):
        w_ref, m_ref, v_ref = refs[:3]
        g_refs = refs[3:3 + n_g]
        g_out, d_out, m_out, v_out = refs[3 + n_g:]
        g = g_refs[0][...].astype(F32)
        for r in g_refs[1:]:
            g = g + r[...].astype(F32)
        mn = ADAM_B1 * m_ref[...] + (1.0 - ADAM_B1) * g
        vn = ADAM_B2 * v_ref[...] + (1.0 - ADAM_B2) * (g * g)
        m_hat = mn / ADAM_C1
        v_hat = vn / ADAM_C2
        g_out[...] = g
        d_out[...] = -ADAM_LR * (m_hat / (jnp.sqrt(v_hat) + ADAM_EPS) + ADAM_WD * w_ref[...])
        m_out[...] = mn
        v_out[...] = vn

    blk = pl.BlockSpec((tr, C), lambda i: (i, 0))
    sh = jax.ShapeDtypeStruct((R, C), F32)
    return pl.pallas_call(
        body, name=name, grid=(R // tr,), in_specs=[blk] * (3 + n_g), out_specs=(blk,) * 4,
        out_shape=(sh,) * 4, compiler_params=_params(("parallel",)),
    )(w, m, v, *grads)


def _sum_parts(parts, *, name, deps=()):
    P, R, C = parts.shape
    tr = _tile(R, (256, 128, 64, 32, 16, 8))

    def body(p_ref, *rest):
        o_ref = rest[-1]
        s = p_ref[0].astype(F32)
        for i in range(1, P):
            s = s + p_ref[i].astype(F32)
        o_ref[...] = s

    return pl.pallas_call(
        body, name=name, grid=(R // tr,),
        in_specs=[pl.BlockSpec((P, tr, C), lambda i: (0, i, 0))] + _dep_specs(deps),
        out_specs=pl.BlockSpec((tr, C), lambda i: (i, 0)),
        out_shape=jax.ShapeDtypeStruct((R, C), F32), compiler_params=_params(("parallel",)),
    )(parts, *deps)


def _ret_tables(T):
    H, C = RET_HEADS, CHUNK
    log_gamma = jnp.log1p(-jnp.exp2(-5.0 - jnp.arange(H, dtype=F32)))
    pos = jnp.arange(C, dtype=F32)
    dist = jnp.abs(pos[:, None] - pos[None, :])
    dm = jnp.exp(log_gamma[:, None, None] * dist)
    qdec = jnp.exp(log_gamma[:, None] * (pos + 1.0))[:, :, None]
    kdec = jnp.exp(log_gamma[:, None] * (C - 1.0 - pos))[:, :, None]
    cdec = jnp.exp(log_gamma * C)[:, None, None]
    inv = ROPE_BASE ** (-jnp.arange(0, RET_DK, 2, dtype=F32) / RET_DK)
    ang = jnp.arange(T, dtype=F32)[:, None] * inv[None, :]
    return dm, qdec, kdec, cdec, jnp.cos(ang), jnp.sin(ang)


def _rot(x, cos, sin):
    h = x.shape[1] // 2
    x1, x2 = x[:, :h], x[:, h:]
    return jnp.concatenate([x1 * cos - x2 * sin, x1 * sin + x2 * cos], axis=1)


def _unrot(dy, cos, sin):
    h = dy.shape[1] // 2
    d1, d2 = dy[:, :h], dy[:, h:]
    return jnp.concatenate([d1 * cos + d2 * sin, d2 * cos - d1 * sin], axis=1)


def _ret_specs(N, rev):
    H, C, DK, DV = RET_HEADS, CHUNK, RET_DK, RET_DV
    cn = (lambda n: N - 1 - n) if rev else (lambda n: n)
    HB, G = RET_HB, H // RET_HB
    q = pl.BlockSpec((C, HB * DK), lambda h, n: (cn(n), h))
    k = pl.BlockSpec((C, HB * DK), lambda h, n: (cn(n), G + h))
    v = pl.BlockSpec((C, HB * DV), lambda h, n: (cn(n), G + h))
    g = pl.BlockSpec((C, HB * DV), lambda h, n: (cn(n), 2 * G + h))
    cs = pl.BlockSpec((C, DK // 2), lambda h, n: (cn(n), 0))
    dm = pl.BlockSpec((HB, C, C), lambda h, n: (h, 0, 0))
    dec = pl.BlockSpec((HB, C, 1), lambda h, n: (h, 0, 0))
    cd = pl.BlockSpec((HB, 1, 1), lambda h, n: (h, 0, 0))
    gain = pl.BlockSpec((HB, 1, DV), lambda h, n: (h, 0, 0))
    st = pl.BlockSpec((HB, 1, DK, DV), lambda h, n: (h, cn(n), 0, 0))
    ov = pl.BlockSpec((C, HB * DV), lambda h, n: (cn(n), h))
    return q, k, v, g, cs, dm, dec, cd, gain, st, ov


def _ret_fwd(proj, gain, tables, *, name):
    T = proj.shape[0]
    H, C, DK, DV = RET_HEADS, CHUNK, RET_DK, RET_DV
    N = T // C
    dm_t, qdec_t, kdec_t, cdec_t, cos_t, sin_t = tables

    def body(q_ref, k_ref, v_ref, g_ref, cos_ref, sin_ref, dm_ref, qd_ref, kd_ref, cd_ref, gain_ref,
             og_ref, st_ref, s_acc):
        n = pl.program_id(1)

        @pl.when(n == 0)
        def _():
            s_acc[...] = jnp.zeros_like(s_acc)

        cos, sin = cos_ref[...], sin_ref[...]

        def head(hb):
            kc, vc = pl.ds(hb * DK, DK), pl.ds(hb * DV, DV)
            qr = _rot(q_ref[:, kc], cos, sin)
            kr = _rot(k_ref[:, kc], cos, sin) * (DK ** -0.5)
            v = v_ref[:, vc]
            sp = s_acc[hb]
            st_ref[hb, 0] = sp.astype(st_ref.dtype)
            scores = _bdot(qr, kr, NT) * dm_ref[hb]
            inter = _bdot(qr * qd_ref[hb], sp)
            s_acc[hb] = sp * cd_ref[hb] + _bdot(kr * kd_ref[hb], v, TN)
            yield
            o = _bdot(scores, v) + inter
            yield
            oc = o - jnp.mean(o, axis=-1, keepdims=True)
            rstd = lax.rsqrt(jnp.mean(oc * oc, axis=-1, keepdims=True) + NORM_EPS)
            og_ref[:, vc] = (oc * rstd * gain_ref[hb] * _silu(g_ref[:, vc])).astype(og_ref.dtype)

        _interleave([head(hb) for hb in range(RET_HB)])

    q, k, v, g, cs, dm, dec, cd, gn, st, ov = _ret_specs(N, False)
    return pl.pallas_call(
        body, name=name, grid=(H // RET_HB, N),
        in_specs=[q, k, v, g, cs, cs, dm, dec, dec, cd, gn], out_specs=(ov, st),
        out_shape=(jax.ShapeDtypeStruct((T, H * DV), BF16), jax.ShapeDtypeStruct((H, N, DK, DV), BF16)),
        scratch_shapes=[pltpu.VMEM((RET_HB, DK, DV), F32)],
        compiler_params=_params(("arbitrary", "arbitrary")),
    )(proj, proj, proj, proj, cos_t, sin_t, dm_t, qdec_t, kdec_t, cdec_t, gain)


def _ret_bwd(proj, gain, tables, states, dog, *, name):
    T = proj.shape[0]
    H, C, DK, DV = RET_HEADS, CHUNK, RET_DK, RET_DV
    N = T // C
    dm_t, qdec_t, kdec_t, cdec_t, cos_t, sin_t = tables

    def body(q_ref, k_ref, v_ref, g_ref, cos_ref, sin_ref, dm_ref, qd_ref, kd_ref, cd_ref, gain_ref,
             st_ref, dog_ref, dq_ref, dk_ref, dv_ref, dg_ref, dgain_ref, ds_acc):
        n = pl.program_id(1)

        @pl.when(n == 0)
        def _():
            ds_acc[...] = jnp.zeros_like(ds_acc)
            dgain_ref[...] = jnp.zeros_like(dgain_ref)

        cos, sin = cos_ref[...], sin_ref[...]
        scale = DK ** -0.5

        def head(hb):
            kc, vc = pl.ds(hb * DK, DK), pl.ds(hb * DV, DV)
            qr = _rot(q_ref[:, kc], cos, sin)
            kr = _rot(k_ref[:, kc], cos, sin) * scale
            v = v_ref[:, vc]
            g = g_ref[:, vc]
            sp = st_ref[hb, 0]
            dm = dm_ref[hb]
            qd, kd = qd_ref[hb], kd_ref[hb]
            gain_v = gain_ref[hb]
            scores = _bdot(qr, kr, NT) * dm
            inter = _bdot(qr * qd, sp)
            yield
            o = _bdot(scores, v) + inter
            yield
            oc = o - jnp.mean(o, axis=-1, keepdims=True)
            rstd = lax.rsqrt(jnp.mean(oc * oc, axis=-1, keepdims=True) + NORM_EPS)
            oh = oc * rstd
            dy = dog_ref[:, vc].astype(F32)
            dg_ref[:, vc] = (dy * oh * gain_v * _dsilu(g)).astype(dg_ref.dtype)
            dnorm = dy * _silu(g)
            dgain_ref[hb] += jnp.sum(dnorm * oh, axis=0, keepdims=True)
            doh = dnorm * gain_v
            do = rstd * (doh - jnp.mean(doh, axis=-1, keepdims=True)
                         - oh * jnp.mean(doh * oh, axis=-1, keepdims=True))
            dsn = ds_acc[hb]
            dp = _bdot(do, v, NT) * dm
            dq_inter = _bdot(do, sp, NT) * qd
            dk_inter = _bdot(v, dsn, NT) * kd
            dv_ref[:, vc] = (_bdot(scores, do, TN) + _bdot(kr * kd, dsn)).astype(dv_ref.dtype)
            ds_acc[hb] = dsn * cd_ref[hb] + _bdot(qr * qd, do, TN)
            yield
            dqr = _bdot(dp, kr) + dq_inter
            dkr = _bdot(dp, qr, TN) + dk_inter
            yield
            dq_ref[:, kc] = _unrot(dqr, cos, sin).astype(dq_ref.dtype)
            dk_ref[:, kc] = _unrot(dkr * scale, cos, sin).astype(dk_ref.dtype)

        _interleave([head(hb) for hb in range(RET_HB)])

    q, k, v, g, cs, dm, dec, cd, gn, st, ov = _ret_specs(N, True)
    return pl.pallas_call(
        body, name=name, grid=(H // RET_HB, N),
        in_specs=[q, k, v, g, cs, cs, dm, dec, dec, cd, gn, st, ov],
        out_specs=(q, q, ov, ov, gn),
        out_shape=(jax.ShapeDtypeStruct((T, H * DK), BF16), jax.ShapeDtypeStruct((T, H * DK), BF16),
                   jax.ShapeDtypeStruct((T, H * DV), BF16), jax.ShapeDtypeStruct((T, H * DV), BF16),
                   jax.ShapeDtypeStruct((H, 1, DV), F32)),
        scratch_shapes=[pltpu.VMEM((RET_HB, DK, DV), F32)],
        compiler_params=_params(("arbitrary", "arbitrary")),
    )(proj, proj, proj, proj, cos_t, sin_t, dm_t, qdec_t, kdec_t, cdec_t, gain, states, dog)


def _gla_specs(N, rev):
    H, C, DK, DV = GLA_HEADS, CHUNK, GLA_DK, GLA_DV
    cn = (lambda n: N - 1 - n) if rev else (lambda n: n)
    HB, G = GLA_HB, H // GLA_HB
    q = pl.BlockSpec((C, HB * DK), lambda h, n: (cn(n), h))
    k = pl.BlockSpec((C, HB * DK), lambda h, n: (cn(n), G + h))
    v = pl.BlockSpec((C, HB * DV), lambda h, n: (cn(n), G + h))
    r = pl.BlockSpec((C, HB * DV), lambda h, n: (cn(n), 2 * G + h))
    bias = pl.BlockSpec((1, HB * DK), lambda h, n: (0, h))
    gain = pl.BlockSpec((HB, 1, DV), lambda h, n: (h, 0, 0))
    st = pl.BlockSpec((HB, 1, DV, DK), lambda h, n: (h, cn(n), 0, 0))
    ov = pl.BlockSpec((C, HB * DV), lambda h, n: (cn(n), h))
    return q, k, v, r, bias, gain, st, ov


def _gla_chunk(q, k, v, gl_raw, bias):
    C, DK = q.shape
    gl = gl_raw + bias
    la = -_softplus(-gl) * (1.0 / GLA_TAU)
    lower = _iota2((C, C), 0) >= _iota2((C, C), 1)
    cum = _fdot(jnp.where(lower, 1.0, 0.0), la)
    yield
    ref = _pick_row(cum, C // 2 - 1)
    clast = _pick_row(cum, C - 1)
    fw, bw = jnp.exp(cum - ref), jnp.exp(ref - cum)
    qs = q * (DK ** -0.5)
    s_lo = _bdot(qs * fw, k * bw, NT)
    s_up = _bdot(qs * bw, k * fw, NT)
    yield
    scores = jnp.where(lower, s_lo, s_up)
    return gl, cum, clast, fw, bw, qs, k, v, scores, lower


def _gla_fwd(proj, glogit, bias, gain, *, name):
    T = proj.shape[0]
    H, C, DK, DV = GLA_HEADS, CHUNK, GLA_DK, GLA_DV
    N = T // C

    def body(q_ref, k_ref, v_ref, r_ref, gl_ref, bias_ref, gain_ref, og_ref, st_ref, s_acc):
        n = pl.program_id(1)

        @pl.when(n == 0)
        def _():
            s_acc[...] = jnp.zeros_like(s_acc)

        def head(hb):
            kc, vc = pl.ds(hb * DK, DK), pl.ds(hb * DV, DV)
            gl, cum, clast, fw, bw, qs, k, v, scores, lower = yield from _gla_chunk(
                q_ref[:, kc], k_ref[:, kc], v_ref[:, vc], gl_ref[:, kc], bias_ref[:, kc])
            sp = s_acc[hb]
            st_ref[hb, 0] = sp
            o = _bdot(scores, v) + _bdot(qs * jnp.exp(cum), sp, NT)
            s_acc[hb] = sp * jnp.exp(clast) + _bdot(v, k * jnp.exp(clast - cum), TN)
            yield
            rstd = lax.rsqrt(jnp.mean(o * o, axis=-1, keepdims=True) + NORM_EPS)
            og_ref[:, vc] = (o * rstd * gain_ref[hb] * _silu(r_ref[:, vc])).astype(og_ref.dtype)

        _interleave([head(hb) for hb in range(GLA_HB)])

    q, k, v, r, bias_s, gn, st, ov = _gla_specs(N, False)
    return pl.pallas_call(
        body, name=name, grid=(H // GLA_HB, N), in_specs=[q, k, v, r, q, bias_s, gn], out_specs=(ov, st),
        out_shape=(jax.ShapeDtypeStruct((T, H * DV), BF16), jax.ShapeDtypeStruct((H, N, DV, DK), F32)),
        scratch_shapes=[pltpu.VMEM((GLA_HB, DV, DK), F32)],
        compiler_params=_params(("arbitrary", "arbitrary")),
    )(proj, proj, proj, proj, glogit, bias, gain)


def _gla_bwd(proj, glogit, bias, gain, states, dog, *, name):
    T = proj.shape[0]
    H, C, DK, DV = GLA_HEADS, CHUNK, GLA_DK, GLA_DV
    N = T // C

    def body(q_ref, k_ref, v_ref, r_ref, gl_ref, bias_ref, gain_ref, st_ref, dog_ref,
             dq_ref, dk_ref, dv_ref, dr_ref, dgl_ref, dgain_ref, ds_acc):
        n = pl.program_id(1)

        @pl.when(n == 0)
        def _():
            ds_acc[...] = jnp.zeros_like(ds_acc)
            dgain_ref[...] = jnp.zeros_like(dgain_ref)

        def head(hb):
            kc, vc = pl.ds(hb * DK, DK), pl.ds(hb * DV, DV)
            gl, cum, clast, fw, bw, qs, k, v, scores, lower = yield from _gla_chunk(
                q_ref[:, kc], k_ref[:, kc], v_ref[:, vc], gl_ref[:, kc], bias_ref[:, kc])
            sp = st_ref[hb, 0]
            ecum, e2, cdec = jnp.exp(cum), jnp.exp(clast - cum), jnp.exp(clast)
            q_in, k_end = qs * ecum, k * e2
            o = _bdot(scores, v) + _bdot(q_in, sp, NT)
            yield
            rstd = lax.rsqrt(jnp.mean(o * o, axis=-1, keepdims=True) + NORM_EPS)
            oh = o * rstd
            r = r_ref[:, vc]
            gain_v = gain_ref[hb]
            dy = dog_ref[:, vc].astype(F32)
            dr_ref[:, vc] = (dy * oh * gain_v * _dsilu(r)).astype(dr_ref.dtype)
            dnorm = dy * _silu(r)
            dgain_ref[hb] += jnp.sum(dnorm * oh, axis=0, keepdims=True)
            doh = dnorm * gain_v
            do = rstd * (doh - oh * jnp.mean(doh * oh, axis=-1, keepdims=True))
            dsn = ds_acc[hb]
            dq_in = _bdot(do, sp)
            dk_end = _bdot(v, dsn)
            dv_ref[:, vc] = (_bdot(k_end, dsn, NT) + _bdot(scores, do, TN)).astype(dv_ref.dtype)
            dcdec = jnp.sum(dsn * sp, axis=0, keepdims=True)
            ds_acc[hb] = dsn * cdec + _bdot(do, q_in, TN)
            dsc = _bdot(do, v, NT)
            yield
            ds_lo = jnp.where(lower, dsc, 0.0)
            ds_up = jnp.where(lower, 0.0, dsc)
            qf, kb, qb, kf = qs * fw, k * bw, qs * bw, k * fw
            dqf, dkb = _bdot(ds_lo, kb), _bdot(ds_lo, qf, TN)
            dqb, dkf = _bdot(ds_up, kf), _bdot(ds_up, qb, TN)
            yield
            dq_ref[:, kc] = ((dqf * fw + dqb * bw + dq_in * ecum) * (DK ** -0.5)).astype(dq_ref.dtype)
            dk_ref[:, kc] = (dkb * bw + dkf * fw + dk_end * e2).astype(dk_ref.dtype)
            dz = (dqf * qs + dkf * k) * fw - (dqb * qs + dkb * k) * bw
            kk = dk_end * k_end
            dcum = dz + dq_in * q_in - kk
            rows = _iota2((C, DK), 0)
            dcum = dcum + jnp.where(rows == C // 2 - 1, -jnp.sum(dz, axis=0, keepdims=True), 0.0)
            dcum = dcum + jnp.where(rows == C - 1, jnp.sum(kk, axis=0, keepdims=True) + dcdec * cdec, 0.0)
            upper = _iota2((C, C), 0) <= _iota2((C, C), 1)
            dla = _fdot(jnp.where(upper, 1.0, 0.0), dcum)
            yield
            dgl_ref[:, kc] = dla * (1.0 / GLA_TAU) * _sigmoid(-gl)

        _interleave([head(hb) for hb in range(GLA_HB)])

    q, k, v, r, bias_s, gn, st, ov = _gla_specs(N, True)
    return pl.pallas_call(
        body, name=name, grid=(H // GLA_HB, N), in_specs=[q, k, v, r, q, bias_s, gn, st, ov],
        out_specs=(q, q, ov, ov, q, gn),
        out_shape=(jax.ShapeDtypeStruct((T, H * DK), BF16), jax.ShapeDtypeStruct((T, H * DK), BF16),
                   jax.ShapeDtypeStruct((T, H * DV), BF16), jax.ShapeDtypeStruct((T, H * DV), BF16),
                   jax.ShapeDtypeStruct((T, H * DK), F32), jax.ShapeDtypeStruct((H, 1, DV), F32)),
        scratch_shapes=[pltpu.VMEM((GLA_HB, DV, DK), F32)],
        compiler_params=_params(("arbitrary", "arbitrary")),
    )(proj, proj, proj, proj, glogit, bias, gain, states, dog)


def _conv(xv, w_ref):
    out = _shift_down(xv, CONV_WIDTH - 1) * w_ref[0:1, :]
    for tap in range(1, CONV_WIDTH):
        out = out + _shift_down(xv, CONV_WIDTH - 1 - tap) * w_ref[tap:tap + 1, :]
    return out


def _conv_bwd(xv, w_ref, dpre, dw_ref):
    dx = None
    for tap in range(CONV_WIDTH):
        s = CONV_WIDTH - 1 - tap
        t = _shift_up(dpre, s) * w_ref[tap:tap + 1, :]
        dx = t if dx is None else dx + t
        dw_ref[tap:tap + 1, :] = jnp.sum(dpre * _shift_down(xv, s), axis=0, keepdims=True)
    return dx


CONV_COLS = 256


def _conv_silu_fwd(x, w, *, name):
    T = x.shape[0]
    n = w.shape[1]

    def body(x_ref, w_ref, o_ref):
        o_ref[...] = _silu(_conv(x_ref[...], w_ref))

    return pl.pallas_call(
        body, name=name, grid=(n // CONV_COLS,),
        in_specs=[pl.BlockSpec((T, CONV_COLS), lambda j: (0, j)), pl.BlockSpec((CONV_WIDTH, CONV_COLS), lambda j: (0, j))],
        out_specs=pl.BlockSpec((T, CONV_COLS), lambda j: (0, j)),
        out_shape=jax.ShapeDtypeStruct((T, n), F32), compiler_params=_params(("parallel",)),
    )(x, w)


def _conv_silu_bwd(x, w, dact, *, name):
    T = x.shape[0]
    n = w.shape[1]

    def body(x_ref, w_ref, da_ref, dx_ref, dw_ref):
        xv = x_ref[...]
        dpre = da_ref[...] * _dsilu(_conv(xv, w_ref))
        dx_ref[...] = _conv_bwd(xv, w_ref, dpre, dw_ref).astype(dx_ref.dtype)

    blk = pl.BlockSpec((T, CONV_COLS), lambda j: (0, j))
    wb = pl.BlockSpec((CONV_WIDTH, CONV_COLS), lambda j: (0, j))
    return pl.pallas_call(
        body, name=name, grid=(n // CONV_COLS,), in_specs=[blk, wb, blk], out_specs=(blk, wb),
        out_shape=(jax.ShapeDtypeStruct((T, n), BF16), jax.ShapeDtypeStruct((CONV_WIDTH, n), F32)),
        compiler_params=_params(("parallel",)),
    )(x, w, dact)


def _interleave(gens):
    results = [None] * len(gens)
    live = list(range(len(gens)))
    while live:
        for i in list(live):
            try:
                next(gens[i])
            except StopIteration as done:
                results[i] = done.value
                live.remove(i)
    return results


def _unit_lower_inverse(a):
    n = a.shape[0]
    eye = jnp.where(_iota2((n, n), 0) == _iota2((n, n), 1), 1.0, 0.0)
    p = -a
    t = eye + p
    for _ in range(5):
        p = _fdot(p, p)
        yield
        t = t + _fdot(t, p)
        yield
    return t


def _gdn_specs(N, rev):
    H, C, DK, DV = GDN_HEADS, CHUNK, GDN_DK, GDN_DV
    cn = (lambda n: N - 1 - n) if rev else (lambda n: n)
    HB, G = GDN_HB, H // GDN_HB
    q = pl.BlockSpec((C, HB * DK), lambda h, n: (cn(n), h))
    k = pl.BlockSpec((C, HB * DK), lambda h, n: (cn(n), G + h))
    v = pl.BlockSpec((C, HB * DV), lambda h, n: (cn(n), 2 * G + h))
    z = pl.BlockSpec((C, HB * DV), lambda h, n: (cn(n), 3 * G + h))
    gates = pl.BlockSpec((C, LANE), lambda h, n: (cn(n), 0))
    sc = pl.BlockSpec((HB, 1, 1), lambda h, n: (h, 0, 0))
    gain = pl.BlockSpec((1, DV), lambda h, n: (0, 0))
    st = pl.BlockSpec((HB, 1, DK, DV), lambda h, n: (h, cn(n), 0, 0))
    return q, k, v, z, gates, sc, gain, st


def _gdn_solved_specs(N, rev):
    C = CHUNK
    cn = (lambda n: N - 1 - n) if rev else (lambda n: n)
    return (pl.BlockSpec((GDN_HB, 1, C, C), lambda h, n: (h, cn(n), 0, 0)),
            pl.BlockSpec((GDN_HB, 1, C, GDN_DV + GDN_DK), lambda h, n: (h, cn(n), 0, 0)))


def _gdn_chunk(q_ref, k_ref, v_ref, gates_ref, alog_ref, dtb_ref, h, solved=None):
    H, C, DK, DV = GDN_HEADS, CHUNK, GDN_DK, GDN_DV
    gates = gates_ref[...]
    lane = _iota2(gates.shape, 1)
    bl = jnp.sum(jnp.where(lane == h, gates, 0.0), axis=1, keepdims=True)
    al = jnp.sum(jnp.where(lane == H + h, gates, 0.0), axis=1, keepdims=True)
    beta = _sigmoid(bl)
    ea = jnp.exp(alog_ref[...])
    xs = al + dtb_ref[...]
    la = -ea * _softplus(xs)
    ii, jj = _iota2((C, C), 0), _iota2((C, C), 1)
    strict = ii > jj
    cum_col = jnp.sum(jnp.where(ii >= jj, _col_to_row(la), 0.0), axis=1, keepdims=True)
    cum_row = jnp.sum(jnp.where(ii <= jj, la, 0.0), axis=0, keepdims=True)
    q, k, v = q_ref[...], k_ref[...], v_ref[...]
    rq = lax.rsqrt(jnp.sum(q * q, axis=-1, keepdims=True) + NORM_EPS)
    rk = lax.rsqrt(jnp.sum(k * k, axis=-1, keepdims=True) + NORM_EPS)
    qn = q * rq * (DK ** -0.5)
    kn = k * rk
    rel = jnp.where(strict, jnp.exp(jnp.where(strict, cum_col - cum_row, 0.0)), 0.0)
    rg = rel * _bdot(kn, kn, NT)
    yield
    a = beta * rg
    e_col = jnp.exp(cum_col)
    clast = _pick_row(cum_col, C - 1)
    if solved is None:
        tm = yield from _unit_lower_inverse(a)
        rhs = jnp.concatenate([beta * v, (beta * e_col) * kn], axis=1)
        sol = _fdot(tm, rhs)
        yield
    else:
        tm, sol = solved
    u, w = sol[:, :DV], sol[:, DV:]
    dd = jnp.exp(clast - cum_col)
    ke = kn * dd
    g = jnp.exp(clast)
    eye = jnp.where(_iota2((DK, DK), 0) == _iota2((DK, DK), 1), 1.0, 0.0)
    trans = g * eye - _bdot(ke, w, TN)
    inject = _bdot(ke, u, TN)
    yield
    return dict(beta=beta, ea=ea, xs=xs, la=la, strict=strict, ii=ii, jj=jj, q=q, k=k, v=v, rq=rq, rk=rk,
                qn=qn, kn=kn, rel=rel, rg=rg, a=a, tm=tm, e_col=e_col, sol=sol, u=u, w=w, dd=dd, ke=ke,
                g=g, eye=eye, trans=trans, inject=inject)


def _gdn_fwd(act, proj, gates, a_log, dt_bias, gain, *, name):
    T = act.shape[0]
    H, C, DK, DV = GDN_HEADS, CHUNK, GDN_DK, GDN_DV
    N = T // C

    def body(q_ref, k_ref, v_ref, z_ref, gates_ref, alog_ref, dtb_ref, gain_ref, og_ref, st_ref, tm_ref, sol_ref,
             s_acc):
        hg, n = pl.program_id(0), pl.program_id(1)

        @pl.when(n == 0)
        def _():
            s_acc[...] = jnp.zeros_like(s_acc)

        def head(hb):
            cols = pl.ds(hb * DK, DK)
            c = yield from _gdn_chunk(q_ref.at[:, cols], k_ref.at[:, cols], v_ref.at[:, cols], gates_ref,
                                      alog_ref.at[hb], dtb_ref.at[hb], hg * GDN_HB + hb)
            tm_ref[hb, 0] = c["tm"]
            sol_ref[hb, 0] = c["sol"]
            sp = s_acc[hb]
            st_ref[hb, 0] = sp
            snew = _bdot(c["trans"], sp) + c["inject"]
            yield
            s_acc[hb] = snew
            o = _bdot(c["qn"], snew)
            yield
            rstd = lax.rsqrt(jnp.mean(o * o, axis=-1, keepdims=True) + NORM_EPS)
            og_ref[:, cols] = (o * rstd * gain_ref[...] * _silu(z_ref[:, cols])).astype(og_ref.dtype)

        _interleave([head(hb) for hb in range(GDN_HB)])

    q, k, v, z, gt, sc, gn, st = _gdn_specs(N, False)
    tm_s, sol_s = _gdn_solved_specs(N, False)
    return pl.pallas_call(
        body, name=name, grid=(H // GDN_HB, N), in_specs=[q, k, v, z, gt, sc, sc, gn],
        out_specs=(q, st, tm_s, sol_s),
        out_shape=(jax.ShapeDtypeStruct((T, H * DV), BF16), jax.ShapeDtypeStruct((H, N, DK, DV), F32),
                   jax.ShapeDtypeStruct((H, N, C, C), F32), jax.ShapeDtypeStruct((H, N, C, DV + DK), F32)),
        scratch_shapes=[pltpu.VMEM((GDN_HB, DK, DV), F32)],
        compiler_params=_params(("arbitrary", "arbitrary")),
    )(act, act, act, proj, gates, a_log, dt_bias, gain)


def _gdn_bwd(act, proj, gates, a_log, dt_bias, gain, states, solved, dog, *, name):
    T = act.shape[0]
    H, C, DK, DV = GDN_HEADS, CHUNK, GDN_DK, GDN_DV
    N = T // C

    def rsum(x):
        return jnp.sum(x, axis=1, keepdims=True)

    def body(q_ref, k_ref, v_ref, z_ref, gates_ref, alog_ref, dtb_ref, gain_ref, st_ref, tm_ref, sol_ref, dog_ref,
             dq_ref, dk_ref, dv_ref, dz_ref, dgates_ref, dscal_ref, dgain_ref, ds_acc):
        hg, n = pl.program_id(0), pl.program_id(1)

        @pl.when(n == 0)
        def _():
            ds_acc[...] = jnp.zeros_like(ds_acc)
            dgain_ref[...] = jnp.zeros_like(dgain_ref)
            dscal_ref[...] = jnp.zeros_like(dscal_ref)

        _interleave([one_head(hb, hg * GDN_HB + hb, q_ref, k_ref, v_ref, z_ref, gates_ref, alog_ref, dtb_ref, gain_ref,
                              st_ref, tm_ref, sol_ref, dog_ref, dq_ref, dk_ref, dv_ref, dz_ref, dgates_ref, dscal_ref,
                              dgain_ref, ds_acc)
                     for hb in range(GDN_HB)])

    def one_head(hb, h, q_ref, k_ref, v_ref, z_ref, gates_ref, alog_ref, dtb_ref, gain_ref, st_ref, tm_ref, sol_ref,
                 dog_ref, dq_ref, dk_ref, dv_ref, dz_ref, dgates_ref, dscal_ref, dgain_ref, ds_acc):
        cols = pl.ds(hb * DK, DK)
        c = yield from _gdn_chunk(q_ref.at[:, cols], k_ref.at[:, cols], v_ref.at[:, cols], gates_ref,
                                  alog_ref.at[hb], dtb_ref.at[hb], h, solved=(tm_ref[hb, 0], sol_ref[hb, 0]))
        beta, kn, qn, v, ke, u, w, dd, e_col = c["beta"], c["kn"], c["qn"], c["v"], c["ke"], c["u"], c["w"], c["dd"], c["e_col"]
        sp = st_ref[hb, 0]
        snew = _bdot(c["trans"], sp) + c["inject"]
        yield
        o = _bdot(qn, snew)
        yield
        rstd = lax.rsqrt(jnp.mean(o * o, axis=-1, keepdims=True) + NORM_EPS)
        oh = o * rstd
        z = z_ref[:, cols]
        gain_v = gain_ref[...]
        dy = dog_ref[:, cols].astype(F32)
        dz_ref[:, cols] = (dy * oh * gain_v * _dsilu(z)).astype(dz_ref.dtype)
        dnorm = dy * _silu(z)
        dgain_ref[hb] += jnp.sum(dnorm * oh, axis=0, keepdims=True)
        doh = dnorm * gain_v
        do = rstd * (doh - oh * jnp.mean(doh * oh, axis=-1, keepdims=True))

        dstot = ds_acc[hb] + _bdot(qn, do, TN)
        dqn = _bdot(do, snew, NT)
        yield
        dtrans = _bdot(dstot, sp, NT)
        ds_acc[hb] = _bdot(c["trans"], dstot, TN)
        yield
        dg = jnp.sum(jnp.sum(dtrans * c["eye"], axis=1, keepdims=True), axis=0, keepdims=True)
        m = -dtrans
        dke = _bdot(w, m, NT) + _bdot(u, dstot, NT)
        dw = _bdot(ke, m)
        du = _bdot(ke, dstot)
        yield
        drhs = _fdot(c["tm"], jnp.concatenate([du, dw], axis=1), TN)
        yield
        da = jnp.where(c["strict"], -_fdot(drhs, c["sol"], NT), 0.0)
        yield
        drhs_u, drhs_w = drhs[:, :DV], drhs[:, DV:]
        rwk = rsum(drhs_w * kn)
        dbeta = rsum(da * c["rg"]) + rsum(drhs_u * v) + rwk * e_col
        dgm = da * beta * c["rel"]
        dkn = _bdot(dgm, kn) + _bdot(dgm, kn, TN) + (beta * e_col) * drhs_w + dd * dke
        yield
        dv_ref[:, cols] = beta * drhs_u
        r_ = da * c["a"]
        ddd = rsum(dke * kn)
        dc = rsum(r_) - _row_to_col(jnp.sum(r_, axis=0, keepdims=True)) + beta * rwk * e_col - ddd * dd
        dclast = jnp.sum(ddd * dd, axis=0, keepdims=True) + dg * c["g"]
        dc = dc + jnp.where(_iota2((C, 1), 0) == C - 1, dclast, 0.0)
        dla = jnp.sum(jnp.where(c["ii"] <= c["jj"], _col_to_row(dc), 0.0), axis=1, keepdims=True)
        dalog = jnp.sum(dla * c["la"], axis=0, keepdims=True)
        dxs = dla * (-c["ea"]) * _sigmoid(c["xs"])
        ddtb = jnp.sum(dxs, axis=0, keepdims=True)
        dbl = dbeta * beta * (1.0 - beta)
        lane = _iota2((C, LANE), 1)
        dgates_ref[hb] = jnp.where(lane == 0, dbl, jnp.where(lane == 1, dxs, 0.0))
        lane8 = _iota2((8, LANE), 1)
        dscal_ref[hb] += jnp.where(lane8 == 0, dalog, jnp.where(lane8 == 1, ddtb, 0.0))
        dk_ref[:, cols] = c["rk"] * (dkn - kn * rsum(dkn * kn))
        qh = c["q"] * c["rq"]
        dqs = dqn * (DK ** -0.5)
        dq_ref[:, cols] = c["rq"] * (dqs - qh * rsum(dqs * qh))

    q, k, v, z, gt, sc, gn, st = _gdn_specs(N, True)
    dgt = pl.BlockSpec((GDN_HB, C, LANE), lambda h, n: (h, N - 1 - n, 0))
    dsc = pl.BlockSpec((GDN_HB, 8, LANE), lambda h, n: (h, 0, 0))
    dgn = pl.BlockSpec((GDN_HB, 1, DV), lambda h, n: (h, 0, 0))
    sh = jax.ShapeDtypeStruct((T, H * DK), F32)
    tm_s, sol_s = _gdn_solved_specs(N, True)
    return pl.pallas_call(
        body, name=name, grid=(H // GDN_HB, N), in_specs=[q, k, v, z, gt, sc, sc, gn, st, tm_s, sol_s, q],
        out_specs=(q, q, q, q, dgt, dsc, dgn),
        out_shape=(sh, sh, sh, jax.ShapeDtypeStruct((T, H * DV), BF16),
                   jax.ShapeDtypeStruct((H, T, LANE), F32), jax.ShapeDtypeStruct((H, 8, LANE), F32),
                   jax.ShapeDtypeStruct((H, 1, DV), F32)),
        scratch_shapes=[pltpu.VMEM((GDN_HB, DK, DV), F32)],
        compiler_params=_params(("arbitrary", "arbitrary")),
    )(act, act, act, proj, gates, a_log, dt_bias, gain, states, *solved, dog)


SUBLANES = 8


def _linear_scan(a_ref, b_ref, h_ref, reverse):
    T, W = a_ref.shape
    nb = T // SUBLANES
    row = _iota2((SUBLANES, W), 0)

    def blk(bi, carry):
        bb = (nb - 1 - bi) if reverse else bi
        off = pl.multiple_of(bb * SUBLANES, SUBLANES)
        a = a_ref[pl.ds(off, SUBLANES), :]
        b = b_ref[pl.ds(off, SUBLANES), :]
        for d in (1, 2, 4):
            if reverse:
                edge = row >= SUBLANES - d
                a_sh = jnp.where(edge, 1.0, pltpu.roll(a, SUBLANES - d, 0))
                b_sh = jnp.where(edge, 0.0, pltpu.roll(b, SUBLANES - d, 0))
            else:
                edge = row < d
                a_sh = jnp.where(edge, 1.0, pltpu.roll(a, d, 0))
                b_sh = jnp.where(edge, 0.0, pltpu.roll(b, d, 0))
            b = a * b_sh + b
            a = a * a_sh
        h = a * carry + b
        h_ref[pl.ds(off, SUBLANES), :] = h
        return h[0:1, :] if reverse else h[SUBLANES - 1:SUBLANES, :]

    lax.fori_loop(0, nb, blk, jnp.zeros((1, W), F32))


def _lru_specs(T):
    B, W = LRU_BLOCKS, LRU_BLOCK
    xb = pl.BlockSpec((T, W), lambda j: (0, j))
    yb = pl.BlockSpec((T, W), lambda j: (0, B + j))
    cw = pl.BlockSpec((CONV_WIDTH, W), lambda j: (0, j))
    vec = pl.BlockSpec((1, W), lambda j: (0, j))
    wg = pl.BlockSpec((1, W, W), lambda j: (j, 0, 0))
    bg = pl.BlockSpec((1, 1, W), lambda j: (j, 0, 0))
    return xb, yb, cw, vec, wg, bg


def _lru_gates(xb_ref, cw_ref, cb_ref, wr_ref, br_ref, wi_ref, bi_ref, lam_ref):
    xv = xb_ref[...]
    xc = _conv(xv, cw_ref) + cb_ref[...]
    r = _sigmoid(_bdot(xc, wr_ref[0]) + br_ref[0])
    i = _sigmoid(_bdot(xc, wi_ref[0]) + bi_ref[0])
    sp = _softplus(-lam_ref[...])
    la = -LRU_C * sp * r
    a = jnp.exp(la)
    s = jnp.sqrt(-_expm1(2.0 * la))
    return xv, xc, r, i, sp, a, s


def _lru_fwd(proj, conv_w, conv_b, w_r, b_r, w_i, b_i, lam, *, name):
    T = proj.shape[0]
    B, W = LRU_BLOCKS, LRU_BLOCK

    def body(xb_ref, yb_ref, cw_ref, cb_ref, wr_ref, br_ref, wi_ref, bi_ref, lam_ref, og_ref, hs_ref, a_s, u_s):
        xv, xc, r, i, sp, a, s = _lru_gates(xb_ref, cw_ref, cb_ref, wr_ref, br_ref, wi_ref, bi_ref, lam_ref)
        a_s[...] = a
        u_s[...] = s * (i * xc)
        _linear_scan(a_s, u_s, hs_ref, False)
        og_ref[...] = (hs_ref[...] * _gelu(yb_ref[...])).astype(og_ref.dtype)

    xb, yb, cw, vec, wg, bg = _lru_specs(T)
    return pl.pallas_call(
        body, name=name, grid=(B,), in_specs=[xb, yb, cw, vec, wg, bg, wg, bg, vec], out_specs=(xb, xb),
        out_shape=(jax.ShapeDtypeStruct((T, B * W), BF16), jax.ShapeDtypeStruct((T, B * W), F32)),
        scratch_shapes=[pltpu.VMEM((T, W), F32), pltpu.VMEM((T, W), F32)],
        compiler_params=_params(("arbitrary",)),
    )(proj, proj, conv_w, conv_b, w_r, b_r, w_i, b_i, lam)


def _lru_bwd(proj, conv_w, conv_b, w_r, b_r, w_i, b_i, lam, hs, dout, *, name):
    T = proj.shape[0]
    B, W = LRU_BLOCKS, LRU_BLOCK

    def csum(x):
        return jnp.sum(x, axis=0, keepdims=True)

    def body(xb_ref, yb_ref, cw_ref, cb_ref, wr_ref, br_ref, wi_ref, bi_ref, lam_ref, hs_ref, do_ref,
             dxb_ref, dyb_ref, dcw_ref, dcb_ref, dwr_ref, dbr_ref, dwi_ref, dbi_ref, dlam_ref, a_s, b_s, g_s):
        xv, xc, r, i, sp, a, s = _lru_gates(xb_ref, cw_ref, cb_ref, wr_ref, br_ref, wi_ref, bi_ref, lam_ref)
        h = hs_ref[...]
        yb = yb_ref[...]
        dout = do_ref[...].astype(F32)
        dyb_ref[...] = (dout * h * _dgelu(yb)).astype(dyb_ref.dtype)
        a_s[...] = _shift_up(a, 1)
        b_s[...] = dout * _gelu(yb)
        _linear_scan(a_s, b_s, g_s, True)
        g = g_s[...]
        da = g * _shift_down(h, 1)
        ds = g * (i * xc)
        di = g * s * xc
        dxc = g * s * i
        dla = da * a - ds * (a * a) / s
        dr = dla * (-LRU_C * sp)
        dlam_ref[...] = csum(dla * r) * (LRU_C * _sigmoid(-lam_ref[...]))
        dpr = dr * r * (1.0 - r)
        dpi = di * i * (1.0 - i)
        dxc = dxc + _bdot(dpr, wr_ref[0], NT) + _bdot(dpi, wi_ref[0], NT)
        dwr_ref[0] = _bdot(xc, dpr, TN)
        dwi_ref[0] = _bdot(xc, dpi, TN)
        dbr_ref[0] = csum(dpr)
        dbi_ref[0] = csum(dpi)
        dcb_ref[...] = csum(dxc)
        dxb_ref[...] = _conv_bwd(xv, cw_ref, dxc, dcw_ref).astype(dxb_ref.dtype)

    xb, yb, cw, vec, wg, bg = _lru_specs(T)
    act = jax.ShapeDtypeStruct((T, B * W), BF16)
    return pl.pallas_call(
        body, name=name, grid=(B,), in_specs=[xb, yb, cw, vec, wg, bg, wg, bg, vec, xb, xb],
        out_specs=(xb, xb, cw, vec, wg, bg, wg, bg, vec),
        out_shape=(act, act, jax.ShapeDtypeStruct((CONV_WIDTH, B * W), F32), jax.ShapeDtypeStruct((1, B * W), F32),
                   jax.ShapeDtypeStruct((B, W, W), F32), jax.ShapeDtypeStruct((B, 1, W), F32),
                   jax.ShapeDtypeStruct((B, W, W), F32), jax.ShapeDtypeStruct((B, 1, W), F32),
                   jax.ShapeDtypeStruct((1, B * W), F32)),
        scratch_shapes=[pltpu.VMEM((T, W), F32)] * 3,
        compiler_params=_params(("arbitrary",)),
    )(proj, proj, conv_w, conv_b, w_r, b_r, w_i, b_i, lam, hs, dout)


MESH = pl.DeviceIdType.MESH
N_CHIPS = 4
AG_COPIES = 7


def _mesh_pos():
    return lax.axis_index("x"), lax.axis_index("y"), lax.axis_index("c")


def _hbm_specs(n):
    return [pl.BlockSpec(memory_space=pltpu.HBM)] * n


def _all_gather(shards, *, name):
    n = len(shards)

    def body(*refs):
        xs, outs = refs[:n], refs[n:2 * n]
        send_sems, recv_sems, local_sems = refs[2 * n:]
        x, y, c = _mesh_pos()
        me, sibling = (x, y, c), (x, y, 1 - c)
        chips = [(1 - x, y), (x, 1 - y), (1 - x, 1 - y)]

        def rows(t, px, py, pc):
            return outs[t].at[4 * px + 2 * py + pc]

        def copy(t, k, block, to, src=None):
            return pltpu.make_async_remote_copy(
                src_ref=rows(t, *block) if src is None else src, dst_ref=rows(t, *block),
                send_sem=send_sems.at[t * AG_COPIES + k], recv_sem=recv_sems.at[t * AG_COPIES + k],
                device_id=to, device_id_type=MESH)

        mine = [pltpu.make_async_copy(xs[t], rows(t, *me), local_sems.at[t]) for t in range(n)]
        for cp in mine:
            cp.start()
        first = []
        for t in range(n):
            first.append(copy(t, 0, me, sibling, src=xs[t]))
            first += [copy(t, 1 + j, me, (*chip, c), src=xs[t]) for j, chip in enumerate(chips)]
        for cp in first:
            cp.start()
        passed = []
        for j, chip in enumerate(chips):
            for t in range(n):
                copy(t, 1 + j, (*chip, c), me).wait_recv()
                cp = copy(t, 4 + j, (*chip, c), sibling)
                cp.start()
                passed.append(cp)
        for t in range(n):
            copy(t, 0, sibling, me).wait_recv()
        for j, chip in enumerate(chips):
            for t in range(n):
                copy(t, 4 + j, (*chip, 1 - c), me).wait_recv()
        for cp in first + passed:
            cp.wait_send()
        for cp in mine:
            cp.wait()

    return pl.pallas_call(
        body, name=name,
        out_shape=[jax.ShapeDtypeStruct((N_DEV,) + s.shape, s.dtype) for s in shards],
        in_specs=_hbm_specs(n), out_specs=_hbm_specs(n),
        scratch_shapes=[pltpu.SemaphoreType.DMA((n * AG_COPIES,)), pltpu.SemaphoreType.DMA((n * AG_COPIES,)),
                        pltpu.SemaphoreType.DMA((n,))],
    )(*shards)


SIDE_EFFECT = pltpu.SideEffectType.DATAFLOW_SIDE_EFFECTING


def _copies(plan, refs, send_sems, recv_sems):
    return [pltpu.make_async_remote_copy(src_ref=src, dst_ref=dst, send_sem=send_sems.at[k], recv_sem=recv_sems.at[k],
                                         device_id=to, device_id_type=MESH)
            for k, (src, dst, to) in enumerate(plan(refs))]


def _split_start(bufs, plan, n_copies, *, name, deps=()):
    n = len(bufs)

    def body(*refs):
        send_sems, recv_sems = refs[n + len(deps)], refs[n + len(deps) + 1]
        token = refs[-1]
        for cp in _copies(plan, refs[:n], send_sems, recv_sems):
            cp.start()
        token[...] = jnp.zeros_like(token)

    hbm, sem = pl.BlockSpec(memory_space=pltpu.HBM), pl.BlockSpec(memory_space=pltpu.SEMAPHORE)
    out = pl.pallas_call(
        body, name=name,
        out_shape=(pltpu.SemaphoreType.DMA((n_copies,)), pltpu.SemaphoreType.DMA((n_copies,)),
                   *[pltpu.HBM(b.shape, b.dtype) for b in bufs], jax.ShapeDtypeStruct(TOKEN_SHAPE, F32)),
        in_specs=[hbm] * n + [pl.BlockSpec(memory_space=pl.ANY)] * len(deps),
        out_specs=(sem, sem, *[hbm] * n, pl.BlockSpec(memory_space=pltpu.VMEM)),
        input_output_aliases={i: 2 + i for i in range(n)},
        compiler_params=pltpu.CompilerParams(has_side_effects=SIDE_EFFECT),
    )(*[pltpu.with_memory_space_constraint(b, pltpu.HBM) for b in bufs], *deps)
    return out[0], out[1], list(out[2:2 + n]), out[-1]


def _split_wait(send_sems, recv_sems, bufs, plan, after, *, name):
    n = len(bufs)

    def body(*refs):
        for cp in _copies(plan, refs[:n], refs[n], refs[n + 1]):
            cp.wait_send()
            cp.wait_recv()

    hbm, sem = pl.BlockSpec(memory_space=pltpu.HBM), pl.BlockSpec(memory_space=pltpu.SEMAPHORE)
    out = pl.pallas_call(
        body, name=name, out_shape=tuple(pltpu.HBM(b.shape, b.dtype) for b in bufs),
        in_specs=[hbm] * n + [sem, sem, pl.BlockSpec(memory_space=pl.ANY)], out_specs=tuple([hbm] * n),
        input_output_aliases={i: i for i in range(n)},
        compiler_params=pltpu.CompilerParams(has_side_effects=SIDE_EFFECT),
    )(*bufs, send_sems, recv_sems, after)
    return list(out)


def _block(ref, in_cols, d):
    if not in_cols:
        return ref.at[d]
    c = ref.shape[1] // N_DEV
    return ref.at[:, pl.ds(pl.multiple_of(d * c, LANE), c)]


def _plan_gather_a(in_cols):
    n = len(in_cols)

    def plan(refs):
        x, y, c = _mesh_pos()
        me = 4 * x + 2 * y + c
        peers = [(x, y, 1 - c), (1 - x, y, c), (x, 1 - y, c), (1 - x, 1 - y, c)]
        return [(refs[t], _block(refs[n + t], in_cols[t], me), to) for t in range(n) for to in peers]
    return plan


def _plan_gather_b(in_cols):
    n = len(in_cols)

    def plan(refs):
        x, y, c = _mesh_pos()
        ds = [4 * px + 2 * py + c for px, py in [(1 - x, y), (x, 1 - y), (1 - x, 1 - y)]]
        return [(_block(refs[t], in_cols[t], d), _block(refs[t], in_cols[t], d), (x, y, 1 - c))
                for t in range(n) for d in ds]
    return plan


def _plan_scatter_pair(in_cols):
    n = len(in_cols)

    def plan(refs):
        x, y, c = _mesh_pos()
        return [(_block(refs[t], in_cols[t], 2 * q + (1 - c)), refs[n + t].at[q], (x, y, 1 - c))
                for t in range(n) for q in range(N_CHIPS)]
    return plan


def _plan_scatter_chips(n):
    def plan(refs):
        x, y, c = _mesh_pos()
        chips = [(1 - x, y), (x, 1 - y), (1 - x, 1 - y)]
        return [(refs[t].at[2 * px + py], refs[n + t].at[j], (px, py, c))
                for t in range(n) for j, (px, py) in enumerate(chips)]
    return plan


def _insert_block(land, shard, device, in_cols, *, name):
    r, c = shard.shape
    tr = _tile(r, (512, 256))

    def body(d_ref, s_ref, land_ref, o_ref):
        if in_cols:
            o_ref[...] = s_ref[...]
        else:
            o_ref[0] = s_ref[...]

    if in_cols:
        o_spec = pl.BlockSpec((tr, c), lambda i, d: (i, d[0]))
    else:
        o_spec = pl.BlockSpec((1, tr, c), lambda i, d: (d[0], i, 0))
    return pl.pallas_call(
        body, name=name,
        grid_spec=pltpu.PrefetchScalarGridSpec(
            num_scalar_prefetch=1, grid=(r // tr,),
            in_specs=[pl.BlockSpec((tr, c), lambda i, d: (i, 0)), pl.BlockSpec(memory_space=pl.ANY)],
            out_specs=o_spec),
        out_shape=jax.ShapeDtypeStruct(land.shape, land.dtype), input_output_aliases={2: 0},
        compiler_params=_params(("parallel",)),
    )(device, shard, land)


PAIR_ROWS = (512, 256)


def _pair_add(g, a, core, in_cols, *, name):
    _, R, C = a.shape
    tr = _tile(R, PAIR_ROWS)

    def body(c_ref, g_ref, a_ref, o_ref):
        gv = g_ref[...] if in_cols else g_ref[0]
        o_ref[0] = (gv.astype(F32) + a_ref[0].astype(F32)).astype(o_ref.dtype)

    blk = pl.BlockSpec((1, tr, C), lambda q, i, c: (q, i, 0))
    if in_cols:
        g_spec = pl.BlockSpec((tr, C), lambda q, i, c: (i, 2 * q + c[0]))
    else:
        g_spec = pl.BlockSpec((1, tr, C), lambda q, i, c: (2 * q + c[0], i, 0))
    return pl.pallas_call(
        body, name=name,
        grid_spec=pltpu.PrefetchScalarGridSpec(
            num_scalar_prefetch=1, grid=(N_CHIPS, R // tr), in_specs=[g_spec, blk], out_specs=blk),
        out_shape=jax.ShapeDtypeStruct((N_CHIPS, R, C), BF16),
        compiler_params=_params(("parallel", "parallel")),
    )(core, g, a)


ADAM_ROWS = 256


def _adamw_sharded(w, m, v, s4, b3, chip, layer, prev, *, name):
    L, R, C = w.shape
    tr = _tile(R, (ADAM_ROWS,))
    if prev is None and L > 1:
        prev = tuple(lax.empty(w.shape, F32) for _ in range(4))
    n_prev = 0 if prev is None else 4

    def body(q_ref, w_ref, m_ref, v_ref, s_ref, b_ref, *rest):
        g_out, d_out, m_out, v_out = rest[n_prev:]
        g = s_ref[0].astype(F32)
        for j in range(N_CHIPS - 1):
            g = g + b_ref[j].astype(F32)
        mn = ADAM_B1 * m_ref[0] + (1.0 - ADAM_B1) * g
        vn = ADAM_B2 * v_ref[0] + (1.0 - ADAM_B2) * (g * g)
        g_out[0] = g
        d_out[0] = -ADAM_LR * ((mn / ADAM_C1) / (jnp.sqrt(vn / ADAM_C2) + ADAM_EPS) + ADAM_WD * w_ref[0])
        m_out[0] = mn
        v_out[0] = vn

    blk = pl.BlockSpec((1, tr, C), lambda i, q: (layer, i, 0))
    sh = jax.ShapeDtypeStruct((L, R, C), F32)
    return pl.pallas_call(
        body, name=name,
        grid_spec=pltpu.PrefetchScalarGridSpec(
            num_scalar_prefetch=1, grid=(R // tr,),
            in_specs=[blk, blk, blk, pl.BlockSpec((1, tr, C), lambda i, q: (q[0], i, 0)),
                      pl.BlockSpec((N_CHIPS - 1, tr, C), lambda i, q: (0, i, 0))]
            + [pl.BlockSpec(memory_space=pl.ANY)] * n_prev,
            out_specs=(blk,) * 4),
        out_shape=(sh,) * 4, input_output_aliases={6 + k: k for k in range(n_prev)},
        compiler_params=_params(("parallel",)),
    )(chip, w, m, v, s4, b3, *(prev or ()))


FWD_NAMES = ['x', 'norm1', 'norm2', 'final_norm', 'ret_w_in', 'ret_gn_gain', 'ret_w_out', 'gdn_w_in', 'gdn_conv_w',
             'gdn_a_log', 'gdn_dt_bias', 'gdn_norm_gain', 'gdn_w_out', 'gla_w_in', 'gla_w_gate_up', 'gla_gate_bias',
             'gla_norm_gain', 'gla_w_out', 'lru_w_in', 'lru_conv_w', 'lru_conv_b', 'lru_w_rgate', 'lru_b_rgate',
             'lru_w_igate', 'lru_b_igate', 'lru_lambda', 'lru_w_out', 'mlp_w_up', 'mlp_w_down']
WEIGHT_NAMES = FWD_NAMES[1:]
ARG_NAMES = FWD_NAMES + ['loss_target'] + ['m_' + n for n in WEIGHT_NAMES] + ['v_' + n for n in WEIGHT_NAMES]

MIXER_IN = ('ret_w_in', 'gdn_w_in', 'gla_w_in', 'lru_w_in')
MIXER_OUT = ('ret_w_out', 'gdn_w_out', 'gla_w_out', 'lru_w_out')
BIG_NAMES = MIXER_IN + MIXER_OUT + ('mlp_w_up', 'mlp_w_down')
SMALL = {'norm1': False, 'norm2': False, 'final_norm': False, 'ret_gn_gain': True, 'gdn_conv_w': True,
         'gdn_a_log': False, 'gdn_dt_bias': False, 'gdn_norm_gain': False, 'gla_w_gate_up': True,
         'gla_gate_bias': True, 'gla_norm_gain': True, 'lru_conv_w': True, 'lru_conv_b': True,
         'lru_w_rgate': False, 'lru_b_rgate': False, 'lru_w_igate': False, 'lru_b_igate': False, 'lru_lambda': True}
SMALL_NAMES = tuple(n for n in WEIGHT_NAMES if n in SMALL)
MEDIUM = ('lru_w_rgate', 'lru_w_igate')
EARLY_SMALL = tuple(n for n in SMALL_NAMES if n.startswith(('gdn_', 'gla_', 'lru_')) and n not in MEDIUM)
LATE_SMALL = tuple(n for n in SMALL_NAMES if n not in EARLY_SMALL + MEDIUM)
GDN_TAIL = 2 * GDN_HEADS
GDN_MAIN = 4 * D_MODEL
GLA_MAIN = 3 * D_MODEL


PACK_ROWS = 256


PACK_TILE = SUBLANES * LANE


def _pack(arrs):
    rows = []
    for a in arrs:
        f = a.reshape(-1).astype(F32)
        rows.append(jnp.pad(f, (0, (-f.shape[0]) % PACK_TILE)).reshape(-1, LANE))
    fill = (-sum(r.shape[0] for r in rows)) % PACK_ROWS
    if fill:
        rows.append(jnp.zeros((fill, LANE), F32))
    return jnp.concatenate(rows, axis=0)


def _unpack(buf, shapes, lead=()):
    out, r0 = [], 0
    for s in shapes:
        n = int(np.prod(s))
        nr = -(-n // PACK_TILE) * SUBLANES
        blk = buf[..., r0:r0 + nr, :].reshape(lead + (nr * LANE,))[..., :n]
        out.append(blk.reshape(lead + tuple(s)))
        r0 += nr
    return out


def _full_cols(g):
    return jnp.transpose(g, (1, 0, 2)).reshape(g.shape[1], N_DEV * g.shape[2])


def _full_rows(g):
    return g.reshape(N_DEV * g.shape[1], g.shape[2])


def _blocks_cols(dw):
    r, c = dw.shape[0], dw.shape[1] // N_DEV
    return jnp.transpose(dw.reshape(r, N_DEV, c), (1, 0, 2))


def _blocks_rows(dw):
    return dw.reshape(N_DEV, dw.shape[0] // N_DEV, dw.shape[1])


def _pad_cols(a, n=LANE):
    return jnp.pad(a, ((0, 0), (0, n - a.shape[1])))


def _mixer_fwd(layer, hn, w_in, sm, tables, deps=()):
    tag = f"l{layer}"
    if layer == 0:
        proj = _matmul(hn, w_in, name=tag + "_in", deps=deps)
        gain = sm['ret_gn_gain'][0][:, None, :]
        og, st = _ret_fwd(proj, gain, tables, name=tag + "_ret_fwd")
        return og, dict(proj=proj, st=st, gain=gain)
    if layer == 1:
        w_main, w_tail = w_in, _pad_cols(w_in[:, GDN_MAIN:])
        proj = _matmul(hn, w_main, b_cols=GDN_MAIN, name=tag + "_in", deps=deps)
        gates = _matmul(hn, w_tail, name=tag + "_in_tail")
        conv_w = sm['gdn_conv_w'][0]
        act = _conv_silu_fwd(proj, conv_w, name=tag + "_conv")
        a_log = sm['gdn_a_log'].reshape(GDN_HEADS, 1, 1)
        dt_bias = sm['gdn_dt_bias'].reshape(GDN_HEADS, 1, 1)
        gain = sm['gdn_norm_gain']
        og, st, tm, sol = _gdn_fwd(act, proj, gates, a_log, dt_bias, gain, name=tag + "_gdn_fwd")
        return og, dict(proj=proj, gates=gates, act=act, st=st, solved=(tm, sol), conv_w=conv_w, a_log=a_log, dt_bias=dt_bias,
                        gain=gain, w_main=w_main, w_tail=w_tail)
    if layer == 2:
        w_main, w_tail = w_in, _pad_cols(w_in[:, GLA_MAIN:])
        proj = _matmul(hn, w_main, b_cols=GLA_MAIN, name=tag + "_in", deps=deps)
        glow = _matmul(hn, w_tail, name=tag + "_in_tail")
        wgu = jnp.pad(sm['gla_w_gate_up'][0], ((0, LANE - GLA_GATE_RANK), (0, 0)))
        glogit = _matmul(glow, wgu, name=tag + "_gate_up")
        bias = sm['gla_gate_bias']
        gain = sm['gla_norm_gain'][0][:, None, :]
        og, st = _gla_fwd(proj, glogit, bias, gain, name=tag + "_gla_fwd")
        return og, dict(proj=proj, glow=glow, glogit=glogit, wgu=wgu, bias=bias, gain=gain, st=st,
                        w_main=w_main, w_tail=w_tail)
    proj = _matmul(hn, w_in, name=tag + "_in", deps=deps)
    args = (proj, sm['lru_conv_w'][0], sm['lru_conv_b'], sm['lru_w_rgate'][0], sm['lru_b_rgate'][0][:, None, :],
            sm['lru_w_igate'][0], sm['lru_b_igate'][0][:, None, :], sm['lru_lambda'])
    og, hs = _lru_fwd(*args, name=tag + "_lru_fwd")
    return og, dict(args=args, hs=hs)


def _mixer_bwd(layer, hn, w_in, dog, sv, tables, on_dw):
    tag = f"l{layer}"
    if layer == 0:
        dq, dk, dv, dg, dgain = _ret_bwd(sv['proj'], sv['gain'], tables, sv['st'], dog, name=tag + "_ret_bwd")
        dproj = jnp.concatenate([dq, dk, dv, dg], axis=1)
        deps = on_dw(_matmul(hn, dproj, ta=True, out_dtype=BF16, name=tag + "_in_dw"))
        dhn = _matmul(dproj, w_in, tb=True, name=tag + "_in_dx", deps=deps)
        return dhn, {'ret_gn_gain': dgain[:, 0][None]}
    if layer == 1:
        dq, dk, dv, dz, dgates, dscal, dgain = _gdn_bwd(
            sv['act'], sv['proj'], sv['gates'], sv['a_log'], sv['dt_bias'], sv['gain'], sv['st'], sv['solved'], dog,
            name=tag + "_gdn_bwd")
        dact = jnp.concatenate([dq, dk, dv], axis=1)
        dqkv, dconv = _conv_silu_bwd(sv['proj'], sv['conv_w'], dact, name=tag + "_conv_bwd")
        dmain = jnp.concatenate([dqkv, dz], axis=1)
        T = dmain.shape[0]
        dtail = _pad_cols(jnp.transpose(dgates[:, :, :2], (1, 2, 0)).reshape(T, GDN_TAIL))
        dw_main = _matmul(hn, dmain, ta=True, out_dtype=BF16, name=tag + "_in_dw")
        dw_tail = _matmul(hn, dtail, ta=True, out_dtype=BF16, name=tag + "_in_tail_dw")
        deps = on_dw(jnp.concatenate([dw_main, dw_tail[:, :GDN_TAIL]], axis=1))
        dhn = _matmul(dmain, sv['w_main'], tb=True, b_cols=GDN_MAIN, name=tag + "_in_dx", deps=deps)
        dhn = _matmul(dtail, sv['w_tail'], tb=True, epi="add", extra=dhn, name=tag + "_in_tail_dx")
        small = {'gdn_conv_w': dconv[None], 'gdn_a_log': dscal[:, 0, 0][None], 'gdn_dt_bias': dscal[:, 0, 1][None],
                 'gdn_norm_gain': jnp.sum(dgain[:, 0], axis=0)[None]}
        return dhn, small
    if layer == 2:
        dq, dk, dv, dr, dgl, dgain = _gla_bwd(sv['proj'], sv['glogit'], sv['bias'], sv['gain'], sv['st'], dog,
                                              name=tag + "_gla_bwd")
        dmain = jnp.concatenate([dq, dk, dv, dr], axis=1)
        dglow = _matmul(dgl, sv['wgu'], tb=True, name=tag + "_gate_up_dx")
        dwgu = _matmul(sv['glow'], dgl, ta=True, name=tag + "_gate_up_dw")
        dbias = _colsum(dgl, name=tag + "_gate_bias")
        dw_main = _matmul(hn, dmain, ta=True, out_dtype=BF16, name=tag + "_in_dw")
        dw_tail = _matmul(hn, dglow, ta=True, out_dtype=BF16, name=tag + "_in_tail_dw")
        deps = on_dw(jnp.concatenate([dw_main, dw_tail[:, :GLA_GATE_RANK]], axis=1))
        dhn = _matmul(dmain, sv['w_main'], tb=True, b_cols=GLA_MAIN, name=tag + "_in_dx", deps=deps)
        dhn = _matmul(dglow, sv['w_tail'], tb=True, epi="add", extra=dhn, name=tag + "_in_tail_dx")
        small = {'gla_w_gate_up': dwgu[:GLA_GATE_RANK][None], 'gla_gate_bias': dbias,
                 'gla_norm_gain': dgain[:, 0][None]}
        return dhn, small
    dxb, dyb, dcw, dcb, dwr, dbr, dwi, dbi, dlam = _lru_bwd(*sv['args'], sv['hs'], dog, name=tag + "_lru_bwd")
    dproj = jnp.concatenate([dxb, dyb], axis=1)
    deps = on_dw(_matmul(hn, dproj, ta=True, out_dtype=BF16, name=tag + "_in_dw"))
    dhn = _matmul(dproj, w_in, tb=True, name=tag + "_in_dx", deps=deps)
    small = {'lru_conv_w': dcw[None], 'lru_conv_b': dcb, 'lru_w_rgate': dwr[None], 'lru_b_rgate': dbr[:, 0][None],
             'lru_w_igate': dwi[None], 'lru_b_igate': dbi[:, 0][None], 'lru_lambda': dlam}
    return dhn, small


def _step(*args):
    assert len(args) == len(ARG_NAMES)
    p = dict(zip(ARG_NAMES, args))
    xi, yi, ci = _mesh_pos()
    dev = 4 * xi + 2 * yi + ci
    device = dev.astype(jnp.int32).reshape(1)
    core = ci.astype(jnp.int32).reshape(1)
    chip = (2 * xi + yi).astype(jnp.int32).reshape(1)
    x = p['x'][0]
    target = p['loss_target'][0]
    T = x.shape[0]
    tables = _ret_tables(T)

    sharded_small = [n for n in SMALL_NAMES if SMALL[n]]
    gathered, = _all_gather([_pack([p[n] for n in sharded_small])], name="gather_small")
    def in_cols_of(w):
        return w.shape[1] % LANE == 0

    gathers = {}
    token = gathered
    for layer in range(DEPTH):
        w_in_shard = p[MIXER_IN[layer]][0]
        groups = {'a': [(w_in_shard, in_cols_of(w_in_shard), _full_cols)],
                  'b': [(p[MIXER_OUT[layer]][0], False, _full_rows)],
                  'c': [(p['mlp_w_up'][layer], True, None), (p['mlp_w_down'][layer], False, _full_rows)]}
        for key, members in groups.items():
            shards = [w.astype(BF16) for w, _, _ in members]
            in_cols = [ic for _, ic, _ in members]
            n = len(shards)
            lands = [_insert_block(lax.empty((s.shape[0], N_DEV * s.shape[1]) if ic else (N_DEV,) + s.shape, BF16),
                                   s, device, ic, name=f"own_l{layer}{key}{i}")
                     for i, (s, ic) in enumerate(zip(shards, in_cols))]
            send, recv, bufs, token = _split_start(shards + lands, _plan_gather_a(in_cols), 4 * n,
                                                   name=f"gather_a_start_l{layer}{key}", deps=(token,))
            gathers[layer, key] = dict(n=n, h=(send, recv, bufs), in_cols=in_cols, full_of=[f for _, _, f in members])

    def gather_forward(layer, key, after):
        g = gathers[layer, key]
        n = g['n']
        send, recv, bufs = g['h']
        bufs = _split_wait(send, recv, bufs, _plan_gather_a(g['in_cols']), after, name=f"gather_a_wait_l{layer}{key}")
        send, recv, lands, tok = _split_start(bufs[n:], _plan_gather_b(g['in_cols']), 3 * n,
                                              name=f"gather_b_start_l{layer}{key}")
        g['h'] = (send, recv, lands)
        return tok

    def gather_finish(layer, key, after):
        g = gathers[layer, key]
        send, recv, lands = g['h']
        lands = _split_wait(send, recv, lands, _plan_gather_b(g['in_cols']), after, name=f"gather_b_wait_l{layer}{key}")
        return [l if ic else full_of(l) for l, ic, full_of in zip(lands, g['in_cols'], g['full_of'])]

    w_in_next, = gather_finish(0, 'a', gather_forward(0, 'a', token))
    parts = _unpack(gathered, [p[n].shape for n in sharded_small], lead=(N_DEV,))
    sm = {n: p[n] for n in SMALL_NAMES if not SMALL[n]}
    for n, blk in zip(sharded_small, parts):
        full = jnp.moveaxis(blk, 0, -2)
        sm[n] = full.reshape(full.shape[:-2] + (N_DEV * full.shape[-1],))

    saved = []
    big = {}
    for layer in range(DEPTH):
        w_in = w_in_next
        tag = f"l{layer}"
        hn = _rmsnorm_fwd(x, sm['norm1'][layer][None], name=tag + "_norm1")
        og, sv = _mixer_fwd(layer, hn, w_in, sm, tables)
        tok_b = gather_forward(layer, 'b', og)
        tok_c = gather_forward(layer, 'c', tok_b)
        w_out, = gather_finish(layer, 'b', tok_c)
        x_mid = _matmul(og, w_out, epi="add", extra=x, name=tag + "_out")
        hn2 = _rmsnorm_fwd(x_mid, sm['norm2'][layer][None], name=tag + "_norm2")
        w_up, w_down = gather_finish(layer, 'c', hn2)
        big[layer] = (w_in, w_out, w_up, w_down)
        u, a = _matmul(hn2, w_up, epi="relu2", name=tag + "_up")
        deps = (gather_forward(layer + 1, 'a', u),) if layer + 1 < DEPTH else ()
        x_new = _matmul(a, w_down, epi="add", extra=x_mid, name=tag + "_down", deps=deps)
        if layer + 1 < DEPTH:
            w_in_next, = gather_finish(layer + 1, 'a', x_new)
        saved.append(dict(x=x, hn=hn, og=og, sv=sv, x_mid=x_mid, hn2=hn2, u=u, a=a))
        x = x_new
    dx, dxb, dfinal, loss_part = _final_loss_bwd(x, sm['final_norm'][None], target, name="final_loss_bwd")
    loss = lax.psum(loss_part[0, 0], ("x", "y", "c"))

    outs = {}
    small_grads = {'final_norm': dfinal[0]}
    dnorm1, dnorm2 = [None] * DEPTH, [None] * DEPTH

    def scatter_start(items, deps, tag):
        grads, in_cols = [], []
        for _, _, dw, on in items:
            ic = on == 'cols' and (dw.shape[1] // N_DEV) % LANE == 0
            in_cols.append(ic)
            grads.append(dw if ic else (_blocks_cols(dw) if on == 'cols' else _blocks_rows(dw)))
        n = len(grads)
        lands = [lax.empty((N_CHIPS, dw.shape[0], dw.shape[1] // N_DEV) if ic else (N_CHIPS,) + dw.shape[1:], BF16)
                 for dw, ic in zip(grads, in_cols)]
        send, recv, bufs, tok = _split_start(grads + lands, _plan_scatter_pair(in_cols), N_CHIPS * n,
                                             name=f"scatter_pair_start_{tag}", deps=deps)
        return dict(items=items, n=n, tag=tag, in_cols=in_cols, h=(send, recv, bufs)), tok

    def scatter_forward(g, after):
        n, tag = g['n'], g['tag']
        send, recv, bufs = g['h']
        bufs = _split_wait(send, recv, bufs, _plan_scatter_pair(g['in_cols']), after, name=f"scatter_pair_wait_{tag}")
        sums = [_pair_add(b, a_, core, ic, name=f"pair_add_{tag}_{i}")
                for i, (b, a_, ic) in enumerate(zip(bufs[:n], bufs[n:], g['in_cols']))]
        lands = [lax.empty((N_CHIPS - 1,) + s_.shape[1:], BF16) for s_ in sums]
        send, recv, bufs, tok = _split_start(sums + lands, _plan_scatter_chips(n), (N_CHIPS - 1) * n,
                                             name=f"scatter_chips_start_{tag}")
        g['h'] = (send, recv, bufs)
        return tok

    def scatter_finish(g, after):
        n, tag = g['n'], g['tag']
        send, recv, bufs = g['h']
        bufs = _split_wait(send, recv, bufs, _plan_scatter_chips(n), after, name=f"scatter_chips_wait_{tag}")
        for i, (wname, idx, _, _) in enumerate(g['items']):
            outs[wname] = _adamw_sharded(p[wname], p['m_' + wname], p['v_' + wname], bufs[i], bufs[n + i], chip,
                                         idx, outs.get(wname), name=f"adamw_{tag}_{i}")

    older = []
    next_deps = ()
    for layer in reversed(range(DEPTH)):
        w_in, w_out, w_up, w_down = big[layer]
        s = saved[layer]
        tag = f"l{layer}"
        du = _matmul(dxb, w_down, tb=True, epi="drelu2", extra=s['u'], out_dtype=BF16, name=tag + "_down_dx",
                     deps=next_deps)
        next_deps = ()
        dw_down = _matmul(s['a'], dxb, ta=True, out_dtype=BF16, name=tag + "_down_dw")
        dhn2 = _matmul(du, w_up, tb=True, name=tag + "_up_dx")
        dw_up = _matmul(s['hn2'], du, ta=True, out_dtype=BF16, name=tag + "_up_dw")
        mlp_group, tok = scatter_start([('mlp_w_up', layer, dw_up, 'cols'),
                                        ('mlp_w_down', layer, dw_down, 'rows')], (), f"mlp_l{layer}")
        dx, dxb, dn2 = _rmsnorm_bwd(s['x_mid'], sm['norm2'][layer][None], dhn2, dx, name=tag + "_norm2_bwd",
                                    deps=(tok,))
        dw_out = _matmul(s['og'], dxb, ta=True, out_dtype=BF16, name=tag + "_out_dw")
        tok = scatter_forward(mlp_group, dw_out)
        dog = _matmul(dxb, w_out, tb=True, name=tag + "_out_dx", deps=(tok,))
        started = []

        def on_dw(dw_in):
            group, tok_ = scatter_start([(MIXER_IN[layer], 0, dw_in, 'cols'),
                                         (MIXER_OUT[layer], 0, dw_out, 'rows')], (), f"mix_l{layer}")
            started.append(group)
            return (tok_,)

        dhn, sg = _mixer_bwd(layer, s['hn'], w_in, dog, s['sv'], tables, on_dw)
        mixer_group, = started
        small_grads.update(sg)
        if layer > 0:
            tok = scatter_forward(mixer_group, dhn)
            dx, dxb, dn1 = _rmsnorm_bwd(s['x'], sm['norm1'][layer][None], dhn, dx, name=tag + "_norm1_bwd", deps=(tok,))
        else:
            tok = gather_forward('small_grads', 'early', dhn)
            dx, dxb, dn1 = _rmsnorm_bwd(s['x'], sm['norm1'][layer][None], dhn, dx, name=tag + "_norm1_bwd", deps=(tok,))
        dnorm1[layer], dnorm2[layer] = dn1[0], dn2[0]
        for g in older:
            scatter_finish(g, dx)
        older = [mlp_group, mixer_group]
        if layer == 1:
            parts = [_pack([small_grads[n] for n in EARLY_SMALL])] + [small_grads[n].reshape(-1, LANE) for n in MEDIUM]
            lands = [_insert_block(lax.empty((N_DEV,) + s_.shape, F32), s_, device, False, name=f"own_small_grads{i}")
                     for i, s_ in enumerate(parts)]
            n_parts = len(parts)
            send, recv, bufs, tok = _split_start(parts + lands, _plan_gather_a([False] * n_parts), 4 * n_parts,
                                                 name="gather_a_start_small_grads", deps=(dx,))
            gathers['small_grads', 'early'] = dict(n=n_parts, h=(send, recv, bufs), in_cols=[False] * n_parts,
                                                   full_of=[lambda l: l] * n_parts)
            next_deps = (tok,)
    small_grads['norm1'] = jnp.stack(dnorm1)
    small_grads['norm2'] = jnp.stack(dnorm2)

    late_parts, = _all_gather([_pack([small_grads[n] for n in LATE_SMALL])], name="gather_small_grads")
    last_token = scatter_forward(mixer_group, late_parts)
    early_parts, *medium_parts = gather_finish('small_grads', 'early', last_token)
    early_sum = _sum_parts(early_parts, name="sum_small_grads_early", deps=(last_token,))
    late_sum = _sum_parts(late_parts, name="sum_small_grads_late")
    for g in older:
        scatter_finish(g, late_sum)
    for n, part in zip(MEDIUM, medium_parts):
        g = _sum_parts(part, name=f"sum_{n}")
        res = _adamw(p[n].reshape(-1, LANE), p['m_' + n].reshape(-1, LANE), p['v_' + n].reshape(-1, LANE), [g],
                     name=f"adamw_{n}")
        outs[n] = tuple(r.reshape(p[n].shape) for r in res)
    by_name = dict(zip(EARLY_SMALL, _unpack(early_sum, [small_grads[n].shape for n in EARLY_SMALL])))
    by_name.update(zip(LATE_SMALL, _unpack(late_sum, [small_grads[n].shape for n in LATE_SMALL])))
    packed_names = EARLY_SMALL + LATE_SMALL
    local_g = []
    for n in packed_names:
        g = by_name[n]
        if SMALL[n]:
            width = p[n].shape[-1]
            g = lax.dynamic_slice_in_dim(g, dev * width, width, axis=g.ndim - 1)
        local_g.append(g.reshape(p[n].shape))
    res = _adamw(_pack([p[n] for n in packed_names]), _pack([p['m_' + n] for n in packed_names]),
                 _pack([p['v_' + n] for n in packed_names]), [_pack(local_g)], name="adamw_small")
    local_shapes = [p[n].shape for n in packed_names]
    unpacked = [_unpack(r, local_shapes) for r in res]
    for i, n in enumerate(packed_names):
        outs[n] = tuple(unpacked[k][i] for k in range(4))

    result = [loss, dx[None]]
    for k in range(4):
        result += [outs[n][k] for n in WEIGHT_NAMES]
    return tuple(result)


def kernel(x, norm1, norm2, final_norm, ret_w_in, ret_gn_gain, ret_w_out, gdn_w_in, gdn_conv_w, gdn_a_log, gdn_dt_bias, gdn_norm_gain, gdn_w_out, gla_w_in, gla_w_gate_up, gla_gate_bias, gla_norm_gain, gla_w_out, lru_w_in, lru_conv_w, lru_conv_b, lru_w_rgate, lru_b_rgate, lru_w_igate, lru_b_igate, lru_lambda, lru_w_out, mlp_w_up, mlp_w_down, loss_target, m_norm1, m_norm2, m_final_norm, m_ret_w_in, m_ret_gn_gain, m_ret_w_out, m_gdn_w_in, m_gdn_conv_w, m_gdn_a_log, m_gdn_dt_bias, m_gdn_norm_gain, m_gdn_w_out, m_gla_w_in, m_gla_w_gate_up, m_gla_gate_bias, m_gla_norm_gain, m_gla_w_out, m_lru_w_in, m_lru_conv_w, m_lru_conv_b, m_lru_w_rgate, m_lru_b_rgate, m_lru_w_igate, m_lru_b_igate, m_lru_lambda, m_lru_w_out, m_mlp_w_up, m_mlp_w_down, v_norm1, v_norm2, v_final_norm, v_ret_w_in, v_ret_gn_gain, v_ret_w_out, v_gdn_w_in, v_gdn_conv_w, v_gdn_a_log, v_gdn_dt_bias, v_gdn_norm_gain, v_gdn_w_out, v_gla_w_in, v_gla_w_gate_up, v_gla_gate_bias, v_gla_norm_gain, v_gla_w_out, v_lru_w_in, v_lru_conv_w, v_lru_conv_b, v_lru_w_rgate, v_lru_b_rgate, v_lru_w_igate, v_lru_b_igate, v_lru_lambda, v_lru_w_out, v_mlp_w_up, v_mlp_w_down):
    return _step(x, norm1, norm2, final_norm, ret_w_in, ret_gn_gain, ret_w_out, gdn_w_in, gdn_conv_w, gdn_a_log, gdn_dt_bias, gdn_norm_gain, gdn_w_out, gla_w_in, gla_w_gate_up, gla_gate_bias, gla_norm_gain, gla_w_out, lru_w_in, lru_conv_w, lru_conv_b, lru_w_rgate, lru_b_rgate, lru_w_igate, lru_b_igate, lru_lambda, lru_w_out, mlp_w_up, mlp_w_down, loss_target, m_norm1, m_norm2, m_final_norm, m_ret_w_in, m_ret_gn_gain, m_ret_w_out, m_gdn_w_in, m_gdn_conv_w, m_gdn_a_log, m_gdn_dt_bias, m_gdn_norm_gain, m_gdn_w_out, m_gla_w_in, m_gla_w_gate_up, m_gla_gate_bias, m_gla_norm_gain, m_gla_w_out, m_lru_w_in, m_lru_conv_w, m_lru_conv_b, m_lru_w_rgate, m_lru_b_rgate, m_lru_w_igate, m_lru_b_igate, m_lru_lambda, m_lru_w_out, m_mlp_w_up, m_mlp_w_down, v_norm1, v_norm2, v_final_norm, v_ret_w_in, v_ret_gn_gain, v_ret_w_out, v_gdn_w_in, v_gdn_conv_w, v_gdn_a_log, v_gdn_dt_bias, v_gdn_norm_gain, v_gdn_w_out, v_gla_w_in, v_gla_w_gate_up, v_gla_gate_bias, v_gla_norm_gain, v_gla_w_out, v_lru_w_in, v_lru_conv_w, v_lru_conv_b, v_lru_w_rgate, v_lru_b_rgate, v_lru_w_igate, v_lru_b_igate, v_lru_lambda, v_lru_w_out, v_mlp_w_up, v_mlp_w_down)
```

```python
import functools
import math

import numpy as np
import jax
import jax.numpy as jnp
from jax import lax
from jax.experimental import pallas as pl
from jax.experimental.pallas import tpu as pltpu

F32 = jnp.float32
BF16 = jnp.bfloat16

D_MODEL = 2048
DEPTH = 4
CHUNK = 64
D_FF = 4 * D_MODEL
NORM_EPS = 1e-6
N_DEV = 8

RET_HEADS, RET_DK, RET_DV = 8, 256, 512
RET_HB = 4
GDN_HEADS, GDN_DK, GDN_DV = 16, 128, 128
GDN_HB = 8
GDN_QKV = GDN_HEADS * (2 * GDN_DK + GDN_DV)
CONV_WIDTH = 4
GLA_HEADS, GLA_DK, GLA_DV = 4, 256, 512
GLA_HB = 4
GLA_GATE_RANK = 16
GLA_TAU = 16.0
LRU_WIDTH, LRU_BLOCKS, LRU_BLOCK = 2048, 16, 128
LRU_C = 8.0
ROPE_BASE = 10000.0

ADAM_LR, ADAM_B1, ADAM_B2, ADAM_EPS, ADAM_WD, ADAM_STEP = 0.001, 0.9, 0.999, 1e-08, 0.01, 10

LANE = 128
VMEM_LIMIT = 48 * 1024 * 1024

NN = (((1,), (0,)), ((), ()))
NT = (((1,), (1,)), ((), ()))
TN = (((0,), (0,)), ((), ()))


def _params(sem=None):
    return pltpu.CompilerParams(dimension_semantics=sem, vmem_limit_bytes=VMEM_LIMIT)


def _bdot(a, b, dn=NN):
    return lax.dot_general(a.astype(BF16), b.astype(BF16), dn, preferred_element_type=F32)


def _split(x):
    hi = x.astype(BF16)
    lo = (x - hi.astype(F32)).astype(BF16)
    return hi, lo


def _fdot(a, b, dn=NN):
    a1, a2 = _split(a)
    b1, b2 = _split(b)
    d = functools.partial(lax.dot_general, dimension_numbers=dn, preferred_element_type=F32)
    return d(a1, b1) + (d(a1, b2) + d(a2, b1))


def _sigmoid(x):
    return 1.0 / (1.0 + jnp.exp(-x))


def _softplus(x):
    return jnp.maximum(x, 0.0) + jnp.log(1.0 + jnp.exp(-jnp.abs(x)))


def _silu(x):
    return x * _sigmoid(x)


def _dsilu(x):
    s = _sigmoid(x)
    return s * (1.0 + x * (1.0 - s))


GELU_C = math.sqrt(2.0 / math.pi)


def _gelu(x):
    return 0.5 * x * (1.0 + jnp.tanh(GELU_C * (x + 0.044715 * x * x * x)))


def _dgelu(x):
    t = jnp.tanh(GELU_C * (x + 0.044715 * x * x * x))
    return 0.5 * (1.0 + t) + 0.5 * x * (1.0 - t * t) * GELU_C * (1.0 + 3.0 * 0.044715 * x * x)


def _expm1(x):
    poly = x * (1.0 + x * 0.5 * (1.0 + x * (1.0 / 3.0) * (1.0 + x * 0.25 * (1.0 + x * 0.2))))
    return jnp.where(jnp.abs(x) < 0.05, poly, jnp.exp(x) - 1.0)


def _iota2(shape, axis):
    return lax.broadcasted_iota(jnp.int32, shape, axis)


def _col_to_row(col):
    n = col.shape[0]
    eye = _iota2((n, n), 0) == _iota2((n, n), 1)
    return jnp.sum(jnp.where(eye, col, 0.0), axis=0, keepdims=True)


def _row_to_col(row):
    n = row.shape[1]
    eye = _iota2((n, n), 0) == _iota2((n, n), 1)
    return jnp.sum(jnp.where(eye, row, 0.0), axis=1, keepdims=True)


def _pick_row(x, r):
    rows = _iota2(x.shape, 0)
    return jnp.sum(jnp.where(rows == r, x, 0.0), axis=0, keepdims=True)


def _shift_down(x, s):
    if s == 0:
        return x
    y = pltpu.roll(x, s, 0)
    return jnp.where(_iota2(x.shape, 0) < s, 0.0, y)


def _shift_up(x, s):
    if s == 0:
        return x
    n = x.shape[0]
    y = pltpu.roll(x, n - s, 0)
    return jnp.where(_iota2(x.shape, 0) >= n - s, 0.0, y)


def _tile(dim, prefs):
    for p in prefs:
        if dim % p == 0:
            return p
    return dim


TOKEN_SHAPE = (8, LANE)


def _dep_specs(deps):
    return [pl.BlockSpec(TOKEN_SHAPE, lambda *_: (0, 0)) for _ in deps]


def _matmul(a, b, *, ta=False, tb=False, epi="none", extra=None, out_dtype=F32, name, deps=(), b_cols=None):
    if ta:
        K, M = a.shape
    else:
        M, K = a.shape
    if tb:
        N, K2 = b.shape[0], b_cols or b.shape[1]
    else:
        K2, N = b.shape[0], b_cols or b.shape[1]
    assert K == K2, (a.shape, b.shape, ta, tb)
    if K <= 2048:
        tk = K
        if N <= 2048:
            tm, tn = _tile(M, (512, 256, 128)), N
        else:
            tm, tn = _tile(M, (1024, 512, 256, 128)), _tile(N, (512, 256, 128))
    else:
        tm, tn, tk = (_tile(d, (1024, 512, 256, 128)) for d in (M, N, K))
    nk = K // tk
    dn = (((0 if ta else 1,), (1 if tb else 0,)), ((), ()))
    n_extra = 0 if extra is None else 1
    n_out = 2 if epi == "relu2" else 1

    def body(*refs):
        a_ref, b_ref = refs[0], refs[1]
        e_ref = refs[2] if n_extra else None
        outs = refs[2 + n_extra + len(deps):2 + n_extra + len(deps) + n_out]

        def finish(r):
            if epi == "none":
                outs[0][...] = r.astype(outs[0].dtype)
            elif epi == "add":
                outs[0][...] = (r + e_ref[...]).astype(outs[0].dtype)
            elif epi == "relu2":
                outs[0][...] = r
                p = jnp.maximum(r, 0.0)
                outs[1][...] = (p * p).astype(outs[1].dtype)
            elif epi == "drelu2":
                outs[0][...] = (r * 2.0 * jnp.maximum(e_ref[...], 0.0)).astype(outs[0].dtype)

        def product():
            return lax.dot_general(a_ref[...].astype(BF16), b_ref[...].astype(BF16), dn, preferred_element_type=F32)

        if nk == 1:
            finish(product())
            return
        acc = refs[-1]
        k = pl.program_id(2)

        @pl.when(k == 0)
        def _():
            acc[...] = jnp.zeros_like(acc)

        acc[...] += product()

        @pl.when(k == nk - 1)
        def _():
            finish(acc[...])

    a_spec = pl.BlockSpec((tk, tm), lambda i, j, k: (k, i)) if ta else pl.BlockSpec((tm, tk), lambda i, j, k: (i, k))
    b_spec = pl.BlockSpec((tn, tk), lambda i, j, k: (j, k)) if tb else pl.BlockSpec((tk, tn), lambda i, j, k: (k, j))
    o_spec = pl.BlockSpec((tm, tn), lambda i, j, k: (i, j))
    in_specs = [a_spec, b_spec] + ([o_spec] if n_extra else []) + _dep_specs(deps)
    if epi == "relu2":
        out_shape = (jax.ShapeDtypeStruct((M, N), F32), jax.ShapeDtypeStruct((M, N), BF16))
        out_specs = (o_spec, o_spec)
    else:
        out_shape = jax.ShapeDtypeStruct((M, N), out_dtype)
        out_specs = o_spec
    args = (a, b) + ((extra,) if n_extra else ()) + tuple(deps)
    return pl.pallas_call(
        body, name=name, grid=(M // tm, N // tn, nk), in_specs=in_specs, out_specs=out_specs,
        out_shape=out_shape, scratch_shapes=[pltpu.VMEM((tm, tn), F32)] if nk > 1 else [],
        compiler_params=_params(("parallel", "parallel", "arbitrary")),
    )(*args)


ROW_BLOCK = 256


def _rmsnorm_fwd(x, g, *, name, deps=()):
    T, D = x.shape
    tr = _tile(T, (ROW_BLOCK, 128, 64))

    def body(x_ref, g_ref, *rest):
        o_ref = rest[-1]
        xv = x_ref[...]
        r = lax.rsqrt(jnp.mean(xv * xv, axis=-1, keepdims=True) + NORM_EPS)
        o_ref[...] = (xv * r * g_ref[...]).astype(o_ref.dtype)

    return pl.pallas_call(
        body, name=name, grid=(T // tr,),
        in_specs=[pl.BlockSpec((tr, D), lambda i: (i, 0)), pl.BlockSpec((1, D), lambda i: (0, 0))] + _dep_specs(deps),
        out_specs=pl.BlockSpec((tr, D), lambda i: (i, 0)),
        out_shape=jax.ShapeDtypeStruct((T, D), BF16), compiler_params=_params(("parallel",)),
    )(x, g, *deps)


def _rmsnorm_bwd(x, g, dy, dres, *, name, deps=()):
    T, D = x.shape
    tr = _tile(T, (ROW_BLOCK, 128, 64))

    def body(x_ref, g_ref, dy_ref, dres_ref, *rest):
        dx_ref, dxb_ref, dg_ref = rest[len(deps):]
        i = pl.program_id(0)
        xv = x_ref[...]
        r = lax.rsqrt(jnp.mean(xv * xv, axis=-1, keepdims=True) + NORM_EPS)
        xh = xv * r
        dyv = dy_ref[...].astype(F32)
        dxh = dyv * g_ref[...]
        dx = dres_ref[...] + r * (dxh - xh * jnp.mean(dxh * xh, axis=-1, keepdims=True))
        dx_ref[...] = dx
        dxb_ref[...] = dx.astype(dxb_ref.dtype)

        @pl.when(i == 0)
        def _():
            dg_ref[...] = jnp.zeros_like(dg_ref)

        dg_ref[...] += jnp.sum(dyv * xh, axis=0, keepdims=True)

    blk = pl.BlockSpec((tr, D), lambda i: (i, 0))
    vec = pl.BlockSpec((1, D), lambda i: (0, 0))
    return pl.pallas_call(
        body, name=name, grid=(T // tr,), in_specs=[blk, vec, blk, blk] + _dep_specs(deps), out_specs=(blk, blk, vec),
        out_shape=(jax.ShapeDtypeStruct((T, D), F32), jax.ShapeDtypeStruct((T, D), BF16),
                   jax.ShapeDtypeStruct((1, D), F32)),
        compiler_params=_params(("arbitrary",)),
    )(x, g, dy, dres, *deps)


def _final_loss_bwd(x, g, target, *, name):
    T, D = x.shape
    tr = _tile(T, (ROW_BLOCK, 128, 64))

    def body(x_ref, g_ref, t_ref, dx_ref, dxb_ref, dg_ref, l_ref):
        i = pl.program_id(0)
        xv = x_ref[...]
        r = lax.rsqrt(jnp.mean(xv * xv, axis=-1, keepdims=True) + NORM_EPS)
        xh = xv * r
        gv = g_ref[...]
        err = xh * gv - t_ref[...]
        dy = err * (1.0 / D)
        dxh = dy * gv
        dx = r * (dxh - xh * jnp.mean(dxh * xh, axis=-1, keepdims=True))
        dx_ref[...] = dx
        dxb_ref[...] = dx.astype(dxb_ref.dtype)

        @pl.when(i == 0)
        def _():
            l_ref[...] = jnp.zeros_like(l_ref)
            dg_ref[...] = jnp.zeros_like(dg_ref)

        dg_ref[...] += jnp.sum(dy * xh, axis=0, keepdims=True)
        part = 0.5 * jnp.sum(jnp.mean(err * err, axis=-1, keepdims=True), axis=0, keepdims=True)
        l_ref[...] += jnp.broadcast_to(part, l_ref.shape)

    blk = pl.BlockSpec((tr, D), lambda i: (i, 0))
    vec = pl.BlockSpec((1, D), lambda i: (0, 0))
    return pl.pallas_call(
        body, name=name, grid=(T // tr,), in_specs=[blk, vec, blk],
        out_specs=(blk, blk, vec, pl.BlockSpec((1, LANE), lambda i: (0, 0))),
        out_shape=(jax.ShapeDtypeStruct((T, D), F32), jax.ShapeDtypeStruct((T, D), BF16),
                   jax.ShapeDtypeStruct((1, D), F32), jax.ShapeDtypeStruct((1, LANE), F32)),
        compiler_params=_params(("arbitrary",)),
    )(x, g, target)


def _colsum(x, *, name):
    T, C = x.shape
    tc = _tile(C, (512, 256, 128))

    def body(x_ref, o_ref):
        o_ref[...] = jnp.sum(x_ref[...], axis=0, keepdims=True)

    return pl.pallas_call(
        body, name=name, grid=(C // tc,), in_specs=[pl.BlockSpec((T, tc), lambda j: (0, j))],
        out_specs=pl.BlockSpec((1, tc), lambda j: (0, j)),
        out_shape=jax.ShapeDtypeStruct((1, C), F32), compiler_params=_params(("parallel",)),
    )(x)


ADAM_C1 = 1.0 - ADAM_B1 ** ADAM_STEP
ADAM_C2 = 1.0 - ADAM_B2 ** ADAM_STEP


def _adamw(w, m, v, grads, *, name):
    R, C = w.shape
    tr = _tile(R, (256, 128, 64, 32, 16, 8))
    n_g = len(grads)

    def body(*refs):
        w_ref, m_ref, v_ref = refs[:3]
        g_refs = refs[3:3 + n_g]
        g_out, d_out, m_out, v_out = refs[3 + n_g:]
        g = g_refs[0][...].astype(F32)
        for r in g_refs[1:]:
            g = g + r[...].astype(F32)
        mn = ADAM_B1 * m_ref[...] + (1.0 - ADAM_B1) * g
        vn = ADAM_B2 * v_ref[...] + (1.0 - ADAM_B2) * (g * g)
        m_hat = mn / ADAM_C1
        v_hat = vn / ADAM_C2
        g_out[...] = g
        d_out[...] = -ADAM_LR * (m_hat / (jnp.sqrt(v_hat) + ADAM_EPS) + ADAM_WD * w_ref[...])
        m_out[...] = mn
        v_out[...] = vn

    blk = pl.BlockSpec((tr, C), lambda i: (i, 0))
    sh = jax.ShapeDtypeStruct((R, C), F32)
    return pl.pallas_call(
        body, name=name, grid=(R // tr,), in_specs=[blk] * (3 + n_g), out_specs=(blk,) * 4,
        out_shape=(sh,) * 4, compiler_params=_params(("parallel",)),
    )(w, m, v, *grads)


def _sum_parts(parts, *, name, deps=()):
    P, R, C = parts.shape
    tr = _tile(R, (256, 128, 64, 32, 16, 8))

    def body(p_ref, *rest):
        o_ref = rest[-1]
        s = p_ref[0].astype(F32)
        for i in range(1, P):
            s = s + p_ref[i].astype(F32)
        o_ref[...] = s

    return pl.pallas_call(
        body, name=name, grid=(R // tr,),
        in_specs=[pl.BlockSpec((P, tr, C), lambda i: (0, i, 0))] + _dep_specs(deps),
        out_specs=pl.BlockSpec((tr, C), lambda i: (i, 0)),
        out_shape=jax.ShapeDtypeStruct((R, C), F32), compiler_params=_params(("parallel",)),
    )(parts, *deps)


def _ret_tables(T):
    H, C = RET_HEADS, CHUNK
    log_gamma = jnp.log1p(-jnp.exp2(-5.0 - jnp.arange(H, dtype=F32)))
    pos = jnp.arange(C, dtype=F32)
    dist = jnp.abs(pos[:, None] - pos[None, :])
    dm = jnp.exp(log_gamma[:, None, None] * dist)
    qdec = jnp.exp(log_gamma[:, None] * (pos + 1.0))[:, :, None]
    kdec = jnp.exp(log_gamma[:, None] * (C - 1.0 - pos))[:, :, None]
    cdec = jnp.exp(log_gamma * C)[:, None, None]
    inv = ROPE_BASE ** (-jnp.arange(0, RET_DK, 2, dtype=F32) / RET_DK)
    ang = jnp.arange(T, dtype=F32)[:, None] * inv[None, :]
    return dm, qdec, kdec, cdec, jnp.cos(ang), jnp.sin(ang)


def _rot(x, cos, sin):
    h = x.shape[1] // 2
    x1, x2 = x[:, :h], x[:, h:]
    return jnp.concatenate([x1 * cos - x2 * sin, x1 * sin + x2 * cos], axis=1)


def _unrot(dy, cos, sin):
    h = dy.shape[1] // 2
    d1, d2 = dy[:, :h], dy[:, h:]
    return jnp.concatenate([d1 * cos + d2 * sin, d2 * cos - d1 * sin], axis=1)


def _ret_specs(N, rev):
    H, C, DK, DV = RET_HEADS, CHUNK, RET_DK, RET_DV
    cn = (lambda n: N - 1 - n) if rev else (lambda n: n)
    HB, G = RET_HB, H // RET_HB
    q = pl.BlockSpec((C, HB * DK), lambda h, n: (cn(n), h))
    k = pl.BlockSpec((C, HB * DK), lambda h, n: (cn(n), G + h))
    v = pl.BlockSpec((C, HB * DV), lambda h, n: (cn(n), G + h))
    g = pl.BlockSpec((C, HB * DV), lambda h, n: (cn(n), 2 * G + h))
    cs = pl.BlockSpec((C, DK // 2), lambda h, n: (cn(n), 0))
    dm = pl.BlockSpec((HB, C, C), lambda h, n: (h, 0, 0))
    dec = pl.BlockSpec((HB, C, 1), lambda h, n: (h, 0, 0))
    cd = pl.BlockSpec((HB, 1, 1), lambda h, n: (h, 0, 0))
    gain = pl.BlockSpec((HB, 1, DV), lambda h, n: (h, 0, 0))
    st = pl.BlockSpec((HB, 1, DK, DV), lambda h, n: (h, cn(n), 0, 0))
    ov = pl.BlockSpec((C, HB * DV), lambda h, n: (cn(n), h))
    return q, k, v, g, cs, dm, dec, cd, gain, st, ov


def _ret_fwd(proj, gain, tables, *, name):
    T = proj.shape[0]
    H, C, DK, DV = RET_HEADS, CHUNK, RET_DK, RET_DV
    N = T // C
    dm_t, qdec_t, kdec_t, cdec_t, cos_t, sin_t = tables

    def body(q_ref, k_ref, v_ref, g_ref, cos_ref, sin_ref, dm_ref, qd_ref, kd_ref, cd_ref, gain_ref,
             og_ref, st_ref, s_acc):
        n = pl.program_id(1)

        @pl.when(n == 0)
        def _():
            s_acc[...] = jnp.zeros_like(s_acc)

        cos, sin = cos_ref[...], sin_ref[...]

        def head(hb):
            kc, vc = pl.ds(hb * DK, DK), pl.ds(hb * DV, DV)
            qr = _rot(q_ref[:, kc], cos, sin)
            kr = _rot(k_ref[:, kc], cos, sin) * (DK ** -0.5)
            v = v_ref[:, vc]
            sp = s_acc[hb]
            st_ref[hb, 0] = sp.astype(st_ref.dtype)
            scores = _bdot(qr, kr, NT) * dm_ref[hb]
            inter = _bdot(qr * qd_ref[hb], sp)
            s_acc[hb] = sp * cd_ref[hb] + _bdot(kr * kd_ref[hb], v, TN)
            yield
            o = _bdot(scores, v) + inter
            yield
            oc = o - jnp.mean(o, axis=-1, keepdims=True)
            rstd = lax.rsqrt(jnp.mean(oc * oc, axis=-1, keepdims=True) + NORM_EPS)
            og_ref[:, vc] = (oc * rstd * gain_ref[hb] * _silu(g_ref[:, vc])).astype(og_ref.dtype)

        _interleave([head(hb) for hb in range(RET_HB)])

    q, k, v, g, cs, dm, dec, cd, gn, st, ov = _ret_specs(N, False)
    return pl.pallas_call(
        body, name=name, grid=(H // RET_HB, N),
        in_specs=[q, k, v, g, cs, cs, dm, dec, dec, cd, gn], out_specs=(ov, st),
        out_shape=(jax.ShapeDtypeStruct((T, H * DV), BF16), jax.ShapeDtypeStruct((H, N, DK, DV), BF16)),
        scratch_shapes=[pltpu.VMEM((RET_HB, DK, DV), F32)],
        compiler_params=_params(("arbitrary", "arbitrary")),
    )(proj, proj, proj, proj, cos_t, sin_t, dm_t, qdec_t, kdec_t, cdec_t, gain)


def _ret_bwd(proj, gain, tables, states, dog, *, name):
    T = proj.shape[0]
    H, C, DK, DV = RET_HEADS, CHUNK, RET_DK, RET_DV
    N = T // C
    dm_t, qdec_t, kdec_t, cdec_t, cos_t, sin_t = tables

    def body(q_ref, k_ref, v_ref, g_ref, cos_ref, sin_ref, dm_ref, qd_ref, kd_ref, cd_ref, gain_ref,
             st_ref, dog_ref, dq_ref, dk_ref, dv_ref, dg_ref, dgain_ref, ds_acc):
        n = pl.program_id(1)

        @pl.when(n == 0)
        def _():
            ds_acc[...] = jnp.zeros_like(ds_acc)
            dgain_ref[...] = jnp.zeros_like(dgain_ref)

        cos, sin = cos_ref[...], sin_ref[...]
        scale = DK ** -0.5

        def head(hb):
            kc, vc = pl.ds(hb * DK, DK), pl.ds(hb * DV, DV)
            qr = _rot(q_ref[:, kc], cos, sin)
            kr = _rot(k_ref[:, kc], cos, sin) * scale
            v = v_ref[:, vc]
            g = g_ref[:, vc]
            sp = st_ref[hb, 0]
            dm = dm_ref[hb]
            qd, kd = qd_ref[hb], kd_ref[hb]
            gain_v = gain_ref[hb]
            scores = _bdot(qr, kr, NT) * dm
            inter = _bdot(qr * qd, sp)
            yield
            o = _bdot(scores, v) + inter
            yield
            oc = o - jnp.mean(o, axis=-1, keepdims=True)
            rstd = lax.rsqrt(jnp.mean(oc * oc, axis=-1, keepdims=True) + NORM_EPS)
            oh = oc * rstd
            dy = dog_ref[:, vc].astype(F32)
            dg_ref[:, vc] = (dy * oh * gain_v * _dsilu(g)).astype(dg_ref.dtype)
            dnorm = dy * _silu(g)
            dgain_ref[hb] += jnp.sum(dnorm * oh, axis=0, keepdims=True)
            doh = dnorm * gain_v
            do = rstd * (doh - jnp.mean(doh, axis=-1, keepdims=True)
                         - oh * jnp.mean(doh * oh, axis=-1, keepdims=True))
            dsn = ds_acc[hb]
            dp = _bdot(do, v, NT) * dm
            dq_inter = _bdot(do, sp, NT) * qd
            dk_inter = _bdot(v, dsn, NT) * kd
            dv_ref[:, vc] = (_bdot(scores, do, TN) + _bdot(kr * kd, dsn)).astype(dv_ref.dtype)
            ds_acc[hb] = dsn * cd_ref[hb] + _bdot(qr * qd, do, TN)
            yield
            dqr = _bdot(dp, kr) + dq_inter
            dkr = _bdot(dp, qr, TN) + dk_inter
            yield
            dq_ref[:, kc] = _unrot(dqr, cos, sin).astype(dq_ref.dtype)
            dk_ref[:, kc] = _unrot(dkr * scale, cos, sin).astype(dk_ref.dtype)

        _interleave([head(hb) for hb in range(RET_HB)])

    q, k, v, g, cs, dm, dec, cd, gn, st, ov = _ret_specs(N, True)
    return pl.pallas_call(
        body, name=name, grid=(H // RET_HB, N),
        in_specs=[q, k, v, g, cs, cs, dm, dec, dec, cd, gn, st, ov],
        out_specs=(q, q, ov, ov, gn),
        out_shape=(jax.ShapeDtypeStruct((T, H * DK), BF16), jax.ShapeDtypeStruct((T, H * DK), BF16),
                   jax.ShapeDtypeStruct((T, H * DV), BF16), jax.ShapeDtypeStruct((T, H * DV), BF16),
                   jax.ShapeDtypeStruct((H, 1, DV), F32)),
        scratch_shapes=[pltpu.VMEM((RET_HB, DK, DV), F32)],
        compiler_params=_params(("arbitrary", "arbitrary")),
    )(proj, proj, proj, proj, cos_t, sin_t, dm_t, qdec_t, kdec_t, cdec_t, gain, states, dog)


def _gla_specs(N, rev):
    H, C, DK, DV = GLA_HEADS, CHUNK, GLA_DK, GLA_DV
    cn = (lambda n: N - 1 - n) if rev else (lambda n: n)
    HB, G = GLA_HB, H // GLA_HB
    q = pl.BlockSpec((C, HB * DK), lambda h, n: (cn(n), h))
    k = pl.BlockSpec((C, HB * DK), lambda h, n: (cn(n), G + h))
    v = pl.BlockSpec((C, HB * DV), lambda h, n: (cn(n), G + h))
    r = pl.BlockSpec((C, HB * DV), lambda h, n: (cn(n), 2 * G + h))
    bias = pl.BlockSpec((1, HB * DK), lambda h, n: (0, h))
    gain = pl.BlockSpec((HB, 1, DV), lambda h, n: (h, 0, 0))
    st = pl.BlockSpec((HB, 1, DV, DK), lambda h, n: (h, cn(n), 0, 0))
    ov = pl.BlockSpec((C, HB * DV), lambda h, n: (cn(n), h))
    return q, k, v, r, bias, gain, st, ov


def _gla_chunk(q, k, v, gl_raw, bias):
    C, DK = q.shape
    gl = gl_raw + bias
    la = -_softplus(-gl) * (1.0 / GLA_TAU)
    lower = _iota2((C, C), 0) >= _iota2((C, C), 1)
    cum = _fdot(jnp.where(lower, 1.0, 0.0), la)
    yield
    ref = _pick_row(cum, C // 2 - 1)
    clast = _pick_row(cum, C - 1)
    fw, bw = jnp.exp(cum - ref), jnp.exp(ref - cum)
    qs = q * (DK ** -0.5)
    s_lo = _bdot(qs * fw, k * bw, NT)
    s_up = _bdot(qs * bw, k * fw, NT)
    yield
    scores = jnp.where(lower, s_lo, s_up)
    return gl, cum, clast, fw, bw, qs, k, v, scores, lower


def _gla_fwd(proj, glogit, bias, gain, *, name):
    T = proj.shape[0]
    H, C, DK, DV = GLA_HEADS, CHUNK, GLA_DK, GLA_DV
    N = T // C

    def body(q_ref, k_ref, v_ref, r_ref, gl_ref, bias_ref, gain_ref, og_ref, st_ref, s_acc):
        n = pl.program_id(1)

        @pl.when(n == 0)
        def _():
            s_acc[...] = jnp.zeros_like(s_acc)

        def head(hb):
            kc, vc = pl.ds(hb * DK, DK), pl.ds(hb * DV, DV)
            gl, cum, clast, fw, bw, qs, k, v, scores, lower = yield from _gla_chunk(
                q_ref[:, kc], k_ref[:, kc], v_ref[:, vc], gl_ref[:, kc], bias_ref[:, kc])
            sp = s_acc[hb]
            st_ref[hb, 0] = sp
            o = _bdot(scores, v) + _bdot(qs * jnp.exp(cum), sp, NT)
            s_acc[hb] = sp * jnp.exp(clast) + _bdot(v, k * jnp.exp(clast - cum), TN)
            yield
            rstd = lax.rsqrt(jnp.mean(o * o, axis=-1, keepdims=True) + NORM_EPS)
            og_ref[:, vc] = (o * rstd * gain_ref[hb] * _silu(r_ref[:, vc])).astype(og_ref.dtype)

        _interleave([head(hb) for hb in range(GLA_HB)])

    q, k, v, r, bias_s, gn, st, ov = _gla_specs(N, False)
    return pl.pallas_call(
        body, name=name, grid=(H // GLA_HB, N), in_specs=[q, k, v, r, q, bias_s, gn], out_specs=(ov, st),
        out_shape=(jax.ShapeDtypeStruct((T, H * DV), BF16), jax.ShapeDtypeStruct((H, N, DV, DK), F32)),
        scratch_shapes=[pltpu.VMEM((GLA_HB, DV, DK), F32)],
        compiler_params=_params(("arbitrary", "arbitrary")),
    )(proj, proj, proj, proj, glogit, bias, gain)


def _gla_bwd(proj, glogit, bias, gain, states, dog, *, name):
    T = proj.shape[0]
    H, C, DK, DV = GLA_HEADS, CHUNK, GLA_DK, GLA_DV
    N = T // C

    def body(q_ref, k_ref, v_ref, r_ref, gl_ref, bias_ref, gain_ref, st_ref, dog_ref,
             dq_ref, dk_ref, dv_ref, dr_ref, dgl_ref, dgain_ref, ds_acc):
        n = pl.program_id(1)

        @pl.when(n == 0)
        def _():
            ds_acc[...] = jnp.zeros_like(ds_acc)
            dgain_ref[...] = jnp.zeros_like(dgain_ref)

        def head(hb):
            kc, vc = pl.ds(hb * DK, DK), pl.ds(hb * DV, DV)
            gl, cum, clast, fw, bw, qs, k, v, scores, lower = yield from _gla_chunk(
                q_ref[:, kc], k_ref[:, kc], v_ref[:, vc], gl_ref[:, kc], bias_ref[:, kc])
            sp = st_ref[hb, 0]
            ecum, e2, cdec = jnp.exp(cum), jnp.exp(clast - cum), jnp.exp(clast)
            q_in, k_end = qs * ecum, k * e2
            o = _bdot(scores, v) + _bdot(q_in, sp, NT)
            yield
            rstd = lax.rsqrt(jnp.mean(o * o, axis=-1, keepdims=True) + NORM_EPS)
            oh = o * rstd
            r = r_ref[:, vc]
            gain_v = gain_ref[hb]
            dy = dog_ref[:, vc].astype(F32)
            dr_ref[:, vc] = (dy * oh * gain_v * _dsilu(r)).astype(dr_ref.dtype)
            dnorm = dy * _silu(r)
            dgain_ref[hb] += jnp.sum(dnorm * oh, axis=0, keepdims=True)
            doh = dnorm * gain_v
            do = rstd * (doh - oh * jnp.mean(doh * oh, axis=-1, keepdims=True))
            dsn = ds_acc[hb]
            dq_in = _bdot(do, sp)
            dk_end = _bdot(v, dsn)
            dv_ref[:, vc] = (_bdot(k_end, dsn, NT) + _bdot(scores, do, TN)).astype(dv_ref.dtype)
            dcdec = jnp.sum(dsn * sp, axis=0, keepdims=True)
            ds_acc[hb] = dsn * cdec + _bdot(do, q_in, TN)
            dsc = _bdot(do, v, NT)
            yield
            ds_lo = jnp.where(lower, dsc, 0.0)
            ds_up = jnp.where(lower, 0.0, dsc)
            qf, kb, qb, kf = qs * fw, k * bw, qs * bw, k * fw
            dqf, dkb = _bdot(ds_lo, kb), _bdot(ds_lo, qf, TN)
            dqb, dkf = _bdot(ds_up, kf), _bdot(ds_up, qb, TN)
            yield
            dq_ref[:, kc] = ((dqf * fw + dqb * bw + dq_in * ecum) * (DK ** -0.5)).astype(dq_ref.dtype)
            dk_ref[:, kc] = (dkb * bw + dkf * fw + dk_end * e2).astype(dk_ref.dtype)
            dz = (dqf * qs + dkf * k) * fw - (dqb * qs + dkb * k) * bw
            kk = dk_end * k_end
            dcum = dz + dq_in * q_in - kk
            rows = _iota2((C, DK), 0)
            dcum = dcum + jnp.where(rows == C // 2 - 1, -jnp.sum(dz, axis=0, keepdims=True), 0.0)
            dcum = dcum + jnp.where(rows == C - 1, jnp.sum(kk, axis=0, keepdims=True) + dcdec * cdec, 0.0)
            upper = _iota2((C, C), 0) <= _iota2((C, C), 1)
            dla = _fdot(jnp.where(upper, 1.0, 0.0), dcum)
            yield
            dgl_ref[:, kc] = dla * (1.0 / GLA_TAU) * _sigmoid(-gl)

        _interleave([head(hb) for hb in range(GLA_HB)])

    q, k, v, r, bias_s, gn, st, ov = _gla_specs(N, True)
    return pl.pallas_call(
        body, name=name, grid=(H // GLA_HB, N), in_specs=[q, k, v, r, q, bias_s, gn, st, ov],
        out_specs=(q, q, ov, ov, q, gn),
        out_shape=(jax.ShapeDtypeStruct((T, H * DK), BF16), jax.ShapeDtypeStruct((T, H * DK), BF16),
                   jax.ShapeDtypeStruct((T, H * DV), BF16), jax.ShapeDtypeStruct((T, H * DV), BF16),
                   jax.ShapeDtypeStruct((T, H * DK), F32), jax.ShapeDtypeStruct((H, 1, DV), F32)),
        scratch_shapes=[pltpu.VMEM((GLA_HB, DV, DK), F32)],
        compiler_params=_params(("arbitrary", "arbitrary")),
    )(proj, proj, proj, proj, glogit, bias, gain, states, dog)


def _conv(xv, w_ref):
    out = _shift_down(xv, CONV_WIDTH - 1) * w_ref[0:1, :]
    for tap in range(1, CONV_WIDTH):
        out = out + _shift_down(xv, CONV_WIDTH - 1 - tap) * w_ref[tap:tap + 1, :]
    return out


def _conv_bwd(xv, w_ref, dpre, dw_ref):
    dx = None
    for tap in range(CONV_WIDTH):
        s = CONV_WIDTH - 1 - tap
        t = _shift_up(dpre, s) * w_ref[tap:tap + 1, :]
        dx = t if dx is None else dx + t
        dw_ref[tap:tap + 1, :] = jnp.sum(dpre * _shift_down(xv, s), axis=0, keepdims=True)
    return dx


CONV_COLS = 256


def _conv_silu_fwd(x, w, *, name):
    T = x.shape[0]
    n = w.shape[1]

    def body(x_ref, w_ref, o_ref):
        o_ref[...] = _silu(_conv(x_ref[...], w_ref))

    return pl.pallas_call(
        body, name=name, grid=(n // CONV_COLS,),
        in_specs=[pl.BlockSpec((T, CONV_COLS), lambda j: (0, j)), pl.BlockSpec((CONV_WIDTH, CONV_COLS), lambda j: (0, j))],
        out_specs=pl.BlockSpec((T, CONV_COLS), lambda j: (0, j)),
        out_shape=jax.ShapeDtypeStruct((T, n), F32), compiler_params=_params(("parallel",)),
    )(x, w)


def _conv_silu_bwd(x, w, dact, *, name):
    T = x.shape[0]
    n = w.shape[1]

    def body(x_ref, w_ref, da_ref, dx_ref, dw_ref):
        xv = x_ref[...]
        dpre = da_ref[...] * _dsilu(_conv(xv, w_ref))
        dx_ref[...] = _conv_bwd(xv, w_ref, dpre, dw_ref).astype(dx_ref.dtype)

    blk = pl.BlockSpec((T, CONV_COLS), lambda j: (0, j))
    wb = pl.BlockSpec((CONV_WIDTH, CONV_COLS), lambda j: (0, j))
    return pl.pallas_call(
        body, name=name, grid=(n // CONV_COLS,), in_specs=[blk, wb, blk], out_specs=(blk, wb),
        out_shape=(jax.ShapeDtypeStruct((T, n), BF16), jax.ShapeDtypeStruct((CONV_WIDTH, n), F32)),
        compiler_params=_params(("parallel",)),
    )(x, w, dact)


def _interleave(gens):
    results = [None] * len(gens)
    live = list(range(len(gens)))
    while live:
        for i in list(live):
            try:
                next(gens[i])
            except StopIteration as done:
                results[i] = done.value
                live.remove(i)
    return results


def _unit_lower_inverse(a):
    n = a.shape[0]
    eye = jnp.where(_iota2((n, n), 0) == _iota2((n, n), 1), 1.0, 0.0)
    p = -a
    t = eye + p
    for _ in range(5):
        p = _fdot(p, p)
        yield
        t = t + _fdot(t, p)
        yield
    return t


def _gdn_specs(N, rev):
    H, C, DK, DV = GDN_HEADS, CHUNK, GDN_DK, GDN_DV
    cn = (lambda n: N - 1 - n) if rev else (lambda n: n)
    HB, G = GDN_HB, H // GDN_HB
    q = pl.BlockSpec((C, HB * DK), lambda h, n: (cn(n), h))
    k = pl.BlockSpec((C, HB * DK), lambda h, n: (cn(n), G + h))
    v = pl.BlockSpec((C, HB * DV), lambda h, n: (cn(n), 2 * G + h))
    z = pl.BlockSpec((C, HB * DV), lambda h, n: (cn(n), 3 * G + h))
    gates = pl.BlockSpec((C, LANE), lambda h, n: (cn(n), 0))
    sc = pl.BlockSpec((HB, 1, 1), lambda h, n: (h, 0, 0))
    gain = pl.BlockSpec((1, DV), lambda h, n: (0, 0))
    st = pl.BlockSpec((HB, 1, DK, DV), lambda h, n: (h, cn(n), 0, 0))
    return q, k, v, z, gates, sc, gain, st


def _gdn_solved_specs(N, rev):
    C = CHUNK
    cn = (lambda n: N - 1 - n) if rev else (lambda n: n)
    return (pl.BlockSpec((GDN_HB, 1, C, C), lambda h, n: (h, cn(n), 0, 0)),
            pl.BlockSpec((GDN_HB, 1, C, GDN_DV + GDN_DK), lambda h, n: (h, cn(n), 0, 0)))


def _gdn_chunk(q_ref, k_ref, v_ref, gates_ref, alog_ref, dtb_ref, h, solved=None):
    H, C, DK, DV = GDN_HEADS, CHUNK, GDN_DK, GDN_DV
    gates = gates_ref[...]
    lane = _iota2(gates.shape, 1)
    bl = jnp.sum(jnp.where(lane == h, gates, 0.0), axis=1, keepdims=True)
    al = jnp.sum(jnp.where(lane == H + h, gates, 0.0), axis=1, keepdims=True)
    beta = _sigmoid(bl)
    ea = jnp.exp(alog_ref[...])
    xs = al + dtb_ref[...]
    la = -ea * _softplus(xs)
    ii, jj = _iota2((C, C), 0), _iota2((C, C), 1)
    strict = ii > jj
    cum_col = jnp.sum(jnp.where(ii >= jj, _col_to_row(la), 0.0), axis=1, keepdims=True)
    cum_row = jnp.sum(jnp.where(ii <= jj, la, 0.0), axis=0, keepdims=True)
    q, k, v = q_ref[...], k_ref[...], v_ref[...]
    rq = lax.rsqrt(jnp.sum(q * q, axis=-1, keepdims=True) + NORM_EPS)
    rk = lax.rsqrt(jnp.sum(k * k, axis=-1, keepdims=True) + NORM_EPS)
    qn = q * rq * (DK ** -0.5)
    kn = k * rk
    rel = jnp.where(strict, jnp.exp(jnp.where(strict, cum_col - cum_row, 0.0)), 0.0)
    rg = rel * _bdot(kn, kn, NT)
    yield
    a = beta * rg
    e_col = jnp.exp(cum_col)
    clast = _pick_row(cum_col, C - 1)
    if solved is None:
        tm = yield from _unit_lower_inverse(a)
        rhs = jnp.concatenate([beta * v, (beta * e_col) * kn], axis=1)
        sol = _fdot(tm, rhs)
        yield
    else:
        tm, sol = solved
    u, w = sol[:, :DV], sol[:, DV:]
    dd = jnp.exp(clast - cum_col)
    ke = kn * dd
    g = jnp.exp(clast)
    eye = jnp.where(_iota2((DK, DK), 0) == _iota2((DK, DK), 1), 1.0, 0.0)
    trans = g * eye - _bdot(ke, w, TN)
    inject = _bdot(ke, u, TN)
    yield
    return dict(beta=beta, ea=ea, xs=xs, la=la, strict=strict, ii=ii, jj=jj, q=q, k=k, v=v, rq=rq, rk=rk,
                qn=qn, kn=kn, rel=rel, rg=rg, a=a, tm=tm, e_col=e_col, sol=sol, u=u, w=w, dd=dd, ke=ke,
                g=g, eye=eye, trans=trans, inject=inject)


def _gdn_fwd(act, proj, gates, a_log, dt_bias, gain, *, name):
    T = act.shape[0]
    H, C, DK, DV = GDN_HEADS, CHUNK, GDN_DK, GDN_DV
    N = T // C

    def body(q_ref, k_ref, v_ref, z_ref, gates_ref, alog_ref, dtb_ref, gain_ref, og_ref, st_ref, tm_ref, sol_ref,
             s_acc):
        hg, n = pl.program_id(0), pl.program_id(1)

        @pl.when(n == 0)
        def _():
            s_acc[...] = jnp.zeros_like(s_acc)

        def head(hb):
            cols = pl.ds(hb * DK, DK)
            c = yield from _gdn_chunk(q_ref.at[:, cols], k_ref.at[:, cols], v_ref.at[:, cols], gates_ref,
                                      alog_ref.at[hb], dtb_ref.at[hb], hg * GDN_HB + hb)
            tm_ref[hb, 0] = c["tm"]
            sol_ref[hb, 0] = c["sol"]
            sp = s_acc[hb]
            st_ref[hb, 0] = sp
            snew = _bdot(c["trans"], sp) + c["inject"]
            yield
            s_acc[hb] = snew
            o = _bdot(c["qn"], snew)
            yield
            rstd = lax.rsqrt(jnp.mean(o * o, axis=-1, keepdims=True) + NORM_EPS)
            og_ref[:, cols] = (o * rstd * gain_ref[...] * _silu(z_ref[:, cols])).astype(og_ref.dtype)

        _interleave([head(hb) for hb in range(GDN_HB)])

    q, k, v, z, gt, sc, gn, st = _gdn_specs(N, False)
    tm_s, sol_s = _gdn_solved_specs(N, False)
    return pl.pallas_call(
        body, name=name, grid=(H // GDN_HB, N), in_specs=[q, k, v, z, gt, sc, sc, gn],
        out_specs=(q, st, tm_s, sol_s),
        out_shape=(jax.ShapeDtypeStruct((T, H * DV), BF16), jax.ShapeDtypeStruct((H, N, DK, DV), F32),
                   jax.ShapeDtypeStruct((H, N, C, C), F32), jax.ShapeDtypeStruct((H, N, C, DV + DK), F32)),
        scratch_shapes=[pltpu.VMEM((GDN_HB, DK, DV), F32)],
        compiler_params=_params(("arbitrary", "arbitrary")),
    )(act, act, act, proj, gates, a_log, dt_bias, gain)


def _gdn_bwd(act, proj, gates, a_log, dt_bias, gain, states, solved, dog, *, name):
    T = act.shape[0]
    H, C, DK, DV = GDN_HEADS, CHUNK, GDN_DK, GDN_DV
    N = T // C

    def rsum(x):
        return jnp.sum(x, axis=1, keepdims=True)

    def body(q_ref, k_ref, v_ref, z_ref, gates_ref, alog_ref, dtb_ref, gain_ref, st_ref, tm_ref, sol_ref, dog_ref,
             dq_ref, dk_ref, dv_ref, dz_ref, dgates_ref, dscal_ref, dgain_ref, ds_acc):
        hg, n = pl.program_id(0), pl.program_id(1)

        @pl.when(n == 0)
        def _():
            ds_acc[...] = jnp.zeros_like(ds_acc)
            dgain_ref[...] = jnp.zeros_like(dgain_ref)
            dscal_ref[...] = jnp.zeros_like(dscal_ref)

        _interleave([one_head(hb, hg * GDN_HB + hb, q_ref, k_ref, v_ref, z_ref, gates_ref, alog_ref, dtb_ref, gain_ref,
                              st_ref, tm_ref, sol_ref, dog_ref, dq_ref, dk_ref, dv_ref, dz_ref, dgates_ref, dscal_ref,
                              dgain_ref, ds_acc)
                     for hb in range(GDN_HB)])

    def one_head(hb, h, q_ref, k_ref, v_ref, z_ref, gates_ref, alog_ref, dtb_ref, gain_ref, st_ref, tm_ref, sol_ref,
                 dog_ref, dq_ref, dk_ref, dv_ref, dz_ref, dgates_ref, dscal_ref, dgain_ref, ds_acc):
        cols = pl.ds(hb * DK, DK)
        c = yield from _gdn_chunk(q_ref.at[:, cols], k_ref.at[:, cols], v_ref.at[:, cols], gates_ref,
                                  alog_ref.at[hb], dtb_ref.at[hb], h, solved=(tm_ref[hb, 0], sol_ref[hb, 0]))
        beta, kn, qn, v, ke, u, w, dd, e_col = c["beta"], c["kn"], c["qn"], c["v"], c["ke"], c["u"], c["w"], c["dd"], c["e_col"]
        sp = st_ref[hb, 0]
        snew = _bdot(c["trans"], sp) + c["inject"]
        yield
        o = _bdot(qn, snew)
        yield
        rstd = lax.rsqrt(jnp.mean(o * o, axis=-1, keepdims=True) + NORM_EPS)
        oh = o * rstd
        z = z_ref[:, cols]
        gain_v = gain_ref[...]
        dy = dog_ref[:, cols].astype(F32)
        dz_ref[:, cols] = (dy * oh * gain_v * _dsilu(z)).astype(dz_ref.dtype)
        dnorm = dy * _silu(z)
        dgain_ref[hb] += jnp.sum(dnorm * oh, axis=0, keepdims=True)
        doh = dnorm * gain_v
        do = rstd * (doh - oh * jnp.mean(doh * oh, axis=-1, keepdims=True))

        dstot = ds_acc[hb] + _bdot(qn, do, TN)
        dqn = _bdot(do, snew, NT)
        yield
        dtrans = _bdot(dstot, sp, NT)
        ds_acc[hb] = _bdot(c["trans"], dstot, TN)
        yield
        dg = jnp.sum(jnp.sum(dtrans * c["eye"], axis=1, keepdims=True), axis=0, keepdims=True)
        m = -dtrans
        dke = _bdot(w, m, NT) + _bdot(u, dstot, NT)
        dw = _bdot(ke, m)
        du = _bdot(ke, dstot)
        yield
        drhs = _fdot(c["tm"], jnp.concatenate([du, dw], axis=1), TN)
        yield
        da = jnp.where(c["strict"], -_fdot(drhs, c["sol"], NT), 0.0)
        yield
        drhs_u, drhs_w = drhs[:, :DV], drhs[:, DV:]
        rwk = rsum(drhs_w * kn)
        dbeta = rsum(da * c["rg"]) + rsum(drhs_u * v) + rwk * e_col
        dgm = da * beta * c["rel"]
        dkn = _bdot(dgm, kn) + _bdot(dgm, kn, TN) + (beta * e_col) * drhs_w + dd * dke
        yield
        dv_ref[:, cols] = beta * drhs_u
        r_ = da * c["a"]
        ddd = rsum(dke * kn)
        dc = rsum(r_) - _row_to_col(jnp.sum(r_, axis=0, keepdims=True)) + beta * rwk * e_col - ddd * dd
        dclast = jnp.sum(ddd * dd, axis=0, keepdims=True) + dg * c["g"]
        dc = dc + jnp.where(_iota2((C, 1), 0) == C - 1, dclast, 0.0)
        dla = jnp.sum(jnp.where(c["ii"] <= c["jj"], _col_to_row(dc), 0.0), axis=1, keepdims=True)
        dalog = jnp.sum(dla * c["la"], axis=0, keepdims=True)
        dxs = dla * (-c["ea"]) * _sigmoid(c["xs"])
        ddtb = jnp.sum(dxs, axis=0, keepdims=True)
        dbl = dbeta * beta * (1.0 - beta)
        lane = _iota2((C, LANE), 1)
        dgates_ref[hb] = jnp.where(lane == 0, dbl, jnp.where(lane == 1, dxs, 0.0))
        lane8 = _iota2((8, LANE), 1)
        dscal_ref[hb] += jnp.where(lane8 == 0, dalog, jnp.where(lane8 == 1, ddtb, 0.0))
        dk_ref[:, cols] = c["rk"] * (dkn - kn * rsum(dkn * kn))
        qh = c["q"] * c["rq"]
        dqs = dqn * (DK ** -0.5)
        dq_ref[:, cols] = c["rq"] * (dqs - qh * rsum(dqs * qh))

    q, k, v, z, gt, sc, gn, st = _gdn_specs(N, True)
    dgt = pl.BlockSpec((GDN_HB, C, LANE), lambda h, n: (h, N - 1 - n, 0))
    dsc = pl.BlockSpec((GDN_HB, 8, LANE), lambda h, n: (h, 0, 0))
    dgn = pl.BlockSpec((GDN_HB, 1, DV), lambda h, n: (h, 0, 0))
    sh = jax.ShapeDtypeStruct((T, H * DK), F32)
    tm_s, sol_s = _gdn_solved_specs(N, True)
    return pl.pallas_call(
        body, name=name, grid=(H // GDN_HB, N), in_specs=[q, k, v, z, gt, sc, sc, gn, st, tm_s, sol_s, q],
        out_specs=(q, q, q, q, dgt, dsc, dgn),
        out_shape=(sh, sh, sh, jax.ShapeDtypeStruct((T, H * DV), BF16),
                   jax.ShapeDtypeStruct((H, T, LANE), F32), jax.ShapeDtypeStruct((H, 8, LANE), F32),
                   jax.ShapeDtypeStruct((H, 1, DV), F32)),
        scratch_shapes=[pltpu.VMEM((GDN_HB, DK, DV), F32)],
        compiler_params=_params(("arbitrary", "arbitrary")),
    )(act, act, act, proj, gates, a_log, dt_bias, gain, states, *solved, dog)


SUBLANES = 8


def _linear_scan(a_ref, b_ref, h_ref, reverse):
    T, W = a_ref.shape
    nb = T // SUBLANES
    row = _iota2((SUBLANES, W), 0)

    def blk(bi, carry):
        bb = (nb - 1 - bi) if reverse else bi
        off = pl.multiple_of(bb * SUBLANES, SUBLANES)
        a = a_ref[pl.ds(off, SUBLANES), :]
        b = b_ref[pl.ds(off, SUBLANES), :]
        for d in (1, 2, 4):
            if reverse:
                edge = row >= SUBLANES - d
                a_sh = jnp.where(edge, 1.0, pltpu.roll(a, SUBLANES - d, 0))
                b_sh = jnp.where(edge, 0.0, pltpu.roll(b, SUBLANES - d, 0))
            else:
                edge = row < d
                a_sh = jnp.where(edge, 1.0, pltpu.roll(a, d, 0))
                b_sh = jnp.where(edge, 0.0, pltpu.roll(b, d, 0))
            b = a * b_sh + b
            a = a * a_sh
        h = a * carry + b
        h_ref[pl.ds(off, SUBLANES), :] = h
        return h[0:1, :] if reverse else h[SUBLANES - 1:SUBLANES, :]

    lax.fori_loop(0, nb, blk, jnp.zeros((1, W), F32))


def _lru_specs(T):
    B, W = LRU_BLOCKS, LRU_BLOCK
    xb = pl.BlockSpec((T, W), lambda j: (0, j))
    yb = pl.BlockSpec((T, W), lambda j: (0, B + j))
    cw = pl.BlockSpec((CONV_WIDTH, W), lambda j: (0, j))
    vec = pl.BlockSpec((1, W), lambda j: (0, j))
    wg = pl.BlockSpec((1, W, W), lambda j: (j, 0, 0))
    bg = pl.BlockSpec((1, 1, W), lambda j: (j, 0, 0))
    return xb, yb, cw, vec, wg, bg


def _lru_gates(xb_ref, cw_ref, cb_ref, wr_ref, br_ref, wi_ref, bi_ref, lam_ref):
    xv = xb_ref[...]
    xc = _conv(xv, cw_ref) + cb_ref[...]
    r = _sigmoid(_bdot(xc, wr_ref[0]) + br_ref[0])
    i = _sigmoid(_bdot(xc, wi_ref[0]) + bi_ref[0])
    sp = _softplus(-lam_ref[...])
    la = -LRU_C * sp * r
    a = jnp.exp(la)
    s = jnp.sqrt(-_expm1(2.0 * la))
    return xv, xc, r, i, sp, a, s


def _lru_fwd(proj, conv_w, conv_b, w_r, b_r, w_i, b_i, lam, *, name):
    T = proj.shape[0]
    B, W = LRU_BLOCKS, LRU_BLOCK

    def body(xb_ref, yb_ref, cw_ref, cb_ref, wr_ref, br_ref, wi_ref, bi_ref, lam_ref, og_ref, hs_ref, a_s, u_s):
        xv, xc, r, i, sp, a, s = _lru_gates(xb_ref, cw_ref, cb_ref, wr_ref, br_ref, wi_ref, bi_ref, lam_ref)
        a_s[...] = a
        u_s[...] = s * (i * xc)
        _linear_scan(a_s, u_s, hs_ref, False)
        og_ref[...] = (hs_ref[...] * _gelu(yb_ref[...])).astype(og_ref.dtype)

    xb, yb, cw, vec, wg, bg = _lru_specs(T)
    return pl.pallas_call(
        body, name=name, grid=(B,), in_specs=[xb, yb, cw, vec, wg, bg, wg, bg, vec], out_specs=(xb, xb),
        out_shape=(jax.ShapeDtypeStruct((T, B * W), BF16), jax.ShapeDtypeStruct((T, B * W), F32)),
        scratch_shapes=[pltpu.VMEM((T, W), F32), pltpu.VMEM((T, W), F32)],
        compiler_params=_params(("arbitrary",)),
    )(proj, proj, conv_w, conv_b, w_r, b_r, w_i, b_i, lam)


def _lru_bwd(proj, conv_w, conv_b, w_r, b_r, w_i, b_i, lam, hs, dout, *, name):
    T = proj.shape[0]
    B, W = LRU_BLOCKS, LRU_BLOCK

    def csum(x):
        return jnp.sum(x, axis=0, keepdims=True)

    def body(xb_ref, yb_ref, cw_ref, cb_ref, wr_ref, br_ref, wi_ref, bi_ref, lam_ref, hs_ref, do_ref,
             dxb_ref, dyb_ref, dcw_ref, dcb_ref, dwr_ref, dbr_ref, dwi_ref, dbi_ref, dlam_ref, a_s, b_s, g_s):
        xv, xc, r, i, sp, a, s = _lru_gates(xb_ref, cw_ref, cb_ref, wr_ref, br_ref, wi_ref, bi_ref, lam_ref)
        h = hs_ref[...]
        yb = yb_ref[...]
        dout = do_ref[...].astype(F32)
        dyb_ref[...] = (dout * h * _dgelu(yb)).astype(dyb_ref.dtype)
        a_s[...] = _shift_up(a, 1)
        b_s[...] = dout * _gelu(yb)
        _linear_scan(a_s, b_s, g_s, True)
        g = g_s[...]
        da = g * _shift_down(h, 1)
        ds = g * (i * xc)
        di = g * s * xc
        dxc = g * s * i
        dla = da * a - ds * (a * a) / s
        dr = dla * (-LRU_C * sp)
        dlam_ref[...] = csum(dla * r) * (LRU_C * _sigmoid(-lam_ref[...]))
        dpr = dr * r * (1.0 - r)
        dpi = di * i * (1.0 - i)
        dxc = dxc + _bdot(dpr, wr_ref[0], NT) + _bdot(dpi, wi_ref[0], NT)
        dwr_ref[0] = _bdot(xc, dpr, TN)
        dwi_ref[0] = _bdot(xc, dpi, TN)
        dbr_ref[0] = csum(dpr)
        dbi_ref[0] = csum(dpi)
        dcb_ref[...] = csum(dxc)
        dxb_ref[...] = _conv_bwd(xv, cw_ref, dxc, dcw_ref).astype(dxb_ref.dtype)

    xb, yb, cw, vec, wg, bg = _lru_specs(T)
    act = jax.ShapeDtypeStruct((T, B * W), BF16)
    return pl.pallas_call(
        body, name=name, grid=(B,), in_specs=[xb, yb, cw, vec, wg, bg, wg, bg, vec, xb, xb],
        out_specs=(xb, xb, cw, vec, wg, bg, wg, bg, vec),
        out_shape=(act, act, jax.ShapeDtypeStruct((CONV_WIDTH, B * W), F32), jax.ShapeDtypeStruct((1, B * W), F32),
                   jax.ShapeDtypeStruct((B, W, W), F32), jax.ShapeDtypeStruct((B, 1, W), F32),
                   jax.ShapeDtypeStruct((B, W, W), F32), jax.ShapeDtypeStruct((B, 1, W), F32),
                   jax.ShapeDtypeStruct((1, B * W), F32)),
        scratch_shapes=[pltpu.VMEM((T, W), F32)] * 3,
        compiler_params=_params(("arbitrary",)),
    )(proj, proj, conv_w, conv_b, w_r, b_r, w_i, b_i, lam, hs, dout)


MESH = pl.DeviceIdType.MESH
N_CHIPS = 4
AG_COPIES = 7


def _mesh_pos():
    return lax.axis_index("x"), lax.axis_index("y"), lax.axis_index("c")


def _hbm_specs(n):
    return [pl.BlockSpec(memory_space=pltpu.HBM)] * n


def _all_gather(shards, *, name):
    n = len(shards)

    def body(*refs):
        xs, outs = refs[:n], refs[n:2 * n]
        send_sems, recv_sems, local_sems = refs[2 * n:]
        x, y, c = _mesh_pos()
        me, sibling = (x, y, c), (x, y, 1 - c)
        chips = [(1 - x, y), (x, 1 - y), (1 - x, 1 - y)]

        def rows(t, px, py, pc):
            return outs[t].at[4 * px + 2 * py + pc]

        def copy(t, k, block, to, src=None):
            return pltpu.make_async_remote_copy(
                src_ref=rows(t, *block) if src is None else src, dst_ref=rows(t, *block),
                send_sem=send_sems.at[t * AG_COPIES + k], recv_sem=recv_sems.at[t * AG_COPIES + k],
                device_id=to, device_id_type=MESH)

        mine = [pltpu.make_async_copy(xs[t], rows(t, *me), local_sems.at[t]) for t in range(n)]
        for cp in mine:
            cp.start()
        first = []
        for t in range(n):
            first.append(copy(t, 0, me, sibling, src=xs[t]))
            first += [copy(t, 1 + j, me, (*chip, c), src=xs[t]) for j, chip in enumerate(chips)]
        for cp in first:
            cp.start()
        passed = []
        for j, chip in enumerate(chips):
            for t in range(n):
                copy(t, 1 + j, (*chip, c), me).wait_recv()
                cp = copy(t, 4 + j, (*chip, c), sibling)
                cp.start()
                passed.append(cp)
        for t in range(n):
            copy(t, 0, sibling, me).wait_recv()
        for j, chip in enumerate(chips):
            for t in range(n):
                copy(t, 4 + j, (*chip, 1 - c), me).wait_recv()
        for cp in first + passed:
            cp.wait_send()
        for cp in mine:
            cp.wait()

    return pl.pallas_call(
        body, name=name,
        out_shape=[jax.ShapeDtypeStruct((N_DEV,) + s.shape, s.dtype) for s in shards],
        in_specs=_hbm_specs(n), out_specs=_hbm_specs(n),
        scratch_shapes=[pltpu.SemaphoreType.DMA((n * AG_COPIES,)), pltpu.SemaphoreType.DMA((n * AG_COPIES,)),
                        pltpu.SemaphoreType.DMA((n,))],
    )(*shards)


SIDE_EFFECT = pltpu.SideEffectType.DATAFLOW_SIDE_EFFECTING


def _copies(plan, refs, send_sems, recv_sems):
    return [pltpu.make_async_remote_copy(src_ref=src, dst_ref=dst, send_sem=send_sems.at[k], recv_sem=recv_sems.at[k],
                                         device_id=to, device_id_type=MESH)
            for k, (src, dst, to) in enumerate(plan(refs))]


def _split_start(bufs, plan, n_copies, *, name, deps=()):
    n = len(bufs)

    def body(*refs):
        send_sems, recv_sems = refs[n + len(deps)], refs[n + len(deps) + 1]
        token = refs[-1]
        for cp in _copies(plan, refs[:n], send_sems, recv_sems):
            cp.start()
        token[...] = jnp.zeros_like(token)

    hbm, sem = pl.BlockSpec(memory_space=pltpu.HBM), pl.BlockSpec(memory_space=pltpu.SEMAPHORE)
    out = pl.pallas_call(
        body, name=name,
        out_shape=(pltpu.SemaphoreType.DMA((n_copies,)), pltpu.SemaphoreType.DMA((n_copies,)),
                   *[pltpu.HBM(b.shape, b.dtype) for b in bufs], jax.ShapeDtypeStruct(TOKEN_SHAPE, F32)),
        in_specs=[hbm] * n + [pl.BlockSpec(memory_space=pl.ANY)] * len(deps),
        out_specs=(sem, sem, *[hbm] * n, pl.BlockSpec(memory_space=pltpu.VMEM)),
        input_output_aliases={i: 2 + i for i in range(n)},
        compiler_params=pltpu.CompilerParams(has_side_effects=SIDE_EFFECT),
    )(*[pltpu.with_memory_space_constraint(b, pltpu.HBM) for b in bufs], *deps)
    return out[0], out[1], list(out[2:2 + n]), out[-1]


def _split_wait(send_sems, recv_sems, bufs, plan, after, *, name):
    n = len(bufs)

    def body(*refs):
        for cp in _copies(plan, refs[:n], refs[n], refs[n + 1]):
            cp.wait_send()
            cp.wait_recv()

    hbm, sem = pl.BlockSpec(memory_space=pltpu.HBM), pl.BlockSpec(memory_space=pltpu.SEMAPHORE)
    out = pl.pallas_call(
        body, name=name, out_shape=tuple(pltpu.HBM(b.shape, b.dtype) for b in bufs),
        in_specs=[hbm] * n + [sem, sem, pl.BlockSpec(memory_space=pl.ANY)], out_specs=tuple([hbm] * n),
        input_output_aliases={i: i for i in range(n)},
        compiler_params=pltpu.CompilerParams(has_side_effects=SIDE_EFFECT),
    )(*bufs, send_sems, recv_sems, after)
    return list(out)


def _block(ref, in_cols, d):
    if not in_cols:
        return ref.at[d]
    c = ref.shape[1] // N_DEV
    return ref.at[:, pl.ds(pl.multiple_of(d * c, LANE), c)]


def _plan_gather_a(in_cols):
    n = len(in_cols)

    def plan(refs):
        x, y, c = _mesh_pos()
        me = 4 * x + 2 * y + c
        peers = [(x, y, 1 - c), (1 - x, y, c), (x, 1 - y, c), (1 - x, 1 - y, c)]
        return [(refs[t], _block(refs[n + t], in_cols[t], me), to) for t in range(n) for to in peers]
    return plan


def _plan_gather_b(in_cols):
    n = len(in_cols)

    def plan(refs):
        x, y, c = _mesh_pos()
        ds = [4 * px + 2 * py + c for px, py in [(1 - x, y), (x, 1 - y), (1 - x, 1 - y)]]
        return [(_block(refs[t], in_cols[t], d), _block(refs[t], in_cols[t], d), (x, y, 1 - c))
                for t in range(n) for d in ds]
    return plan


def _plan_scatter_pair(in_cols):
    n = len(in_cols)

    def plan(refs):
        x, y, c = _mesh_pos()
        return [(_block(refs[t], in_cols[t], 2 * q + (1 - c)), refs[n + t].at[q], (x, y, 1 - c))
                for t in range(n) for q in range(N_CHIPS)]
    return plan


def _plan_scatter_chips(n):
    def plan(refs):
        x, y, c = _mesh_pos()
        chips = [(1 - x, y), (x, 1 - y), (1 - x, 1 - y)]
        return [(refs[t].at[2 * px + py], refs[n + t].at[j], (px, py, c))
                for t in range(n) for j, (px, py) in enumerate(chips)]
    return plan


def _insert_block(land, shard, device, in_cols, *, name):
    r, c = shard.shape
    tr = _tile(r, (512, 256))

    def body(d_ref, s_ref, land_ref, o_ref):
        if in_cols:
            o_ref[...] = s_ref[...]
        else:
            o_ref[0] = s_ref[...]

    if in_cols:
        o_spec = pl.BlockSpec((tr, c), lambda i, d: (i, d[0]))
    else:
        o_spec = pl.BlockSpec((1, tr, c), lambda i, d: (d[0], i, 0))
    return pl.pallas_call(
        body, name=name,
        grid_spec=pltpu.PrefetchScalarGridSpec(
            num_scalar_prefetch=1, grid=(r // tr,),
            in_specs=[pl.BlockSpec((tr, c), lambda i, d: (i, 0)), pl.BlockSpec(memory_space=pl.ANY)],
            out_specs=o_spec),
        out_shape=jax.ShapeDtypeStruct(land.shape, land.dtype), input_output_aliases={2: 0},
        compiler_params=_params(("parallel",)),
    )(device, shard, land)


PAIR_ROWS = (512, 256)


def _pair_add(g, a, core, in_cols, *, name):
    _, R, C = a.shape
    tr = _tile(R, PAIR_ROWS)

    def body(c_ref, g_ref, a_ref, o_ref):
        gv = g_ref[...] if in_cols else g_ref[0]
        o_ref[0] = (gv.astype(F32) + a_ref[0].astype(F32)).astype(o_ref.dtype)

    blk = pl.BlockSpec((1, tr, C), lambda q, i, c: (q, i, 0))
    if in_cols:
        g_spec = pl.BlockSpec((tr, C), lambda q, i, c: (i, 2 * q + c[0]))
    else:
        g_spec = pl.BlockSpec((1, tr, C), lambda q, i, c: (2 * q + c[0], i, 0))
    return pl.pallas_call(
        body, name=name,
        grid_spec=pltpu.PrefetchScalarGridSpec(
            num_scalar_prefetch=1, grid=(N_CHIPS, R // tr), in_specs=[g_spec, blk], out_specs=blk),
        out_shape=jax.ShapeDtypeStruct((N_CHIPS, R, C), BF16),
        compiler_params=_params(("parallel", "parallel")),
    )(core, g, a)


ADAM_ROWS = 256


def _adamw_sharded(w, m, v, s4, b3, chip, layer, prev, *, name):
    L, R, C = w.shape
    tr = _tile(R, (ADAM_ROWS,))
    if prev is None and L > 1:
        prev = tuple(lax.empty(w.shape, F32) for _ in range(4))
    n_prev = 0 if prev is None else 4

    def body(q_ref, w_ref, m_ref, v_ref, s_ref, b_ref, *rest):
        g_out, d_out, m_out, v_out = rest[n_prev:]
        g = s_ref[0].astype(F32)
        for j in range(N_CHIPS - 1):
            g = g + b_ref[j].astype(F32)
        mn = ADAM_B1 * m_ref[0] + (1.0 - ADAM_B1) * g
        vn = ADAM_B2 * v_ref[0] + (1.0 - ADAM_B2) * (g * g)
        g_out[0] = g
        d_out[0] = -ADAM_LR * ((mn / ADAM_C1) / (jnp.sqrt(vn / ADAM_C2) + ADAM_EPS) + ADAM_WD * w_ref[0])
        m_out[0] = mn
        v_out[0] = vn

    blk = pl.BlockSpec((1, tr, C), lambda i, q: (layer, i, 0))
    sh = jax.ShapeDtypeStruct((L, R, C), F32)
    return pl.pallas_call(
        body, name=name,
        grid_spec=pltpu.PrefetchScalarGridSpec(
            num_scalar_prefetch=1, grid=(R // tr,),
            in_specs=[blk, blk, blk, pl.BlockSpec((1, tr, C), lambda i, q: (q[0], i, 0)),
                      pl.BlockSpec((N_CHIPS - 1, tr, C), lambda i, q: (0, i, 0))]
            + [pl.BlockSpec(memory_space=pl.ANY)] * n_prev,
            out_specs=(blk,) * 4),
        out_shape=(sh,) * 4, input_output_aliases={6 + k: k for k in range(n_prev)},
        compiler_params=_params(("parallel",)),
    )(chip, w, m, v, s4, b3, *(prev or ()))


FWD_NAMES = ['x', 'norm1', 'norm2', 'final_norm', 'ret_w_in', 'ret_gn_gain', 'ret_w_out', 'gdn_w_in', 'gdn_conv_w',
             'gdn_a_log', 'gdn_dt_bias', 'gdn_norm_gain', 'gdn_w_out', 'gla_w_in', 'gla_w_gate_up', 'gla_gate_bias',
             'gla_norm_gain', 'gla_w_out', 'lru_w_in', 'lru_conv_w', 'lru_conv_b', 'lru_w_rgate', 'lru_b_rgate',
             'lru_w_igate', 'lru_b_igate', 'lru_lambda', 'lru_w_out', 'mlp_w_up', 'mlp_w_down']
WEIGHT_NAMES = FWD_NAMES[1:]
ARG_NAMES = FWD_NAMES + ['loss_target'] + ['m_' + n for n in WEIGHT_NAMES] + ['v_' + n for n in WEIGHT_NAMES]

MIXER_IN = ('ret_w_in', 'gdn_w_in', 'gla_w_in', 'lru_w_in')
MIXER_OUT = ('ret_w_out', 'gdn_w_out', 'gla_w_out', 'lru_w_out')
BIG_NAMES = MIXER_IN + MIXER_OUT + ('mlp_w_up', 'mlp_w_down')
SMALL = {'norm1': False, 'norm2': False, 'final_norm': False, 'ret_gn_gain': True, 'gdn_conv_w': True,
         'gdn_a_log': False, 'gdn_dt_bias': False, 'gdn_norm_gain': False, 'gla_w_gate_up': True,
         'gla_gate_bias': True, 'gla_norm_gain': True, 'lru_conv_w': True, 'lru_conv_b': True,
         'lru_w_rgate': False, 'lru_b_rgate': False, 'lru_w_igate': False, 'lru_b_igate': False, 'lru_lambda': True}
SMALL_NAMES = tuple(n for n in WEIGHT_NAMES if n in SMALL)
MEDIUM = ('lru_w_rgate', 'lru_w_igate')
EARLY_SMALL = tuple(n for n in SMALL_NAMES if n.startswith(('gdn_', 'gla_', 'lru_')) and n not in MEDIUM)
LATE_SMALL = tuple(n for n in SMALL_NAMES if n not in EARLY_SMALL + MEDIUM)
GDN_TAIL = 2 * GDN_HEADS
GDN_MAIN = 4 * D_MODEL
GLA_MAIN = 3 * D_MODEL


PACK_ROWS = 256


PACK_TILE = SUBLANES * LANE


def _pack(arrs):
    rows = []
    for a in arrs:
        f = a.reshape(-1).astype(F32)
        rows.append(jnp.pad(f, (0, (-f.shape[0]) % PACK_TILE)).reshape(-1, LANE))
    fill = (-sum(r.shape[0] for r in rows)) % PACK_ROWS
    if fill:
        rows.append(jnp.zeros((fill, LANE), F32))
    return jnp.concatenate(rows, axis=0)


def _unpack(buf, shapes, lead=()):
    out, r0 = [], 0
    for s in shapes:
        n = int(np.prod(s))
        nr = -(-n // PACK_TILE) * SUBLANES
        blk = buf[..., r0:r0 + nr, :].reshape(lead + (nr * LANE,))[..., :n]
        out.append(blk.reshape(lead + tuple(s)))
        r0 += nr
    return out


def _full_cols(g):
    return jnp.transpose(g, (1, 0, 2)).reshape(g.shape[1], N_DEV * g.shape[2])


def _full_rows(g):
    return g.reshape(N_DEV * g.shape[1], g.shape[2])


def _blocks_cols(dw):
    r, c = dw.shape[0], dw.shape[1] // N_DEV
    return jnp.transpose(dw.reshape(r, N_DEV, c), (1, 0, 2))


def _blocks_rows(dw):
    return dw.reshape(N_DEV, dw.shape[0] // N_DEV, dw.shape[1])


def _pad_cols(a, n=LANE):
    return jnp.pad(a, ((0, 0), (0, n - a.shape[1])))


def _mixer_fwd(layer, hn, w_in, sm, tables, deps=()):
    tag = f"l{layer}"
    if layer == 0:
        proj = _matmul(hn, w_in, name=tag + "_in", deps=deps)
        gain = sm['ret_gn_gain'][0][:, None, :]
        og, st = _ret_fwd(proj, gain, tables, name=tag + "_ret_fwd")
        return og, dict(proj=proj, st=st, gain=gain)
    if layer == 1:
        w_main, w_tail = w_in, _pad_cols(w_in[:, GDN_MAIN:])
        proj = _matmul(hn, w_main, b_cols=GDN_MAIN, name=tag + "_in", deps=deps)
        gates = _matmul(hn, w_tail, name=tag + "_in_tail")
        conv_w = sm['gdn_conv_w'][0]
        act = _conv_silu_fwd(proj, conv_w, name=tag + "_conv")
        a_log = sm['gdn_a_log'].reshape(GDN_HEADS, 1, 1)
        dt_bias = sm['gdn_dt_bias'].reshape(GDN_HEADS, 1, 1)
        gain = sm['gdn_norm_gain']
        og, st, tm, sol = _gdn_fwd(act, proj, gates, a_log, dt_bias, gain, name=tag + "_gdn_fwd")
        return og, dict(proj=proj, gates=gates, act=act, st=st, solved=(tm, sol), conv_w=conv_w, a_log=a_log, dt_bias=dt_bias,
                        gain=gain, w_main=w_main, w_tail=w_tail)
    if layer == 2:
        w_main, w_tail = w_in, _pad_cols(w_in[:, GLA_MAIN:])
        proj = _matmul(hn, w_main, b_cols=GLA_MAIN, name=tag + "_in", deps=deps)
        glow = _matmul(hn, w_tail, name=tag + "_in_tail")
        wgu = jnp.pad(sm['gla_w_gate_up'][0], ((0, LANE - GLA_GATE_RANK), (0, 0)))
        glogit = _matmul(glow, wgu, name=tag + "_gate_up")
        bias = sm['gla_gate_bias']
        gain = sm['gla_norm_gain'][0][:, None, :]
        og, st = _gla_fwd(proj, glogit, bias, gain, name=tag + "_gla_fwd")
        return og, dict(proj=proj, glow=glow, glogit=glogit, wgu=wgu, bias=bias, gain=gain, st=st,
                        w_main=w_main, w_tail=w_tail)
    proj = _matmul(hn, w_in, name=tag + "_in", deps=deps)
    args = (proj, sm['lru_conv_w'][0], sm['lru_conv_b'], sm['lru_w_rgate'][0], sm['lru_b_rgate'][0][:, None, :],
            sm['lru_w_igate'][0], sm['lru_b_igate'][0][:, None, :], sm['lru_lambda'])
    og, hs = _lru_fwd(*args, name=tag + "_lru_fwd")
    return og, dict(args=args, hs=hs)


def _mixer_bwd(layer, hn, w_in, dog, sv, tables, on_dw):
    tag = f"l{layer}"
    if layer == 0:
        dq, dk, dv, dg, dgain = _ret_bwd(sv['proj'], sv['gain'], tables, sv['st'], dog, name=tag + "_ret_bwd")
        dproj = jnp.concatenate([dq, dk, dv, dg], axis=1)
        deps = on_dw(_matmul(hn, dproj, ta=True, out_dtype=BF16, name=tag + "_in_dw"))
        dhn = _matmul(dproj, w_in, tb=True, name=tag + "_in_dx", deps=deps)
        return dhn, {'ret_gn_gain': dgain[:, 0][None]}
    if layer == 1:
        dq, dk, dv, dz, dgates, dscal, dgain = _gdn_bwd(
            sv['act'], sv['proj'], sv['gates'], sv['a_log'], sv['dt_bias'], sv['gain'], sv['st'], sv['solved'], dog,
            name=tag + "_gdn_bwd")
        dact = jnp.concatenate([dq, dk, dv], axis=1)
        dqkv, dconv = _conv_silu_bwd(sv['proj'], sv['conv_w'], dact, name=tag + "_conv_bwd")
        dmain = jnp.concatenate([dqkv, dz], axis=1)
        T = dmain.shape[0]
        dtail = _pad_cols(jnp.transpose(dgates[:, :, :2], (1, 2, 0)).reshape(T, GDN_TAIL))
        dw_main = _matmul(hn, dmain, ta=True, out_dtype=BF16, name=tag + "_in_dw")
        dw_tail = _matmul(hn, dtail, ta=True, out_dtype=BF16, name=tag + "_in_tail_dw")
        deps = on_dw(jnp.concatenate([dw_main, dw_tail[:, :GDN_TAIL]], axis=1))
        dhn = _matmul(dmain, sv['w_main'], tb=True, b_cols=GDN_MAIN, name=tag + "_in_dx", deps=deps)
        dhn = _matmul(dtail, sv['w_tail'], tb=True, epi="add", extra=dhn, name=tag + "_in_tail_dx")
        small = {'gdn_conv_w': dconv[None], 'gdn_a_log': dscal[:, 0, 0][None], 'gdn_dt_bias': dscal[:, 0, 1][None],
                 'gdn_norm_gain': jnp.sum(dgain[:, 0], axis=0)[None]}
        return dhn, small
    if layer == 2:
        dq, dk, dv, dr, dgl, dgain = _gla_bwd(sv['proj'], sv['glogit'], sv['bias'], sv['gain'], sv['st'], dog,
                                              name=tag + "_gla_bwd")
        dmain = jnp.concatenate([dq, dk, dv, dr], axis=1)
        dglow = _matmul(dgl, sv['wgu'], tb=True, name=tag + "_gate_up_dx")
        dwgu = _matmul(sv['glow'], dgl, ta=True, name=tag + "_gate_up_dw")
        dbias = _colsum(dgl, name=tag + "_gate_bias")
        dw_main = _matmul(hn, dmain, ta=True, out_dtype=BF16, name=tag + "_in_dw")
        dw_tail = _matmul(hn, dglow, ta=True, out_dtype=BF16, name=tag + "_in_tail_dw")
        deps = on_dw(jnp.concatenate([dw_main, dw_tail[:, :GLA_GATE_RANK]], axis=1))
        dhn = _matmul(dmain, sv['w_main'], tb=True, b_cols=GLA_MAIN, name=tag + "_in_dx", deps=deps)
        dhn = _matmul(dglow, sv['w_tail'], tb=True, epi="add", extra=dhn, name=tag + "_in_tail_dx")
        small = {'gla_w_gate_up': dwgu[:GLA_GATE_RANK][None], 'gla_gate_bias': dbias,
                 'gla_norm_gain': dgain[:, 0][None]}
        return dhn, small
    dxb, dyb, dcw, dcb, dwr, dbr, dwi, dbi, dlam = _lru_bwd(*sv['args'], sv['hs'], dog, name=tag + "_lru_bwd")
    dproj = jnp.concatenate([dxb, dyb], axis=1)
    deps = on_dw(_matmul(hn, dproj, ta=True, out_dtype=BF16, name=tag + "_in_dw"))
    dhn = _matmul(dproj, w_in, tb=True, name=tag + "_in_dx", deps=deps)
    small = {'lru_conv_w': dcw[None], 'lru_conv_b': dcb, 'lru_w_rgate': dwr[None], 'lru_b_rgate': dbr[:, 0][None],
             'lru_w_igate': dwi[None], 'lru_b_igate': dbi[:, 0][None], 'lru_lambda': dlam}
    return dhn, small


def _step(*args):
    assert len(args) == len(ARG_NAMES)
    p = dict(zip(ARG_NAMES, args))
    xi, yi, ci = _mesh_pos()
    dev = 4 * xi + 2 * yi + ci
    device = dev.astype(jnp.int32).reshape(1)
    core = ci.astype(jnp.int32).reshape(1)
    chip = (2 * xi + yi).astype(jnp.int32).reshape(1)
    x = p['x'][0]
    target = p['loss_target'][0]
    T = x.shape[0]
    tables = _ret_tables(T)

    sharded_small = [n for n in SMALL_NAMES if SMALL[n]]
    gathered, = _all_gather([_pack([p[n] for n in sharded_small])], name="gather_small")
    def in_cols_of(w):
        return w.shape[1] % LANE == 0

    gathers = {}
    token = gathered
    for layer in range(DEPTH):
        w_in_shard = p[MIXER_IN[layer]][0]
        groups = {'a': [(w_in_shard, in_cols_of(w_in_shard), _full_cols)],
                  'b': [(p[MIXER_OUT[layer]][0], False, _full_rows)],
                  'c': [(p['mlp_w_up'][layer], True, None), (p['mlp_w_down'][layer], False, _full_rows)]}
        for key, members in groups.items():
            shards = [w.astype(BF16) for w, _, _ in members]
            in_cols = [ic for _, ic, _ in members]
            n = len(shards)
            lands = [_insert_block(lax.empty((s.shape[0], N_DEV * s.shape[1]) if ic else (N_DEV,) + s.shape, BF16),
                                   s, device, ic, name=f"own_l{layer}{key}{i}")
                     for i, (s, ic) in enumerate(zip(shards, in_cols))]
            send, recv, bufs, token = _split_start(shards + lands, _plan_gather_a(in_cols), 4 * n,
                                                   name=f"gather_a_start_l{layer}{key}", deps=(token,))
            gathers[layer, key] = dict(n=n, h=(send, recv, bufs), in_cols=in_cols, full_of=[f for _, _, f in members])

    def gather_forward(layer, key, after):
        g = gathers[layer, key]
        n = g['n']
        send, recv, bufs = g['h']
        bufs = _split_wait(send, recv, bufs, _plan_gather_a(g['in_cols']), after, name=f"gather_a_wait_l{layer}{key}")
        send, recv, lands, tok = _split_start(bufs[n:], _plan_gather_b(g['in_cols']), 3 * n,
                                              name=f"gather_b_start_l{layer}{key}")
        g['h'] = (send, recv, lands)
        return tok

    def gather_finish(layer, key, after):
        g = gathers[layer, key]
        send, recv, lands = g['h']
        lands = _split_wait(send, recv, lands, _plan_gather_b(g['in_cols']), after, name=f"gather_b_wait_l{layer}{key}")
        return [l if ic else full_of(l) for l, ic, full_of in zip(lands, g['in_cols'], g['full_of'])]

    w_in_next, = gather_finish(0, 'a', gather_forward(0, 'a', token))
    parts = _unpack(gathered, [p[n].shape for n in sharded_small], lead=(N_DEV,))
    sm = {n: p[n] for n in SMALL_NAMES if not SMALL[n]}
    for n, blk in zip(sharded_small, parts):
        full = jnp.moveaxis(blk, 0, -2)
        sm[n] = full.reshape(full.shape[:-2] + (N_DEV * full.shape[-1],))

    saved = []
    big = {}
    for layer in range(DEPTH):
        w_in = w_in_next
        tag = f"l{layer}"
        hn = _rmsnorm_fwd(x, sm['norm1'][layer][None], name=tag + "_norm1")
        og, sv = _mixer_fwd(layer, hn, w_in, sm, tables)
        tok_b = gather_forward(layer, 'b', og)
        tok_c = gather_forward(layer, 'c', tok_b)
        w_out, = gather_finish(layer, 'b', tok_c)
        x_mid = _matmul(og, w_out, epi="add", extra=x, name=tag + "_out")
        hn2 = _rmsnorm_fwd(x_mid, sm['norm2'][layer][None], name=tag + "_norm2")
        w_up, w_down = gather_finish(layer, 'c', hn2)
        big[layer] = (w_in, w_out, w_up, w_down)
        u, a = _matmul(hn2, w_up, epi="relu2", name=tag + "_up")
        deps = (gather_forward(layer + 1, 'a', u),) if layer + 1 < DEPTH else ()
        x_new = _matmul(a, w_down, epi="add", extra=x_mid, name=tag + "_down", deps=deps)
        if layer + 1 < DEPTH:
            w_in_next, = gather_finish(layer + 1, 'a', x_new)
        saved.append(dict(x=x, hn=hn, og=og, sv=sv, x_mid=x_mid, hn2=hn2, u=u, a=a))
        x = x_new
    dx, dxb, dfinal, loss_part = _final_loss_bwd(x, sm['final_norm'][None], target, name="final_loss_bwd")
    loss = lax.psum(loss_part[0, 0], ("x", "y", "c"))

    outs = {}
    small_grads = {'final_norm': dfinal[0]}
    dnorm1, dnorm2 = [None] * DEPTH, [None] * DEPTH

    def scatter_start(items, deps, tag):
        grads, in_cols = [], []
        for _, _, dw, on in items:
            ic = on == 'cols' and (dw.shape[1] // N_DEV) % LANE == 0
            in_cols.append(ic)
            grads.append(dw if ic else (_blocks_cols(dw) if on == 'cols' else _blocks_rows(dw)))
        n = len(grads)
        lands = [lax.empty((N_CHIPS, dw.shape[0], dw.shape[1] // N_DEV) if ic else (N_CHIPS,) + dw.shape[1:], BF16)
                 for dw, ic in zip(grads, in_cols)]
        send, recv, bufs, tok = _split_start(grads + lands, _plan_scatter_pair(in_cols), N_CHIPS * n,
                                             name=f"scatter_pair_start_{tag}", deps=deps)
        return dict(items=items, n=n, tag=tag, in_cols=in_cols, h=(send, recv, bufs)), tok

    def scatter_forward(g, after):
        n, tag = g['n'], g['tag']
        send, recv, bufs = g['h']
        bufs = _split_wait(send, recv, bufs, _plan_scatter_pair(g['in_cols']), after, name=f"scatter_pair_wait_{tag}")
        sums = [_pair_add(b, a_, core, ic, name=f"pair_add_{tag}_{i}")
                for i, (b, a_, ic) in enumerate(zip(bufs[:n], bufs[n:], g['in_cols']))]
        lands = [lax.empty((N_CHIPS - 1,) + s_.shape[1:], BF16) for s_ in sums]
        send, recv, bufs, tok = _split_start(sums + lands, _plan_scatter_chips(n), (N_CHIPS - 1) * n,
                                             name=f"scatter_chips_start_{tag}")
        g['h'] = (send, recv, bufs)
        return tok

    def scatter_finish(g, after):
        n, tag = g['n'], g['tag']
        send, recv, bufs = g['h']
        bufs = _split_wait(send, recv, bufs, _plan_scatter_chips(n), after, name=f"scatter_chips_wait_{tag}")
        for i, (wname, idx, _, _) in enumerate(g['items']):
            outs[wname] = _adamw_sharded(p[wname], p['m_' + wname], p['v_' + wname], bufs[i], bufs[n + i], chip,
                                         idx, outs.get(wname), name=f"adamw_{tag}_{i}")

    older = []
    next_deps = ()
    for layer in reversed(range(DEPTH)):
        w_in, w_out, w_up, w_down = big[layer]
        s = saved[layer]
        tag = f"l{layer}"
        du = _matmul(dxb, w_down, tb=True, epi="drelu2", extra=s['u'], out_dtype=BF16, name=tag + "_down_dx",
                     deps=next_deps)
        next_deps = ()
        dw_down = _matmul(s['a'], dxb, ta=True, out_dtype=BF16, name=tag + "_down_dw")
        dhn2 = _matmul(du, w_up, tb=True, name=tag + "_up_dx")
        dw_up = _matmul(s['hn2'], du, ta=True, out_dtype=BF16, name=tag + "_up_dw")
        mlp_group, tok = scatter_start([('mlp_w_up', layer, dw_up, 'cols'),
                                        ('mlp_w_down', layer, dw_down, 'rows')], (), f"mlp_l{layer}")
        dx, dxb, dn2 = _rmsnorm_bwd(s['x_mid'], sm['norm2'][layer][None], dhn2, dx, name=tag + "_norm2_bwd",
                                    deps=(tok,))
        dw_out = _matmul(s['og'], dxb, ta=True, out_dtype=BF16, name=tag + "_out_dw")
        tok = scatter_forward(mlp_group, dw_out)
        dog = _matmul(dxb, w_out, tb=True, name=tag + "_out_dx", deps=(tok,))
        started = []

        def on_dw(dw_in):
            group, tok_ = scatter_start([(MIXER_IN[layer], 0, dw_in, 'cols'),
                                         (MIXER_OUT[layer], 0, dw_out, 'rows')], (), f"mix_l{layer}")
            started.append(group)
            return (tok_,)

        dhn, sg = _mixer_bwd(layer, s['hn'], w_in, dog, s['sv'], tables, on_dw)
        mixer_group, = started
        small_grads.update(sg)
        if layer > 0:
            tok = scatter_forward(mixer_group, dhn)
            dx, dxb, dn1 = _rmsnorm_bwd(s['x'], sm['norm1'][layer][None], dhn, dx, name=tag + "_norm1_bwd", deps=(tok,))
        else:
            tok = gather_forward('small_grads', 'early', dhn)
            dx, dxb, dn1 = _rmsnorm_bwd(s['x'], sm['norm1'][layer][None], dhn, dx, name=tag + "_norm1_bwd", deps=(tok,))
        dnorm1[layer], dnorm2[layer] = dn1[0], dn2[0]
        for g in older:
            scatter_finish(g, dx)
        older = [mlp_group, mixer_group]
        if layer == 1:
            parts = [_pack([small_grads[n] for n in EARLY_SMALL])] + [small_grads[n].reshape(-1, LANE) for n in MEDIUM]
            lands = [_insert_block(lax.empty((N_DEV,) + s_.shape, F32), s_, device, False, name=f"own_small_grads{i}")
                     for i, s_ in enumerate(parts)]
            n_parts = len(parts)
            send, recv, bufs, tok = _split_start(parts + lands, _plan_gather_a([False] * n_parts), 4 * n_parts,
                                                 name="gather_a_start_small_grads", deps=(dx,))
            gathers['small_grads', 'early'] = dict(n=n_parts, h=(send, recv, bufs), in_cols=[False] * n_parts,
                                                   full_of=[lambda l: l] * n_parts)
            next_deps = (tok,)
    small_grads['norm1'] = jnp.stack(dnorm1)
    small_grads['norm2'] = jnp.stack(dnorm2)

    late_parts, = _all_gather([_pack([small_grads[n] for n in LATE_SMALL])], name="gather_small_grads")
    last_token = scatter_forward(mixer_group, late_parts)
    early_parts, *medium_parts = gather_finish('small_grads', 'early', last_token)
    early_sum = _sum_parts(early_parts, name="sum_small_grads_early", deps=(last_token,))
    late_sum = _sum_parts(late_parts, name="sum_small_grads_late")
    for g in older:
        if g is not mixer_group:
            scatter_finish(g, late_sum)
    done_first = []
    for n, part in zip(MEDIUM, medium_parts):
        g = _sum_parts(part, name=f"sum_{n}")
        res = _adamw(p[n].reshape(-1, LANE), p['m_' + n].reshape(-1, LANE), p['v_' + n].reshape(-1, LANE), [g],
                     name=f"adamw_{n}")
        outs[n] = tuple(r.reshape(p[n].shape) for r in res)
        done_first.append(res[0][:SUBLANES])
    by_name = dict(zip(EARLY_SMALL, _unpack(early_sum, [small_grads[n].shape for n in EARLY_SMALL])))
    by_name.update(zip(LATE_SMALL, _unpack(late_sum, [small_grads[n].shape for n in LATE_SMALL])))
    packed_names = EARLY_SMALL + LATE_SMALL
    local_g = []
    for n in packed_names:
        g = by_name[n]
        if SMALL[n]:
            width = p[n].shape[-1]
            g = lax.dynamic_slice_in_dim(g, dev * width, width, axis=g.ndim - 1)
        local_g.append(g.reshape(p[n].shape))
    res = _adamw(_pack([p[n] for n in packed_names]), _pack([p['m_' + n] for n in packed_names]),
                 _pack([p['v_' + n] for n in packed_names]), [_pack(local_g)], name="adamw_small")
    local_shapes = [p[n].shape for n in packed_names]
    unpacked = [_unpack(r, local_shapes) for r in res]
    for i, n in enumerate(packed_names):
        outs[n] = tuple(unpacked[k][i] for k in range(4))
    done_first.append(res[0][:SUBLANES])
    done_first += [outs[n][0][0, :SUBLANES, :LANE] for n in BIG_NAMES if n not in (MIXER_IN[0], MIXER_OUT[0])]
    scatter_finish(mixer_group, functools.reduce(jnp.add, done_first))

    result = [loss, dx[None]]
    for k in range(4):
        result += [outs[n][k] for n in WEIGHT_NAMES]
    return tuple(result)


def kernel(x, norm1, norm2, final_norm, ret_w_in, ret_gn_gain, ret_w_out, gdn_w_in, gdn_conv_w, gdn_a_log, gdn_dt_bias, gdn_norm_gain, gdn_w_out, gla_w_in, gla_w_gate_up, gla_gate_bias, gla_norm_gain, gla_w_out, lru_w_in, lru_conv_w, lru_conv_b, lru_w_rgate, lru_b_rgate, lru_w_igate, lru_b_igate, lru_lambda, lru_w_out, mlp_w_up, mlp_w_down, loss_target, m_norm1, m_norm2, m_final_norm, m_ret_w_in, m_ret_gn_gain, m_ret_w_out, m_gdn_w_in, m_gdn_conv_w, m_gdn_a_log, m_gdn_dt_bias, m_gdn_norm_gain, m_gdn_w_out, m_gla_w_in, m_gla_w_gate_up, m_gla_gate_bias, m_gla_norm_gain, m_gla_w_out, m_lru_w_in, m_lru_conv_w, m_lru_conv_b, m_lru_w_rgate, m_lru_b_rgate, m_lru_w_igate, m_lru_b_igate, m_lru_lambda, m_lru_w_out, m_mlp_w_up, m_mlp_w_down, v_norm1, v_norm2, v_final_norm, v_ret_w_in, v_ret_gn_gain, v_ret_w_out, v_gdn_w_in, v_gdn_conv_w, v_gdn_a_log, v_gdn_dt_bias, v_gdn_norm_gain, v_gdn_w_out, v_gla_w_in, v_gla_w_gate_up, v_gla_gate_bias, v_gla_norm_gain, v_gla_w_out, v_lru_w_in, v_lru_conv_w, v_lru_conv_b, v_lru_w_rgate, v_lru_b_rgate, v_lru_w_igate, v_lru_b_igate, v_lru_lambda, v_lru_w_out, v_mlp_w_up, v_mlp_w_down):
    return _step(x, norm1, norm2, final_norm, ret_w_in, ret_gn_gain, ret_w_out, gdn_w_in, gdn_conv_w, gdn_a_log, gdn_dt_bias, gdn_norm_gain, gdn_w_out, gla_w_in, gla_w_gate_up, gla_gate_bias, gla_norm_gain, gla_w_out, lru_w_in, lru_conv_w, lru_conv_b, lru_w_rgate, lru_b_rgate, lru_w_igate, lru_b_igate, lru_lambda, lru_w_out, mlp_w_up, mlp_w_down, loss_target, m_norm1, m_norm2, m_final_norm, m_ret_w_in, m_ret_gn_gain, m_ret_w_out, m_gdn_w_in, m_gdn_conv_w, m_gdn_a_log, m_gdn_dt_bias, m_gdn_norm_gain, m_gdn_w_out, m_gla_w_in, m_gla_w_gate_up, m_gla_gate_bias, m_gla_norm_gain, m_gla_w_out, m_lru_w_in, m_lru_conv_w, m_lru_conv_b, m_lru_w_rgate, m_lru_b_rgate, m_lru_w_igate, m_lru_b_igate, m_lru_lambda, m_lru_w_out, m_mlp_w_up, m_mlp_w_down, v_norm1, v_norm2, v_final_norm, v_ret_w_in, v_ret_gn_gain, v_ret_w_out, v_gdn_w_in, v_gdn_conv_w, v_gdn_a_log, v_gdn_dt_bias, v_gdn_norm_gain, v_gdn_w_out, v_gla_w_in, v_gla_w_gate_up, v_gla_gate_bias, v_gla_norm_gain, v_gla_w_out, v_lru_w_in, v_lru_conv_w, v_lru_conv_b, v_lru_w_rgate, v_lru_b_rgate, v_lru_w_igate, v_lru_b_igate, v_lru_lambda, v_lru_w_out, v_mlp_w_up, v_mlp_w_down)
```

```python
import functools
import math

import numpy as np
import jax
import jax.numpy as jnp
from jax import lax
from jax.experimental import pallas as pl
from jax.experimental.pallas import tpu as pltpu

F32 = jnp.float32
BF16 = jnp.bfloat16

D_MODEL = 2048
DEPTH = 4
CHUNK = 64
D_FF = 4 * D_MODEL
NORM_EPS = 1e-6
N_DEV = 8

RET_HEADS, RET_DK, RET_DV = 8, 256, 512
RET_HB = 4
GDN_HEADS, GDN_DK, GDN_DV = 16, 128, 128
GDN_HB = 8
GDN_QKV = GDN_HEADS * (2 * GDN_DK + GDN_DV)
CONV_WIDTH = 4
GLA_HEADS, GLA_DK, GLA_DV = 4, 256, 512
GLA_HB = 4
GLA_GATE_RANK = 16
GLA_TAU = 16.0
LRU_WIDTH, LRU_BLOCKS, LRU_BLOCK = 2048, 16, 128
LRU_C = 8.0
ROPE_BASE = 10000.0

ADAM_LR, ADAM_B1, ADAM_B2, ADAM_EPS, ADAM_WD, ADAM_STEP = 0.001, 0.9, 0.999, 1e-08, 0.01, 10

LANE = 128
VMEM_LIMIT = 48 * 1024 * 1024

NN = (((1,), (0,)), ((), ()))
NT = (((1,), (1,)), ((), ()))
TN = (((0,), (0,)), ((), ()))


def _params(sem=None):
    return pltpu.CompilerParams(dimension_semantics=sem, vmem_limit_bytes=VMEM_LIMIT)


def _bdot(a, b, dn=NN):
    return lax.dot_general(a.astype(BF16), b.astype(BF16), dn, preferred_element_type=F32)


def _split(x):
    hi = x.astype(BF16)
    lo = (x - hi.astype(F32)).astype(BF16)
    return hi, lo


def _fdot(a, b, dn=NN):
    a1, a2 = _split(a)
    b1, b2 = _split(b)
    d = functools.partial(lax.dot_general, dimension_numbers=dn, preferred_element_type=F32)
    return d(a1, b1) + (d(a1, b2) + d(a2, b1))


def _sigmoid(x):
    return 1.0 / (1.0 + jnp.exp(-x))


def _softplus(x):
    return jnp.maximum(x, 0.0) + jnp.log(1.0 + jnp.exp(-jnp.abs(x)))


def _silu(x):
    return x * _sigmoid(x)


def _dsilu(x):
    s = _sigmoid(x)
    return s * (1.0 + x * (1.0 - s))


GELU_C = math.sqrt(2.0 / math.pi)


def _gelu(x):
    return 0.5 * x * (1.0 + jnp.tanh(GELU_C * (x + 0.044715 * x * x * x)))


def _dgelu(x):
    t = jnp.tanh(GELU_C * (x + 0.044715 * x * x * x))
    return 0.5 * (1.0 + t) + 0.5 * x * (1.0 - t * t) * GELU_C * (1.0 + 3.0 * 0.044715 * x * x)


def _expm1(x):
    poly = x * (1.0 + x * 0.5 * (1.0 + x * (1.0 / 3.0) * (1.0 + x * 0.25 * (1.0 + x * 0.2))))
    return jnp.where(jnp.abs(x) < 0.05, poly, jnp.exp(x) - 1.0)


def _iota2(shape, axis):
    return lax.broadcasted_iota(jnp.int32, shape, axis)


def _col_to_row(col):
    n = col.shape[0]
    eye = _iota2((n, n), 0) == _iota2((n, n), 1)
    return jnp.sum(jnp.where(eye, col, 0.0), axis=0, keepdims=True)


def _row_to_col(row):
    n = row.shape[1]
    eye = _iota2((n, n), 0) == _iota2((n, n), 1)
    return jnp.sum(jnp.where(eye, row, 0.0), axis=1, keepdims=True)


def _pick_row(x, r):
    rows = _iota2(x.shape, 0)
    return jnp.sum(jnp.where(rows == r, x, 0.0), axis=0, keepdims=True)


def _shift_down(x, s):
    if s == 0:
        return x
    y = pltpu.roll(x, s, 0)
    return jnp.where(_iota2(x.shape, 0) < s, 0.0, y)


def _shift_up(x, s):
    if s == 0:
        return x
    n = x.shape[0]
    y = pltpu.roll(x, n - s, 0)
    return jnp.where(_iota2(x.shape, 0) >= n - s, 0.0, y)


def _tile(dim, prefs):
    for p in prefs:
        if dim % p == 0:
            return p
    return dim


TOKEN_SHAPE = (8, LANE)


def _dep_specs(deps):
    return [pl.BlockSpec(TOKEN_SHAPE, lambda *_: (0, 0)) for _ in deps]


def _matmul(a, b, *, ta=False, tb=False, epi="none", extra=None, out_dtype=F32, name, deps=(), b_cols=None):
    if ta:
        K, M = a.shape
    else:
        M, K = a.shape
    if tb:
        N, K2 = b.shape[0], b_cols or b.shape[1]
    else:
        K2, N = b.shape[0], b_cols or b.shape[1]
    assert K == K2, (a.shape, b.shape, ta, tb)
    if K <= 2048:
        tk = K
        if N <= 2048:
            tm, tn = _tile(M, (512, 256, 128)), N
        else:
            tm, tn = _tile(M, (1024, 512, 256, 128)), _tile(N, (512, 256, 128))
    else:
        tm, tn, tk = (_tile(d, (1024, 512, 256, 128)) for d in (M, N, K))
    nk = K // tk
    dn = (((0 if ta else 1,), (1 if tb else 0,)), ((), ()))
    n_extra = 0 if extra is None else 1
    n_out = 2 if epi == "relu2" else 1

    def body(*refs):
        a_ref, b_ref = refs[0], refs[1]
        e_ref = refs[2] if n_extra else None
        outs = refs[2 + n_extra + len(deps):2 + n_extra + len(deps) + n_out]

        def finish(r):
            if epi == "none":
                outs[0][...] = r.astype(outs[0].dtype)
            elif epi == "add":
                outs[0][...] = (r + e_ref[...]).astype(outs[0].dtype)
            elif epi == "relu2":
                outs[0][...] = r.astype(outs[0].dtype)
                p = jnp.maximum(r, 0.0)
                outs[1][...] = (p * p).astype(outs[1].dtype)
            elif epi == "drelu2":
                outs[0][...] = (r * 2.0 * jnp.maximum(e_ref[...], 0.0)).astype(outs[0].dtype)

        def product():
            return lax.dot_general(a_ref[...].astype(BF16), b_ref[...].astype(BF16), dn, preferred_element_type=F32)

        if nk == 1:
            finish(product())
            return
        acc = refs[-1]
        k = pl.program_id(2)

        @pl.when(k == 0)
        def _():
            acc[...] = jnp.zeros_like(acc)

        acc[...] += product()

        @pl.when(k == nk - 1)
        def _():
            finish(acc[...])

    a_spec = pl.BlockSpec((tk, tm), lambda i, j, k: (k, i)) if ta else pl.BlockSpec((tm, tk), lambda i, j, k: (i, k))
    b_spec = pl.BlockSpec((tn, tk), lambda i, j, k: (j, k)) if tb else pl.BlockSpec((tk, tn), lambda i, j, k: (k, j))
    o_spec = pl.BlockSpec((tm, tn), lambda i, j, k: (i, j))
    in_specs = [a_spec, b_spec] + ([o_spec] if n_extra else []) + _dep_specs(deps)
    if epi == "relu2":
        out_shape = (jax.ShapeDtypeStruct((M, N), BF16), jax.ShapeDtypeStruct((M, N), BF16))
        out_specs = (o_spec, o_spec)
    else:
        out_shape = jax.ShapeDtypeStruct((M, N), out_dtype)
        out_specs = o_spec
    args = (a, b) + ((extra,) if n_extra else ()) + tuple(deps)
    return pl.pallas_call(
        body, name=name, grid=(M // tm, N // tn, nk), in_specs=in_specs, out_specs=out_specs,
        out_shape=out_shape, scratch_shapes=[pltpu.VMEM((tm, tn), F32)] if nk > 1 else [],
        compiler_params=_params(("parallel", "parallel", "arbitrary")),
    )(*args)


ROW_BLOCK = 256


def _rmsnorm_fwd(x, g, *, name, deps=()):
    T, D = x.shape
    tr = _tile(T, (ROW_BLOCK, 128, 64))

    def body(x_ref, g_ref, *rest):
        o_ref = rest[-1]
        xv = x_ref[...]
        r = lax.rsqrt(jnp.mean(xv * xv, axis=-1, keepdims=True) + NORM_EPS)
        o_ref[...] = (xv * r * g_ref[...]).astype(o_ref.dtype)

    return pl.pallas_call(
        body, name=name, grid=(T // tr,),
        in_specs=[pl.BlockSpec((tr, D), lambda i: (i, 0)), pl.BlockSpec((1, D), lambda i: (0, 0))] + _dep_specs(deps),
        out_specs=pl.BlockSpec((tr, D), lambda i: (i, 0)),
        out_shape=jax.ShapeDtypeStruct((T, D), BF16), compiler_params=_params(("parallel",)),
    )(x, g, *deps)


def _rmsnorm_bwd(x, g, dy, dres, *, name, deps=()):
    T, D = x.shape
    tr = _tile(T, (ROW_BLOCK, 128, 64))

    def body(x_ref, g_ref, dy_ref, dres_ref, *rest):
        dx_ref, dxb_ref, dg_ref = rest[len(deps):]
        i = pl.program_id(0)
        xv = x_ref[...]
        r = lax.rsqrt(jnp.mean(xv * xv, axis=-1, keepdims=True) + NORM_EPS)
        xh = xv * r
        dyv = dy_ref[...].astype(F32)
        dxh = dyv * g_ref[...]
        dx = dres_ref[...] + r * (dxh - xh * jnp.mean(dxh * xh, axis=-1, keepdims=True))
        dx_ref[...] = dx
        dxb_ref[...] = dx.astype(dxb_ref.dtype)

        @pl.when(i == 0)
        def _():
            dg_ref[...] = jnp.zeros_like(dg_ref)

        dg_ref[...] += jnp.sum(dyv * xh, axis=0, keepdims=True)

    blk = pl.BlockSpec((tr, D), lambda i: (i, 0))
    vec = pl.BlockSpec((1, D), lambda i: (0, 0))
    return pl.pallas_call(
        body, name=name, grid=(T // tr,), in_specs=[blk, vec, blk, blk] + _dep_specs(deps), out_specs=(blk, blk, vec),
        out_shape=(jax.ShapeDtypeStruct((T, D), F32), jax.ShapeDtypeStruct((T, D), BF16),
                   jax.ShapeDtypeStruct((1, D), F32)),
        compiler_params=_params(("arbitrary",)),
    )(x, g, dy, dres, *deps)


def _final_loss_bwd(x, g, target, *, name):
    T, D = x.shape
    tr = _tile(T, (ROW_BLOCK, 128, 64))

    def body(x_ref, g_ref, t_ref, dx_ref, dxb_ref, dg_ref, l_ref):
        i = pl.program_id(0)
        xv = x_ref[...]
        r = lax.rsqrt(jnp.mean(xv * xv, axis=-1, keepdims=True) + NORM_EPS)
        xh = xv * r
        gv = g_ref[...]
        err = xh * gv - t_ref[...]
        dy = err * (1.0 / D)
        dxh = dy * gv
        dx = r * (dxh - xh * jnp.mean(dxh * xh, axis=-1, keepdims=True))
        dx_ref[...] = dx
        dxb_ref[...] = dx.astype(dxb_ref.dtype)

        @pl.when(i == 0)
        def _():
            l_ref[...] = jnp.zeros_like(l_ref)
            dg_ref[...] = jnp.zeros_like(dg_ref)

        dg_ref[...] += jnp.sum(dy * xh, axis=0, keepdims=True)
        part = 0.5 * jnp.sum(jnp.mean(err * err, axis=-1, keepdims=True), axis=0, keepdims=True)
        l_ref[...] += jnp.broadcast_to(part, l_ref.shape)

    blk = pl.BlockSpec((tr, D), lambda i: (i, 0))
    vec = pl.BlockSpec((1, D), lambda i: (0, 0))
    return pl.pallas_call(
        body, name=name, grid=(T // tr,), in_specs=[blk, vec, blk],
        out_specs=(blk, blk, vec, pl.BlockSpec((1, LANE), lambda i: (0, 0))),
        out_shape=(jax.ShapeDtypeStruct((T, D), F32), jax.ShapeDtypeStruct((T, D), BF16),
                   jax.ShapeDtypeStruct((1, D), F32), jax.ShapeDtypeStruct((1, LANE), F32)),
        compiler_params=_params(("arbitrary",)),
    )(x, g, target)


def _colsum(x, *, name):
    T, C = x.shape
    tc = _tile(C, (512, 256, 128))

    def body(x_ref, o_ref):
        o_ref[...] = jnp.sum(x_ref[...], axis=0, keepdims=True)

    return pl.pallas_call(
        body, name=name, grid=(C // tc,), in_specs=[pl.BlockSpec((T, tc), lambda j: (0, j))],
        out_specs=pl.BlockSpec((1, tc), lambda j: (0, j)),
        out_shape=jax.ShapeDtypeStruct((1, C), F32), compiler_params=_params(("parallel",)),
    )(x)


ADAM_C1 = 1.0 - ADAM_B1 ** ADAM_STEP
ADAM_C2 = 1.0 - ADAM_B2 ** ADAM_STEP


def _adamw(w, m, v, grads, *, name):
    R, C = w.shape
    tr = _tile(R, (256, 128, 64, 32, 16, 8))
    n_g = len(grads)

    def body(*refs):
        w_ref, m_ref, v_ref = refs[:3]
        g_refs = refs[3:3 + n_g]
        g_out, d_out, m_out, v_out = refs[3 + n_g:]
        g = g_refs[0][...].astype(F32)
        for r in g_refs[1:]:
            g = g + r[...].astype(F32)
        mn = ADAM_B1 * m_ref[...] + (1.0 - ADAM_B1) * g
        vn = ADAM_B2 * v_ref[...] + (1.0 - ADAM_B2) * (g * g)
        m_hat = mn / ADAM_C1
        v_hat = vn / ADAM_C2
        g_out[...] = g
        d_out[...] = -ADAM_LR * (m_hat / (jnp.sqrt(v_hat) + ADAM_EPS) + ADAM_WD * w_ref[...])
        m_out[...] = mn
        v_out[...] = vn

    blk = pl.BlockSpec((tr, C), lambda i: (i, 0))
    sh = jax.ShapeDtypeStruct((R, C), F32)
    return pl.pallas_call(
        body, name=name, grid=(R // tr,), in_specs=[blk] * (3 + n_g), out_specs=(blk,) * 4,
        out_shape=(sh,) * 4, compiler_params=_params(("parallel",)),
    )(w, m, v, *grads)


def _sum_parts(parts, *, name, deps=()):
    P, R, C = parts.shape
    tr = _tile(R, (256, 128, 64, 32, 16, 8))

    def body(p_ref, *rest):
        o_ref = rest[-1]
        s = p_ref[0].astype(F32)
        for i in range(1, P):
            s = s + p_ref[i].astype(F32)
        o_ref[...] = s

    return pl.pallas_call(
        body, name=name, grid=(R // tr,),
        in_specs=[pl.BlockSpec((P, tr, C), lambda i: (0, i, 0))] + _dep_specs(deps),
        out_specs=pl.BlockSpec((tr, C), lambda i: (i, 0)),
        out_shape=jax.ShapeDtypeStruct((R, C), F32), compiler_params=_params(("parallel",)),
    )(parts, *deps)


def _ret_tables(T):
    H, C = RET_HEADS, CHUNK
    log_gamma = jnp.log1p(-jnp.exp2(-5.0 - jnp.arange(H, dtype=F32)))
    pos = jnp.arange(C, dtype=F32)
    dist = jnp.abs(pos[:, None] - pos[None, :])
    dm = jnp.exp(log_gamma[:, None, None] * dist)
    qdec = jnp.exp(log_gamma[:, None] * (pos + 1.0))[:, :, None]
    kdec = jnp.exp(log_gamma[:, None] * (C - 1.0 - pos))[:, :, None]
    cdec = jnp.exp(log_gamma * C)[:, None, None]
    inv = ROPE_BASE ** (-jnp.arange(0, RET_DK, 2, dtype=F32) / RET_DK)
    ang = jnp.arange(T, dtype=F32)[:, None] * inv[None, :]
    return dm, qdec, kdec, cdec, jnp.cos(ang), jnp.sin(ang)


def _rot(x, cos, sin):
    h = x.shape[1] // 2
    x1, x2 = x[:, :h], x[:, h:]
    return jnp.concatenate([x1 * cos - x2 * sin, x1 * sin + x2 * cos], axis=1)


def _unrot(dy, cos, sin):
    h = dy.shape[1] // 2
    d1, d2 = dy[:, :h], dy[:, h:]
    return jnp.concatenate([d1 * cos + d2 * sin, d2 * cos - d1 * sin], axis=1)


def _ret_specs(N, rev):
    H, C, DK, DV = RET_HEADS, CHUNK, RET_DK, RET_DV
    cn = (lambda n: N - 1 - n) if rev else (lambda n: n)
    HB, G = RET_HB, H // RET_HB
    q = pl.BlockSpec((C, HB * DK), lambda h, n: (cn(n), h))
    k = pl.BlockSpec((C, HB * DK), lambda h, n: (cn(n), G + h))
    v = pl.BlockSpec((C, HB * DV), lambda h, n: (cn(n), G + h))
    g = pl.BlockSpec((C, HB * DV), lambda h, n: (cn(n), 2 * G + h))
    cs = pl.BlockSpec((C, DK // 2), lambda h, n: (cn(n), 0))
    dm = pl.BlockSpec((HB, C, C), lambda h, n: (h, 0, 0))
    dec = pl.BlockSpec((HB, C, 1), lambda h, n: (h, 0, 0))
    cd = pl.BlockSpec((HB, 1, 1), lambda h, n: (h, 0, 0))
    gain = pl.BlockSpec((HB, 1, DV), lambda h, n: (h, 0, 0))
    st = pl.BlockSpec((HB, 1, DK, DV), lambda h, n: (h, cn(n), 0, 0))
    ov = pl.BlockSpec((C, HB * DV), lambda h, n: (cn(n), h))
    return q, k, v, g, cs, dm, dec, cd, gain, st, ov


def _ret_fwd(proj, gain, tables, *, name):
    T = proj.shape[0]
    H, C, DK, DV = RET_HEADS, CHUNK, RET_DK, RET_DV
    N = T // C
    dm_t, qdec_t, kdec_t, cdec_t, cos_t, sin_t = tables

    def body(q_ref, k_ref, v_ref, g_ref, cos_ref, sin_ref, dm_ref, qd_ref, kd_ref, cd_ref, gain_ref,
             og_ref, st_ref, s_acc):
        n = pl.program_id(1)

        @pl.when(n == 0)
        def _():
            s_acc[...] = jnp.zeros_like(s_acc)

        cos, sin = cos_ref[...], sin_ref[...]

        def head(hb):
            kc, vc = pl.ds(hb * DK, DK), pl.ds(hb * DV, DV)
            qr = _rot(q_ref[:, kc], cos, sin)
            kr = _rot(k_ref[:, kc], cos, sin) * (DK ** -0.5)
            v = v_ref[:, vc]
            sp = s_acc[hb]
            st_ref[hb, 0] = sp.astype(st_ref.dtype)
            scores = _bdot(qr, kr, NT) * dm_ref[hb]
            inter = _bdot(qr * qd_ref[hb], sp)
            s_acc[hb] = sp * cd_ref[hb] + _bdot(kr * kd_ref[hb], v, TN)
            yield
            o = _bdot(scores, v) + inter
            yield
            oc = o - jnp.mean(o, axis=-1, keepdims=True)
            rstd = lax.rsqrt(jnp.mean(oc * oc, axis=-1, keepdims=True) + NORM_EPS)
            og_ref[:, vc] = (oc * rstd * gain_ref[hb] * _silu(g_ref[:, vc])).astype(og_ref.dtype)

        _interleave([head(hb) for hb in range(RET_HB)])

    q, k, v, g, cs, dm, dec, cd, gn, st, ov = _ret_specs(N, False)
    return pl.pallas_call(
        body, name=name, grid=(H // RET_HB, N),
        in_specs=[q, k, v, g, cs, cs, dm, dec, dec, cd, gn], out_specs=(ov, st),
        out_shape=(jax.ShapeDtypeStruct((T, H * DV), BF16), jax.ShapeDtypeStruct((H, N, DK, DV), BF16)),
        scratch_shapes=[pltpu.VMEM((RET_HB, DK, DV), F32)],
        compiler_params=_params(("arbitrary", "arbitrary")),
    )(proj, proj, proj, proj, cos_t, sin_t, dm_t, qdec_t, kdec_t, cdec_t, gain)


def _ret_bwd(proj, gain, tables, states, dog, *, name):
    T = proj.shape[0]
    H, C, DK, DV = RET_HEADS, CHUNK, RET_DK, RET_DV
    N = T // C
    dm_t, qdec_t, kdec_t, cdec_t, cos_t, sin_t = tables

    def body(q_ref, k_ref, v_ref, g_ref, cos_ref, sin_ref, dm_ref, qd_ref, kd_ref, cd_ref, gain_ref,
             st_ref, dog_ref, dq_ref, dk_ref, dv_ref, dg_ref, dgain_ref, ds_acc):
        n = pl.program_id(1)

        @pl.when(n == 0)
        def _():
            ds_acc[...] = jnp.zeros_like(ds_acc)
            dgain_ref[...] = jnp.zeros_like(dgain_ref)

        cos, sin = cos_ref[...], sin_ref[...]
        scale = DK ** -0.5

        def head(hb):
            kc, vc = pl.ds(hb * DK, DK), pl.ds(hb * DV, DV)
            qr = _rot(q_ref[:, kc], cos, sin)
            kr = _rot(k_ref[:, kc], cos, sin) * scale
            v = v_ref[:, vc]
            g = g_ref[:, vc]
            sp = st_ref[hb, 0]
            dm = dm_ref[hb]
            qd, kd = qd_ref[hb], kd_ref[hb]
            gain_v = gain_ref[hb]
            scores = _bdot(qr, kr, NT) * dm
            inter = _bdot(qr * qd, sp)
            yield
            o = _bdot(scores, v) + inter
            yield
            oc = o - jnp.mean(o, axis=-1, keepdims=True)
            rstd = lax.rsqrt(jnp.mean(oc * oc, axis=-1, keepdims=True) + NORM_EPS)
            oh = oc * rstd
            dy = dog_ref[:, vc].astype(F32)
            dg_ref[:, vc] = (dy * oh * gain_v * _dsilu(g)).astype(dg_ref.dtype)
            dnorm = dy * _silu(g)
            dgain_ref[hb] += jnp.sum(dnorm * oh, axis=0, keepdims=True)
            doh = dnorm * gain_v
            do = rstd * (doh - jnp.mean(doh, axis=-1, keepdims=True)
                         - oh * jnp.mean(doh * oh, axis=-1, keepdims=True))
            dsn = ds_acc[hb]
            dp = _bdot(do, v, NT) * dm
            dq_inter = _bdot(do, sp, NT) * qd
            dk_inter = _bdot(v, dsn, NT) * kd
            dv_ref[:, vc] = (_bdot(scores, do, TN) + _bdot(kr * kd, dsn)).astype(dv_ref.dtype)
            ds_acc[hb] = dsn * cd_ref[hb] + _bdot(qr * qd, do, TN)
            yield
            dqr = _bdot(dp, kr) + dq_inter
            dkr = _bdot(dp, qr, TN) + dk_inter
            yield
            dq_ref[:, kc] = _unrot(dqr, cos, sin).astype(dq_ref.dtype)
            dk_ref[:, kc] = _unrot(dkr * scale, cos, sin).astype(dk_ref.dtype)

        _interleave([head(hb) for hb in range(RET_HB)])

    q, k, v, g, cs, dm, dec, cd, gn, st, ov = _ret_specs(N, True)
    return pl.pallas_call(
        body, name=name, grid=(H // RET_HB, N),
        in_specs=[q, k, v, g, cs, cs, dm, dec, dec, cd, gn, st, ov],
        out_specs=(q, q, ov, ov, gn),
        out_shape=(jax.ShapeDtypeStruct((T, H * DK), BF16), jax.ShapeDtypeStruct((T, H * DK), BF16),
                   jax.ShapeDtypeStruct((T, H * DV), BF16), jax.ShapeDtypeStruct((T, H * DV), BF16),
                   jax.ShapeDtypeStruct((H, 1, DV), F32)),
        scratch_shapes=[pltpu.VMEM((RET_HB, DK, DV), F32)],
        compiler_params=_params(("arbitrary", "arbitrary")),
    )(proj, proj, proj, proj, cos_t, sin_t, dm_t, qdec_t, kdec_t, cdec_t, gain, states, dog)


def _gla_specs(N, rev):
    H, C, DK, DV = GLA_HEADS, CHUNK, GLA_DK, GLA_DV
    cn = (lambda n: N - 1 - n) if rev else (lambda n: n)
    HB, G = GLA_HB, H // GLA_HB
    q = pl.BlockSpec((C, HB * DK), lambda h, n: (cn(n), h))
    k = pl.BlockSpec((C, HB * DK), lambda h, n: (cn(n), G + h))
    v = pl.BlockSpec((C, HB * DV), lambda h, n: (cn(n), G + h))
    r = pl.BlockSpec((C, HB * DV), lambda h, n: (cn(n), 2 * G + h))
    bias = pl.BlockSpec((1, HB * DK), lambda h, n: (0, h))
    gain = pl.BlockSpec((HB, 1, DV), lambda h, n: (h, 0, 0))
    st = pl.BlockSpec((HB, 1, DV, DK), lambda h, n: (h, cn(n), 0, 0))
    ov = pl.BlockSpec((C, HB * DV), lambda h, n: (cn(n), h))
    return q, k, v, r, bias, gain, st, ov


def _gla_chunk(q, k, v, gl_raw, bias):
    C, DK = q.shape
    gl = gl_raw + bias
    la = -_softplus(-gl) * (1.0 / GLA_TAU)
    lower = _iota2((C, C), 0) >= _iota2((C, C), 1)
    cum = _fdot(jnp.where(lower, 1.0, 0.0), la)
    yield
    ref = _pick_row(cum, C // 2 - 1)
    clast = _pick_row(cum, C - 1)
    fw, bw = jnp.exp(cum - ref), jnp.exp(ref - cum)
    qs = q * (DK ** -0.5)
    s_lo = _bdot(qs * fw, k * bw, NT)
    s_up = _bdot(qs * bw, k * fw, NT)
    yield
    scores = jnp.where(lower, s_lo, s_up)
    return gl, cum, clast, fw, bw, qs, k, v, scores, lower


def _gla_fwd(proj, glogit, bias, gain, *, name):
    T = proj.shape[0]
    H, C, DK, DV = GLA_HEADS, CHUNK, GLA_DK, GLA_DV
    N = T // C

    def body(q_ref, k_ref, v_ref, r_ref, gl_ref, bias_ref, gain_ref, og_ref, st_ref, s_acc):
        n = pl.program_id(1)

        @pl.when(n == 0)
        def _():
            s_acc[...] = jnp.zeros_like(s_acc)

        def head(hb):
            kc, vc = pl.ds(hb * DK, DK), pl.ds(hb * DV, DV)
            gl, cum, clast, fw, bw, qs, k, v, scores, lower = yield from _gla_chunk(
                q_ref[:, kc], k_ref[:, kc], v_ref[:, vc], gl_ref[:, kc], bias_ref[:, kc])
            sp = s_acc[hb]
            st_ref[hb, 0] = sp
            o = _bdot(scores, v) + _bdot(qs * jnp.exp(cum), sp, NT)
            s_acc[hb] = sp * jnp.exp(clast) + _bdot(v, k * jnp.exp(clast - cum), TN)
            yield
            rstd = lax.rsqrt(jnp.mean(o * o, axis=-1, keepdims=True) + NORM_EPS)
            og_ref[:, vc] = (o * rstd * gain_ref[hb] * _silu(r_ref[:, vc])).astype(og_ref.dtype)

        _interleave([head(hb) for hb in range(GLA_HB)])

    q, k, v, r, bias_s, gn, st, ov = _gla_specs(N, False)
    return pl.pallas_call(
        body, name=name, grid=(H // GLA_HB, N), in_specs=[q, k, v, r, q, bias_s, gn], out_specs=(ov, st),
        out_shape=(jax.ShapeDtypeStruct((T, H * DV), BF16), jax.ShapeDtypeStruct((H, N, DV, DK), F32)),
        scratch_shapes=[pltpu.VMEM((GLA_HB, DV, DK), F32)],
        compiler_params=_params(("arbitrary", "arbitrary")),
    )(proj, proj, proj, proj, glogit, bias, gain)


def _gla_bwd(proj, glogit, bias, gain, states, dog, *, name):
    T = proj.shape[0]
    H, C, DK, DV = GLA_HEADS, CHUNK, GLA_DK, GLA_DV
    N = T // C

    def body(q_ref, k_ref, v_ref, r_ref, gl_ref, bias_ref, gain_ref, st_ref, dog_ref,
             dq_ref, dk_ref, dv_ref, dr_ref, dgl_ref, dgain_ref, ds_acc):
        n = pl.program_id(1)

        @pl.when(n == 0)
        def _():
            ds_acc[...] = jnp.zeros_like(ds_acc)
            dgain_ref[...] = jnp.zeros_like(dgain_ref)

        def head(hb):
            kc, vc = pl.ds(hb * DK, DK), pl.ds(hb * DV, DV)
            gl, cum, clast, fw, bw, qs, k, v, scores, lower = yield from _gla_chunk(
                q_ref[:, kc], k_ref[:, kc], v_ref[:, vc], gl_ref[:, kc], bias_ref[:, kc])
            sp = st_ref[hb, 0]
            ecum, e2, cdec = jnp.exp(cum), jnp.exp(clast - cum), jnp.exp(clast)
            q_in, k_end = qs * ecum, k * e2
            o = _bdot(scores, v) + _bdot(q_in, sp, NT)
            yield
            rstd = lax.rsqrt(jnp.mean(o * o, axis=-1, keepdims=True) + NORM_EPS)
            oh = o * rstd
            r = r_ref[:, vc]
            gain_v = gain_ref[hb]
            dy = dog_ref[:, vc].astype(F32)
            dr_ref[:, vc] = (dy * oh * gain_v * _dsilu(r)).astype(dr_ref.dtype)
            dnorm = dy * _silu(r)
            dgain_ref[hb] += jnp.sum(dnorm * oh, axis=0, keepdims=True)
            doh = dnorm * gain_v
            do = rstd * (doh - oh * jnp.mean(doh * oh, axis=-1, keepdims=True))
            dsn = ds_acc[hb]
            dq_in = _bdot(do, sp)
            dk_end = _bdot(v, dsn)
            dv_ref[:, vc] = (_bdot(k_end, dsn, NT) + _bdot(scores, do, TN)).astype(dv_ref.dtype)
            dcdec = jnp.sum(dsn * sp, axis=0, keepdims=True)
            ds_acc[hb] = dsn * cdec + _bdot(do, q_in, TN)
            dsc = _bdot(do, v, NT)
            yield
            ds_lo = jnp.where(lower, dsc, 0.0)
            ds_up = jnp.where(lower, 0.0, dsc)
            qf, kb, qb, kf = qs * fw, k * bw, qs * bw, k * fw
            dqf, dkb = _bdot(ds_lo, kb), _bdot(ds_lo, qf, TN)
            dqb, dkf = _bdot(ds_up, kf), _bdot(ds_up, qb, TN)
            yield
            dq_ref[:, kc] = ((dqf * fw + dqb * bw + dq_in * ecum) * (DK ** -0.5)).astype(dq_ref.dtype)
            dk_ref[:, kc] = (dkb * bw + dkf * fw + dk_end * e2).astype(dk_ref.dtype)
            dz = (dqf * qs + dkf * k) * fw - (dqb * qs + dkb * k) * bw
            kk = dk_end * k_end
            dcum = dz + dq_in * q_in - kk
            rows = _iota2((C, DK), 0)
            dcum = dcum + jnp.where(rows == C // 2 - 1, -jnp.sum(dz, axis=0, keepdims=True), 0.0)
            dcum = dcum + jnp.where(rows == C - 1, jnp.sum(kk, axis=0, keepdims=True) + dcdec * cdec, 0.0)
            upper = _iota2((C, C), 0) <= _iota2((C, C), 1)
            dla = _fdot(jnp.where(upper, 1.0, 0.0), dcum)
            yield
            dgl_ref[:, kc] = dla * (1.0 / GLA_TAU) * _sigmoid(-gl)

        _interleave([head(hb) for hb in range(GLA_HB)])

    q, k, v, r, bias_s, gn, st, ov = _gla_specs(N, True)
    return pl.pallas_call(
        body, name=name, grid=(H // GLA_HB, N), in_specs=[q, k, v, r, q, bias_s, gn, st, ov],
        out_specs=(q, q, ov, ov, q, gn),
        out_shape=(jax.ShapeDtypeStruct((T, H * DK), BF16), jax.ShapeDtypeStruct((T, H * DK), BF16),
                   jax.ShapeDtypeStruct((T, H * DV), BF16), jax.ShapeDtypeStruct((T, H * DV), BF16),
                   jax.ShapeDtypeStruct((T, H * DK), F32), jax.ShapeDtypeStruct((H, 1, DV), F32)),
        scratch_shapes=[pltpu.VMEM((GLA_HB, DV, DK), F32)],
        compiler_params=_params(("arbitrary", "arbitrary")),
    )(proj, proj, proj, proj, glogit, bias, gain, states, dog)


def _conv(xv, w_ref):
    out = _shift_down(xv, CONV_WIDTH - 1) * w_ref[0:1, :]
    for tap in range(1, CONV_WIDTH):
        out = out + _shift_down(xv, CONV_WIDTH - 1 - tap) * w_ref[tap:tap + 1, :]
    return out


def _conv_bwd(xv, w_ref, dpre, dw_ref):
    dx = None
    for tap in range(CONV_WIDTH):
        s = CONV_WIDTH - 1 - tap
        t = _shift_up(dpre, s) * w_ref[tap:tap + 1, :]
        dx = t if dx is None else dx + t
        dw_ref[tap:tap + 1, :] = jnp.sum(dpre * _shift_down(xv, s), axis=0, keepdims=True)
    return dx


CONV_COLS = 256


def _conv_silu_fwd(x, w, *, name):
    T = x.shape[0]
    n = w.shape[1]

    def body(x_ref, w_ref, o_ref):
        o_ref[...] = _silu(_conv(x_ref[...], w_ref))

    return pl.pallas_call(
        body, name=name, grid=(n // CONV_COLS,),
        in_specs=[pl.BlockSpec((T, CONV_COLS), lambda j: (0, j)), pl.BlockSpec((CONV_WIDTH, CONV_COLS), lambda j: (0, j))],
        out_specs=pl.BlockSpec((T, CONV_COLS), lambda j: (0, j)),
        out_shape=jax.ShapeDtypeStruct((T, n), F32), compiler_params=_params(("parallel",)),
    )(x, w)


def _conv_silu_bwd(x, w, dact, *, name):
    T = x.shape[0]
    n = w.shape[1]

    def body(x_ref, w_ref, da_ref, dx_ref, dw_ref):
        xv = x_ref[...]
        dpre = da_ref[...] * _dsilu(_conv(xv, w_ref))
        dx_ref[...] = _conv_bwd(xv, w_ref, dpre, dw_ref).astype(dx_ref.dtype)

    blk = pl.BlockSpec((T, CONV_COLS), lambda j: (0, j))
    wb = pl.BlockSpec((CONV_WIDTH, CONV_COLS), lambda j: (0, j))
    return pl.pallas_call(
        body, name=name, grid=(n // CONV_COLS,), in_specs=[blk, wb, blk], out_specs=(blk, wb),
        out_shape=(jax.ShapeDtypeStruct((T, n), BF16), jax.ShapeDtypeStruct((CONV_WIDTH, n), F32)),
        compiler_params=_params(("parallel",)),
    )(x, w, dact)


def _interleave(gens):
    results = [None] * len(gens)
    live = list(range(len(gens)))
    while live:
        for i in list(live):
            try:
                next(gens[i])
            except StopIteration as done:
                results[i] = done.value
                live.remove(i)
    return results


def _unit_lower_inverse(a):
    n = a.shape[0]
    eye = jnp.where(_iota2((n, n), 0) == _iota2((n, n), 1), 1.0, 0.0)
    p = -a
    t = eye + p
    for _ in range(5):
        p = _fdot(p, p)
        yield
        t = t + _fdot(t, p)
        yield
    return t


def _gdn_specs(N, rev):
    H, C, DK, DV = GDN_HEADS, CHUNK, GDN_DK, GDN_DV
    cn = (lambda n: N - 1 - n) if rev else (lambda n: n)
    HB, G = GDN_HB, H // GDN_HB
    q = pl.BlockSpec((C, HB * DK), lambda h, n: (cn(n), h))
    k = pl.BlockSpec((C, HB * DK), lambda h, n: (cn(n), G + h))
    v = pl.BlockSpec((C, HB * DV), lambda h, n: (cn(n), 2 * G + h))
    z = pl.BlockSpec((C, HB * DV), lambda h, n: (cn(n), 3 * G + h))
    gates = pl.BlockSpec((C, LANE), lambda h, n: (cn(n), 0))
    sc = pl.BlockSpec((HB, 1, 1), lambda h, n: (h, 0, 0))
    gain = pl.BlockSpec((1, DV), lambda h, n: (0, 0))
    st = pl.BlockSpec((HB, 1, DK, DV), lambda h, n: (h, cn(n), 0, 0))
    return q, k, v, z, gates, sc, gain, st


def _gdn_solved_specs(N, rev):
    C = CHUNK
    cn = (lambda n: N - 1 - n) if rev else (lambda n: n)
    return (pl.BlockSpec((GDN_HB, 1, C, C), lambda h, n: (h, cn(n), 0, 0)),
            pl.BlockSpec((GDN_HB, 1, C, GDN_DV + GDN_DK), lambda h, n: (h, cn(n), 0, 0)))


def _gdn_chunk(q_ref, k_ref, v_ref, gates_ref, alog_ref, dtb_ref, h, solved=None):
    H, C, DK, DV = GDN_HEADS, CHUNK, GDN_DK, GDN_DV
    gates = gates_ref[...]
    lane = _iota2(gates.shape, 1)
    bl = jnp.sum(jnp.where(lane == h, gates, 0.0), axis=1, keepdims=True)
    al = jnp.sum(jnp.where(lane == H + h, gates, 0.0), axis=1, keepdims=True)
    beta = _sigmoid(bl)
    ea = jnp.exp(alog_ref[...])
    xs = al + dtb_ref[...]
    la = -ea * _softplus(xs)
    ii, jj = _iota2((C, C), 0), _iota2((C, C), 1)
    strict = ii > jj
    cum_col = jnp.sum(jnp.where(ii >= jj, _col_to_row(la), 0.0), axis=1, keepdims=True)
    cum_row = jnp.sum(jnp.where(ii <= jj, la, 0.0), axis=0, keepdims=True)
    q, k, v = q_ref[...], k_ref[...], v_ref[...]
    rq = lax.rsqrt(jnp.sum(q * q, axis=-1, keepdims=True) + NORM_EPS)
    rk = lax.rsqrt(jnp.sum(k * k, axis=-1, keepdims=True) + NORM_EPS)
    qn = q * rq * (DK ** -0.5)
    kn = k * rk
    rel = jnp.where(strict, jnp.exp(jnp.where(strict, cum_col - cum_row, 0.0)), 0.0)
    rg = rel * _bdot(kn, kn, NT)
    yield
    a = beta * rg
    e_col = jnp.exp(cum_col)
    clast = _pick_row(cum_col, C - 1)
    if solved is None:
        tm = yield from _unit_lower_inverse(a)
        rhs = jnp.concatenate([beta * v, (beta * e_col) * kn], axis=1)
        sol = _fdot(tm, rhs)
        yield
    else:
        tm, sol = solved
    u, w = sol[:, :DV], sol[:, DV:]
    dd = jnp.exp(clast - cum_col)
    ke = kn * dd
    g = jnp.exp(clast)
    eye = jnp.where(_iota2((DK, DK), 0) == _iota2((DK, DK), 1), 1.0, 0.0)
    trans = g * eye - _bdot(ke, w, TN)
    inject = _bdot(ke, u, TN)
    yield
    return dict(beta=beta, ea=ea, xs=xs, la=la, strict=strict, ii=ii, jj=jj, q=q, k=k, v=v, rq=rq, rk=rk,
                qn=qn, kn=kn, rel=rel, rg=rg, a=a, tm=tm, e_col=e_col, sol=sol, u=u, w=w, dd=dd, ke=ke,
                g=g, eye=eye, trans=trans, inject=inject)


def _gdn_fwd(act, proj, gates, a_log, dt_bias, gain, *, name):
    T = act.shape[0]
    H, C, DK, DV = GDN_HEADS, CHUNK, GDN_DK, GDN_DV
    N = T // C

    def body(q_ref, k_ref, v_ref, z_ref, gates_ref, alog_ref, dtb_ref, gain_ref, og_ref, st_ref, tm_ref, sol_ref,
             s_acc):
        hg, n = pl.program_id(0), pl.program_id(1)

        @pl.when(n == 0)
        def _():
            s_acc[...] = jnp.zeros_like(s_acc)

        def head(hb):
            cols = pl.ds(hb * DK, DK)
            c = yield from _gdn_chunk(q_ref.at[:, cols], k_ref.at[:, cols], v_ref.at[:, cols], gates_ref,
                                      alog_ref.at[hb], dtb_ref.at[hb], hg * GDN_HB + hb)
            tm_ref[hb, 0] = c["tm"]
            sol_ref[hb, 0] = c["sol"]
            sp = s_acc[hb]
            st_ref[hb, 0] = sp
            snew = _bdot(c["trans"], sp) + c["inject"]
            yield
            s_acc[hb] = snew
            o = _bdot(c["qn"], snew)
            yield
            rstd = lax.rsqrt(jnp.mean(o * o, axis=-1, keepdims=True) + NORM_EPS)
            og_ref[:, cols] = (o * rstd * gain_ref[...] * _silu(z_ref[:, cols])).astype(og_ref.dtype)

        _interleave([head(hb) for hb in range(GDN_HB)])

    q, k, v, z, gt, sc, gn, st = _gdn_specs(N, False)
    tm_s, sol_s = _gdn_solved_specs(N, False)
    return pl.pallas_call(
        body, name=name, grid=(H // GDN_HB, N), in_specs=[q, k, v, z, gt, sc, sc, gn],
        out_specs=(q, st, tm_s, sol_s),
        out_shape=(jax.ShapeDtypeStruct((T, H * DV), BF16), jax.ShapeDtypeStruct((H, N, DK, DV), F32),
                   jax.ShapeDtypeStruct((H, N, C, C), F32), jax.ShapeDtypeStruct((H, N, C, DV + DK), F32)),
        scratch_shapes=[pltpu.VMEM((GDN_HB, DK, DV), F32)],
        compiler_params=_params(("arbitrary", "arbitrary")),
    )(act, act, act, proj, gates, a_log, dt_bias, gain)


def _gdn_bwd(act, proj, gates, a_log, dt_bias, gain, states, solved, dog, *, name):
    T = act.shape[0]
    H, C, DK, DV = GDN_HEADS, CHUNK, GDN_DK, GDN_DV
    N = T // C

    def rsum(x):
        return jnp.sum(x, axis=1, keepdims=True)

    def body(q_ref, k_ref, v_ref, z_ref, gates_ref, alog_ref, dtb_ref, gain_ref, st_ref, tm_ref, sol_ref, dog_ref,
             dq_ref, dk_ref, dv_ref, dz_ref, dgates_ref, dscal_ref, dgain_ref, ds_acc):
        hg, n = pl.program_id(0), pl.program_id(1)

        @pl.when(n == 0)
        def _():
            ds_acc[...] = jnp.zeros_like(ds_acc)
            dgain_ref[...] = jnp.zeros_like(dgain_ref)
            dscal_ref[...] = jnp.zeros_like(dscal_ref)

        _interleave([one_head(hb, hg * GDN_HB + hb, q_ref, k_ref, v_ref, z_ref, gates_ref, alog_ref, dtb_ref, gain_ref,
                              st_ref, tm_ref, sol_ref, dog_ref, dq_ref, dk_ref, dv_ref, dz_ref, dgates_ref, dscal_ref,
                              dgain_ref, ds_acc)
                     for hb in range(GDN_HB)])

    def one_head(hb, h, q_ref, k_ref, v_ref, z_ref, gates_ref, alog_ref, dtb_ref, gain_ref, st_ref, tm_ref, sol_ref,
                 dog_ref, dq_ref, dk_ref, dv_ref, dz_ref, dgates_ref, dscal_ref, dgain_ref, ds_acc):
        cols = pl.ds(hb * DK, DK)
        c = yield from _gdn_chunk(q_ref.at[:, cols], k_ref.at[:, cols], v_ref.at[:, cols], gates_ref,
                                  alog_ref.at[hb], dtb_ref.at[hb], h, solved=(tm_ref[hb, 0], sol_ref[hb, 0]))
        beta, kn, qn, v, ke, u, w, dd, e_col = c["beta"], c["kn"], c["qn"], c["v"], c["ke"], c["u"], c["w"], c["dd"], c["e_col"]
        sp = st_ref[hb, 0]
        snew = _bdot(c["trans"], sp) + c["inject"]
        yield
        o = _bdot(qn, snew)
        yield
        rstd = lax.rsqrt(jnp.mean(o * o, axis=-1, keepdims=True) + NORM_EPS)
        oh = o * rstd
        z = z_ref[:, cols]
        gain_v = gain_ref[...]
        dy = dog_ref[:, cols].astype(F32)
        dz_ref[:, cols] = (dy * oh * gain_v * _dsilu(z)).astype(dz_ref.dtype)
        dnorm = dy * _silu(z)
        dgain_ref[hb] += jnp.sum(dnorm * oh, axis=0, keepdims=True)
        doh = dnorm * gain_v
        do = rstd * (doh - oh * jnp.mean(doh * oh, axis=-1, keepdims=True))

        dstot = ds_acc[hb] + _bdot(qn, do, TN)
        dqn = _bdot(do, snew, NT)
        yield
        dtrans = _bdot(dstot, sp, NT)
        ds_acc[hb] = _bdot(c["trans"], dstot, TN)
        yield
        dg = jnp.sum(jnp.sum(dtrans * c["eye"], axis=1, keepdims=True), axis=0, keepdims=True)
        m = -dtrans
        dke = _bdot(w, m, NT) + _bdot(u, dstot, NT)
        dw = _bdot(ke, m)
        du = _bdot(ke, dstot)
        yield
        drhs = _fdot(c["tm"], jnp.concatenate([du, dw], axis=1), TN)
        yield
        da = jnp.where(c["strict"], -_fdot(drhs, c["sol"], NT), 0.0)
        yield
        drhs_u, drhs_w = drhs[:, :DV], drhs[:, DV:]
        rwk = rsum(drhs_w * kn)
        dbeta = rsum(da * c["rg"]) + rsum(drhs_u * v) + rwk * e_col
        dgm = da * beta * c["rel"]
        dkn = _bdot(dgm, kn) + _bdot(dgm, kn, TN) + (beta * e_col) * drhs_w + dd * dke
        yield
        dv_ref[:, cols] = beta * drhs_u
        r_ = da * c["a"]
        ddd = rsum(dke * kn)
        dc = rsum(r_) - _row_to_col(jnp.sum(r_, axis=0, keepdims=True)) + beta * rwk * e_col - ddd * dd
        dclast = jnp.sum(ddd * dd, axis=0, keepdims=True) + dg * c["g"]
        dc = dc + jnp.where(_iota2((C, 1), 0) == C - 1, dclast, 0.0)
        dla = jnp.sum(jnp.where(c["ii"] <= c["jj"], _col_to_row(dc), 0.0), axis=1, keepdims=True)
        dalog = jnp.sum(dla * c["la"], axis=0, keepdims=True)
        dxs = dla * (-c["ea"]) * _sigmoid(c["xs"])
        ddtb = jnp.sum(dxs, axis=0, keepdims=True)
        dbl = dbeta * beta * (1.0 - beta)
        lane = _iota2((C, LANE), 1)
        dgates_ref[hb] = jnp.where(lane == 0, dbl, jnp.where(lane == 1, dxs, 0.0))
        lane8 = _iota2((8, LANE), 1)
        dscal_ref[hb] += jnp.where(lane8 == 0, dalog, jnp.where(lane8 == 1, ddtb, 0.0))
        dk_ref[:, cols] = c["rk"] * (dkn - kn * rsum(dkn * kn))
        qh = c["q"] * c["rq"]
        dqs = dqn * (DK ** -0.5)
        dq_ref[:, cols] = c["rq"] * (dqs - qh * rsum(dqs * qh))

    q, k, v, z, gt, sc, gn, st = _gdn_specs(N, True)
    dgt = pl.BlockSpec((GDN_HB, C, LANE), lambda h, n: (h, N - 1 - n, 0))
    dsc = pl.BlockSpec((GDN_HB, 8, LANE), lambda h, n: (h, 0, 0))
    dgn = pl.BlockSpec((GDN_HB, 1, DV), lambda h, n: (h, 0, 0))
    sh = jax.ShapeDtypeStruct((T, H * DK), F32)
    tm_s, sol_s = _gdn_solved_specs(N, True)
    return pl.pallas_call(
        body, name=name, grid=(H // GDN_HB, N), in_specs=[q, k, v, z, gt, sc, sc, gn, st, tm_s, sol_s, q],
        out_specs=(q, q, q, q, dgt, dsc, dgn),
        out_shape=(sh, sh, sh, jax.ShapeDtypeStruct((T, H * DV), BF16),
                   jax.ShapeDtypeStruct((H, T, LANE), F32), jax.ShapeDtypeStruct((H, 8, LANE), F32),
                   jax.ShapeDtypeStruct((H, 1, DV), F32)),
        scratch_shapes=[pltpu.VMEM((GDN_HB, DK, DV), F32)],
        compiler_params=_params(("arbitrary", "arbitrary")),
    )(act, act, act, proj, gates, a_log, dt_bias, gain, states, *solved, dog)


SUBLANES = 8


def _linear_scan(a_ref, b_ref, h_ref, reverse):
    T, W = a_ref.shape
    nb = T // SUBLANES
    row = _iota2((SUBLANES, W), 0)

    def blk(bi, carry):
        bb = (nb - 1 - bi) if reverse else bi
        off = pl.multiple_of(bb * SUBLANES, SUBLANES)
        a = a_ref[pl.ds(off, SUBLANES), :]
        b = b_ref[pl.ds(off, SUBLANES), :]
        for d in (1, 2, 4):
            if reverse:
                edge = row >= SUBLANES - d
                a_sh = jnp.where(edge, 1.0, pltpu.roll(a, SUBLANES - d, 0))
                b_sh = jnp.where(edge, 0.0, pltpu.roll(b, SUBLANES - d, 0))
            else:
                edge = row < d
                a_sh = jnp.where(edge, 1.0, pltpu.roll(a, d, 0))
                b_sh = jnp.where(edge, 0.0, pltpu.roll(b, d, 0))
            b = a * b_sh + b
            a = a * a_sh
        h = a * carry + b
        h_ref[pl.ds(off, SUBLANES), :] = h
        return h[0:1, :] if reverse else h[SUBLANES - 1:SUBLANES, :]

    lax.fori_loop(0, nb, blk, jnp.zeros((1, W), F32))


def _lru_specs(T):
    B, W = LRU_BLOCKS, LRU_BLOCK
    xb = pl.BlockSpec((T, W), lambda j: (0, j))
    yb = pl.BlockSpec((T, W), lambda j: (0, B + j))
    cw = pl.BlockSpec((CONV_WIDTH, W), lambda j: (0, j))
    vec = pl.BlockSpec((1, W), lambda j: (0, j))
    wg = pl.BlockSpec((1, W, W), lambda j: (j, 0, 0))
    bg = pl.BlockSpec((1, 1, W), lambda j: (j, 0, 0))
    return xb, yb, cw, vec, wg, bg


def _lru_gates(xb_ref, cw_ref, cb_ref, wr_ref, br_ref, wi_ref, bi_ref, lam_ref):
    xv = xb_ref[...]
    xc = _conv(xv, cw_ref) + cb_ref[...]
    r = _sigmoid(_bdot(xc, wr_ref[0]) + br_ref[0])
    i = _sigmoid(_bdot(xc, wi_ref[0]) + bi_ref[0])
    sp = _softplus(-lam_ref[...])
    la = -LRU_C * sp * r
    a = jnp.exp(la)
    s = jnp.sqrt(-_expm1(2.0 * la))
    return xv, xc, r, i, sp, a, s


def _lru_fwd(proj, conv_w, conv_b, w_r, b_r, w_i, b_i, lam, *, name):
    T = proj.shape[0]
    B, W = LRU_BLOCKS, LRU_BLOCK

    def body(xb_ref, yb_ref, cw_ref, cb_ref, wr_ref, br_ref, wi_ref, bi_ref, lam_ref, og_ref, hs_ref, a_s, u_s):
        xv, xc, r, i, sp, a, s = _lru_gates(xb_ref, cw_ref, cb_ref, wr_ref, br_ref, wi_ref, bi_ref, lam_ref)
        a_s[...] = a
        u_s[...] = s * (i * xc)
        _linear_scan(a_s, u_s, hs_ref, False)
        og_ref[...] = (hs_ref[...] * _gelu(yb_ref[...])).astype(og_ref.dtype)

    xb, yb, cw, vec, wg, bg = _lru_specs(T)
    return pl.pallas_call(
        body, name=name, grid=(B,), in_specs=[xb, yb, cw, vec, wg, bg, wg, bg, vec], out_specs=(xb, xb),
        out_shape=(jax.ShapeDtypeStruct((T, B * W), BF16), jax.ShapeDtypeStruct((T, B * W), F32)),
        scratch_shapes=[pltpu.VMEM((T, W), F32), pltpu.VMEM((T, W), F32)],
        compiler_params=_params(("arbitrary",)),
    )(proj, proj, conv_w, conv_b, w_r, b_r, w_i, b_i, lam)


def _lru_bwd(proj, conv_w, conv_b, w_r, b_r, w_i, b_i, lam, hs, dout, *, name):
    T = proj.shape[0]
    B, W = LRU_BLOCKS, LRU_BLOCK

    def csum(x):
        return jnp.sum(x, axis=0, keepdims=True)

    def body(xb_ref, yb_ref, cw_ref, cb_ref, wr_ref, br_ref, wi_ref, bi_ref, lam_ref, hs_ref, do_ref,
             dxb_ref, dyb_ref, dcw_ref, dcb_ref, dwr_ref, dbr_ref, dwi_ref, dbi_ref, dlam_ref, a_s, b_s, g_s):
        xv, xc, r, i, sp, a, s = _lru_gates(xb_ref, cw_ref, cb_ref, wr_ref, br_ref, wi_ref, bi_ref, lam_ref)
        h = hs_ref[...]
        yb = yb_ref[...]
        dout = do_ref[...].astype(F32)
        dyb_ref[...] = (dout * h * _dgelu(yb)).astype(dyb_ref.dtype)
        a_s[...] = _shift_up(a, 1)
        b_s[...] = dout * _gelu(yb)
        _linear_scan(a_s, b_s, g_s, True)
        g = g_s[...]
        da = g * _shift_down(h, 1)
        ds = g * (i * xc)
        di = g * s * xc
        dxc = g * s * i
        dla = da * a - ds * (a * a) / s
        dr = dla * (-LRU_C * sp)
        dlam_ref[...] = csum(dla * r) * (LRU_C * _sigmoid(-lam_ref[...]))
        dpr = dr * r * (1.0 - r)
        dpi = di * i * (1.0 - i)
        dxc = dxc + _bdot(dpr, wr_ref[0], NT) + _bdot(dpi, wi_ref[0], NT)
        dwr_ref[0] = _bdot(xc, dpr, TN)
        dwi_ref[0] = _bdot(xc, dpi, TN)
        dbr_ref[0] = csum(dpr)
        dbi_ref[0] = csum(dpi)
        dcb_ref[...] = csum(dxc)
        dxb_ref[...] = _conv_bwd(xv, cw_ref, dxc, dcw_ref).astype(dxb_ref.dtype)

    xb, yb, cw, vec, wg, bg = _lru_specs(T)
    act = jax.ShapeDtypeStruct((T, B * W), BF16)
    return pl.pallas_call(
        body, name=name, grid=(B,), in_specs=[xb, yb, cw, vec, wg, bg, wg, bg, vec, xb, xb],
        out_specs=(xb, xb, cw, vec, wg, bg, wg, bg, vec),
        out_shape=(act, act, jax.ShapeDtypeStruct((CONV_WIDTH, B * W), F32), jax.ShapeDtypeStruct((1, B * W), F32),
                   jax.ShapeDtypeStruct((B, W, W), F32), jax.ShapeDtypeStruct((B, 1, W), F32),
                   jax.ShapeDtypeStruct((B, W, W), F32), jax.ShapeDtypeStruct((B, 1, W), F32),
                   jax.ShapeDtypeStruct((1, B * W), F32)),
        scratch_shapes=[pltpu.VMEM((T, W), F32)] * 3,
        compiler_params=_params(("arbitrary",)),
    )(proj, proj, conv_w, conv_b, w_r, b_r, w_i, b_i, lam, hs, dout)


MESH = pl.DeviceIdType.MESH
N_CHIPS = 4
AG_COPIES = 7


def _mesh_pos():
    return lax.axis_index("x"), lax.axis_index("y"), lax.axis_index("c")


def _hbm_specs(n):
    return [pl.BlockSpec(memory_space=pltpu.HBM)] * n


def _all_gather(shards, *, name):
    n = len(shards)

    def body(*refs):
        xs, outs = refs[:n], refs[n:2 * n]
        send_sems, recv_sems, local_sems = refs[2 * n:]
        x, y, c = _mesh_pos()
        me, sibling = (x, y, c), (x, y, 1 - c)
        chips = [(1 - x, y), (x, 1 - y), (1 - x, 1 - y)]

        def rows(t, px, py, pc):
            return outs[t].at[4 * px + 2 * py + pc]

        def copy(t, k, block, to, src=None):
            return pltpu.make_async_remote_copy(
                src_ref=rows(t, *block) if src is None else src, dst_ref=rows(t, *block),
                send_sem=send_sems.at[t * AG_COPIES + k], recv_sem=recv_sems.at[t * AG_COPIES + k],
                device_id=to, device_id_type=MESH)

        mine = [pltpu.make_async_copy(xs[t], rows(t, *me), local_sems.at[t]) for t in range(n)]
        for cp in mine:
            cp.start()
        first = []
        for t in range(n):
            first.append(copy(t, 0, me, sibling, src=xs[t]))
            first += [copy(t, 1 + j, me, (*chip, c), src=xs[t]) for j, chip in enumerate(chips)]
        for cp in first:
            cp.start()
        passed = []
        for j, chip in enumerate(chips):
            for t in range(n):
                copy(t, 1 + j, (*chip, c), me).wait_recv()
                cp = copy(t, 4 + j, (*chip, c), sibling)
                cp.start()
                passed.append(cp)
        for t in range(n):
            copy(t, 0, sibling, me).wait_recv()
        for j, chip in enumerate(chips):
            for t in range(n):
                copy(t, 4 + j, (*chip, 1 - c), me).wait_recv()
        for cp in first + passed:
            cp.wait_send()
        for cp in mine:
            cp.wait()

    return pl.pallas_call(
        body, name=name,
        out_shape=[jax.ShapeDtypeStruct((N_DEV,) + s.shape, s.dtype) for s in shards],
        in_specs=_hbm_specs(n), out_specs=_hbm_specs(n),
        scratch_shapes=[pltpu.SemaphoreType.DMA((n * AG_COPIES,)), pltpu.SemaphoreType.DMA((n * AG_COPIES,)),
                        pltpu.SemaphoreType.DMA((n,))],
    )(*shards)


SIDE_EFFECT = pltpu.SideEffectType.DATAFLOW_SIDE_EFFECTING


def _copies(plan, refs, send_sems, recv_sems):
    return [pltpu.make_async_remote_copy(src_ref=src, dst_ref=dst, send_sem=send_sems.at[k], recv_sem=recv_sems.at[k],
                                         device_id=to, device_id_type=MESH)
            for k, (src, dst, to) in enumerate(plan(refs))]


def _split_start(bufs, plan, n_copies, *, name, deps=()):
    n = len(bufs)

    def body(*refs):
        send_sems, recv_sems = refs[n + len(deps)], refs[n + len(deps) + 1]
        token = refs[-1]
        for cp in _copies(plan, refs[:n], send_sems, recv_sems):
            cp.start()
        token[...] = jnp.zeros_like(token)

    hbm, sem = pl.BlockSpec(memory_space=pltpu.HBM), pl.BlockSpec(memory_space=pltpu.SEMAPHORE)
    out = pl.pallas_call(
        body, name=name,
        out_shape=(pltpu.SemaphoreType.DMA((n_copies,)), pltpu.SemaphoreType.DMA((n_copies,)),
                   *[pltpu.HBM(b.shape, b.dtype) for b in bufs], jax.ShapeDtypeStruct(TOKEN_SHAPE, F32)),
        in_specs=[hbm] * n + [pl.BlockSpec(memory_space=pl.ANY)] * len(deps),
        out_specs=(sem, sem, *[hbm] * n, pl.BlockSpec(memory_space=pltpu.VMEM)),
        input_output_aliases={i: 2 + i for i in range(n)},
        compiler_params=pltpu.CompilerParams(has_side_effects=SIDE_EFFECT),
    )(*[pltpu.with_memory_space_constraint(b, pltpu.HBM) for b in bufs], *deps)
    return out[0], out[1], list(out[2:2 + n]), out[-1]


def _split_wait(send_sems, recv_sems, bufs, plan, after, *, name):
    n = len(bufs)

    def body(*refs):
        for cp in _copies(plan, refs[:n], refs[n], refs[n + 1]):
            cp.wait_send()
            cp.wait_recv()

    hbm, sem = pl.BlockSpec(memory_space=pltpu.HBM), pl.BlockSpec(memory_space=pltpu.SEMAPHORE)
    out = pl.pallas_call(
        body, name=name, out_shape=tuple(pltpu.HBM(b.shape, b.dtype) for b in bufs),
        in_specs=[hbm] * n + [sem, sem, pl.BlockSpec(memory_space=pl.ANY)], out_specs=tuple([hbm] * n),
        input_output_aliases={i: i for i in range(n)},
        compiler_params=pltpu.CompilerParams(has_side_effects=SIDE_EFFECT),
    )(*bufs, send_sems, recv_sems, after)
    return list(out)


def _block(ref, in_cols, d):
    if not in_cols:
        return ref.at[d]
    c = ref.shape[1] // N_DEV
    return ref.at[:, pl.ds(pl.multiple_of(d * c, LANE), c)]


def _plan_gather_a(in_cols):
    n = len(in_cols)

    def plan(refs):
        x, y, c = _mesh_pos()
        me = 4 * x + 2 * y + c
        peers = [(x, y, 1 - c), (1 - x, y, c), (x, 1 - y, c), (1 - x, 1 - y, c)]
        return [(refs[t], _block(refs[n + t], in_cols[t], me), to) for t in range(n) for to in peers]
    return plan


def _plan_gather_b(in_cols):
    n = len(in_cols)

    def plan(refs):
        x, y, c = _mesh_pos()
        ds = [4 * px + 2 * py + c for px, py in [(1 - x, y), (x, 1 - y), (1 - x, 1 - y)]]
        return [(_block(refs[t], in_cols[t], d), _block(refs[t], in_cols[t], d), (x, y, 1 - c))
                for t in range(n) for d in ds]
    return plan


def _plan_scatter_pair(in_cols):
    n = len(in_cols)

    def plan(refs):
        x, y, c = _mesh_pos()
        return [(_block(refs[t], in_cols[t], 2 * q + (1 - c)), refs[n + t].at[q], (x, y, 1 - c))
                for t in range(n) for q in range(N_CHIPS)]
    return plan


def _plan_scatter_chips(n):
    def plan(refs):
        x, y, c = _mesh_pos()
        chips = [(1 - x, y), (x, 1 - y), (1 - x, 1 - y)]
        return [(refs[t].at[2 * px + py], refs[n + t].at[j], (px, py, c))
                for t in range(n) for j, (px, py) in enumerate(chips)]
    return plan


def _insert_block(land, shard, device, in_cols, *, name):
    r, c = shard.shape
    tr = _tile(r, (512, 256))

    def body(d_ref, s_ref, land_ref, o_ref):
        if in_cols:
            o_ref[...] = s_ref[...]
        else:
            o_ref[0] = s_ref[...]

    if in_cols:
        o_spec = pl.BlockSpec((tr, c), lambda i, d: (i, d[0]))
    else:
        o_spec = pl.BlockSpec((1, tr, c), lambda i, d: (d[0], i, 0))
    return pl.pallas_call(
        body, name=name,
        grid_spec=pltpu.PrefetchScalarGridSpec(
            num_scalar_prefetch=1, grid=(r // tr,),
            in_specs=[pl.BlockSpec((tr, c), lambda i, d: (i, 0)), pl.BlockSpec(memory_space=pl.ANY)],
            out_specs=o_spec),
        out_shape=jax.ShapeDtypeStruct(land.shape, land.dtype), input_output_aliases={2: 0},
        compiler_params=_params(("parallel",)),
    )(device, shard, land)


PAIR_ROWS = (512, 256)


def _pair_add(g, a, core, in_cols, *, name):
    _, R, C = a.shape
    tr = _tile(R, PAIR_ROWS)

    def body(c_ref, g_ref, a_ref, o_ref):
        gv = g_ref[...] if in_cols else g_ref[0]
        o_ref[0] = (gv.astype(F32) + a_ref[0].astype(F32)).astype(o_ref.dtype)

    blk = pl.BlockSpec((1, tr, C), lambda q, i, c: (q, i, 0))
    if in_cols:
        g_spec = pl.BlockSpec((tr, C), lambda q, i, c: (i, 2 * q + c[0]))
    else:
        g_spec = pl.BlockSpec((1, tr, C), lambda q, i, c: (2 * q + c[0], i, 0))
    return pl.pallas_call(
        body, name=name,
        grid_spec=pltpu.PrefetchScalarGridSpec(
            num_scalar_prefetch=1, grid=(N_CHIPS, R // tr), in_specs=[g_spec, blk], out_specs=blk),
        out_shape=jax.ShapeDtypeStruct((N_CHIPS, R, C), BF16),
        compiler_params=_params(("parallel", "parallel")),
    )(core, g, a)


ADAM_ROWS = 256


def _adamw_sharded(w, m, v, s4, b3, chip, layer, prev, *, name):
    L, R, C = w.shape
    tr = _tile(R, (ADAM_ROWS,))
    if prev is None and L > 1:
        prev = tuple(lax.empty(w.shape, F32) for _ in range(4))
    n_prev = 0 if prev is None else 4

    def body(q_ref, w_ref, m_ref, v_ref, s_ref, b_ref, *rest):
        g_out, d_out, m_out, v_out = rest[n_prev:]
        g = s_ref[0].astype(F32)
        for j in range(N_CHIPS - 1):
            g = g + b_ref[j].astype(F32)
        mn = ADAM_B1 * m_ref[0] + (1.0 - ADAM_B1) * g
        vn = ADAM_B2 * v_ref[0] + (1.0 - ADAM_B2) * (g * g)
        g_out[0] = g
        d_out[0] = -ADAM_LR * ((mn / ADAM_C1) / (jnp.sqrt(vn / ADAM_C2) + ADAM_EPS) + ADAM_WD * w_ref[0])
        m_out[0] = mn
        v_out[0] = vn

    blk = pl.BlockSpec((1, tr, C), lambda i, q: (layer, i, 0))
    sh = jax.ShapeDtypeStruct((L, R, C), F32)
    return pl.pallas_call(
        body, name=name,
        grid_spec=pltpu.PrefetchScalarGridSpec(
            num_scalar_prefetch=1, grid=(R // tr,),
            in_specs=[blk, blk, blk, pl.BlockSpec((1, tr, C), lambda i, q: (q[0], i, 0)),
                      pl.BlockSpec((N_CHIPS - 1, tr, C), lambda i, q: (0, i, 0))]
            + [pl.BlockSpec(memory_space=pl.ANY)] * n_prev,
            out_specs=(blk,) * 4),
        out_shape=(sh,) * 4, input_output_aliases={6 + k: k for k in range(n_prev)},
        compiler_params=_params(("parallel",)),
    )(chip, w, m, v, s4, b3, *(prev or ()))


FWD_NAMES = ['x', 'norm1', 'norm2', 'final_norm', 'ret_w_in', 'ret_gn_gain', 'ret_w_out', 'gdn_w_in', 'gdn_conv_w',
             'gdn_a_log', 'gdn_dt_bias', 'gdn_norm_gain', 'gdn_w_out', 'gla_w_in', 'gla_w_gate_up', 'gla_gate_bias',
             'gla_norm_gain', 'gla_w_out', 'lru_w_in', 'lru_conv_w', 'lru_conv_b', 'lru_w_rgate', 'lru_b_rgate',
             'lru_w_igate', 'lru_b_igate', 'lru_lambda', 'lru_w_out', 'mlp_w_up', 'mlp_w_down']
WEIGHT_NAMES = FWD_NAMES[1:]
ARG_NAMES = FWD_NAMES + ['loss_target'] + ['m_' + n for n in WEIGHT_NAMES] + ['v_' + n for n in WEIGHT_NAMES]

MIXER_IN = ('ret_w_in', 'gdn_w_in', 'gla_w_in', 'lru_w_in')
MIXER_OUT = ('ret_w_out', 'gdn_w_out', 'gla_w_out', 'lru_w_out')
BIG_NAMES = MIXER_IN + MIXER_OUT + ('mlp_w_up', 'mlp_w_down')
SMALL = {'norm1': False, 'norm2': False, 'final_norm': False, 'ret_gn_gain': True, 'gdn_conv_w': True,
         'gdn_a_log': False, 'gdn_dt_bias': False, 'gdn_norm_gain': False, 'gla_w_gate_up': True,
         'gla_gate_bias': True, 'gla_norm_gain': True, 'lru_conv_w': True, 'lru_conv_b': True,
         'lru_w_rgate': False, 'lru_b_rgate': False, 'lru_w_igate': False, 'lru_b_igate': False, 'lru_lambda': True}
SMALL_NAMES = tuple(n for n in WEIGHT_NAMES if n in SMALL)
MEDIUM = ('lru_w_rgate', 'lru_w_igate')
EARLY_SMALL = tuple(n for n in SMALL_NAMES if n.startswith(('gdn_', 'gla_', 'lru_')) and n not in MEDIUM)
LATE_SMALL = tuple(n for n in SMALL_NAMES if n not in EARLY_SMALL + MEDIUM)
GDN_TAIL = 2 * GDN_HEADS
GDN_MAIN = 4 * D_MODEL
GLA_MAIN = 3 * D_MODEL


PACK_ROWS = 256


PACK_TILE = SUBLANES * LANE


def _pack(arrs):
    rows = []
    for a in arrs:
        f = a.reshape(-1).astype(F32)
        rows.append(jnp.pad(f, (0, (-f.shape[0]) % PACK_TILE)).reshape(-1, LANE))
    fill = (-sum(r.shape[0] for r in rows)) % PACK_ROWS
    if fill:
        rows.append(jnp.zeros((fill, LANE), F32))
    return jnp.concatenate(rows, axis=0)


def _unpack(buf, shapes, lead=()):
    out, r0 = [], 0
    for s in shapes:
        n = int(np.prod(s))
        nr = -(-n // PACK_TILE) * SUBLANES
        blk = buf[..., r0:r0 + nr, :].reshape(lead + (nr * LANE,))[..., :n]
        out.append(blk.reshape(lead + tuple(s)))
        r0 += nr
    return out


def _full_cols(g):
    return jnp.transpose(g, (1, 0, 2)).reshape(g.shape[1], N_DEV * g.shape[2])


def _full_rows(g):
    return g.reshape(N_DEV * g.shape[1], g.shape[2])


def _blocks_cols(dw):
    r, c = dw.shape[0], dw.shape[1] // N_DEV
    return jnp.transpose(dw.reshape(r, N_DEV, c), (1, 0, 2))


def _blocks_rows(dw):
    return dw.reshape(N_DEV, dw.shape[0] // N_DEV, dw.shape[1])


def _pad_cols(a, n=LANE):
    return jnp.pad(a, ((0, 0), (0, n - a.shape[1])))


def _mixer_fwd(layer, hn, w_in, sm, tables, deps=()):
    tag = f"l{layer}"
    if layer == 0:
        proj = _matmul(hn, w_in, name=tag + "_in", deps=deps)
        gain = sm['ret_gn_gain'][0][:, None, :]
        og, st = _ret_fwd(proj, gain, tables, name=tag + "_ret_fwd")
        return og, dict(proj=proj, st=st, gain=gain)
    if layer == 1:
        w_main, w_tail = w_in, _pad_cols(w_in[:, GDN_MAIN:])
        proj = _matmul(hn, w_main, b_cols=GDN_MAIN, name=tag + "_in", deps=deps)
        gates = _matmul(hn, w_tail, name=tag + "_in_tail")
        conv_w = sm['gdn_conv_w'][0]
        act = _conv_silu_fwd(proj, conv_w, name=tag + "_conv")
        a_log = sm['gdn_a_log'].reshape(GDN_HEADS, 1, 1)
        dt_bias = sm['gdn_dt_bias'].reshape(GDN_HEADS, 1, 1)
        gain = sm['gdn_norm_gain']
        og, st, tm, sol = _gdn_fwd(act, proj, gates, a_log, dt_bias, gain, name=tag + "_gdn_fwd")
        return og, dict(proj=proj, gates=gates, act=act, st=st, solved=(tm, sol), conv_w=conv_w, a_log=a_log, dt_bias=dt_bias,
                        gain=gain, w_main=w_main, w_tail=w_tail)
    if layer == 2:
        w_main, w_tail = w_in, _pad_cols(w_in[:, GLA_MAIN:])
        proj = _matmul(hn, w_main, b_cols=GLA_MAIN, name=tag + "_in", deps=deps)
        glow = _matmul(hn, w_tail, name=tag + "_in_tail")
        wgu = jnp.pad(sm['gla_w_gate_up'][0], ((0, LANE - GLA_GATE_RANK), (0, 0)))
        glogit = _matmul(glow, wgu, name=tag + "_gate_up")
        bias = sm['gla_gate_bias']
        gain = sm['gla_norm_gain'][0][:, None, :]
        og, st = _gla_fwd(proj, glogit, bias, gain, name=tag + "_gla_fwd")
        return og, dict(proj=proj, glow=glow, glogit=glogit, wgu=wgu, bias=bias, gain=gain, st=st,
                        w_main=w_main, w_tail=w_tail)
    proj = _matmul(hn, w_in, name=tag + "_in", deps=deps)
    args = (proj, sm['lru_conv_w'][0], sm['lru_conv_b'], sm['lru_w_rgate'][0], sm['lru_b_rgate'][0][:, None, :],
            sm['lru_w_igate'][0], sm['lru_b_igate'][0][:, None, :], sm['lru_lambda'])
    og, hs = _lru_fwd(*args, name=tag + "_lru_fwd")
    return og, dict(args=args, hs=hs)


def _mixer_bwd(layer, hn, w_in, dog, sv, tables, on_dw):
    tag = f"l{layer}"
    if layer == 0:
        dq, dk, dv, dg, dgain = _ret_bwd(sv['proj'], sv['gain'], tables, sv['st'], dog, name=tag + "_ret_bwd")
        dproj = jnp.concatenate([dq, dk, dv, dg], axis=1)
        deps = on_dw(_matmul(hn, dproj, ta=True, out_dtype=BF16, name=tag + "_in_dw"))
        dhn = _matmul(dproj, w_in, tb=True, name=tag + "_in_dx", deps=deps)
        return dhn, {'ret_gn_gain': dgain[:, 0][None]}
    if layer == 1:
        dq, dk, dv, dz, dgates, dscal, dgain = _gdn_bwd(
            sv['act'], sv['proj'], sv['gates'], sv['a_log'], sv['dt_bias'], sv['gain'], sv['st'], sv['solved'], dog,
            name=tag + "_gdn_bwd")
        dact = jnp.concatenate([dq, dk, dv], axis=1)
        dqkv, dconv = _conv_silu_bwd(sv['proj'], sv['conv_w'], dact, name=tag + "_conv_bwd")
        dmain = jnp.concatenate([dqkv, dz], axis=1)
        T = dmain.shape[0]
        dtail = _pad_cols(jnp.transpose(dgates[:, :, :2], (1, 2, 0)).reshape(T, GDN_TAIL))
        dw_main = _matmul(hn, dmain, ta=True, out_dtype=BF16, name=tag + "_in_dw")
        dw_tail = _matmul(hn, dtail, ta=True, out_dtype=BF16, name=tag + "_in_tail_dw")
        deps = on_dw(jnp.concatenate([dw_main, dw_tail[:, :GDN_TAIL]], axis=1))
        dhn = _matmul(dmain, sv['w_main'], tb=True, b_cols=GDN_MAIN, name=tag + "_in_dx", deps=deps)
        dhn = _matmul(dtail, sv['w_tail'], tb=True, epi="add", extra=dhn, name=tag + "_in_tail_dx")
        small = {'gdn_conv_w': dconv[None], 'gdn_a_log': dscal[:, 0, 0][None], 'gdn_dt_bias': dscal[:, 0, 1][None],
                 'gdn_norm_gain': jnp.sum(dgain[:, 0], axis=0)[None]}
        return dhn, small
    if layer == 2:
        dq, dk, dv, dr, dgl, dgain = _gla_bwd(sv['proj'], sv['glogit'], sv['bias'], sv['gain'], sv['st'], dog,
                                              name=tag + "_gla_bwd")
        dmain = jnp.concatenate([dq, dk, dv, dr], axis=1)
        dglow = _matmul(dgl, sv['wgu'], tb=True, name=tag + "_gate_up_dx")
        dwgu = _matmul(sv['glow'], dgl, ta=True, name=tag + "_gate_up_dw")
        dbias = _colsum(dgl, name=tag + "_gate_bias")
        dw_main = _matmul(hn, dmain, ta=True, out_dtype=BF16, name=tag + "_in_dw")
        dw_tail = _matmul(hn, dglow, ta=True, out_dtype=BF16, name=tag + "_in_tail_dw")
        deps = on_dw(jnp.concatenate([dw_main, dw_tail[:, :GLA_GATE_RANK]], axis=1))
        dhn = _matmul(dmain, sv['w_main'], tb=True, b_cols=GLA_MAIN, name=tag + "_in_dx", deps=deps)
        dhn = _matmul(dglow, sv['w_tail'], tb=True, epi="add", extra=dhn, name=tag + "_in_tail_dx")
        small = {'gla_w_gate_up': dwgu[:GLA_GATE_RANK][None], 'gla_gate_bias': dbias,
                 'gla_norm_gain': dgain[:, 0][None]}
        return dhn, small
    dxb, dyb, dcw, dcb, dwr, dbr, dwi, dbi, dlam = _lru_bwd(*sv['args'], sv['hs'], dog, name=tag + "_lru_bwd")
    dproj = jnp.concatenate([dxb, dyb], axis=1)
    deps = on_dw(_matmul(hn, dproj, ta=True, out_dtype=BF16, name=tag + "_in_dw"))
    dhn = _matmul(dproj, w_in, tb=True, name=tag + "_in_dx", deps=deps)
    small = {'lru_conv_w': dcw[None], 'lru_conv_b': dcb, 'lru_w_rgate': dwr[None], 'lru_b_rgate': dbr[:, 0][None],
             'lru_w_igate': dwi[None], 'lru_b_igate': dbi[:, 0][None], 'lru_lambda': dlam}
    return dhn, small


def _step(*args):
    assert len(args) == len(ARG_NAMES)
    p = dict(zip(ARG_NAMES, args))
    xi, yi, ci = _mesh_pos()
    dev = 4 * xi + 2 * yi + ci
    device = dev.astype(jnp.int32).reshape(1)
    core = ci.astype(jnp.int32).reshape(1)
    chip = (2 * xi + yi).astype(jnp.int32).reshape(1)
    x = p['x'][0]
    target = p['loss_target'][0]
    T = x.shape[0]
    tables = _ret_tables(T)

    sharded_small = [n for n in SMALL_NAMES if SMALL[n]]
    gathered, = _all_gather([_pack([p[n] for n in sharded_small])], name="gather_small")
    def in_cols_of(w):
        return w.shape[1] % LANE == 0

    gathers = {}
    token = gathered
    for layer in range(DEPTH):
        w_in_shard = p[MIXER_IN[layer]][0]
        groups = {'a': [(w_in_shard, in_cols_of(w_in_shard), _full_cols)],
                  'b': [(p[MIXER_OUT[layer]][0], False, _full_rows)],
                  'c': [(p['mlp_w_up'][layer], True, None), (p['mlp_w_down'][layer], False, _full_rows)]}
        for key, members in groups.items():
            shards = [w.astype(BF16) for w, _, _ in members]
            in_cols = [ic for _, ic, _ in members]
            n = len(shards)
            lands = [_insert_block(lax.empty((s.shape[0], N_DEV * s.shape[1]) if ic else (N_DEV,) + s.shape, BF16),
                                   s, device, ic, name=f"own_l{layer}{key}{i}")
                     for i, (s, ic) in enumerate(zip(shards, in_cols))]
            send, recv, bufs, token = _split_start(shards + lands, _plan_gather_a(in_cols), 4 * n,
                                                   name=f"gather_a_start_l{layer}{key}", deps=(token,))
            gathers[layer, key] = dict(n=n, h=(send, recv, bufs), in_cols=in_cols, full_of=[f for _, _, f in members])

    def gather_forward(layer, key, after):
        g = gathers[layer, key]
        n = g['n']
        send, recv, bufs = g['h']
        bufs = _split_wait(send, recv, bufs, _plan_gather_a(g['in_cols']), after, name=f"gather_a_wait_l{layer}{key}")
        send, recv, lands, tok = _split_start(bufs[n:], _plan_gather_b(g['in_cols']), 3 * n,
                                              name=f"gather_b_start_l{layer}{key}")
        g['h'] = (send, recv, lands)
        return tok

    def gather_finish(layer, key, after):
        g = gathers[layer, key]
        send, recv, lands = g['h']
        lands = _split_wait(send, recv, lands, _plan_gather_b(g['in_cols']), after, name=f"gather_b_wait_l{layer}{key}")
        return [l if ic else full_of(l) for l, ic, full_of in zip(lands, g['in_cols'], g['full_of'])]

    w_in_next, = gather_finish(0, 'a', gather_forward(0, 'a', token))
    parts = _unpack(gathered, [p[n].shape for n in sharded_small], lead=(N_DEV,))
    sm = {n: p[n] for n in SMALL_NAMES if not SMALL[n]}
    for n, blk in zip(sharded_small, parts):
        full = jnp.moveaxis(blk, 0, -2)
        sm[n] = full.reshape(full.shape[:-2] + (N_DEV * full.shape[-1],))

    saved = []
    big = {}
    for layer in range(DEPTH):
        w_in = w_in_next
        tag = f"l{layer}"
        hn = _rmsnorm_fwd(x, sm['norm1'][layer][None], name=tag + "_norm1")
        og, sv = _mixer_fwd(layer, hn, w_in, sm, tables)
        tok_b = gather_forward(layer, 'b', og)
        tok_c = gather_forward(layer, 'c', tok_b)
        w_out, = gather_finish(layer, 'b', tok_c)
        x_mid = _matmul(og, w_out, epi="add", extra=x, name=tag + "_out")
        hn2 = _rmsnorm_fwd(x_mid, sm['norm2'][layer][None], name=tag + "_norm2")
        w_up, w_down = gather_finish(layer, 'c', hn2)
        big[layer] = (w_in, w_out, w_up, w_down)
        u, a = _matmul(hn2, w_up, epi="relu2", name=tag + "_up")
        deps = (gather_forward(layer + 1, 'a', u),) if layer + 1 < DEPTH else ()
        x_new = _matmul(a, w_down, epi="add", extra=x_mid, name=tag + "_down", deps=deps)
        if layer + 1 < DEPTH:
            w_in_next, = gather_finish(layer + 1, 'a', x_new)
        saved.append(dict(x=x, hn=hn, og=og, sv=sv, x_mid=x_mid, hn2=hn2, u=u, a=a))
        x = x_new
    dx, dxb, dfinal, loss_part = _final_loss_bwd(x, sm['final_norm'][None], target, name="final_loss_bwd")
    loss = lax.psum(loss_part[0, 0], ("x", "y", "c"))

    outs = {}
    small_grads = {'final_norm': dfinal[0]}
    dnorm1, dnorm2 = [None] * DEPTH, [None] * DEPTH

    def scatter_start(items, deps, tag):
        grads, in_cols = [], []
        for _, _, dw, on in items:
            ic = on == 'cols' and (dw.shape[1] // N_DEV) % LANE == 0
            in_cols.append(ic)
            grads.append(dw if ic else (_blocks_cols(dw) if on == 'cols' else _blocks_rows(dw)))
        n = len(grads)
        lands = [lax.empty((N_CHIPS, dw.shape[0], dw.shape[1] // N_DEV) if ic else (N_CHIPS,) + dw.shape[1:], BF16)
                 for dw, ic in zip(grads, in_cols)]
        send, recv, bufs, tok = _split_start(grads + lands, _plan_scatter_pair(in_cols), N_CHIPS * n,
                                             name=f"scatter_pair_start_{tag}", deps=deps)
        return dict(items=items, n=n, tag=tag, in_cols=in_cols, h=(send, recv, bufs)), tok

    def scatter_forward(g, after):
        n, tag = g['n'], g['tag']
        send, recv, bufs = g['h']
        bufs = _split_wait(send, recv, bufs, _plan_scatter_pair(g['in_cols']), after, name=f"scatter_pair_wait_{tag}")
        sums = [_pair_add(b, a_, core, ic, name=f"pair_add_{tag}_{i}")
                for i, (b, a_, ic) in enumerate(zip(bufs[:n], bufs[n:], g['in_cols']))]
        lands = [lax.empty((N_CHIPS - 1,) + s_.shape[1:], BF16) for s_ in sums]
        send, recv, bufs, tok = _split_start(sums + lands, _plan_scatter_chips(n), (N_CHIPS - 1) * n,
                                             name=f"scatter_chips_start_{tag}")
        g['h'] = (send, recv, bufs)
        return tok

    def scatter_finish(g, after):
        n, tag = g['n'], g['tag']
        send, recv, bufs = g['h']
        bufs = _split_wait(send, recv, bufs, _plan_scatter_chips(n), after, name=f"scatter_chips_wait_{tag}")
        for i, (wname, idx, _, _) in enumerate(g['items']):
            outs[wname] = _adamw_sharded(p[wname], p['m_' + wname], p['v_' + wname], bufs[i], bufs[n + i], chip,
                                         idx, outs.get(wname), name=f"adamw_{tag}_{i}")

    older = []
    next_deps = ()
    for layer in reversed(range(DEPTH)):
        w_in, w_out, w_up, w_down = big[layer]
        s = saved[layer]
        tag = f"l{layer}"
        du = _matmul(dxb, w_down, tb=True, epi="drelu2", extra=s['u'], out_dtype=BF16, name=tag + "_down_dx",
                     deps=next_deps)
        next_deps = ()
        dw_down = _matmul(s['a'], dxb, ta=True, out_dtype=BF16, name=tag + "_down_dw")
        dhn2 = _matmul(du, w_up, tb=True, name=tag + "_up_dx")
        dw_up = _matmul(s['hn2'], du, ta=True, out_dtype=BF16, name=tag + "_up_dw")
        mlp_group, tok = scatter_start([('mlp_w_up', layer, dw_up, 'cols'),
                                        ('mlp_w_down', layer, dw_down, 'rows')], (), f"mlp_l{layer}")
        dx, dxb, dn2 = _rmsnorm_bwd(s['x_mid'], sm['norm2'][layer][None], dhn2, dx, name=tag + "_norm2_bwd",
                                    deps=(tok,))
        dw_out = _matmul(s['og'], dxb, ta=True, out_dtype=BF16, name=tag + "_out_dw")
        tok = scatter_forward(mlp_group, dw_out)
        dog = _matmul(dxb, w_out, tb=True, name=tag + "_out_dx", deps=(tok,))
        started = []

        def on_dw(dw_in):
            group, tok_ = scatter_start([(MIXER_IN[layer], 0, dw_in, 'cols'),
                                         (MIXER_OUT[layer], 0, dw_out, 'rows')], (), f"mix_l{layer}")
            started.append(group)
            return (tok_,)

        dhn, sg = _mixer_bwd(layer, s['hn'], w_in, dog, s['sv'], tables, on_dw)
        mixer_group, = started
        small_grads.update(sg)
        if layer > 0:
            tok = scatter_forward(mixer_group, dhn)
            dx, dxb, dn1 = _rmsnorm_bwd(s['x'], sm['norm1'][layer][None], dhn, dx, name=tag + "_norm1_bwd", deps=(tok,))
        else:
            tok = gather_forward('small_grads', 'early', dhn)
            dx, dxb, dn1 = _rmsnorm_bwd(s['x'], sm['norm1'][layer][None], dhn, dx, name=tag + "_norm1_bwd", deps=(tok,))
        dnorm1[layer], dnorm2[layer] = dn1[0], dn2[0]
        for g in older:
            scatter_finish(g, dx)
        older = [mlp_group, mixer_group]
        if layer == 1:
            parts = [_pack([small_grads[n] for n in EARLY_SMALL])] + [small_grads[n].reshape(-1, LANE) for n in MEDIUM]
            lands = [_insert_block(lax.empty((N_DEV,) + s_.shape, F32), s_, device, False, name=f"own_small_grads{i}")
                     for i, s_ in enumerate(parts)]
            n_parts = len(parts)
            send, recv, bufs, tok = _split_start(parts + lands, _plan_gather_a([False] * n_parts), 4 * n_parts,
                                                 name="gather_a_start_small_grads", deps=(dx,))
            gathers['small_grads', 'early'] = dict(n=n_parts, h=(send, recv, bufs), in_cols=[False] * n_parts,
                                                   full_of=[lambda l: l] * n_parts)
            next_deps = (tok,)
    small_grads['norm1'] = jnp.stack(dnorm1)
    small_grads['norm2'] = jnp.stack(dnorm2)

    late_parts, = _all_gather([_pack([small_grads[n] for n in LATE_SMALL])], name="gather_small_grads")
    last_token = scatter_forward(mixer_group, late_parts)
    early_parts, *medium_parts = gather_finish('small_grads', 'early', last_token)
    early_sum = _sum_parts(early_parts, name="sum_small_grads_early", deps=(last_token,))
    late_sum = _sum_parts(late_parts, name="sum_small_grads_late")
    for g in older:
        if g is not mixer_group:
            scatter_finish(g, late_sum)
    done_first = []
    for n, part in zip(MEDIUM, medium_parts):
        g = _sum_parts(part, name=f"sum_{n}")
        res = _adamw(p[n].reshape(-1, LANE), p['m_' + n].reshape(-1, LANE), p['v_' + n].reshape(-1, LANE), [g],
                     name=f"adamw_{n}")
        outs[n] = tuple(r.reshape(p[n].shape) for r in res)
        done_first.append(res[0][:SUBLANES])
    by_name = dict(zip(EARLY_SMALL, _unpack(early_sum, [small_grads[n].shape for n in EARLY_SMALL])))
    by_name.update(zip(LATE_SMALL, _unpack(late_sum, [small_grads[n].shape for n in LATE_SMALL])))
    packed_names = EARLY_SMALL + LATE_SMALL
    local_g = []
    for n in packed_names:
        g = by_name[n]
        if SMALL[n]:
            width = p[n].shape[-1]
            g = lax.dynamic_slice_in_dim(g, dev * width, width, axis=g.ndim - 1)
        local_g.append(g.reshape(p[n].shape))
    res = _adamw(_pack([p[n] for n in packed_names]), _pack([p['m_' + n] for n in packed_names]),
                 _pack([p['v_' + n] for n in packed_names]), [_pack(local_g)], name="adamw_small")
    local_shapes = [p[n].shape for n in packed_names]
    unpacked = [_unpack(r, local_shapes) for r in res]
    for i, n in enumerate(packed_names):
        outs[n] = tuple(unpacked[k][i] for k in range(4))
    done_first.append(res[0][:SUBLANES])
    done_first += [outs[n][0][0, :SUBLANES, :LANE] for n in BIG_NAMES if n not in (MIXER_IN[0], MIXER_OUT[0])]
    scatter_finish(mixer_group, functools.reduce(jnp.add, done_first))

    result = [loss, dx[None]]
    for k in range(4):
        result += [outs[n][k] for n in WEIGHT_NAMES]
    return tuple(result)


def kernel(x, norm1, norm2, final_norm, ret_w_in, ret_gn_gain, ret_w_out, gdn_w_in, gdn_conv_w, gdn_a_log, gdn_dt_bias, gdn_norm_gain, gdn_w_out, gla_w_in, gla_w_gate_up, gla_gate_bias, gla_norm_gain, gla_w_out, lru_w_in, lru_conv_w, lru_conv_b, lru_w_rgate, lru_b_rgate, lru_w_igate, lru_b_igate, lru_lambda, lru_w_out, mlp_w_up, mlp_w_down, loss_target, m_norm1, m_norm2, m_final_norm, m_ret_w_in, m_ret_gn_gain, m_ret_w_out, m_gdn_w_in, m_gdn_conv_w, m_gdn_a_log, m_gdn_dt_bias, m_gdn_norm_gain, m_gdn_w_out, m_gla_w_in, m_gla_w_gate_up, m_gla_gate_bias, m_gla_norm_gain, m_gla_w_out, m_lru_w_in, m_lru_conv_w, m_lru_conv_b, m_lru_w_rgate, m_lru_b_rgate, m_lru_w_igate, m_lru_b_igate, m_lru_lambda, m_lru_w_out, m_mlp_w_up, m_mlp_w_down, v_norm1, v_norm2, v_final_norm, v_ret_w_in, v_ret_gn_gain, v_ret_w_out, v_gdn_w_in, v_gdn_conv_w, v_gdn_a_log, v_gdn_dt_bias, v_gdn_norm_gain, v_gdn_w_out, v_gla_w_in, v_gla_w_gate_up, v_gla_gate_bias, v_gla_norm_gain, v_gla_w_out, v_lru_w_in, v_lru_conv_w, v_lru_conv_b, v_lru_w_rgate, v_lru_b_rgate, v_lru_w_igate, v_lru_b_igate, v_lru_lambda, v_lru_w_out, v_mlp_w_up, v_mlp_w_down):
    return _step(x, norm1, norm2, final_norm, ret_w_in, ret_gn_gain, ret_w_out, gdn_w_in, gdn_conv_w, gdn_a_log, gdn_dt_bias, gdn_norm_gain, gdn_w_out, gla_w_in, gla_w_gate_up, gla_gate_bias, gla_norm_gain, gla_w_out, lru_w_in, lru_conv_w, lru_conv_b, lru_w_rgate, lru_b_rgate, lru_w_igate, lru_b_igate, lru_lambda, lru_w_out, mlp_w_up, mlp_w_down, loss_target, m_norm1, m_norm2, m_final_norm, m_ret_w_in, m_ret_gn_gain, m_ret_w_out, m_gdn_w_in, m_gdn_conv_w, m_gdn_a_log, m_gdn_dt_bias, m_gdn_norm_gain, m_gdn_w_out, m_gla_w_in, m_gla_w_gate_up, m_gla_gate_bias, m_gla_norm_gain, m_gla_w_out, m_lru_w_in, m_lru_conv_w, m_lru_conv_b, m_lru_w_rgate, m_lru_b_rgate, m_lru_w_igate, m_lru_b_igate, m_lru_lambda, m_lru_w_out, m_mlp_w_up, m_mlp_w_down, v_norm1, v_norm2, v_final_norm, v_ret_w_in, v_ret_gn_gain, v_ret_w_out, v_gdn_w_in, v_gdn_conv_w, v_gdn_a_log, v_gdn_dt_bias, v_gdn_norm_gain, v_gdn_w_out, v_gla_w_in, v_gla_w_gate_up, v_gla_gate_bias, v_gla_norm_gain, v_gla_w_out, v_lru_w_in, v_lru_conv_w, v_lru_conv_b, v_lru_w_rgate, v_lru_b_rgate, v_lru_w_igate, v_lru_b_igate, v_lru_lambda, v_lru_w_out, v_mlp_w_up, v_mlp_w_down)
```

```python
import functools
import math

import numpy as np
import jax
import jax.numpy as jnp
from jax import lax
from jax.experimental import pallas as pl
from jax.experimental.pallas import tpu as pltpu

F32 = jnp.float32
BF16 = jnp.bfloat16

D_MODEL = 2048
DEPTH = 4
CHUNK = 64
D_FF = 4 * D_MODEL
NORM_EPS = 1e-6
N_DEV = 8

RET_HEADS, RET_DK, RET_DV = 8, 256, 512
RET_HB = 4
GDN_HEADS, GDN_DK, GDN_DV = 16, 128, 128
GDN_HB = 8
GDN_QKV = GDN_HEADS * (2 * GDN_DK + GDN_DV)
CONV_WIDTH = 4
GLA_HEADS, GLA_DK, GLA_DV = 4, 256, 512
GLA_HB = 4
GLA_GATE_RANK = 16
GLA_TAU = 16.0
LRU_WIDTH, LRU_BLOCKS, LRU_BLOCK = 2048, 16, 128
LRU_C = 8.0
ROPE_BASE = 10000.0

ADAM_LR, ADAM_B1, ADAM_B2, ADAM_EPS, ADAM_WD, ADAM_STEP = 0.001, 0.9, 0.999, 1e-08, 0.01, 10

LANE = 128
VMEM_LIMIT = 48 * 1024 * 1024

NN = (((1,), (0,)), ((), ()))
NT = (((1,), (1,)), ((), ()))
TN = (((0,), (0,)), ((), ()))


def _params(sem=None):
    return pltpu.CompilerParams(dimension_semantics=sem, vmem_limit_bytes=VMEM_LIMIT)


def _bdot(a, b, dn=NN):
    return lax.dot_general(a.astype(BF16), b.astype(BF16), dn, preferred_element_type=F32)


def _split(x):
    hi = x.astype(BF16)
    lo = (x - hi.astype(F32)).astype(BF16)
    return hi, lo


def _fdot(a, b, dn=NN):
    a1, a2 = _split(a)
    b1, b2 = _split(b)
    d = functools.partial(lax.dot_general, dimension_numbers=dn, preferred_element_type=F32)
    return d(a1, b1) + (d(a1, b2) + d(a2, b1))


def _sigmoid(x):
    return 1.0 / (1.0 + jnp.exp(-x))


def _softplus(x):
    return jnp.maximum(x, 0.0) + jnp.log(1.0 + jnp.exp(-jnp.abs(x)))


def _silu(x):
    return x * _sigmoid(x)


def _dsilu(x):
    s = _sigmoid(x)
    return s * (1.0 + x * (1.0 - s))


GELU_C = math.sqrt(2.0 / math.pi)


def _gelu(x):
    return 0.5 * x * (1.0 + jnp.tanh(GELU_C * (x + 0.044715 * x * x * x)))


def _dgelu(x):
    t = jnp.tanh(GELU_C * (x + 0.044715 * x * x * x))
    return 0.5 * (1.0 + t) + 0.5 * x * (1.0 - t * t) * GELU_C * (1.0 + 3.0 * 0.044715 * x * x)


def _expm1(x):
    poly = x * (1.0 + x * 0.5 * (1.0 + x * (1.0 / 3.0) * (1.0 + x * 0.25 * (1.0 + x * 0.2))))
    return jnp.where(jnp.abs(x) < 0.05, poly, jnp.exp(x) - 1.0)


def _iota2(shape, axis):
    return lax.broadcasted_iota(jnp.int32, shape, axis)


def _col_to_row(col):
    n = col.shape[0]
    eye = _iota2((n, n), 0) == _iota2((n, n), 1)
    return jnp.sum(jnp.where(eye, col, 0.0), axis=0, keepdims=True)


def _row_to_col(row):
    n = row.shape[1]
    eye = _iota2((n, n), 0) == _iota2((n, n), 1)
    return jnp.sum(jnp.where(eye, row, 0.0), axis=1, keepdims=True)


def _pick_row(x, r):
    rows = _iota2(x.shape, 0)
    return jnp.sum(jnp.where(rows == r, x, 0.0), axis=0, keepdims=True)


def _shift_down(x, s):
    if s == 0:
        return x
    y = pltpu.roll(x, s, 0)
    return jnp.where(_iota2(x.shape, 0) < s, 0.0, y)


def _shift_up(x, s):
    if s == 0:
        return x
    n = x.shape[0]
    y = pltpu.roll(x, n - s, 0)
    return jnp.where(_iota2(x.shape, 0) >= n - s, 0.0, y)


def _tile(dim, prefs):
    for p in prefs:
        if dim % p == 0:
            return p
    return dim


TOKEN_SHAPE = (8, LANE)


def _dep_specs(deps):
    return [pl.BlockSpec(TOKEN_SHAPE, lambda *_: (0, 0)) for _ in deps]


def _matmul(a, b, *, ta=False, tb=False, epi="none", extra=None, out_dtype=F32, name, deps=(), b_cols=None,
            gain=None):
    if ta:
        K, M = a.shape
    else:
        M, K = a.shape
    if tb:
        N, K2 = b.shape[0], b_cols or b.shape[1]
    else:
        K2, N = b.shape[0], b_cols or b.shape[1]
    assert K == K2, (a.shape, b.shape, ta, tb)
    if K <= 2048:
        tk = K
        if N <= 2048:
            tm, tn = _tile(M, (512, 256, 128)), N
        else:
            tm, tn = _tile(M, (1024, 512, 256, 128)), _tile(N, (512, 256, 128))
    else:
        tm, tn, tk = (_tile(d, (1024, 512, 256, 128)) for d in (M, N, K))
    nk = K // tk
    dn = (((0 if ta else 1,), (1 if tb else 0,)), ((), ()))
    n_extra = 0 if extra is None else 1
    n_gain = 0 if gain is None else 1
    n_out = 2 if epi in ("relu2", "add_norm") else 1
    assert epi != "add_norm" or (tn == N and n_gain and n_extra), "the fused norm needs whole rows"
    n_in = 2 + n_extra + n_gain + len(deps)

    def body(*refs):
        a_ref, b_ref = refs[0], refs[1]
        e_ref = refs[2] if n_extra else None
        g_ref = refs[2 + n_extra] if n_gain else None
        outs = refs[n_in:n_in + n_out]

        def finish(r):
            if epi == "none":
                outs[0][...] = r.astype(outs[0].dtype)
            elif epi == "add":
                outs[0][...] = (r + e_ref[...]).astype(outs[0].dtype)
            elif epi == "add_norm":
                y = r + e_ref[...]
                outs[0][...] = y
                rs = lax.rsqrt(jnp.mean(y * y, axis=-1, keepdims=True) + NORM_EPS)
                outs[1][...] = (y * rs * g_ref[...]).astype(outs[1].dtype)
            elif epi == "relu2":
                outs[0][...] = r.astype(outs[0].dtype)
                p = jnp.maximum(r, 0.0)
                outs[1][...] = (p * p).astype(outs[1].dtype)
            elif epi == "drelu2":
                outs[0][...] = (r * 2.0 * jnp.maximum(e_ref[...], 0.0)).astype(outs[0].dtype)

        def product():
            return lax.dot_general(a_ref[...].astype(BF16), b_ref[...].astype(BF16), dn, preferred_element_type=F32)

        if nk == 1:
            finish(product())
            return
        acc = refs[-1]
        k = pl.program_id(2)

        @pl.when(k == 0)
        def _():
            acc[...] = jnp.zeros_like(acc)

        acc[...] += product()

        @pl.when(k == nk - 1)
        def _():
            finish(acc[...])

    a_spec = pl.BlockSpec((tk, tm), lambda i, j, k: (k, i)) if ta else pl.BlockSpec((tm, tk), lambda i, j, k: (i, k))
    b_spec = pl.BlockSpec((tn, tk), lambda i, j, k: (j, k)) if tb else pl.BlockSpec((tk, tn), lambda i, j, k: (k, j))
    o_spec = pl.BlockSpec((tm, tn), lambda i, j, k: (i, j))
    g_specs = [pl.BlockSpec((1, tn), lambda i, j, k: (0, j))] if n_gain else []
    in_specs = [a_spec, b_spec] + ([o_spec] if n_extra else []) + g_specs + _dep_specs(deps)
    if epi == "relu2":
        out_shape = (jax.ShapeDtypeStruct((M, N), BF16), jax.ShapeDtypeStruct((M, N), BF16))
        out_specs = (o_spec, o_spec)
    elif epi == "add_norm":
        out_shape = (jax.ShapeDtypeStruct((M, N), F32), jax.ShapeDtypeStruct((M, N), BF16))
        out_specs = (o_spec, o_spec)
    else:
        out_shape = jax.ShapeDtypeStruct((M, N), out_dtype)
        out_specs = o_spec
    args = (a, b) + ((extra,) if n_extra else ()) + ((gain,) if n_gain else ()) + tuple(deps)
    return pl.pallas_call(
        body, name=name, grid=(M // tm, N // tn, nk), in_specs=in_specs, out_specs=out_specs,
        out_shape=out_shape, scratch_shapes=[pltpu.VMEM((tm, tn), F32)] if nk > 1 else [],
        compiler_params=_params(("parallel", "parallel", "arbitrary")),
    )(*args)


ROW_BLOCK = 256


def _rmsnorm_fwd(x, g, *, name, deps=()):
    T, D = x.shape
    tr = _tile(T, (ROW_BLOCK, 128, 64))

    def body(x_ref, g_ref, *rest):
        o_ref = rest[-1]
        xv = x_ref[...]
        r = lax.rsqrt(jnp.mean(xv * xv, axis=-1, keepdims=True) + NORM_EPS)
        o_ref[...] = (xv * r * g_ref[...]).astype(o_ref.dtype)

    return pl.pallas_call(
        body, name=name, grid=(T // tr,),
        in_specs=[pl.BlockSpec((tr, D), lambda i: (i, 0)), pl.BlockSpec((1, D), lambda i: (0, 0))] + _dep_specs(deps),
        out_specs=pl.BlockSpec((tr, D), lambda i: (i, 0)),
        out_shape=jax.ShapeDtypeStruct((T, D), BF16), compiler_params=_params(("parallel",)),
    )(x, g, *deps)


def _rmsnorm_bwd(x, g, dy, dres, *, name, deps=()):
    T, D = x.shape
    tr = _tile(T, (ROW_BLOCK, 128, 64))

    def body(x_ref, g_ref, dy_ref, dres_ref, *rest):
        dx_ref, dxb_ref, dg_ref = rest[len(deps):]
        i = pl.program_id(0)
        xv = x_ref[...]
        r = lax.rsqrt(jnp.mean(xv * xv, axis=-1, keepdims=True) + NORM_EPS)
        xh = xv * r
        dyv = dy_ref[...].astype(F32)
        dxh = dyv * g_ref[...]
        dx = dres_ref[...] + r * (dxh - xh * jnp.mean(dxh * xh, axis=-1, keepdims=True))
        dx_ref[...] = dx
        dxb_ref[...] = dx.astype(dxb_ref.dtype)

        @pl.when(i == 0)
        def _():
            dg_ref[...] = jnp.zeros_like(dg_ref)

        dg_ref[...] += jnp.sum(dyv * xh, axis=0, keepdims=True)

    blk = pl.BlockSpec((tr, D), lambda i: (i, 0))
    vec = pl.BlockSpec((1, D), lambda i: (0, 0))
    return pl.pallas_call(
        body, name=name, grid=(T // tr,), in_specs=[blk, vec, blk, blk] + _dep_specs(deps), out_specs=(blk, blk, vec),
        out_shape=(jax.ShapeDtypeStruct((T, D), F32), jax.ShapeDtypeStruct((T, D), BF16),
                   jax.ShapeDtypeStruct((1, D), F32)),
        compiler_params=_params(("arbitrary",)),
    )(x, g, dy, dres, *deps)


def _final_loss_bwd(x, g, target, *, name):
    T, D = x.shape
    tr = _tile(T, (ROW_BLOCK, 128, 64))

    def body(x_ref, g_ref, t_ref, dx_ref, dxb_ref, dg_ref, l_ref):
        i = pl.program_id(0)
        xv = x_ref[...]
        r = lax.rsqrt(jnp.mean(xv * xv, axis=-1, keepdims=True) + NORM_EPS)
        xh = xv * r
        gv = g_ref[...]
        err = xh * gv - t_ref[...]
        dy = err * (1.0 / D)
        dxh = dy * gv
        dx = r * (dxh - xh * jnp.mean(dxh * xh, axis=-1, keepdims=True))
        dx_ref[...] = dx
        dxb_ref[...] = dx.astype(dxb_ref.dtype)

        @pl.when(i == 0)
        def _():
            l_ref[...] = jnp.zeros_like(l_ref)
            dg_ref[...] = jnp.zeros_like(dg_ref)

        dg_ref[...] += jnp.sum(dy * xh, axis=0, keepdims=True)
        part = 0.5 * jnp.sum(jnp.mean(err * err, axis=-1, keepdims=True), axis=0, keepdims=True)
        l_ref[...] += jnp.broadcast_to(part, l_ref.shape)

    blk = pl.BlockSpec((tr, D), lambda i: (i, 0))
    vec = pl.BlockSpec((1, D), lambda i: (0, 0))
    return pl.pallas_call(
        body, name=name, grid=(T // tr,), in_specs=[blk, vec, blk],
        out_specs=(blk, blk, vec, pl.BlockSpec((1, LANE), lambda i: (0, 0))),
        out_shape=(jax.ShapeDtypeStruct((T, D), F32), jax.ShapeDtypeStruct((T, D), BF16),
                   jax.ShapeDtypeStruct((1, D), F32), jax.ShapeDtypeStruct((1, LANE), F32)),
        compiler_params=_params(("arbitrary",)),
    )(x, g, target)


def _colsum(x, *, name):
    T, C = x.shape
    tc = _tile(C, (512, 256, 128))

    def body(x_ref, o_ref):
        o_ref[...] = jnp.sum(x_ref[...], axis=0, keepdims=True)

    return pl.pallas_call(
        body, name=name, grid=(C // tc,), in_specs=[pl.BlockSpec((T, tc), lambda j: (0, j))],
        out_specs=pl.BlockSpec((1, tc), lambda j: (0, j)),
        out_shape=jax.ShapeDtypeStruct((1, C), F32), compiler_params=_params(("parallel",)),
    )(x)


ADAM_C1 = 1.0 - ADAM_B1 ** ADAM_STEP
ADAM_C2 = 1.0 - ADAM_B2 ** ADAM_STEP


def _adamw(w, m, v, grads, *, name):
    R, C = w.shape
    tr = _tile(R, (256, 128, 64, 32, 16, 8))
    n_g = len(grads)

    def body(*refs):
        w_ref, m_ref, v_ref = refs[:3]
        g_refs = refs[3:3 + n_g]
        g_out, d_out, m_out, v_out = refs[3 + n_g:]
        g = g_refs[0][...].astype(F32)
        for r in g_refs[1:]:
            g = g + r[...].astype(F32)
        mn = ADAM_B1 * m_ref[...] + (1.0 - ADAM_B1) * g
        vn = ADAM_B2 * v_ref[...] + (1.0 - ADAM_B2) * (g * g)
        m_hat = mn / ADAM_C1
        v_hat = vn / ADAM_C2
        g_out[...] = g
        d_out[...] = -ADAM_LR * (m_hat / (jnp.sqrt(v_hat) + ADAM_EPS) + ADAM_WD * w_ref[...])
        m_out[...] = mn
        v_out[...] = vn

    blk = pl.BlockSpec((tr, C), lambda i: (i, 0))
    sh = jax.ShapeDtypeStruct((R, C), F32)
    return pl.pallas_call(
        body, name=name, grid=(R // tr,), in_specs=[blk] * (3 + n_g), out_specs=(blk,) * 4,
        out_shape=(sh,) * 4, compiler_params=_params(("parallel",)),
    )(w, m, v, *grads)


def _sum_parts(parts, *, name, deps=()):
    P, R, C = parts.shape
    tr = _tile(R, (256, 128, 64, 32, 16, 8))

    def body(p_ref, *rest):
        o_ref = rest[-1]
        s = p_ref[0].astype(F32)
        for i in range(1, P):
            s = s + p_ref[i].astype(F32)
        o_ref[...] = s

    return pl.pallas_call(
        body, name=name, grid=(R // tr,),
        in_specs=[pl.BlockSpec((P, tr, C), lambda i: (0, i, 0))] + _dep_specs(deps),
        out_specs=pl.BlockSpec((tr, C), lambda i: (i, 0)),
        out_shape=jax.ShapeDtypeStruct((R, C), F32), compiler_params=_params(("parallel",)),
    )(parts, *deps)


def _ret_tables(T):
    H, C = RET_HEADS, CHUNK
    log_gamma = jnp.log1p(-jnp.exp2(-5.0 - jnp.arange(H, dtype=F32)))
    pos = jnp.arange(C, dtype=F32)
    dist = jnp.abs(pos[:, None] - pos[None, :])
    dm = jnp.exp(log_gamma[:, None, None] * dist)
    qdec = jnp.exp(log_gamma[:, None] * (pos + 1.0))[:, :, None]
    kdec = jnp.exp(log_gamma[:, None] * (C - 1.0 - pos))[:, :, None]
    cdec = jnp.exp(log_gamma * C)[:, None, None]
    inv = ROPE_BASE ** (-jnp.arange(0, RET_DK, 2, dtype=F32) / RET_DK)
    ang = jnp.arange(T, dtype=F32)[:, None] * inv[None, :]
    return dm, qdec, kdec, cdec, jnp.cos(ang), jnp.sin(ang)


def _rot(x, cos, sin):
    h = x.shape[1] // 2
    x1, x2 = x[:, :h], x[:, h:]
    return jnp.concatenate([x1 * cos - x2 * sin, x1 * sin + x2 * cos], axis=1)


def _unrot(dy, cos, sin):
    h = dy.shape[1] // 2
    d1, d2 = dy[:, :h], dy[:, h:]
    return jnp.concatenate([d1 * cos + d2 * sin, d2 * cos - d1 * sin], axis=1)


def _ret_specs(N, rev):
    H, C, DK, DV = RET_HEADS, CHUNK, RET_DK, RET_DV
    cn = (lambda n: N - 1 - n) if rev else (lambda n: n)
    HB, G = RET_HB, H // RET_HB
    q = pl.BlockSpec((C, HB * DK), lambda h, n: (cn(n), h))
    k = pl.BlockSpec((C, HB * DK), lambda h, n: (cn(n), G + h))
    v = pl.BlockSpec((C, HB * DV), lambda h, n: (cn(n), G + h))
    g = pl.BlockSpec((C, HB * DV), lambda h, n: (cn(n), 2 * G + h))
    cs = pl.BlockSpec((C, DK // 2), lambda h, n: (cn(n), 0))
    dm = pl.BlockSpec((HB, C, C), lambda h, n: (h, 0, 0))
    dec = pl.BlockSpec((HB, C, 1), lambda h, n: (h, 0, 0))
    cd = pl.BlockSpec((HB, 1, 1), lambda h, n: (h, 0, 0))
    gain = pl.BlockSpec((HB, 1, DV), lambda h, n: (h, 0, 0))
    st = pl.BlockSpec((HB, 1, DK, DV), lambda h, n: (h, cn(n), 0, 0))
    ov = pl.BlockSpec((C, HB * DV), lambda h, n: (cn(n), h))
    return q, k, v, g, cs, dm, dec, cd, gain, st, ov


def _ret_fwd(proj, gain, tables, *, name):
    T = proj.shape[0]
    H, C, DK, DV = RET_HEADS, CHUNK, RET_DK, RET_DV
    N = T // C
    dm_t, qdec_t, kdec_t, cdec_t, cos_t, sin_t = tables

    def body(q_ref, k_ref, v_ref, g_ref, cos_ref, sin_ref, dm_ref, qd_ref, kd_ref, cd_ref, gain_ref,
             og_ref, st_ref, s_acc):
        n = pl.program_id(1)

        @pl.when(n == 0)
        def _():
            s_acc[...] = jnp.zeros_like(s_acc)

        cos, sin = cos_ref[...], sin_ref[...]

        def head(hb):
            kc, vc = pl.ds(hb * DK, DK), pl.ds(hb * DV, DV)
            qr = _rot(q_ref[:, kc], cos, sin)
            kr = _rot(k_ref[:, kc], cos, sin) * (DK ** -0.5)
            v = v_ref[:, vc]
            sp = s_acc[hb]
            st_ref[hb, 0] = sp.astype(st_ref.dtype)
            scores = _bdot(qr, kr, NT) * dm_ref[hb]
            inter = _bdot(qr * qd_ref[hb], sp)
            s_acc[hb] = sp * cd_ref[hb] + _bdot(kr * kd_ref[hb], v, TN)
            yield
            o = _bdot(scores, v) + inter
            yield
            oc = o - jnp.mean(o, axis=-1, keepdims=True)
            rstd = lax.rsqrt(jnp.mean(oc * oc, axis=-1, keepdims=True) + NORM_EPS)
            og_ref[:, vc] = (oc * rstd * gain_ref[hb] * _silu(g_ref[:, vc])).astype(og_ref.dtype)

        _interleave([head(hb) for hb in range(RET_HB)])

    q, k, v, g, cs, dm, dec, cd, gn, st, ov = _ret_specs(N, False)
    return pl.pallas_call(
        body, name=name, grid=(H // RET_HB, N),
        in_specs=[q, k, v, g, cs, cs, dm, dec, dec, cd, gn], out_specs=(ov, st),
        out_shape=(jax.ShapeDtypeStruct((T, H * DV), BF16), jax.ShapeDtypeStruct((H, N, DK, DV), BF16)),
        scratch_shapes=[pltpu.VMEM((RET_HB, DK, DV), F32)],
        compiler_params=_params(("arbitrary", "arbitrary")),
    )(proj, proj, proj, proj, cos_t, sin_t, dm_t, qdec_t, kdec_t, cdec_t, gain)


def _ret_bwd(proj, gain, tables, states, dog, *, name):
    T = proj.shape[0]
    H, C, DK, DV = RET_HEADS, CHUNK, RET_DK, RET_DV
    N = T // C
    dm_t, qdec_t, kdec_t, cdec_t, cos_t, sin_t = tables

    def body(q_ref, k_ref, v_ref, g_ref, cos_ref, sin_ref, dm_ref, qd_ref, kd_ref, cd_ref, gain_ref,
             st_ref, dog_ref, dq_ref, dk_ref, dv_ref, dg_ref, dgain_ref, ds_acc):
        n = pl.program_id(1)

        @pl.when(n == 0)
        def _():
            ds_acc[...] = jnp.zeros_like(ds_acc)
            dgain_ref[...] = jnp.zeros_like(dgain_ref)

        cos, sin = cos_ref[...], sin_ref[...]
        scale = DK ** -0.5

        def head(hb):
            kc, vc = pl.ds(hb * DK, DK), pl.ds(hb * DV, DV)
            qr = _rot(q_ref[:, kc], cos, sin)
            kr = _rot(k_ref[:, kc], cos, sin) * scale
            v = v_ref[:, vc]
            g = g_ref[:, vc]
            sp = st_ref[hb, 0]
            dm = dm_ref[hb]
            qd, kd = qd_ref[hb], kd_ref[hb]
            gain_v = gain_ref[hb]
            scores = _bdot(qr, kr, NT) * dm
            inter = _bdot(qr * qd, sp)
            yield
            o = _bdot(scores, v) + inter
            yield
            oc = o - jnp.mean(o, axis=-1, keepdims=True)
            rstd = lax.rsqrt(jnp.mean(oc * oc, axis=-1, keepdims=True) + NORM_EPS)
            oh = oc * rstd
            dy = dog_ref[:, vc].astype(F32)
            dg_ref[:, vc] = (dy * oh * gain_v * _dsilu(g)).astype(dg_ref.dtype)
            dnorm = dy * _silu(g)
            dgain_ref[hb] += jnp.sum(dnorm * oh, axis=0, keepdims=True)
            doh = dnorm * gain_v
            do = rstd * (doh - jnp.mean(doh, axis=-1, keepdims=True)
                         - oh * jnp.mean(doh * oh, axis=-1, keepdims=True))
            dsn = ds_acc[hb]
            dp = _bdot(do, v, NT) * dm
            dq_inter = _bdot(do, sp, NT) * qd
            dk_inter = _bdot(v, dsn, NT) * kd
            dv_ref[:, vc] = (_bdot(scores, do, TN) + _bdot(kr * kd, dsn)).astype(dv_ref.dtype)
            ds_acc[hb] = dsn * cd_ref[hb] + _bdot(qr * qd, do, TN)
            yield
            dqr = _bdot(dp, kr) + dq_inter
            dkr = _bdot(dp, qr, TN) + dk_inter
            yield
            dq_ref[:, kc] = _unrot(dqr, cos, sin).astype(dq_ref.dtype)
            dk_ref[:, kc] = _unrot(dkr * scale, cos, sin).astype(dk_ref.dtype)

        _interleave([head(hb) for hb in range(RET_HB)])

    q, k, v, g, cs, dm, dec, cd, gn, st, ov = _ret_specs(N, True)
    return pl.pallas_call(
        body, name=name, grid=(H // RET_HB, N),
        in_specs=[q, k, v, g, cs, cs, dm, dec, dec, cd, gn, st, ov],
        out_specs=(q, q, ov, ov, gn),
        out_shape=(jax.ShapeDtypeStruct((T, H * DK), BF16), jax.ShapeDtypeStruct((T, H * DK), BF16),
                   jax.ShapeDtypeStruct((T, H * DV), BF16), jax.ShapeDtypeStruct((T, H * DV), BF16),
                   jax.ShapeDtypeStruct((H, 1, DV), F32)),
        scratch_shapes=[pltpu.VMEM((RET_HB, DK, DV), F32)],
        compiler_params=_params(("arbitrary", "arbitrary")),
    )(proj, proj, proj, proj, cos_t, sin_t, dm_t, qdec_t, kdec_t, cdec_t, gain, states, dog)


def _gla_specs(N, rev):
    H, C, DK, DV = GLA_HEADS, CHUNK, GLA_DK, GLA_DV
    cn = (lambda n: N - 1 - n) if rev else (lambda n: n)
    HB, G = GLA_HB, H // GLA_HB
    q = pl.BlockSpec((C, HB * DK), lambda h, n: (cn(n), h))
    k = pl.BlockSpec((C, HB * DK), lambda h, n: (cn(n), G + h))
    v = pl.BlockSpec((C, HB * DV), lambda h, n: (cn(n), G + h))
    r = pl.BlockSpec((C, HB * DV), lambda h, n: (cn(n), 2 * G + h))
    bias = pl.BlockSpec((1, HB * DK), lambda h, n: (0, h))
    gain = pl.BlockSpec((HB, 1, DV), lambda h, n: (h, 0, 0))
    st = pl.BlockSpec((HB, 1, DV, DK), lambda h, n: (h, cn(n), 0, 0))
    ov = pl.BlockSpec((C, HB * DV), lambda h, n: (cn(n), h))
    return q, k, v, r, bias, gain, st, ov


def _gla_chunk(q, k, v, gl_raw, bias):
    C, DK = q.shape
    gl = gl_raw + bias
    la = -_softplus(-gl) * (1.0 / GLA_TAU)
    lower = _iota2((C, C), 0) >= _iota2((C, C), 1)
    cum = _fdot(jnp.where(lower, 1.0, 0.0), la)
    yield
    ref = _pick_row(cum, C // 2 - 1)
    clast = _pick_row(cum, C - 1)
    fw, bw = jnp.exp(cum - ref), jnp.exp(ref - cum)
    qs = q * (DK ** -0.5)
    s_lo = _bdot(qs * fw, k * bw, NT)
    s_up = _bdot(qs * bw, k * fw, NT)
    yield
    scores = jnp.where(lower, s_lo, s_up)
    return gl, cum, clast, fw, bw, qs, k, v, scores, lower


def _gla_fwd(proj, glogit, bias, gain, *, name):
    T = proj.shape[0]
    H, C, DK, DV = GLA_HEADS, CHUNK, GLA_DK, GLA_DV
    N = T // C

    def body(q_ref, k_ref, v_ref, r_ref, gl_ref, bias_ref, gain_ref, og_ref, st_ref, s_acc):
        n = pl.program_id(1)

        @pl.when(n == 0)
        def _():
            s_acc[...] = jnp.zeros_like(s_acc)

        def head(hb):
            kc, vc = pl.ds(hb * DK, DK), pl.ds(hb * DV, DV)
            gl, cum, clast, fw, bw, qs, k, v, scores, lower = yield from _gla_chunk(
                q_ref[:, kc], k_ref[:, kc], v_ref[:, vc], gl_ref[:, kc], bias_ref[:, kc])
            sp = s_acc[hb]
            st_ref[hb, 0] = sp
            o = _bdot(scores, v) + _bdot(qs * jnp.exp(cum), sp, NT)
            s_acc[hb] = sp * jnp.exp(clast) + _bdot(v, k * jnp.exp(clast - cum), TN)
            yield
            rstd = lax.rsqrt(jnp.mean(o * o, axis=-1, keepdims=True) + NORM_EPS)
            og_ref[:, vc] = (o * rstd * gain_ref[hb] * _silu(r_ref[:, vc])).astype(og_ref.dtype)

        _interleave([head(hb) for hb in range(GLA_HB)])

    q, k, v, r, bias_s, gn, st, ov = _gla_specs(N, False)
    return pl.pallas_call(
        body, name=name, grid=(H // GLA_HB, N), in_specs=[q, k, v, r, q, bias_s, gn], out_specs=(ov, st),
        out_shape=(jax.ShapeDtypeStruct((T, H * DV), BF16), jax.ShapeDtypeStruct((H, N, DV, DK), F32)),
        scratch_shapes=[pltpu.VMEM((GLA_HB, DV, DK), F32)],
        compiler_params=_params(("arbitrary", "arbitrary")),
    )(proj, proj, proj, proj, glogit, bias, gain)


def _gla_bwd(proj, glogit, bias, gain, states, dog, *, name):
    T = proj.shape[0]
    H, C, DK, DV = GLA_HEADS, CHUNK, GLA_DK, GLA_DV
    N = T // C

    def body(q_ref, k_ref, v_ref, r_ref, gl_ref, bias_ref, gain_ref, st_ref, dog_ref,
             dq_ref, dk_ref, dv_ref, dr_ref, dgl_ref, dgain_ref, ds_acc):
        n = pl.program_id(1)

        @pl.when(n == 0)
        def _():
            ds_acc[...] = jnp.zeros_like(ds_acc)
            dgain_ref[...] = jnp.zeros_like(dgain_ref)

        def head(hb):
            kc, vc = pl.ds(hb * DK, DK), pl.ds(hb * DV, DV)
            gl, cum, clast, fw, bw, qs, k, v, scores, lower = yield from _gla_chunk(
                q_ref[:, kc], k_ref[:, kc], v_ref[:, vc], gl_ref[:, kc], bias_ref[:, kc])
            sp = st_ref[hb, 0]
            ecum, e2, cdec = jnp.exp(cum), jnp.exp(clast - cum), jnp.exp(clast)
            q_in, k_end = qs * ecum, k * e2
            o = _bdot(scores, v) + _bdot(q_in, sp, NT)
            yield
            rstd = lax.rsqrt(jnp.mean(o * o, axis=-1, keepdims=True) + NORM_EPS)
            oh = o * rstd
            r = r_ref[:, vc]
            gain_v = gain_ref[hb]
            dy = dog_ref[:, vc].astype(F32)
            dr_ref[:, vc] = (dy * oh * gain_v * _dsilu(r)).astype(dr_ref.dtype)
            dnorm = dy * _silu(r)
            dgain_ref[hb] += jnp.sum(dnorm * oh, axis=0, keepdims=True)
            doh = dnorm * gain_v
            do = rstd * (doh - oh * jnp.mean(doh * oh, axis=-1, keepdims=True))
            dsn = ds_acc[hb]
            dq_in = _bdot(do, sp)
            dk_end = _bdot(v, dsn)
            dv_ref[:, vc] = (_bdot(k_end, dsn, NT) + _bdot(scores, do, TN)).astype(dv_ref.dtype)
            dcdec = jnp.sum(dsn * sp, axis=0, keepdims=True)
            ds_acc[hb] = dsn * cdec + _bdot(do, q_in, TN)
            dsc = _bdot(do, v, NT)
            yield
            ds_lo = jnp.where(lower, dsc, 0.0)
            ds_up = jnp.where(lower, 0.0, dsc)
            qf, kb, qb, kf = qs * fw, k * bw, qs * bw, k * fw
            dqf, dkb = _bdot(ds_lo, kb), _bdot(ds_lo, qf, TN)
            dqb, dkf = _bdot(ds_up, kf), _bdot(ds_up, qb, TN)
            yield
            dq_ref[:, kc] = ((dqf * fw + dqb * bw + dq_in * ecum) * (DK ** -0.5)).astype(dq_ref.dtype)
            dk_ref[:, kc] = (dkb * bw + dkf * fw + dk_end * e2).astype(dk_ref.dtype)
            dz = (dqf * qs + dkf * k) * fw - (dqb * qs + dkb * k) * bw
            kk = dk_end * k_end
            dcum = dz + dq_in * q_in - kk
            rows = _iota2((C, DK), 0)
            dcum = dcum + jnp.where(rows == C // 2 - 1, -jnp.sum(dz, axis=0, keepdims=True), 0.0)
            dcum = dcum + jnp.where(rows == C - 1, jnp.sum(kk, axis=0, keepdims=True) + dcdec * cdec, 0.0)
            upper = _iota2((C, C), 0) <= _iota2((C, C), 1)
            dla = _fdot(jnp.where(upper, 1.0, 0.0), dcum)
            yield
            dgl_ref[:, kc] = dla * (1.0 / GLA_TAU) * _sigmoid(-gl)

        _interleave([head(hb) for hb in range(GLA_HB)])

    q, k, v, r, bias_s, gn, st, ov = _gla_specs(N, True)
    return pl.pallas_call(
        body, name=name, grid=(H // GLA_HB, N), in_specs=[q, k, v, r, q, bias_s, gn, st, ov],
        out_specs=(q, q, ov, ov, q, gn),
        out_shape=(jax.ShapeDtypeStruct((T, H * DK), BF16), jax.ShapeDtypeStruct((T, H * DK), BF16),
                   jax.ShapeDtypeStruct((T, H * DV), BF16), jax.ShapeDtypeStruct((T, H * DV), BF16),
                   jax.ShapeDtypeStruct((T, H * DK), F32), jax.ShapeDtypeStruct((H, 1, DV), F32)),
        scratch_shapes=[pltpu.VMEM((GLA_HB, DV, DK), F32)],
        compiler_params=_params(("arbitrary", "arbitrary")),
    )(proj, proj, proj, proj, glogit, bias, gain, states, dog)


def _conv(xv, w_ref):
    out = _shift_down(xv, CONV_WIDTH - 1) * w_ref[0:1, :]
    for tap in range(1, CONV_WIDTH):
        out = out + _shift_down(xv, CONV_WIDTH - 1 - tap) * w_ref[tap:tap + 1, :]
    return out


def _conv_bwd(xv, w_ref, dpre, dw_ref):
    dx = None
    for tap in range(CONV_WIDTH):
        s = CONV_WIDTH - 1 - tap
        t = _shift_up(dpre, s) * w_ref[tap:tap + 1, :]
        dx = t if dx is None else dx + t
        dw_ref[tap:tap + 1, :] = jnp.sum(dpre * _shift_down(xv, s), axis=0, keepdims=True)
    return dx


CONV_COLS = 256


def _conv_silu_fwd(x, w, *, name):
    T = x.shape[0]
    n = w.shape[1]

    def body(x_ref, w_ref, o_ref):
        o_ref[...] = _silu(_conv(x_ref[...], w_ref))

    return pl.pallas_call(
        body, name=name, grid=(n // CONV_COLS,),
        in_specs=[pl.BlockSpec((T, CONV_COLS), lambda j: (0, j)), pl.BlockSpec((CONV_WIDTH, CONV_COLS), lambda j: (0, j))],
        out_specs=pl.BlockSpec((T, CONV_COLS), lambda j: (0, j)),
        out_shape=jax.ShapeDtypeStruct((T, n), F32), compiler_params=_params(("parallel",)),
    )(x, w)


def _conv_silu_bwd(x, w, dact, *, name):
    T = x.shape[0]
    n = w.shape[1]

    def body(x_ref, w_ref, da_ref, dx_ref, dw_ref):
        xv = x_ref[...]
        dpre = da_ref[...] * _dsilu(_conv(xv, w_ref))
        dx_ref[...] = _conv_bwd(xv, w_ref, dpre, dw_ref).astype(dx_ref.dtype)

    blk = pl.BlockSpec((T, CONV_COLS), lambda j: (0, j))
    wb = pl.BlockSpec((CONV_WIDTH, CONV_COLS), lambda j: (0, j))
    return pl.pallas_call(
        body, name=name, grid=(n // CONV_COLS,), in_specs=[blk, wb, blk], out_specs=(blk, wb),
        out_shape=(jax.ShapeDtypeStruct((T, n), BF16), jax.ShapeDtypeStruct((CONV_WIDTH, n), F32)),
        compiler_params=_params(("parallel",)),
    )(x, w, dact)


def _interleave(gens):
    results = [None] * len(gens)
    live = list(range(len(gens)))
    while live:
        for i in list(live):
            try:
                next(gens[i])
            except StopIteration as done:
                results[i] = done.value
                live.remove(i)
    return results


def _unit_lower_inverse(a):
    n = a.shape[0]
    eye = jnp.where(_iota2((n, n), 0) == _iota2((n, n), 1), 1.0, 0.0)
    p = -a
    t = eye + p
    for _ in range(5):
        p = _fdot(p, p)
        yield
        t = t + _fdot(t, p)
        yield
    return t


def _gdn_specs(N, rev):
    H, C, DK, DV = GDN_HEADS, CHUNK, GDN_DK, GDN_DV
    cn = (lambda n: N - 1 - n) if rev else (lambda n: n)
    HB, G = GDN_HB, H // GDN_HB
    q = pl.BlockSpec((C, HB * DK), lambda h, n: (cn(n), h))
    k = pl.BlockSpec((C, HB * DK), lambda h, n: (cn(n), G + h))
    v = pl.BlockSpec((C, HB * DV), lambda h, n: (cn(n), 2 * G + h))
    z = pl.BlockSpec((C, HB * DV), lambda h, n: (cn(n), 3 * G + h))
    gates = pl.BlockSpec((C, LANE), lambda h, n: (cn(n), 0))
    sc = pl.BlockSpec((HB, 1, 1), lambda h, n: (h, 0, 0))
    gain = pl.BlockSpec((1, DV), lambda h, n: (0, 0))
    st = pl.BlockSpec((HB, 1, DK, DV), lambda h, n: (h, cn(n), 0, 0))
    return q, k, v, z, gates, sc, gain, st


def _gdn_solved_specs(N, rev):
    C = CHUNK
    cn = (lambda n: N - 1 - n) if rev else (lambda n: n)
    return (pl.BlockSpec((GDN_HB, 1, C, C), lambda h, n: (h, cn(n), 0, 0)),
            pl.BlockSpec((GDN_HB, 1, C, GDN_DV + GDN_DK), lambda h, n: (h, cn(n), 0, 0)))


def _gdn_chunk(q_ref, k_ref, v_ref, gates_ref, alog_ref, dtb_ref, h, solved=None):
    H, C, DK, DV = GDN_HEADS, CHUNK, GDN_DK, GDN_DV
    gates = gates_ref[...]
    lane = _iota2(gates.shape, 1)
    bl = jnp.sum(jnp.where(lane == h, gates, 0.0), axis=1, keepdims=True)
    al = jnp.sum(jnp.where(lane == H + h, gates, 0.0), axis=1, keepdims=True)
    beta = _sigmoid(bl)
    ea = jnp.exp(alog_ref[...])
    xs = al + dtb_ref[...]
    la = -ea * _softplus(xs)
    ii, jj = _iota2((C, C), 0), _iota2((C, C), 1)
    strict = ii > jj
    cum_col = jnp.sum(jnp.where(ii >= jj, _col_to_row(la), 0.0), axis=1, keepdims=True)
    cum_row = jnp.sum(jnp.where(ii <= jj, la, 0.0), axis=0, keepdims=True)
    q, k, v = q_ref[...], k_ref[...], v_ref[...]
    rq = lax.rsqrt(jnp.sum(q * q, axis=-1, keepdims=True) + NORM_EPS)
    rk = lax.rsqrt(jnp.sum(k * k, axis=-1, keepdims=True) + NORM_EPS)
    qn = q * rq * (DK ** -0.5)
    kn = k * rk
    rel = jnp.where(strict, jnp.exp(jnp.where(strict, cum_col - cum_row, 0.0)), 0.0)
    rg = rel * _bdot(kn, kn, NT)
    yield
    a = beta * rg
    e_col = jnp.exp(cum_col)
    clast = _pick_row(cum_col, C - 1)
    if solved is None:
        tm = yield from _unit_lower_inverse(a)
        rhs = jnp.concatenate([beta * v, (beta * e_col) * kn], axis=1)
        sol = _fdot(tm, rhs)
        yield
    else:
        tm, sol = solved
    u, w = sol[:, :DV], sol[:, DV:]
    dd = jnp.exp(clast - cum_col)
    ke = kn * dd
    g = jnp.exp(clast)
    eye = jnp.where(_iota2((DK, DK), 0) == _iota2((DK, DK), 1), 1.0, 0.0)
    trans = g * eye - _bdot(ke, w, TN)
    inject = _bdot(ke, u, TN)
    yield
    return dict(beta=beta, ea=ea, xs=xs, la=la, strict=strict, ii=ii, jj=jj, q=q, k=k, v=v, rq=rq, rk=rk,
                qn=qn, kn=kn, rel=rel, rg=rg, a=a, tm=tm, e_col=e_col, sol=sol, u=u, w=w, dd=dd, ke=ke,
                g=g, eye=eye, trans=trans, inject=inject)


def _gdn_fwd(act, proj, gates, a_log, dt_bias, gain, *, name):
    T = act.shape[0]
    H, C, DK, DV = GDN_HEADS, CHUNK, GDN_DK, GDN_DV
    N = T // C

    def body(q_ref, k_ref, v_ref, z_ref, gates_ref, alog_ref, dtb_ref, gain_ref, og_ref, st_ref, tm_ref, sol_ref,
             s_acc):
        hg, n = pl.program_id(0), pl.program_id(1)

        @pl.when(n == 0)
        def _():
            s_acc[...] = jnp.zeros_like(s_acc)

        def head(hb):
            cols = pl.ds(hb * DK, DK)
            c = yield from _gdn_chunk(q_ref.at[:, cols], k_ref.at[:, cols], v_ref.at[:, cols], gates_ref,
                                      alog_ref.at[hb], dtb_ref.at[hb], hg * GDN_HB + hb)
            tm_ref[hb, 0] = c["tm"]
            sol_ref[hb, 0] = c["sol"]
            sp = s_acc[hb]
            st_ref[hb, 0] = sp
            snew = _bdot(c["trans"], sp) + c["inject"]
            yield
            s_acc[hb] = snew
            o = _bdot(c["qn"], snew)
            yield
            rstd = lax.rsqrt(jnp.mean(o * o, axis=-1, keepdims=True) + NORM_EPS)
            og_ref[:, cols] = (o * rstd * gain_ref[...] * _silu(z_ref[:, cols])).astype(og_ref.dtype)

        _interleave([head(hb) for hb in range(GDN_HB)])

    q, k, v, z, gt, sc, gn, st = _gdn_specs(N, False)
    tm_s, sol_s = _gdn_solved_specs(N, False)
    return pl.pallas_call(
        body, name=name, grid=(H // GDN_HB, N), in_specs=[q, k, v, z, gt, sc, sc, gn],
        out_specs=(q, st, tm_s, sol_s),
        out_shape=(jax.ShapeDtypeStruct((T, H * DV), BF16), jax.ShapeDtypeStruct((H, N, DK, DV), F32),
                   jax.ShapeDtypeStruct((H, N, C, C), F32), jax.ShapeDtypeStruct((H, N, C, DV + DK), F32)),
        scratch_shapes=[pltpu.VMEM((GDN_HB, DK, DV), F32)],
        compiler_params=_params(("arbitrary", "arbitrary")),
    )(act, act, act, proj, gates, a_log, dt_bias, gain)


def _gdn_bwd(act, proj, gates, a_log, dt_bias, gain, states, solved, dog, *, name):
    T = act.shape[0]
    H, C, DK, DV = GDN_HEADS, CHUNK, GDN_DK, GDN_DV
    N = T // C

    def rsum(x):
        return jnp.sum(x, axis=1, keepdims=True)

    def body(q_ref, k_ref, v_ref, z_ref, gates_ref, alog_ref, dtb_ref, gain_ref, st_ref, tm_ref, sol_ref, dog_ref,
             dq_ref, dk_ref, dv_ref, dz_ref, dgates_ref, dscal_ref, dgain_ref, ds_acc):
        hg, n = pl.program_id(0), pl.program_id(1)

        @pl.when(n == 0)
        def _():
            ds_acc[...] = jnp.zeros_like(ds_acc)
            dgain_ref[...] = jnp.zeros_like(dgain_ref)
            dscal_ref[...] = jnp.zeros_like(dscal_ref)

        _interleave([one_head(hb, hg * GDN_HB + hb, q_ref, k_ref, v_ref, z_ref, gates_ref, alog_ref, dtb_ref, gain_ref,
                              st_ref, tm_ref, sol_ref, dog_ref, dq_ref, dk_ref, dv_ref, dz_ref, dgates_ref, dscal_ref,
                              dgain_ref, ds_acc)
                     for hb in range(GDN_HB)])

    def one_head(hb, h, q_ref, k_ref, v_ref, z_ref, gates_ref, alog_ref, dtb_ref, gain_ref, st_ref, tm_ref, sol_ref,
                 dog_ref, dq_ref, dk_ref, dv_ref, dz_ref, dgates_ref, dscal_ref, dgain_ref, ds_acc):
        cols = pl.ds(hb * DK, DK)
        c = yield from _gdn_chunk(q_ref.at[:, cols], k_ref.at[:, cols], v_ref.at[:, cols], gates_ref,
                                  alog_ref.at[hb], dtb_ref.at[hb], h, solved=(tm_ref[hb, 0], sol_ref[hb, 0]))
        beta, kn, qn, v, ke, u, w, dd, e_col = c["beta"], c["kn"], c["qn"], c["v"], c["ke"], c["u"], c["w"], c["dd"], c["e_col"]
        sp = st_ref[hb, 0]
        snew = _bdot(c["trans"], sp) + c["inject"]
        yield
        o = _bdot(qn, snew)
        yield
        rstd = lax.rsqrt(jnp.mean(o * o, axis=-1, keepdims=True) + NORM_EPS)
        oh = o * rstd
        z = z_ref[:, cols]
        gain_v = gain_ref[...]
        dy = dog_ref[:, cols].astype(F32)
        dz_ref[:, cols] = (dy * oh * gain_v * _dsilu(z)).astype(dz_ref.dtype)
        dnorm = dy * _silu(z)
        dgain_ref[hb] += jnp.sum(dnorm * oh, axis=0, keepdims=True)
        doh = dnorm * gain_v
        do = rstd * (doh - oh * jnp.mean(doh * oh, axis=-1, keepdims=True))

        dstot = ds_acc[hb] + _bdot(qn, do, TN)
        dqn = _bdot(do, snew, NT)
        yield
        dtrans = _bdot(dstot, sp, NT)
        ds_acc[hb] = _bdot(c["trans"], dstot, TN)
        yield
        dg = jnp.sum(jnp.sum(dtrans * c["eye"], axis=1, keepdims=True), axis=0, keepdims=True)
        m = -dtrans
        dke = _bdot(w, m, NT) + _bdot(u, dstot, NT)
        dw = _bdot(ke, m)
        du = _bdot(ke, dstot)
        yield
        drhs = _fdot(c["tm"], jnp.concatenate([du, dw], axis=1), TN)
        yield
        da = jnp.where(c["strict"], -_fdot(drhs, c["sol"], NT), 0.0)
        yield
        drhs_u, drhs_w = drhs[:, :DV], drhs[:, DV:]
        rwk = rsum(drhs_w * kn)
        dbeta = rsum(da * c["rg"]) + rsum(drhs_u * v) + rwk * e_col
        dgm = da * beta * c["rel"]
        dkn = _bdot(dgm, kn) + _bdot(dgm, kn, TN) + (beta * e_col) * drhs_w + dd * dke
        yield
        dv_ref[:, cols] = beta * drhs_u
        r_ = da * c["a"]
        ddd = rsum(dke * kn)
        dc = rsum(r_) - _row_to_col(jnp.sum(r_, axis=0, keepdims=True)) + beta * rwk * e_col - ddd * dd
        dclast = jnp.sum(ddd * dd, axis=0, keepdims=True) + dg * c["g"]
        dc = dc + jnp.where(_iota2((C, 1), 0) == C - 1, dclast, 0.0)
        dla = jnp.sum(jnp.where(c["ii"] <= c["jj"], _col_to_row(dc), 0.0), axis=1, keepdims=True)
        dalog = jnp.sum(dla * c["la"], axis=0, keepdims=True)
        dxs = dla * (-c["ea"]) * _sigmoid(c["xs"])
        ddtb = jnp.sum(dxs, axis=0, keepdims=True)
        dbl = dbeta * beta * (1.0 - beta)
        lane = _iota2((C, LANE), 1)
        dgates_ref[hb] = jnp.where(lane == 0, dbl, jnp.where(lane == 1, dxs, 0.0))
        lane8 = _iota2((8, LANE), 1)
        dscal_ref[hb] += jnp.where(lane8 == 0, dalog, jnp.where(lane8 == 1, ddtb, 0.0))
        dk_ref[:, cols] = c["rk"] * (dkn - kn * rsum(dkn * kn))
        qh = c["q"] * c["rq"]
        dqs = dqn * (DK ** -0.5)
        dq_ref[:, cols] = c["rq"] * (dqs - qh * rsum(dqs * qh))

    q, k, v, z, gt, sc, gn, st = _gdn_specs(N, True)
    dgt = pl.BlockSpec((GDN_HB, C, LANE), lambda h, n: (h, N - 1 - n, 0))
    dsc = pl.BlockSpec((GDN_HB, 8, LANE), lambda h, n: (h, 0, 0))
    dgn = pl.BlockSpec((GDN_HB, 1, DV), lambda h, n: (h, 0, 0))
    sh = jax.ShapeDtypeStruct((T, H * DK), F32)
    tm_s, sol_s = _gdn_solved_specs(N, True)
    return pl.pallas_call(
        body, name=name, grid=(H // GDN_HB, N), in_specs=[q, k, v, z, gt, sc, sc, gn, st, tm_s, sol_s, q],
        out_specs=(q, q, q, q, dgt, dsc, dgn),
        out_shape=(sh, sh, sh, jax.ShapeDtypeStruct((T, H * DV), BF16),
                   jax.ShapeDtypeStruct((H, T, LANE), F32), jax.ShapeDtypeStruct((H, 8, LANE), F32),
                   jax.ShapeDtypeStruct((H, 1, DV), F32)),
        scratch_shapes=[pltpu.VMEM((GDN_HB, DK, DV), F32)],
        compiler_params=_params(("arbitrary", "arbitrary")),
    )(act, act, act, proj, gates, a_log, dt_bias, gain, states, *solved, dog)


SUBLANES = 8


def _linear_scan(a_ref, b_ref, h_ref, reverse):
    T, W = a_ref.shape
    nb = T // SUBLANES
    row = _iota2((SUBLANES, W), 0)

    def blk(bi, carry):
        bb = (nb - 1 - bi) if reverse else bi
        off = pl.multiple_of(bb * SUBLANES, SUBLANES)
        a = a_ref[pl.ds(off, SUBLANES), :]
        b = b_ref[pl.ds(off, SUBLANES), :]
        for d in (1, 2, 4):
            if reverse:
                edge = row >= SUBLANES - d
                a_sh = jnp.where(edge, 1.0, pltpu.roll(a, SUBLANES - d, 0))
                b_sh = jnp.where(edge, 0.0, pltpu.roll(b, SUBLANES - d, 0))
            else:
                edge = row < d
                a_sh = jnp.where(edge, 1.0, pltpu.roll(a, d, 0))
                b_sh = jnp.where(edge, 0.0, pltpu.roll(b, d, 0))
            b = a * b_sh + b
            a = a * a_sh
        h = a * carry + b
        h_ref[pl.ds(off, SUBLANES), :] = h
        return h[0:1, :] if reverse else h[SUBLANES - 1:SUBLANES, :]

    lax.fori_loop(0, nb, blk, jnp.zeros((1, W), F32))


def _lru_specs(T):
    B, W = LRU_BLOCKS, LRU_BLOCK
    xb = pl.BlockSpec((T, W), lambda j: (0, j))
    yb = pl.BlockSpec((T, W), lambda j: (0, B + j))
    cw = pl.BlockSpec((CONV_WIDTH, W), lambda j: (0, j))
    vec = pl.BlockSpec((1, W), lambda j: (0, j))
    wg = pl.BlockSpec((1, W, W), lambda j: (j, 0, 0))
    bg = pl.BlockSpec((1, 1, W), lambda j: (j, 0, 0))
    return xb, yb, cw, vec, wg, bg


def _lru_gates(xb_ref, cw_ref, cb_ref, wr_ref, br_ref, wi_ref, bi_ref, lam_ref):
    xv = xb_ref[...]
    xc = _conv(xv, cw_ref) + cb_ref[...]
    r = _sigmoid(_bdot(xc, wr_ref[0]) + br_ref[0])
    i = _sigmoid(_bdot(xc, wi_ref[0]) + bi_ref[0])
    sp = _softplus(-lam_ref[...])
    la = -LRU_C * sp * r
    a = jnp.exp(la)
    s = jnp.sqrt(-_expm1(2.0 * la))
    return xv, xc, r, i, sp, a, s


def _lru_fwd(proj, conv_w, conv_b, w_r, b_r, w_i, b_i, lam, *, name):
    T = proj.shape[0]
    B, W = LRU_BLOCKS, LRU_BLOCK

    def body(xb_ref, yb_ref, cw_ref, cb_ref, wr_ref, br_ref, wi_ref, bi_ref, lam_ref, og_ref, hs_ref, a_s, u_s):
        xv, xc, r, i, sp, a, s = _lru_gates(xb_ref, cw_ref, cb_ref, wr_ref, br_ref, wi_ref, bi_ref, lam_ref)
        a_s[...] = a
        u_s[...] = s * (i * xc)
        _linear_scan(a_s, u_s, hs_ref, False)
        og_ref[...] = (hs_ref[...] * _gelu(yb_ref[...])).astype(og_ref.dtype)

    xb, yb, cw, vec, wg, bg = _lru_specs(T)
    return pl.pallas_call(
        body, name=name, grid=(B,), in_specs=[xb, yb, cw, vec, wg, bg, wg, bg, vec], out_specs=(xb, xb),
        out_shape=(jax.ShapeDtypeStruct((T, B * W), BF16), jax.ShapeDtypeStruct((T, B * W), F32)),
        scratch_shapes=[pltpu.VMEM((T, W), F32), pltpu.VMEM((T, W), F32)],
        compiler_params=_params(("arbitrary",)),
    )(proj, proj, conv_w, conv_b, w_r, b_r, w_i, b_i, lam)


def _lru_bwd(proj, conv_w, conv_b, w_r, b_r, w_i, b_i, lam, hs, dout, *, name):
    T = proj.shape[0]
    B, W = LRU_BLOCKS, LRU_BLOCK

    def csum(x):
        return jnp.sum(x, axis=0, keepdims=True)

    def body(xb_ref, yb_ref, cw_ref, cb_ref, wr_ref, br_ref, wi_ref, bi_ref, lam_ref, hs_ref, do_ref,
             dxb_ref, dyb_ref, dcw_ref, dcb_ref, dwr_ref, dbr_ref, dwi_ref, dbi_ref, dlam_ref, a_s, b_s, g_s):
        xv, xc, r, i, sp, a, s = _lru_gates(xb_ref, cw_ref, cb_ref, wr_ref, br_ref, wi_ref, bi_ref, lam_ref)
        h = hs_ref[...]
        yb = yb_ref[...]
        dout = do_ref[...].astype(F32)
        dyb_ref[...] = (dout * h * _dgelu(yb)).astype(dyb_ref.dtype)
        a_s[...] = _shift_up(a, 1)
        b_s[...] = dout * _gelu(yb)
        _linear_scan(a_s, b_s, g_s, True)
        g = g_s[...]
        da = g * _shift_down(h, 1)
        ds = g * (i * xc)
        di = g * s * xc
        dxc = g * s * i
        dla = da * a - ds * (a * a) / s
        dr = dla * (-LRU_C * sp)
        dlam_ref[...] = csum(dla * r) * (LRU_C * _sigmoid(-lam_ref[...]))
        dpr = dr * r * (1.0 - r)
        dpi = di * i * (1.0 - i)
        dxc = dxc + _bdot(dpr, wr_ref[0], NT) + _bdot(dpi, wi_ref[0], NT)
        dwr_ref[0] = _bdot(xc, dpr, TN)
        dwi_ref[0] = _bdot(xc, dpi, TN)
        dbr_ref[0] = csum(dpr)
        dbi_ref[0] = csum(dpi)
        dcb_ref[...] = csum(dxc)
        dxb_ref[...] = _conv_bwd(xv, cw_ref, dxc, dcw_ref).astype(dxb_ref.dtype)

    xb, yb, cw, vec, wg, bg = _lru_specs(T)
    act = jax.ShapeDtypeStruct((T, B * W), BF16)
    return pl.pallas_call(
        body, name=name, grid=(B,), in_specs=[xb, yb, cw, vec, wg, bg, wg, bg, vec, xb, xb],
        out_specs=(xb, xb, cw, vec, wg, bg, wg, bg, vec),
        out_shape=(act, act, jax.ShapeDtypeStruct((CONV_WIDTH, B * W), F32), jax.ShapeDtypeStruct((1, B * W), F32),
                   jax.ShapeDtypeStruct((B, W, W), F32), jax.ShapeDtypeStruct((B, 1, W), F32),
                   jax.ShapeDtypeStruct((B, W, W), F32), jax.ShapeDtypeStruct((B, 1, W), F32),
                   jax.ShapeDtypeStruct((1, B * W), F32)),
        scratch_shapes=[pltpu.VMEM((T, W), F32)] * 3,
        compiler_params=_params(("arbitrary",)),
    )(proj, proj, conv_w, conv_b, w_r, b_r, w_i, b_i, lam, hs, dout)


MESH = pl.DeviceIdType.MESH
N_CHIPS = 4
AG_COPIES = 7


def _mesh_pos():
    return lax.axis_index("x"), lax.axis_index("y"), lax.axis_index("c")


def _hbm_specs(n):
    return [pl.BlockSpec(memory_space=pltpu.HBM)] * n


def _all_gather(shards, *, name):
    n = len(shards)

    def body(*refs):
        xs, outs = refs[:n], refs[n:2 * n]
        send_sems, recv_sems, local_sems = refs[2 * n:]
        x, y, c = _mesh_pos()
        me, sibling = (x, y, c), (x, y, 1 - c)
        chips = [(1 - x, y), (x, 1 - y), (1 - x, 1 - y)]

        def rows(t, px, py, pc):
            return outs[t].at[4 * px + 2 * py + pc]

        def copy(t, k, block, to, src=None):
            return pltpu.make_async_remote_copy(
                src_ref=rows(t, *block) if src is None else src, dst_ref=rows(t, *block),
                send_sem=send_sems.at[t * AG_COPIES + k], recv_sem=recv_sems.at[t * AG_COPIES + k],
                device_id=to, device_id_type=MESH)

        mine = [pltpu.make_async_copy(xs[t], rows(t, *me), local_sems.at[t]) for t in range(n)]
        for cp in mine:
            cp.start()
        first = []
        for t in range(n):
            first.append(copy(t, 0, me, sibling, src=xs[t]))
            first += [copy(t, 1 + j, me, (*chip, c), src=xs[t]) for j, chip in enumerate(chips)]
        for cp in first:
            cp.start()
        passed = []
        for j, chip in enumerate(chips):
            for t in range(n):
                copy(t, 1 + j, (*chip, c), me).wait_recv()
                cp = copy(t, 4 + j, (*chip, c), sibling)
                cp.start()
                passed.append(cp)
        for t in range(n):
            copy(t, 0, sibling, me).wait_recv()
        for j, chip in enumerate(chips):
            for t in range(n):
                copy(t, 4 + j, (*chip, 1 - c), me).wait_recv()
        for cp in first + passed:
            cp.wait_send()
        for cp in mine:
            cp.wait()

    return pl.pallas_call(
        body, name=name,
        out_shape=[jax.ShapeDtypeStruct((N_DEV,) + s.shape, s.dtype) for s in shards],
        in_specs=_hbm_specs(n), out_specs=_hbm_specs(n),
        scratch_shapes=[pltpu.SemaphoreType.DMA((n * AG_COPIES,)), pltpu.SemaphoreType.DMA((n * AG_COPIES,)),
                        pltpu.SemaphoreType.DMA((n,))],
    )(*shards)


SIDE_EFFECT = pltpu.SideEffectType.DATAFLOW_SIDE_EFFECTING


def _copies(plan, refs, send_sems, recv_sems):
    return [pltpu.make_async_remote_copy(src_ref=src, dst_ref=dst, send_sem=send_sems.at[k], recv_sem=recv_sems.at[k],
                                         device_id=to, device_id_type=MESH)
            for k, (src, dst, to) in enumerate(plan(refs))]


def _split_start(bufs, plan, n_copies, *, name, deps=()):
    n = len(bufs)

    def body(*refs):
        send_sems, recv_sems = refs[n + len(deps)], refs[n + len(deps) + 1]
        token = refs[-1]
        for cp in _copies(plan, refs[:n], send_sems, recv_sems):
            cp.start()
        token[...] = jnp.zeros_like(token)

    hbm, sem = pl.BlockSpec(memory_space=pltpu.HBM), pl.BlockSpec(memory_space=pltpu.SEMAPHORE)
    out = pl.pallas_call(
        body, name=name,
        out_shape=(pltpu.SemaphoreType.DMA((n_copies,)), pltpu.SemaphoreType.DMA((n_copies,)),
                   *[pltpu.HBM(b.shape, b.dtype) for b in bufs], jax.ShapeDtypeStruct(TOKEN_SHAPE, F32)),
        in_specs=[hbm] * n + [pl.BlockSpec(memory_space=pl.ANY)] * len(deps),
        out_specs=(sem, sem, *[hbm] * n, pl.BlockSpec(memory_space=pltpu.VMEM)),
        input_output_aliases={i: 2 + i for i in range(n)},
        compiler_params=pltpu.CompilerParams(has_side_effects=SIDE_EFFECT),
    )(*[pltpu.with_memory_space_constraint(b, pltpu.HBM) for b in bufs], *deps)
    return out[0], out[1], list(out[2:2 + n]), out[-1]


def _split_wait(send_sems, recv_sems, bufs, plan, after, *, name):
    n = len(bufs)

    def body(*refs):
        for cp in _copies(plan, refs[:n], refs[n], refs[n + 1]):
            cp.wait_send()
            cp.wait_recv()

    hbm, sem = pl.BlockSpec(memory_space=pltpu.HBM), pl.BlockSpec(memory_space=pltpu.SEMAPHORE)
    out = pl.pallas_call(
        body, name=name, out_shape=tuple(pltpu.HBM(b.shape, b.dtype) for b in bufs),
        in_specs=[hbm] * n + [sem, sem, pl.BlockSpec(memory_space=pl.ANY)], out_specs=tuple([hbm] * n),
        input_output_aliases={i: i for i in range(n)},
        compiler_params=pltpu.CompilerParams(has_side_effects=SIDE_EFFECT),
    )(*bufs, send_sems, recv_sems, after)
    return list(out)


def _block(ref, in_cols, d):
    if not in_cols:
        return ref.at[d]
    c = ref.shape[1] // N_DEV
    return ref.at[:, pl.ds(pl.multiple_of(d * c, LANE), c)]


def _plan_gather_a(in_cols):
    n = len(in_cols)

    def plan(refs):
        x, y, c = _mesh_pos()
        me = 4 * x + 2 * y + c
        peers = [(x, y, 1 - c), (1 - x, y, c), (x, 1 - y, c), (1 - x, 1 - y, c)]
        return [(refs[t], _block(refs[n + t], in_cols[t], me), to) for t in range(n) for to in peers]
    return plan


def _plan_gather_b(in_cols):
    n = len(in_cols)

    def plan(refs):
        x, y, c = _mesh_pos()
        ds = [4 * px + 2 * py + c for px, py in [(1 - x, y), (x, 1 - y), (1 - x, 1 - y)]]
        return [(_block(refs[t], in_cols[t], d), _block(refs[t], in_cols[t], d), (x, y, 1 - c))
                for t in range(n) for d in ds]
    return plan


def _plan_scatter_pair(in_cols):
    n = len(in_cols)

    def plan(refs):
        x, y, c = _mesh_pos()
        return [(_block(refs[t], in_cols[t], 2 * q + (1 - c)), refs[n + t].at[q], (x, y, 1 - c))
                for t in range(n) for q in range(N_CHIPS)]
    return plan


def _plan_scatter_chips(n):
    def plan(refs):
        x, y, c = _mesh_pos()
        chips = [(1 - x, y), (x, 1 - y), (1 - x, 1 - y)]
        return [(refs[t].at[2 * px + py], refs[n + t].at[j], (px, py, c))
                for t in range(n) for j, (px, py) in enumerate(chips)]
    return plan


def _insert_block(land, shard, device, in_cols, *, name):
    r, c = shard.shape
    tr = _tile(r, (512, 256))

    def body(d_ref, s_ref, land_ref, o_ref):
        if in_cols:
            o_ref[...] = s_ref[...]
        else:
            o_ref[0] = s_ref[...]

    if in_cols:
        o_spec = pl.BlockSpec((tr, c), lambda i, d: (i, d[0]))
    else:
        o_spec = pl.BlockSpec((1, tr, c), lambda i, d: (d[0], i, 0))
    return pl.pallas_call(
        body, name=name,
        grid_spec=pltpu.PrefetchScalarGridSpec(
            num_scalar_prefetch=1, grid=(r // tr,),
            in_specs=[pl.BlockSpec((tr, c), lambda i, d: (i, 0)), pl.BlockSpec(memory_space=pl.ANY)],
            out_specs=o_spec),
        out_shape=jax.ShapeDtypeStruct(land.shape, land.dtype), input_output_aliases={2: 0},
        compiler_params=_params(("parallel",)),
    )(device, shard, land)


PAIR_ROWS = (512, 256)


def _pair_add(g, a, core, in_cols, *, name):
    _, R, C = a.shape
    tr = _tile(R, PAIR_ROWS)

    def body(c_ref, g_ref, a_ref, o_ref):
        gv = g_ref[...] if in_cols else g_ref[0]
        o_ref[0] = (gv.astype(F32) + a_ref[0].astype(F32)).astype(o_ref.dtype)

    blk = pl.BlockSpec((1, tr, C), lambda q, i, c: (q, i, 0))
    if in_cols:
        g_spec = pl.BlockSpec((tr, C), lambda q, i, c: (i, 2 * q + c[0]))
    else:
        g_spec = pl.BlockSpec((1, tr, C), lambda q, i, c: (2 * q + c[0], i, 0))
    return pl.pallas_call(
        body, name=name,
        grid_spec=pltpu.PrefetchScalarGridSpec(
            num_scalar_prefetch=1, grid=(N_CHIPS, R // tr), in_specs=[g_spec, blk], out_specs=blk),
        out_shape=jax.ShapeDtypeStruct((N_CHIPS, R, C), BF16),
        compiler_params=_params(("parallel", "parallel")),
    )(core, g, a)


ADAM_ROWS = 256


def _adamw_sharded(w, m, v, s4, b3, chip, layer, prev, *, name):
    L, R, C = w.shape
    tr = _tile(R, (ADAM_ROWS,))
    if prev is None and L > 1:
        prev = tuple(lax.empty(w.shape, F32) for _ in range(4))
    n_prev = 0 if prev is None else 4

    def body(q_ref, w_ref, m_ref, v_ref, s_ref, b_ref, *rest):
        g_out, d_out, m_out, v_out = rest[n_prev:]
        g = s_ref[0].astype(F32)
        for j in range(N_CHIPS - 1):
            g = g + b_ref[j].astype(F32)
        mn = ADAM_B1 * m_ref[0] + (1.0 - ADAM_B1) * g
        vn = ADAM_B2 * v_ref[0] + (1.0 - ADAM_B2) * (g * g)
        g_out[0] = g
        d_out[0] = -ADAM_LR * ((mn / ADAM_C1) / (jnp.sqrt(vn / ADAM_C2) + ADAM_EPS) + ADAM_WD * w_ref[0])
        m_out[0] = mn
        v_out[0] = vn

    blk = pl.BlockSpec((1, tr, C), lambda i, q: (layer, i, 0))
    sh = jax.ShapeDtypeStruct((L, R, C), F32)
    return pl.pallas_call(
        body, name=name,
        grid_spec=pltpu.PrefetchScalarGridSpec(
            num_scalar_prefetch=1, grid=(R // tr,),
            in_specs=[blk, blk, blk, pl.BlockSpec((1, tr, C), lambda i, q: (q[0], i, 0)),
                      pl.BlockSpec((N_CHIPS - 1, tr, C), lambda i, q: (0, i, 0))]
            + [pl.BlockSpec(memory_space=pl.ANY)] * n_prev,
            out_specs=(blk,) * 4),
        out_shape=(sh,) * 4, input_output_aliases={6 + k: k for k in range(n_prev)},
        compiler_params=_params(("parallel",)),
    )(chip, w, m, v, s4, b3, *(prev or ()))


FWD_NAMES = ['x', 'norm1', 'norm2', 'final_norm', 'ret_w_in', 'ret_gn_gain', 'ret_w_out', 'gdn_w_in', 'gdn_conv_w',
             'gdn_a_log', 'gdn_dt_bias', 'gdn_norm_gain', 'gdn_w_out', 'gla_w_in', 'gla_w_gate_up', 'gla_gate_bias',
             'gla_norm_gain', 'gla_w_out', 'lru_w_in', 'lru_conv_w', 'lru_conv_b', 'lru_w_rgate', 'lru_b_rgate',
             'lru_w_igate', 'lru_b_igate', 'lru_lambda', 'lru_w_out', 'mlp_w_up', 'mlp_w_down']
WEIGHT_NAMES = FWD_NAMES[1:]
ARG_NAMES = FWD_NAMES + ['loss_target'] + ['m_' + n for n in WEIGHT_NAMES] + ['v_' + n for n in WEIGHT_NAMES]

MIXER_IN = ('ret_w_in', 'gdn_w_in', 'gla_w_in', 'lru_w_in')
MIXER_OUT = ('ret_w_out', 'gdn_w_out', 'gla_w_out', 'lru_w_out')
BIG_NAMES = MIXER_IN + MIXER_OUT + ('mlp_w_up', 'mlp_w_down')
SMALL = {'norm1': False, 'norm2': False, 'final_norm': False, 'ret_gn_gain': True, 'gdn_conv_w': True,
         'gdn_a_log': False, 'gdn_dt_bias': False, 'gdn_norm_gain': False, 'gla_w_gate_up': True,
         'gla_gate_bias': True, 'gla_norm_gain': True, 'lru_conv_w': True, 'lru_conv_b': True,
         'lru_w_rgate': False, 'lru_b_rgate': False, 'lru_w_igate': False, 'lru_b_igate': False, 'lru_lambda': True}
SMALL_NAMES = tuple(n for n in WEIGHT_NAMES if n in SMALL)
MEDIUM = ('lru_w_rgate', 'lru_w_igate')
EARLY_SMALL = tuple(n for n in SMALL_NAMES if n.startswith(('gdn_', 'gla_', 'lru_')) and n not in MEDIUM)
LATE_SMALL = tuple(n for n in SMALL_NAMES if n not in EARLY_SMALL + MEDIUM)
GDN_TAIL = 2 * GDN_HEADS
GDN_MAIN = 4 * D_MODEL
GLA_MAIN = 3 * D_MODEL


PACK_ROWS = 256


PACK_TILE = SUBLANES * LANE


def _pack(arrs):
    rows = []
    for a in arrs:
        f = a.reshape(-1).astype(F32)
        rows.append(jnp.pad(f, (0, (-f.shape[0]) % PACK_TILE)).reshape(-1, LANE))
    fill = (-sum(r.shape[0] for r in rows)) % PACK_ROWS
    if fill:
        rows.append(jnp.zeros((fill, LANE), F32))
    return jnp.concatenate(rows, axis=0)


def _unpack(buf, shapes, lead=()):
    out, r0 = [], 0
    for s in shapes:
        n = int(np.prod(s))
        nr = -(-n // PACK_TILE) * SUBLANES
        blk = buf[..., r0:r0 + nr, :].reshape(lead + (nr * LANE,))[..., :n]
        out.append(blk.reshape(lead + tuple(s)))
        r0 += nr
    return out


def _full_cols(g):
    return jnp.transpose(g, (1, 0, 2)).reshape(g.shape[1], N_DEV * g.shape[2])


def _full_rows(g):
    return g.reshape(N_DEV * g.shape[1], g.shape[2])


def _blocks_cols(dw):
    r, c = dw.shape[0], dw.shape[1] // N_DEV
    return jnp.transpose(dw.reshape(r, N_DEV, c), (1, 0, 2))


def _blocks_rows(dw):
    return dw.reshape(N_DEV, dw.shape[0] // N_DEV, dw.shape[1])


def _pad_cols(a, n=LANE):
    return jnp.pad(a, ((0, 0), (0, n - a.shape[1])))


def _mixer_fwd(layer, hn, w_in, sm, tables, deps=()):
    tag = f"l{layer}"
    if layer == 0:
        proj = _matmul(hn, w_in, name=tag + "_in", deps=deps)
        gain = sm['ret_gn_gain'][0][:, None, :]
        og, st = _ret_fwd(proj, gain, tables, name=tag + "_ret_fwd")
        return og, dict(proj=proj, st=st, gain=gain)
    if layer == 1:
        w_main, w_tail = w_in, _pad_cols(w_in[:, GDN_MAIN:])
        proj = _matmul(hn, w_main, b_cols=GDN_MAIN, name=tag + "_in", deps=deps)
        gates = _matmul(hn, w_tail, name=tag + "_in_tail")
        conv_w = sm['gdn_conv_w'][0]
        act = _conv_silu_fwd(proj, conv_w, name=tag + "_conv")
        a_log = sm['gdn_a_log'].reshape(GDN_HEADS, 1, 1)
        dt_bias = sm['gdn_dt_bias'].reshape(GDN_HEADS, 1, 1)
        gain = sm['gdn_norm_gain']
        og, st, tm, sol = _gdn_fwd(act, proj, gates, a_log, dt_bias, gain, name=tag + "_gdn_fwd")
        return og, dict(proj=proj, gates=gates, act=act, st=st, solved=(tm, sol), conv_w=conv_w, a_log=a_log, dt_bias=dt_bias,
                        gain=gain, w_main=w_main, w_tail=w_tail)
    if layer == 2:
        w_main, w_tail = w_in, _pad_cols(w_in[:, GLA_MAIN:])
        proj = _matmul(hn, w_main, b_cols=GLA_MAIN, name=tag + "_in", deps=deps)
        glow = _matmul(hn, w_tail, name=tag + "_in_tail")
        wgu = jnp.pad(sm['gla_w_gate_up'][0], ((0, LANE - GLA_GATE_RANK), (0, 0)))
        glogit = _matmul(glow, wgu, name=tag + "_gate_up")
        bias = sm['gla_gate_bias']
        gain = sm['gla_norm_gain'][0][:, None, :]
        og, st = _gla_fwd(proj, glogit, bias, gain, name=tag + "_gla_fwd")
        return og, dict(proj=proj, glow=glow, glogit=glogit, wgu=wgu, bias=bias, gain=gain, st=st,
                        w_main=w_main, w_tail=w_tail)
    proj = _matmul(hn, w_in, name=tag + "_in", deps=deps)
    args = (proj, sm['lru_conv_w'][0], sm['lru_conv_b'], sm['lru_w_rgate'][0], sm['lru_b_rgate'][0][:, None, :],
            sm['lru_w_igate'][0], sm['lru_b_igate'][0][:, None, :], sm['lru_lambda'])
    og, hs = _lru_fwd(*args, name=tag + "_lru_fwd")
    return og, dict(args=args, hs=hs)


def _mixer_bwd(layer, hn, w_in, dog, sv, tables, on_dw):
    tag = f"l{layer}"
    if layer == 0:
        dq, dk, dv, dg, dgain = _ret_bwd(sv['proj'], sv['gain'], tables, sv['st'], dog, name=tag + "_ret_bwd")
        dproj = jnp.concatenate([dq, dk, dv, dg], axis=1)
        deps = on_dw(_matmul(hn, dproj, ta=True, out_dtype=BF16, name=tag + "_in_dw"))
        dhn = _matmul(dproj, w_in, tb=True, name=tag + "_in_dx", deps=deps)
        return dhn, {'ret_gn_gain': dgain[:, 0][None]}
    if layer == 1:
        dq, dk, dv, dz, dgates, dscal, dgain = _gdn_bwd(
            sv['act'], sv['proj'], sv['gates'], sv['a_log'], sv['dt_bias'], sv['gain'], sv['st'], sv['solved'], dog,
            name=tag + "_gdn_bwd")
        dact = jnp.concatenate([dq, dk, dv], axis=1)
        dqkv, dconv = _conv_silu_bwd(sv['proj'], sv['conv_w'], dact, name=tag + "_conv_bwd")
        dmain = jnp.concatenate([dqkv, dz], axis=1)
        T = dmain.shape[0]
        dtail = _pad_cols(jnp.transpose(dgates[:, :, :2], (1, 2, 0)).reshape(T, GDN_TAIL))
        dw_main = _matmul(hn, dmain, ta=True, out_dtype=BF16, name=tag + "_in_dw")
        dw_tail = _matmul(hn, dtail, ta=True, out_dtype=BF16, name=tag + "_in_tail_dw")
        deps = on_dw(jnp.concatenate([dw_main, dw_tail[:, :GDN_TAIL]], axis=1))
        dhn = _matmul(dmain, sv['w_main'], tb=True, b_cols=GDN_MAIN, name=tag + "_in_dx", deps=deps)
        dhn = _matmul(dtail, sv['w_tail'], tb=True, epi="add", extra=dhn, name=tag + "_in_tail_dx")
        small = {'gdn_conv_w': dconv[None], 'gdn_a_log': dscal[:, 0, 0][None], 'gdn_dt_bias': dscal[:, 0, 1][None],
                 'gdn_norm_gain': jnp.sum(dgain[:, 0], axis=0)[None]}
        return dhn, small
    if layer == 2:
        dq, dk, dv, dr, dgl, dgain = _gla_bwd(sv['proj'], sv['glogit'], sv['bias'], sv['gain'], sv['st'], dog,
                                              name=tag + "_gla_bwd")
        dmain = jnp.concatenate([dq, dk, dv, dr], axis=1)
        dglow = _matmul(dgl, sv['wgu'], tb=True, name=tag + "_gate_up_dx")
        dwgu = _matmul(sv['glow'], dgl, ta=True, name=tag + "_gate_up_dw")
        dbias = _colsum(dgl, name=tag + "_gate_bias")
        dw_main = _matmul(hn, dmain, ta=True, out_dtype=BF16, name=tag + "_in_dw")
        dw_tail = _matmul(hn, dglow, ta=True, out_dtype=BF16, name=tag + "_in_tail_dw")
        deps = on_dw(jnp.concatenate([dw_main, dw_tail[:, :GLA_GATE_RANK]], axis=1))
        dhn = _matmul(dmain, sv['w_main'], tb=True, b_cols=GLA_MAIN, name=tag + "_in_dx", deps=deps)
        dhn = _matmul(dglow, sv['w_tail'], tb=True, epi="add", extra=dhn, name=tag + "_in_tail_dx")
        small = {'gla_w_gate_up': dwgu[:GLA_GATE_RANK][None], 'gla_gate_bias': dbias,
                 'gla_norm_gain': dgain[:, 0][None]}
        return dhn, small
    dxb, dyb, dcw, dcb, dwr, dbr, dwi, dbi, dlam = _lru_bwd(*sv['args'], sv['hs'], dog, name=tag + "_lru_bwd")
    dproj = jnp.concatenate([dxb, dyb], axis=1)
    deps = on_dw(_matmul(hn, dproj, ta=True, out_dtype=BF16, name=tag + "_in_dw"))
    dhn = _matmul(dproj, w_in, tb=True, name=tag + "_in_dx", deps=deps)
    small = {'lru_conv_w': dcw[None], 'lru_conv_b': dcb, 'lru_w_rgate': dwr[None], 'lru_b_rgate': dbr[:, 0][None],
             'lru_w_igate': dwi[None], 'lru_b_igate': dbi[:, 0][None], 'lru_lambda': dlam}
    return dhn, small


def _step(*args):
    assert len(args) == len(ARG_NAMES)
    p = dict(zip(ARG_NAMES, args))
    xi, yi, ci = _mesh_pos()
    dev = 4 * xi + 2 * yi + ci
    device = dev.astype(jnp.int32).reshape(1)
    core = ci.astype(jnp.int32).reshape(1)
    chip = (2 * xi + yi).astype(jnp.int32).reshape(1)
    x = p['x'][0]
    target = p['loss_target'][0]
    T = x.shape[0]
    tables = _ret_tables(T)

    sharded_small = [n for n in SMALL_NAMES if SMALL[n]]
    gathered, = _all_gather([_pack([p[n] for n in sharded_small])], name="gather_small")
    def in_cols_of(w):
        return w.shape[1] % LANE == 0

    gathers = {}
    token = gathered
    for layer in range(DEPTH):
        w_in_shard = p[MIXER_IN[layer]][0]
        groups = {'a': [(w_in_shard, in_cols_of(w_in_shard), _full_cols)],
                  'b': [(p[MIXER_OUT[layer]][0], False, _full_rows)],
                  'c': [(p['mlp_w_up'][layer], True, None), (p['mlp_w_down'][layer], False, _full_rows)]}
        for key, members in groups.items():
            shards = [w.astype(BF16) for w, _, _ in members]
            in_cols = [ic for _, ic, _ in members]
            n = len(shards)
            lands = [_insert_block(lax.empty((s.shape[0], N_DEV * s.shape[1]) if ic else (N_DEV,) + s.shape, BF16),
                                   s, device, ic, name=f"own_l{layer}{key}{i}")
                     for i, (s, ic) in enumerate(zip(shards, in_cols))]
            send, recv, bufs, token = _split_start(shards + lands, _plan_gather_a(in_cols), 4 * n,
                                                   name=f"gather_a_start_l{layer}{key}", deps=(token,))
            gathers[layer, key] = dict(n=n, h=(send, recv, bufs), in_cols=in_cols, full_of=[f for _, _, f in members])

    def gather_forward(layer, key, after):
        g = gathers[layer, key]
        n = g['n']
        send, recv, bufs = g['h']
        bufs = _split_wait(send, recv, bufs, _plan_gather_a(g['in_cols']), after, name=f"gather_a_wait_l{layer}{key}")
        send, recv, lands, tok = _split_start(bufs[n:], _plan_gather_b(g['in_cols']), 3 * n,
                                              name=f"gather_b_start_l{layer}{key}")
        g['h'] = (send, recv, lands)
        return tok

    def gather_finish(layer, key, after):
        g = gathers[layer, key]
        send, recv, lands = g['h']
        lands = _split_wait(send, recv, lands, _plan_gather_b(g['in_cols']), after, name=f"gather_b_wait_l{layer}{key}")
        return [l if ic else full_of(l) for l, ic, full_of in zip(lands, g['in_cols'], g['full_of'])]

    w_in_next, = gather_finish(0, 'a', gather_forward(0, 'a', token))
    parts = _unpack(gathered, [p[n].shape for n in sharded_small], lead=(N_DEV,))
    sm = {n: p[n] for n in SMALL_NAMES if not SMALL[n]}
    for n, blk in zip(sharded_small, parts):
        full = jnp.moveaxis(blk, 0, -2)
        sm[n] = full.reshape(full.shape[:-2] + (N_DEV * full.shape[-1],))

    saved = []
    big = {}
    for layer in range(DEPTH):
        w_in = w_in_next
        tag = f"l{layer}"
        hn = _rmsnorm_fwd(x, sm['norm1'][layer][None], name=tag + "_norm1")
        og, sv = _mixer_fwd(layer, hn, w_in, sm, tables)
        tok_b = gather_forward(layer, 'b', og)
        tok_c = gather_forward(layer, 'c', tok_b)
        w_out, = gather_finish(layer, 'b', tok_c)
        if og.shape[1] <= D_MODEL:
            x_mid, hn2 = _matmul(og, w_out, epi="add_norm", extra=x, gain=sm['norm2'][layer][None], name=tag + "_out")
        else:
            x_mid = _matmul(og, w_out, epi="add", extra=x, name=tag + "_out")
            hn2 = _rmsnorm_fwd(x_mid, sm['norm2'][layer][None], name=tag + "_norm2")
        w_up, w_down = gather_finish(layer, 'c', hn2)
        big[layer] = (w_in, w_out, w_up, w_down)
        u, a = _matmul(hn2, w_up, epi="relu2", name=tag + "_up")
        deps = (gather_forward(layer + 1, 'a', u),) if layer + 1 < DEPTH else ()
        x_new = _matmul(a, w_down, epi="add", extra=x_mid, name=tag + "_down", deps=deps)
        if layer + 1 < DEPTH:
            w_in_next, = gather_finish(layer + 1, 'a', x_new)
        saved.append(dict(x=x, hn=hn, og=og, sv=sv, x_mid=x_mid, hn2=hn2, u=u, a=a))
        x = x_new
    dx, dxb, dfinal, loss_part = _final_loss_bwd(x, sm['final_norm'][None], target, name="final_loss_bwd")
    loss = lax.psum(loss_part[0, 0], ("x", "y", "c"))

    outs = {}
    small_grads = {'final_norm': dfinal[0]}
    dnorm1, dnorm2 = [None] * DEPTH, [None] * DEPTH

    def scatter_start(items, deps, tag):
        grads, in_cols = [], []
        for _, _, dw, on in items:
            ic = on == 'cols' and (dw.shape[1] // N_DEV) % LANE == 0
            in_cols.append(ic)
            grads.append(dw if ic else (_blocks_cols(dw) if on == 'cols' else _blocks_rows(dw)))
        n = len(grads)
        lands = [lax.empty((N_CHIPS, dw.shape[0], dw.shape[1] // N_DEV) if ic else (N_CHIPS,) + dw.shape[1:], BF16)
                 for dw, ic in zip(grads, in_cols)]
        send, recv, bufs, tok = _split_start(grads + lands, _plan_scatter_pair(in_cols), N_CHIPS * n,
                                             name=f"scatter_pair_start_{tag}", deps=deps)
        return dict(items=items, n=n, tag=tag, in_cols=in_cols, h=(send, recv, bufs)), tok

    def scatter_forward(g, after):
        n, tag = g['n'], g['tag']
        send, recv, bufs = g['h']
        bufs = _split_wait(send, recv, bufs, _plan_scatter_pair(g['in_cols']), after, name=f"scatter_pair_wait_{tag}")
        sums = [_pair_add(b, a_, core, ic, name=f"pair_add_{tag}_{i}")
                for i, (b, a_, ic) in enumerate(zip(bufs[:n], bufs[n:], g['in_cols']))]
        lands = [lax.empty((N_CHIPS - 1,) + s_.shape[1:], BF16) for s_ in sums]
        send, recv, bufs, tok = _split_start(sums + lands, _plan_scatter_chips(n), (N_CHIPS - 1) * n,
                                             name=f"scatter_chips_start_{tag}")
        g['h'] = (send, recv, bufs)
        return tok

    def scatter_finish(g, after):
        n, tag = g['n'], g['tag']
        send, recv, bufs = g['h']
        bufs = _split_wait(send, recv, bufs, _plan_scatter_chips(n), after, name=f"scatter_chips_wait_{tag}")
        for i, (wname, idx, _, _) in enumerate(g['items']):
            outs[wname] = _adamw_sharded(p[wname], p['m_' + wname], p['v_' + wname], bufs[i], bufs[n + i], chip,
                                         idx, outs.get(wname), name=f"adamw_{tag}_{i}")

    older = []
    next_deps = ()
    for layer in reversed(range(DEPTH)):
        w_in, w_out, w_up, w_down = big[layer]
        s = saved[layer]
        tag = f"l{layer}"
        du = _matmul(dxb, w_down, tb=True, epi="drelu2", extra=s['u'], out_dtype=BF16, name=tag + "_down_dx",
                     deps=next_deps)
        next_deps = ()
        dw_down = _matmul(s['a'], dxb, ta=True, out_dtype=BF16, name=tag + "_down_dw")
        dhn2 = _matmul(du, w_up, tb=True, name=tag + "_up_dx")
        dw_up = _matmul(s['hn2'], du, ta=True, out_dtype=BF16, name=tag + "_up_dw")
        mlp_group, tok = scatter_start([('mlp_w_up', layer, dw_up, 'cols'),
                                        ('mlp_w_down', layer, dw_down, 'rows')], (), f"mlp_l{layer}")
        dx, dxb, dn2 = _rmsnorm_bwd(s['x_mid'], sm['norm2'][layer][None], dhn2, dx, name=tag + "_norm2_bwd",
                                    deps=(tok,))
        dw_out = _matmul(s['og'], dxb, ta=True, out_dtype=BF16, name=tag + "_out_dw")
        tok = scatter_forward(mlp_group, dw_out)
        dog = _matmul(dxb, w_out, tb=True, name=tag + "_out_dx", deps=(tok,))
        started = []

        def on_dw(dw_in):
            group, tok_ = scatter_start([(MIXER_IN[layer], 0, dw_in, 'cols'),
                                         (MIXER_OUT[layer], 0, dw_out, 'rows')], (), f"mix_l{layer}")
            started.append(group)
            return (tok_,)

        dhn, sg = _mixer_bwd(layer, s['hn'], w_in, dog, s['sv'], tables, on_dw)
        mixer_group, = started
        small_grads.update(sg)
        if layer > 0:
            tok = scatter_forward(mixer_group, dhn)
            dx, dxb, dn1 = _rmsnorm_bwd(s['x'], sm['norm1'][layer][None], dhn, dx, name=tag + "_norm1_bwd", deps=(tok,))
        else:
            tok = gather_forward('small_grads', 'early', dhn)
            dx, dxb, dn1 = _rmsnorm_bwd(s['x'], sm['norm1'][layer][None], dhn, dx, name=tag + "_norm1_bwd", deps=(tok,))
        dnorm1[layer], dnorm2[layer] = dn1[0], dn2[0]
        for g in older:
            scatter_finish(g, dx)
        older = [mlp_group, mixer_group]
        if layer == 1:
            parts = [_pack([small_grads[n] for n in EARLY_SMALL])] + [small_grads[n].reshape(-1, LANE) for n in MEDIUM]
            lands = [_insert_block(lax.empty((N_DEV,) + s_.shape, F32), s_, device, False, name=f"own_small_grads{i}")
                     for i, s_ in enumerate(parts)]
            n_parts = len(parts)
            send, recv, bufs, tok = _split_start(parts + lands, _plan_gather_a([False] * n_parts), 4 * n_parts,
                                                 name="gather_a_start_small_grads", deps=(dx,))
            gathers['small_grads', 'early'] = dict(n=n_parts, h=(send, recv, bufs), in_cols=[False] * n_parts,
                                                   full_of=[lambda l: l] * n_parts)
            next_deps = (tok,)
    small_grads['norm1'] = jnp.stack(dnorm1)
    small_grads['norm2'] = jnp.stack(dnorm2)

    late_parts, = _all_gather([_pack([small_grads[n] for n in LATE_SMALL])], name="gather_small_grads")
    last_token = scatter_forward(mixer_group, late_parts)
    early_parts, *medium_parts = gather_finish('small_grads', 'early', last_token)
    early_sum = _sum_parts(early_parts, name="sum_small_grads_early", deps=(last_token,))
    late_sum = _sum_parts(late_parts, name="sum_small_grads_late")
    for g in older:
        if g is not mixer_group:
            scatter_finish(g, late_sum)
    done_first = []
    for n, part in zip(MEDIUM, medium_parts):
        g = _sum_parts(part, name=f"sum_{n}")
        res = _adamw(p[n].reshape(-1, LANE), p['m_' + n].reshape(-1, LANE), p['v_' + n].reshape(-1, LANE), [g],
                     name=f"adamw_{n}")
        outs[n] = tuple(r.reshape(p[n].shape) for r in res)
        done_first.append(res[0][:SUBLANES])
    by_name = dict(zip(EARLY_SMALL, _unpack(early_sum, [small_grads[n].shape for n in EARLY_SMALL])))
    by_name.update(zip(LATE_SMALL, _unpack(late_sum, [small_grads[n].shape for n in LATE_SMALL])))
    packed_names = EARLY_SMALL + LATE_SMALL
    local_g = []
    for n in packed_names:
        g = by_name[n]
        if SMALL[n]:
            width = p[n].shape[-1]
            g = lax.dynamic_slice_in_dim(g, dev * width, width, axis=g.ndim - 1)
        local_g.append(g.reshape(p[n].shape))
    res = _adamw(_pack([p[n] for n in packed_names]), _pack([p['m_' + n] for n in packed_names]),
                 _pack([p['v_' + n] for n in packed_names]), [_pack(local_g)], name="adamw_small")
    local_shapes = [p[n].shape for n in packed_names]
    unpacked = [_unpack(r, local_shapes) for r in res]
    for i, n in enumerate(packed_names):
        outs[n] = tuple(unpacked[k][i] for k in range(4))
    done_first.append(res[0][:SUBLANES])
    done_first += [outs[n][0][0, :SUBLANES, :LANE] for n in BIG_NAMES if n not in (MIXER_IN[0], MIXER_OUT[0])]
    scatter_finish(mixer_group, functools.reduce(jnp.add, done_first))

    result = [loss, dx[None]]
    for k in range(4):
        result += [outs[n][k] for n in WEIGHT_NAMES]
    return tuple(result)


def kernel(x, norm1, norm2, final_norm, ret_w_in, ret_gn_gain, ret_w_out, gdn_w_in, gdn_conv_w, gdn_a_log, gdn_dt_bias, gdn_norm_gain, gdn_w_out, gla_w_in, gla_w_gate_up, gla_gate_bias, gla_norm_gain, gla_w_out, lru_w_in, lru_conv_w, lru_conv_b, lru_w_rgate, lru_b_rgate, lru_w_igate, lru_b_igate, lru_lambda, lru_w_out, mlp_w_up, mlp_w_down, loss_target, m_norm1, m_norm2, m_final_norm, m_ret_w_in, m_ret_gn_gain, m_ret_w_out, m_gdn_w_in, m_gdn_conv_w, m_gdn_a_log, m_gdn_dt_bias, m_gdn_norm_gain, m_gdn_w_out, m_gla_w_in, m_gla_w_gate_up, m_gla_gate_bias, m_gla_norm_gain, m_gla_w_out, m_lru_w_in, m_lru_conv_w, m_lru_conv_b, m_lru_w_rgate, m_lru_b_rgate, m_lru_w_igate, m_lru_b_igate, m_lru_lambda, m_lru_w_out, m_mlp_w_up, m_mlp_w_down, v_norm1, v_norm2, v_final_norm, v_ret_w_in, v_ret_gn_gain, v_ret_w_out, v_gdn_w_in, v_gdn_conv_w, v_gdn_a_log, v_gdn_dt_bias, v_gdn_norm_gain, v_gdn_w_out, v_gla_w_in, v_gla_w_gate_up, v_gla_gate_bias, v_gla_norm_gain, v_gla_w_out, v_lru_w_in, v_lru_conv_w, v_lru_conv_b, v_lru_w_rgate, v_lru_b_rgate, v_lru_w_igate, v_lru_b_igate, v_lru_lambda, v_lru_w_out, v_mlp_w_up, v_mlp_w_down):
    return _step(x, norm1, norm2, final_norm, ret_w_in, ret_gn_gain, ret_w_out, gdn_w_in, gdn_conv_w, gdn_a_log, gdn_dt_bias, gdn_norm_gain, gdn_w_out, gla_w_in, gla_w_gate_up, gla_gate_bias, gla_norm_gain, gla_w_out, lru_w_in, lru_conv_w, lru_conv_b, lru_w_rgate, lru_b_rgate, lru_w_igate, lru_b_igate, lru_lambda, lru_w_out, mlp_w_up, mlp_w_down, loss_target, m_norm1, m_norm2, m_final_norm, m_ret_w_in, m_ret_gn_gain, m_ret_w_out, m_gdn_w_in, m_gdn_conv_w, m_gdn_a_log, m_gdn_dt_bias, m_gdn_norm_gain, m_gdn_w_out, m_gla_w_in, m_gla_w_gate_up, m_gla_gate_bias, m_gla_norm_gain, m_gla_w_out, m_lru_w_in, m_lru_conv_w, m_lru_conv_b, m_lru_w_rgate, m_lru_b_rgate, m_lru_w_igate, m_lru_b_igate, m_lru_lambda, m_lru_w_out, m_mlp_w_up, m_mlp_w_down, v_norm1, v_norm2, v_final_norm, v_ret_w_in, v_ret_gn_gain, v_ret_w_out, v_gdn_w_in, v_gdn_conv_w, v_gdn_a_log, v_gdn_dt_bias, v_gdn_norm_gain, v_gdn_w_out, v_gla_w_in, v_gla_w_gate_up, v_gla_gate_bias, v_gla_norm_gain, v_gla_w_out, v_lru_w_in, v_lru_conv_w, v_lru_conv_b, v_lru_w_rgate, v_lru_b_rgate, v_lru_w_igate, v_lru_b_igate, v_lru_lambda, v_lru_w_out, v_mlp_w_up, v_mlp_w_down)
```
